```python
import jax, jax.numpy as jnp
from jax import lax
import numpy as np

D_MODEL = 1024
BATCH = 8
SEQ = 8192
DEPTH = 4

CHUNK = 64
N_A_LAYERS = DEPTH // 2
N_B_LAYERS = DEPTH - N_A_LAYERS
EXPAND = 2
D_INNER = EXPAND * D_MODEL
POOL_WINDOWS = (2, 4, 8, 16)
N_POOL_GROUPS = len(POOL_WINDOWS)
POOL_GROUP_W = D_INNER // N_POOL_GROUPS
N_HEADS = 16
HEAD_DIM = D_INNER // N_HEADS
LEFT_CHUNKS = 8
BAND = (LEFT_CHUNKS + 1) * CHUNK
REL_CLIP = 128
EPS = 1e-6

kernel_name = "yoco_pool_chunkattn_adaln_trunk"


def rms_norm(x, g):
    xf = x.astype(jnp.float32)
    y = xf * lax.rsqrt(jnp.mean(xf * xf, axis=-1, keepdims=True) + EPS)
    return (y * g.astype(jnp.float32)).astype(x.dtype)


def modulate(h, shift, scale):
    return h * (1 + scale[:, None, :]) + shift[:, None, :]


def multiscale_pool(v):
    B, S, _ = v.shape
    vf = v.astype(jnp.float32).reshape(B, S, N_POOL_GROUPS, POOL_GROUP_W)
    cs = jnp.cumsum(vf, axis=1)
    t = jnp.arange(S)
    outs = []
    for g, w in enumerate(POOL_WINDOWS):
        csg = cs[:, :, g]
        lagged = jnp.pad(csg, ((0, 0), (w, 0), (0, 0)))[:, :S]
        cnt = jnp.minimum(t + 1, w).astype(jnp.float32)[None, :, None]
        outs.append((csg - lagged) / cnt - vf[:, :, g])
    return jnp.stack(outs, axis=2)


def chunk_band_attention(q, k, v, rel_bias):
    B, S, H, Dh = q.shape
    n_chunks = S // CHUNK
    pad = LEFT_CHUNKS * CHUNK
    k_pad = jnp.pad(k, ((0, 0), (pad, 0), (0, 0), (0, 0)))
    v_pad = jnp.pad(v, ((0, 0), (pad, 0), (0, 0), (0, 0)))
    qi = jnp.arange(CHUNK)[:, None]
    kj = jnp.arange(BAND)[None, :]
    dist = pad + qi - kj
    idx = jnp.clip(dist, -REL_CLIP, REL_CLIP) + REL_CLIP
    bias = rel_bias[:, idx].astype(jnp.float32)
    sm_scale = HEAD_DIM ** -0.5
    band_pos = jnp.arange(BAND) - pad

    def one_chunk(n):
        start = n * CHUNK
        qb = lax.dynamic_slice_in_dim(q, start, CHUNK, axis=1)
        kb = lax.dynamic_slice_in_dim(k_pad, start, BAND, axis=1)
        vb = lax.dynamic_slice_in_dim(v_pad, start, BAND, axis=1)
        s = jnp.einsum('bqhd,bkhd->bhqk', qb, kb).astype(jnp.float32) * sm_scale + bias[None]
        valid = (start + band_pos) >= 0
        s = jnp.where(valid[None, None, None, :], s, -jnp.inf)
        p = jax.nn.softmax(s, axis=-1)
        return jnp.einsum('bhqk,bkhd->bqhd', p.astype(vb.dtype), vb)

    out = lax.map(one_chunk, jnp.arange(n_chunks))
    return out.transpose(1, 0, 2, 3, 4).reshape(B, S, H * Dh)


def _fwd_setup_inputs(seed: int = 0) -> dict:
    key = jax.random.key(seed)
    ks = jax.random.split(key, 20)
    D, E, Gw = D_MODEL, D_INNER, POOL_GROUP_W
    nrm = jax.random.normal
    return {
        "x": nrm(ks[0], (BATCH, SEQ, D), jnp.float32),
        "c": nrm(ks[1], (BATCH, D), jnp.float32),
        "ada_w": nrm(ks[2], (DEPTH, D, 3 * D), jnp.float32) * D ** -0.5,
        "ada_b": 0.01 * nrm(ks[3], (DEPTH, 3 * D), jnp.float32),
        "norm_g": 1.0 + 0.05 * nrm(ks[4], (DEPTH, D), jnp.float32),
        "a_w_in": nrm(ks[5], (N_A_LAYERS, D, 2 * E), jnp.float32) * D ** -0.5,
        "a_w_group": nrm(ks[6], (N_A_LAYERS, N_POOL_GROUPS, Gw, Gw), jnp.float32) * Gw ** -0.5,
        "a_scale": 1.0 + 0.1 * nrm(ks[7], (N_A_LAYERS, E), jnp.float32),
        "a_w_out": nrm(ks[8], (N_A_LAYERS, E, D), jnp.float32) * E ** -0.5,
        "kv_norm_g": 1.0 + 0.05 * nrm(ks[9], (D,), jnp.float32),
        "kv_ada_w": nrm(ks[10], (D, 2 * D), jnp.float32) * D ** -0.5,
        "kv_ada_b": 0.01 * nrm(ks[11], (2 * D,), jnp.float32),
        "w_kv": nrm(ks[12], (D, 2 * E), jnp.float32) * D ** -0.5,
        "b_w_in": nrm(ks[13], (N_B_LAYERS, D, 2 * E), jnp.float32) * D ** -0.5,
        "b_rel_bias": 0.5 * nrm(ks[14], (N_B_LAYERS, N_HEADS, 2 * REL_CLIP + 1), jnp.float32),
        "b_w_out": nrm(ks[15], (N_B_LAYERS, E, D), jnp.float32) * E ** -0.5,
        "final_g": 1.0 + 0.05 * nrm(ks[16], (D,), jnp.float32),
    }


def _fwd_reference(x, c, ada_w, ada_b, norm_g, a_w_in, a_w_group, a_scale, a_w_out,
              kv_norm_g, kv_ada_w, kv_ada_b, w_kv, b_w_in, b_rel_bias, b_w_out,
              final_g):
    B, S, _ = x.shape
    c_act = jax.nn.silu(c)
    h = x
    k = v = None
    for layer in range(DEPTH):
        shift, scale, gate = jnp.split(c_act @ ada_w[layer] + ada_b[layer], 3, axis=-1)
        u = modulate(rms_norm(h, norm_g[layer]), shift, scale)
        if layer < N_A_LAYERS:
            a = layer
            val, z = jnp.split(u @ a_w_in[a], 2, axis=-1)
            pooled = multiscale_pool(val).astype(val.dtype)
            mixed = jnp.einsum('bsgi,gio->bsgo', pooled, a_w_group[a]).reshape(B, S, D_INNER)
            mixed = mixed * a_scale[a]
            y = (mixed * jax.nn.silu(z)) @ a_w_out[a]
        else:
            if layer == N_A_LAYERS:
                kv_shift, kv_scale = jnp.split(c_act @ kv_ada_w + kv_ada_b, 2, axis=-1)
                hk = modulate(rms_norm(h, kv_norm_g), kv_shift, kv_scale)
                k, v = jnp.split(hk @ w_kv, 2, axis=-1)
                k = k.reshape(B, S, N_HEADS, HEAD_DIM)
                v = v.reshape(B, S, N_HEADS, HEAD_DIM)
            bi = layer - N_A_LAYERS
            qv, z = jnp.split(u @ b_w_in[bi], 2, axis=-1)
            q = qv.reshape(B, S, N_HEADS, HEAD_DIM)
            att = chunk_band_attention(q, k, v, b_rel_bias[bi])
            y = (att * jax.nn.silu(z)) @ b_w_out[bi]
        h = h + gate[:, None, :] * y
    return rms_norm(h, final_g)


import jax as _jax
import jax.numpy as _jnp

TWIN_FORMAT = 'train_step'
FWD_PARAMS = ['x', 'c', 'ada_w', 'ada_b', 'norm_g', 'a_w_in', 'a_w_group', 'a_scale', 'a_w_out', 'kv_norm_g', 'kv_ada_w', 'kv_ada_b', 'w_kv', 'b_w_in', 'b_rel_bias', 'b_w_out', 'final_g']
TWIN_WEIGHTS = ['ada_w', 'ada_b', 'norm_g', 'a_w_in', 'a_w_group', 'a_scale', 'a_w_out', 'kv_norm_g', 'kv_ada_w', 'kv_ada_b', 'w_kv', 'b_w_in', 'b_rel_bias', 'b_w_out', 'final_g']
TWIN_DIFF_INPUT = 'x'
TWIN_INPUTS = ['x', 'c', 'ada_w', 'ada_b', 'norm_g', 'a_w_in', 'a_w_group', 'a_scale', 'a_w_out', 'kv_norm_g', 'kv_ada_w', 'kv_ada_b', 'w_kv', 'b_w_in', 'b_rel_bias', 'b_w_out', 'final_g', 'loss_target', 'm_ada_w', 'm_ada_b', 'm_norm_g', 'm_a_w_in', 'm_a_w_group', 'm_a_scale', 'm_a_w_out', 'm_kv_norm_g', 'm_kv_ada_w', 'm_kv_ada_b', 'm_w_kv', 'm_b_w_in', 'm_b_rel_bias', 'm_b_w_out', 'm_final_g', 'v_ada_w', 'v_ada_b', 'v_norm_g', 'v_a_w_in', 'v_a_w_group', 'v_a_scale', 'v_a_w_out', 'v_kv_norm_g', 'v_kv_ada_w', 'v_kv_ada_b', 'v_w_kv', 'v_b_w_in', 'v_b_rel_bias', 'v_b_w_out', 'v_final_g']
TWIN_OUTPUTS = ['loss', 'grad_x', 'grad_ada_w', 'grad_ada_b', 'grad_norm_g', 'grad_a_w_in', 'grad_a_w_group', 'grad_a_scale', 'grad_a_w_out', 'grad_kv_norm_g', 'grad_kv_ada_w', 'grad_kv_ada_b', 'grad_w_kv', 'grad_b_w_in', 'grad_b_rel_bias', 'grad_b_w_out', 'grad_final_g', 'delta_ada_w', 'delta_ada_b', 'delta_norm_g', 'delta_a_w_in', 'delta_a_w_group', 'delta_a_scale', 'delta_a_w_out', 'delta_kv_norm_g', 'delta_kv_ada_w', 'delta_kv_ada_b', 'delta_w_kv', 'delta_b_w_in', 'delta_b_rel_bias', 'delta_b_w_out', 'delta_final_g', 'new_m_ada_w', 'new_m_ada_b', 'new_m_norm_g', 'new_m_a_w_in', 'new_m_a_w_group', 'new_m_a_scale', 'new_m_a_w_out', 'new_m_kv_norm_g', 'new_m_kv_ada_w', 'new_m_kv_ada_b', 'new_m_w_kv', 'new_m_b_w_in', 'new_m_b_rel_bias', 'new_m_b_w_out', 'new_m_final_g', 'new_v_ada_w', 'new_v_ada_b', 'new_v_norm_g', 'new_v_a_w_in', 'new_v_a_w_group', 'new_v_a_scale', 'new_v_a_w_out', 'new_v_kv_norm_g', 'new_v_kv_ada_w', 'new_v_kv_ada_b', 'new_v_w_kv', 'new_v_b_w_in', 'new_v_b_rel_bias', 'new_v_b_w_out', 'new_v_final_g']
TWIN_LEAF_KINDS = {'loss': 'loss', 'grad_x': 'grad_x', 'grad_ada_w': 'grad_w', 'grad_ada_b': 'grad_w', 'grad_norm_g': 'grad_w', 'grad_a_w_in': 'grad_w', 'grad_a_w_group': 'grad_w', 'grad_a_scale': 'grad_w', 'grad_a_w_out': 'grad_w', 'grad_kv_norm_g': 'grad_w', 'grad_kv_ada_w': 'grad_w', 'grad_kv_ada_b': 'grad_w', 'grad_w_kv': 'grad_w', 'grad_b_w_in': 'grad_w', 'grad_b_rel_bias': 'grad_w', 'grad_b_w_out': 'grad_w', 'grad_final_g': 'grad_w', 'delta_ada_w': 'delta_w', 'delta_ada_b': 'delta_w', 'delta_norm_g': 'delta_w', 'delta_a_w_in': 'delta_w', 'delta_a_w_group': 'delta_w', 'delta_a_scale': 'delta_w', 'delta_a_w_out': 'delta_w', 'delta_kv_norm_g': 'delta_w', 'delta_kv_ada_w': 'delta_w', 'delta_kv_ada_b': 'delta_w', 'delta_w_kv': 'delta_w', 'delta_b_w_in': 'delta_w', 'delta_b_rel_bias': 'delta_w', 'delta_b_w_out': 'delta_w', 'delta_final_g': 'delta_w', 'new_m_ada_w': 'new_m', 'new_m_ada_b': 'new_m', 'new_m_norm_g': 'new_m', 'new_m_a_w_in': 'new_m', 'new_m_a_w_group': 'new_m', 'new_m_a_scale': 'new_m', 'new_m_a_w_out': 'new_m', 'new_m_kv_norm_g': 'new_m', 'new_m_kv_ada_w': 'new_m', 'new_m_kv_ada_b': 'new_m', 'new_m_w_kv': 'new_m', 'new_m_b_w_in': 'new_m', 'new_m_b_rel_bias': 'new_m', 'new_m_b_w_out': 'new_m', 'new_m_final_g': 'new_m', 'new_v_ada_w': 'new_v', 'new_v_ada_b': 'new_v', 'new_v_norm_g': 'new_v', 'new_v_a_w_in': 'new_v', 'new_v_a_w_group': 'new_v', 'new_v_a_scale': 'new_v', 'new_v_a_w_out': 'new_v', 'new_v_kv_norm_g': 'new_v', 'new_v_kv_ada_w': 'new_v', 'new_v_kv_ada_b': 'new_v', 'new_v_w_kv': 'new_v', 'new_v_b_w_in': 'new_v', 'new_v_b_rel_bias': 'new_v', 'new_v_b_w_out': 'new_v', 'new_v_final_g': 'new_v'}


def _forward(args):
    return _fwd_reference(*[args[k] for k in FWD_PARAMS])


def _output_shape():
    def fwd():
        inp = _fwd_setup_inputs(0)
        return _fwd_reference(*[inp[k] for k in FWD_PARAMS])
    out = _jax.eval_shape(fwd)
    return out.shape, out.dtype

N_MICROBATCH = 1
ADAM_LR = 0.001
ADAM_B1 = 0.9
ADAM_B2 = 0.999
ADAM_EPS = 1e-08
ADAM_WD = 0.01
ADAM_STEP = 10
PER_EXAMPLE_BATCH_AXIS = {'x': 0, 'c': 0, 'loss_target': 0}
SHARED_INPUTS = []
_WEIGHT_DTYPES = {'ada_w': _jnp.float32, 'ada_b': _jnp.float32, 'norm_g': _jnp.float32, 'a_w_in': _jnp.float32, 'a_w_group': _jnp.float32, 'a_scale': _jnp.float32, 'a_w_out': _jnp.float32, 'kv_norm_g': _jnp.float32, 'kv_ada_w': _jnp.float32, 'kv_ada_b': _jnp.float32, 'w_kv': _jnp.float32, 'b_w_in': _jnp.float32, 'b_rel_bias': _jnp.float32, 'b_w_out': _jnp.float32, 'final_g': _jnp.float32}
MOMENT_SCALE = {'ada_w': 1.104178e-01, 'ada_b': 2.256744e-01, 'norm_g': 1.250620e-01, 'a_w_in': 8.539007e-02, 'a_w_group': 8.393865e-02, 'a_scale': 8.439752e-02, 'a_w_out': 1.187944e-01, 'kv_norm_g': 6.481398e-02, 'kv_ada_w': 6.423805e-02, 'kv_ada_b': 1.113073e-01, 'w_kv': 5.319115e-02, 'b_w_in': 3.821527e-02, 'b_rel_bias': 1.119486e-02, 'b_w_out': 6.645804e-02, 'final_g': 6.460660e+01}


def _to_microbatches(a, axis):
    t = _jnp.moveaxis(a, axis, 0)
    t = t.reshape((N_MICROBATCH, t.shape[0] // N_MICROBATCH) + t.shape[1:])
    return _jnp.moveaxis(t, 1, axis + 1)


def setup_inputs(seed: int = 0) -> dict:
    inp = _fwd_setup_inputs(seed)
    key = _jax.random.fold_in(_jax.random.key(seed), 7919)
    shape, _ = _output_shape()
    out = dict(inp)
    out["loss_target"] = _jax.random.normal(_jax.random.fold_in(key, 0), shape, _jnp.float32)
    for i, name in enumerate(TWIN_WEIGHTS):
        w = inp[name].astype(_jnp.float32)
        if MOMENT_SCALE is None:
            s = _jnp.sqrt(_jnp.mean(_jnp.square(w)) + 1e-30)
        else:
            s = MOMENT_SCALE[name]
        km, kv = _jax.random.split(_jax.random.fold_in(key, i + 1))
        out[name] = w
        out["m_" + name] = s * _jax.random.normal(km, w.shape, _jnp.float32)
        out["v_" + name] = (s * s) * _jax.random.uniform(kv, w.shape, _jnp.float32, 0.5, 1.5)
    if N_MICROBATCH > 1:
        for name, axis in PER_EXAMPLE_BATCH_AXIS.items():
            out[name] = _to_microbatches(out[name], axis)
    return {'x': out['x'], 'c': out['c'], 'ada_w': out['ada_w'], 'ada_b': out['ada_b'], 'norm_g': out['norm_g'], 'a_w_in': out['a_w_in'], 'a_w_group': out['a_w_group'], 'a_scale': out['a_scale'], 'a_w_out': out['a_w_out'], 'kv_norm_g': out['kv_norm_g'], 'kv_ada_w': out['kv_ada_w'], 'kv_ada_b': out['kv_ada_b'], 'w_kv': out['w_kv'], 'b_w_in': out['b_w_in'], 'b_rel_bias': out['b_rel_bias'], 'b_w_out': out['b_w_out'], 'final_g': out['final_g'], 'loss_target': out['loss_target'], 'm_ada_w': out['m_ada_w'], 'm_ada_b': out['m_ada_b'], 'm_norm_g': out['m_norm_g'], 'm_a_w_in': out['m_a_w_in'], 'm_a_w_group': out['m_a_w_group'], 'm_a_scale': out['m_a_scale'], 'm_a_w_out': out['m_a_w_out'], 'm_kv_norm_g': out['m_kv_norm_g'], 'm_kv_ada_w': out['m_kv_ada_w'], 'm_kv_ada_b': out['m_kv_ada_b'], 'm_w_kv': out['m_w_kv'], 'm_b_w_in': out['m_b_w_in'], 'm_b_rel_bias': out['m_b_rel_bias'], 'm_b_w_out': out['m_b_w_out'], 'm_final_g': out['m_final_g'], 'v_ada_w': out['v_ada_w'], 'v_ada_b': out['v_ada_b'], 'v_norm_g': out['v_norm_g'], 'v_a_w_in': out['v_a_w_in'], 'v_a_w_group': out['v_a_w_group'], 'v_a_scale': out['v_a_scale'], 'v_a_w_out': out['v_a_w_out'], 'v_kv_norm_g': out['v_kv_norm_g'], 'v_kv_ada_w': out['v_kv_ada_w'], 'v_kv_ada_b': out['v_kv_ada_b'], 'v_w_kv': out['v_w_kv'], 'v_b_w_in': out['v_b_w_in'], 'v_b_rel_bias': out['v_b_rel_bias'], 'v_b_w_out': out['v_b_w_out'], 'v_final_g': out['v_final_g']}


def _loss(weights, diff, rest, loss_target):
    with _jax.named_scope("forward"):
        args = {**rest, TWIN_DIFF_INPUT: diff, **{k: w.astype(_WEIGHT_DTYPES[k]) for k, w in weights.items()}}
        y = _forward(args)
    with _jax.named_scope("loss_head"):
        err = _jnp.square(y.astype(_jnp.float32) - loss_target)
        return 0.5 * _jnp.sum(_jnp.mean(err, axis=-1)) if err.ndim else 0.5 * err


def _adamw(w, g, m, v):
    m = ADAM_B1 * m + (1.0 - ADAM_B1) * g
    v = ADAM_B2 * v + (1.0 - ADAM_B2) * _jnp.square(g)
    m_hat = m / (1.0 - ADAM_B1 ** ADAM_STEP)
    v_hat = v / (1.0 - ADAM_B2 ** ADAM_STEP)
    delta = -ADAM_LR * (m_hat / (_jnp.sqrt(v_hat) + ADAM_EPS) + ADAM_WD * w)
    return delta, m, v


def reference(x, c, ada_w, ada_b, norm_g, a_w_in, a_w_group, a_scale, a_w_out, kv_norm_g, kv_ada_w, kv_ada_b, w_kv, b_w_in, b_rel_bias, b_w_out, final_g, loss_target, m_ada_w, m_ada_b, m_norm_g, m_a_w_in, m_a_w_group, m_a_scale, m_a_w_out, m_kv_norm_g, m_kv_ada_w, m_kv_ada_b, m_w_kv, m_b_w_in, m_b_rel_bias, m_b_w_out, m_final_g, v_ada_w, v_ada_b, v_norm_g, v_a_w_in, v_a_w_group, v_a_scale, v_a_w_out, v_kv_norm_g, v_kv_ada_w, v_kv_ada_b, v_w_kv, v_b_w_in, v_b_rel_bias, v_b_w_out, v_final_g):
    given = dict(x=x, c=c, ada_w=ada_w, ada_b=ada_b, norm_g=norm_g, a_w_in=a_w_in, a_w_group=a_w_group, a_scale=a_scale, a_w_out=a_w_out, kv_norm_g=kv_norm_g, kv_ada_w=kv_ada_w, kv_ada_b=kv_ada_b, w_kv=w_kv, b_w_in=b_w_in, b_rel_bias=b_rel_bias, b_w_out=b_w_out, final_g=final_g, loss_target=loss_target, m_ada_w=m_ada_w, m_ada_b=m_ada_b, m_norm_g=m_norm_g, m_a_w_in=m_a_w_in, m_a_w_group=m_a_w_group, m_a_scale=m_a_scale, m_a_w_out=m_a_w_out, m_kv_norm_g=m_kv_norm_g, m_kv_ada_w=m_kv_ada_w, m_kv_ada_b=m_kv_ada_b, m_w_kv=m_w_kv, m_b_w_in=m_b_w_in, m_b_rel_bias=m_b_rel_bias, m_b_w_out=m_b_w_out, m_final_g=m_final_g, v_ada_w=v_ada_w, v_ada_b=v_ada_b, v_norm_g=v_norm_g, v_a_w_in=v_a_w_in, v_a_w_group=v_a_w_group, v_a_scale=v_a_scale, v_a_w_out=v_a_w_out, v_kv_norm_g=v_kv_norm_g, v_kv_ada_w=v_kv_ada_w, v_kv_ada_b=v_kv_ada_b, v_w_kv=v_w_kv, v_b_w_in=v_b_w_in, v_b_rel_bias=v_b_rel_bias, v_b_w_out=v_b_w_out, v_final_g=v_final_g)
    weights = {n: given[n] for n in TWIN_WEIGHTS}
    shared = {n: given[n] for n in SHARED_INPUTS}
    per_example = {n: given[n] for n in ['x', 'c']}
    grad_fn = _jax.value_and_grad(_loss, argnums=(0, 1))

    def one_microbatch(ex, loss_target):
        ex = dict(ex)
        diff = ex.pop(TWIN_DIFF_INPUT)
        return grad_fn(weights, diff, {**shared, **ex}, loss_target)

    if N_MICROBATCH == 1:
        loss, (grad_w, grad_x) = one_microbatch(per_example, given["loss_target"])
    else:
        def body(carry, xs):
            loss_sum, grad_sum = carry
            l_k, (gw_k, gx_k) = one_microbatch(xs[0], xs[1])
            with _jax.named_scope("update"):
                return (loss_sum + l_k, _jax.tree.map(_jnp.add, grad_sum, gw_k)), gx_k

        init = (_jnp.zeros((), _jnp.float32), _jax.tree.map(_jnp.zeros_like, weights))
        (loss, grad_w), grad_x = _jax.lax.scan(body, init, (per_example, given["loss_target"]))
    with _jax.named_scope("update"):
        delta_w, new_m, new_v = {}, {}, {}
        for n in TWIN_WEIGHTS:
            delta_w[n], new_m[n], new_v[n] = _adamw(weights[n], grad_w[n], given["m_" + n], given["v_" + n])
    return (loss, grad_x, *[grad_w[n] for n in TWIN_WEIGHTS], *[delta_w[n] for n in TWIN_WEIGHTS],
            *[new_m[n] for n in TWIN_WEIGHTS], *[new_v[n] for n in TWIN_WEIGHTS])
```

```python
import jax
import jax.numpy as jnp
from jax import lax
from jax.experimental import pallas as pl
from jax.experimental.pallas import tpu as pltpu

F32 = jnp.float32
BF16 = jnp.bfloat16

D = 1024
E = 2048
NH = 16
HD = 128
CHUNK = 64
LEFT = 8
PAD = LEFT * CHUNK
NREL = 257
NRELP = 384
REL_CLIP = 128
EPS = 1e-6
NEG = -1e30
SM_SCALE = HD ** -0.5
POOL_W = (2, 4, 8, 16)
GW = 512
HALO = 16
QC = 4
QB = QC * CHUNK
WIN = (QC + LEFT) * CHUNK
BW = (LEFT + 2) * CHUNK
NCHIP = 4
LANES = 128
SUBLANES = 8

ADAM_LR = 0.001
ADAM_B1 = 0.9
ADAM_B2 = 0.999
ADAM_EPS = 1e-08
ADAM_WD = 0.01
ADAM_STEP = 10

MESH = pl.DeviceIdType.MESH
ANY = pl.BlockSpec(memory_space=pl.ANY)


def _params(n_axes, vmem_mb):
    return pltpu.CompilerParams(dimension_semantics=("arbitrary",) * n_axes, vmem_limit_bytes=vmem_mb * 2 ** 20)


def _nn(a, b):
    return jnp.dot(a, b, preferred_element_type=F32)


def _nt(a, b):
    return lax.dot_general(a, b, (((1,), (1,)), ((), ())), preferred_element_type=F32)


def _tn(a, b):
    return lax.dot_general(a, b, (((0,), (0,)), ((), ())), preferred_element_type=F32)


def _row(n):
    return pl.BlockSpec((1, n), lambda i: (0, 0))


def _colsum(x):
    return jnp.sum(x, axis=0, keepdims=True)


def _weight_src(w_hbm, layer):
    return w_hbm if layer is None else w_hbm.at[:, layer]


def _in_fwd(h, g, shift, scale, w, layer, dt_a, dt_b, name, tm=512):
    S = h.shape[0]

    def body(h_ref, g_ref, sh_ref, sc_ref, w_hbm, u_ref, oa_ref, ob_ref, w_v, sem):
        @pl.when(pl.program_id(0) == 0)
        def _():
            cp = pltpu.make_async_copy(_weight_src(w_hbm, layer), w_v, sem)
            cp.start()
            cp.wait()

        hh = h_ref[...]
        r = lax.rsqrt(jnp.mean(hh * hh, axis=-1, keepdims=True) + EPS)
        u = (hh * r * g_ref[...]) * (1.0 + sc_ref[...]) + sh_ref[...]
        ub = u.astype(BF16)
        u_ref[...] = ub
        for q in range(NCHIP):
            o_ref = oa_ref if q < 2 else ob_ref
            o_ref[:, (q % 2) * D:(q % 2 + 1) * D] = _nn(ub, w_v[q]).astype(o_ref.dtype)

    return pl.pallas_call(
        body, name=name, grid=(S // tm,),
        in_specs=[pl.BlockSpec((tm, D), lambda i: (i, 0)), _row(D), _row(D), _row(D), ANY],
        out_specs=[pl.BlockSpec((tm, D), lambda i: (i, 0)), pl.BlockSpec((tm, E), lambda i: (i, 0)),
                   pl.BlockSpec((tm, E), lambda i: (i, 0))],
        out_shape=[jax.ShapeDtypeStruct((S, D), BF16), jax.ShapeDtypeStruct((S, E), dt_a),
                   jax.ShapeDtypeStruct((S, E), dt_b)],
        scratch_shapes=[pltpu.VMEM((NCHIP, D, D), BF16), pltpu.SemaphoreType.DMA],
        compiler_params=_params(1, 52),
    )(h, g, shift, scale, w)


def _pool_fwd(val, wg, name, tm=512):
    S = val.shape[0]

    def body(v_ref, wg_ref, p_ref, m_ref, buf):
        i = pl.program_id(0)

        @pl.when(i == 0)
        def _():
            buf[0:HALO, :] = jnp.zeros((HALO, E), F32)

        buf[HALO:HALO + tm, :] = v_ref[...]
        t = i * tm + lax.broadcasted_iota(jnp.int32, (tm, 1), 0)
        for gi, w in enumerate(POOL_W):
            cols = slice(gi * GW, (gi + 1) * GW)
            x = buf[:, cols]
            s = x
            k = 1
            while k < w:
                s = s + pltpu.roll(s, k, 0)
                k *= 2
            cnt = jnp.minimum(t + 1, w).astype(F32)
            pb = (s[HALO:, :] / cnt - x[HALO:, :]).astype(BF16)
            p_ref[:, cols] = pb
            m_ref[:, cols] = _nn(pb, wg_ref[gi])
        buf[0:HALO, :] = buf[tm:tm + HALO, :]

    return pl.pallas_call(
        body, name=name, grid=(S // tm,),
        in_specs=[pl.BlockSpec((tm, E), lambda i: (i, 0)), pl.BlockSpec((4, GW, GW), lambda i: (0, 0, 0))],
        out_specs=[pl.BlockSpec((tm, E), lambda i: (i, 0)), pl.BlockSpec((tm, E), lambda i: (i, 0))],
        out_shape=[jax.ShapeDtypeStruct((S, E), BF16), jax.ShapeDtypeStruct((S, E), F32)],
        scratch_shapes=[pltpu.VMEM((tm + HALO, E), F32)],
        compiler_params=_params(1, 52),
    )(val, wg)


def _out_fwd(a, cs, z, w, layer, gate, h, name, tm=512):
    S = h.shape[0]
    kb = E // NCHIP

    def body(a_ref, cs_ref, z_ref, w_hbm, gate_ref, h_ref, y_ref, ho_ref, w_v, sem):
        @pl.when(pl.program_id(0) == 0)
        def _():
            cp = pltpu.make_async_copy(_weight_src(w_hbm, layer), w_v, sem)
            cp.start()
            cp.wait()

        y = None
        for p in range(NCHIP):
            cols = slice(p * kb, (p + 1) * kb)
            zz = z_ref[:, cols]
            act = ((a_ref[:, cols] * cs_ref[:, cols]) * (zz * jax.nn.sigmoid(zz))).astype(BF16)
            part = _nn(act, w_v[p])
            y = part if y is None else y + part
        y_ref[...] = y
        ho_ref[...] = h_ref[...] + gate_ref[...] * y

    return pl.pallas_call(
        body, name=name, grid=(S // tm,),
        in_specs=[pl.BlockSpec((tm, E), lambda i: (i, 0)), _row(E), pl.BlockSpec((tm, E), lambda i: (i, 0)), ANY,
                  _row(D), pl.BlockSpec((tm, D), lambda i: (i, 0))],
        out_specs=[pl.BlockSpec((tm, D), lambda i: (i, 0)), pl.BlockSpec((tm, D), lambda i: (i, 0))],
        out_shape=[jax.ShapeDtypeStruct((S, D), F32), jax.ShapeDtypeStruct((S, D), F32)],
        scratch_shapes=[pltpu.VMEM((NCHIP, kb, D), BF16), pltpu.SemaphoreType.DMA],
        compiler_params=_params(1, 52),
    )(a, cs, z, w, gate, h)


def _bias_onehot(qi, shifted, transpose):
    shape = (BW, NRELP) if transpose else (NRELP, BW)
    m = lax.broadcasted_iota(jnp.int32, shape, 0 if transpose else 1)
    r = lax.broadcasted_iota(jnp.int32, shape, 1 if transpose else 0)
    kj = m - CHUNK if shifted else m
    valid = (kj >= 0) & (kj < (LEFT + 1) * CHUNK)
    idx = jnp.clip(PAD + qi - kj, -REL_CLIP, REL_CLIP) + REL_CLIP
    return jnp.where(valid & (idx == r), 1.0, 0.0).astype(BF16), valid


def _bias_build(rb, name):
    def body(rb_ref, a_ref, b_ref):
        qi = pl.program_id(0)
        x = rb_ref[...]
        hi = x.astype(BF16)
        r1 = x - hi.astype(F32)
        mid = r1.astype(BF16)
        lo = (r1 - mid.astype(F32)).astype(BF16)
        for shifted, o_ref in ((False, a_ref), (True, b_ref)):
            oh, _ = _bias_onehot(qi, shifted, False)
            m = lax.broadcasted_iota(jnp.int32, (NH, BW), 1)
            kj = m - CHUNK if shifted else m
            valid = (kj >= 0) & (kj < (LEFT + 1) * CHUNK)
            val = (_nn(hi, oh) + _nn(mid, oh)) + _nn(lo, oh)
            o_ref[...] = jnp.where(valid, val, NEG)

    return pl.pallas_call(
        body, name=name, grid=(CHUNK,),
        in_specs=[pl.BlockSpec((NH, NRELP), lambda i: (0, 0))],
        out_specs=[pl.BlockSpec((None, NH, BW), lambda i: (i, 0, 0)), pl.BlockSpec((None, NH, BW), lambda i: (i, 0, 0))],
        out_shape=[jax.ShapeDtypeStruct((CHUNK, NH, BW), F32), jax.ShapeDtypeStruct((CHUNK, NH, BW), F32)],
        compiler_params=_params(1, 32),
    )(rb)


def _dbias_reduce(dba, dbb, name):
    def body(a_ref, b_ref, o_ref):
        qi = pl.program_id(0)

        @pl.when(qi == 0)
        def _():
            o_ref[...] = jnp.zeros((NH, NRELP), F32)

        acc = None
        for shifted, i_ref in ((False, a_ref), (True, b_ref)):
            oh, _ = _bias_onehot(qi, shifted, True)
            x = i_ref[...]
            hi = x.astype(BF16)
            mid = (x - hi.astype(F32)).astype(BF16)
            part = _nn(hi, oh) + _nn(mid, oh)
            acc = part if acc is None else acc + part
        o_ref[...] += acc

    return pl.pallas_call(
        body, name=name, grid=(CHUNK,),
        in_specs=[pl.BlockSpec((None, NH, BW), lambda i: (i, 0, 0)), pl.BlockSpec((None, NH, BW), lambda i: (i, 0, 0))],
        out_specs=pl.BlockSpec((NH, NRELP), lambda i: (0, 0)),
        out_shape=jax.ShapeDtypeStruct((NH, NRELP), F32),
        compiler_params=_params(1, 32),
    )(dba, dbb)


def _build_bfull(bfull, ba_ref, bb_ref):
    bfull[...] = jnp.full((QB, WIN), NEG, F32)
    for qc in range(QC):
        rows = slice(qc * CHUNK, (qc + 1) * CHUNK)
        if qc % 2 == 0:
            bfull[rows, qc * CHUNK:qc * CHUNK + BW] = ba_ref[...]
        else:
            bfull[rows, (qc - 1) * CHUNK:(qc - 1) * CHUNK + BW] = bb_ref[...]


def _scores(q_ref, k_ref, bfull, sub, row0):
    qq = q_ref[sub * QB:(sub + 1) * QB, :]
    kw = k_ref[pl.ds(row0, WIN), :]
    col = lax.broadcasted_iota(jnp.int32, (QB, WIN), 1)
    s = _nt(qq, kw) * SM_SCALE + bfull[...]
    s = jnp.where(col < PAD - row0, NEG, s)
    m = jnp.max(s, axis=-1, keepdims=True)
    e = jnp.exp(s - m)
    p = e / jnp.sum(e, axis=-1, keepdims=True)
    return qq, kw, p


def _attn_fwd(q, kp, vp, ba, bb, name):
    S = q.shape[0]
    nsub = min(4, S // QB)
    R = nsub * QB

    def body(q_ref, k_ref, v_ref, ba_ref, bb_ref, o_ref, bfull):
        i = pl.program_id(1)

        @pl.when(i == 0)
        def _():
            _build_bfull(bfull, ba_ref, bb_ref)

        for sub in range(nsub):
            row0 = pl.multiple_of((i * nsub + sub) * QB, QB)
            _, _, p = _scores(q_ref, k_ref, bfull, sub, row0)
            o_ref[sub * QB:(sub + 1) * QB, :] = _nn(p.astype(BF16), v_ref[pl.ds(row0, WIN), :])

    return pl.pallas_call(
        body, name=name, grid=(NH, S // R),
        in_specs=[pl.BlockSpec((R, HD), lambda h, i: (i, h)), pl.BlockSpec((S + PAD, HD), lambda h, i: (0, h)),
                  pl.BlockSpec((S + PAD, HD), lambda h, i: (0, h)), pl.BlockSpec((None, CHUNK, BW), lambda h, i: (h, 0, 0)),
                  pl.BlockSpec((None, CHUNK, BW), lambda h, i: (h, 0, 0))],
        out_specs=pl.BlockSpec((R, HD), lambda h, i: (i, h)),
        out_shape=jax.ShapeDtypeStruct((S, E), F32),
        scratch_shapes=[pltpu.VMEM((QB, WIN), F32)],
        compiler_params=_params(2, 48),
    )(q, kp, vp, ba, bb)


def _final(h, g, tgt, name, tm=512):
    S = h.shape[0]

    def body(h_ref, g_ref, t_ref, dh_ref, st_ref):
        @pl.when(pl.program_id(0) == 0)
        def _():
            st_ref[...] = jnp.zeros((SUBLANES, D), F32)

        hh = h_ref[...]
        r = lax.rsqrt(jnp.mean(hh * hh, axis=-1, keepdims=True) + EPS)
        xhat = hh * r
        diff = xhat * g_ref[...] - t_ref[...]
        st_ref[1:2, :] += _colsum(diff * diff)
        dout = diff * (1.0 / D)
        st_ref[0:1, :] += _colsum(dout * xhat)
        dx = dout * g_ref[...]
        dh_ref[...] = r * (dx - xhat * jnp.mean(dx * xhat, axis=-1, keepdims=True))

    return pl.pallas_call(
        body, name=name, grid=(S // tm,),
        in_specs=[pl.BlockSpec((tm, D), lambda i: (i, 0)), _row(D), pl.BlockSpec((tm, D), lambda i: (i, 0))],
        out_specs=[pl.BlockSpec((tm, D), lambda i: (i, 0)), pl.BlockSpec((SUBLANES, D), lambda i: (0, 0))],
        out_shape=[jax.ShapeDtypeStruct((S, D), F32), jax.ShapeDtypeStruct((SUBLANES, D), F32)],
        compiler_params=_params(1, 32),
    )(h, g, tgt)


def _out_bwd(dh, y, gate, a, cs, z, w, layer, dt_da, name, tm=256):
    S = dh.shape[0]
    kb = E // NCHIP
    n_t = S // tm

    def body(dh_ref, y_ref, gate_ref, a_ref, cs_ref, z_ref, w_hbm, da_ref, dz_ref, dw_hbm, st_ref, w_v, acc, sem):
        i = pl.program_id(0)

        @pl.when(i == 0)
        def _():
            cp = pltpu.make_async_copy(_weight_src(w_hbm, layer), w_v, sem)
            cp.start()
            acc[...] = jnp.zeros(acc.shape, F32)
            st_ref[...] = jnp.zeros((SUBLANES, D), F32)
            cp.wait()

        dhh = dh_ref[...]
        st_ref[0:1, :] += _colsum(dhh * y_ref[...])
        dy = (dhh * gate_ref[...]).astype(BF16)
        for p in range(NCHIP):
            cols = slice(p * kb, (p + 1) * kb)
            zz = z_ref[:, cols]
            sig = jax.nn.sigmoid(zz)
            sz = zz * sig
            ae = a_ref[:, cols] * cs_ref[:, cols]
            acc[p] += _tn((ae * sz).astype(BF16), dy)
            dact = _nt(dy, w_v[p])
            da_ref[:, cols] = (dact * sz).astype(da_ref.dtype)
            dz_ref[:, cols] = (dact * ae * (sig * (1.0 + zz * (1.0 - sig)))).astype(BF16)

        @pl.when(i == n_t - 1)
        def _():
            cp = pltpu.make_async_copy(acc, dw_hbm, sem)
            cp.start()
            cp.wait()

    return pl.pallas_call(
        body, name=name, grid=(n_t,),
        in_specs=[pl.BlockSpec((tm, D), lambda i: (i, 0)), pl.BlockSpec((tm, D), lambda i: (i, 0)), _row(D),
                  pl.BlockSpec((tm, E), lambda i: (i, 0)), _row(E), pl.BlockSpec((tm, E), lambda i: (i, 0)), ANY],
        out_specs=[pl.BlockSpec((tm, E), lambda i: (i, 0)), pl.BlockSpec((tm, E), lambda i: (i, 0)), ANY,
                   pl.BlockSpec((SUBLANES, D), lambda i: (0, 0))],
        out_shape=[jax.ShapeDtypeStruct((S, E), dt_da), jax.ShapeDtypeStruct((S, E), BF16),
                   jax.ShapeDtypeStruct((NCHIP, kb, D), F32), jax.ShapeDtypeStruct((SUBLANES, D), F32)],
        scratch_shapes=[pltpu.VMEM((NCHIP, kb, D), BF16), pltpu.VMEM((NCHIP, kb, D), F32), pltpu.SemaphoreType.DMA],
        compiler_params=_params(1, 52),
    )(dh, y, gate, a, cs, z, w)


def _attn_bwd(q, kp, vp, ba, bb, do, name):
    S = q.shape[0]
    nsub = min(4, S // QB)
    R = nsub * QB
    n_i = S // R

    def body(q_ref, k_ref, v_ref, ba_ref, bb_ref, do_ref, dq_ref, dk_ref, dv_ref, dba_ref, dbb_ref, bfull, dbfull):
        i = pl.program_id(1)

        @pl.when(i == 0)
        def _():
            _build_bfull(bfull, ba_ref, bb_ref)
            dbfull[...] = jnp.zeros((QB, WIN), F32)
            dk_ref[...] = jnp.zeros((S + PAD, HD), F32)
            dv_ref[...] = jnp.zeros((S + PAD, HD), F32)

        for sub in range(nsub):
            rows = slice(sub * QB, (sub + 1) * QB)
            row0 = pl.multiple_of((i * nsub + sub) * QB, QB)
            qq, kw, p = _scores(q_ref, k_ref, bfull, sub, row0)
            dd = do_ref[rows, :]
            dp = _nt(dd, v_ref[pl.ds(row0, WIN), :])
            ds = p * (dp - jnp.sum(p * dp, axis=-1, keepdims=True))
            dbfull[...] += ds
            dsb = (ds * SM_SCALE).astype(BF16)
            dq_ref[rows, :] = _nn(dsb, kw).astype(BF16)
            dk_ref[pl.ds(row0, WIN), :] += _tn(dsb, qq)
            dv_ref[pl.ds(row0, WIN), :] += _tn(p.astype(BF16), dd)

        @pl.when(i == n_i - 1)
        def _():
            da = None
            db = None
            for qc in range(QC):
                rows = slice(qc * CHUNK, (qc + 1) * CHUNK)
                if qc % 2 == 0:
                    part = dbfull[rows, qc * CHUNK:qc * CHUNK + BW]
                    da = part if da is None else da + part
                else:
                    part = dbfull[rows, (qc - 1) * CHUNK:(qc - 1) * CHUNK + BW]
                    db = part if db is None else db + part
            dba_ref[...] = da
            dbb_ref[...] = db

    return pl.pallas_call(
        body, name=name, grid=(NH, n_i),
        in_specs=[pl.BlockSpec((R, HD), lambda h, i: (i, h)), pl.BlockSpec((S + PAD, HD), lambda h, i: (0, h)),
                  pl.BlockSpec((S + PAD, HD), lambda h, i: (0, h)), pl.BlockSpec((None, CHUNK, BW), lambda h, i: (h, 0, 0)),
                  pl.BlockSpec((None, CHUNK, BW), lambda h, i: (h, 0, 0)), pl.BlockSpec((R, HD), lambda h, i: (i, h))],
        out_specs=[pl.BlockSpec((R, HD), lambda h, i: (i, h)), pl.BlockSpec((S + PAD, HD), lambda h, i: (0, h)),
                   pl.BlockSpec((S + PAD, HD), lambda h, i: (0, h)), pl.BlockSpec((None, CHUNK, BW), lambda h, i: (h, 0, 0)),
                   pl.BlockSpec((None, CHUNK, BW), lambda h, i: (h, 0, 0))],
        out_shape=[jax.ShapeDtypeStruct((S, E), BF16), jax.ShapeDtypeStruct((S + PAD, E), F32),
                   jax.ShapeDtypeStruct((S + PAD, E), F32), jax.ShapeDtypeStruct((NH, CHUNK, BW), F32),
                   jax.ShapeDtypeStruct((NH, CHUNK, BW), F32)],
        scratch_shapes=[pltpu.VMEM((QB, WIN), F32), pltpu.VMEM((QB, WIN), F32)],
        compiler_params=_params(2, 52),
    )(q, kp, vp, ba, bb, do)


def _pool_bwd(dms, mixed, pooled, wg, a_scale, name, tm=512):
    S = dms.shape[0]
    n_t = S // tm

    def rev(i):
        return (n_t - 1 - i, 0)

    def body(d_ref, m_ref, p_ref, wg_ref, as_ref, dv_ref, dwg_ref, st_ref, buf):
        i = pl.program_id(0)

        @pl.when(i == 0)
        def _():
            buf[tm:tm + HALO, :] = jnp.zeros((HALO, E), F32)
            dwg_ref[...] = jnp.zeros((4, GW, GW), F32)
            st_ref[...] = jnp.zeros((SUBLANES, E), F32)

        t = (n_t - 1 - i) * tm + lax.broadcasted_iota(jnp.int32, (tm, 1), 0)
        st_ref[0:1, :] += _colsum(d_ref[...] * m_ref[...])
        for gi, w in enumerate(POOL_W):
            cols = slice(gi * GW, (gi + 1) * GW)
            dm = (d_ref[:, cols] * as_ref[:, cols]).astype(BF16)
            dpool = _nt(dm, wg_ref[gi])
            dwg_ref[gi] += _tn(p_ref[:, cols], dm)
            cnt = jnp.minimum(t + 1, w).astype(F32)
            buf[0:tm, cols] = dpool / cnt
            s = buf[:, cols]
            k = 1
            while k < w:
                s = s + pltpu.roll(s, tm + HALO - k, 0)
                k *= 2
            dv_ref[:, cols] = (s[0:tm, :] - dpool).astype(BF16)
        buf[tm:tm + HALO, :] = buf[0:HALO, :]

    return pl.pallas_call(
        body, name=name, grid=(n_t,),
        in_specs=[pl.BlockSpec((tm, E), rev), pl.BlockSpec((tm, E), rev), pl.BlockSpec((tm, E), rev),
                  pl.BlockSpec((4, GW, GW), lambda i: (0, 0, 0)), _row(E)],
        out_specs=[pl.BlockSpec((tm, E), rev), pl.BlockSpec((4, GW, GW), lambda i: (0, 0, 0)),
                   pl.BlockSpec((SUBLANES, E), lambda i: (0, 0))],
        out_shape=[jax.ShapeDtypeStruct((S, E), BF16), jax.ShapeDtypeStruct((4, GW, GW), F32),
                   jax.ShapeDtypeStruct((SUBLANES, E), F32)],
        scratch_shapes=[pltpu.VMEM((tm + HALO, E), F32)],
        compiler_params=_params(1, 52),
    )(dms, mixed, pooled, wg, a_scale)


def _in_bwd(parts_a, parts_b, row_off, u, h, g, scale, w, layer, dh_out, name, tm=256):
    S = h.shape[0]
    n_t = S // tm
    na, nb = len(parts_a), len(parts_b)
    off = row_off // tm

    def body(*refs):
        pa = refs[:na]
        pb = refs[na:na + nb]
        u_ref, h_ref, g_ref, sc_ref, w_hbm, dho_ref, dhi_ref, dw_hbm, st_ref, w_v, acc, sem = refs[na + nb:]
        i = pl.program_id(0)

        @pl.when(i == 0)
        def _():
            cp = pltpu.make_async_copy(_weight_src(w_hbm, layer), w_v, sem)
            cp.start()
            acc[...] = jnp.zeros(acc.shape, F32)
            st_ref[...] = jnp.zeros((SUBLANES, D), F32)
            cp.wait()

        ub = u_ref[...]
        du = None
        for q in range(NCHIP):
            prefs = pa if q < 2 else pb
            cols = slice((q % 2) * D, (q % 2 + 1) * D)
            dv = None
            for r in prefs:
                x = r[:, cols].astype(F32)
                dv = x if dv is None else dv + x
            dv = dv.astype(BF16)
            acc[q] += _tn(ub, dv)
            part = _nt(dv, w_v[q])
            du = part if du is None else du + part

        hh = h_ref[...]
        r = lax.rsqrt(jnp.mean(hh * hh, axis=-1, keepdims=True) + EPS)
        xhat = hh * r
        gg = g_ref[...]
        st_ref[0:1, :] += _colsum(du)
        st_ref[1:2, :] += _colsum(du * (xhat * gg))
        dn = du * (1.0 + sc_ref[...])
        st_ref[2:3, :] += _colsum(dn * xhat)
        dx = dn * gg
        dhi_ref[...] = dho_ref[...] + r * (dx - xhat * jnp.mean(dx * xhat, axis=-1, keepdims=True))

        @pl.when(i == n_t - 1)
        def _():
            cp = pltpu.make_async_copy(acc, dw_hbm, sem)
            cp.start()
            cp.wait()

    part_spec = pl.BlockSpec((tm, E), lambda i: (i + off, 0))
    return pl.pallas_call(
        body, name=name, grid=(n_t,),
        in_specs=[part_spec] * (na + nb) + [pl.BlockSpec((tm, D), lambda i: (i, 0)), pl.BlockSpec((tm, D), lambda i: (i, 0)),
                                            _row(D), _row(D), ANY, pl.BlockSpec((tm, D), lambda i: (i, 0))],
        out_specs=[pl.BlockSpec((tm, D), lambda i: (i, 0)), ANY, pl.BlockSpec((SUBLANES, D), lambda i: (0, 0))],
        out_shape=[jax.ShapeDtypeStruct((S, D), F32), jax.ShapeDtypeStruct((NCHIP, D, D), F32),
                   jax.ShapeDtypeStruct((SUBLANES, D), F32)],
        scratch_shapes=[pltpu.VMEM((NCHIP, D, D), BF16), pltpu.VMEM((NCHIP, D, D), F32), pltpu.SemaphoreType.DMA],
        compiler_params=_params(1, 56),
    )(*parts_a, *parts_b, u, h, g, scale, w, dh_out)


def _cmat(c_all, w, b, name):
    L, _, n = w.shape

    def body(c_ref, w_ref, b_ref, ca_ref, o_ref):
        cc = c_ref[...]
        ca = cc * jax.nn.sigmoid(cc)
        ca_ref[...] = ca
        o_ref[...] = _nn(ca.astype(BF16), w_ref[...].astype(BF16)) + b_ref[...]

    return pl.pallas_call(
        body, name=name, grid=(L,),
        in_specs=[pl.BlockSpec((SUBLANES, D), lambda l: (0, 0)), pl.BlockSpec((None, D, n), lambda l: (l, 0, 0)),
                  pl.BlockSpec((None, 1, n), lambda l: (l, 0, 0))],
        out_specs=[pl.BlockSpec((SUBLANES, D), lambda l: (0, 0)), pl.BlockSpec((None, SUBLANES, n), lambda l: (l, 0, 0))],
        out_shape=[jax.ShapeDtypeStruct((SUBLANES, D), F32), jax.ShapeDtypeStruct((L, SUBLANES, n), F32)],
        compiler_params=_params(1, 32),
    )(c_all, w, b)


def _grad_ada(c_act_t, dmod, name):
    L, _, n = dmod.shape

    def body(c_ref, d_ref, o_ref):
        acc = None
        for b in range(SUBLANES):
            part = c_ref[:, b:b + 1] * d_ref[b:b + 1, :]
            acc = part if acc is None else acc + part
        o_ref[...] = acc

    return pl.pallas_call(
        body, name=name, grid=(L,),
        in_specs=[pl.BlockSpec((D, SUBLANES), lambda l: (0, 0)), pl.BlockSpec((None, SUBLANES, n), lambda l: (l, 0, 0))],
        out_specs=pl.BlockSpec((None, D, n), lambda l: (l, 0, 0)),
        out_shape=jax.ShapeDtypeStruct((L, D, n), F32),
        compiler_params=_params(1, 32),
    )(c_act_t, dmod)


def _stats_reduce(g3, loss_row, name):
    n_dev, rows, _ = g3.shape

    def body(g_ref, o_ref, l_ref):
        acc = g_ref[0]
        for d in range(1, n_dev):
            acc = acc + g_ref[d]
        o_ref[...] = acc
        tot = jnp.sum(o_ref[loss_row:loss_row + 1, :], axis=-1, keepdims=True)
        l_ref[...] = jnp.broadcast_to(tot * (0.5 / D), (SUBLANES, LANES))

    return pl.pallas_call(
        body, name=name,
        in_specs=[pl.BlockSpec(memory_space=pltpu.VMEM)],
        out_specs=[pl.BlockSpec(memory_space=pltpu.VMEM), pl.BlockSpec(memory_space=pltpu.VMEM)],
        out_shape=[jax.ShapeDtypeStruct((rows, D), F32), jax.ShapeDtypeStruct((SUBLANES, LANES), F32)],
        compiler_params=pltpu.CompilerParams(vmem_limit_bytes=32 * 2 ** 20),
    )(g3)


def _sum4(own, land, chip, name, tr=256):
    _, R, C = own.shape
    tr = min(tr, R)

    def body(p_ref, own_ref, land_ref, o_ref):
        o_ref[...] = ((own_ref[...] + land_ref[0]) + land_ref[1]) + land_ref[2]

    return pl.pallas_call(
        body, name=name,
        grid_spec=pltpu.PrefetchScalarGridSpec(
            num_scalar_prefetch=1, grid=(R // tr,),
            in_specs=[pl.BlockSpec((None, tr, C), lambda i, p: (p[0], i, 0)), pl.BlockSpec((3, tr, C), lambda i, p: (0, i, 0))],
            out_specs=pl.BlockSpec((tr, C), lambda i, p: (i, 0))),
        out_shape=jax.ShapeDtypeStruct((R, C), F32),
        compiler_params=_params(1, 32),
    )(chip, own, land)


def _adamw(w, m, v, g, name, tr=256):
    L, R, C = w.shape
    tr = min(tr, R)
    stacked = not isinstance(g, (list, tuple))
    n_g = None if stacked else [len(ps) for ps in g]
    flat = [g] if stacked else [a for ps in g for a in ps]

    def body(*refs):
        w_ref, m_ref, v_ref = refs[:3]
        g_refs = refs[3:3 + len(flat)]
        go_ref, d_ref, mo_ref, vo_ref = refs[3 + len(flat):]
        if stacked:
            gg = g_refs[0][...]
        else:
            layer = pl.program_id(0)
            gg = None
            k = 0
            for li in range(L):
                gl = None
                for _ in range(n_g[li]):
                    x = g_refs[k][...]
                    gl = x if gl is None else gl + x
                    k += 1
                gg = gl if gg is None else jnp.where(layer == li, gl, gg)
        m2 = ADAM_B1 * m_ref[...] + (1.0 - ADAM_B1) * gg
        v2 = ADAM_B2 * v_ref[...] + (1.0 - ADAM_B2) * (gg * gg)
        m_hat = m2 / (1.0 - ADAM_B1 ** ADAM_STEP)
        v_hat = v2 / (1.0 - ADAM_B2 ** ADAM_STEP)
        go_ref[...] = gg
        d_ref[...] = -ADAM_LR * (m_hat / (jnp.sqrt(v_hat) + ADAM_EPS) + ADAM_WD * w_ref[...])
        mo_ref[...] = m2
        vo_ref[...] = v2

    big = pl.BlockSpec((None, tr, C), lambda l, i: (l, i, 0))
    g_specs = [big] if stacked else [pl.BlockSpec((tr, C), lambda l, i: (i, 0))] * len(flat)
    return pl.pallas_call(
        body, name=name, grid=(L, R // tr),
        in_specs=[big, big, big] + g_specs,
        out_specs=[big, big, big, big],
        out_shape=[jax.ShapeDtypeStruct((L, R, C), F32)] * 4,
        compiler_params=_params(2, 40),
    )(w, m, v, *flat)


def _place():
    return lax.axis_index("x"), lax.axis_index("y"), lax.axis_index("c")


def _allgather8(xs, name):
    m, n = xs.shape

    def body(x_ref, out_ref, send_sems, recv_sems, local_sem):
        x, y, c = _place()
        me, sibling = (x, y, c), (x, y, 1 - c)
        chips = [(1 - x, y), (x, 1 - y), (1 - x, 1 - y)]

        def rows(px, py, pc):
            return out_ref.at[pl.ds((4 * px + 2 * py + pc) * m, m), :]

        def copy(k, block, to, src=None):
            return pltpu.make_async_remote_copy(
                src_ref=rows(*block) if src is None else src, dst_ref=rows(*block),
                send_sem=send_sems.at[k], recv_sem=recv_sems.at[k], device_id=to, device_id_type=MESH)

        mine = pltpu.make_async_copy(x_ref, rows(*me), local_sem)
        mine.start()
        first = [copy(0, me, sibling, src=x_ref)]
        first += [copy(1 + j, me, (*chip, c), src=x_ref) for j, chip in enumerate(chips)]
        for cp in first:
            cp.start()
        passed = [copy(4 + j, (*chip, c), sibling) for j, chip in enumerate(chips)]
        for j, chip in enumerate(chips):
            copy(1 + j, (*chip, c), me).wait_recv()
            passed[j].start()
        copy(0, sibling, me).wait_recv()
        for j, chip in enumerate(chips):
            copy(4 + j, (*chip, 1 - c), me).wait_recv()
        for cp in first + passed:
            cp.wait_send()
        mine.wait()

    return pl.pallas_call(
        body, name=name,
        out_shape=jax.ShapeDtypeStruct((8 * m, n), xs.dtype),
        in_specs=[pl.BlockSpec(memory_space=pltpu.VMEM)],
        out_specs=pl.BlockSpec(memory_space=pltpu.VMEM),
        scratch_shapes=[pltpu.SemaphoreType.DMA((7,)), pltpu.SemaphoreType.DMA((7,)), pltpu.SemaphoreType.DMA],
        compiler_params=pltpu.CompilerParams(vmem_limit_bytes=32 * 2 ** 20),
    )(xs)


def _gather_weights(shards, name):
    n = len(shards)

    def body(*refs):
        ins, outs = refs[:n], refs[n:2 * n]
        send_sems, recv_sems, local_sems = refs[2 * n:]
        x, y, c = _place()
        chips = [(1 - x, y), (x, 1 - y), (1 - x, 1 - y)]
        mine = 2 * x + y
        local, remote = [], []
        for k in range(n):
            cp = pltpu.make_async_copy(ins[k], outs[k].at[mine], local_sems.at[k])
            cp.start()
            local.append(cp)
            for j, (cx, cy) in enumerate(chips):
                cp = pltpu.make_async_remote_copy(
                    src_ref=ins[k], dst_ref=outs[k].at[mine], send_sem=send_sems.at[3 * k + j],
                    recv_sem=recv_sems.at[3 * k + j], device_id=(cx, cy, c), device_id_type=MESH)
                cp.start()
                remote.append(cp)
        for k in range(n):
            for j, (cx, cy) in enumerate(chips):
                pltpu.make_async_remote_copy(
                    src_ref=ins[k], dst_ref=outs[k].at[2 * cx + cy], send_sem=send_sems.at[3 * k + j],
                    recv_sem=recv_sems.at[3 * k + j], device_id=(cx, cy, c), device_id_type=MESH).wait_recv()
        for cp in remote:
            cp.wait_send()
        for cp in local:
            cp.wait()

    return pl.pallas_call(
        body, name=name,
        out_shape=[jax.ShapeDtypeStruct((NCHIP,) + s.shape, s.dtype) for s in shards],
        in_specs=[ANY] * n, out_specs=[ANY] * n,
        scratch_shapes=[pltpu.SemaphoreType.DMA((3 * n,)), pltpu.SemaphoreType.DMA((3 * n,)), pltpu.SemaphoreType.DMA((n,))],
    )(*shards)


def _scatter_grads(grads, name):
    n = len(grads)

    def body(*refs):
        ins, outs = refs[:n], refs[n:2 * n]
        send_sems, recv_sems = refs[2 * n:]
        x, y, c = _place()
        chips = [(1 - x, y), (x, 1 - y), (1 - x, 1 - y)]
        cps = []
        for k in range(n):
            for j, (cx, cy) in enumerate(chips):
                cp = pltpu.make_async_remote_copy(
                    src_ref=ins[k].at[2 * cx + cy], dst_ref=outs[k].at[j], send_sem=send_sems.at[3 * k + j],
                    recv_sem=recv_sems.at[3 * k + j], device_id=(cx, cy, c), device_id_type=MESH)
                cp.start()
                cps.append(cp)
        for cp in cps:
            cp.wait_recv()
        for cp in cps:
            cp.wait_send()

    return pl.pallas_call(
        body, name=name,
        out_shape=[jax.ShapeDtypeStruct((3,) + g.shape[1:], g.dtype) for g in grads],
        in_specs=[ANY] * n, out_specs=[ANY] * n,
        scratch_shapes=[pltpu.SemaphoreType.DMA((3 * n,)), pltpu.SemaphoreType.DMA((3 * n,))],
    )(*grads)


def _swap_sibling(parts, name):
    n = len(parts)

    def body(*refs):
        ins, outs = refs[:n], refs[n:2 * n]
        send_sems, recv_sems = refs[2 * n:]
        x, y, c = _place()
        cps = []
        for k in range(n):
            cp = pltpu.make_async_remote_copy(
                src_ref=ins[k], dst_ref=outs[k], send_sem=send_sems.at[k], recv_sem=recv_sems.at[k],
                device_id=(x, y, 1 - c), device_id_type=MESH)
            cp.start()
            cps.append(cp)
        for cp in cps:
            cp.wait_recv()
        for cp in cps:
            cp.wait_send()

    return pl.pallas_call(
        body, name=name,
        out_shape=[jax.ShapeDtypeStruct(a.shape, a.dtype) for a in parts],
        in_specs=[ANY] * n, out_specs=[ANY] * n,
        scratch_shapes=[pltpu.SemaphoreType.DMA((n,)), pltpu.SemaphoreType.DMA((n,))],
    )(*parts)


def _pad8(a):
    return jnp.pad(a, ((0, SUBLANES - a.shape[0]), (0, 0)))


def _local_step(h0, tgt, mods, kvmod, a_scale, norm_g, kv_norm_g, final_g, b_rel_bias, wa_in, wa_g, wa_out, w_kv, wb_in, wb_out):
    ones_e = jnp.ones((1, E), F32)
    shift = [mods[l:l + 1, 0:D] for l in range(4)]
    scale = [mods[l:l + 1, D:2 * D] for l in range(4)]
    gate = [mods[l:l + 1, 2 * D:3 * D] for l in range(4)]
    gl = [norm_g[l:l + 1] for l in range(4)]
    kv_shift, kv_scale = kvmod[None, 0:D], kvmod[None, D:2 * D]
    kv_g = kv_norm_g[None]

    hs = [h0]
    saved = []
    for l in range(2):
        u, val, z = _in_fwd(hs[-1], gl[l], shift[l], scale[l], wa_in, l, F32, F32, f"a{l}_in_fwd")
        pooled, mixed = _pool_fwd(val, wa_g[l], f"a{l}_pool_fwd")
        y, hn = _out_fwd(mixed, a_scale[l:l + 1], z, wa_out, l, gate[l], hs[-1], f"a{l}_out_fwd")
        saved.append((u, z, pooled, mixed, y))
        hs.append(hn)

    uk, k, v = _in_fwd(hs[2], kv_g, kv_shift, kv_scale, w_kv, None, BF16, BF16, "kv_in_fwd")
    kp = jnp.pad(k, ((PAD, 0), (0, 0)))
    vp = jnp.pad(v, ((PAD, 0), (0, 0)))

    strips = []
    for bi in range(2):
        l = 2 + bi
        sa, sb = _bias_build(jnp.pad(b_rel_bias[bi], ((0, 0), (0, NRELP - NREL))), f"b{bi}_bias")
        sa, sb = sa.transpose(1, 0, 2), sb.transpose(1, 0, 2)
        strips.append((sa, sb))
        u, q, z = _in_fwd(hs[-1], gl[l], shift[l], scale[l], wb_in, bi, BF16, F32, f"b{bi}_in_fwd")
        att = _attn_fwd(q, kp, vp, sa, sb, f"b{bi}_attn_fwd")
        y, hn = _out_fwd(att, ones_e, z, wb_out, bi, gate[l], hs[-1], f"b{bi}_out_fwd")
        saved.append((u, z, q, att, y))
        hs.append(hn)

    dh, st_fin = _final(hs[4], final_g[None], tgt, "final")

    st_in = [None] * 4
    st_out = [None] * 4
    g_in = [None] * 4
    g_out = [None] * 4
    dks, dvs, drb = [], [], [None, None]
    for bi in (1, 0):
        l = 2 + bi
        u, z, q, att, y = saved[l]
        sa, sb = strips[bi]
        datt, dz, g_out[l], st_out[l] = _out_bwd(dh, y, gate[l], att, ones_e, z, wb_out, bi, BF16, f"b{bi}_out_bwd")
        dq, dk, dv, dsa, dsb = _attn_bwd(q, kp, vp, sa, sb, datt, f"b{bi}_attn_bwd")
        dks.append(dk)
        dvs.append(dv)
        drb[bi] = _dbias_reduce(dsa.transpose(1, 0, 2), dsb.transpose(1, 0, 2), f"b{bi}_dbias")
        dh, g_in[l], st_in[l] = _in_bwd([dq], [dz], 0, u, hs[l], gl[l], scale[l], wb_in, bi, dh, f"b{bi}_in_bwd")

    dh, g_kv, st_kv = _in_bwd(dks, dvs, PAD, uk, hs[2], kv_g, kv_scale, w_kv, None, dh, "kv_in_bwd", tm=128)

    g_grp = [None] * 2
    st_pool = [None] * 2
    for l in (1, 0):
        u, z, pooled, mixed, y = saved[l]
        asl = a_scale[l:l + 1]
        dms, dz, g_out[l], st_out[l] = _out_bwd(dh, y, gate[l], mixed, asl, z, wa_out, l, F32, f"a{l}_out_bwd")
        dval, dwg, st_pool[l] = _pool_bwd(dms, mixed, pooled, wa_g[l], asl, f"a{l}_pool_bwd")
        g_grp[l] = dwg.reshape(4, NCHIP, GW // NCHIP, GW).transpose(1, 0, 2, 3).reshape(NCHIP, GW, GW)
        dh, g_in[l], st_in[l] = _in_bwd([dval], [dz], 0, u, hs[l], gl[l], scale[l], wa_in, l, dh, f"a{l}_in_bwd")

    pieces = st_in + [st_kv] + st_out + [st_fin]
    pieces += [_pad8(st_pool[l][0].reshape(2, D)) for l in range(2)]
    pieces += [_pad8(drb[bi].reshape(NH * NRELP // D, D)) for bi in range(2)]
    stats = jnp.concatenate(pieces, axis=0)
    grads = dict(a_in=g_in[0:2], a_grp=g_grp, a_out=g_out[0:2], kv=[g_kv], b_in=g_in[2:4], b_out=g_out[2:4])
    return dh, grads, stats


ROW_IN = [8 * l for l in range(4)]
ROW_KV = 32
ROW_OUT = [40 + 8 * l for l in range(4)]
ROW_FIN = 72
ROW_ASC = [80, 88]
ROW_RB = [96, 104]
N_STAT = 112


def kernel(x, c, ada_w, ada_b, norm_g, a_w_in, a_w_group, a_scale, a_w_out, kv_norm_g, kv_ada_w, kv_ada_b, w_kv, b_w_in, b_rel_bias, b_w_out, final_g, loss_target, m_ada_w, m_ada_b, m_norm_g, m_a_w_in, m_a_w_group, m_a_scale, m_a_w_out, m_kv_norm_g, m_kv_ada_w, m_kv_ada_b, m_w_kv, m_b_w_in, m_b_rel_bias, m_b_w_out, m_final_g, v_ada_w, v_ada_b, v_norm_g, v_a_w_in, v_a_w_group, v_a_scale, v_a_w_out, v_kv_norm_g, v_kv_ada_w, v_kv_ada_b, v_w_kv, v_b_w_in, v_b_rel_bias, v_b_w_out, v_final_g):
    xi, yi, ci = _place()
    chip = 2 * xi + yi
    dev = 4 * xi + 2 * yi + ci
    n_ada = ada_w.shape[2]
    n_kva = kv_ada_w.shape[1]
    n_asc = a_scale.shape[1]

    c_all = _allgather8(jnp.broadcast_to(c, (SUBLANES, D)), "gather_c")[::SUBLANES]
    ada_b_sh = lax.dynamic_slice_in_dim(ada_b, chip * n_ada, n_ada, axis=1)
    kvb_sh = lax.dynamic_slice_in_dim(kv_ada_b, chip * n_kva, n_kva, axis=0)
    c_act, mod_ada = _cmat(c_all, ada_w, ada_b_sh[:, None, :], "mod_ada")
    _, mod_kv = _cmat(c_all, kv_ada_w[None], kvb_sh[None, None, :], "mod_kv")
    part = jnp.concatenate([mod_ada.transpose(1, 0, 2).reshape(SUBLANES, 4 * n_ada), mod_kv[0],
                            jnp.broadcast_to(a_scale.reshape(1, 2 * n_asc), (SUBLANES, 2 * n_asc))], axis=1)
    gathered = _allgather8(part, "gather_mod")
    rows = jnp.concatenate([lax.dynamic_slice_in_dim(gathered, SUBLANES * (2 * p + ci) + dev, 1, axis=0)
                            for p in range(NCHIP)], axis=0)
    mods = jnp.stack([rows[:, l * n_ada:(l + 1) * n_ada].reshape(3 * D) for l in range(4)])
    kvmod = rows[:, 4 * n_ada:4 * n_ada + n_kva].reshape(2 * D)
    o_asc = 4 * n_ada + n_kva
    a_scale_full = jnp.stack([rows[:, o_asc + l * n_asc:o_asc + (l + 1) * n_asc].reshape(E) for l in range(2)])

    wa_in, wa_g, wa_out, wkv, wb_in, wb_out = _gather_weights(
        [a_w_in.astype(BF16), a_w_group.astype(BF16), a_w_out.astype(BF16), w_kv.astype(BF16), b_w_in.astype(BF16),
         b_w_out.astype(BF16)], "gather_weights")
    wa_g = wa_g.transpose(1, 2, 0, 3, 4).reshape(2, 4, GW, GW)

    dh, grads, stats = _local_step(x[0], loss_target[0], mods, kvmod, a_scale_full, norm_g, kv_norm_g, final_g, b_rel_bias,
                                   wa_in, wa_g, wa_out, wkv, wb_in, wb_out)
    grad_x = dh[None]

    g3 = _allgather8(stats, "gather_stats").reshape(8, N_STAT, D)
    red, loss_tile = _stats_reduce(g3, ROW_FIN + 1, "stats_reduce")
    loss = loss_tile[0, 0]

    def cat(rows_):
        return jnp.concatenate(rows_, axis=-1)

    g_ada_b = jnp.stack([cat([red[ROW_IN[l]], red[ROW_IN[l] + 1], red[ROW_OUT[l]]]) for l in range(4)])
    g_norm_g = jnp.stack([red[ROW_IN[l] + 2] for l in range(4)])
    g_kv_norm_g = red[ROW_KV + 2]
    g_kv_ada_b = cat([red[ROW_KV], red[ROW_KV + 1]])
    g_final_g = red[ROW_FIN]
    g_asc_full = jnp.stack([red[ROW_ASC[l]:ROW_ASC[l] + 2].reshape(E) for l in range(2)])
    g_a_scale = lax.dynamic_slice_in_dim(g_asc_full, chip * n_asc, n_asc, axis=1)
    g_rel = jnp.stack([red[ROW_RB[bi]:ROW_RB[bi] + NH * NRELP // D].reshape(NH, NRELP)[:, :NREL] for bi in range(2)])

    dmod = jnp.stack([cat([g3[:, ROW_IN[l]], g3[:, ROW_IN[l] + 1], g3[:, ROW_OUT[l]]]) for l in range(4)])
    dmod_sh = lax.dynamic_slice_in_dim(dmod, chip * n_ada, n_ada, axis=2)
    dkv = cat([g3[:, ROW_KV], g3[:, ROW_KV + 1]])[None]
    dkv_sh = lax.dynamic_slice_in_dim(dkv, chip * n_kva, n_kva, axis=2)
    c_act_t = c_act.T
    g_ada_w = _grad_ada(c_act_t, dmod_sh, "grad_ada_w")
    g_kv_ada_w = _grad_ada(c_act_t, dkv_sh, "grad_kv_ada_w")

    order = ["a_in", "a_grp", "a_out", "kv", "b_in", "b_out"]
    flat = [g for name in order for g in grads[name]]
    landed = _scatter_grads(flat, "scatter_grads")
    chip_arr = jnp.reshape(chip, (1,)).astype(jnp.int32)
    partial = [_sum4(g, ld, chip_arr, f"sum4_{k}") for k, (g, ld) in enumerate(zip(flat, landed))]
    other = _swap_sibling(partial, "swap_sibling")
    pairs = {}
    k = 0
    for name in order:
        pairs[name] = []
        for _ in grads[name]:
            pairs[name].append([partial[k], other[k]])
            k += 1

    def upd(w, m, v, g, name, shape3):
        g = g.reshape(shape3) if not isinstance(g, list) else g
        outs = _adamw(w.reshape(shape3), m.reshape(shape3), v.reshape(shape3), g, name)
        return [o.reshape(w.shape) for o in outs]

    res = {}
    res["ada_w"] = upd(ada_w, m_ada_w, v_ada_w, g_ada_w, "adamw_ada_w", ada_w.shape)
    res["ada_b"] = upd(ada_b, m_ada_b, v_ada_b, g_ada_b, "adamw_ada_b", (1,) + ada_b.shape)
    res["norm_g"] = upd(norm_g, m_norm_g, v_norm_g, g_norm_g, "adamw_norm_g", (1,) + norm_g.shape)
    res["a_w_in"] = upd(a_w_in, m_a_w_in, v_a_w_in, pairs["a_in"], "adamw_a_w_in", a_w_in.shape)
    res["a_w_group"] = upd(a_w_group, m_a_w_group, v_a_w_group, pairs["a_grp"], "adamw_a_w_group", (2, GW, GW))
    res["a_scale"] = upd(a_scale, m_a_scale, v_a_scale, g_a_scale, "adamw_a_scale", (1,) + a_scale.shape)
    res["a_w_out"] = upd(a_w_out, m_a_w_out, v_a_w_out, pairs["a_out"], "adamw_a_w_out", a_w_out.shape)
    res["kv_norm_g"] = upd(kv_norm_g, m_kv_norm_g, v_kv_norm_g, g_kv_norm_g, "adamw_kv_norm_g", (1, 1, D))
    res["kv_ada_w"] = upd(kv_ada_w, m_kv_ada_w, v_kv_ada_w, g_kv_ada_w, "adamw_kv_ada_w", (1,) + kv_ada_w.shape)
    res["kv_ada_b"] = upd(kv_ada_b, m_kv_ada_b, v_kv_ada_b, g_kv_ada_b, "adamw_kv_ada_b", (1, 1, 2 * D))
    res["w_kv"] = upd(w_kv, m_w_kv, v_w_kv, pairs["kv"], "adamw_w_kv", (1,) + w_kv.shape)
    res["b_w_in"] = upd(b_w_in, m_b_w_in, v_b_w_in, pairs["b_in"], "adamw_b_w_in", b_w_in.shape)
    res["b_rel_bias"] = upd(b_rel_bias, m_b_rel_bias, v_b_rel_bias, g_rel, "adamw_b_rel_bias", (1, 2 * NH, NREL))
    res["b_w_out"] = upd(b_w_out, m_b_w_out, v_b_w_out, pairs["b_out"], "adamw_b_w_out", b_w_out.shape)
    res["final_g"] = upd(final_g, m_final_g, v_final_g, g_final_g, "adamw_final_g", (1, 1, D))

    names = ["ada_w", "ada_b", "norm_g", "a_w_in", "a_w_group", "a_scale", "a_w_out", "kv_norm_g", "kv_ada_w", "kv_ada_b",
             "w_kv", "b_w_in", "b_rel_bias", "b_w_out", "final_g"]
    return (loss, grad_x, *[res[n][0] for n in names], *[res[n][1] for n in names], *[res[n][2] for n in names],
            *[res[n][3] for n in names])
```

```python
import math

import jax
import jax.numpy as jnp
from jax import lax
from jax.experimental import pallas as pl
from jax.experimental.pallas import tpu as pltpu

F32 = jnp.float32
BF16 = jnp.bfloat16

D = 1024
E = 2048
NH = 16
HD = 128
CHUNK = 64
LEFT = 8
PAD = LEFT * CHUNK
NREL = 257
NRELP = 384
REL_CLIP = 128
EPS = 1e-6
NEG = -1e30
LOG2E = math.log2(math.e)
SM_SCALE = HD ** -0.5
POOL_W = (2, 4, 8, 16)
GW = 512
HALO = 16
QC = 4
QB = QC * CHUNK
WIN = (QC + LEFT) * CHUNK
BW = (LEFT + 2) * CHUNK
NSUB = 4
NCHIP = 4
LANES = 128
SUBLANES = 8

ADAM_LR = 0.001
ADAM_B1 = 0.9
ADAM_B2 = 0.999
ADAM_EPS = 1e-08
ADAM_WD = 0.01
ADAM_STEP = 10

MESH = pl.DeviceIdType.MESH
ANY = pl.BlockSpec(memory_space=pl.ANY)


def _params(n_axes, vmem_mb):
    return pltpu.CompilerParams(dimension_semantics=("arbitrary",) * n_axes, vmem_limit_bytes=vmem_mb * 2 ** 20)


def _nn(a, b):
    return jnp.dot(a, b, preferred_element_type=F32)


def _nt(a, b):
    return lax.dot_general(a, b, (((1,), (1,)), ((), ())), preferred_element_type=F32)


def _tn(a, b):
    return lax.dot_general(a, b, (((0,), (0,)), ((), ())), preferred_element_type=F32)


def _row(n):
    return pl.BlockSpec((1, n), lambda i: (0, 0))


def _colsum(x):
    return jnp.sum(x, axis=0, keepdims=True)


def _place():
    return lax.axis_index("x"), lax.axis_index("y"), lax.axis_index("c")


class _Comm:
    def __init__(self, gathers=(), scatters=()):
        self.n_g = len(gathers)
        self.arrays = list(gathers) + list(scatters)
        self.n = len(self.arrays)
        self.out_shape = ([jax.ShapeDtypeStruct((NCHIP,) + a.shape, a.dtype) for a in gathers]
                          + [jax.ShapeDtypeStruct((3,) + a.shape[1:], a.dtype) for a in scatters])
        self.scratch = [pltpu.SemaphoreType.DMA((3 * self.n,)), pltpu.SemaphoreType.DMA((3 * self.n,)),
                        pltpu.SemaphoreType.DMA((max(self.n_g, 1),))]

    def _copies(self, ins, outs, send, recv, loc, landing):
        x, y, c = _place()
        chips = [(1 - x, y), (x, 1 - y), (1 - x, 1 - y)]
        mine = 2 * x + y
        local, remote = [], []
        for k in range(self.n):
            gather = k < self.n_g
            if gather and not landing:
                local.append(pltpu.make_async_copy(ins[k], outs[k].at[mine], loc.at[k]))
            for j, (cx, cy) in enumerate(chips):
                q = 2 * cx + cy
                src = ins[k] if gather else ins[k].at[q]
                if landing:
                    dst = outs[k].at[q] if gather else outs[k].at[j]
                else:
                    dst = outs[k].at[mine] if gather else outs[k].at[j]
                remote.append(pltpu.make_async_remote_copy(
                    src_ref=src, dst_ref=dst, send_sem=send.at[3 * k + j], recv_sem=recv.at[3 * k + j],
                    device_id=(cx, cy, c), device_id_type=MESH))
        return local, remote

    def start(self, ins, outs, send, recv, loc):
        local, sends = self._copies(ins, outs, send, recv, loc, False)
        for cp in local + sends:
            cp.start()

    def wait(self, ins, outs, send, recv, loc):
        _, lands = self._copies(ins, outs, send, recv, loc, True)
        for cp in lands:
            cp.wait_recv()
        local, sends = self._copies(ins, outs, send, recv, loc, False)
        for cp in sends:
            cp.wait_send()
        for cp in local:
            cp.wait()


def _call(body, name, grid, in_specs, out_specs, out_shape, scratch, params, args, comm=None):
    n_in, n_out, n_sc = len(in_specs), len(out_specs), len(scratch)
    if comm is None:
        outs = pl.pallas_call(body, name=name, grid=grid, in_specs=in_specs, out_specs=out_specs, out_shape=out_shape,
                              scratch_shapes=scratch, compiler_params=params)(*args)
        return list(outs), []
    n = comm.n
    o0 = n_in + n
    s0 = o0 + n_out + n

    def wrapped(*refs):
        c_refs = (refs[n_in:o0], refs[o0 + n_out:s0]) + tuple(refs[s0 + n_sc:])
        ids = [pl.program_id(a) for a in range(len(grid))]
        first = ids[0] == 0
        last = ids[0] == grid[0] - 1
        for a in range(1, len(grid)):
            first = first & (ids[a] == 0)
            last = last & (ids[a] == grid[a] - 1)

        @pl.when(first)
        def _():
            comm.start(*c_refs)

        body(*refs[:n_in], *refs[o0:o0 + n_out], *refs[s0:s0 + n_sc])

        @pl.when(last)
        def _():
            comm.wait(*c_refs)

    outs = pl.pallas_call(
        wrapped, name=name, grid=grid, in_specs=list(in_specs) + [ANY] * n, out_specs=list(out_specs) + [ANY] * n,
        out_shape=list(out_shape) + comm.out_shape, scratch_shapes=list(scratch) + comm.scratch, compiler_params=params,
    )(*args, *comm.arrays)
    return list(outs[:n_out]), list(outs[n_out:])


def _comm_only(comm, name):
    def body(*refs):
        c_refs = (refs[:comm.n], refs[comm.n:2 * comm.n]) + tuple(refs[2 * comm.n:])
        comm.start(*c_refs)
        comm.wait(*c_refs)

    return pl.pallas_call(body, name=name, in_specs=[ANY] * comm.n, out_specs=[ANY] * comm.n, out_shape=comm.out_shape,
                          scratch_shapes=comm.scratch)(*comm.arrays)


def _in_fwd(h, g, shift, scale, w, dt_a, dt_b, name, pad_rows=0, comm=None, tm=512):
    S = h.shape[0]
    n_pad = pad_rows // tm

    def body(h_ref, g_ref, sh_ref, sc_ref, w_hbm, u_ref, oa_ref, ob_ref, w_v, sem):
        i = pl.program_id(0)

        @pl.when(i == 0)
        def _():
            cp = pltpu.make_async_copy(w_hbm, w_v, sem)
            cp.start()
            cp.wait()

        hh = h_ref[...]
        r = lax.rsqrt(jnp.mean(hh * hh, axis=-1, keepdims=True) + EPS)
        u = (hh * r * g_ref[...]) * (1.0 + sc_ref[...]) + sh_ref[...]
        ub = u.astype(BF16)
        u_ref[...] = ub
        for q in range(NCHIP):
            o_ref = oa_ref if q < 2 else ob_ref
            o_ref[:, (q % 2) * D:(q % 2 + 1) * D] = _nn(ub, w_v[q]).astype(o_ref.dtype)

        if n_pad:
            @pl.when(i < n_pad)
            def _():
                oa_ref[...] = jnp.zeros(oa_ref.shape, oa_ref.dtype)
                ob_ref[...] = jnp.zeros(ob_ref.shape, ob_ref.dtype)

    def src(i):
        return (jnp.maximum(i - n_pad, 0), 0)

    outs, landed = _call(
        body, name, (S // tm + n_pad,),
        [pl.BlockSpec((tm, D), src), _row(D), _row(D), _row(D), ANY],
        [pl.BlockSpec((tm, D), src), pl.BlockSpec((tm, E), lambda i: (i, 0)), pl.BlockSpec((tm, E), lambda i: (i, 0))],
        [jax.ShapeDtypeStruct((S, D), BF16), jax.ShapeDtypeStruct((S + pad_rows, E), dt_a),
         jax.ShapeDtypeStruct((S + pad_rows, E), dt_b)],
        [pltpu.VMEM((NCHIP, D, D), BF16), pltpu.SemaphoreType.DMA],
        _params(1, 52), (h, g, shift, scale, w), comm)
    return outs, landed


def _pool_fwd(val, wg, name, comm=None, tm=512):
    S = val.shape[0]

    def body(v_ref, wg_ref, p_ref, m_ref, buf):
        i = pl.program_id(0)

        @pl.when(i == 0)
        def _():
            buf[0:HALO, :] = jnp.zeros((HALO, E), F32)

        buf[HALO:HALO + tm, :] = v_ref[...]
        t = i * tm + lax.broadcasted_iota(jnp.int32, (tm, 1), 0)
        for gi, w in enumerate(POOL_W):
            cols = slice(gi * GW, (gi + 1) * GW)
            x = buf[:, cols]
            s = x
            k = 1
            while k < w:
                s = s + pltpu.roll(s, k, 0)
                k *= 2
            cnt = jnp.minimum(t + 1, w).astype(F32)
            pb = (s[HALO:, :] / cnt - x[HALO:, :]).astype(BF16)
            p_ref[:, cols] = pb
            m_ref[:, cols] = _nn(pb, wg_ref[gi])
        buf[0:HALO, :] = buf[tm:tm + HALO, :]

    return _call(
        body, name, (S // tm,),
        [pl.BlockSpec((tm, E), lambda i: (i, 0)), pl.BlockSpec((4, GW, GW), lambda i: (0, 0, 0))],
        [pl.BlockSpec((tm, E), lambda i: (i, 0)), pl.BlockSpec((tm, E), lambda i: (i, 0))],
        [jax.ShapeDtypeStruct((S, E), BF16), jax.ShapeDtypeStruct((S, E), F32)],
        [pltpu.VMEM((tm + HALO, E), F32)],
        _params(1, 52), (val, wg), comm)


def _out_fwd(a, cs, z, w, gate, h, name, comm=None, tm=512):
    S = h.shape[0]
    kb = E // NCHIP

    def body(a_ref, cs_ref, z_ref, w_hbm, gate_ref, h_ref, y_ref, ho_ref, w_v, sem):
        @pl.when(pl.program_id(0) == 0)
        def _():
            cp = pltpu.make_async_copy(w_hbm, w_v, sem)
            cp.start()
            cp.wait()

        y = None
        for p in range(NCHIP):
            cols = slice(p * kb, (p + 1) * kb)
            zz = z_ref[:, cols]
            act = ((a_ref[:, cols] * cs_ref[:, cols]) * (zz * jax.nn.sigmoid(zz))).astype(BF16)
            part = _nn(act, w_v[p])
            y = part if y is None else y + part
        y_ref[...] = y
        ho_ref[...] = h_ref[...] + gate_ref[...] * y

    return _call(
        body, name, (S // tm,),
        [pl.BlockSpec((tm, E), lambda i: (i, 0)), _row(E), pl.BlockSpec((tm, E), lambda i: (i, 0)), ANY, _row(D),
         pl.BlockSpec((tm, D), lambda i: (i, 0))],
        [pl.BlockSpec((tm, D), lambda i: (i, 0)), pl.BlockSpec((tm, D), lambda i: (i, 0))],
        [jax.ShapeDtypeStruct((S, D), F32), jax.ShapeDtypeStruct((S, D), F32)],
        [pltpu.VMEM((NCHIP, kb, D), BF16), pltpu.SemaphoreType.DMA],
        _params(1, 52), (a, cs, z, w, gate, h), comm)


def _bias_onehot(qi, shifted, transpose):
    shape = (BW, NRELP) if transpose else (NRELP, BW)
    m = lax.broadcasted_iota(jnp.int32, shape, 0 if transpose else 1)
    r = lax.broadcasted_iota(jnp.int32, shape, 1 if transpose else 0)
    kj = m - CHUNK if shifted else m
    valid = (kj >= 0) & (kj < (LEFT + 1) * CHUNK)
    idx = jnp.clip(PAD + qi - kj, -REL_CLIP, REL_CLIP) + REL_CLIP
    return jnp.where(valid & (idx == r), 1.0, 0.0).astype(BF16)


def _bias_build(rb, name):
    def body(rb_ref, a_ref, b_ref):
        qi = pl.program_id(0)
        x = rb_ref[...]
        hi = x.astype(BF16)
        r1 = x - hi.astype(F32)
        mid = r1.astype(BF16)
        lo = (r1 - mid.astype(F32)).astype(BF16)
        for shifted, o_ref in ((False, a_ref), (True, b_ref)):
            oh = _bias_onehot(qi, shifted, False)
            m = lax.broadcasted_iota(jnp.int32, (NH, BW), 1)
            kj = m - CHUNK if shifted else m
            valid = (kj >= 0) & (kj < (LEFT + 1) * CHUNK)
            val = (_nn(hi, oh) + _nn(mid, oh)) + _nn(lo, oh)
            o_ref[...] = jnp.where(valid, val, NEG)

    return pl.pallas_call(
        body, name=name, grid=(CHUNK,),
        in_specs=[pl.BlockSpec((NH, NRELP), lambda i: (0, 0))],
        out_specs=[pl.BlockSpec((None, NH, BW), lambda i: (i, 0, 0)), pl.BlockSpec((None, NH, BW), lambda i: (i, 0, 0))],
        out_shape=[jax.ShapeDtypeStruct((CHUNK, NH, BW), F32), jax.ShapeDtypeStruct((CHUNK, NH, BW), F32)],
        compiler_params=_params(1, 32),
    )(rb)


def _dbias_reduce(dba, dbb, name):
    def body(a_ref, b_ref, o_ref):
        qi = pl.program_id(0)

        @pl.when(qi == 0)
        def _():
            o_ref[...] = jnp.zeros((NH, NRELP), F32)

        acc = None
        for shifted, i_ref in ((False, a_ref), (True, b_ref)):
            oh = _bias_onehot(qi, shifted, True)
            x = i_ref[...]
            hi = x.astype(BF16)
            mid = (x - hi.astype(F32)).astype(BF16)
            part = _nn(hi, oh) + _nn(mid, oh)
            acc = part if acc is None else acc + part
        o_ref[...] += acc

    return pl.pallas_call(
        body, name=name, grid=(CHUNK,),
        in_specs=[pl.BlockSpec((None, NH, BW), lambda i: (i, 0, 0)), pl.BlockSpec((None, NH, BW), lambda i: (i, 0, 0))],
        out_specs=pl.BlockSpec((NH, NRELP), lambda i: (0, 0)),
        out_shape=jax.ShapeDtypeStruct((NH, NRELP), F32),
        compiler_params=_params(1, 32),
    )(dba, dbb)


def _build_bias(bias3, ba_ref, bb_ref):
    bias3[2] = jnp.full((QB, WIN), NEG, F32)
    for qc in range(QC):
        rows = slice(qc * CHUNK, (qc + 1) * CHUNK)
        if qc % 2 == 0:
            bias3[2, rows, qc * CHUNK:qc * CHUNK + BW] = ba_ref[...] * LOG2E
        else:
            bias3[2, rows, (qc - 1) * CHUNK:(qc - 1) * CHUNK + BW] = bb_ref[...] * LOG2E
    col = lax.broadcasted_iota(jnp.int32, (QB, WIN), 1)
    for sub in range(PAD // QB):
        bias3[sub] = jnp.where(col < PAD - sub * QB, NEG, bias3[2])


def _softmax_parts(q_ref, k_ref, bias3, i, sub, row0):
    qq = q_ref[sub * QB:(sub + 1) * QB, :]
    kw = k_ref[pl.ds(row0, WIN), :]
    which = jnp.where(i == 0, sub, 2) if sub < PAD // QB else 2
    s = _nt(qq, kw) * (SM_SCALE * LOG2E) + bias3[which]
    e = jnp.exp2(s - jnp.max(s, axis=-1, keepdims=True))
    return qq, kw, e, jnp.sum(e, axis=-1, keepdims=True)


def _attn_fwd(q, kp, vp, ba, bb, name, comm=None):
    S = q.shape[0]
    R = NSUB * QB
    assert S % R == 0 and NSUB >= PAD // QB

    def body(q_ref, k_ref, v_ref, ba_ref, bb_ref, o_ref, bias3):
        i = pl.program_id(1)

        @pl.when(i == 0)
        def _():
            _build_bias(bias3, ba_ref, bb_ref)

        for sub in range(NSUB):
            row0 = pl.multiple_of((i * NSUB + sub) * QB, QB)
            _, _, e, l = _softmax_parts(q_ref, k_ref, bias3, i, sub, row0)
            o_ref[sub * QB:(sub + 1) * QB, :] = _nn(e.astype(BF16), v_ref[pl.ds(row0, WIN), :]) / l

    return _call(
        body, name, (NH, S // R),
        [pl.BlockSpec((R, HD), lambda h, i: (i, h)), pl.BlockSpec((S + PAD, HD), lambda h, i: (0, h)),
         pl.BlockSpec((S + PAD, HD), lambda h, i: (0, h)), pl.BlockSpec((None, CHUNK, BW), lambda h, i: (h, 0, 0)),
         pl.BlockSpec((None, CHUNK, BW), lambda h, i: (h, 0, 0))],
        [pl.BlockSpec((R, HD), lambda h, i: (i, h))],
        [jax.ShapeDtypeStruct((S, E), F32)],
        [pltpu.VMEM((3, QB, WIN), F32)],
        _params(2, 48), (q, kp, vp, ba, bb), comm)


def _final(h, g, tgt, name, tm=512):
    S = h.shape[0]

    def body(h_ref, g_ref, t_ref, dh_ref, st_ref):
        @pl.when(pl.program_id(0) == 0)
        def _():
            st_ref[...] = jnp.zeros((SUBLANES, D), F32)

        hh = h_ref[...]
        r = lax.rsqrt(jnp.mean(hh * hh, axis=-1, keepdims=True) + EPS)
        xhat = hh * r
        diff = xhat * g_ref[...] - t_ref[...]
        st_ref[1:2, :] += _colsum(diff * diff)
        dout = diff * (1.0 / D)
        st_ref[0:1, :] += _colsum(dout * xhat)
        dx = dout * g_ref[...]
        dh_ref[...] = r * (dx - xhat * jnp.mean(dx * xhat, axis=-1, keepdims=True))

    return pl.pallas_call(
        body, name=name, grid=(S // tm,),
        in_specs=[pl.BlockSpec((tm, D), lambda i: (i, 0)), _row(D), pl.BlockSpec((tm, D), lambda i: (i, 0))],
        out_specs=[pl.BlockSpec((tm, D), lambda i: (i, 0)), pl.BlockSpec((SUBLANES, D), lambda i: (0, 0))],
        out_shape=[jax.ShapeDtypeStruct((S, D), F32), jax.ShapeDtypeStruct((SUBLANES, D), F32)],
        compiler_params=_params(1, 32),
    )(h, g, tgt)


def _store_grad(acc, stage, dw_hbm, sem):
    for q in range(NCHIP):
        stage[...] = acc[q].astype(BF16)
        cp = pltpu.make_async_copy(stage, dw_hbm.at[q], sem)
        cp.start()
        cp.wait()


def _out_bwd(dh, y, gate, a, cs, z, w, dt_da, name, comm=None, tm=256):
    S = dh.shape[0]
    kb = E // NCHIP
    n_t = S // tm

    def body(dh_ref, y_ref, gate_ref, a_ref, cs_ref, z_ref, w_hbm, da_ref, dz_ref, dw_hbm, st_ref, w_v, acc, stage, sem):
        i = pl.program_id(0)

        @pl.when(i == 0)
        def _():
            cp = pltpu.make_async_copy(w_hbm, w_v, sem)
            cp.start()
            acc[...] = jnp.zeros(acc.shape, F32)
            st_ref[...] = jnp.zeros((SUBLANES, D), F32)
            cp.wait()

        dhh = dh_ref[...]
        st_ref[0:1, :] += _colsum(dhh * y_ref[...])
        dy = (dhh * gate_ref[...]).astype(BF16)
        for p in range(NCHIP):
            cols = slice(p * kb, (p + 1) * kb)
            zz = z_ref[:, cols]
            sig = jax.nn.sigmoid(zz)
            sz = zz * sig
            ae = a_ref[:, cols] * cs_ref[:, cols]
            acc[p] += _tn((ae * sz).astype(BF16), dy)
            dact = _nt(dy, w_v[p])
            da_ref[:, cols] = (dact * sz).astype(da_ref.dtype)
            dz_ref[:, cols] = (dact * ae * (sig * (1.0 + zz * (1.0 - sig)))).astype(BF16)

        @pl.when(i == n_t - 1)
        def _():
            _store_grad(acc, stage, dw_hbm, sem)

    return _call(
        body, name, (n_t,),
        [pl.BlockSpec((tm, D), lambda i: (i, 0)), pl.BlockSpec((tm, D), lambda i: (i, 0)), _row(D),
         pl.BlockSpec((tm, E), lambda i: (i, 0)), _row(E), pl.BlockSpec((tm, E), lambda i: (i, 0)), ANY],
        [pl.BlockSpec((tm, E), lambda i: (i, 0)), pl.BlockSpec((tm, E), lambda i: (i, 0)), ANY,
         pl.BlockSpec((SUBLANES, D), lambda i: (0, 0))],
        [jax.ShapeDtypeStruct((S, E), dt_da), jax.ShapeDtypeStruct((S, E), BF16),
         jax.ShapeDtypeStruct((NCHIP, kb, D), BF16), jax.ShapeDtypeStruct((SUBLANES, D), F32)],
        [pltpu.VMEM((NCHIP, kb, D), BF16), pltpu.VMEM((NCHIP, kb, D), F32), pltpu.VMEM((kb, D), BF16),
         pltpu.SemaphoreType.DMA],
        _params(1, 52), (dh, y, gate, a, cs, z, w), comm)


def _attn_bwd(q, kp, vp, ba, bb, do, prev, name, comm=None):
    S = q.shape[0]
    R = NSUB * QB
    n_i = S // R
    dt_kv = F32 if prev is None else BF16

    def body(*refs):
        q_ref, k_ref, v_ref, ba_ref, bb_ref, do_ref = refs[:6]
        refs = refs[6:]
        if prev is not None:
            pk_hbm, pv_hbm = refs[:2]
            refs = refs[2:]
        dq_ref, dk_ref, dv_ref, dba_ref, dbb_ref, bias3, dbias, dk_acc, dv_acc = refs[:9]
        if prev is not None:
            pk_v, pv_v, sems = refs[9:]
        h = pl.program_id(0)
        i = pl.program_id(1)

        def prev_copies():
            cols = pl.ds(pl.multiple_of(h * HD, HD), HD)
            return (pltpu.make_async_copy(pk_hbm.at[:, cols], pk_v, sems.at[0]),
                    pltpu.make_async_copy(pv_hbm.at[:, cols], pv_v, sems.at[1]))

        @pl.when(i == 0)
        def _():
            if prev is not None:
                for cp in prev_copies():
                    cp.start()
            _build_bias(bias3, ba_ref, bb_ref)
            dbias[...] = jnp.zeros((QB, WIN), F32)
            dk_acc[...] = jnp.zeros((S + PAD, HD), F32)
            dv_acc[...] = jnp.zeros((S + PAD, HD), F32)

        for sub in range(NSUB):
            rows = slice(sub * QB, (sub + 1) * QB)
            row0 = pl.multiple_of((i * NSUB + sub) * QB, QB)
            qq, kw, e, l = _softmax_parts(q_ref, k_ref, bias3, i, sub, row0)
            p = e * (1.0 / l)
            dd = do_ref[rows, :]
            dp = _nt(dd, v_ref[pl.ds(row0, WIN), :])
            ds = p * (dp - jnp.sum(p * dp, axis=-1, keepdims=True))
            dbias[...] += ds
            dsb = (ds * SM_SCALE).astype(BF16)
            dq_ref[rows, :] = _nn(dsb, kw).astype(BF16)
            dk_acc[pl.ds(row0, WIN), :] += _tn(dsb, qq)
            dv_acc[pl.ds(row0, WIN), :] += _tn(p.astype(BF16), dd)

        @pl.when(i == n_i - 1)
        def _():
            da = None
            db = None
            for qc in range(QC):
                rows = slice(qc * CHUNK, (qc + 1) * CHUNK)
                if qc % 2 == 0:
                    part = dbias[rows, qc * CHUNK:qc * CHUNK + BW]
                    da = part if da is None else da + part
                else:
                    part = dbias[rows, (qc - 1) * CHUNK:(qc - 1) * CHUNK + BW]
                    db = part if db is None else db + part
            dba_ref[...] = da
            dbb_ref[...] = db
            if prev is None:
                dk_ref[...] = dk_acc[...]
                dv_ref[...] = dv_acc[...]
            else:
                for cp in prev_copies():
                    cp.wait()
                dk_ref[...] = (dk_acc[...] + pk_v[...]).astype(BF16)
                dv_ref[...] = (dv_acc[...] + pv_v[...]).astype(BF16)

    head = pl.BlockSpec((S + PAD, HD), lambda h, i: (0, h))
    strip = pl.BlockSpec((None, CHUNK, BW), lambda h, i: (h, 0, 0))
    blk = pl.BlockSpec((R, HD), lambda h, i: (i, h))
    in_specs = [blk, head, head, strip, strip, blk]
    scratch = [pltpu.VMEM((3, QB, WIN), F32), pltpu.VMEM((QB, WIN), F32), pltpu.VMEM((S + PAD, HD), F32),
               pltpu.VMEM((S + PAD, HD), F32)]
    args = (q, kp, vp, ba, bb, do)
    if prev is not None:
        in_specs += [ANY, ANY]
        scratch += [pltpu.VMEM((S + PAD, HD), F32), pltpu.VMEM((S + PAD, HD), F32), pltpu.SemaphoreType.DMA((2,))]
        args += tuple(prev)
    return _call(
        body, name, (NH, n_i), in_specs, [blk, head, head, strip, strip],
        [jax.ShapeDtypeStruct((S, E), BF16), jax.ShapeDtypeStruct((S + PAD, E), dt_kv),
         jax.ShapeDtypeStruct((S + PAD, E), dt_kv), jax.ShapeDtypeStruct((NH, CHUNK, BW), F32),
         jax.ShapeDtypeStruct((NH, CHUNK, BW), F32)],
        scratch, _params(2, 56), args, comm)


def _pool_bwd(dms, mixed, pooled, wg, a_scale, name, comm=None, tm=512):
    S = dms.shape[0]
    n_t = S // tm

    def rev(i):
        return (n_t - 1 - i, 0)

    def body(d_ref, m_ref, p_ref, wg_ref, as_ref, dv_ref, dwg_ref, st_ref, buf):
        i = pl.program_id(0)

        @pl.when(i == 0)
        def _():
            buf[tm:tm + HALO, :] = jnp.zeros((HALO, E), F32)
            dwg_ref[...] = jnp.zeros((4, GW, GW), F32)
            st_ref[...] = jnp.zeros((SUBLANES, E), F32)

        t = (n_t - 1 - i) * tm + lax.broadcasted_iota(jnp.int32, (tm, 1), 0)
        st_ref[0:1, :] += _colsum(d_ref[...] * m_ref[...])
        for gi, w in enumerate(POOL_W):
            cols = slice(gi * GW, (gi + 1) * GW)
            dm = (d_ref[:, cols] * as_ref[:, cols]).astype(BF16)
            dpool = _nt(dm, wg_ref[gi])
            dwg_ref[gi] += _tn(p_ref[:, cols], dm)
            cnt = jnp.minimum(t + 1, w).astype(F32)
            buf[0:tm, cols] = dpool / cnt
            s = buf[:, cols]
            k = 1
            while k < w:
                s = s + pltpu.roll(s, tm + HALO - k, 0)
                k *= 2
            dv_ref[:, cols] = (s[0:tm, :] - dpool).astype(BF16)
        buf[tm:tm + HALO, :] = buf[0:HALO, :]

    return _call(
        body, name, (n_t,),
        [pl.BlockSpec((tm, E), rev), pl.BlockSpec((tm, E), rev), pl.BlockSpec((tm, E), rev),
         pl.BlockSpec((4, GW, GW), lambda i: (0, 0, 0)), _row(E)],
        [pl.BlockSpec((tm, E), rev), pl.BlockSpec((4, GW, GW), lambda i: (0, 0, 0)),
         pl.BlockSpec((SUBLANES, E), lambda i: (0, 0))],
        [jax.ShapeDtypeStruct((S, E), BF16), jax.ShapeDtypeStruct((4, GW, GW), F32),
         jax.ShapeDtypeStruct((SUBLANES, E), F32)],
        [pltpu.VMEM((tm + HALO, E), F32)],
        _params(1, 52), (dms, mixed, pooled, wg, a_scale), comm)


def _in_bwd(da, db, row_off, u, h, g, scale, w, dh_out, name, comm=None, tm=256):
    S = h.shape[0]
    n_t = S // tm
    off = row_off // tm

    def body(da_ref, db_ref, u_ref, h_ref, g_ref, sc_ref, w_hbm, dho_ref, dhi_ref, dw_hbm, st_ref, w_v, acc, stage, sem):
        i = pl.program_id(0)

        @pl.when(i == 0)
        def _():
            cp = pltpu.make_async_copy(w_hbm, w_v, sem)
            cp.start()
            acc[...] = jnp.zeros(acc.shape, F32)
            st_ref[...] = jnp.zeros((SUBLANES, D), F32)
            cp.wait()

        ub = u_ref[...]
        du = None
        for q in range(NCHIP):
            d_ref = da_ref if q < 2 else db_ref
            dv = d_ref[:, (q % 2) * D:(q % 2 + 1) * D]
            acc[q] += _tn(ub, dv)
            part = _nt(dv, w_v[q])
            du = part if du is None else du + part

        hh = h_ref[...]
        r = lax.rsqrt(jnp.mean(hh * hh, axis=-1, keepdims=True) + EPS)
        xhat = hh * r
        gg = g_ref[...]
        st_ref[0:1, :] += _colsum(du)
        st_ref[1:2, :] += _colsum(du * (xhat * gg))
        dn = du * (1.0 + sc_ref[...])
        st_ref[2:3, :] += _colsum(dn * xhat)
        dx = dn * gg
        dhi_ref[...] = dho_ref[...] + r * (dx - xhat * jnp.mean(dx * xhat, axis=-1, keepdims=True))

        @pl.when(i == n_t - 1)
        def _():
            _store_grad(acc, stage, dw_hbm, sem)

    part_spec = pl.BlockSpec((tm, E), lambda i: (i + off, 0))
    return _call(
        body, name, (n_t,),
        [part_spec, part_spec, pl.BlockSpec((tm, D), lambda i: (i, 0)), pl.BlockSpec((tm, D), lambda i: (i, 0)),
         _row(D), _row(D), ANY, pl.BlockSpec((tm, D), lambda i: (i, 0))],
        [pl.BlockSpec((tm, D), lambda i: (i, 0)), ANY, pl.BlockSpec((SUBLANES, D), lambda i: (0, 0))],
        [jax.ShapeDtypeStruct((S, D), F32), jax.ShapeDtypeStruct((NCHIP, D, D), BF16),
         jax.ShapeDtypeStruct((SUBLANES, D), F32)],
        [pltpu.VMEM((NCHIP, D, D), BF16), pltpu.VMEM((NCHIP, D, D), F32), pltpu.VMEM((D, D), BF16),
         pltpu.SemaphoreType.DMA],
        _params(1, 56), (da, db, u, h, g, scale, w, dh_out), comm)


def _cmat(c_all, w, b, name):
    L, _, n = w.shape

    def body(c_ref, w_ref, b_ref, ca_ref, o_ref):
        cc = c_ref[...]
        ca = cc * jax.nn.sigmoid(cc)
        ca_ref[...] = ca
        o_ref[...] = _nn(ca.astype(BF16), w_ref[...].astype(BF16)) + b_ref[...]

    return pl.pallas_call(
        body, name=name, grid=(L,),
        in_specs=[pl.BlockSpec((SUBLANES, D), lambda l: (0, 0)), pl.BlockSpec((None, D, n), lambda l: (l, 0, 0)),
                  pl.BlockSpec((None, 1, n), lambda l: (l, 0, 0))],
        out_specs=[pl.BlockSpec((SUBLANES, D), lambda l: (0, 0)), pl.BlockSpec((None, SUBLANES, n), lambda l: (l, 0, 0))],
        out_shape=[jax.ShapeDtypeStruct((SUBLANES, D), F32), jax.ShapeDtypeStruct((L, SUBLANES, n), F32)],
        compiler_params=_params(1, 32),
    )(c_all, w, b)


def _grad_ada(c_act_t, dmod, name):
    L, _, n = dmod.shape

    def body(c_ref, d_ref, o_ref):
        acc = None
        for b in range(SUBLANES):
            part = c_ref[:, b:b + 1] * d_ref[b:b + 1, :]
            acc = part if acc is None else acc + part
        o_ref[...] = acc

    return pl.pallas_call(
        body, name=name, grid=(L,),
        in_specs=[pl.BlockSpec((D, SUBLANES), lambda l: (0, 0)), pl.BlockSpec((None, SUBLANES, n), lambda l: (l, 0, 0))],
        out_specs=pl.BlockSpec((None, D, n), lambda l: (l, 0, 0)),
        out_shape=jax.ShapeDtypeStruct((L, D, n), F32),
        compiler_params=_params(1, 32),
    )(c_act_t, dmod)


def _stats_reduce(g3, loss_row, name):
    n_dev, rows, _ = g3.shape

    def body(g_ref, o_ref, l_ref):
        acc = g_ref[0]
        for d in range(1, n_dev):
            acc = acc + g_ref[d]
        o_ref[...] = acc
        tot = jnp.sum(o_ref[loss_row:loss_row + 1, :], axis=-1, keepdims=True)
        l_ref[...] = jnp.broadcast_to(tot * (0.5 / D), (SUBLANES, LANES))

    return pl.pallas_call(
        body, name=name,
        in_specs=[pl.BlockSpec(memory_space=pltpu.VMEM)],
        out_specs=[pl.BlockSpec(memory_space=pltpu.VMEM), pl.BlockSpec(memory_space=pltpu.VMEM)],
        out_shape=[jax.ShapeDtypeStruct((rows, D), F32), jax.ShapeDtypeStruct((SUBLANES, LANES), F32)],
        compiler_params=pltpu.CompilerParams(vmem_limit_bytes=32 * 2 ** 20),
    )(g3)


def _sum4(own, land, chip, name, tr=256):
    _, R, C = own.shape
    tr = min(tr, R)

    def body(p_ref, own_ref, land_ref, o_ref):
        o_ref[...] = ((own_ref[...].astype(F32) + land_ref[0].astype(F32)) + land_ref[1].astype(F32)) + land_ref[2].astype(F32)

    return pl.pallas_call(
        body, name=name,
        grid_spec=pltpu.PrefetchScalarGridSpec(
            num_scalar_prefetch=1, grid=(R // tr,),
            in_specs=[pl.BlockSpec((None, tr, C), lambda i, p: (p[0], i, 0)), pl.BlockSpec((3, tr, C), lambda i, p: (0, i, 0))],
            out_specs=pl.BlockSpec((tr, C), lambda i, p: (i, 0))),
        out_shape=jax.ShapeDtypeStruct((R, C), F32),
        compiler_params=_params(1, 32),
    )(chip, own, land)


def _adamw(w, m, v, g, name, tr=256):
    L, R, C = w.shape
    tr = min(tr, R)
    stacked = not isinstance(g, (list, tuple))
    n_g = None if stacked else [len(ps) for ps in g]
    flat = [g] if stacked else [a for ps in g for a in ps]

    def body(*refs):
        w_ref, m_ref, v_ref = refs[:3]
        g_refs = refs[3:3 + len(flat)]
        go_ref, d_ref, mo_ref, vo_ref = refs[3 + len(flat):]
        if stacked:
            gg = g_refs[0][...]
        else:
            layer = pl.program_id(0)
            gg = None
            k = 0
            for li in range(L):
                gl = None
                for _ in range(n_g[li]):
                    x = g_refs[k][...]
                    gl = x if gl is None else gl + x
                    k += 1
                gg = gl if gg is None else jnp.where(layer == li, gl, gg)
        m2 = ADAM_B1 * m_ref[...] + (1.0 - ADAM_B1) * gg
        v2 = ADAM_B2 * v_ref[...] + (1.0 - ADAM_B2) * (gg * gg)
        m_hat = m2 / (1.0 - ADAM_B1 ** ADAM_STEP)
        v_hat = v2 / (1.0 - ADAM_B2 ** ADAM_STEP)
        go_ref[...] = gg
        d_ref[...] = -ADAM_LR * (m_hat / (jnp.sqrt(v_hat) + ADAM_EPS) + ADAM_WD * w_ref[...])
        mo_ref[...] = m2
        vo_ref[...] = v2

    big = pl.BlockSpec((None, tr, C), lambda l, i: (l, i, 0))
    g_specs = [big] if stacked else [pl.BlockSpec((tr, C), lambda l, i: (i, 0))] * len(flat)
    return pl.pallas_call(
        body, name=name, grid=(L, R // tr),
        in_specs=[big, big, big] + g_specs,
        out_specs=[big, big, big, big],
        out_shape=[jax.ShapeDtypeStruct((L, R, C), F32)] * 4,
        compiler_params=_params(2, 40),
    )(w, m, v, *flat)


def _allgather8(xs, name):
    m, n = xs.shape

    def body(x_ref, out_ref, send_sems, recv_sems, local_sem):
        x, y, c = _place()
        me, sibling = (x, y, c), (x, y, 1 - c)
        chips = [(1 - x, y), (x, 1 - y), (1 - x, 1 - y)]

        def rows(px, py, pc):
            return out_ref.at[pl.ds((4 * px + 2 * py + pc) * m, m), :]

        def copy(k, block, to, src=None):
            return pltpu.make_async_remote_copy(
                src_ref=rows(*block) if src is None else src, dst_ref=rows(*block),
                send_sem=send_sems.at[k], recv_sem=recv_sems.at[k], device_id=to, device_id_type=MESH)

        mine = pltpu.make_async_copy(x_ref, rows(*me), local_sem)
        mine.start()
        first = [copy(0, me, sibling, src=x_ref)]
        first += [copy(1 + j, me, (*chip, c), src=x_ref) for j, chip in enumerate(chips)]
        for cp in first:
            cp.start()
        passed = [copy(4 + j, (*chip, c), sibling) for j, chip in enumerate(chips)]
        for j, chip in enumerate(chips):
            copy(1 + j, (*chip, c), me).wait_recv()
            passed[j].start()
        copy(0, sibling, me).wait_recv()
        for j, chip in enumerate(chips):
            copy(4 + j, (*chip, 1 - c), me).wait_recv()
        for cp in first + passed:
            cp.wait_send()
        mine.wait()

    return pl.pallas_call(
        body, name=name,
        out_shape=jax.ShapeDtypeStruct((8 * m, n), xs.dtype),
        in_specs=[pl.BlockSpec(memory_space=pltpu.VMEM)],
        out_specs=pl.BlockSpec(memory_space=pltpu.VMEM),
        scratch_shapes=[pltpu.SemaphoreType.DMA((7,)), pltpu.SemaphoreType.DMA((7,)), pltpu.SemaphoreType.DMA],
        compiler_params=pltpu.CompilerParams(vmem_limit_bytes=32 * 2 ** 20),
    )(xs)


def _swap_sibling(parts, name):
    n = len(parts)

    def body(*refs):
        ins, outs = refs[:n], refs[n:2 * n]
        send_sems, recv_sems = refs[2 * n:]
        x, y, c = _place()
        cps = []
        for k in range(n):
            cp = pltpu.make_async_remote_copy(
                src_ref=ins[k], dst_ref=outs[k], send_sem=send_sems.at[k], recv_sem=recv_sems.at[k],
                device_id=(x, y, 1 - c), device_id_type=MESH)
            cp.start()
            cps.append(cp)
        for cp in cps:
            cp.wait_recv()
        for cp in cps:
            cp.wait_send()

    return pl.pallas_call(
        body, name=name,
        out_shape=[jax.ShapeDtypeStruct(a.shape, a.dtype) for a in parts],
        in_specs=[ANY] * n, out_specs=[ANY] * n,
        scratch_shapes=[pltpu.SemaphoreType.DMA((n,)), pltpu.SemaphoreType.DMA((n,))],
    )(*parts)


def _pad8(a):
    return jnp.pad(a, ((0, SUBLANES - a.shape[0]), (0, 0)))


def _group_rows(wg):
    return wg.transpose(1, 0, 2, 3).reshape(4, GW, GW)


def _example_step(h0, tgt, mods, kvmod, a_scale, norm_g, kv_norm_g, final_g, b_rel_bias, sh):
    ones_e = jnp.ones((1, E), F32)
    shift = [mods[l:l + 1, 0:D] for l in range(4)]
    scale = [mods[l:l + 1, D:2 * D] for l in range(4)]
    gate = [mods[l:l + 1, 2 * D:3 * D] for l in range(4)]
    gl = [norm_g[l:l + 1] for l in range(4)]
    kv_shift, kv_scale = kvmod[None, 0:D], kvmod[None, D:2 * D]
    kv_g = kv_norm_g[None]

    (wa_in0,) = _comm_only(_Comm(gathers=[sh["a_in"][0]]), "gather_first")
    hs = [h0]
    saved = []
    wa_in = [wa_in0, None]
    nxt_a = [dict(i=[sh["a_grp"][0], sh["a_out"][0]], p=[sh["a_in"][1]], o=[sh["a_grp"][1], sh["a_out"][1]]),
             dict(i=[sh["kv"][0]], p=[sh["b_in"][0]], o=[sh["b_out"][0]])]
    wg_l, wo_l = None, None
    for l in range(2):
        (u, val, z), got_i = _in_fwd(hs[-1], gl[l], shift[l], scale[l], wa_in[l], F32, F32, f"a{l}_in_fwd",
                                     comm=_Comm(gathers=nxt_a[l]["i"]))
        if l == 0:
            wg_l, wo_l = got_i
        else:
            (w_kv,) = got_i
        wg_full = _group_rows(wg_l)
        (pooled, mixed), got_p = _pool_fwd(val, wg_full, f"a{l}_pool_fwd", comm=_Comm(gathers=nxt_a[l]["p"]))
        (y, hn), got_o = _out_fwd(mixed, a_scale[l:l + 1], z, wo_l, gate[l], hs[-1], f"a{l}_out_fwd",
                                  comm=_Comm(gathers=nxt_a[l]["o"]))
        saved.append((u, z, pooled, mixed, y, wa_in[l], wg_full, wo_l))
        hs.append(hn)
        if l == 0:
            (wa_in[1],) = got_p
            wg_l, wo_l = got_o
        else:
            (wb_in0,) = got_p
            (wb_out0,) = got_o

    (uk, kp, vp), (wb_in1,) = _in_fwd(hs[2], kv_g, kv_shift, kv_scale, w_kv, BF16, BF16, "kv_in_fwd", pad_rows=PAD,
                                      comm=_Comm(gathers=[sh["b_in"][1]]))
    wb_in = [wb_in0, wb_in1]
    wb_out = [wb_out0, None]

    strips = []
    for bi in range(2):
        l = 2 + bi
        sa, sb = _bias_build(jnp.pad(b_rel_bias[bi], ((0, 0), (0, NRELP - NREL))), f"b{bi}_bias")
        sa, sb = sa.transpose(1, 0, 2), sb.transpose(1, 0, 2)
        strips.append((sa, sb))
        (u, q, z), got = _in_fwd(hs[-1], gl[l], shift[l], scale[l], wb_in[bi], BF16, F32, f"b{bi}_in_fwd",
                                 comm=_Comm(gathers=[sh["b_out"][1]]) if bi == 0 else None)
        if bi == 0:
            (wb_out[1],) = got
        (att,), _ = _attn_fwd(q, kp, vp, sa, sb, f"b{bi}_attn_fwd")
        (y, hn), _ = _out_fwd(att, ones_e, z, wb_out[bi], gate[l], hs[-1], f"b{bi}_out_fwd")
        saved.append((u, z, q, att, y))
        hs.append(hn)

    dh, st_fin = _final(hs[4], final_g[None], tgt, "final")

    st_in = [None] * 4
    st_out = [None] * 4
    grads = {}
    landed = {}

    def carry(names):
        return _Comm(scatters=[grads[n] for n in names]) if names else None

    def land(names, got):
        for n, a in zip(names, got):
            landed[n] = a

    u, z, q, att, y = saved[3]
    sa, sb = strips[1]
    (datt, dz, grads["b_out1"], st_out[3]), _ = _out_bwd(dh, y, gate[3], att, ones_e, z, wb_out[1], BF16, "b1_out_bwd")
    (dq, dk1, dv1, dsa, dsb), got = _attn_bwd(q, kp, vp, sa, sb, datt, None, "b1_attn_bwd", comm=carry(["b_out1"]))
    land(["b_out1"], got)
    drb1 = _dbias_reduce(dsa.transpose(1, 0, 2), dsb.transpose(1, 0, 2), "b1_dbias")
    (dh, grads["b_in1"], st_in[3]), _ = _in_bwd(dq, dz, 0, u, hs[3], gl[3], scale[3], wb_in[1], dh, "b1_in_bwd")
    u, z, q, att, y = saved[2]
    sa, sb = strips[0]
    (datt, dz, grads["b_out0"], st_out[2]), _ = _out_bwd(dh, y, gate[2], att, ones_e, z, wb_out[0], BF16, "b0_out_bwd")
    (dq, dk, dv, dsa, dsb), got = _attn_bwd(q, kp, vp, sa, sb, datt, (dk1, dv1), "b0_attn_bwd",
                                            comm=carry(["b_in1", "b_out0"]))
    land(["b_in1", "b_out0"], got)
    drb0 = _dbias_reduce(dsa.transpose(1, 0, 2), dsb.transpose(1, 0, 2), "b0_dbias")
    (dh, grads["b_in0"], st_in[2]), _ = _in_bwd(dq, dz, 0, u, hs[2], gl[2], scale[2], wb_in[0], dh, "b0_in_bwd")
    (dh, grads["kv"], st_kv), got = _in_bwd(dk, dv, PAD, uk, hs[2], kv_g, kv_scale, w_kv, dh, "kv_in_bwd",
                                            comm=carry(["b_in0"]))
    land(["b_in0"], got)
    st_pool = [None] * 2
    plan = {1: dict(o=[], p=["a_out1"], i=["kv", "a_grp1"]), 0: dict(o=["a_in1"], p=["a_out0"], i=["a_grp0"])}
    for l in (1, 0):
        u, z, pooled, mixed, y, w_in_l, wg_full, wo = saved[l]
        asl = a_scale[l:l + 1]
        (dms, dz, grads[f"a_out{l}"], st_out[l]), got = _out_bwd(dh, y, gate[l], mixed, asl, z, wo, F32, f"a{l}_out_bwd",
                                                                comm=carry(plan[l]["o"]))
        land(plan[l]["o"], got)
        (dval, dwg, st_pool[l]), got = _pool_bwd(dms, mixed, pooled, wg_full, asl, f"a{l}_pool_bwd", comm=carry(plan[l]["p"]))
        land(plan[l]["p"], got)
        grads[f"a_grp{l}"] = (dwg.reshape(4, NCHIP, GW // NCHIP, GW).transpose(1, 0, 2, 3).reshape(NCHIP, GW, GW)
                              .astype(BF16))
        (dh, grads[f"a_in{l}"], st_in[l]), got = _in_bwd(dval, dz, 0, u, hs[l], gl[l], scale[l], w_in_l, dh, f"a{l}_in_bwd",
                                                         comm=carry(plan[l]["i"]))
        land(plan[l]["i"], got)
    land(["a_in0"], _comm_only(carry(["a_in0"]), "scatter_last"))

    pieces = st_in + [st_kv] + st_out + [st_fin]
    pieces += [_pad8(st_pool[l][0].reshape(2, D)) for l in range(2)]
    pieces += [_pad8(d.reshape(NH * NRELP // D, D)) for d in (drb0, drb1)]
    stats = jnp.concatenate(pieces, axis=0)
    return dh, grads, landed, stats


ROW_IN = [8 * l for l in range(4)]
ROW_KV = 32
ROW_OUT = [40 + 8 * l for l in range(4)]
ROW_FIN = 72
ROW_ASC = [80, 88]
ROW_RB = [96, 104]
N_STAT = 112


def kernel(x, c, ada_w, ada_b, norm_g, a_w_in, a_w_group, a_scale, a_w_out, kv_norm_g, kv_ada_w, kv_ada_b, w_kv, b_w_in, b_rel_bias, b_w_out, final_g, loss_target, m_ada_w, m_ada_b, m_norm_g, m_a_w_in, m_a_w_group, m_a_scale, m_a_w_out, m_kv_norm_g, m_kv_ada_w, m_kv_ada_b, m_w_kv, m_b_w_in, m_b_rel_bias, m_b_w_out, m_final_g, v_ada_w, v_ada_b, v_norm_g, v_a_w_in, v_a_w_group, v_a_scale, v_a_w_out, v_kv_norm_g, v_kv_ada_w, v_kv_ada_b, v_w_kv, v_b_w_in, v_b_rel_bias, v_b_w_out, v_final_g):
    xi, yi, ci = _place()
    chip = 2 * xi + yi
    dev = 4 * xi + 2 * yi + ci
    n_ada = ada_w.shape[2]
    n_kva = kv_ada_w.shape[1]
    n_asc = a_scale.shape[1]

    c_all = _allgather8(jnp.broadcast_to(c, (SUBLANES, D)), "gather_c")[::SUBLANES]
    ada_b_sh = lax.dynamic_slice_in_dim(ada_b, chip * n_ada, n_ada, axis=1)
    kvb_sh = lax.dynamic_slice_in_dim(kv_ada_b, chip * n_kva, n_kva, axis=0)
    c_act, mod_ada = _cmat(c_all, ada_w, ada_b_sh[:, None, :], "mod_ada")
    _, mod_kv = _cmat(c_all, kv_ada_w[None], kvb_sh[None, None, :], "mod_kv")
    part = jnp.concatenate([mod_ada.transpose(1, 0, 2).reshape(SUBLANES, 4 * n_ada), mod_kv[0],
                            jnp.broadcast_to(a_scale.reshape(1, 2 * n_asc), (SUBLANES, 2 * n_asc))], axis=1)
    gathered = _allgather8(part, "gather_mod")
    rows = jnp.concatenate([lax.dynamic_slice_in_dim(gathered, SUBLANES * (2 * p + ci) + dev, 1, axis=0)
                            for p in range(NCHIP)], axis=0)
    mods = jnp.stack([rows[:, l * n_ada:(l + 1) * n_ada].reshape(3 * D) for l in range(4)])
    kvmod = rows[:, 4 * n_ada:4 * n_ada + n_kva].reshape(2 * D)
    o_asc = 4 * n_ada + n_kva
    a_scale_full = jnp.stack([rows[:, o_asc + l * n_asc:o_asc + (l + 1) * n_asc].reshape(E) for l in range(2)])

    sh = dict(a_in=[a_w_in[l].astype(BF16) for l in range(2)], a_grp=[a_w_group[l].astype(BF16) for l in range(2)],
              a_out=[a_w_out[l].astype(BF16) for l in range(2)], kv=[w_kv.astype(BF16)],
              b_in=[b_w_in[l].astype(BF16) for l in range(2)], b_out=[b_w_out[l].astype(BF16) for l in range(2)])
    dh, grads, landed, stats = _example_step(x[0], loss_target[0], mods, kvmod, a_scale_full, norm_g, kv_norm_g, final_g,
                                             b_rel_bias, sh)
    grad_x = dh[None]

    g3 = _allgather8(stats, "gather_stats").reshape(8, N_STAT, D)
    red, loss_tile = _stats_reduce(g3, ROW_FIN + 1, "stats_reduce")
    loss = loss_tile[0, 0]

    def cat(rows_):
        return jnp.concatenate(rows_, axis=-1)

    g_ada_b = jnp.stack([cat([red[ROW_IN[l]], red[ROW_IN[l] + 1], red[ROW_OUT[l]]]) for l in range(4)])
    g_norm_g = jnp.stack([red[ROW_IN[l] + 2] for l in range(4)])
    g_kv_norm_g = red[ROW_KV + 2]
    g_kv_ada_b = cat([red[ROW_KV], red[ROW_KV + 1]])
    g_final_g = red[ROW_FIN]
    g_asc_full = jnp.stack([red[ROW_ASC[l]:ROW_ASC[l] + 2].reshape(E) for l in range(2)])
    g_a_scale = lax.dynamic_slice_in_dim(g_asc_full, chip * n_asc, n_asc, axis=1)
    g_rel = jnp.stack([red[ROW_RB[bi]:ROW_RB[bi] + NH * NRELP // D].reshape(NH, NRELP)[:, :NREL] for bi in range(2)])

    dmod = jnp.stack([cat([g3[:, ROW_IN[l]], g3[:, ROW_IN[l] + 1], g3[:, ROW_OUT[l]]]) for l in range(4)])
    dmod_sh = lax.dynamic_slice_in_dim(dmod, chip * n_ada, n_ada, axis=2)
    dkv = cat([g3[:, ROW_KV], g3[:, ROW_KV + 1]])[None]
    dkv_sh = lax.dynamic_slice_in_dim(dkv, chip * n_kva, n_kva, axis=2)
    c_act_t = c_act.T
    g_ada_w = _grad_ada(c_act_t, dmod_sh, "grad_ada_w")
    g_kv_ada_w = _grad_ada(c_act_t, dkv_sh, "grad_kv_ada_w")

    order = ["a_in0", "a_in1", "a_grp0", "a_grp1", "a_out0", "a_out1", "kv", "b_in0", "b_in1", "b_out0", "b_out1"]
    chip_arr = jnp.reshape(chip, (1,)).astype(jnp.int32)
    partial = [_sum4(grads[n], landed[n], chip_arr, f"sum4_{n}") for n in order]
    other = _swap_sibling(partial, "swap_sibling")
    both = {n: [partial[k], other[k]] for k, n in enumerate(order)}

    def upd(w, m, v, g, name, shape3):
        g = g.reshape(shape3) if not isinstance(g, list) else g
        outs = _adamw(w.reshape(shape3), m.reshape(shape3), v.reshape(shape3), g, name)
        return [o.reshape(w.shape) for o in outs]

    def pair(name):
        return [both[name + "0"], both[name + "1"]]

    res = {}
    res["ada_w"] = upd(ada_w, m_ada_w, v_ada_w, g_ada_w, "adamw_ada_w", ada_w.shape)
    res["ada_b"] = upd(ada_b, m_ada_b, v_ada_b, g_ada_b, "adamw_ada_b", (1,) + ada_b.shape)
    res["norm_g"] = upd(norm_g, m_norm_g, v_norm_g, g_norm_g, "adamw_norm_g", (1,) + norm_g.shape)
    res["a_w_in"] = upd(a_w_in, m_a_w_in, v_a_w_in, pair("a_in"), "adamw_a_w_in", a_w_in.shape)
    res["a_w_group"] = upd(a_w_group, m_a_w_group, v_a_w_group, pair("a_grp"), "adamw_a_w_group", (2, GW, GW))
    res["a_scale"] = upd(a_scale, m_a_scale, v_a_scale, g_a_scale, "adamw_a_scale", (1,) + a_scale.shape)
    res["a_w_out"] = upd(a_w_out, m_a_w_out, v_a_w_out, pair("a_out"), "adamw_a_w_out", a_w_out.shape)
    res["kv_norm_g"] = upd(kv_norm_g, m_kv_norm_g, v_kv_norm_g, g_kv_norm_g, "adamw_kv_norm_g", (1, 1, D))
    res["kv_ada_w"] = upd(kv_ada_w, m_kv_ada_w, v_kv_ada_w, g_kv_ada_w, "adamw_kv_ada_w", (1,) + kv_ada_w.shape)
    res["kv_ada_b"] = upd(kv_ada_b, m_kv_ada_b, v_kv_ada_b, g_kv_ada_b, "adamw_kv_ada_b", (1, 1, 2 * D))
    res["w_kv"] = upd(w_kv, m_w_kv, v_w_kv, [both["kv"]], "adamw_w_kv", (1,) + w_kv.shape)
    res["b_w_in"] = upd(b_w_in, m_b_w_in, v_b_w_in, pair("b_in"), "adamw_b_w_in", b_w_in.shape)
    res["b_rel_bias"] = upd(b_rel_bias, m_b_rel_bias, v_b_rel_bias, g_rel, "adamw_b_rel_bias", (1, 2 * NH, NREL))
    res["b_w_out"] = upd(b_w_out, m_b_w_out, v_b_w_out, pair("b_out"), "adamw_b_w_out", b_w_out.shape)
    res["final_g"] = upd(final_g, m_final_g, v_final_g, g_final_g, "adamw_final_g", (1, 1, D))

    names = ["ada_w", "ada_b", "norm_g", "a_w_in", "a_w_group", "a_scale", "a_w_out", "kv_norm_g", "kv_ada_w", "kv_ada_b",
             "w_kv", "b_w_in", "b_rel_bias", "b_w_out", "final_g"]
    return (loss, grad_x, *[res[n][0] for n in names], *[res[n][1] for n in names], *[res[n][2] for n in names],
            *[res[n][3] for n in names])
```

```python
import math

import jax
import jax.numpy as jnp
from jax import lax
from jax.experimental import pallas as pl
from jax.experimental.pallas import tpu as pltpu

F32 = jnp.float32
BF16 = jnp.bfloat16

D = 1024
E = 2048
NH = 16
HD = 128
CHUNK = 64
LEFT = 8
PAD = LEFT * CHUNK
NREL = 257
NRELP = 384
REL_CLIP = 128
EPS = 1e-6
NEG = -1e30
LOG2E = math.log2(math.e)
SM_SCALE = HD ** -0.5
POOL_W = (2, 4, 8, 16)
GW = 512
HALO = 16
QC = 4
QB = QC * CHUNK
WIN = (QC + LEFT) * CHUNK
BW = (LEFT + 2) * CHUNK
NSUB = 4
NCHIP = 4
LANES = 128
SUBLANES = 8

ADAM_LR = 0.001
ADAM_B1 = 0.9
ADAM_B2 = 0.999
ADAM_EPS = 1e-08
ADAM_WD = 0.01
ADAM_STEP = 10

MESH = pl.DeviceIdType.MESH
ANY = pl.BlockSpec(memory_space=pl.ANY)


def _params(n_axes, vmem_mb):
    return pltpu.CompilerParams(dimension_semantics=("arbitrary",) * n_axes, vmem_limit_bytes=vmem_mb * 2 ** 20)


def _nn(a, b):
    return jnp.dot(a, b, preferred_element_type=F32)


def _nt(a, b):
    return lax.dot_general(a, b, (((1,), (1,)), ((), ())), preferred_element_type=F32)


def _tn(a, b):
    return lax.dot_general(a, b, (((0,), (0,)), ((), ())), preferred_element_type=F32)


def _row(n):
    return pl.BlockSpec((1, n), lambda i: (0, 0))


def _colsum(x):
    return jnp.sum(x, axis=0, keepdims=True)


def _place():
    return lax.axis_index("x"), lax.axis_index("y"), lax.axis_index("c")


class _Comm:
    def __init__(self, gathers=(), scatters=()):
        self.n_g = len(gathers)
        self.arrays = list(gathers) + list(scatters)
        self.n = len(self.arrays)
        self.half = [a.shape[0] // 2 for a in gathers]
        self.out_shape = ([jax.ShapeDtypeStruct((NCHIP,) + a.shape, a.dtype) for a in gathers]
                          + [jax.ShapeDtypeStruct((3,) + a.shape[1:], a.dtype) for a in scatters])
        n_f = max(3 * self.n_g, 1)
        self.scratch = [pltpu.SemaphoreType.DMA((3 * self.n,)), pltpu.SemaphoreType.DMA((3 * self.n,)),
                        pltpu.SemaphoreType.DMA((max(self.n_g, 1),)), pltpu.SemaphoreType.DMA((n_f,)),
                        pltpu.SemaphoreType.DMA((n_f,))]

    def _chip_copies(self, ins, outs, send, recv, landing):
        x, y, c = _place()
        chips = [(1 - x, y), (x, 1 - y), (1 - x, 1 - y)]
        mine = 2 * x + y
        cps = []
        for k in range(self.n):
            for j, (cx, cy) in enumerate(chips):
                q = 2 * cx + cy
                if k < self.n_g:
                    part = pl.ds(c * self.half[k], self.half[k])
                    src = ins[k].at[part]
                    dst = outs[k].at[q if landing else mine, part]
                else:
                    src = ins[k].at[q]
                    dst = outs[k].at[j]
                cps.append(pltpu.make_async_remote_copy(
                    src_ref=src, dst_ref=dst, send_sem=send.at[3 * k + j], recv_sem=recv.at[3 * k + j],
                    device_id=(cx, cy, c), device_id_type=MESH))
        return cps

    def _core_copies(self, outs, fsend, frecv, landing):
        x, y, c = _place()
        chips = [(1 - x, y), (x, 1 - y), (1 - x, 1 - y)]
        cps = []
        for k in range(self.n_g):
            for j, (cx, cy) in enumerate(chips):
                part = pl.ds((1 - c if landing else c) * self.half[k], self.half[k])
                blk = outs[k].at[2 * cx + cy, part]
                cps.append(pltpu.make_async_remote_copy(
                    src_ref=blk, dst_ref=blk, send_sem=fsend.at[3 * k + j], recv_sem=frecv.at[3 * k + j],
                    device_id=(x, y, 1 - c), device_id_type=MESH))
        return cps

    def _local_copies(self, ins, outs, loc):
        x, y, _ = _place()
        return [pltpu.make_async_copy(ins[k], outs[k].at[2 * x + y], loc.at[k]) for k in range(self.n_g)]

    def start(self, ins, outs, send, recv, loc, fsend, frecv):
        for cp in self._local_copies(ins, outs, loc) + self._chip_copies(ins, outs, send, recv, False):
            cp.start()

    def wait(self, ins, outs, send, recv, loc, fsend, frecv):
        lands = self._chip_copies(ins, outs, send, recv, True)
        passes = self._core_copies(outs, fsend, frecv, False)
        for k in range(self.n):
            for j in range(3):
                lands[3 * k + j].wait_recv()
                if k < self.n_g:
                    passes[3 * k + j].start()
        for cp in self._core_copies(outs, fsend, frecv, True):
            cp.wait_recv()
        for cp in self._chip_copies(ins, outs, send, recv, False) + passes:
            cp.wait_send()
        for cp in self._local_copies(ins, outs, loc):
            cp.wait()


def _call(body, name, grid, in_specs, out_specs, out_shape, scratch, params, args, comm=None):
    n_in, n_out, n_sc = len(in_specs), len(out_specs), len(scratch)
    if comm is None:
        outs = pl.pallas_call(body, name=name, grid=grid, in_specs=in_specs, out_specs=out_specs, out_shape=out_shape,
                              scratch_shapes=scratch, compiler_params=params)(*args)
        return list(outs), []
    n = comm.n
    o0 = n_in + n
    s0 = o0 + n_out + n

    def wrapped(*refs):
        c_refs = (refs[n_in:o0], refs[o0 + n_out:s0]) + tuple(refs[s0 + n_sc:])
        ids = [pl.program_id(a) for a in range(len(grid))]
        first = ids[0] == 0
        last = ids[0] == grid[0] - 1
        for a in range(1, len(grid)):
            first = first & (ids[a] == 0)
            last = last & (ids[a] == grid[a] - 1)

        @pl.when(first)
        def _():
            comm.start(*c_refs)

        body(*refs[:n_in], *refs[o0:o0 + n_out], *refs[s0:s0 + n_sc])

        @pl.when(last)
        def _():
            comm.wait(*c_refs)

    outs = pl.pallas_call(
        wrapped, name=name, grid=grid, in_specs=list(in_specs) + [ANY] * n, out_specs=list(out_specs) + [ANY] * n,
        out_shape=list(out_shape) + comm.out_shape, scratch_shapes=list(scratch) + comm.scratch, compiler_params=params,
    )(*args, *comm.arrays)
    return list(outs[:n_out]), list(outs[n_out:])


def _comm_only(comm, name):
    def body(*refs):
        c_refs = (refs[:comm.n], refs[comm.n:2 * comm.n]) + tuple(refs[2 * comm.n:])
        comm.start(*c_refs)
        comm.wait(*c_refs)

    return pl.pallas_call(body, name=name, in_specs=[ANY] * comm.n, out_specs=[ANY] * comm.n, out_shape=comm.out_shape,
                          scratch_shapes=comm.scratch)(*comm.arrays)


def _in_fwd(h, g, shift, scale, w, dt_a, dt_b, name, pad_rows=0, comm=None, tm=512):
    S = h.shape[0]
    n_pad = pad_rows // tm

    def body(h_ref, g_ref, sh_ref, sc_ref, w_hbm, u_ref, oa_ref, ob_ref, w_v, sem):
        i = pl.program_id(0)

        @pl.when(i == 0)
        def _():
            cp = pltpu.make_async_copy(w_hbm, w_v, sem)
            cp.start()
            cp.wait()

        hh = h_ref[...]
        r = lax.rsqrt(jnp.mean(hh * hh, axis=-1, keepdims=True) + EPS)
        u = (hh * r * g_ref[...]) * (1.0 + sc_ref[...]) + sh_ref[...]
        ub = u.astype(BF16)
        u_ref[...] = ub
        for q in range(NCHIP):
            o_ref = oa_ref if q < 2 else ob_ref
            o_ref[:, (q % 2) * D:(q % 2 + 1) * D] = _nn(ub, w_v[q]).astype(o_ref.dtype)

        if n_pad:
            @pl.when(i < n_pad)
            def _():
                oa_ref[...] = jnp.zeros(oa_ref.shape, oa_ref.dtype)
                ob_ref[...] = jnp.zeros(ob_ref.shape, ob_ref.dtype)

    def src(i):
        return (jnp.maximum(i - n_pad, 0), 0)

    outs, landed = _call(
        body, name, (S // tm + n_pad,),
        [pl.BlockSpec((tm, D), src), _row(D), _row(D), _row(D), ANY],
        [pl.BlockSpec((tm, D), src), pl.BlockSpec((tm, E), lambda i: (i, 0)), pl.BlockSpec((tm, E), lambda i: (i, 0))],
        [jax.ShapeDtypeStruct((S, D), BF16), jax.ShapeDtypeStruct((S + pad_rows, E), dt_a),
         jax.ShapeDtypeStruct((S + pad_rows, E), dt_b)],
        [pltpu.VMEM((NCHIP, D, D), BF16), pltpu.SemaphoreType.DMA],
        _params(1, 52), (h, g, shift, scale, w), comm)
    return outs, landed


def _a_fwd(h, g, shift, scale, asc, gate, w_in, wg, w_out, name, comm=None, tm=512):
    S = h.shape[0]

    def body(h_ref, g_ref, sh_ref, sc_ref, as_ref, gate_ref, wi_hbm, wg_hbm, wo_hbm,
             u_ref, z_ref, p_ref, m_ref, y_ref, ho_ref, wi_v, wg_v, wo_v, buf, sems):
        i = pl.program_id(0)

        @pl.when(i == 0)
        def _():
            cps = [pltpu.make_async_copy(wi_hbm, wi_v, sems.at[0]), pltpu.make_async_copy(wg_hbm, wg_v, sems.at[1]),
                   pltpu.make_async_copy(wo_hbm, wo_v, sems.at[2])]
            for cp in cps:
                cp.start()
            buf[0:HALO, :] = jnp.zeros((HALO, E), F32)
            for cp in cps:
                cp.wait()

        hh = h_ref[...]
        r = lax.rsqrt(jnp.mean(hh * hh, axis=-1, keepdims=True) + EPS)
        ub = ((hh * r * g_ref[...]) * (1.0 + sc_ref[...]) + sh_ref[...]).astype(BF16)
        u_ref[...] = ub
        for q in range(2):
            buf[HALO:HALO + tm, q * D:(q + 1) * D] = _nn(ub, wi_v[q])
        t = i * tm + lax.broadcasted_iota(jnp.int32, (tm, 1), 0)
        y = None
        for gi, w in enumerate(POOL_W):
            cols = slice(gi * GW, (gi + 1) * GW)
            x = buf[:, cols]
            s = x
            k = 1
            while k < w:
                s = s + pltpu.roll(s, k, 0)
                k *= 2
            cnt = jnp.minimum(t + 1, w).astype(F32)
            pb = (s[HALO:, :] / cnt - x[HALO:, :]).astype(BF16)
            p_ref[:, cols] = pb
            mb = _nn(pb, wg_v[gi]).astype(BF16)
            m_ref[:, cols] = mb
            zb = _nn(ub, wi_v[2 + gi // 2, :, (gi % 2) * GW:(gi % 2 + 1) * GW]).astype(BF16)
            z_ref[:, cols] = zb
            zz = zb.astype(F32)
            act = ((mb.astype(F32) * as_ref[:, cols]) * (zz * jax.nn.sigmoid(zz))).astype(BF16)
            part = _nn(act, wo_v[gi])
            y = part if y is None else y + part
        buf[0:HALO, :] = buf[tm:tm + HALO, :]
        y_ref[...] = y.astype(BF16)
        ho_ref[...] = hh + gate_ref[...] * y

    rows_d = pl.BlockSpec((tm, D), lambda i: (i, 0))
    rows_e = pl.BlockSpec((tm, E), lambda i: (i, 0))
    return _call(
        body, name, (S // tm,),
        [rows_d, _row(D), _row(D), _row(D), _row(E), _row(D), ANY, ANY, ANY],
        [rows_d, rows_e, rows_e, rows_e, rows_d, rows_d],
        [jax.ShapeDtypeStruct((S, D), BF16), jax.ShapeDtypeStruct((S, E), BF16), jax.ShapeDtypeStruct((S, E), BF16),
         jax.ShapeDtypeStruct((S, E), BF16), jax.ShapeDtypeStruct((S, D), BF16), jax.ShapeDtypeStruct((S, D), F32)],
        [pltpu.VMEM((NCHIP, D, D), BF16), pltpu.VMEM((4, GW, GW), BF16), pltpu.VMEM((NCHIP, GW, D), BF16),
         pltpu.VMEM((tm + HALO, E), F32), pltpu.SemaphoreType.DMA((3,))],
        _params(1, 60), (h, g, shift, scale, asc, gate, w_in, wg, w_out), comm)


def _out_fwd(a, z, w, gate, h, name, comm=None, tm=512):
    S = h.shape[0]
    kb = E // NCHIP

    def body(a_ref, z_ref, w_hbm, gate_ref, h_ref, y_ref, ho_ref, w_v, sem):
        @pl.when(pl.program_id(0) == 0)
        def _():
            cp = pltpu.make_async_copy(w_hbm, w_v, sem)
            cp.start()
            cp.wait()

        y = None
        for p in range(NCHIP):
            cols = slice(p * kb, (p + 1) * kb)
            zz = z_ref[:, cols].astype(F32)
            act = (a_ref[:, cols].astype(F32) * (zz * jax.nn.sigmoid(zz))).astype(BF16)
            part = _nn(act, w_v[p])
            y = part if y is None else y + part
        y_ref[...] = y.astype(BF16)
        ho_ref[...] = h_ref[...] + gate_ref[...] * y

    return _call(
        body, name, (S // tm,),
        [pl.BlockSpec((tm, E), lambda i: (i, 0)), pl.BlockSpec((tm, E), lambda i: (i, 0)), ANY, _row(D),
         pl.BlockSpec((tm, D), lambda i: (i, 0))],
        [pl.BlockSpec((tm, D), lambda i: (i, 0)), pl.BlockSpec((tm, D), lambda i: (i, 0))],
        [jax.ShapeDtypeStruct((S, D), BF16), jax.ShapeDtypeStruct((S, D), F32)],
        [pltpu.VMEM((NCHIP, kb, D), BF16), pltpu.SemaphoreType.DMA],
        _params(1, 52), (a, z, w, gate, h), comm)


TW = BW + LANES


def _diag_onehot(transpose):
    shape = (TW, NRELP) if transpose else (NRELP, TW)
    j = lax.broadcasted_iota(jnp.int32, shape, 0 if transpose else 1)
    r = lax.broadcasted_iota(jnp.int32, shape, 1 if transpose else 0)
    idx = jnp.clip(PAD - (j - LANES), -REL_CLIP, REL_CLIP) + REL_CLIP
    return jnp.where(idx == r, 1.0, 0.0).astype(BF16)


def _strip_valid():
    m = lax.broadcasted_iota(jnp.int32, (NH, BW), 1)
    return m < (LEFT + 1) * CHUNK, m >= CHUNK


def _bias_build(rb, name):
    def body(rb_ref, a_ref, b_ref):
        x = rb_ref[...]
        hi = x.astype(BF16)
        r1 = x - hi.astype(F32)
        mid = r1.astype(BF16)
        lo = (r1 - mid.astype(F32)).astype(BF16)
        oh = _diag_onehot(False)
        diag = (_nn(hi, oh) + _nn(mid, oh)) + _nn(lo, oh)
        valid_a, valid_b = _strip_valid()
        for qi in range(CHUNK):
            a_ref[qi] = jnp.where(valid_a, pltpu.roll(diag, TW - (LANES - qi), 1)[:, :BW], NEG)
            b_ref[qi] = jnp.where(valid_b, pltpu.roll(diag, TW - (CHUNK - qi), 1)[:, :BW], NEG)

    vmem = pl.BlockSpec(memory_space=pltpu.VMEM)
    return pl.pallas_call(
        body, name=name, in_specs=[vmem], out_specs=[vmem, vmem],
        out_shape=[jax.ShapeDtypeStruct((CHUNK, NH, BW), F32), jax.ShapeDtypeStruct((CHUNK, NH, BW), F32)],
        compiler_params=pltpu.CompilerParams(vmem_limit_bytes=32 * 2 ** 20),
    )(rb)


def _dbias_reduce(dba, dbb, name):
    def body(a_ref, b_ref, o_ref):
        valid_a, valid_b = _strip_valid()
        zeros = jnp.zeros((NH, TW - BW), F32)
        acc = jnp.zeros((NH, TW), F32)
        for qi in range(CHUNK):
            xa = jnp.concatenate([jnp.where(valid_a, a_ref[qi], 0.0), zeros], axis=1)
            xb = jnp.concatenate([jnp.where(valid_b, b_ref[qi], 0.0), zeros], axis=1)
            acc = acc + (pltpu.roll(xa, LANES - qi, 1) + pltpu.roll(xb, CHUNK - qi, 1))
        oh = _diag_onehot(True)
        hi = acc.astype(BF16)
        mid = (acc - hi.astype(F32)).astype(BF16)
        o_ref[...] = _nn(hi, oh) + _nn(mid, oh)

    vmem = pl.BlockSpec(memory_space=pltpu.VMEM)
    return pl.pallas_call(
        body, name=name, in_specs=[vmem, vmem], out_specs=vmem,
        out_shape=jax.ShapeDtypeStruct((NH, NRELP), F32),
        compiler_params=pltpu.CompilerParams(vmem_limit_bytes=32 * 2 ** 20),
    )(dba, dbb)


def _build_bias(bias3, ba_ref, bb_ref):
    bias3[2] = jnp.full((QB, WIN), NEG, F32)
    for qc in range(QC):
        rows = slice(qc * CHUNK, (qc + 1) * CHUNK)
        if qc % 2 == 0:
            bias3[2, rows, qc * CHUNK:qc * CHUNK + BW] = ba_ref[...] * LOG2E
        else:
            bias3[2, rows, (qc - 1) * CHUNK:(qc - 1) * CHUNK + BW] = bb_ref[...] * LOG2E
    col = lax.broadcasted_iota(jnp.int32, (QB, WIN), 1)
    for sub in range(PAD // QB):
        bias3[sub] = jnp.where(col < PAD - sub * QB, NEG, bias3[2])


def _softmax_parts(q_ref, k_ref, bias3, i, sub, row0):
    qq = q_ref[sub * QB:(sub + 1) * QB, :]
    kw = k_ref[pl.ds(row0, WIN), :]
    which = jnp.where(i == 0, sub, 2) if sub < PAD // QB else 2
    s = _nt(qq, kw) * (SM_SCALE * LOG2E) + bias3[which]
    e = jnp.exp2(s - jnp.max(s, axis=-1, keepdims=True))
    return qq, kw, e, jnp.sum(e, axis=-1, keepdims=True)


def _attn_fwd(q, kp, vp, ba, bb, name, comm=None):
    S = q.shape[0]
    R = NSUB * QB
    assert S % R == 0 and NSUB >= PAD // QB

    def body(q_ref, k_ref, v_ref, ba_ref, bb_ref, o_ref, bias3):
        i = pl.program_id(1)

        @pl.when(i == 0)
        def _():
            _build_bias(bias3, ba_ref, bb_ref)

        for sub in range(NSUB):
            row0 = pl.multiple_of((i * NSUB + sub) * QB, QB)
            _, _, e, l = _softmax_parts(q_ref, k_ref, bias3, i, sub, row0)
            o_ref[sub * QB:(sub + 1) * QB, :] = (_nn(e.astype(BF16), v_ref[pl.ds(row0, WIN), :]) / l).astype(BF16)

    return _call(
        body, name, (NH, S // R),
        [pl.BlockSpec((R, HD), lambda h, i: (i, h)), pl.BlockSpec((S + PAD, HD), lambda h, i: (0, h)),
         pl.BlockSpec((S + PAD, HD), lambda h, i: (0, h)), pl.BlockSpec((None, CHUNK, BW), lambda h, i: (h, 0, 0)),
         pl.BlockSpec((None, CHUNK, BW), lambda h, i: (h, 0, 0))],
        [pl.BlockSpec((R, HD), lambda h, i: (i, h))],
        [jax.ShapeDtypeStruct((S, E), BF16)],
        [pltpu.VMEM((3, QB, WIN), F32)],
        _params(2, 48), (q, kp, vp, ba, bb), comm)


def _final(h, g, tgt, name, tm=512):
    S = h.shape[0]

    def body(h_ref, g_ref, t_ref, dh_ref, st_ref):
        @pl.when(pl.program_id(0) == 0)
        def _():
            st_ref[...] = jnp.zeros((SUBLANES, D), F32)

        hh = h_ref[...]
        r = lax.rsqrt(jnp.mean(hh * hh, axis=-1, keepdims=True) + EPS)
        xhat = hh * r
        diff = xhat * g_ref[...] - t_ref[...]
        st_ref[1:2, :] += _colsum(diff * diff)
        dout = diff * (1.0 / D)
        st_ref[0:1, :] += _colsum(dout * xhat)
        dx = dout * g_ref[...]
        dh_ref[...] = r * (dx - xhat * jnp.mean(dx * xhat, axis=-1, keepdims=True))

    return pl.pallas_call(
        body, name=name, grid=(S // tm,),
        in_specs=[pl.BlockSpec((tm, D), lambda i: (i, 0)), _row(D), pl.BlockSpec((tm, D), lambda i: (i, 0))],
        out_specs=[pl.BlockSpec((tm, D), lambda i: (i, 0)), pl.BlockSpec((SUBLANES, D), lambda i: (0, 0))],
        out_shape=[jax.ShapeDtypeStruct((S, D), F32), jax.ShapeDtypeStruct((SUBLANES, D), F32)],
        compiler_params=_params(1, 32),
    )(h, g, tgt)


def _store_grad(acc, stage, dw_hbm, sem):
    for q in range(NCHIP):
        stage[...] = acc[q].astype(BF16)
        cp = pltpu.make_async_copy(stage, dw_hbm.at[q], sem)
        cp.start()
        cp.wait()


def _out_bwd(dh, y, gate, a, cs, z, w, name, comm=None, tm=256):
    S = dh.shape[0]
    kb = E // NCHIP
    n_t = S // tm

    def body(dh_ref, y_ref, gate_ref, a_ref, cs_ref, z_ref, w_hbm, da_ref, dz_ref, dw_hbm, st_ref, w_v, acc, stage, sem):
        i = pl.program_id(0)

        @pl.when(i == 0)
        def _():
            cp = pltpu.make_async_copy(w_hbm, w_v, sem)
            cp.start()
            acc[...] = jnp.zeros(acc.shape, F32)
            st_ref[...] = jnp.zeros((SUBLANES, D), F32)
            cp.wait()

        dhh = dh_ref[...]
        st_ref[0:1, :] += _colsum(dhh * y_ref[...].astype(F32))
        dy = (dhh * gate_ref[...]).astype(BF16)
        for p in range(NCHIP):
            cols = slice(p * kb, (p + 1) * kb)
            zz = z_ref[:, cols].astype(F32)
            sig = jax.nn.sigmoid(zz)
            sz = zz * sig
            ae = a_ref[:, cols].astype(F32) * cs_ref[:, cols]
            acc[p] += _tn((ae * sz).astype(BF16), dy)
            dact = _nt(dy, w_v[p])
            da_ref[:, cols] = (dact * sz).astype(BF16)
            dz_ref[:, cols] = (dact * ae * (sig * (1.0 + zz * (1.0 - sig)))).astype(BF16)

        @pl.when(i == n_t - 1)
        def _():
            _store_grad(acc, stage, dw_hbm, sem)

    return _call(
        body, name, (n_t,),
        [pl.BlockSpec((tm, D), lambda i: (i, 0)), pl.BlockSpec((tm, D), lambda i: (i, 0)), _row(D),
         pl.BlockSpec((tm, E), lambda i: (i, 0)), _row(E), pl.BlockSpec((tm, E), lambda i: (i, 0)), ANY],
        [pl.BlockSpec((tm, E), lambda i: (i, 0)), pl.BlockSpec((tm, E), lambda i: (i, 0)), ANY,
         pl.BlockSpec((SUBLANES, D), lambda i: (0, 0))],
        [jax.ShapeDtypeStruct((S, E), BF16), jax.ShapeDtypeStruct((S, E), BF16),
         jax.ShapeDtypeStruct((NCHIP, kb, D), BF16), jax.ShapeDtypeStruct((SUBLANES, D), F32)],
        [pltpu.VMEM((NCHIP, kb, D), BF16), pltpu.VMEM((NCHIP, kb, D), F32), pltpu.VMEM((kb, D), BF16),
         pltpu.SemaphoreType.DMA],
        _params(1, 52), (dh, y, gate, a, cs, z, w), comm)


def _attn_bwd(q, kp, vp, ba, bb, do, prev, name, comm=None):
    S = q.shape[0]
    R = NSUB * QB
    n_i = S // R
    dt_kv = F32 if prev is None else BF16

    def body(*refs):
        q_ref, k_ref, v_ref, ba_ref, bb_ref, do_ref = refs[:6]
        refs = refs[6:]
        if prev is not None:
            pk_hbm, pv_hbm = refs[:2]
            refs = refs[2:]
        dq_ref, dk_ref, dv_ref, dba_ref, dbb_ref, bias3, dbias, dk_acc, dv_acc = refs[:9]
        if prev is not None:
            pk_v, pv_v, sems = refs[9:]
        h = pl.program_id(0)
        i = pl.program_id(1)

        def prev_copies():
            cols = pl.ds(pl.multiple_of(h * HD, HD), HD)
            return (pltpu.make_async_copy(pk_hbm.at[:, cols], pk_v, sems.at[0]),
                    pltpu.make_async_copy(pv_hbm.at[:, cols], pv_v, sems.at[1]))

        @pl.when(i == 0)
        def _():
            if prev is not None:
                for cp in prev_copies():
                    cp.start()
            _build_bias(bias3, ba_ref, bb_ref)
            dbias[...] = jnp.zeros((QB, WIN), F32)
            dk_acc[...] = jnp.zeros((S + PAD, HD), F32)
            dv_acc[...] = jnp.zeros((S + PAD, HD), F32)

        for sub in range(NSUB):
            rows = slice(sub * QB, (sub + 1) * QB)
            row0 = pl.multiple_of((i * NSUB + sub) * QB, QB)
            qq, kw, e, l = _softmax_parts(q_ref, k_ref, bias3, i, sub, row0)
            p = e * (1.0 / l)
            dd = do_ref[rows, :]
            dp = _nt(dd, v_ref[pl.ds(row0, WIN), :])
            ds = p * (dp - jnp.sum(p * dp, axis=-1, keepdims=True))
            dbias[...] += ds
            dsb = (ds * SM_SCALE).astype(BF16)
            dq_ref[rows, :] = _nn(dsb, kw).astype(BF16)
            dk_acc[pl.ds(row0, WIN), :] += _tn(dsb, qq)
            dv_acc[pl.ds(row0, WIN), :] += _tn(p.astype(BF16), dd)

        @pl.when(i == n_i - 1)
        def _():
            da = None
            db = None
            for qc in range(QC):
                rows = slice(qc * CHUNK, (qc + 1) * CHUNK)
                if qc % 2 == 0:
                    part = dbias[rows, qc * CHUNK:qc * CHUNK + BW]
                    da = part if da is None else da + part
                else:
                    part = dbias[rows, (qc - 1) * CHUNK:(qc - 1) * CHUNK + BW]
                    db = part if db is None else db + part
            dba_ref[...] = da
            dbb_ref[...] = db
            if prev is None:
                dk_ref[...] = dk_acc[...]
                dv_ref[...] = dv_acc[...]
            else:
                for cp in prev_copies():
                    cp.wait()
                dk_ref[...] = (dk_acc[...] + pk_v[...]).astype(BF16)
                dv_ref[...] = (dv_acc[...] + pv_v[...]).astype(BF16)

    head = pl.BlockSpec((S + PAD, HD), lambda h, i: (0, h))
    strip = pl.BlockSpec((None, CHUNK, BW), lambda h, i: (h, 0, 0))
    blk = pl.BlockSpec((R, HD), lambda h, i: (i, h))
    in_specs = [blk, head, head, strip, strip, blk]
    scratch = [pltpu.VMEM((3, QB, WIN), F32), pltpu.VMEM((QB, WIN), F32), pltpu.VMEM((S + PAD, HD), F32),
               pltpu.VMEM((S + PAD, HD), F32)]
    args = (q, kp, vp, ba, bb, do)
    if prev is not None:
        in_specs += [ANY, ANY]
        scratch += [pltpu.VMEM((S + PAD, HD), F32), pltpu.VMEM((S + PAD, HD), F32), pltpu.SemaphoreType.DMA((2,))]
        args += tuple(prev)
    return _call(
        body, name, (NH, n_i), in_specs, [blk, head, head, strip, strip],
        [jax.ShapeDtypeStruct((S, E), BF16), jax.ShapeDtypeStruct((S + PAD, E), dt_kv),
         jax.ShapeDtypeStruct((S + PAD, E), dt_kv), jax.ShapeDtypeStruct((NH, CHUNK, BW), F32),
         jax.ShapeDtypeStruct((NH, CHUNK, BW), F32)],
        scratch, _params(2, 56), args, comm)


def _pool_bwd(dms, mixed, pooled, wg, a_scale, name, comm=None, tm=512):
    S = dms.shape[0]
    n_t = S // tm

    def rev(i):
        return (n_t - 1 - i, 0)

    def body(d_ref, m_ref, p_ref, wg_ref, as_ref, dv_ref, dwg_ref, st_ref, buf):
        i = pl.program_id(0)

        @pl.when(i == 0)
        def _():
            buf[tm:tm + HALO, :] = jnp.zeros((HALO, E), F32)
            dwg_ref[...] = jnp.zeros((4, GW, GW), F32)
            st_ref[...] = jnp.zeros((SUBLANES, E), F32)

        t = (n_t - 1 - i) * tm + lax.broadcasted_iota(jnp.int32, (tm, 1), 0)
        st_ref[0:1, :] += _colsum(d_ref[...].astype(F32) * m_ref[...].astype(F32))
        for gi, w in enumerate(POOL_W):
            cols = slice(gi * GW, (gi + 1) * GW)
            dm = (d_ref[:, cols].astype(F32) * as_ref[:, cols]).astype(BF16)
            dpool = _nt(dm, wg_ref[gi])
            dwg_ref[gi] += _tn(p_ref[:, cols], dm)
            cnt = jnp.minimum(t + 1, w).astype(F32)
            buf[0:tm, cols] = dpool / cnt
            s = buf[:, cols]
            k = 1
            while k < w:
                s = s + pltpu.roll(s, tm + HALO - k, 0)
                k *= 2
            dv_ref[:, cols] = (s[0:tm, :] - dpool).astype(BF16)
        buf[tm:tm + HALO, :] = buf[0:HALO, :]

    return _call(
        body, name, (n_t,),
        [pl.BlockSpec((tm, E), rev), pl.BlockSpec((tm, E), rev), pl.BlockSpec((tm, E), rev),
         pl.BlockSpec((4, GW, GW), lambda i: (0, 0, 0)), _row(E)],
        [pl.BlockSpec((tm, E), rev), pl.BlockSpec((4, GW, GW), lambda i: (0, 0, 0)),
         pl.BlockSpec((SUBLANES, E), lambda i: (0, 0))],
        [jax.ShapeDtypeStruct((S, E), BF16), jax.ShapeDtypeStruct((4, GW, GW), F32),
         jax.ShapeDtypeStruct((SUBLANES, E), F32)],
        [pltpu.VMEM((tm + HALO, E), F32)],
        _params(1, 52), (dms, mixed, pooled, wg, a_scale), comm)


def _in_bwd(da, db, row_off, u, h, g, scale, w, dh_out, name, comm=None, tm=256):
    S = h.shape[0]
    n_t = S // tm
    off = row_off // tm

    def body(da_ref, db_ref, u_ref, h_ref, g_ref, sc_ref, w_hbm, dho_ref, dhi_ref, dw_hbm, st_ref, w_v, acc, stage, sem):
        i = pl.program_id(0)

        @pl.when(i == 0)
        def _():
            cp = pltpu.make_async_copy(w_hbm, w_v, sem)
            cp.start()
            acc[...] = jnp.zeros(acc.shape, F32)
            st_ref[...] = jnp.zeros((SUBLANES, D), F32)
            cp.wait()

        ub = u_ref[...]
        du = None
        for q in range(NCHIP):
            d_ref = da_ref if q < 2 else db_ref
            dv = d_ref[:, (q % 2) * D:(q % 2 + 1) * D]
            acc[q] += _tn(ub, dv)
            part = _nt(dv, w_v[q])
            du = part if du is None else du + part

        hh = h_ref[...]
        r = lax.rsqrt(jnp.mean(hh * hh, axis=-1, keepdims=True) + EPS)
        xhat = hh * r
        gg = g_ref[...]
        st_ref[0:1, :] += _colsum(du)
        st_ref[1:2, :] += _colsum(du * (xhat * gg))
        dn = du * (1.0 + sc_ref[...])
        st_ref[2:3, :] += _colsum(dn * xhat)
        dx = dn * gg
        dhi_ref[...] = dho_ref[...] + r * (dx - xhat * jnp.mean(dx * xhat, axis=-1, keepdims=True))

        @pl.when(i == n_t - 1)
        def _():
            _store_grad(acc, stage, dw_hbm, sem)

    part_spec = pl.BlockSpec((tm, E), lambda i: (i + off, 0))
    return _call(
        body, name, (n_t,),
        [part_spec, part_spec, pl.BlockSpec((tm, D), lambda i: (i, 0)), pl.BlockSpec((tm, D), lambda i: (i, 0)),
         _row(D), _row(D), ANY, pl.BlockSpec((tm, D), lambda i: (i, 0))],
        [pl.BlockSpec((tm, D), lambda i: (i, 0)), ANY, pl.BlockSpec((SUBLANES, D), lambda i: (0, 0))],
        [jax.ShapeDtypeStruct((S, D), F32), jax.ShapeDtypeStruct((NCHIP, D, D), BF16),
         jax.ShapeDtypeStruct((SUBLANES, D), F32)],
        [pltpu.VMEM((NCHIP, D, D), BF16), pltpu.VMEM((NCHIP, D, D), F32), pltpu.VMEM((D, D), BF16),
         pltpu.SemaphoreType.DMA],
        _params(1, 56), (da, db, u, h, g, scale, w, dh_out), comm)


def _cmat(c_all, w, b, name):
    L, _, n = w.shape

    def body(c_ref, w_ref, b_ref, ca_ref, o_ref):
        cc = c_ref[...]
        ca = cc * jax.nn.sigmoid(cc)
        ca_ref[...] = ca
        o_ref[...] = _nn(ca.astype(BF16), w_ref[...].astype(BF16)) + b_ref[...]

    return pl.pallas_call(
        body, name=name, grid=(L,),
        in_specs=[pl.BlockSpec((SUBLANES, D), lambda l: (0, 0)), pl.BlockSpec((None, D, n), lambda l: (l, 0, 0)),
                  pl.BlockSpec((None, 1, n), lambda l: (l, 0, 0))],
        out_specs=[pl.BlockSpec((SUBLANES, D), lambda l: (0, 0)), pl.BlockSpec((None, SUBLANES, n), lambda l: (l, 0, 0))],
        out_shape=[jax.ShapeDtypeStruct((SUBLANES, D), F32), jax.ShapeDtypeStruct((L, SUBLANES, n), F32)],
        compiler_params=_params(1, 32),
    )(c_all, w, b)


def _grad_ada(c_act_t, dmod, name):
    L, _, n = dmod.shape

    def body(c_ref, d_ref, o_ref):
        acc = None
        for b in range(SUBLANES):
            part = c_ref[:, b:b + 1] * d_ref[b:b + 1, :]
            acc = part if acc is None else acc + part
        o_ref[...] = acc

    return pl.pallas_call(
        body, name=name, grid=(L,),
        in_specs=[pl.BlockSpec((D, SUBLANES), lambda l: (0, 0)), pl.BlockSpec((None, SUBLANES, n), lambda l: (l, 0, 0))],
        out_specs=pl.BlockSpec((None, D, n), lambda l: (l, 0, 0)),
        out_shape=jax.ShapeDtypeStruct((L, D, n), F32),
        compiler_params=_params(1, 32),
    )(c_act_t, dmod)


def _stats_reduce(g3, loss_row, name):
    n_dev, rows, _ = g3.shape

    def body(g_ref, o_ref, l_ref):
        acc = g_ref[0]
        for d in range(1, n_dev):
            acc = acc + g_ref[d]
        o_ref[...] = acc
        tot = jnp.sum(o_ref[loss_row:loss_row + 1, :], axis=-1, keepdims=True)
        l_ref[...] = jnp.broadcast_to(tot * (0.5 / D), (SUBLANES, LANES))

    return pl.pallas_call(
        body, name=name,
        in_specs=[pl.BlockSpec(memory_space=pltpu.VMEM)],
        out_specs=[pl.BlockSpec(memory_space=pltpu.VMEM), pl.BlockSpec(memory_space=pltpu.VMEM)],
        out_shape=[jax.ShapeDtypeStruct((rows, D), F32), jax.ShapeDtypeStruct((SUBLANES, LANES), F32)],
        compiler_params=pltpu.CompilerParams(vmem_limit_bytes=32 * 2 ** 20),
    )(g3)


def _sum4(own, land, chip, name, tr=256):
    _, R, C = own.shape
    tr = min(tr, R)

    def body(p_ref, own_ref, land_ref, o_ref):
        o_ref[...] = ((own_ref[...].astype(F32) + land_ref[0].astype(F32)) + land_ref[1].astype(F32)) + land_ref[2].astype(F32)

    return pl.pallas_call(
        body, name=name,
        grid_spec=pltpu.PrefetchScalarGridSpec(
            num_scalar_prefetch=1, grid=(R // tr,),
            in_specs=[pl.BlockSpec((None, tr, C), lambda i, p: (p[0], i, 0)), pl.BlockSpec((3, tr, C), lambda i, p: (0, i, 0))],
            out_specs=pl.BlockSpec((tr, C), lambda i, p: (i, 0))),
        out_shape=jax.ShapeDtypeStruct((R, C), F32),
        compiler_params=_params(1, 32),
    )(chip, own, land)


def _adamw(w, m, v, g, name, tr=256):
    L, R, C = w.shape
    tr = min(tr, R)
    stacked = not isinstance(g, (list, tuple))
    n_g = None if stacked else [len(ps) for ps in g]
    flat = [g] if stacked else [a for ps in g for a in ps]

    def body(*refs):
        w_ref, m_ref, v_ref = refs[:3]
        g_refs = refs[3:3 + len(flat)]
        go_ref, d_ref, mo_ref, vo_ref = refs[3 + len(flat):]
        if stacked:
            gg = g_refs[0][...]
        else:
            layer = pl.program_id(0)
            gg = None
            k = 0
            for li in range(L):
                gl = None
                for _ in range(n_g[li]):
                    x = g_refs[k][...]
                    gl = x if gl is None else gl + x
                    k += 1
                gg = gl if gg is None else jnp.where(layer == li, gl, gg)
        m2 = ADAM_B1 * m_ref[...] + (1.0 - ADAM_B1) * gg
        v2 = ADAM_B2 * v_ref[...] + (1.0 - ADAM_B2) * (gg * gg)
        m_hat = m2 / (1.0 - ADAM_B1 ** ADAM_STEP)
        v_hat = v2 / (1.0 - ADAM_B2 ** ADAM_STEP)
        go_ref[...] = gg
        d_ref[...] = -ADAM_LR * (m_hat / (jnp.sqrt(v_hat) + ADAM_EPS) + ADAM_WD * w_ref[...])
        mo_ref[...] = m2
        vo_ref[...] = v2

    big = pl.BlockSpec((None, tr, C), lambda l, i: (l, i, 0))
    g_specs = [big] if stacked else [pl.BlockSpec((tr, C), lambda l, i: (i, 0))] * len(flat)
    return pl.pallas_call(
        body, name=name, grid=(L, R // tr),
        in_specs=[big, big, big] + g_specs,
        out_specs=[big, big, big, big],
        out_shape=[jax.ShapeDtypeStruct((L, R, C), F32)] * 4,
        compiler_params=_params(2, 40),
    )(w, m, v, *flat)


def _allgather8(xs, name):
    m, n = xs.shape

    def body(x_ref, out_ref, send_sems, recv_sems, local_sem):
        x, y, c = _place()
        me, sibling = (x, y, c), (x, y, 1 - c)
        chips = [(1 - x, y), (x, 1 - y), (1 - x, 1 - y)]

        def rows(px, py, pc):
            return out_ref.at[pl.ds((4 * px + 2 * py + pc) * m, m), :]

        def copy(k, block, to, src=None):
            return pltpu.make_async_remote_copy(
                src_ref=rows(*block) if src is None else src, dst_ref=rows(*block),
                send_sem=send_sems.at[k], recv_sem=recv_sems.at[k], device_id=to, device_id_type=MESH)

        mine = pltpu.make_async_copy(x_ref, rows(*me), local_sem)
        mine.start()
        first = [copy(0, me, sibling, src=x_ref)]
        first += [copy(1 + j, me, (*chip, c), src=x_ref) for j, chip in enumerate(chips)]
        for cp in first:
            cp.start()
        passed = [copy(4 + j, (*chip, c), sibling) for j, chip in enumerate(chips)]
        for j, chip in enumerate(chips):
            copy(1 + j, (*chip, c), me).wait_recv()
            passed[j].start()
        copy(0, sibling, me).wait_recv()
        for j, chip in enumerate(chips):
            copy(4 + j, (*chip, 1 - c), me).wait_recv()
        for cp in first + passed:
            cp.wait_send()
        mine.wait()

    return pl.pallas_call(
        body, name=name,
        out_shape=jax.ShapeDtypeStruct((8 * m, n), xs.dtype),
        in_specs=[pl.BlockSpec(memory_space=pltpu.VMEM)],
        out_specs=pl.BlockSpec(memory_space=pltpu.VMEM),
        scratch_shapes=[pltpu.SemaphoreType.DMA((7,)), pltpu.SemaphoreType.DMA((7,)), pltpu.SemaphoreType.DMA],
        compiler_params=pltpu.CompilerParams(vmem_limit_bytes=32 * 2 ** 20),
    )(xs)


def _swap_sibling(parts, name):
    n = len(parts)

    def body(*refs):
        ins, outs = refs[:n], refs[n:2 * n]
        send_sems, recv_sems = refs[2 * n:]
        x, y, c = _place()
        cps = []
        for k in range(n):
            cp = pltpu.make_async_remote_copy(
                src_ref=ins[k], dst_ref=outs[k], send_sem=send_sems.at[k], recv_sem=recv_sems.at[k],
                device_id=(x, y, 1 - c), device_id_type=MESH)
            cp.start()
            cps.append(cp)
        for cp in cps:
            cp.wait_recv()
        for cp in cps:
            cp.wait_send()

    return pl.pallas_call(
        body, name=name,
        out_shape=[jax.ShapeDtypeStruct(a.shape, a.dtype) for a in parts],
        in_specs=[ANY] * n, out_specs=[ANY] * n,
        scratch_shapes=[pltpu.SemaphoreType.DMA((n,)), pltpu.SemaphoreType.DMA((n,))],
    )(*parts)


def _pad8(a):
    return jnp.pad(a, ((0, SUBLANES - a.shape[0]), (0, 0)))


def _group_rows(wg):
    return wg.transpose(1, 0, 2, 3).reshape(4, GW, GW)


def _example_step(h0, tgt, mods, kvmod, a_scale, norm_g, kv_norm_g, final_g, b_rel_bias, sh):
    ones_e = jnp.ones((1, E), F32)
    shift = [mods[l:l + 1, 0:D] for l in range(4)]
    scale = [mods[l:l + 1, D:2 * D] for l in range(4)]
    gate = [mods[l:l + 1, 2 * D:3 * D] for l in range(4)]
    gl = [norm_g[l:l + 1] for l in range(4)]
    kv_shift, kv_scale = kvmod[None, 0:D], kvmod[None, D:2 * D]
    kv_g = kv_norm_g[None]

    w_a = _comm_only(_Comm(gathers=[sh["a_in"][0], sh["a_grp"][0], sh["a_out"][0]]), "gather_first")
    hs = [h0]
    saved = []
    nxt = [[sh["a_in"][1], sh["a_grp"][1], sh["a_out"][1]], [sh["kv"][0], sh["b_in"][0]]]
    for l in range(2):
        w_in_l, wg_l, wo_l = w_a
        wg_full = _group_rows(wg_l)
        (u, z, pooled, mixed, y, hn), got = _a_fwd(hs[-1], gl[l], shift[l], scale[l], a_scale[l:l + 1], gate[l], w_in_l,
                                                   wg_full, wo_l, f"a{l}_fwd", comm=_Comm(gathers=nxt[l]))
        saved.append((u, z, pooled, mixed, y, w_in_l, wg_full, wo_l))
        hs.append(hn)
        if l == 0:
            w_a = got
        else:
            w_kv, wb_in0 = got

    (uk, kp, vp), (wb_out0, wb_in1) = _in_fwd(hs[2], kv_g, kv_shift, kv_scale, w_kv, BF16, BF16, "kv_in_fwd", pad_rows=PAD,
                                              comm=_Comm(gathers=[sh["b_out"][0], sh["b_in"][1]]))
    wb_in = [wb_in0, wb_in1]
    wb_out = [wb_out0, None]

    strips = []
    for bi in range(2):
        l = 2 + bi
        sa, sb = _bias_build(jnp.pad(b_rel_bias[bi], ((0, 0), (0, NRELP - NREL))), f"b{bi}_bias")
        sa, sb = sa.transpose(1, 0, 2), sb.transpose(1, 0, 2)
        strips.append((sa, sb))
        (u, q, z), got = _in_fwd(hs[-1], gl[l], shift[l], scale[l], wb_in[bi], BF16, BF16, f"b{bi}_in_fwd",
                                 comm=_Comm(gathers=[sh["b_out"][1]]) if bi == 0 else None)
        if bi == 0:
            (wb_out[1],) = got
        (att,), _ = _attn_fwd(q, kp, vp, sa, sb, f"b{bi}_attn_fwd")
        (y, hn), _ = _out_fwd(att, z, wb_out[bi], gate[l], hs[-1], f"b{bi}_out_fwd")
        saved.append((u, z, q, att, y))
        hs.append(hn)

    dh, st_fin = _final(hs[4], final_g[None], tgt, "final")

    st_in = [None] * 4
    st_out = [None] * 4
    grads = {}
    landed = {}

    def carry(names):
        return _Comm(scatters=[grads[n] for n in names]) if names else None

    def land(names, got):
        for n, a in zip(names, got):
            landed[n] = a

    u, z, q, att, y = saved[3]
    sa, sb = strips[1]
    (datt, dz, grads["b_out1"], st_out[3]), _ = _out_bwd(dh, y, gate[3], att, ones_e, z, wb_out[1], "b1_out_bwd")
    (dq, dk1, dv1, dsa, dsb), got = _attn_bwd(q, kp, vp, sa, sb, datt, None, "b1_attn_bwd", comm=carry(["b_out1"]))
    land(["b_out1"], got)
    drb1 = _dbias_reduce(dsa.transpose(1, 0, 2), dsb.transpose(1, 0, 2), "b1_dbias")
    (dh, grads["b_in1"], st_in[3]), _ = _in_bwd(dq, dz, 0, u, hs[3], gl[3], scale[3], wb_in[1], dh, "b1_in_bwd")
    u, z, q, att, y = saved[2]
    sa, sb = strips[0]
    (datt, dz, grads["b_out0"], st_out[2]), _ = _out_bwd(dh, y, gate[2], att, ones_e, z, wb_out[0], "b0_out_bwd")
    (dq, dk, dv, dsa, dsb), got = _attn_bwd(q, kp, vp, sa, sb, datt, (dk1, dv1), "b0_attn_bwd",
                                            comm=carry(["b_in1", "b_out0"]))
    land(["b_in1", "b_out0"], got)
    drb0 = _dbias_reduce(dsa.transpose(1, 0, 2), dsb.transpose(1, 0, 2), "b0_dbias")
    (dh, grads["b_in0"], st_in[2]), _ = _in_bwd(dq, dz, 0, u, hs[2], gl[2], scale[2], wb_in[0], dh, "b0_in_bwd")
    (dh, grads["kv"], st_kv), got = _in_bwd(dk, dv, PAD, uk, hs[2], kv_g, kv_scale, w_kv, dh, "kv_in_bwd",
                                            comm=carry(["b_in0"]))
    land(["b_in0"], got)
    st_pool = [None] * 2
    plan = {1: dict(o=[], p=["a_out1"], i=["kv", "a_grp1"]), 0: dict(o=["a_in1"], p=["a_out0"], i=["a_grp0"])}
    for l in (1, 0):
        u, z, pooled, mixed, y, w_in_l, wg_full, wo = saved[l]
        asl = a_scale[l:l + 1]
        (dms, dz, grads[f"a_out{l}"], st_out[l]), got = _out_bwd(dh, y, gate[l], mixed, asl, z, wo, f"a{l}_out_bwd",
                                                                comm=carry(plan[l]["o"]))
        land(plan[l]["o"], got)
        (dval, dwg, st_pool[l]), got = _pool_bwd(dms, mixed, pooled, wg_full, asl, f"a{l}_pool_bwd", comm=carry(plan[l]["p"]))
        land(plan[l]["p"], got)
        grads[f"a_grp{l}"] = (dwg.reshape(4, NCHIP, GW // NCHIP, GW).transpose(1, 0, 2, 3).reshape(NCHIP, GW, GW)
                              .astype(BF16))
        (dh, grads[f"a_in{l}"], st_in[l]), got = _in_bwd(dval, dz, 0, u, hs[l], gl[l], scale[l], w_in_l, dh, f"a{l}_in_bwd",
                                                         comm=carry(plan[l]["i"]))
        land(plan[l]["i"], got)
    land(["a_in0"], _comm_only(carry(["a_in0"]), "scatter_last"))

    pieces = st_in + [st_kv] + st_out + [st_fin]
    pieces += [_pad8(st_pool[l][0].reshape(2, D)) for l in range(2)]
    pieces += [_pad8(d.reshape(NH * NRELP // D, D)) for d in (drb0, drb1)]
    stats = jnp.concatenate(pieces, axis=0)
    return dh, grads, landed, stats


ROW_IN = [8 * l for l in range(4)]
ROW_KV = 32
ROW_OUT = [40 + 8 * l for l in range(4)]
ROW_FIN = 72
ROW_ASC = [80, 88]
ROW_RB = [96, 104]
N_STAT = 112


def kernel(x, c, ada_w, ada_b, norm_g, a_w_in, a_w_group, a_scale, a_w_out, kv_norm_g, kv_ada_w, kv_ada_b, w_kv, b_w_in, b_rel_bias, b_w_out, final_g, loss_target, m_ada_w, m_ada_b, m_norm_g, m_a_w_in, m_a_w_group, m_a_scale, m_a_w_out, m_kv_norm_g, m_kv_ada_w, m_kv_ada_b, m_w_kv, m_b_w_in, m_b_rel_bias, m_b_w_out, m_final_g, v_ada_w, v_ada_b, v_norm_g, v_a_w_in, v_a_w_group, v_a_scale, v_a_w_out, v_kv_norm_g, v_kv_ada_w, v_kv_ada_b, v_w_kv, v_b_w_in, v_b_rel_bias, v_b_w_out, v_final_g):
    xi, yi, ci = _place()
    chip = 2 * xi + yi
    dev = 4 * xi + 2 * yi + ci
    n_ada = ada_w.shape[2]
    n_kva = kv_ada_w.shape[1]
    n_asc = a_scale.shape[1]

    c_all = _allgather8(jnp.broadcast_to(c, (SUBLANES, D)), "gather_c")[::SUBLANES]
    ada_b_sh = lax.dynamic_slice_in_dim(ada_b, chip * n_ada, n_ada, axis=1)
    kvb_sh = lax.dynamic_slice_in_dim(kv_ada_b, chip * n_kva, n_kva, axis=0)
    c_act, mod_ada = _cmat(c_all, ada_w, ada_b_sh[:, None, :], "mod_ada")
    _, mod_kv = _cmat(c_all, kv_ada_w[None], kvb_sh[None, None, :], "mod_kv")
    part = jnp.concatenate([mod_ada.transpose(1, 0, 2).reshape(SUBLANES, 4 * n_ada), mod_kv[0],
                            jnp.broadcast_to(a_scale.reshape(1, 2 * n_asc), (SUBLANES, 2 * n_asc))], axis=1)
    gathered = _allgather8(part, "gather_mod")
    rows = jnp.concatenate([lax.dynamic_slice_in_dim(gathered, SUBLANES * (2 * p + ci) + dev, 1, axis=0)
                            for p in range(NCHIP)], axis=0)
    mods = jnp.stack([rows[:, l * n_ada:(l + 1) * n_ada].reshape(3 * D) for l in range(4)])
    kvmod = rows[:, 4 * n_ada:4 * n_ada + n_kva].reshape(2 * D)
    o_asc = 4 * n_ada + n_kva
    a_scale_full = jnp.stack([rows[:, o_asc + l * n_asc:o_asc + (l + 1) * n_asc].reshape(E) for l in range(2)])

    sh = dict(a_in=[a_w_in[l].astype(BF16) for l in range(2)], a_grp=[a_w_group[l].astype(BF16) for l in range(2)],
              a_out=[a_w_out[l].astype(BF16) for l in range(2)], kv=[w_kv.astype(BF16)],
              b_in=[b_w_in[l].astype(BF16) for l in range(2)], b_out=[b_w_out[l].astype(BF16) for l in range(2)])
    dh, grads, landed, stats = _example_step(x[0], loss_target[0], mods, kvmod, a_scale_full, norm_g, kv_norm_g, final_g,
                                             b_rel_bias, sh)
    grad_x = dh[None]

    g3 = _allgather8(stats, "gather_stats").reshape(8, N_STAT, D)
    red, loss_tile = _stats_reduce(g3, ROW_FIN + 1, "stats_reduce")
    loss = loss_tile[0, 0]

    def cat(rows_):
        return jnp.concatenate(rows_, axis=-1)

    g_ada_b = jnp.stack([cat([red[ROW_IN[l]], red[ROW_IN[l] + 1], red[ROW_OUT[l]]]) for l in range(4)])
    g_norm_g = jnp.stack([red[ROW_IN[l] + 2] for l in range(4)])
    g_kv_norm_g = red[ROW_KV + 2]
    g_kv_ada_b = cat([red[ROW_KV], red[ROW_KV + 1]])
    g_final_g = red[ROW_FIN]
    g_asc_full = jnp.stack([red[ROW_ASC[l]:ROW_ASC[l] + 2].reshape(E) for l in range(2)])
    g_a_scale = lax.dynamic_slice_in_dim(g_asc_full, chip * n_asc, n_asc, axis=1)
    g_rel = jnp.stack([red[ROW_RB[bi]:ROW_RB[bi] + NH * NRELP // D].reshape(NH, NRELP)[:, :NREL] for bi in range(2)])

    dmod = jnp.stack([cat([g3[:, ROW_IN[l]], g3[:, ROW_IN[l] + 1], g3[:, ROW_OUT[l]]]) for l in range(4)])
    dmod_sh = lax.dynamic_slice_in_dim(dmod, chip * n_ada, n_ada, axis=2)
    dkv = cat([g3[:, ROW_KV], g3[:, ROW_KV + 1]])[None]
    dkv_sh = lax.dynamic_slice_in_dim(dkv, chip * n_kva, n_kva, axis=2)
    c_act_t = c_act.T
    g_ada_w = _grad_ada(c_act_t, dmod_sh, "grad_ada_w")
    g_kv_ada_w = _grad_ada(c_act_t, dkv_sh, "grad_kv_ada_w")

    order = ["a_in0", "a_in1", "a_grp0", "a_grp1", "a_out0", "a_out1", "kv", "b_in0", "b_in1", "b_out0", "b_out1"]
    chip_arr = jnp.reshape(chip, (1,)).astype(jnp.int32)
    partial = [_sum4(grads[n], landed[n], chip_arr, f"sum4_{n}") for n in order]
    other = _swap_sibling(partial, "swap_sibling")
    both = {n: [partial[k], other[k]] for k, n in enumerate(order)}

    def upd(w, m, v, g, name, shape3):
        g = g.reshape(shape3) if not isinstance(g, list) else g
        outs = _adamw(w.reshape(shape3), m.reshape(shape3), v.reshape(shape3), g, name)
        return [o.reshape(w.shape) for o in outs]

    def pair(name):
        return [both[name + "0"], both[name + "1"]]

    res = {}
    res["ada_w"] = upd(ada_w, m_ada_w, v_ada_w, g_ada_w, "adamw_ada_w", ada_w.shape)
    res["ada_b"] = upd(ada_b, m_ada_b, v_ada_b, g_ada_b, "adamw_ada_b", (1,) + ada_b.shape)
    res["norm_g"] = upd(norm_g, m_norm_g, v_norm_g, g_norm_g, "adamw_norm_g", (1,) + norm_g.shape)
    res["a_w_in"] = upd(a_w_in, m_a_w_in, v_a_w_in, pair("a_in"), "adamw_a_w_in", a_w_in.shape)
    res["a_w_group"] = upd(a_w_group, m_a_w_group, v_a_w_group, pair("a_grp"), "adamw_a_w_group", (2, GW, GW))
    res["a_scale"] = upd(a_scale, m_a_scale, v_a_scale, g_a_scale, "adamw_a_scale", (1,) + a_scale.shape)
    res["a_w_out"] = upd(a_w_out, m_a_w_out, v_a_w_out, pair("a_out"), "adamw_a_w_out", a_w_out.shape)
    res["kv_norm_g"] = upd(kv_norm_g, m_kv_norm_g, v_kv_norm_g, g_kv_norm_g, "adamw_kv_norm_g", (1, 1, D))
    res["kv_ada_w"] = upd(kv_ada_w, m_kv_ada_w, v_kv_ada_w, g_kv_ada_w, "adamw_kv_ada_w", (1,) + kv_ada_w.shape)
    res["kv_ada_b"] = upd(kv_ada_b, m_kv_ada_b, v_kv_ada_b, g_kv_ada_b, "adamw_kv_ada_b", (1, 1, 2 * D))
    res["w_kv"] = upd(w_kv, m_w_kv, v_w_kv, [both["kv"]], "adamw_w_kv", (1,) + w_kv.shape)
    res["b_w_in"] = upd(b_w_in, m_b_w_in, v_b_w_in, pair("b_in"), "adamw_b_w_in", b_w_in.shape)
    res["b_rel_bias"] = upd(b_rel_bias, m_b_rel_bias, v_b_rel_bias, g_rel, "adamw_b_rel_bias", (1, 2 * NH, NREL))
    res["b_w_out"] = upd(b_w_out, m_b_w_out, v_b_w_out, pair("b_out"), "adamw_b_w_out", b_w_out.shape)
    res["final_g"] = upd(final_g, m_final_g, v_final_g, g_final_g, "adamw_final_g", (1, 1, D))

    names = ["ada_w", "ada_b", "norm_g", "a_w_in", "a_w_group", "a_scale", "a_w_out", "kv_norm_g", "kv_ada_w", "kv_ada_b",
             "w_kv", "b_w_in", "b_rel_bias", "b_w_out", "final_g"]
    return (loss, grad_x, *[res[n][0] for n in names], *[res[n][1] for n in names], *[res[n][2] for n in names],
            *[res[n][3] for n in names])
```

```python
import math

import jax
import jax.numpy as jnp
from jax import lax
from jax.experimental import pallas as pl
from jax.experimental.pallas import tpu as pltpu

F32 = jnp.float32
BF16 = jnp.bfloat16

D = 1024
E = 2048
NH = 16
HD = 128
CHUNK = 64
LEFT = 8
PAD = LEFT * CHUNK
NREL = 257
NRELP = 384
REL_CLIP = 128
EPS = 1e-6
NEG = -1e30
LOG2E = math.log2(math.e)
SM_SCALE = HD ** -0.5
POOL_W = (2, 4, 8, 16)
GW = 512
HALO = 16
QC = 4
QB = QC * CHUNK
WIN = (QC + LEFT) * CHUNK
BW = (LEFT + 2) * CHUNK
DBW = 4 * CHUNK
NSUB = 8
NCHIP = 4
LANES = 128
SUBLANES = 8

ADAM_LR = 0.001
ADAM_B1 = 0.9
ADAM_B2 = 0.999
ADAM_EPS = 1e-08
ADAM_WD = 0.01
ADAM_STEP = 10

MESH = pl.DeviceIdType.MESH
ANY = pl.BlockSpec(memory_space=pl.ANY)


def _params(n_axes, vmem_mb):
    return pltpu.CompilerParams(dimension_semantics=("arbitrary",) * n_axes, vmem_limit_bytes=vmem_mb * 2 ** 20)


def _nn(a, b):
    return jnp.dot(a, b, preferred_element_type=F32)


def _nt(a, b):
    return lax.dot_general(a, b, (((1,), (1,)), ((), ())), preferred_element_type=F32)


def _tn(a, b):
    return lax.dot_general(a, b, (((0,), (0,)), ((), ())), preferred_element_type=F32)


def _row(n):
    return pl.BlockSpec((1, n), lambda i: (0, 0))


def _colsum(x):
    return jnp.sum(x, axis=0, keepdims=True)


def _place():
    return lax.axis_index("x"), lax.axis_index("y"), lax.axis_index("c")


class _Comm:
    def __init__(self, gathers=(), scatters=(), swaps=()):
        self.n_g = len(gathers)
        self.n_chip = len(gathers) + len(scatters)
        self.n_sw = len(swaps)
        self.arrays = list(gathers) + list(scatters) + list(swaps)
        self.n = len(self.arrays)
        self.half = [a.shape[0] // 2 for a in gathers]
        self.out_shape = ([jax.ShapeDtypeStruct((NCHIP,) + a.shape, a.dtype) for a in gathers]
                          + [jax.ShapeDtypeStruct((3,) + a.shape[1:], a.dtype) for a in scatters]
                          + [jax.ShapeDtypeStruct(a.shape, a.dtype) for a in swaps])
        n_c, n_f, n_s = max(3 * self.n_chip, 1), max(3 * self.n_g, 1), max(self.n_sw, 1)
        self.scratch = [pltpu.SemaphoreType.DMA((n_c,)), pltpu.SemaphoreType.DMA((n_c,)),
                        pltpu.SemaphoreType.DMA((max(self.n_g, 1),)), pltpu.SemaphoreType.DMA((n_f,)),
                        pltpu.SemaphoreType.DMA((n_f,)), pltpu.SemaphoreType.DMA((n_s,)), pltpu.SemaphoreType.DMA((n_s,))]

    def _chip_copies(self, ins, outs, send, recv, landing):
        x, y, c = _place()
        chips = [(1 - x, y), (x, 1 - y), (1 - x, 1 - y)]
        mine = 2 * x + y
        cps = []
        for k in range(self.n_chip):
            for j, (cx, cy) in enumerate(chips):
                q = 2 * cx + cy
                if k < self.n_g:
                    part = pl.ds(c * self.half[k], self.half[k])
                    src = ins[k].at[part]
                    dst = outs[k].at[q if landing else mine, part]
                else:
                    src = ins[k].at[q]
                    dst = outs[k].at[j]
                cps.append(pltpu.make_async_remote_copy(
                    src_ref=src, dst_ref=dst, send_sem=send.at[3 * k + j], recv_sem=recv.at[3 * k + j],
                    device_id=(cx, cy, c), device_id_type=MESH))
        return cps

    def _core_copies(self, outs, fsend, frecv, landing):
        x, y, c = _place()
        chips = [(1 - x, y), (x, 1 - y), (1 - x, 1 - y)]
        cps = []
        for k in range(self.n_g):
            for j, (cx, cy) in enumerate(chips):
                part = pl.ds((1 - c if landing else c) * self.half[k], self.half[k])
                blk = outs[k].at[2 * cx + cy, part]
                cps.append(pltpu.make_async_remote_copy(
                    src_ref=blk, dst_ref=blk, send_sem=fsend.at[3 * k + j], recv_sem=frecv.at[3 * k + j],
                    device_id=(x, y, 1 - c), device_id_type=MESH))
        return cps

    def _local_copies(self, ins, outs, loc):
        x, y, _ = _place()
        return [pltpu.make_async_copy(ins[k], outs[k].at[2 * x + y], loc.at[k]) for k in range(self.n_g)]

    def _swap_copies(self, ins, outs, ssend, srecv):
        x, y, c = _place()
        return [pltpu.make_async_remote_copy(
            src_ref=ins[k], dst_ref=outs[k], send_sem=ssend.at[k - self.n_chip], recv_sem=srecv.at[k - self.n_chip],
            device_id=(x, y, 1 - c), device_id_type=MESH) for k in range(self.n_chip, self.n)]

    def start(self, ins, outs, send, recv, loc, fsend, frecv, ssend, srecv):
        for cp in (self._local_copies(ins, outs, loc) + self._chip_copies(ins, outs, send, recv, False)
                   + self._swap_copies(ins, outs, ssend, srecv)):
            cp.start()

    def wait(self, ins, outs, send, recv, loc, fsend, frecv, ssend, srecv):
        lands = self._chip_copies(ins, outs, send, recv, True)
        passes = self._core_copies(outs, fsend, frecv, False)
        for k in range(self.n_chip):
            for j in range(3):
                lands[3 * k + j].wait_recv()
                if k < self.n_g:
                    passes[3 * k + j].start()
        for cp in self._core_copies(outs, fsend, frecv, True):
            cp.wait_recv()
        swaps = self._swap_copies(ins, outs, ssend, srecv)
        for cp in swaps:
            cp.wait_recv()
        for cp in self._chip_copies(ins, outs, send, recv, False) + passes + swaps:
            cp.wait_send()
        for cp in self._local_copies(ins, outs, loc):
            cp.wait()


def _call(body, name, grid, in_specs, out_specs, out_shape, scratch, params, args, comm=None):
    n_in, n_out, n_sc = len(in_specs), len(out_specs), len(scratch)
    if comm is None:
        outs = pl.pallas_call(body, name=name, grid=grid, in_specs=in_specs, out_specs=out_specs, out_shape=out_shape,
                              scratch_shapes=scratch, compiler_params=params)(*args)
        return list(outs), []
    n = comm.n
    o0 = n_in + n
    s0 = o0 + n_out + n

    def wrapped(*refs):
        c_refs = (refs[n_in:o0], refs[o0 + n_out:s0]) + tuple(refs[s0 + n_sc:])
        ids = [pl.program_id(a) for a in range(len(grid))]
        first = ids[0] == 0
        last = ids[0] == grid[0] - 1
        for a in range(1, len(grid)):
            first = first & (ids[a] == 0)
            last = last & (ids[a] == grid[a] - 1)

        @pl.when(first)
        def _():
            comm.start(*c_refs)

        body(*refs[:n_in], *refs[o0:o0 + n_out], *refs[s0:s0 + n_sc])

        @pl.when(last)
        def _():
            comm.wait(*c_refs)

    outs = pl.pallas_call(
        wrapped, name=name, grid=grid, in_specs=list(in_specs) + [ANY] * n, out_specs=list(out_specs) + [ANY] * n,
        out_shape=list(out_shape) + comm.out_shape, scratch_shapes=list(scratch) + comm.scratch, compiler_params=params,
    )(*args, *comm.arrays)
    return list(outs[:n_out]), list(outs[n_out:])


def _comm_only(comm, name):
    def body(*refs):
        c_refs = (refs[:comm.n], refs[comm.n:2 * comm.n]) + tuple(refs[2 * comm.n:])
        comm.start(*c_refs)
        comm.wait(*c_refs)

    return pl.pallas_call(body, name=name, in_specs=[ANY] * comm.n, out_specs=[ANY] * comm.n, out_shape=comm.out_shape,
                          scratch_shapes=comm.scratch)(*comm.arrays)


def _in_fwd(h, g, shift, scale, w, dt_a, dt_b, name, pad_rows=0, comm=None, tm=512):
    S = h.shape[0]
    n_pad = pad_rows // tm

    def body(h_ref, g_ref, sh_ref, sc_ref, w_hbm, u_ref, oa_ref, ob_ref, w_v, sem):
        i = pl.program_id(0)

        @pl.when(i == 0)
        def _():
            cp = pltpu.make_async_copy(w_hbm, w_v, sem)
            cp.start()
            cp.wait()

        hh = h_ref[...]
        r = lax.rsqrt(jnp.mean(hh * hh, axis=-1, keepdims=True) + EPS)
        u = (hh * r * g_ref[...]) * (1.0 + sc_ref[...]) + sh_ref[...]
        ub = u.astype(BF16)
        u_ref[...] = ub
        for q in range(NCHIP):
            o_ref = oa_ref if q < 2 else ob_ref
            o_ref[:, (q % 2) * D:(q % 2 + 1) * D] = _nn(ub, w_v[q]).astype(o_ref.dtype)

        if n_pad:
            @pl.when(i < n_pad)
            def _():
                oa_ref[...] = jnp.zeros(oa_ref.shape, oa_ref.dtype)
                ob_ref[...] = jnp.zeros(ob_ref.shape, ob_ref.dtype)

    def src(i):
        return (jnp.maximum(i - n_pad, 0), 0)

    outs, landed = _call(
        body, name, (S // tm + n_pad,),
        [pl.BlockSpec((tm, D), src), _row(D), _row(D), _row(D), ANY],
        [pl.BlockSpec((tm, D), src), pl.BlockSpec((tm, E), lambda i: (i, 0)), pl.BlockSpec((tm, E), lambda i: (i, 0))],
        [jax.ShapeDtypeStruct((S, D), BF16), jax.ShapeDtypeStruct((S + pad_rows, E), dt_a),
         jax.ShapeDtypeStruct((S + pad_rows, E), dt_b)],
        [pltpu.VMEM((NCHIP, D, D), BF16), pltpu.SemaphoreType.DMA],
        _params(1, 52), (h, g, shift, scale, w), comm)
    return outs, landed


def _a_fwd(h, g, shift, scale, asc, gate, w_in, wg, w_out, name, comm=None, tm=512):
    S = h.shape[0]

    def body(h_ref, g_ref, sh_ref, sc_ref, as_ref, gate_ref, wi_hbm, wg_hbm, wo_hbm,
             u_ref, z_ref, p_ref, m_ref, y_ref, ho_ref, wi_v, wg_v, wo_v, buf, sems):
        i = pl.program_id(0)

        @pl.when(i == 0)
        def _():
            cps = [pltpu.make_async_copy(wi_hbm, wi_v, sems.at[0]), pltpu.make_async_copy(wg_hbm, wg_v, sems.at[1]),
                   pltpu.make_async_copy(wo_hbm, wo_v, sems.at[2])]
            for cp in cps:
                cp.start()
            buf[0:HALO, :] = jnp.zeros((HALO, E), F32)
            for cp in cps:
                cp.wait()

        hh = h_ref[...]
        r = lax.rsqrt(jnp.mean(hh * hh, axis=-1, keepdims=True) + EPS)
        ub = ((hh * r * g_ref[...]) * (1.0 + sc_ref[...]) + sh_ref[...]).astype(BF16)
        u_ref[...] = ub
        for q in range(2):
            buf[HALO:HALO + tm, q * D:(q + 1) * D] = _nn(ub, wi_v[q])
        t = i * tm + lax.broadcasted_iota(jnp.int32, (tm, 1), 0)
        y = None
        for gi, w in enumerate(POOL_W):
            cols = slice(gi * GW, (gi + 1) * GW)
            x = buf[:, cols]
            s = x
            k = 1
            while k < w:
                s = s + pltpu.roll(s, k, 0)
                k *= 2
            cnt = jnp.minimum(t + 1, w).astype(F32)
            pb = (s[HALO:, :] / cnt - x[HALO:, :]).astype(BF16)
            p_ref[:, cols] = pb
            mb = _nn(pb, wg_v[gi]).astype(BF16)
            m_ref[:, cols] = mb
            zb = _nn(ub, wi_v[2 + gi // 2, :, (gi % 2) * GW:(gi % 2 + 1) * GW]).astype(BF16)
            z_ref[:, cols] = zb
            zz = zb.astype(F32)
            act = ((mb.astype(F32) * as_ref[:, cols]) * (zz * jax.nn.sigmoid(zz))).astype(BF16)
            part = _nn(act, wo_v[gi])
            y = part if y is None else y + part
        buf[0:HALO, :] = buf[tm:tm + HALO, :]
        y_ref[...] = y.astype(BF16)
        ho_ref[...] = hh + gate_ref[...] * y

    rows_d = pl.BlockSpec((tm, D), lambda i: (i, 0))
    rows_e = pl.BlockSpec((tm, E), lambda i: (i, 0))
    return _call(
        body, name, (S // tm,),
        [rows_d, _row(D), _row(D), _row(D), _row(E), _row(D), ANY, ANY, ANY],
        [rows_d, rows_e, rows_e, rows_e, rows_d, rows_d],
        [jax.ShapeDtypeStruct((S, D), BF16), jax.ShapeDtypeStruct((S, E), BF16), jax.ShapeDtypeStruct((S, E), BF16),
         jax.ShapeDtypeStruct((S, E), BF16), jax.ShapeDtypeStruct((S, D), BF16), jax.ShapeDtypeStruct((S, D), F32)],
        [pltpu.VMEM((NCHIP, D, D), BF16), pltpu.VMEM((4, GW, GW), BF16), pltpu.VMEM((NCHIP, GW, D), BF16),
         pltpu.VMEM((tm + HALO, E), F32), pltpu.SemaphoreType.DMA((3,))],
        _params(1, 60), (h, g, shift, scale, asc, gate, w_in, wg, w_out), comm)


def _out_fwd(a, z, w, gate, h, name, comm=None, tm=512):
    S = h.shape[0]
    kb = E // NCHIP

    def body(a_ref, z_ref, w_hbm, gate_ref, h_ref, y_ref, ho_ref, w_v, sem):
        @pl.when(pl.program_id(0) == 0)
        def _():
            cp = pltpu.make_async_copy(w_hbm, w_v, sem)
            cp.start()
            cp.wait()

        y = None
        for p in range(NCHIP):
            cols = slice(p * kb, (p + 1) * kb)
            zz = z_ref[:, cols].astype(F32)
            act = (a_ref[:, cols].astype(F32) * (zz * jax.nn.sigmoid(zz))).astype(BF16)
            part = _nn(act, w_v[p])
            y = part if y is None else y + part
        y_ref[...] = y.astype(BF16)
        ho_ref[...] = h_ref[...] + gate_ref[...] * y

    return _call(
        body, name, (S // tm,),
        [pl.BlockSpec((tm, E), lambda i: (i, 0)), pl.BlockSpec((tm, E), lambda i: (i, 0)), ANY, _row(D),
         pl.BlockSpec((tm, D), lambda i: (i, 0))],
        [pl.BlockSpec((tm, D), lambda i: (i, 0)), pl.BlockSpec((tm, D), lambda i: (i, 0))],
        [jax.ShapeDtypeStruct((S, D), BF16), jax.ShapeDtypeStruct((S, D), F32)],
        [pltpu.VMEM((NCHIP, kb, D), BF16), pltpu.SemaphoreType.DMA],
        _params(1, 52), (a, z, w, gate, h), comm)


TW = BW + LANES


def _diag_onehot(transpose):
    shape = (TW, NRELP) if transpose else (NRELP, TW)
    j = lax.broadcasted_iota(jnp.int32, shape, 0 if transpose else 1)
    r = lax.broadcasted_iota(jnp.int32, shape, 1 if transpose else 0)
    idx = jnp.clip(PAD - (j - LANES), -REL_CLIP, REL_CLIP) + REL_CLIP
    return jnp.where(idx == r, 1.0, 0.0).astype(BF16)


def _strip_valid():
    m = lax.broadcasted_iota(jnp.int32, (NH, BW), 1)
    return m < (LEFT + 1) * CHUNK, m >= CHUNK


def _bias_build(rb, name):
    def body(rb_ref, a_ref, b_ref):
        x = rb_ref[...]
        hi = x.astype(BF16)
        r1 = x - hi.astype(F32)
        mid = r1.astype(BF16)
        lo = (r1 - mid.astype(F32)).astype(BF16)
        oh = _diag_onehot(False)
        diag = (_nn(hi, oh) + _nn(mid, oh)) + _nn(lo, oh)
        valid_a, valid_b = _strip_valid()
        for qi in range(CHUNK):
            a_ref[qi] = jnp.where(valid_a, pltpu.roll(diag, TW - (LANES - qi), 1)[:, :BW], NEG)
            b_ref[qi] = jnp.where(valid_b, pltpu.roll(diag, TW - (CHUNK - qi), 1)[:, :BW], NEG)

    vmem = pl.BlockSpec(memory_space=pltpu.VMEM)
    return pl.pallas_call(
        body, name=name, in_specs=[vmem], out_specs=[vmem, vmem],
        out_shape=[jax.ShapeDtypeStruct((CHUNK, NH, BW), F32), jax.ShapeDtypeStruct((CHUNK, NH, BW), F32)],
        compiler_params=pltpu.CompilerParams(vmem_limit_bytes=32 * 2 ** 20),
    )(rb)


def _dbias_reduce(dba, dbb, name):
    def body(a_ref, b_ref, o_ref):
        valid_a, valid_b = _strip_valid()
        zeros = jnp.zeros((NH, TW - BW), F32)
        acc = jnp.zeros((NH, TW), F32)
        for qi in range(CHUNK):
            xa = jnp.concatenate([jnp.where(valid_a, a_ref[qi], 0.0), zeros], axis=1)
            xb = jnp.concatenate([jnp.where(valid_b, b_ref[qi], 0.0), zeros], axis=1)
            acc = acc + (pltpu.roll(xa, LANES - qi, 1) + pltpu.roll(xb, CHUNK - qi, 1))
        oh = _diag_onehot(True)
        hi = acc.astype(BF16)
        mid = (acc - hi.astype(F32)).astype(BF16)
        r = lax.broadcasted_iota(jnp.int32, (NH, NRELP), 1)
        near = jnp.where(r < 2 * REL_CLIP, _nn(hi, oh) + _nn(mid, oh), 0.0)
        o_ref[...] = jnp.where(r == 2 * REL_CLIP, -jnp.sum(near, axis=-1, keepdims=True), near)

    vmem = pl.BlockSpec(memory_space=pltpu.VMEM)
    return pl.pallas_call(
        body, name=name, in_specs=[vmem, vmem], out_specs=vmem,
        out_shape=jax.ShapeDtypeStruct((NH, NRELP), F32),
        compiler_params=pltpu.CompilerParams(vmem_limit_bytes=32 * 2 ** 20),
    )(dba, dbb)


def _build_bias(bias3, ba_ref, bb_ref):
    bias3[2] = jnp.full((QB, WIN), NEG, F32)
    for qc in range(QC):
        rows = slice(qc * CHUNK, (qc + 1) * CHUNK)
        if qc % 2 == 0:
            bias3[2, rows, qc * CHUNK:qc * CHUNK + BW] = ba_ref[...] * LOG2E
        else:
            bias3[2, rows, (qc - 1) * CHUNK:(qc - 1) * CHUNK + BW] = bb_ref[...] * LOG2E
    col = lax.broadcasted_iota(jnp.int32, (QB, WIN), 1)
    for sub in range(PAD // QB):
        bias3[sub] = jnp.where(col < PAD - sub * QB, NEG, bias3[2])


def _nsub(S):
    n = min(NSUB, S // QB)
    assert S % (n * QB) == 0 and n >= PAD // QB
    return n


def _row0(i, sub, nsub):
    return pl.multiple_of((i * nsub + sub) * QB, QB)


def _scores(q_ref, k_ref, i, sub, nsub):
    return _nt(q_ref[sub * QB:(sub + 1) * QB, :], k_ref[pl.ds(_row0(i, sub, nsub), WIN), :])


def _exp_parts(s, bias3, i, sub):
    which = jnp.where(i == 0, sub, 2) if sub < PAD // QB else 2
    s = s * (SM_SCALE * LOG2E) + bias3[which]
    e = jnp.exp2(s - jnp.max(s, axis=-1, keepdims=True))
    return e, jnp.sum(e, axis=-1, keepdims=True)


def _attn_fwd(q, kp, vp, ba, bb, name, comm=None):
    S = q.shape[0]
    nsub = _nsub(S)
    R = nsub * QB

    def body(q_ref, k_ref, v_ref, ba_ref, bb_ref, o_ref, bias3):
        i = pl.program_id(1)

        @pl.when(i == 0)
        def _():
            _build_bias(bias3, ba_ref, bb_ref)

        s_next = _scores(q_ref, k_ref, i, 0, nsub)
        for sub in range(nsub):
            s = s_next
            if sub + 1 < nsub:
                s_next = _scores(q_ref, k_ref, i, sub + 1, nsub)
            e, l = _exp_parts(s, bias3, i, sub)
            o = _nn(e.astype(BF16), v_ref[pl.ds(_row0(i, sub, nsub), WIN), :])
            o_ref[sub * QB:(sub + 1) * QB, :] = (o / l).astype(BF16)

    return _call(
        body, name, (NH, S // R),
        [pl.BlockSpec((R, HD), lambda h, i: (i, h)), pl.BlockSpec((S + PAD, HD), lambda h, i: (0, h)),
         pl.BlockSpec((S + PAD, HD), lambda h, i: (0, h)), pl.BlockSpec((None, CHUNK, BW), lambda h, i: (h, 0, 0)),
         pl.BlockSpec((None, CHUNK, BW), lambda h, i: (h, 0, 0))],
        [pl.BlockSpec((R, HD), lambda h, i: (i, h))],
        [jax.ShapeDtypeStruct((S, E), BF16)],
        [pltpu.VMEM((3, QB, WIN), F32)],
        _params(2, 48), (q, kp, vp, ba, bb), comm)


def _final(h, g, tgt, name, tm=512):
    S = h.shape[0]

    def body(h_ref, g_ref, t_ref, dh_ref, st_ref):
        @pl.when(pl.program_id(0) == 0)
        def _():
            st_ref[...] = jnp.zeros((SUBLANES, D), F32)

        hh = h_ref[...]
        r = lax.rsqrt(jnp.mean(hh * hh, axis=-1, keepdims=True) + EPS)
        xhat = hh * r
        diff = xhat * g_ref[...] - t_ref[...]
        st_ref[1:2, :] += _colsum(diff * diff)
        dout = diff * (1.0 / D)
        st_ref[0:1, :] += _colsum(dout * xhat)
        dx = dout * g_ref[...]
        dh_ref[...] = r * (dx - xhat * jnp.mean(dx * xhat, axis=-1, keepdims=True))

    return pl.pallas_call(
        body, name=name, grid=(S // tm,),
        in_specs=[pl.BlockSpec((tm, D), lambda i: (i, 0)), _row(D), pl.BlockSpec((tm, D), lambda i: (i, 0))],
        out_specs=[pl.BlockSpec((tm, D), lambda i: (i, 0)), pl.BlockSpec((SUBLANES, D), lambda i: (0, 0))],
        out_shape=[jax.ShapeDtypeStruct((S, D), F32), jax.ShapeDtypeStruct((SUBLANES, D), F32)],
        compiler_params=_params(1, 32),
    )(h, g, tgt)


def _store_grad(acc, stage, dw_hbm, sem):
    for q in range(NCHIP):
        stage[...] = acc[q].astype(BF16)
        cp = pltpu.make_async_copy(stage, dw_hbm.at[q], sem)
        cp.start()
        cp.wait()


def _out_bwd(dh, y, gate, a, cs, z, w, name, comm=None, tm=256):
    S = dh.shape[0]
    kb = E // NCHIP
    n_t = S // tm

    def body(dh_ref, y_ref, gate_ref, a_ref, cs_ref, z_ref, w_hbm, da_ref, dz_ref, dw_hbm, st_ref, w_v, acc, stage, sem):
        i = pl.program_id(0)

        @pl.when(i == 0)
        def _():
            cp = pltpu.make_async_copy(w_hbm, w_v, sem)
            cp.start()
            acc[...] = jnp.zeros(acc.shape, F32)
            st_ref[...] = jnp.zeros((SUBLANES, D), F32)
            cp.wait()

        dhh = dh_ref[...]
        st_ref[0:1, :] += _colsum(dhh * y_ref[...].astype(F32))
        dy = (dhh * gate_ref[...]).astype(BF16)
        for p in range(NCHIP):
            cols = slice(p * kb, (p + 1) * kb)
            zz = z_ref[:, cols].astype(F32)
            sig = jax.nn.sigmoid(zz)
            sz = zz * sig
            ae = a_ref[:, cols].astype(F32) * cs_ref[:, cols]
            acc[p] += _tn((ae * sz).astype(BF16), dy)
            dact = _nt(dy, w_v[p])
            da_ref[:, cols] = (dact * sz).astype(BF16)
            dz_ref[:, cols] = (dact * ae * (sig * (1.0 + zz * (1.0 - sig)))).astype(BF16)

        @pl.when(i == n_t - 1)
        def _():
            _store_grad(acc, stage, dw_hbm, sem)

    return _call(
        body, name, (n_t,),
        [pl.BlockSpec((tm, D), lambda i: (i, 0)), pl.BlockSpec((tm, D), lambda i: (i, 0)), _row(D),
         pl.BlockSpec((tm, E), lambda i: (i, 0)), _row(E), pl.BlockSpec((tm, E), lambda i: (i, 0)), ANY],
        [pl.BlockSpec((tm, E), lambda i: (i, 0)), pl.BlockSpec((tm, E), lambda i: (i, 0)), ANY,
         pl.BlockSpec((SUBLANES, D), lambda i: (0, 0))],
        [jax.ShapeDtypeStruct((S, E), BF16), jax.ShapeDtypeStruct((S, E), BF16),
         jax.ShapeDtypeStruct((NCHIP, kb, D), BF16), jax.ShapeDtypeStruct((SUBLANES, D), F32)],
        [pltpu.VMEM((NCHIP, kb, D), BF16), pltpu.VMEM((NCHIP, kb, D), F32), pltpu.VMEM((kb, D), BF16),
         pltpu.SemaphoreType.DMA],
        _params(1, 52), (dh, y, gate, a, cs, z, w), comm)


def _attn_bwd(q, kp, vp, ba, bb, do, prev, name, comm=None):
    S = q.shape[0]
    nsub = _nsub(S)
    R = nsub * QB
    n_i = S // R
    dt_kv = F32 if prev is None else BF16

    def body(*refs):
        q_ref, k_ref, v_ref, ba_ref, bb_ref, do_ref = refs[:6]
        refs = refs[6:]
        if prev is not None:
            pk_hbm, pv_hbm = refs[:2]
            refs = refs[2:]
        dq_ref, dk_ref, dv_ref, dba_ref, dbb_ref, bias3, dbias, dk_acc, dv_acc = refs[:9]
        if prev is not None:
            pk_v, pv_v, sems = refs[9:]
        h = pl.program_id(0)
        i = pl.program_id(1)

        def prev_copies():
            cols = pl.ds(pl.multiple_of(h * HD, HD), HD)
            return (pltpu.make_async_copy(pk_hbm.at[:, cols], pk_v, sems.at[0]),
                    pltpu.make_async_copy(pv_hbm.at[:, cols], pv_v, sems.at[1]))

        @pl.when(i == 0)
        def _():
            if prev is not None:
                for cp in prev_copies():
                    cp.start()
            _build_bias(bias3, ba_ref, bb_ref)
            dbias[...] = jnp.zeros((2, CHUNK, DBW), F32)
            dk_acc[...] = jnp.zeros((S + PAD, HD), F32)
            dv_acc[...] = jnp.zeros((S + PAD, HD), F32)

        def mxu_in(sub):
            return (_scores(q_ref, k_ref, i, sub, nsub),
                    _nt(do_ref[sub * QB:(sub + 1) * QB, :], v_ref[pl.ds(_row0(i, sub, nsub), WIN), :]))

        nxt = mxu_in(0)
        for sub in range(nsub):
            rows = slice(sub * QB, (sub + 1) * QB)
            win = pl.ds(_row0(i, sub, nsub), WIN)
            s, dp = nxt
            if sub + 1 < nsub:
                nxt = mxu_in(sub + 1)
            e, l = _exp_parts(s, bias3, i, sub)
            p = e * (1.0 / l)
            ds = p * (dp - jnp.sum(p * dp, axis=-1, keepdims=True))
            for par in range(2):
                part = None
                for qc in range(par, QC, 2):
                    c0 = (qc - par) * CHUNK + BW - DBW
                    blk_ = ds[qc * CHUNK:(qc + 1) * CHUNK, c0:c0 + DBW]
                    part = blk_ if part is None else part + blk_
                dbias[par] += part
            dsb = (ds * SM_SCALE).astype(BF16)
            dq_ref[rows, :] = _nn(dsb, k_ref[win, :]).astype(BF16)
            dk_acc[win, :] += _tn(dsb, q_ref[rows, :])
            dv_acc[win, :] += _tn(p.astype(BF16), do_ref[rows, :])

        @pl.when(i == n_i - 1)
        def _():
            zeros = jnp.zeros((CHUNK, BW - DBW), F32)
            dba_ref[...] = jnp.concatenate([zeros, dbias[0]], axis=1)
            dbb_ref[...] = jnp.concatenate([zeros, dbias[1]], axis=1)
            if prev is None:
                dk_ref[...] = dk_acc[...]
                dv_ref[...] = dv_acc[...]
            else:
                for cp in prev_copies():
                    cp.wait()
                dk_ref[...] = (dk_acc[...] + pk_v[...]).astype(BF16)
                dv_ref[...] = (dv_acc[...] + pv_v[...]).astype(BF16)

    head = pl.BlockSpec((S + PAD, HD), lambda h, i: (0, h))
    strip = pl.BlockSpec((None, CHUNK, BW), lambda h, i: (h, 0, 0))
    blk = pl.BlockSpec((R, HD), lambda h, i: (i, h))
    in_specs = [blk, head, head, strip, strip, blk]
    scratch = [pltpu.VMEM((3, QB, WIN), F32), pltpu.VMEM((2, CHUNK, DBW), F32), pltpu.VMEM((S + PAD, HD), F32),
               pltpu.VMEM((S + PAD, HD), F32)]
    args = (q, kp, vp, ba, bb, do)
    if prev is not None:
        in_specs += [ANY, ANY]
        scratch += [pltpu.VMEM((S + PAD, HD), F32), pltpu.VMEM((S + PAD, HD), F32), pltpu.SemaphoreType.DMA((2,))]
        args += tuple(prev)
    return _call(
        body, name, (NH, n_i), in_specs, [blk, head, head, strip, strip],
        [jax.ShapeDtypeStruct((S, E), BF16), jax.ShapeDtypeStruct((S + PAD, E), dt_kv),
         jax.ShapeDtypeStruct((S + PAD, E), dt_kv), jax.ShapeDtypeStruct((NH, CHUNK, BW), F32),
         jax.ShapeDtypeStruct((NH, CHUNK, BW), F32)],
        scratch, _params(2, 56), args, comm)


def _pool_bwd(dms, mixed, pooled, wg, a_scale, name, comm=None, tm=512):
    S = dms.shape[0]
    n_t = S // tm

    def rev(i):
        return (n_t - 1 - i, 0)

    def body(d_ref, m_ref, p_ref, wg_ref, as_ref, dv_ref, dwg_ref, st_ref, buf):
        i = pl.program_id(0)

        @pl.when(i == 0)
        def _():
            buf[tm:tm + HALO, :] = jnp.zeros((HALO, E), F32)
            dwg_ref[...] = jnp.zeros((4, GW, GW), F32)
            st_ref[...] = jnp.zeros((SUBLANES, E), F32)

        t = (n_t - 1 - i) * tm + lax.broadcasted_iota(jnp.int32, (tm, 1), 0)
        st_ref[0:1, :] += _colsum(d_ref[...].astype(F32) * m_ref[...].astype(F32))
        for gi, w in enumerate(POOL_W):
            cols = slice(gi * GW, (gi + 1) * GW)
            dm = (d_ref[:, cols].astype(F32) * as_ref[:, cols]).astype(BF16)
            dpool = _nt(dm, wg_ref[gi])
            dwg_ref[gi] += _tn(p_ref[:, cols], dm)
            cnt = jnp.minimum(t + 1, w).astype(F32)
            buf[0:tm, cols] = dpool / cnt
            s = buf[:, cols]
            k = 1
            while k < w:
                s = s + pltpu.roll(s, tm + HALO - k, 0)
                k *= 2
            dv_ref[:, cols] = (s[0:tm, :] - dpool).astype(BF16)
        buf[tm:tm + HALO, :] = buf[0:HALO, :]

    return _call(
        body, name, (n_t,),
        [pl.BlockSpec((tm, E), rev), pl.BlockSpec((tm, E), rev), pl.BlockSpec((tm, E), rev),
         pl.BlockSpec((4, GW, GW), lambda i: (0, 0, 0)), _row(E)],
        [pl.BlockSpec((tm, E), rev), pl.BlockSpec((4, GW, GW), lambda i: (0, 0, 0)),
         pl.BlockSpec((SUBLANES, E), lambda i: (0, 0))],
        [jax.ShapeDtypeStruct((S, E), BF16), jax.ShapeDtypeStruct((4, GW, GW), F32),
         jax.ShapeDtypeStruct((SUBLANES, E), F32)],
        [pltpu.VMEM((tm + HALO, E), F32)],
        _params(1, 52), (dms, mixed, pooled, wg, a_scale), comm)


def _in_bwd(da, db, row_off, u, h, g, scale, w, dh_out, name, comm=None, tm=256):
    S = h.shape[0]
    n_t = S // tm
    off = row_off // tm

    def body(da_ref, db_ref, u_ref, h_ref, g_ref, sc_ref, w_hbm, dho_ref, dhi_ref, dw_hbm, st_ref, w_v, acc, stage, sem):
        i = pl.program_id(0)

        @pl.when(i == 0)
        def _():
            cp = pltpu.make_async_copy(w_hbm, w_v, sem)
            cp.start()
            acc[...] = jnp.zeros(acc.shape, F32)
            st_ref[...] = jnp.zeros((SUBLANES, D), F32)
            cp.wait()

        ub = u_ref[...]
        du = None
        for q in range(NCHIP):
            d_ref = da_ref if q < 2 else db_ref
            dv = d_ref[:, (q % 2) * D:(q % 2 + 1) * D]
            acc[q] += _tn(ub, dv)
            part = _nt(dv, w_v[q])
            du = part if du is None else du + part

        hh = h_ref[...]
        r = lax.rsqrt(jnp.mean(hh * hh, axis=-1, keepdims=True) + EPS)
        xhat = hh * r
        gg = g_ref[...]
        st_ref[0:1, :] += _colsum(du)
        st_ref[1:2, :] += _colsum(du * (xhat * gg))
        dn = du * (1.0 + sc_ref[...])
        st_ref[2:3, :] += _colsum(dn * xhat)
        dx = dn * gg
        dhi_ref[...] = dho_ref[...] + r * (dx - xhat * jnp.mean(dx * xhat, axis=-1, keepdims=True))

        @pl.when(i == n_t - 1)
        def _():
            _store_grad(acc, stage, dw_hbm, sem)

    part_spec = pl.BlockSpec((tm, E), lambda i: (i + off, 0))
    return _call(
        body, name, (n_t,),
        [part_spec, part_spec, pl.BlockSpec((tm, D), lambda i: (i, 0)), pl.BlockSpec((tm, D), lambda i: (i, 0)),
         _row(D), _row(D), ANY, pl.BlockSpec((tm, D), lambda i: (i, 0))],
        [pl.BlockSpec((tm, D), lambda i: (i, 0)), ANY, pl.BlockSpec((SUBLANES, D), lambda i: (0, 0))],
        [jax.ShapeDtypeStruct((S, D), F32), jax.ShapeDtypeStruct((NCHIP, D, D), BF16),
         jax.ShapeDtypeStruct((SUBLANES, D), F32)],
        [pltpu.VMEM((NCHIP, D, D), BF16), pltpu.VMEM((NCHIP, D, D), F32), pltpu.VMEM((D, D), BF16),
         pltpu.SemaphoreType.DMA],
        _params(1, 56), (da, db, u, h, g, scale, w, dh_out), comm)


def _cmat(c_all, w, b, name):
    L, _, n = w.shape

    def body(c_ref, w_ref, b_ref, ca_ref, o_ref):
        cc = c_ref[...]
        ca = cc * jax.nn.sigmoid(cc)
        ca_ref[...] = ca
        o_ref[...] = _nn(ca.astype(BF16), w_ref[...].astype(BF16)) + b_ref[...]

    return pl.pallas_call(
        body, name=name, grid=(L,),
        in_specs=[pl.BlockSpec((SUBLANES, D), lambda l: (0, 0)), pl.BlockSpec((None, D, n), lambda l: (l, 0, 0)),
                  pl.BlockSpec((None, 1, n), lambda l: (l, 0, 0))],
        out_specs=[pl.BlockSpec((SUBLANES, D), lambda l: (0, 0)), pl.BlockSpec((None, SUBLANES, n), lambda l: (l, 0, 0))],
        out_shape=[jax.ShapeDtypeStruct((SUBLANES, D), F32), jax.ShapeDtypeStruct((L, SUBLANES, n), F32)],
        compiler_params=_params(1, 32),
    )(c_all, w, b)


def _grad_ada(c_act_t, dmod, name):
    L, _, n = dmod.shape

    def body(c_ref, d_ref, o_ref):
        acc = None
        for b in range(SUBLANES):
            part = c_ref[:, b:b + 1] * d_ref[b:b + 1, :]
            acc = part if acc is None else acc + part
        o_ref[...] = acc

    return pl.pallas_call(
        body, name=name, grid=(L,),
        in_specs=[pl.BlockSpec((D, SUBLANES), lambda l: (0, 0)), pl.BlockSpec((None, SUBLANES, n), lambda l: (l, 0, 0))],
        out_specs=pl.BlockSpec((None, D, n), lambda l: (l, 0, 0)),
        out_shape=jax.ShapeDtypeStruct((L, D, n), F32),
        compiler_params=_params(1, 32),
    )(c_act_t, dmod)


def _stats_reduce(g3, loss_row, name):
    n_dev, rows, _ = g3.shape

    def body(g_ref, o_ref, l_ref):
        acc = g_ref[0]
        for d in range(1, n_dev):
            acc = acc + g_ref[d]
        o_ref[...] = acc
        tot = jnp.sum(o_ref[loss_row:loss_row + 1, :], axis=-1, keepdims=True)
        l_ref[...] = jnp.broadcast_to(tot * (0.5 / D), (SUBLANES, LANES))

    return pl.pallas_call(
        body, name=name,
        in_specs=[pl.BlockSpec(memory_space=pltpu.VMEM)],
        out_specs=[pl.BlockSpec(memory_space=pltpu.VMEM), pl.BlockSpec(memory_space=pltpu.VMEM)],
        out_shape=[jax.ShapeDtypeStruct((rows, D), F32), jax.ShapeDtypeStruct((SUBLANES, LANES), F32)],
        compiler_params=pltpu.CompilerParams(vmem_limit_bytes=32 * 2 ** 20),
    )(g3)


def _sum4(own, land, chip, name, tr=256):
    _, R, C = own.shape
    tr = min(tr, R)

    def body(p_ref, own_ref, land_ref, o_ref):
        o_ref[...] = ((own_ref[...].astype(F32) + land_ref[0].astype(F32)) + land_ref[1].astype(F32)) + land_ref[2].astype(F32)

    return pl.pallas_call(
        body, name=name,
        grid_spec=pltpu.PrefetchScalarGridSpec(
            num_scalar_prefetch=1, grid=(R // tr,),
            in_specs=[pl.BlockSpec((None, tr, C), lambda i, p: (p[0], i, 0)), pl.BlockSpec((3, tr, C), lambda i, p: (0, i, 0))],
            out_specs=pl.BlockSpec((tr, C), lambda i, p: (i, 0))),
        out_shape=jax.ShapeDtypeStruct((R, C), F32),
        compiler_params=_params(1, 32),
    )(chip, own, land)


def _adamw(w, m, v, g, name, tr=256):
    L, R, C = w.shape
    tr = min(tr, R)
    stacked = not isinstance(g, (list, tuple))
    n_g = None if stacked else [len(ps) for ps in g]
    flat = [g] if stacked else [a for ps in g for a in ps]

    def body(*refs):
        w_ref, m_ref, v_ref = refs[:3]
        g_refs = refs[3:3 + len(flat)]
        go_ref, d_ref, mo_ref, vo_ref = refs[3 + len(flat):]
        if stacked:
            gg = g_refs[0][...]
        else:
            layer = pl.program_id(0)
            gg = None
            k = 0
            for li in range(L):
                gl = None
                for _ in range(n_g[li]):
                    x = g_refs[k][...]
                    gl = x if gl is None else gl + x
                    k += 1
                gg = gl if gg is None else jnp.where(layer == li, gl, gg)
        m2 = ADAM_B1 * m_ref[...] + (1.0 - ADAM_B1) * gg
        v2 = ADAM_B2 * v_ref[...] + (1.0 - ADAM_B2) * (gg * gg)
        m_hat = m2 / (1.0 - ADAM_B1 ** ADAM_STEP)
        v_hat = v2 / (1.0 - ADAM_B2 ** ADAM_STEP)
        go_ref[...] = gg
        d_ref[...] = -ADAM_LR * (m_hat / (jnp.sqrt(v_hat) + ADAM_EPS) + ADAM_WD * w_ref[...])
        mo_ref[...] = m2
        vo_ref[...] = v2

    big = pl.BlockSpec((None, tr, C), lambda l, i: (l, i, 0))
    g_specs = [big] if stacked else [pl.BlockSpec((tr, C), lambda l, i: (i, 0))] * len(flat)
    return pl.pallas_call(
        body, name=name, grid=(L, R // tr),
        in_specs=[big, big, big] + g_specs,
        out_specs=[big, big, big, big],
        out_shape=[jax.ShapeDtypeStruct((L, R, C), F32)] * 4,
        compiler_params=_params(2, 40),
    )(w, m, v, *flat)


def _allgather8(xs, name):
    m, n = xs.shape

    def body(x_ref, out_ref, send_sems, recv_sems, local_sem):
        x, y, c = _place()
        me, sibling = (x, y, c), (x, y, 1 - c)
        chips = [(1 - x, y), (x, 1 - y), (1 - x, 1 - y)]

        def rows(px, py, pc):
            return out_ref.at[pl.ds((4 * px + 2 * py + pc) * m, m), :]

        def copy(k, block, to, src=None):
            return pltpu.make_async_remote_copy(
                src_ref=rows(*block) if src is None else src, dst_ref=rows(*block),
                send_sem=send_sems.at[k], recv_sem=recv_sems.at[k], device_id=to, device_id_type=MESH)

        mine = pltpu.make_async_copy(x_ref, rows(*me), local_sem)
        mine.start()
        first = [copy(0, me, sibling, src=x_ref)]
        first += [copy(1 + j, me, (*chip, c), src=x_ref) for j, chip in enumerate(chips)]
        for cp in first:
            cp.start()
        passed = [copy(4 + j, (*chip, c), sibling) for j, chip in enumerate(chips)]
        for j, chip in enumerate(chips):
            copy(1 + j, (*chip, c), me).wait_recv()
            passed[j].start()
        copy(0, sibling, me).wait_recv()
        for j, chip in enumerate(chips):
            copy(4 + j, (*chip, 1 - c), me).wait_recv()
        for cp in first + passed:
            cp.wait_send()
        mine.wait()

    return pl.pallas_call(
        body, name=name,
        out_shape=jax.ShapeDtypeStruct((8 * m, n), xs.dtype),
        in_specs=[pl.BlockSpec(memory_space=pltpu.VMEM)],
        out_specs=pl.BlockSpec(memory_space=pltpu.VMEM),
        scratch_shapes=[pltpu.SemaphoreType.DMA((7,)), pltpu.SemaphoreType.DMA((7,)), pltpu.SemaphoreType.DMA],
        compiler_params=pltpu.CompilerParams(vmem_limit_bytes=32 * 2 ** 20),
    )(xs)


def _pad8(a):
    return jnp.pad(a, ((0, SUBLANES - a.shape[0]), (0, 0)))


def _group_rows(wg):
    return wg.transpose(1, 0, 2, 3).reshape(4, GW, GW)


def _example_step(h0, tgt, mods, kvmod, a_scale, norm_g, kv_norm_g, final_g, b_rel_bias, sh, chip_arr):
    ones_e = jnp.ones((1, E), F32)
    shift = [mods[l:l + 1, 0:D] for l in range(4)]
    scale = [mods[l:l + 1, D:2 * D] for l in range(4)]
    gate = [mods[l:l + 1, 2 * D:3 * D] for l in range(4)]
    gl = [norm_g[l:l + 1] for l in range(4)]
    kv_shift, kv_scale = kvmod[None, 0:D], kvmod[None, D:2 * D]
    kv_g = kv_norm_g[None]

    w_a = _comm_only(_Comm(gathers=[sh["a_in"][0], sh["a_grp"][0], sh["a_out"][0]]), "gather_first")
    hs = [h0]
    saved = []
    nxt = [[sh["a_in"][1], sh["a_grp"][1], sh["a_out"][1]], [sh["kv"][0], sh["b_in"][0]]]
    for l in range(2):
        w_in_l, wg_l, wo_l = w_a
        wg_full = _group_rows(wg_l)
        (u, z, pooled, mixed, y, hn), got = _a_fwd(hs[-1], gl[l], shift[l], scale[l], a_scale[l:l + 1], gate[l], w_in_l,
                                                   wg_full, wo_l, f"a{l}_fwd", comm=_Comm(gathers=nxt[l]))
        saved.append((u, z, pooled, mixed, y, w_in_l, wg_full, wo_l))
        hs.append(hn)
        if l == 0:
            w_a = got
        else:
            w_kv, wb_in0 = got

    (uk, kp, vp), (wb_out0, wb_in1) = _in_fwd(hs[2], kv_g, kv_shift, kv_scale, w_kv, BF16, BF16, "kv_in_fwd", pad_rows=PAD,
                                              comm=_Comm(gathers=[sh["b_out"][0], sh["b_in"][1]]))
    wb_in = [wb_in0, wb_in1]
    wb_out = [wb_out0, None]

    strips = []
    for bi in range(2):
        l = 2 + bi
        sa, sb = _bias_build(jnp.pad(b_rel_bias[bi], ((0, 0), (0, NRELP - NREL))), f"b{bi}_bias")
        sa, sb = sa.transpose(1, 0, 2), sb.transpose(1, 0, 2)
        strips.append((sa, sb))
        (u, q, z), got = _in_fwd(hs[-1], gl[l], shift[l], scale[l], wb_in[bi], BF16, BF16, f"b{bi}_in_fwd",
                                 comm=_Comm(gathers=[sh["b_out"][1]]) if bi == 0 else None)
        if bi == 0:
            (wb_out[1],) = got
        (att,), _ = _attn_fwd(q, kp, vp, sa, sb, f"b{bi}_attn_fwd")
        (y, hn), _ = _out_fwd(att, z, wb_out[bi], gate[l], hs[-1], f"b{bi}_out_fwd")
        saved.append((u, z, q, att, y))
        hs.append(hn)

    dh, st_fin = _final(hs[4], final_g[None], tgt, "final")

    st_in = [None] * 4
    st_out = [None] * 4
    grads = {}
    landed = {}

    def carry(names):
        return _Comm(scatters=[grads[n] for n in names]) if names else None

    def land(names, got):
        for n, a in zip(names, got):
            landed[n] = a

    u, z, q, att, y = saved[3]
    sa, sb = strips[1]
    (datt, dz, grads["b_out1"], st_out[3]), _ = _out_bwd(dh, y, gate[3], att, ones_e, z, wb_out[1], "b1_out_bwd")
    (dq, dk1, dv1, dsa, dsb), got = _attn_bwd(q, kp, vp, sa, sb, datt, None, "b1_attn_bwd", comm=carry(["b_out1"]))
    land(["b_out1"], got)
    drb1 = _dbias_reduce(dsa.transpose(1, 0, 2), dsb.transpose(1, 0, 2), "b1_dbias")
    (dh, grads["b_in1"], st_in[3]), _ = _in_bwd(dq, dz, 0, u, hs[3], gl[3], scale[3], wb_in[1], dh, "b1_in_bwd")
    u, z, q, att, y = saved[2]
    sa, sb = strips[0]
    (datt, dz, grads["b_out0"], st_out[2]), _ = _out_bwd(dh, y, gate[2], att, ones_e, z, wb_out[0], "b0_out_bwd")
    (dq, dk, dv, dsa, dsb), got = _attn_bwd(q, kp, vp, sa, sb, datt, (dk1, dv1), "b0_attn_bwd",
                                            comm=carry(["b_in1", "b_out0"]))
    land(["b_in1", "b_out0"], got)
    drb0 = _dbias_reduce(dsa.transpose(1, 0, 2), dsb.transpose(1, 0, 2), "b0_dbias")
    (dh, grads["b_in0"], st_in[2]), _ = _in_bwd(dq, dz, 0, u, hs[2], gl[2], scale[2], wb_in[0], dh, "b0_in_bwd")
    (dh, grads["kv"], st_kv), got = _in_bwd(dk, dv, PAD, uk, hs[2], kv_g, kv_scale, w_kv, dh, "kv_in_bwd",
                                            comm=carry(["b_in0"]))
    land(["b_in0"], got)
    st_pool = [None] * 2
    plan = {1: dict(o=[], p=["a_out1"], i=["kv", "a_grp1"]), 0: dict(o=["a_in1"], p=["a_out0"], i=[])}
    early = ["b_out1", "b_in1", "b_out0", "b_in0", "kv", "a_out1", "a_grp1", "a_in1"]
    late = ["a_out0", "a_grp0", "a_in0"]
    both = {}

    def sum4(n):
        return _sum4(grads[n], landed[n], chip_arr, f"sum4_{n}")

    for l in (1, 0):
        u, z, pooled, mixed, y, w_in_l, wg_full, wo = saved[l]
        asl = a_scale[l:l + 1]
        (dms, dz, grads[f"a_out{l}"], st_out[l]), got = _out_bwd(dh, y, gate[l], mixed, asl, z, wo, f"a{l}_out_bwd",
                                                                comm=carry(plan[l]["o"]))
        land(plan[l]["o"], got)
        comm = carry(plan[l]["p"])
        if l == 0:
            mine = [sum4(n) for n in early]
            comm = _Comm(scatters=[grads[n] for n in plan[l]["p"]], swaps=mine)
        (dval, dwg, st_pool[l]), got = _pool_bwd(dms, mixed, pooled, wg_full, asl, f"a{l}_pool_bwd", comm=comm)
        land(plan[l]["p"], got)
        if l == 0:
            both.update({n: [a, b] for n, a, b in zip(early, mine, got[len(plan[l]["p"]):])})
        grads[f"a_grp{l}"] = (dwg.reshape(4, NCHIP, GW // NCHIP, GW).transpose(1, 0, 2, 3).reshape(NCHIP, GW, GW)
                              .astype(BF16))
        (dh, grads[f"a_in{l}"], st_in[l]), got = _in_bwd(dval, dz, 0, u, hs[l], gl[l], scale[l], w_in_l, dh, f"a{l}_in_bwd",
                                                         comm=carry(plan[l]["i"]))
        land(plan[l]["i"], got)
    land(["a_grp0", "a_in0"], _comm_only(carry(["a_grp0", "a_in0"]), "scatter_last"))
    mine = [sum4(n) for n in late]
    both.update({n: [a, b] for n, a, b in zip(late, mine, _comm_only(_Comm(swaps=mine), "swap_last"))})

    pieces = st_in + [st_kv] + st_out + [st_fin]
    pieces += [_pad8(st_pool[l][0].reshape(2, D)) for l in range(2)]
    pieces += [_pad8(d.reshape(NH * NRELP // D, D)) for d in (drb0, drb1)]
    stats = jnp.concatenate(pieces, axis=0)
    return dh, both, stats


ROW_IN = [8 * l for l in range(4)]
ROW_KV = 32
ROW_OUT = [40 + 8 * l for l in range(4)]
ROW_FIN = 72
ROW_ASC = [80, 88]
ROW_RB = [96, 104]
N_STAT = 112


def kernel(x, c, ada_w, ada_b, norm_g, a_w_in, a_w_group, a_scale, a_w_out, kv_norm_g, kv_ada_w, kv_ada_b, w_kv, b_w_in, b_rel_bias, b_w_out, final_g, loss_target, m_ada_w, m_ada_b, m_norm_g, m_a_w_in, m_a_w_group, m_a_scale, m_a_w_out, m_kv_norm_g, m_kv_ada_w, m_kv_ada_b, m_w_kv, m_b_w_in, m_b_rel_bias, m_b_w_out, m_final_g, v_ada_w, v_ada_b, v_norm_g, v_a_w_in, v_a_w_group, v_a_scale, v_a_w_out, v_kv_norm_g, v_kv_ada_w, v_kv_ada_b, v_w_kv, v_b_w_in, v_b_rel_bias, v_b_w_out, v_final_g):
    xi, yi, ci = _place()
    chip = 2 * xi + yi
    dev = 4 * xi + 2 * yi + ci
    n_ada = ada_w.shape[2]
    n_kva = kv_ada_w.shape[1]
    n_asc = a_scale.shape[1]

    c_all = _allgather8(jnp.broadcast_to(c, (SUBLANES, D)), "gather_c")[::SUBLANES]
    ada_b_sh = lax.dynamic_slice_in_dim(ada_b, chip * n_ada, n_ada, axis=1)
    kvb_sh = lax.dynamic_slice_in_dim(kv_ada_b, chip * n_kva, n_kva, axis=0)
    c_act, mod_ada = _cmat(c_all, ada_w, ada_b_sh[:, None, :], "mod_ada")
    _, mod_kv = _cmat(c_all, kv_ada_w[None], kvb_sh[None, None, :], "mod_kv")
    part = jnp.concatenate([mod_ada.transpose(1, 0, 2).reshape(SUBLANES, 4 * n_ada), mod_kv[0],
                            jnp.broadcast_to(a_scale.reshape(1, 2 * n_asc), (SUBLANES, 2 * n_asc))], axis=1)
    gathered = _allgather8(part, "gather_mod")
    rows = jnp.concatenate([lax.dynamic_slice_in_dim(gathered, SUBLANES * (2 * p + ci) + dev, 1, axis=0)
                            for p in range(NCHIP)], axis=0)
    mods = jnp.stack([rows[:, l * n_ada:(l + 1) * n_ada].reshape(3 * D) for l in range(4)])
    kvmod = rows[:, 4 * n_ada:4 * n_ada + n_kva].reshape(2 * D)
    o_asc = 4 * n_ada + n_kva
    a_scale_full = jnp.stack([rows[:, o_asc + l * n_asc:o_asc + (l + 1) * n_asc].reshape(E) for l in range(2)])

    sh = dict(a_in=[a_w_in[l].astype(BF16) for l in range(2)], a_grp=[a_w_group[l].astype(BF16) for l in range(2)],
              a_out=[a_w_out[l].astype(BF16) for l in range(2)], kv=[w_kv.astype(BF16)],
              b_in=[b_w_in[l].astype(BF16) for l in range(2)], b_out=[b_w_out[l].astype(BF16) for l in range(2)])
    chip_arr = jnp.reshape(chip, (1,)).astype(jnp.int32)
    dh, both, stats = _example_step(x[0], loss_target[0], mods, kvmod, a_scale_full, norm_g, kv_norm_g, final_g,
                                    b_rel_bias, sh, chip_arr)
    grad_x = dh[None]

    g3 = _allgather8(stats, "gather_stats").reshape(8, N_STAT, D)
    red, loss_tile = _stats_reduce(g3, ROW_FIN + 1, "stats_reduce")
    loss = loss_tile[0, 0]

    def cat(rows_):
        return jnp.concatenate(rows_, axis=-1)

    g_ada_b = jnp.stack([cat([red[ROW_IN[l]], red[ROW_IN[l] + 1], red[ROW_OUT[l]]]) for l in range(4)])
    g_norm_g = jnp.stack([red[ROW_IN[l] + 2] for l in range(4)])
    g_kv_norm_g = red[ROW_KV + 2]
    g_kv_ada_b = cat([red[ROW_KV], red[ROW_KV + 1]])
    g_final_g = red[ROW_FIN]
    g_asc_full = jnp.stack([red[ROW_ASC[l]:ROW_ASC[l] + 2].reshape(E) for l in range(2)])
    g_a_scale = lax.dynamic_slice_in_dim(g_asc_full, chip * n_asc, n_asc, axis=1)
    g_rel = jnp.stack([red[ROW_RB[bi]:ROW_RB[bi] + NH * NRELP // D].reshape(NH, NRELP)[:, :NREL] for bi in range(2)])

    dmod = jnp.stack([cat([g3[:, ROW_IN[l]], g3[:, ROW_IN[l] + 1], g3[:, ROW_OUT[l]]]) for l in range(4)])
    dmod_sh = lax.dynamic_slice_in_dim(dmod, chip * n_ada, n_ada, axis=2)
    dkv = cat([g3[:, ROW_KV], g3[:, ROW_KV + 1]])[None]
    dkv_sh = lax.dynamic_slice_in_dim(dkv, chip * n_kva, n_kva, axis=2)
    c_act_t = c_act.T
    g_ada_w = _grad_ada(c_act_t, dmod_sh, "grad_ada_w")
    g_kv_ada_w = _grad_ada(c_act_t, dkv_sh, "grad_kv_ada_w")

    def upd(w, m, v, g, name, shape3):
        g = g.reshape(shape3) if not isinstance(g, list) else g
        outs = _adamw(w.reshape(shape3), m.reshape(shape3), v.reshape(shape3), g, name)
        return [o.reshape(w.shape) for o in outs]

    def pair(name):
        return [both[name + "0"], both[name + "1"]]

    res = {}
    res["ada_w"] = upd(ada_w, m_ada_w, v_ada_w, g_ada_w, "adamw_ada_w", ada_w.shape)
    res["ada_b"] = upd(ada_b, m_ada_b, v_ada_b, g_ada_b, "adamw_ada_b", (1,) + ada_b.shape)
    res["norm_g"] = upd(norm_g, m_norm_g, v_norm_g, g_norm_g, "adamw_norm_g", (1,) + norm_g.shape)
    res["a_w_in"] = upd(a_w_in, m_a_w_in, v_a_w_in, pair("a_in"), "adamw_a_w_in", a_w_in.shape)
    res["a_w_group"] = upd(a_w_group, m_a_w_group, v_a_w_group, pair("a_grp"), "adamw_a_w_group", (2, GW, GW))
    res["a_scale"] = upd(a_scale, m_a_scale, v_a_scale, g_a_scale, "adamw_a_scale", (1,) + a_scale.shape)
    res["a_w_out"] = upd(a_w_out, m_a_w_out, v_a_w_out, pair("a_out"), "adamw_a_w_out", a_w_out.shape)
    res["kv_norm_g"] = upd(kv_norm_g, m_kv_norm_g, v_kv_norm_g, g_kv_norm_g, "adamw_kv_norm_g", (1, 1, D))
    res["kv_ada_w"] = upd(kv_ada_w, m_kv_ada_w, v_kv_ada_w, g_kv_ada_w, "adamw_kv_ada_w", (1,) + kv_ada_w.shape)
    res["kv_ada_b"] = upd(kv_ada_b, m_kv_ada_b, v_kv_ada_b, g_kv_ada_b, "adamw_kv_ada_b", (1, 1, 2 * D))
    res["w_kv"] = upd(w_kv, m_w_kv, v_w_kv, [both["kv"]], "adamw_w_kv", (1,) + w_kv.shape)
    res["b_w_in"] = upd(b_w_in, m_b_w_in, v_b_w_in, pair("b_in"), "adamw_b_w_in", b_w_in.shape)
    res["b_rel_bias"] = upd(b_rel_bias, m_b_rel_bias, v_b_rel_bias, g_rel, "adamw_b_rel_bias", (1, 2 * NH, NREL))
    res["b_w_out"] = upd(b_w_out, m_b_w_out, v_b_w_out, pair("b_out"), "adamw_b_w_out", b_w_out.shape)
    res["final_g"] = upd(final_g, m_final_g, v_final_g, g_final_g, "adamw_final_g", (1, 1, D))

    names = ["ada_w", "ada_b", "norm_g", "a_w_in", "a_w_group", "a_scale", "a_w_out", "kv_norm_g", "kv_ada_w", "kv_ada_b",
             "w_kv", "b_w_in", "b_rel_bias", "b_w_out", "final_g"]
    return (loss, grad_x, *[res[n][0] for n in names], *[res[n][1] for n in names], *[res[n][2] for n in names],
            *[res[n][3] for n in names])
```

```python
import math

import jax
import jax.numpy as jnp
from jax import lax
from jax.experimental import pallas as pl
from jax.experimental.pallas import tpu as pltpu

F32 = jnp.float32
BF16 = jnp.bfloat16

D = 1024
E = 2048
NH = 16
HD = 128
CHUNK = 64
LEFT = 8
PAD = LEFT * CHUNK
NREL = 257
NRELP = 384
REL_CLIP = 128
EPS = 1e-6
NEG = -1e30
LOG2E = math.log2(math.e)
SM_SCALE = HD ** -0.5
POOL_W = (2, 4, 8, 16)
GW = 512
HALO = 16
QC = 4
QB = QC * CHUNK
NMASK = PAD // QB
WIN = (QC + LEFT) * CHUNK
BW = (LEFT + 2) * CHUNK
DBW = 4 * CHUNK
NSUB = 8
NCHIP = 4
LANES = 128
SUBLANES = 8

ADAM_LR = 0.001
ADAM_B1 = 0.9
ADAM_B2 = 0.999
ADAM_EPS = 1e-08
ADAM_WD = 0.01
ADAM_STEP = 10

MESH = pl.DeviceIdType.MESH
ANY = pl.BlockSpec(memory_space=pl.ANY)


def _params(n_axes, vmem_mb):
    return pltpu.CompilerParams(dimension_semantics=("arbitrary",) * n_axes, vmem_limit_bytes=vmem_mb * 2 ** 20)


def _nn(a, b):
    return jnp.dot(a, b, preferred_element_type=F32)


def _nt(a, b):
    return lax.dot_general(a, b, (((1,), (1,)), ((), ())), preferred_element_type=F32)


def _tn(a, b):
    return lax.dot_general(a, b, (((0,), (0,)), ((), ())), preferred_element_type=F32)


def _row(n):
    return pl.BlockSpec((1, n), lambda i: (0, 0))


def _colsum(x):
    return jnp.sum(x, axis=0, keepdims=True)


def _place():
    return lax.axis_index("x"), lax.axis_index("y"), lax.axis_index("c")


class _Comm:
    def __init__(self, gathers=(), scatters=(), swaps=()):
        self.n_g = len(gathers)
        self.n_chip = len(gathers) + len(scatters)
        self.n_sw = len(swaps)
        self.arrays = list(gathers) + list(scatters) + list(swaps)
        self.n = len(self.arrays)
        self.half = [a.shape[0] // 2 for a in gathers]
        self.out_shape = ([jax.ShapeDtypeStruct((NCHIP,) + a.shape, a.dtype) for a in gathers]
                          + [jax.ShapeDtypeStruct((3,) + a.shape[1:], a.dtype) for a in scatters]
                          + [jax.ShapeDtypeStruct(a.shape, a.dtype) for a in swaps])
        n_c, n_f, n_s = max(3 * self.n_chip, 1), max(3 * self.n_g, 1), max(self.n_sw, 1)
        self.scratch = [pltpu.SemaphoreType.DMA((n_c,)), pltpu.SemaphoreType.DMA((n_c,)),
                        pltpu.SemaphoreType.DMA((max(self.n_g, 1),)), pltpu.SemaphoreType.DMA((n_f,)),
                        pltpu.SemaphoreType.DMA((n_f,)), pltpu.SemaphoreType.DMA((n_s,)), pltpu.SemaphoreType.DMA((n_s,))]

    def _chip_copies(self, ins, outs, send, recv, landing):
        x, y, c = _place()
        chips = [(1 - x, y), (x, 1 - y), (1 - x, 1 - y)]
        mine = 2 * x + y
        cps = []
        for k in range(self.n_chip):
            for j, (cx, cy) in enumerate(chips):
                q = 2 * cx + cy
                if k < self.n_g:
                    part = pl.ds(c * self.half[k], self.half[k])
                    src = ins[k].at[part]
                    dst = outs[k].at[q if landing else mine, part]
                else:
                    src = ins[k].at[q]
                    dst = outs[k].at[j]
                cps.append(pltpu.make_async_remote_copy(
                    src_ref=src, dst_ref=dst, send_sem=send.at[3 * k + j], recv_sem=recv.at[3 * k + j],
                    device_id=(cx, cy, c), device_id_type=MESH))
        return cps

    def _core_copies(self, outs, fsend, frecv, landing):
        x, y, c = _place()
        chips = [(1 - x, y), (x, 1 - y), (1 - x, 1 - y)]
        cps = []
        for k in range(self.n_g):
            for j, (cx, cy) in enumerate(chips):
                part = pl.ds((1 - c if landing else c) * self.half[k], self.half[k])
                blk = outs[k].at[2 * cx + cy, part]
                cps.append(pltpu.make_async_remote_copy(
                    src_ref=blk, dst_ref=blk, send_sem=fsend.at[3 * k + j], recv_sem=frecv.at[3 * k + j],
                    device_id=(x, y, 1 - c), device_id_type=MESH))
        return cps

    def _local_copies(self, ins, outs, loc):
        x, y, _ = _place()
        return [pltpu.make_async_copy(ins[k], outs[k].at[2 * x + y], loc.at[k]) for k in range(self.n_g)]

    def _swap_copies(self, ins, outs, ssend, srecv):
        x, y, c = _place()
        return [pltpu.make_async_remote_copy(
            src_ref=ins[k], dst_ref=outs[k], send_sem=ssend.at[k - self.n_chip], recv_sem=srecv.at[k - self.n_chip],
            device_id=(x, y, 1 - c), device_id_type=MESH) for k in range(self.n_chip, self.n)]

    def start(self, ins, outs, send, recv, loc, fsend, frecv, ssend, srecv):
        for cp in (self._local_copies(ins, outs, loc) + self._chip_copies(ins, outs, send, recv, False)
                   + self._swap_copies(ins, outs, ssend, srecv)):
            cp.start()

    def wait(self, ins, outs, send, recv, loc, fsend, frecv, ssend, srecv):
        lands = self._chip_copies(ins, outs, send, recv, True)
        passes = self._core_copies(outs, fsend, frecv, False)
        for k in range(self.n_chip):
            for j in range(3):
                lands[3 * k + j].wait_recv()
                if k < self.n_g:
                    passes[3 * k + j].start()
        for cp in self._core_copies(outs, fsend, frecv, True):
            cp.wait_recv()
        swaps = self._swap_copies(ins, outs, ssend, srecv)
        for cp in swaps:
            cp.wait_recv()
        for cp in self._chip_copies(ins, outs, send, recv, False) + passes + swaps:
            cp.wait_send()
        for cp in self._local_copies(ins, outs, loc):
            cp.wait()


def _call(body, name, grid, in_specs, out_specs, out_shape, scratch, params, args, comm=None):
    n_in, n_out, n_sc = len(in_specs), len(out_specs), len(scratch)
    if comm is None:
        outs = pl.pallas_call(body, name=name, grid=grid, in_specs=in_specs, out_specs=out_specs, out_shape=out_shape,
                              scratch_shapes=scratch, compiler_params=params)(*args)
        return list(outs), []
    n = comm.n
    o0 = n_in + n
    s0 = o0 + n_out + n

    def wrapped(*refs):
        c_refs = (refs[n_in:o0], refs[o0 + n_out:s0]) + tuple(refs[s0 + n_sc:])
        ids = [pl.program_id(a) for a in range(len(grid))]
        first = ids[0] == 0
        last = ids[0] == grid[0] - 1
        for a in range(1, len(grid)):
            first = first & (ids[a] == 0)
            last = last & (ids[a] == grid[a] - 1)

        @pl.when(first)
        def _():
            comm.start(*c_refs)

        body(*refs[:n_in], *refs[o0:o0 + n_out], *refs[s0:s0 + n_sc])

        @pl.when(last)
        def _():
            comm.wait(*c_refs)

    outs = pl.pallas_call(
        wrapped, name=name, grid=grid, in_specs=list(in_specs) + [ANY] * n, out_specs=list(out_specs) + [ANY] * n,
        out_shape=list(out_shape) + comm.out_shape, scratch_shapes=list(scratch) + comm.scratch, compiler_params=params,
    )(*args, *comm.arrays)
    return list(outs[:n_out]), list(outs[n_out:])


def _comm_only(comm, name):
    def body(*refs):
        c_refs = (refs[:comm.n], refs[comm.n:2 * comm.n]) + tuple(refs[2 * comm.n:])
        comm.start(*c_refs)
        comm.wait(*c_refs)

    return pl.pallas_call(body, name=name, in_specs=[ANY] * comm.n, out_specs=[ANY] * comm.n, out_shape=comm.out_shape,
                          scratch_shapes=comm.scratch)(*comm.arrays)


def _in_fwd(h, g, shift, scale, w, dt_a, dt_b, name, pad_rows=0, comm=None, tm=512):
    S = h.shape[0]
    n_pad = pad_rows // tm

    def body(h_ref, g_ref, sh_ref, sc_ref, w_hbm, u_ref, oa_ref, ob_ref, w_v, sem):
        i = pl.program_id(0)

        @pl.when(i == 0)
        def _():
            cp = pltpu.make_async_copy(w_hbm, w_v, sem)
            cp.start()
            cp.wait()

        hh = h_ref[...]
        r = lax.rsqrt(jnp.mean(hh * hh, axis=-1, keepdims=True) + EPS)
        u = (hh * r * g_ref[...]) * (1.0 + sc_ref[...]) + sh_ref[...]
        ub = u.astype(BF16)
        u_ref[...] = ub
        for q in range(NCHIP):
            o_ref = oa_ref if q < 2 else ob_ref
            o_ref[:, (q % 2) * D:(q % 2 + 1) * D] = _nn(ub, w_v[q]).astype(o_ref.dtype)

        if n_pad:
            @pl.when(i < n_pad)
            def _():
                oa_ref[...] = jnp.zeros(oa_ref.shape, oa_ref.dtype)
                ob_ref[...] = jnp.zeros(ob_ref.shape, ob_ref.dtype)

    def src(i):
        return (jnp.maximum(i - n_pad, 0), 0)

    outs, landed = _call(
        body, name, (S // tm + n_pad,),
        [pl.BlockSpec((tm, D), src), _row(D), _row(D), _row(D), ANY],
        [pl.BlockSpec((tm, D), src), pl.BlockSpec((tm, E), lambda i: (i, 0)), pl.BlockSpec((tm, E), lambda i: (i, 0))],
        [jax.ShapeDtypeStruct((S, D), BF16), jax.ShapeDtypeStruct((S + pad_rows, E), dt_a),
         jax.ShapeDtypeStruct((S + pad_rows, E), dt_b)],
        [pltpu.VMEM((NCHIP, D, D), BF16), pltpu.SemaphoreType.DMA],
        _params(1, 52), (h, g, shift, scale, w), comm)
    return outs, landed


def _a_fwd(h, g, shift, scale, asc, gate, w_in, wg, w_out, name, comm=None, tm=512):
    S = h.shape[0]

    def body(h_ref, g_ref, sh_ref, sc_ref, as_ref, gate_ref, wi_hbm, wg_hbm, wo_hbm,
             u_ref, z_ref, p_ref, m_ref, y_ref, ho_ref, wi_v, wg_v, wo_v, buf, sems):
        i = pl.program_id(0)

        @pl.when(i == 0)
        def _():
            cps = [pltpu.make_async_copy(wi_hbm, wi_v, sems.at[0]), pltpu.make_async_copy(wg_hbm, wg_v, sems.at[1]),
                   pltpu.make_async_copy(wo_hbm, wo_v, sems.at[2])]
            for cp in cps:
                cp.start()
            buf[0:HALO, :] = jnp.zeros((HALO, E), F32)
            for cp in cps:
                cp.wait()

        hh = h_ref[...]
        r = lax.rsqrt(jnp.mean(hh * hh, axis=-1, keepdims=True) + EPS)
        ub = ((hh * r * g_ref[...]) * (1.0 + sc_ref[...]) + sh_ref[...]).astype(BF16)
        u_ref[...] = ub
        for q in range(2):
            buf[HALO:HALO + tm, q * D:(q + 1) * D] = _nn(ub, wi_v[q])
        t = i * tm + lax.broadcasted_iota(jnp.int32, (tm, 1), 0)
        y = None
        for gi, w in enumerate(POOL_W):
            cols = slice(gi * GW, (gi + 1) * GW)
            x = buf[:, cols]
            s = x
            k = 1
            while k < w:
                s = s + pltpu.roll(s, k, 0)
                k *= 2
            inv_cnt = 1.0 / jnp.minimum(t + 1, w).astype(F32)
            pb = (s[HALO:, :] * inv_cnt - x[HALO:, :]).astype(BF16)
            p_ref[:, cols] = pb
            mb = _nn(pb, wg_v[gi]).astype(BF16)
            m_ref[:, cols] = mb
            zb = _nn(ub, wi_v[2 + gi // 2, :, (gi % 2) * GW:(gi % 2 + 1) * GW]).astype(BF16)
            z_ref[:, cols] = zb
            zz = zb.astype(F32)
            act = ((mb.astype(F32) * as_ref[:, cols]) * (zz * jax.nn.sigmoid(zz))).astype(BF16)
            part = _nn(act, wo_v[gi])
            y = part if y is None else y + part
        buf[0:HALO, :] = buf[tm:tm + HALO, :]
        y_ref[...] = y.astype(BF16)
        ho_ref[...] = hh + gate_ref[...] * y

    rows_d = pl.BlockSpec((tm, D), lambda i: (i, 0))
    rows_e = pl.BlockSpec((tm, E), lambda i: (i, 0))
    return _call(
        body, name, (S // tm,),
        [rows_d, _row(D), _row(D), _row(D), _row(E), _row(D), ANY, ANY, ANY],
        [rows_d, rows_e, rows_e, rows_e, rows_d, rows_d],
        [jax.ShapeDtypeStruct((S, D), BF16), jax.ShapeDtypeStruct((S, E), BF16), jax.ShapeDtypeStruct((S, E), BF16),
         jax.ShapeDtypeStruct((S, E), BF16), jax.ShapeDtypeStruct((S, D), BF16), jax.ShapeDtypeStruct((S, D), F32)],
        [pltpu.VMEM((NCHIP, D, D), BF16), pltpu.VMEM((4, GW, GW), BF16), pltpu.VMEM((NCHIP, GW, D), BF16),
         pltpu.VMEM((tm + HALO, E), F32), pltpu.SemaphoreType.DMA((3,))],
        _params(1, 60), (h, g, shift, scale, asc, gate, w_in, wg, w_out), comm)


def _out_fwd(a, z, w, gate, h, name, comm=None, tm=512):
    S = h.shape[0]
    kb = E // NCHIP

    def body(a_ref, z_ref, w_hbm, gate_ref, h_ref, y_ref, ho_ref, w_v, sem):
        @pl.when(pl.program_id(0) == 0)
        def _():
            cp = pltpu.make_async_copy(w_hbm, w_v, sem)
            cp.start()
            cp.wait()

        y = None
        for p in range(NCHIP):
            cols = slice(p * kb, (p + 1) * kb)
            zz = z_ref[:, cols].astype(F32)
            act = (a_ref[:, cols].astype(F32) * (zz * jax.nn.sigmoid(zz))).astype(BF16)
            part = _nn(act, w_v[p])
            y = part if y is None else y + part
        y_ref[...] = y.astype(BF16)
        ho_ref[...] = h_ref[...] + gate_ref[...] * y

    return _call(
        body, name, (S // tm,),
        [pl.BlockSpec((tm, E), lambda i: (i, 0)), pl.BlockSpec((tm, E), lambda i: (i, 0)), ANY, _row(D),
         pl.BlockSpec((tm, D), lambda i: (i, 0))],
        [pl.BlockSpec((tm, D), lambda i: (i, 0)), pl.BlockSpec((tm, D), lambda i: (i, 0))],
        [jax.ShapeDtypeStruct((S, D), BF16), jax.ShapeDtypeStruct((S, D), F32)],
        [pltpu.VMEM((NCHIP, kb, D), BF16), pltpu.SemaphoreType.DMA],
        _params(1, 52), (a, z, w, gate, h), comm)


TW = BW + LANES


def _diag_onehot(transpose):
    shape = (TW, NRELP) if transpose else (NRELP, TW)
    j = lax.broadcasted_iota(jnp.int32, shape, 0 if transpose else 1)
    r = lax.broadcasted_iota(jnp.int32, shape, 1 if transpose else 0)
    idx = jnp.clip(PAD - (j - LANES), -REL_CLIP, REL_CLIP) + REL_CLIP
    return jnp.where(idx == r, 1.0, 0.0).astype(BF16)


def _strip_valid():
    m = lax.broadcasted_iota(jnp.int32, (NH, BW), 1)
    return m < (LEFT + 1) * CHUNK, m >= CHUNK


def _bias_build(rb, name):
    def body(rb_ref, a_ref, b_ref):
        x = rb_ref[...]
        hi = x.astype(BF16)
        r1 = x - hi.astype(F32)
        mid = r1.astype(BF16)
        lo = (r1 - mid.astype(F32)).astype(BF16)
        oh = _diag_onehot(False)
        diag = (_nn(hi, oh) + _nn(mid, oh)) + _nn(lo, oh)
        valid_a, valid_b = _strip_valid()
        for qi in range(CHUNK):
            a_ref[qi] = jnp.where(valid_a, pltpu.roll(diag, TW - (LANES - qi), 1)[:, :BW], NEG)
            b_ref[qi] = jnp.where(valid_b, pltpu.roll(diag, TW - (CHUNK - qi), 1)[:, :BW], NEG)

    vmem = pl.BlockSpec(memory_space=pltpu.VMEM)
    return pl.pallas_call(
        body, name=name, in_specs=[vmem], out_specs=[vmem, vmem],
        out_shape=[jax.ShapeDtypeStruct((CHUNK, NH, BW), F32), jax.ShapeDtypeStruct((CHUNK, NH, BW), F32)],
        compiler_params=pltpu.CompilerParams(vmem_limit_bytes=32 * 2 ** 20),
    )(rb)


def _dbias_reduce(dba, dbb, name):
    def body(a_ref, b_ref, o_ref):
        valid_a, valid_b = _strip_valid()
        zeros = jnp.zeros((NH, TW - BW), F32)
        acc = jnp.zeros((NH, TW), F32)
        for qi in range(CHUNK):
            xa = jnp.concatenate([jnp.where(valid_a, a_ref[qi], 0.0), zeros], axis=1)
            xb = jnp.concatenate([jnp.where(valid_b, b_ref[qi], 0.0), zeros], axis=1)
            acc = acc + (pltpu.roll(xa, LANES - qi, 1) + pltpu.roll(xb, CHUNK - qi, 1))
        oh = _diag_onehot(True)
        hi = acc.astype(BF16)
        mid = (acc - hi.astype(F32)).astype(BF16)
        r = lax.broadcasted_iota(jnp.int32, (NH, NRELP), 1)
        near = jnp.where(r < 2 * REL_CLIP, _nn(hi, oh) + _nn(mid, oh), 0.0)
        o_ref[...] = jnp.where(r == 2 * REL_CLIP, -jnp.sum(near, axis=-1, keepdims=True), near)

    vmem = pl.BlockSpec(memory_space=pltpu.VMEM)
    return pl.pallas_call(
        body, name=name, in_specs=[vmem, vmem], out_specs=vmem,
        out_shape=jax.ShapeDtypeStruct((NH, NRELP), F32),
        compiler_params=pltpu.CompilerParams(vmem_limit_bytes=32 * 2 ** 20),
    )(dba, dbb)


def _build_bias(bias3, ba_ref, bb_ref):
    bias3[NMASK] = jnp.full((QB, WIN), NEG, F32)
    for qc in range(QC):
        rows = slice(qc * CHUNK, (qc + 1) * CHUNK)
        if qc % 2 == 0:
            bias3[NMASK, rows, qc * CHUNK:qc * CHUNK + BW] = ba_ref[...] * LOG2E
        else:
            bias3[NMASK, rows, (qc - 1) * CHUNK:(qc - 1) * CHUNK + BW] = bb_ref[...] * LOG2E
    col = lax.broadcasted_iota(jnp.int32, (QB, WIN), 1)
    for sub in range(NMASK):
        bias3[sub] = jnp.where(col < PAD - sub * QB, NEG, bias3[NMASK])


def _nsub(S):
    n = min(NSUB, S // QB)
    assert S % (n * QB) == 0 and n >= NMASK
    return n


def _row0(i, sub, nsub):
    return pl.multiple_of((i * nsub + sub) * QB, QB)


def _scores(q_ref, k_ref, i, sub, nsub):
    return _nt(q_ref[sub * QB:(sub + 1) * QB, :], k_ref[pl.ds(_row0(i, sub, nsub), WIN), :])


def _exp_parts(s, bias3, i, sub):
    which = jnp.where(i == 0, sub, NMASK) if sub < NMASK else NMASK
    s = s * (SM_SCALE * LOG2E) + bias3[which]
    e = jnp.exp2(s - jnp.max(s, axis=-1, keepdims=True))
    return e, jnp.sum(e, axis=-1, keepdims=True)


def _attn_fwd(q, kp, vp, ba, bb, name, comm=None):
    S = q.shape[0]
    nsub = _nsub(S)
    R = nsub * QB

    def body(q_ref, k_ref, v_ref, ba_ref, bb_ref, o_ref, p_ref, bias3):
        i = pl.program_id(1)

        @pl.when(i == 0)
        def _():
            _build_bias(bias3, ba_ref, bb_ref)

        s_next = _scores(q_ref, k_ref, i, 0, nsub)
        for sub in range(nsub):
            s = s_next
            if sub + 1 < nsub:
                s_next = _scores(q_ref, k_ref, i, sub + 1, nsub)
            e, l = _exp_parts(s, bias3, i, sub)
            pb = (e * (1.0 / l)).astype(BF16)
            p_ref[sub] = pb
            o_ref[sub * QB:(sub + 1) * QB, :] = _nn(pb, v_ref[pl.ds(_row0(i, sub, nsub), WIN), :]).astype(BF16)

    return _call(
        body, name, (NH, S // R),
        [pl.BlockSpec((R, HD), lambda h, i: (i, h)), pl.BlockSpec((S + PAD, HD), lambda h, i: (0, h)),
         pl.BlockSpec((S + PAD, HD), lambda h, i: (0, h)), pl.BlockSpec((None, CHUNK, BW), lambda h, i: (h, 0, 0)),
         pl.BlockSpec((None, CHUNK, BW), lambda h, i: (h, 0, 0))],
        [pl.BlockSpec((R, HD), lambda h, i: (i, h)), pl.BlockSpec((None, nsub, QB, WIN), lambda h, i: (h, i, 0, 0))],
        [jax.ShapeDtypeStruct((S, E), BF16), jax.ShapeDtypeStruct((NH, S // QB, QB, WIN), BF16)],
        [pltpu.VMEM((NMASK + 1, QB, WIN), F32)],
        _params(2, 48), (q, kp, vp, ba, bb), comm)


def _final(h, g, tgt, name, tm=512):
    S = h.shape[0]

    def body(h_ref, g_ref, t_ref, dh_ref, st_ref):
        @pl.when(pl.program_id(0) == 0)
        def _():
            st_ref[...] = jnp.zeros((SUBLANES, D), F32)

        hh = h_ref[...]
        r = lax.rsqrt(jnp.mean(hh * hh, axis=-1, keepdims=True) + EPS)
        xhat = hh * r
        diff = xhat * g_ref[...] - t_ref[...]
        st_ref[1:2, :] += _colsum(diff * diff)
        dout = diff * (1.0 / D)
        st_ref[0:1, :] += _colsum(dout * xhat)
        dx = dout * g_ref[...]
        dh_ref[...] = r * (dx - xhat * jnp.mean(dx * xhat, axis=-1, keepdims=True))

    return pl.pallas_call(
        body, name=name, grid=(S // tm,),
        in_specs=[pl.BlockSpec((tm, D), lambda i: (i, 0)), _row(D), pl.BlockSpec((tm, D), lambda i: (i, 0))],
        out_specs=[pl.BlockSpec((tm, D), lambda i: (i, 0)), pl.BlockSpec((SUBLANES, D), lambda i: (0, 0))],
        out_shape=[jax.ShapeDtypeStruct((S, D), F32), jax.ShapeDtypeStruct((SUBLANES, D), F32)],
        compiler_params=_params(1, 32),
    )(h, g, tgt)


def _store_grad(acc, stage, dw_hbm, sem):
    for q in range(NCHIP):
        stage[...] = acc[q].astype(BF16)
        cp = pltpu.make_async_copy(stage, dw_hbm.at[q], sem)
        cp.start()
        cp.wait()


def _out_bwd(dh, y, gate, a, cs, z, w, name, comm=None, tm=256):
    S = dh.shape[0]
    kb = E // NCHIP
    n_t = S // tm

    def body(dh_ref, y_ref, gate_ref, a_ref, cs_ref, z_ref, w_hbm, da_ref, dz_ref, dw_hbm, st_ref, w_v, acc, stage, sem):
        i = pl.program_id(0)

        @pl.when(i == 0)
        def _():
            cp = pltpu.make_async_copy(w_hbm, w_v, sem)
            cp.start()
            acc[...] = jnp.zeros(acc.shape, F32)
            st_ref[...] = jnp.zeros((SUBLANES, D), F32)
            cp.wait()

        dhh = dh_ref[...]
        st_ref[0:1, :] += _colsum(dhh * y_ref[...].astype(F32))
        dy = (dhh * gate_ref[...]).astype(BF16)
        for p in range(NCHIP):
            cols = slice(p * kb, (p + 1) * kb)
            zz = z_ref[:, cols].astype(F32)
            sig = jax.nn.sigmoid(zz)
            sz = zz * sig
            ae = a_ref[:, cols].astype(F32) * cs_ref[:, cols]
            acc[p] += _tn((ae * sz).astype(BF16), dy)
            dact = _nt(dy, w_v[p])
            da_ref[:, cols] = (dact * sz).astype(BF16)
            dz_ref[:, cols] = (dact * ae * (sig * (1.0 + zz * (1.0 - sig)))).astype(BF16)

        @pl.when(i == n_t - 1)
        def _():
            _store_grad(acc, stage, dw_hbm, sem)

    return _call(
        body, name, (n_t,),
        [pl.BlockSpec((tm, D), lambda i: (i, 0)), pl.BlockSpec((tm, D), lambda i: (i, 0)), _row(D),
         pl.BlockSpec((tm, E), lambda i: (i, 0)), _row(E), pl.BlockSpec((tm, E), lambda i: (i, 0)), ANY],
        [pl.BlockSpec((tm, E), lambda i: (i, 0)), pl.BlockSpec((tm, E), lambda i: (i, 0)), ANY,
         pl.BlockSpec((SUBLANES, D), lambda i: (0, 0))],
        [jax.ShapeDtypeStruct((S, E), BF16), jax.ShapeDtypeStruct((S, E), BF16),
         jax.ShapeDtypeStruct((NCHIP, kb, D), BF16), jax.ShapeDtypeStruct((SUBLANES, D), F32)],
        [pltpu.VMEM((NCHIP, kb, D), BF16), pltpu.VMEM((NCHIP, kb, D), F32), pltpu.VMEM((kb, D), BF16),
         pltpu.SemaphoreType.DMA],
        _params(1, 52), (dh, y, gate, a, cs, z, w), comm)


def _attn_bwd(q, kp, vp, probs, do, prev, name, comm=None):
    S = q.shape[0]
    nsub = _nsub(S)
    R = nsub * QB
    n_i = S // R
    dt_kv = F32 if prev is None else BF16

    def body(*refs):
        q_ref, k_ref, v_ref, p_ref, do_ref = refs[:5]
        refs = refs[5:]
        if prev is not None:
            pk_hbm, pv_hbm = refs[:2]
            refs = refs[2:]
        dq_ref, dk_ref, dv_ref, dba_ref, dbb_ref, dbias, dk_acc, dv_acc = refs[:8]
        if prev is not None:
            pk_v, pv_v, sems = refs[8:]
        h = pl.program_id(0)
        i = pl.program_id(1)

        def prev_copies():
            cols = pl.ds(pl.multiple_of(h * HD, HD), HD)
            return (pltpu.make_async_copy(pk_hbm.at[:, cols], pk_v, sems.at[0]),
                    pltpu.make_async_copy(pv_hbm.at[:, cols], pv_v, sems.at[1]))

        @pl.when(i == 0)
        def _():
            if prev is not None:
                for cp in prev_copies():
                    cp.start()
            dbias[...] = jnp.zeros((2, CHUNK, DBW), F32)
            dk_acc[...] = jnp.zeros((S + PAD, HD), F32)
            dv_acc[...] = jnp.zeros((S + PAD, HD), F32)

        def mxu_in(sub):
            return _nt(do_ref[sub * QB:(sub + 1) * QB, :], v_ref[pl.ds(_row0(i, sub, nsub), WIN), :])

        nxt = mxu_in(0)
        for sub in range(nsub):
            rows = slice(sub * QB, (sub + 1) * QB)
            win = pl.ds(_row0(i, sub, nsub), WIN)
            dp = nxt
            if sub + 1 < nsub:
                nxt = mxu_in(sub + 1)
            pb = p_ref[sub]
            p = pb.astype(F32)
            ds = p * (dp - jnp.sum(p * dp, axis=-1, keepdims=True))
            for par in range(2):
                part = None
                for qc in range(par, QC, 2):
                    c0 = (qc - par) * CHUNK + BW - DBW
                    blk_ = ds[qc * CHUNK:(qc + 1) * CHUNK, c0:c0 + DBW]
                    part = blk_ if part is None else part + blk_
                dbias[par] += part
            dsb = (ds * SM_SCALE).astype(BF16)
            dq_ref[rows, :] = _nn(dsb, k_ref[win, :]).astype(BF16)
            dk_acc[win, :] += _tn(dsb, q_ref[rows, :])
            dv_acc[win, :] += _tn(pb, do_ref[rows, :])

        @pl.when(i == n_i - 1)
        def _():
            zeros = jnp.zeros((CHUNK, BW - DBW), F32)
            dba_ref[...] = jnp.concatenate([zeros, dbias[0]], axis=1)
            dbb_ref[...] = jnp.concatenate([zeros, dbias[1]], axis=1)
            if prev is None:
                dk_ref[...] = dk_acc[...]
                dv_ref[...] = dv_acc[...]
            else:
                for cp in prev_copies():
                    cp.wait()
                dk_ref[...] = (dk_acc[...] + pk_v[...]).astype(BF16)
                dv_ref[...] = (dv_acc[...] + pv_v[...]).astype(BF16)

    head = pl.BlockSpec((S + PAD, HD), lambda h, i: (0, h))
    strip = pl.BlockSpec((None, CHUNK, BW), lambda h, i: (h, 0, 0))
    blk = pl.BlockSpec((R, HD), lambda h, i: (i, h))
    in_specs = [blk, head, head, pl.BlockSpec((None, nsub, QB, WIN), lambda h, i: (h, i, 0, 0)), blk]
    scratch = [pltpu.VMEM((2, CHUNK, DBW), F32), pltpu.VMEM((S + PAD, HD), F32), pltpu.VMEM((S + PAD, HD), F32)]
    args = (q, kp, vp, probs, do)
    if prev is not None:
        in_specs += [ANY, ANY]
        scratch += [pltpu.VMEM((S + PAD, HD), F32), pltpu.VMEM((S + PAD, HD), F32), pltpu.SemaphoreType.DMA((2,))]
        args += tuple(prev)
    return _call(
        body, name, (NH, n_i), in_specs, [blk, head, head, strip, strip],
        [jax.ShapeDtypeStruct((S, E), BF16), jax.ShapeDtypeStruct((S + PAD, E), dt_kv),
         jax.ShapeDtypeStruct((S + PAD, E), dt_kv), jax.ShapeDtypeStruct((NH, CHUNK, BW), F32),
         jax.ShapeDtypeStruct((NH, CHUNK, BW), F32)],
        scratch, _params(2, 56), args, comm)


def _pool_bwd(dms, mixed, pooled, wg, a_scale, name, comm=None, tm=512):
    S = dms.shape[0]
    n_t = S // tm

    def rev(i):
        return (n_t - 1 - i, 0)

    def body(d_ref, m_ref, p_ref, wg_ref, as_ref, dv_ref, dwg_ref, st_ref, buf):
        i = pl.program_id(0)

        @pl.when(i == 0)
        def _():
            buf[tm:tm + HALO, :] = jnp.zeros((HALO, E), F32)
            dwg_ref[...] = jnp.zeros((4, GW, GW), F32)
            st_ref[...] = jnp.zeros((SUBLANES, E), F32)

        t = (n_t - 1 - i) * tm + lax.broadcasted_iota(jnp.int32, (tm, 1), 0)
        st_ref[0:1, :] += _colsum(d_ref[...].astype(F32) * m_ref[...].astype(F32))
        for gi, w in enumerate(POOL_W):
            cols = slice(gi * GW, (gi + 1) * GW)
            dm = (d_ref[:, cols].astype(F32) * as_ref[:, cols]).astype(BF16)
            dpool = _nt(dm, wg_ref[gi])
            dwg_ref[gi] += _tn(p_ref[:, cols], dm)
            inv_cnt = 1.0 / jnp.minimum(t + 1, w).astype(F32)
            buf[0:tm, cols] = dpool * inv_cnt
            s = buf[:, cols]
            k = 1
            while k < w:
                s = s + pltpu.roll(s, tm + HALO - k, 0)
                k *= 2
            dv_ref[:, cols] = (s[0:tm, :] - dpool).astype(BF16)
        buf[tm:tm + HALO, :] = buf[0:HALO, :]

    return _call(
        body, name, (n_t,),
        [pl.BlockSpec((tm, E), rev), pl.BlockSpec((tm, E), rev), pl.BlockSpec((tm, E), rev),
         pl.BlockSpec((4, GW, GW), lambda i: (0, 0, 0)), _row(E)],
        [pl.BlockSpec((tm, E), rev), pl.BlockSpec((4, GW, GW), lambda i: (0, 0, 0)),
         pl.BlockSpec((SUBLANES, E), lambda i: (0, 0))],
        [jax.ShapeDtypeStruct((S, E), BF16), jax.ShapeDtypeStruct((4, GW, GW), F32),
         jax.ShapeDtypeStruct((SUBLANES, E), F32)],
        [pltpu.VMEM((tm + HALO, E), F32)],
        _params(1, 52), (dms, mixed, pooled, wg, a_scale), comm)


def _in_bwd(da, db, row_off, u, h, g, scale, w, dh_out, name, comm=None, tm=256):
    S = h.shape[0]
    n_t = S // tm
    off = row_off // tm

    def body(da_ref, db_ref, u_ref, h_ref, g_ref, sc_ref, w_hbm, dho_ref, dhi_ref, dw_hbm, st_ref, w_v, acc, stage, sem):
        i = pl.program_id(0)

        @pl.when(i == 0)
        def _():
            cp = pltpu.make_async_copy(w_hbm, w_v, sem)
            cp.start()
            acc[...] = jnp.zeros(acc.shape, F32)
            st_ref[...] = jnp.zeros((SUBLANES, D), F32)
            cp.wait()

        ub = u_ref[...]
        du = None
        for q in range(NCHIP):
            d_ref = da_ref if q < 2 else db_ref
            dv = d_ref[:, (q % 2) * D:(q % 2 + 1) * D]
            acc[q] += _tn(ub, dv)
            part = _nt(dv, w_v[q])
            du = part if du is None else du + part

        hh = h_ref[...]
        r = lax.rsqrt(jnp.mean(hh * hh, axis=-1, keepdims=True) + EPS)
        xhat = hh * r
        gg = g_ref[...]
        st_ref[0:1, :] += _colsum(du)
        st_ref[1:2, :] += _colsum(du * (xhat * gg))
        dn = du * (1.0 + sc_ref[...])
        st_ref[2:3, :] += _colsum(dn * xhat)
        dx = dn * gg
        dhi_ref[...] = dho_ref[...] + r * (dx - xhat * jnp.mean(dx * xhat, axis=-1, keepdims=True))

        @pl.when(i == n_t - 1)
        def _():
            _store_grad(acc, stage, dw_hbm, sem)

    part_spec = pl.BlockSpec((tm, E), lambda i: (i + off, 0))
    return _call(
        body, name, (n_t,),
        [part_spec, part_spec, pl.BlockSpec((tm, D), lambda i: (i, 0)), pl.BlockSpec((tm, D), lambda i: (i, 0)),
         _row(D), _row(D), ANY, pl.BlockSpec((tm, D), lambda i: (i, 0))],
        [pl.BlockSpec((tm, D), lambda i: (i, 0)), ANY, pl.BlockSpec((SUBLANES, D), lambda i: (0, 0))],
        [jax.ShapeDtypeStruct((S, D), F32), jax.ShapeDtypeStruct((NCHIP, D, D), BF16),
         jax.ShapeDtypeStruct((SUBLANES, D), F32)],
        [pltpu.VMEM((NCHIP, D, D), BF16), pltpu.VMEM((NCHIP, D, D), F32), pltpu.VMEM((D, D), BF16),
         pltpu.SemaphoreType.DMA],
        _params(1, 56), (da, db, u, h, g, scale, w, dh_out), comm)


def _cmat(c_all, w, b, name):
    L, _, n = w.shape

    def body(c_ref, w_ref, b_ref, ca_ref, o_ref):
        cc = c_ref[...]
        ca = cc * jax.nn.sigmoid(cc)
        ca_ref[...] = ca
        o_ref[...] = _nn(ca.astype(BF16), w_ref[...].astype(BF16)) + b_ref[...]

    return pl.pallas_call(
        body, name=name, grid=(L,),
        in_specs=[pl.BlockSpec((SUBLANES, D), lambda l: (0, 0)), pl.BlockSpec((None, D, n), lambda l: (l, 0, 0)),
                  pl.BlockSpec((None, 1, n), lambda l: (l, 0, 0))],
        out_specs=[pl.BlockSpec((SUBLANES, D), lambda l: (0, 0)), pl.BlockSpec((None, SUBLANES, n), lambda l: (l, 0, 0))],
        out_shape=[jax.ShapeDtypeStruct((SUBLANES, D), F32), jax.ShapeDtypeStruct((L, SUBLANES, n), F32)],
        compiler_params=_params(1, 32),
    )(c_all, w, b)


def _grad_ada(c_act_t, dmod, name):
    L, _, n = dmod.shape

    def body(c_ref, d_ref, o_ref):
        acc = None
        for b in range(SUBLANES):
            part = c_ref[:, b:b + 1] * d_ref[b:b + 1, :]
            acc = part if acc is None else acc + part
        o_ref[...] = acc

    return pl.pallas_call(
        body, name=name, grid=(L,),
        in_specs=[pl.BlockSpec((D, SUBLANES), lambda l: (0, 0)), pl.BlockSpec((None, SUBLANES, n), lambda l: (l, 0, 0))],
        out_specs=pl.BlockSpec((None, D, n), lambda l: (l, 0, 0)),
        out_shape=jax.ShapeDtypeStruct((L, D, n), F32),
        compiler_params=_params(1, 32),
    )(c_act_t, dmod)


def _stats_reduce(g3, loss_row, name):
    n_dev, rows, _ = g3.shape

    def body(g_ref, o_ref, l_ref):
        acc = g_ref[0]
        for d in range(1, n_dev):
            acc = acc + g_ref[d]
        o_ref[...] = acc
        tot = jnp.sum(o_ref[loss_row:loss_row + 1, :], axis=-1, keepdims=True)
        l_ref[...] = jnp.broadcast_to(tot * (0.5 / D), (SUBLANES, LANES))

    return pl.pallas_call(
        body, name=name,
        in_specs=[pl.BlockSpec(memory_space=pltpu.VMEM)],
        out_specs=[pl.BlockSpec(memory_space=pltpu.VMEM), pl.BlockSpec(memory_space=pltpu.VMEM)],
        out_shape=[jax.ShapeDtypeStruct((rows, D), F32), jax.ShapeDtypeStruct((SUBLANES, LANES), F32)],
        compiler_params=pltpu.CompilerParams(vmem_limit_bytes=32 * 2 ** 20),
    )(g3)


def _sum4(own, land, chip, name, tr=256):
    _, R, C = own.shape
    tr = min(tr, R)

    def body(p_ref, own_ref, land_ref, o_ref):
        o_ref[...] = ((own_ref[...].astype(F32) + land_ref[0].astype(F32)) + land_ref[1].astype(F32)) + land_ref[2].astype(F32)

    return pl.pallas_call(
        body, name=name,
        grid_spec=pltpu.PrefetchScalarGridSpec(
            num_scalar_prefetch=1, grid=(R // tr,),
            in_specs=[pl.BlockSpec((None, tr, C), lambda i, p: (p[0], i, 0)), pl.BlockSpec((3, tr, C), lambda i, p: (0, i, 0))],
            out_specs=pl.BlockSpec((tr, C), lambda i, p: (i, 0))),
        out_shape=jax.ShapeDtypeStruct((R, C), F32),
        compiler_params=_params(1, 32),
    )(chip, own, land)


def _adamw(w, m, v, g, name, tr=256):
    L, R, C = w.shape
    tr = min(tr, R)
    stacked = not isinstance(g, (list, tuple))
    n_g = None if stacked else [len(ps) for ps in g]
    flat = [g] if stacked else [a for ps in g for a in ps]

    def body(*refs):
        w_ref, m_ref, v_ref = refs[:3]
        g_refs = refs[3:3 + len(flat)]
        go_ref, d_ref, mo_ref, vo_ref = refs[3 + len(flat):]
        if stacked:
            gg = g_refs[0][...]
        else:
            layer = pl.program_id(0)
            gg = None
            k = 0
            for li in range(L):
                gl = None
                for _ in range(n_g[li]):
                    x = g_refs[k][...]
                    gl = x if gl is None else gl + x
                    k += 1
                gg = gl if gg is None else jnp.where(layer == li, gl, gg)
        m2 = ADAM_B1 * m_ref[...] + (1.0 - ADAM_B1) * gg
        v2 = ADAM_B2 * v_ref[...] + (1.0 - ADAM_B2) * (gg * gg)
        m_hat = m2 / (1.0 - ADAM_B1 ** ADAM_STEP)
        v_hat = v2 / (1.0 - ADAM_B2 ** ADAM_STEP)
        go_ref[...] = gg
        d_ref[...] = -ADAM_LR * (m_hat / (jnp.sqrt(v_hat) + ADAM_EPS) + ADAM_WD * w_ref[...])
        mo_ref[...] = m2
        vo_ref[...] = v2

    big = pl.BlockSpec((None, tr, C), lambda l, i: (l, i, 0))
    g_specs = [big] if stacked else [pl.BlockSpec((tr, C), lambda l, i: (i, 0))] * len(flat)
    return pl.pallas_call(
        body, name=name, grid=(L, R // tr),
        in_specs=[big, big, big] + g_specs,
        out_specs=[big, big, big, big],
        out_shape=[jax.ShapeDtypeStruct((L, R, C), F32)] * 4,
        compiler_params=_params(2, 40),
    )(w, m, v, *flat)


def _allgather8(xs, name):
    m, n = xs.shape

    def body(x_ref, out_ref, send_sems, recv_sems, local_sem):
        x, y, c = _place()
        me, sibling = (x, y, c), (x, y, 1 - c)
        chips = [(1 - x, y), (x, 1 - y), (1 - x, 1 - y)]

        def rows(px, py, pc):
            return out_ref.at[pl.ds((4 * px + 2 * py + pc) * m, m), :]

        def copy(k, block, to, src=None):
            return pltpu.make_async_remote_copy(
                src_ref=rows(*block) if src is None else src, dst_ref=rows(*block),
                send_sem=send_sems.at[k], recv_sem=recv_sems.at[k], device_id=to, device_id_type=MESH)

        mine = pltpu.make_async_copy(x_ref, rows(*me), local_sem)
        mine.start()
        first = [copy(0, me, sibling, src=x_ref)]
        first += [copy(1 + j, me, (*chip, c), src=x_ref) for j, chip in enumerate(chips)]
        for cp in first:
            cp.start()
        passed = [copy(4 + j, (*chip, c), sibling) for j, chip in enumerate(chips)]
        for j, chip in enumerate(chips):
            copy(1 + j, (*chip, c), me).wait_recv()
            passed[j].start()
        copy(0, sibling, me).wait_recv()
        for j, chip in enumerate(chips):
            copy(4 + j, (*chip, 1 - c), me).wait_recv()
        for cp in first + passed:
            cp.wait_send()
        mine.wait()

    return pl.pallas_call(
        body, name=name,
        out_shape=jax.ShapeDtypeStruct((8 * m, n), xs.dtype),
        in_specs=[pl.BlockSpec(memory_space=pltpu.VMEM)],
        out_specs=pl.BlockSpec(memory_space=pltpu.VMEM),
        scratch_shapes=[pltpu.SemaphoreType.DMA((7,)), pltpu.SemaphoreType.DMA((7,)), pltpu.SemaphoreType.DMA],
        compiler_params=pltpu.CompilerParams(vmem_limit_bytes=32 * 2 ** 20),
    )(xs)


def _pad8(a):
    return jnp.pad(a, ((0, SUBLANES - a.shape[0]), (0, 0)))


def _group_rows(wg):
    return wg.transpose(1, 0, 2, 3).reshape(4, GW, GW)


def _example_step(h0, tgt, mods, kvmod, a_scale, norm_g, kv_norm_g, final_g, b_rel_bias, sh, chip_arr):
    ones_e = jnp.ones((1, E), F32)
    shift = [mods[l:l + 1, 0:D] for l in range(4)]
    scale = [mods[l:l + 1, D:2 * D] for l in range(4)]
    gate = [mods[l:l + 1, 2 * D:3 * D] for l in range(4)]
    gl = [norm_g[l:l + 1] for l in range(4)]
    kv_shift, kv_scale = kvmod[None, 0:D], kvmod[None, D:2 * D]
    kv_g = kv_norm_g[None]

    w_a = _comm_only(_Comm(gathers=[sh["a_in"][0], sh["a_grp"][0], sh["a_out"][0]]), "gather_first")
    hs = [h0]
    saved = []
    nxt = [[sh["a_in"][1], sh["a_grp"][1], sh["a_out"][1]], [sh["kv"][0], sh["b_in"][0]]]
    for l in range(2):
        w_in_l, wg_l, wo_l = w_a
        wg_full = _group_rows(wg_l)
        (u, z, pooled, mixed, y, hn), got = _a_fwd(hs[-1], gl[l], shift[l], scale[l], a_scale[l:l + 1], gate[l], w_in_l,
                                                   wg_full, wo_l, f"a{l}_fwd", comm=_Comm(gathers=nxt[l]))
        saved.append((u, z, pooled, mixed, y, w_in_l, wg_full, wo_l))
        hs.append(hn)
        if l == 0:
            w_a = got
        else:
            w_kv, wb_in0 = got

    (uk, kp, vp), (wb_out0, wb_in1) = _in_fwd(hs[2], kv_g, kv_shift, kv_scale, w_kv, BF16, BF16, "kv_in_fwd", pad_rows=PAD,
                                              comm=_Comm(gathers=[sh["b_out"][0], sh["b_in"][1]]))
    wb_in = [wb_in0, wb_in1]
    wb_out = [wb_out0, None]

    for bi in range(2):
        l = 2 + bi
        sa, sb = _bias_build(jnp.pad(b_rel_bias[bi], ((0, 0), (0, NRELP - NREL))), f"b{bi}_bias")
        (u, q, z), got = _in_fwd(hs[-1], gl[l], shift[l], scale[l], wb_in[bi], BF16, BF16, f"b{bi}_in_fwd",
                                 comm=_Comm(gathers=[sh["b_out"][1]]) if bi == 0 else None)
        if bi == 0:
            (wb_out[1],) = got
        (att, probs), _ = _attn_fwd(q, kp, vp, sa.transpose(1, 0, 2), sb.transpose(1, 0, 2), f"b{bi}_attn_fwd")
        (y, hn), _ = _out_fwd(att, z, wb_out[bi], gate[l], hs[-1], f"b{bi}_out_fwd")
        saved.append((u, z, q, att, y, probs))
        hs.append(hn)

    dh, st_fin = _final(hs[4], final_g[None], tgt, "final")

    st_in = [None] * 4
    st_out = [None] * 4
    grads = {}
    landed = {}

    def carry(names):
        return _Comm(scatters=[grads[n] for n in names]) if names else None

    def land(names, got):
        for n, a in zip(names, got):
            landed[n] = a

    u, z, q, att, y, probs = saved[3]
    (datt, dz, grads["b_out1"], st_out[3]), _ = _out_bwd(dh, y, gate[3], att, ones_e, z, wb_out[1], "b1_out_bwd")
    (dq, dk1, dv1, dsa, dsb), got = _attn_bwd(q, kp, vp, probs, datt, None, "b1_attn_bwd", comm=carry(["b_out1"]))
    land(["b_out1"], got)
    drb1 = _dbias_reduce(dsa.transpose(1, 0, 2), dsb.transpose(1, 0, 2), "b1_dbias")
    (dh, grads["b_in1"], st_in[3]), _ = _in_bwd(dq, dz, 0, u, hs[3], gl[3], scale[3], wb_in[1], dh, "b1_in_bwd")
    u, z, q, att, y, probs = saved[2]
    (datt, dz, grads["b_out0"], st_out[2]), _ = _out_bwd(dh, y, gate[2], att, ones_e, z, wb_out[0], "b0_out_bwd")
    (dq, dk, dv, dsa, dsb), got = _attn_bwd(q, kp, vp, probs, datt, (dk1, dv1), "b0_attn_bwd",
                                            comm=carry(["b_in1", "b_out0"]))
    land(["b_in1", "b_out0"], got)
    drb0 = _dbias_reduce(dsa.transpose(1, 0, 2), dsb.transpose(1, 0, 2), "b0_dbias")
    (dh, grads["b_in0"], st_in[2]), _ = _in_bwd(dq, dz, 0, u, hs[2], gl[2], scale[2], wb_in[0], dh, "b0_in_bwd")
    (dh, grads["kv"], st_kv), got = _in_bwd(dk, dv, PAD, uk, hs[2], kv_g, kv_scale, w_kv, dh, "kv_in_bwd",
                                            comm=carry(["b_in0"]))
    land(["b_in0"], got)
    st_pool = [None] * 2
    plan = {1: dict(o=[], p=["a_out1"], i=["kv", "a_grp1"]), 0: dict(o=["a_in1"], p=["a_out0"], i=[])}
    early = ["b_out1", "b_in1", "b_out0", "b_in0", "kv", "a_out1", "a_grp1", "a_in1"]
    late = ["a_out0", "a_grp0", "a_in0"]
    both = {}

    def sum4(n):
        return _sum4(grads[n], landed[n], chip_arr, f"sum4_{n}")

    for l in (1, 0):
        u, z, pooled, mixed, y, w_in_l, wg_full, wo = saved[l]
        asl = a_scale[l:l + 1]
        (dms, dz, grads[f"a_out{l}"], st_out[l]), got = _out_bwd(dh, y, gate[l], mixed, asl, z, wo, f"a{l}_out_bwd",
                                                                comm=carry(plan[l]["o"]))
        land(plan[l]["o"], got)
        comm = carry(plan[l]["p"])
        if l == 0:
            mine = [sum4(n) for n in early]
            comm = _Comm(scatters=[grads[n] for n in plan[l]["p"]], swaps=mine)
        (dval, dwg, st_pool[l]), got = _pool_bwd(dms, mixed, pooled, wg_full, asl, f"a{l}_pool_bwd", comm=comm)
        land(plan[l]["p"], got)
        if l == 0:
            both.update({n: [a, b] for n, a, b in zip(early, mine, got[len(plan[l]["p"]):])})
        grads[f"a_grp{l}"] = (dwg.reshape(4, NCHIP, GW // NCHIP, GW).transpose(1, 0, 2, 3).reshape(NCHIP, GW, GW)
                              .astype(BF16))
        (dh, grads[f"a_in{l}"], st_in[l]), got = _in_bwd(dval, dz, 0, u, hs[l], gl[l], scale[l], w_in_l, dh, f"a{l}_in_bwd",
                                                         comm=carry(plan[l]["i"]))
        land(plan[l]["i"], got)
    land(["a_grp0", "a_in0"], _comm_only(carry(["a_grp0", "a_in0"]), "scatter_last"))
    mine = [sum4(n) for n in late]
    both.update({n: [a, b] for n, a, b in zip(late, mine, _comm_only(_Comm(swaps=mine), "swap_last"))})

    pieces = st_in + [st_kv] + st_out + [st_fin]
    pieces += [_pad8(st_pool[l][0].reshape(2, D)) for l in range(2)]
    pieces += [_pad8(d.reshape(NH * NRELP // D, D)) for d in (drb0, drb1)]
    stats = jnp.concatenate(pieces, axis=0)
    return dh, both, stats


ROW_IN = [8 * l for l in range(4)]
ROW_KV = 32
ROW_OUT = [40 + 8 * l for l in range(4)]
ROW_FIN = 72
ROW_ASC = [80, 88]
ROW_RB = [96, 104]
N_STAT = 112


def kernel(x, c, ada_w, ada_b, norm_g, a_w_in, a_w_group, a_scale, a_w_out, kv_norm_g, kv_ada_w, kv_ada_b, w_kv, b_w_in, b_rel_bias, b_w_out, final_g, loss_target, m_ada_w, m_ada_b, m_norm_g, m_a_w_in, m_a_w_group, m_a_scale, m_a_w_out, m_kv_norm_g, m_kv_ada_w, m_kv_ada_b, m_w_kv, m_b_w_in, m_b_rel_bias, m_b_w_out, m_final_g, v_ada_w, v_ada_b, v_norm_g, v_a_w_in, v_a_w_group, v_a_scale, v_a_w_out, v_kv_norm_g, v_kv_ada_w, v_kv_ada_b, v_w_kv, v_b_w_in, v_b_rel_bias, v_b_w_out, v_final_g):
    xi, yi, ci = _place()
    chip = 2 * xi + yi
    dev = 4 * xi + 2 * yi + ci
    n_ada = ada_w.shape[2]
    n_kva = kv_ada_w.shape[1]
    n_asc = a_scale.shape[1]

    c_all = _allgather8(jnp.broadcast_to(c, (SUBLANES, D)), "gather_c")[::SUBLANES]
    ada_b_sh = lax.dynamic_slice_in_dim(ada_b, chip * n_ada, n_ada, axis=1)
    kvb_sh = lax.dynamic_slice_in_dim(kv_ada_b, chip * n_kva, n_kva, axis=0)
    c_act, mod_ada = _cmat(c_all, ada_w, ada_b_sh[:, None, :], "mod_ada")
    _, mod_kv = _cmat(c_all, kv_ada_w[None], kvb_sh[None, None, :], "mod_kv")
    part = jnp.concatenate([mod_ada.transpose(1, 0, 2).reshape(SUBLANES, 4 * n_ada), mod_kv[0],
                            jnp.broadcast_to(a_scale.reshape(1, 2 * n_asc), (SUBLANES, 2 * n_asc))], axis=1)
    gathered = _allgather8(part, "gather_mod")
    rows = jnp.concatenate([lax.dynamic_slice_in_dim(gathered, SUBLANES * (2 * p + ci) + dev, 1, axis=0)
                            for p in range(NCHIP)], axis=0)
    mods = jnp.stack([rows[:, l * n_ada:(l + 1) * n_ada].reshape(3 * D) for l in range(4)])
    kvmod = rows[:, 4 * n_ada:4 * n_ada + n_kva].reshape(2 * D)
    o_asc = 4 * n_ada + n_kva
    a_scale_full = jnp.stack([rows[:, o_asc + l * n_asc:o_asc + (l + 1) * n_asc].reshape(E) for l in range(2)])

    sh = dict(a_in=[a_w_in[l].astype(BF16) for l in range(2)], a_grp=[a_w_group[l].astype(BF16) for l in range(2)],
              a_out=[a_w_out[l].astype(BF16) for l in range(2)], kv=[w_kv.astype(BF16)],
              b_in=[b_w_in[l].astype(BF16) for l in range(2)], b_out=[b_w_out[l].astype(BF16) for l in range(2)])
    chip_arr = jnp.reshape(chip, (1,)).astype(jnp.int32)
    dh, both, stats = _example_step(x[0], loss_target[0], mods, kvmod, a_scale_full, norm_g, kv_norm_g, final_g,
                                    b_rel_bias, sh, chip_arr)
    grad_x = dh[None]

    g3 = _allgather8(stats, "gather_stats").reshape(8, N_STAT, D)
    red, loss_tile = _stats_reduce(g3, ROW_FIN + 1, "stats_reduce")
    loss = loss_tile[0, 0]

    def cat(rows_):
        return jnp.concatenate(rows_, axis=-1)

    g_ada_b = jnp.stack([cat([red[ROW_IN[l]], red[ROW_IN[l] + 1], red[ROW_OUT[l]]]) for l in range(4)])
    g_norm_g = jnp.stack([red[ROW_IN[l] + 2] for l in range(4)])
    g_kv_norm_g = red[ROW_KV + 2]
    g_kv_ada_b = cat([red[ROW_KV], red[ROW_KV + 1]])
    g_final_g = red[ROW_FIN]
    g_asc_full = jnp.stack([red[ROW_ASC[l]:ROW_ASC[l] + 2].reshape(E) for l in range(2)])
    g_a_scale = lax.dynamic_slice_in_dim(g_asc_full, chip * n_asc, n_asc, axis=1)
    g_rel = jnp.stack([red[ROW_RB[bi]:ROW_RB[bi] + NH * NRELP // D].reshape(NH, NRELP)[:, :NREL] for bi in range(2)])

    dmod = jnp.stack([cat([g3[:, ROW_IN[l]], g3[:, ROW_IN[l] + 1], g3[:, ROW_OUT[l]]]) for l in range(4)])
    dmod_sh = lax.dynamic_slice_in_dim(dmod, chip * n_ada, n_ada, axis=2)
    dkv = cat([g3[:, ROW_KV], g3[:, ROW_KV + 1]])[None]
    dkv_sh = lax.dynamic_slice_in_dim(dkv, chip * n_kva, n_kva, axis=2)
    c_act_t = c_act.T
    g_ada_w = _grad_ada(c_act_t, dmod_sh, "grad_ada_w")
    g_kv_ada_w = _grad_ada(c_act_t, dkv_sh, "grad_kv_ada_w")

    def upd(w, m, v, g, name, shape3):
        g = g.reshape(shape3) if not isinstance(g, list) else g
        outs = _adamw(w.reshape(shape3), m.reshape(shape3), v.reshape(shape3), g, name)
        return [o.reshape(w.shape) for o in outs]

    def pair(name):
        return [both[name + "0"], both[name + "1"]]

    res = {}
    res["ada_w"] = upd(ada_w, m_ada_w, v_ada_w, g_ada_w, "adamw_ada_w", ada_w.shape)
    res["ada_b"] = upd(ada_b, m_ada_b, v_ada_b, g_ada_b, "adamw_ada_b", (1,) + ada_b.shape)
    res["norm_g"] = upd(norm_g, m_norm_g, v_norm_g, g_norm_g, "adamw_norm_g", (1,) + norm_g.shape)
    res["a_w_in"] = upd(a_w_in, m_a_w_in, v_a_w_in, pair("a_in"), "adamw_a_w_in", a_w_in.shape)
    res["a_w_group"] = upd(a_w_group, m_a_w_group, v_a_w_group, pair("a_grp"), "adamw_a_w_group", (2, GW, GW))
    res["a_scale"] = upd(a_scale, m_a_scale, v_a_scale, g_a_scale, "adamw_a_scale", (1,) + a_scale.shape)
    res["a_w_out"] = upd(a_w_out, m_a_w_out, v_a_w_out, pair("a_out"), "adamw_a_w_out", a_w_out.shape)
    res["kv_norm_g"] = upd(kv_norm_g, m_kv_norm_g, v_kv_norm_g, g_kv_norm_g, "adamw_kv_norm_g", (1, 1, D))
    res["kv_ada_w"] = upd(kv_ada_w, m_kv_ada_w, v_kv_ada_w, g_kv_ada_w, "adamw_kv_ada_w", (1,) + kv_ada_w.shape)
    res["kv_ada_b"] = upd(kv_ada_b, m_kv_ada_b, v_kv_ada_b, g_kv_ada_b, "adamw_kv_ada_b", (1, 1, 2 * D))
    res["w_kv"] = upd(w_kv, m_w_kv, v_w_kv, [both["kv"]], "adamw_w_kv", (1,) + w_kv.shape)
    res["b_w_in"] = upd(b_w_in, m_b_w_in, v_b_w_in, pair("b_in"), "adamw_b_w_in", b_w_in.shape)
    res["b_rel_bias"] = upd(b_rel_bias, m_b_rel_bias, v_b_rel_bias, g_rel, "adamw_b_rel_bias", (1, 2 * NH, NREL))
    res["b_w_out"] = upd(b_w_out, m_b_w_out, v_b_w_out, pair("b_out"), "adamw_b_w_out", b_w_out.shape)
    res["final_g"] = upd(final_g, m_final_g, v_final_g, g_final_g, "adamw_final_g", (1, 1, D))

    names = ["ada_w", "ada_b", "norm_g", "a_w_in", "a_w_group", "a_scale", "a_w_out", "kv_norm_g", "kv_ada_w", "kv_ada_b",
             "w_kv", "b_w_in", "b_rel_bias", "b_w_out", "final_g"]
    return (loss, grad_x, *[res[n][0] for n in names], *[res[n][1] for n in names], *[res[n][2] for n in names],
            *[res[n][3] for n in names])
```

```python
import math

import jax
import jax.numpy as jnp
from jax import lax
from jax.experimental import pallas as pl
from jax.experimental.pallas import tpu as pltpu

F32 = jnp.float32
BF16 = jnp.bfloat16

D = 1024
E = 2048
NH = 16
HD = 128
CHUNK = 64
LEFT = 8
PAD = LEFT * CHUNK
NREL = 257
NRELP = 384
REL_CLIP = 128
EPS = 1e-6
NEG = -1e30
LOG2E = math.log2(math.e)
SM_SCALE = HD ** -0.5
POOL_W = (2, 4, 8, 16)
GW = 512
HALO = 16
QC = 4
QB = QC * CHUNK
NMASK = PAD // QB
WIN = (QC + LEFT) * CHUNK
BW = (LEFT + 2) * CHUNK
DBW = 4 * CHUNK
NSUB = 8
NCHIP = 4
LANES = 128
SUBLANES = 8

ADAM_LR = 0.001
ADAM_B1 = 0.9
ADAM_B2 = 0.999
ADAM_EPS = 1e-08
ADAM_WD = 0.01
ADAM_STEP = 10

MESH = pl.DeviceIdType.MESH
ANY = pl.BlockSpec(memory_space=pl.ANY)


def _params(n_axes, vmem_mb):
    return pltpu.CompilerParams(dimension_semantics=("arbitrary",) * n_axes, vmem_limit_bytes=vmem_mb * 2 ** 20)


def _nn(a, b):
    return jnp.dot(a, b, preferred_element_type=F32)


def _nt(a, b):
    return lax.dot_general(a, b, (((1,), (1,)), ((), ())), preferred_element_type=F32)


def _tn(a, b):
    return lax.dot_general(a, b, (((0,), (0,)), ((), ())), preferred_element_type=F32)


def _row(n):
    return pl.BlockSpec((1, n), lambda i: (0, 0))


def _colsum(x):
    return jnp.sum(x, axis=0, keepdims=True)


def _place():
    return lax.axis_index("x"), lax.axis_index("y"), lax.axis_index("c")


class _Comm:
    def __init__(self, gathers=(), scatters=(), swaps=()):
        self.n_g = len(gathers)
        self.n_chip = len(gathers) + len(scatters)
        self.n_sw = len(swaps)
        self.arrays = list(gathers) + list(scatters) + list(swaps)
        self.n = len(self.arrays)
        self.half = [a.shape[0] // 2 for a in gathers]
        self.out_shape = ([jax.ShapeDtypeStruct((NCHIP,) + a.shape, a.dtype) for a in gathers]
                          + [jax.ShapeDtypeStruct((3,) + a.shape[1:], a.dtype) for a in scatters]
                          + [jax.ShapeDtypeStruct(a.shape, a.dtype) for a in swaps])
        n_c, n_f, n_s = max(3 * self.n_chip, 1), max(3 * self.n_g, 1), max(self.n_sw, 1)
        self.scratch = [pltpu.SemaphoreType.DMA((n_c,)), pltpu.SemaphoreType.DMA((n_c,)),
                        pltpu.SemaphoreType.DMA((max(self.n_g, 1),)), pltpu.SemaphoreType.DMA((n_f,)),
                        pltpu.SemaphoreType.DMA((n_f,)), pltpu.SemaphoreType.DMA((n_s,)), pltpu.SemaphoreType.DMA((n_s,))]

    def _chip_copies(self, ins, outs, send, recv, landing):
        x, y, c = _place()
        chips = [(1 - x, y), (x, 1 - y), (1 - x, 1 - y)]
        mine = 2 * x + y
        cps = []
        for k in range(self.n_chip):
            for j, (cx, cy) in enumerate(chips):
                q = 2 * cx + cy
                if k < self.n_g:
                    part = pl.ds(c * self.half[k], self.half[k])
                    src = ins[k].at[part]
                    dst = outs[k].at[q if landing else mine, part]
                else:
                    src = ins[k].at[q]
                    dst = outs[k].at[j]
                cps.append(pltpu.make_async_remote_copy(
                    src_ref=src, dst_ref=dst, send_sem=send.at[3 * k + j], recv_sem=recv.at[3 * k + j],
                    device_id=(cx, cy, c), device_id_type=MESH))
        return cps

    def _core_copies(self, outs, fsend, frecv, landing):
        x, y, c = _place()
        chips = [(1 - x, y), (x, 1 - y), (1 - x, 1 - y)]
        cps = []
        for k in range(self.n_g):
            for j, (cx, cy) in enumerate(chips):
                part = pl.ds((1 - c if landing else c) * self.half[k], self.half[k])
                blk = outs[k].at[2 * cx + cy, part]
                cps.append(pltpu.make_async_remote_copy(
                    src_ref=blk, dst_ref=blk, send_sem=fsend.at[3 * k + j], recv_sem=frecv.at[3 * k + j],
                    device_id=(x, y, 1 - c), device_id_type=MESH))
        return cps

    def _local_copies(self, ins, outs, loc):
        x, y, _ = _place()
        return [pltpu.make_async_copy(ins[k], outs[k].at[2 * x + y], loc.at[k]) for k in range(self.n_g)]

    def _swap_copies(self, ins, outs, ssend, srecv):
        x, y, c = _place()
        return [pltpu.make_async_remote_copy(
            src_ref=ins[k], dst_ref=outs[k], send_sem=ssend.at[k - self.n_chip], recv_sem=srecv.at[k - self.n_chip],
            device_id=(x, y, 1 - c), device_id_type=MESH) for k in range(self.n_chip, self.n)]

    def start(self, ins, outs, send, recv, loc, fsend, frecv, ssend, srecv):
        for cp in (self._local_copies(ins, outs, loc) + self._chip_copies(ins, outs, send, recv, False)
                   + self._swap_copies(ins, outs, ssend, srecv)):
            cp.start()

    def wait(self, ins, outs, send, recv, loc, fsend, frecv, ssend, srecv):
        lands = self._chip_copies(ins, outs, send, recv, True)
        passes = self._core_copies(outs, fsend, frecv, False)
        for k in range(self.n_chip):
            for j in range(3):
                lands[3 * k + j].wait_recv()
                if k < self.n_g:
                    passes[3 * k + j].start()
        for cp in self._core_copies(outs, fsend, frecv, True):
            cp.wait_recv()
        swaps = self._swap_copies(ins, outs, ssend, srecv)
        for cp in swaps:
            cp.wait_recv()
        for cp in self._chip_copies(ins, outs, send, recv, False) + passes + swaps:
            cp.wait_send()
        for cp in self._local_copies(ins, outs, loc):
            cp.wait()


def _call(body, name, grid, in_specs, out_specs, out_shape, scratch, params, args, comm=None):
    n_in, n_out, n_sc = len(in_specs), len(out_specs), len(scratch)
    if comm is None:
        outs = pl.pallas_call(body, name=name, grid=grid, in_specs=in_specs, out_specs=out_specs, out_shape=out_shape,
                              scratch_shapes=scratch, compiler_params=params)(*args)
        return list(outs), []
    n = comm.n
    o0 = n_in + n
    s0 = o0 + n_out + n

    def wrapped(*refs):
        c_refs = (refs[n_in:o0], refs[o0 + n_out:s0]) + tuple(refs[s0 + n_sc:])
        ids = [pl.program_id(a) for a in range(len(grid))]
        first = ids[0] == 0
        last = ids[0] == grid[0] - 1
        for a in range(1, len(grid)):
            first = first & (ids[a] == 0)
            last = last & (ids[a] == grid[a] - 1)

        @pl.when(first)
        def _():
            comm.start(*c_refs)

        body(*refs[:n_in], *refs[o0:o0 + n_out], *refs[s0:s0 + n_sc])

        @pl.when(last)
        def _():
            comm.wait(*c_refs)

    outs = pl.pallas_call(
        wrapped, name=name, grid=grid, in_specs=list(in_specs) + [ANY] * n, out_specs=list(out_specs) + [ANY] * n,
        out_shape=list(out_shape) + comm.out_shape, scratch_shapes=list(scratch) + comm.scratch, compiler_params=params,
    )(*args, *comm.arrays)
    return list(outs[:n_out]), list(outs[n_out:])


def _comm_only(comm, name):
    def body(*refs):
        c_refs = (refs[:comm.n], refs[comm.n:2 * comm.n]) + tuple(refs[2 * comm.n:])
        comm.start(*c_refs)
        comm.wait(*c_refs)

    return pl.pallas_call(body, name=name, in_specs=[ANY] * comm.n, out_specs=[ANY] * comm.n, out_shape=comm.out_shape,
                          scratch_shapes=comm.scratch)(*comm.arrays)


def _in_fwd(h, g, shift, scale, w, dt_a, dt_b, name, pad_rows=0, comm=None, tm=512):
    S = h.shape[0]
    n_pad = pad_rows // tm

    def body(h_ref, g_ref, sh_ref, sc_ref, w_hbm, u_ref, oa_ref, ob_ref, w_v, sem):
        i = pl.program_id(0)

        @pl.when(i == 0)
        def _():
            cp = pltpu.make_async_copy(w_hbm, w_v, sem)
            cp.start()
            cp.wait()

        hh = h_ref[...]
        r = lax.rsqrt(jnp.mean(hh * hh, axis=-1, keepdims=True) + EPS)
        u = (hh * r * g_ref[...]) * (1.0 + sc_ref[...]) + sh_ref[...]
        ub = u.astype(BF16)
        u_ref[...] = ub
        for q in range(NCHIP):
            o_ref = oa_ref if q < 2 else ob_ref
            o_ref[:, (q % 2) * D:(q % 2 + 1) * D] = _nn(ub, w_v[q]).astype(o_ref.dtype)

        if n_pad:
            @pl.when(i < n_pad)
            def _():
                oa_ref[...] = jnp.zeros(oa_ref.shape, oa_ref.dtype)
                ob_ref[...] = jnp.zeros(ob_ref.shape, ob_ref.dtype)

    def src(i):
        return (jnp.maximum(i - n_pad, 0), 0)

    outs, landed = _call(
        body, name, (S // tm + n_pad,),
        [pl.BlockSpec((tm, D), src), _row(D), _row(D), _row(D), ANY],
        [pl.BlockSpec((tm, D), src), pl.BlockSpec((tm, E), lambda i: (i, 0)), pl.BlockSpec((tm, E), lambda i: (i, 0))],
        [jax.ShapeDtypeStruct((S, D), BF16), jax.ShapeDtypeStruct((S + pad_rows, E), dt_a),
         jax.ShapeDtypeStruct((S + pad_rows, E), dt_b)],
        [pltpu.VMEM((NCHIP, D, D), BF16), pltpu.SemaphoreType.DMA],
        _params(1, 52), (h, g, shift, scale, w), comm)
    return outs, landed


def _a_fwd(h, g, shift, scale, asc, gate, w_in, wg, w_out, name, comm=None, tm=512):
    S = h.shape[0]

    def body(h_ref, g_ref, sh_ref, sc_ref, as_ref, gate_ref, wi_hbm, wg_hbm, wo_hbm,
             u_ref, z_ref, p_ref, m_ref, y_ref, ho_ref, wi_v, wg_v, wo_v, buf, sems):
        i = pl.program_id(0)

        @pl.when(i == 0)
        def _():
            cps = [pltpu.make_async_copy(wi_hbm, wi_v, sems.at[0]), pltpu.make_async_copy(wg_hbm, wg_v, sems.at[1]),
                   pltpu.make_async_copy(wo_hbm, wo_v, sems.at[2])]
            for cp in cps:
                cp.start()
            buf[0:HALO, :] = jnp.zeros((HALO, E), F32)
            for cp in cps:
                cp.wait()

        hh = h_ref[...]
        r = lax.rsqrt(jnp.mean(hh * hh, axis=-1, keepdims=True) + EPS)
        ub = ((hh * r * g_ref[...]) * (1.0 + sc_ref[...]) + sh_ref[...]).astype(BF16)
        u_ref[...] = ub
        for q in range(2):
            buf[HALO:HALO + tm, q * D:(q + 1) * D] = _nn(ub, wi_v[q])
        t = i * tm + lax.broadcasted_iota(jnp.int32, (tm, 1), 0)
        y = None
        for gi, w in enumerate(POOL_W):
            cols = slice(gi * GW, (gi + 1) * GW)
            x = buf[:, cols]
            s = x
            k = 1
            while k < w:
                s = s + pltpu.roll(s, k, 0)
                k *= 2
            inv_cnt = 1.0 / jnp.minimum(t + 1, w).astype(F32)
            pb = (s[HALO:, :] * inv_cnt - x[HALO:, :]).astype(BF16)
            p_ref[:, cols] = pb
            mb = _nn(pb, wg_v[gi]).astype(BF16)
            m_ref[:, cols] = mb
            zb = _nn(ub, wi_v[2 + gi // 2, :, (gi % 2) * GW:(gi % 2 + 1) * GW]).astype(BF16)
            z_ref[:, cols] = zb
            zz = zb.astype(F32)
            act = ((mb.astype(F32) * as_ref[:, cols]) * (zz * jax.nn.sigmoid(zz))).astype(BF16)
            part = _nn(act, wo_v[gi])
            y = part if y is None else y + part
        buf[0:HALO, :] = buf[tm:tm + HALO, :]
        y_ref[...] = y.astype(BF16)
        ho_ref[...] = hh + gate_ref[...] * y

    rows_d = pl.BlockSpec((tm, D), lambda i: (i, 0))
    rows_e = pl.BlockSpec((tm, E), lambda i: (i, 0))
    return _call(
        body, name, (S // tm,),
        [rows_d, _row(D), _row(D), _row(D), _row(E), _row(D), ANY, ANY, ANY],
        [rows_d, rows_e, rows_e, rows_e, rows_d, rows_d],
        [jax.ShapeDtypeStruct((S, D), BF16), jax.ShapeDtypeStruct((S, E), BF16), jax.ShapeDtypeStruct((S, E), BF16),
         jax.ShapeDtypeStruct((S, E), BF16), jax.ShapeDtypeStruct((S, D), BF16), jax.ShapeDtypeStruct((S, D), F32)],
        [pltpu.VMEM((NCHIP, D, D), BF16), pltpu.VMEM((4, GW, GW), BF16), pltpu.VMEM((NCHIP, GW, D), BF16),
         pltpu.VMEM((tm + HALO, E), F32), pltpu.SemaphoreType.DMA((3,))],
        _params(1, 60), (h, g, shift, scale, asc, gate, w_in, wg, w_out), comm)


def _out_fwd(a, z, w, gate, h, name, head=None, comm=None, tm=512):
    S = h.shape[0]
    kb = E // NCHIP
    n_in = 5 if head is None else 7

    def body(*refs):
        a_ref, z_ref, w_hbm, gate_ref, h_ref = refs[:5]
        w_v, sem = refs[-2:]
        i = pl.program_id(0)

        @pl.when(i == 0)
        def _():
            cp = pltpu.make_async_copy(w_hbm, w_v, sem)
            cp.start()
            cp.wait()

        y = None
        for p in range(NCHIP):
            cols = slice(p * kb, (p + 1) * kb)
            zz = z_ref[:, cols].astype(F32)
            act = (a_ref[:, cols].astype(F32) * (zz * jax.nn.sigmoid(zz))).astype(BF16)
            part = _nn(act, w_v[p])
            y = part if y is None else y + part
        refs[n_in][...] = y.astype(BF16)
        hh = h_ref[...] + gate_ref[...] * y
        if head is None:
            refs[n_in + 1][...] = hh
            return
        g_ref, t_ref = refs[5:7]
        dh_ref, st_ref = refs[n_in + 1:n_in + 3]

        @pl.when(i == 0)
        def _():
            st_ref[...] = jnp.zeros((SUBLANES, D), F32)

        r = lax.rsqrt(jnp.mean(hh * hh, axis=-1, keepdims=True) + EPS)
        xhat = hh * r
        diff = xhat * g_ref[...] - t_ref[...]
        st_ref[1:2, :] += _colsum(diff * diff)
        dout = diff * (1.0 / D)
        st_ref[0:1, :] += _colsum(dout * xhat)
        dx = dout * g_ref[...]
        dh_ref[...] = r * (dx - xhat * jnp.mean(dx * xhat, axis=-1, keepdims=True))

    rows_d = pl.BlockSpec((tm, D), lambda i: (i, 0))
    rows_e = pl.BlockSpec((tm, E), lambda i: (i, 0))
    in_specs = [rows_e, rows_e, ANY, _row(D), rows_d]
    out_specs = [rows_d, rows_d]
    out_shape = [jax.ShapeDtypeStruct((S, D), BF16), jax.ShapeDtypeStruct((S, D), F32)]
    args = (a, z, w, gate, h)
    if head is not None:
        in_specs += [_row(D), rows_d]
        out_specs += [pl.BlockSpec((SUBLANES, D), lambda i: (0, 0))]
        out_shape += [jax.ShapeDtypeStruct((SUBLANES, D), F32)]
        args += tuple(head)
    return _call(body, name, (S // tm,), in_specs, out_specs, out_shape,
                 [pltpu.VMEM((NCHIP, kb, D), BF16), pltpu.SemaphoreType.DMA], _params(1, 52), args, comm)


TW = BW + LANES


def _diag_onehot(transpose):
    shape = (TW, NRELP) if transpose else (NRELP, TW)
    j = lax.broadcasted_iota(jnp.int32, shape, 0 if transpose else 1)
    r = lax.broadcasted_iota(jnp.int32, shape, 1 if transpose else 0)
    idx = jnp.clip(PAD - (j - LANES), -REL_CLIP, REL_CLIP) + REL_CLIP
    return jnp.where(idx == r, 1.0, 0.0).astype(BF16)


def _strip_valid():
    m = lax.broadcasted_iota(jnp.int32, (NH, BW), 1)
    return m < (LEFT + 1) * CHUNK, m >= CHUNK


def _bias_build(rb, name):
    def body(rb_ref, a_ref, b_ref):
        x = rb_ref[...]
        hi = x.astype(BF16)
        r1 = x - hi.astype(F32)
        mid = r1.astype(BF16)
        lo = (r1 - mid.astype(F32)).astype(BF16)
        oh = _diag_onehot(False)
        diag = (_nn(hi, oh) + _nn(mid, oh)) + _nn(lo, oh)
        valid_a, valid_b = _strip_valid()
        for qi in range(CHUNK):
            a_ref[qi] = jnp.where(valid_a, pltpu.roll(diag, TW - (LANES - qi), 1)[:, :BW], NEG)
            b_ref[qi] = jnp.where(valid_b, pltpu.roll(diag, TW - (CHUNK - qi), 1)[:, :BW], NEG)

    vmem = pl.BlockSpec(memory_space=pltpu.VMEM)
    return pl.pallas_call(
        body, name=name, in_specs=[vmem], out_specs=[vmem, vmem],
        out_shape=[jax.ShapeDtypeStruct((CHUNK, NH, BW), F32), jax.ShapeDtypeStruct((CHUNK, NH, BW), F32)],
        compiler_params=pltpu.CompilerParams(vmem_limit_bytes=32 * 2 ** 20),
    )(rb)


def _dbias_reduce(dba, dbb, name):
    def body(a_ref, b_ref, o_ref):
        valid_a, valid_b = _strip_valid()
        zeros = jnp.zeros((NH, TW - BW), F32)
        acc = jnp.zeros((NH, TW), F32)
        for qi in range(CHUNK):
            xa = jnp.concatenate([jnp.where(valid_a, a_ref[qi], 0.0), zeros], axis=1)
            xb = jnp.concatenate([jnp.where(valid_b, b_ref[qi], 0.0), zeros], axis=1)
            acc = acc + (pltpu.roll(xa, LANES - qi, 1) + pltpu.roll(xb, CHUNK - qi, 1))
        oh = _diag_onehot(True)
        hi = acc.astype(BF16)
        mid = (acc - hi.astype(F32)).astype(BF16)
        r = lax.broadcasted_iota(jnp.int32, (NH, NRELP), 1)
        near = jnp.where(r < 2 * REL_CLIP, _nn(hi, oh) + _nn(mid, oh), 0.0)
        o_ref[...] = jnp.where(r == 2 * REL_CLIP, -jnp.sum(near, axis=-1, keepdims=True), near)

    vmem = pl.BlockSpec(memory_space=pltpu.VMEM)
    return pl.pallas_call(
        body, name=name, in_specs=[vmem, vmem], out_specs=vmem,
        out_shape=jax.ShapeDtypeStruct((NH, NRELP), F32),
        compiler_params=pltpu.CompilerParams(vmem_limit_bytes=32 * 2 ** 20),
    )(dba, dbb)


def _build_bias(bias3, ba_ref, bb_ref):
    bias3[NMASK] = jnp.full((QB, WIN), NEG, F32)
    for qc in range(QC):
        rows = slice(qc * CHUNK, (qc + 1) * CHUNK)
        if qc % 2 == 0:
            bias3[NMASK, rows, qc * CHUNK:qc * CHUNK + BW] = ba_ref[...] * LOG2E
        else:
            bias3[NMASK, rows, (qc - 1) * CHUNK:(qc - 1) * CHUNK + BW] = bb_ref[...] * LOG2E
    col = lax.broadcasted_iota(jnp.int32, (QB, WIN), 1)
    for sub in range(NMASK):
        bias3[sub] = jnp.where(col < PAD - sub * QB, NEG, bias3[NMASK])


def _nsub(S):
    n = min(NSUB, S // QB)
    assert S % (n * QB) == 0 and n >= NMASK
    return n


def _row0(i, sub, nsub):
    return pl.multiple_of((i * nsub + sub) * QB, QB)


def _scores(q_ref, k_ref, i, sub, nsub):
    return _nt(q_ref[sub * QB:(sub + 1) * QB, :], k_ref[pl.ds(_row0(i, sub, nsub), WIN), :])


def _exp_parts(s, bias3, i, sub):
    which = jnp.where(i == 0, sub, NMASK) if sub < NMASK else NMASK
    s = s * (SM_SCALE * LOG2E) + bias3[which]
    e = jnp.exp2(s - jnp.max(s, axis=-1, keepdims=True))
    return e, jnp.sum(e, axis=-1, keepdims=True)


def _attn_fwd(q, kp, vp, ba, bb, name, comm=None):
    S = q.shape[0]
    nsub = _nsub(S)
    R = nsub * QB

    def body(q_ref, k_ref, v_ref, ba_ref, bb_ref, o_ref, p_ref, bias3):
        i = pl.program_id(1)

        @pl.when(i == 0)
        def _():
            _build_bias(bias3, ba_ref, bb_ref)

        s_next = _scores(q_ref, k_ref, i, 0, nsub)
        for sub in range(nsub):
            s = s_next
            if sub + 1 < nsub:
                s_next = _scores(q_ref, k_ref, i, sub + 1, nsub)
            e, l = _exp_parts(s, bias3, i, sub)
            pb = (e * (1.0 / l)).astype(BF16)
            p_ref[sub] = pb
            o_ref[sub * QB:(sub + 1) * QB, :] = _nn(pb, v_ref[pl.ds(_row0(i, sub, nsub), WIN), :]).astype(BF16)

    return _call(
        body, name, (NH, S // R),
        [pl.BlockSpec((R, HD), lambda h, i: (i, h)), pl.BlockSpec((S + PAD, HD), lambda h, i: (0, h)),
         pl.BlockSpec((S + PAD, HD), lambda h, i: (0, h)), pl.BlockSpec((None, CHUNK, BW), lambda h, i: (h, 0, 0)),
         pl.BlockSpec((None, CHUNK, BW), lambda h, i: (h, 0, 0))],
        [pl.BlockSpec((R, HD), lambda h, i: (i, h)), pl.BlockSpec((None, nsub, QB, WIN), lambda h, i: (h, i, 0, 0))],
        [jax.ShapeDtypeStruct((S, E), BF16), jax.ShapeDtypeStruct((NH, S // QB, QB, WIN), BF16)],
        [pltpu.VMEM((NMASK + 1, QB, WIN), F32)],
        _params(2, 48), (q, kp, vp, ba, bb), comm)


def _store_grad(acc, stage, dw_hbm, sem):
    for q in range(NCHIP):
        stage[...] = acc[q].astype(BF16)
        cp = pltpu.make_async_copy(stage, dw_hbm.at[q], sem)
        cp.start()
        cp.wait()


def _out_bwd(dh, y, gate, a, cs, z, w, name, comm=None, tm=256):
    S = dh.shape[0]
    kb = E // NCHIP
    cb = 256
    n_t = S // tm

    def body(dh_ref, y_ref, gate_ref, a_ref, cs_ref, z_ref, w_hbm, da_ref, dz_ref, dw_hbm, st_ref, w_v, acc, stage, sem):
        i = pl.program_id(0)

        @pl.when(i == 0)
        def _():
            cp = pltpu.make_async_copy(w_hbm, w_v, sem)
            cp.start()
            acc[...] = jnp.zeros(acc.shape, F32)
            st_ref[...] = jnp.zeros((SUBLANES, D), F32)
            cp.wait()

        dhh = dh_ref[...]
        st_ref[0:1, :] += _colsum(dhh * y_ref[...].astype(F32))
        dy = (dhh * gate_ref[...]).astype(BF16)
        for blk in range(E // cb):
            p, r0 = divmod(blk * cb, kb)
            cols = slice(blk * cb, (blk + 1) * cb)
            zz = z_ref[:, cols].astype(F32)
            sig = jax.nn.sigmoid(zz)
            sz = zz * sig
            ae = a_ref[:, cols].astype(F32) * cs_ref[:, cols]
            acc[p, r0:r0 + cb, :] += _tn((ae * sz).astype(BF16), dy)
            dact = _nt(dy, w_v[p, r0:r0 + cb, :])
            da_ref[:, cols] = (dact * sz).astype(BF16)
            dz_ref[:, cols] = (dact * ae * (sig * (1.0 + zz * (1.0 - sig)))).astype(BF16)

        @pl.when(i == n_t - 1)
        def _():
            _store_grad(acc, stage, dw_hbm, sem)

    return _call(
        body, name, (n_t,),
        [pl.BlockSpec((tm, D), lambda i: (i, 0)), pl.BlockSpec((tm, D), lambda i: (i, 0)), _row(D),
         pl.BlockSpec((tm, E), lambda i: (i, 0)), _row(E), pl.BlockSpec((tm, E), lambda i: (i, 0)), ANY],
        [pl.BlockSpec((tm, E), lambda i: (i, 0)), pl.BlockSpec((tm, E), lambda i: (i, 0)), ANY,
         pl.BlockSpec((SUBLANES, D), lambda i: (0, 0))],
        [jax.ShapeDtypeStruct((S, E), BF16), jax.ShapeDtypeStruct((S, E), BF16),
         jax.ShapeDtypeStruct((NCHIP, kb, D), BF16), jax.ShapeDtypeStruct((SUBLANES, D), F32)],
        [pltpu.VMEM((NCHIP, kb, D), BF16), pltpu.VMEM((NCHIP, kb, D), F32), pltpu.VMEM((kb, D), BF16),
         pltpu.SemaphoreType.DMA],
        _params(1, 52), (dh, y, gate, a, cs, z, w), comm)


def _attn_bwd(q, kp, vp, probs, do, prev, name, comm=None):
    S = q.shape[0]
    nsub = _nsub(S)
    R = nsub * QB
    n_i = S // R
    dt_kv = F32 if prev is None else BF16

    def body(*refs):
        q_ref, k_ref, v_ref, p_ref, do_ref = refs[:5]
        refs = refs[5:]
        if prev is not None:
            pk_hbm, pv_hbm = refs[:2]
            refs = refs[2:]
        dq_ref, dk_ref, dv_ref, dba_ref, dbb_ref, dbias, dk_acc, dv_acc = refs[:8]
        if prev is not None:
            pk_v, pv_v, sems = refs[8:]
        h = pl.program_id(0)
        i = pl.program_id(1)

        def prev_copies():
            cols = pl.ds(pl.multiple_of(h * HD, HD), HD)
            return (pltpu.make_async_copy(pk_hbm.at[:, cols], pk_v, sems.at[0]),
                    pltpu.make_async_copy(pv_hbm.at[:, cols], pv_v, sems.at[1]))

        @pl.when(i == 0)
        def _():
            if prev is not None:
                for cp in prev_copies():
                    cp.start()
            dbias[...] = jnp.zeros((2, CHUNK, DBW), F32)
            dk_acc[...] = jnp.zeros((S + PAD, HD), F32)
            dv_acc[...] = jnp.zeros((S + PAD, HD), F32)

        def mxu_in(sub):
            return _nt(do_ref[sub * QB:(sub + 1) * QB, :], v_ref[pl.ds(_row0(i, sub, nsub), WIN), :])

        nxt = mxu_in(0)
        for sub in range(nsub):
            rows = slice(sub * QB, (sub + 1) * QB)
            win = pl.ds(_row0(i, sub, nsub), WIN)
            dp = nxt
            if sub + 1 < nsub:
                nxt = mxu_in(sub + 1)
            pb = p_ref[sub]
            p = pb.astype(F32)
            ds = p * (dp - jnp.sum(p * dp, axis=-1, keepdims=True))
            for par in range(2):
                part = None
                for qc in range(par, QC, 2):
                    c0 = (qc - par) * CHUNK + BW - DBW
                    blk_ = ds[qc * CHUNK:(qc + 1) * CHUNK, c0:c0 + DBW]
                    part = blk_ if part is None else part + blk_
                dbias[par] += part
            dsb = (ds * SM_SCALE).astype(BF16)
            dq_ref[rows, :] = _nn(dsb, k_ref[win, :]).astype(BF16)
            dk_acc[win, :] += _tn(dsb, q_ref[rows, :])
            dv_acc[win, :] += _tn(pb, do_ref[rows, :])

        @pl.when(i == n_i - 1)
        def _():
            zeros = jnp.zeros((CHUNK, BW - DBW), F32)
            dba_ref[...] = jnp.concatenate([zeros, dbias[0]], axis=1)
            dbb_ref[...] = jnp.concatenate([zeros, dbias[1]], axis=1)
            if prev is None:
                dk_ref[...] = dk_acc[...]
                dv_ref[...] = dv_acc[...]
            else:
                for cp in prev_copies():
                    cp.wait()
                dk_ref[...] = (dk_acc[...] + pk_v[...]).astype(BF16)
                dv_ref[...] = (dv_acc[...] + pv_v[...]).astype(BF16)

    head = pl.BlockSpec((S + PAD, HD), lambda h, i: (0, h))
    strip = pl.BlockSpec((None, CHUNK, BW), lambda h, i: (h, 0, 0))
    blk = pl.BlockSpec((R, HD), lambda h, i: (i, h))
    in_specs = [blk, head, head, pl.BlockSpec((None, nsub, QB, WIN), lambda h, i: (h, i, 0, 0)), blk]
    scratch = [pltpu.VMEM((2, CHUNK, DBW), F32), pltpu.VMEM((S + PAD, HD), F32), pltpu.VMEM((S + PAD, HD), F32)]
    args = (q, kp, vp, probs, do)
    if prev is not None:
        in_specs += [ANY, ANY]
        scratch += [pltpu.VMEM((S + PAD, HD), F32), pltpu.VMEM((S + PAD, HD), F32), pltpu.SemaphoreType.DMA((2,))]
        args += tuple(prev)
    return _call(
        body, name, (NH, n_i), in_specs, [blk, head, head, strip, strip],
        [jax.ShapeDtypeStruct((S, E), BF16), jax.ShapeDtypeStruct((S + PAD, E), dt_kv),
         jax.ShapeDtypeStruct((S + PAD, E), dt_kv), jax.ShapeDtypeStruct((NH, CHUNK, BW), F32),
         jax.ShapeDtypeStruct((NH, CHUNK, BW), F32)],
        scratch, _params(2, 56), args, comm)


def _pool_bwd(dms, mixed, pooled, wg, a_scale, name, comm=None, tm=512):
    S = dms.shape[0]
    n_t = S // tm

    def rev(i):
        return (n_t - 1 - i, 0)

    def body(d_ref, m_ref, p_ref, wg_ref, as_ref, dv_ref, dwg_ref, st_ref, buf):
        i = pl.program_id(0)

        @pl.when(i == 0)
        def _():
            buf[tm:tm + HALO, :] = jnp.zeros((HALO, E), F32)
            dwg_ref[...] = jnp.zeros((4, GW, GW), F32)
            st_ref[...] = jnp.zeros((SUBLANES, E), F32)

        t = (n_t - 1 - i) * tm + lax.broadcasted_iota(jnp.int32, (tm, 1), 0)
        st_ref[0:1, :] += _colsum(d_ref[...].astype(F32) * m_ref[...].astype(F32))
        for gi, w in enumerate(POOL_W):
            cols = slice(gi * GW, (gi + 1) * GW)
            dm = (d_ref[:, cols].astype(F32) * as_ref[:, cols]).astype(BF16)
            dpool = _nt(dm, wg_ref[gi])
            dwg_ref[gi] += _tn(p_ref[:, cols], dm)
            inv_cnt = 1.0 / jnp.minimum(t + 1, w).astype(F32)
            buf[0:tm, cols] = dpool * inv_cnt
            s = buf[:, cols]
            k = 1
            while k < w:
                s = s + pltpu.roll(s, tm + HALO - k, 0)
                k *= 2
            dv_ref[:, cols] = (s[0:tm, :] - dpool).astype(BF16)
        buf[tm:tm + HALO, :] = buf[0:HALO, :]

    return _call(
        body, name, (n_t,),
        [pl.BlockSpec((tm, E), rev), pl.BlockSpec((tm, E), rev), pl.BlockSpec((tm, E), rev),
         pl.BlockSpec((4, GW, GW), lambda i: (0, 0, 0)), _row(E)],
        [pl.BlockSpec((tm, E), rev), pl.BlockSpec((4, GW, GW), lambda i: (0, 0, 0)),
         pl.BlockSpec((SUBLANES, E), lambda i: (0, 0))],
        [jax.ShapeDtypeStruct((S, E), BF16), jax.ShapeDtypeStruct((4, GW, GW), F32),
         jax.ShapeDtypeStruct((SUBLANES, E), F32)],
        [pltpu.VMEM((tm + HALO, E), F32)],
        _params(1, 52), (dms, mixed, pooled, wg, a_scale), comm)


def _in_bwd(da, db, row_off, u, h, g, scale, w, dh_out, name, comm=None, tm=256):
    S = h.shape[0]
    n_t = S // tm
    off = row_off // tm

    def body(da_ref, db_ref, u_ref, h_ref, g_ref, sc_ref, w_hbm, dho_ref, dhi_ref, dw_hbm, st_ref, w_v, acc, stage, sem):
        i = pl.program_id(0)

        @pl.when(i == 0)
        def _():
            cp = pltpu.make_async_copy(w_hbm, w_v, sem)
            cp.start()
            acc[...] = jnp.zeros(acc.shape, F32)
            st_ref[...] = jnp.zeros((SUBLANES, D), F32)
            cp.wait()

        ub = u_ref[...]
        du = None
        for q in range(NCHIP):
            d_ref = da_ref if q < 2 else db_ref
            dv = d_ref[:, (q % 2) * D:(q % 2 + 1) * D]
            acc[q] += _tn(ub, dv)
            part = _nt(dv, w_v[q])
            du = part if du is None else du + part

        hh = h_ref[...]
        r = lax.rsqrt(jnp.mean(hh * hh, axis=-1, keepdims=True) + EPS)
        xhat = hh * r
        gg = g_ref[...]
        st_ref[0:1, :] += _colsum(du)
        st_ref[1:2, :] += _colsum(du * (xhat * gg))
        dn = du * (1.0 + sc_ref[...])
        st_ref[2:3, :] += _colsum(dn * xhat)
        dx = dn * gg
        dhi_ref[...] = dho_ref[...] + r * (dx - xhat * jnp.mean(dx * xhat, axis=-1, keepdims=True))

        @pl.when(i == n_t - 1)
        def _():
            _store_grad(acc, stage, dw_hbm, sem)

    part_spec = pl.BlockSpec((tm, E), lambda i: (i + off, 0))
    return _call(
        body, name, (n_t,),
        [part_spec, part_spec, pl.BlockSpec((tm, D), lambda i: (i, 0)), pl.BlockSpec((tm, D), lambda i: (i, 0)),
         _row(D), _row(D), ANY, pl.BlockSpec((tm, D), lambda i: (i, 0))],
        [pl.BlockSpec((tm, D), lambda i: (i, 0)), ANY, pl.BlockSpec((SUBLANES, D), lambda i: (0, 0))],
        [jax.ShapeDtypeStruct((S, D), F32), jax.ShapeDtypeStruct((NCHIP, D, D), BF16),
         jax.ShapeDtypeStruct((SUBLANES, D), F32)],
        [pltpu.VMEM((NCHIP, D, D), BF16), pltpu.VMEM((NCHIP, D, D), F32), pltpu.VMEM((D, D), BF16),
         pltpu.SemaphoreType.DMA],
        _params(1, 56), (da, db, u, h, g, scale, w, dh_out), comm)


def _cmat(c_all, w, b, name):
    L, _, n = w.shape

    def body(c_ref, w_ref, b_ref, ca_ref, o_ref):
        cc = c_ref[...]
        ca = cc * jax.nn.sigmoid(cc)
        ca_ref[...] = ca
        o_ref[...] = _nn(ca.astype(BF16), w_ref[...].astype(BF16)) + b_ref[...]

    return pl.pallas_call(
        body, name=name, grid=(L,),
        in_specs=[pl.BlockSpec((SUBLANES, D), lambda l: (0, 0)), pl.BlockSpec((None, D, n), lambda l: (l, 0, 0)),
                  pl.BlockSpec((None, 1, n), lambda l: (l, 0, 0))],
        out_specs=[pl.BlockSpec((SUBLANES, D), lambda l: (0, 0)), pl.BlockSpec((None, SUBLANES, n), lambda l: (l, 0, 0))],
        out_shape=[jax.ShapeDtypeStruct((SUBLANES, D), F32), jax.ShapeDtypeStruct((L, SUBLANES, n), F32)],
        compiler_params=_params(1, 32),
    )(c_all, w, b)


def _grad_ada(c_act_t, dmod, name):
    L, _, n = dmod.shape

    def body(c_ref, d_ref, o_ref):
        acc = None
        for b in range(SUBLANES):
            part = c_ref[:, b:b + 1] * d_ref[b:b + 1, :]
            acc = part if acc is None else acc + part
        o_ref[...] = acc

    return pl.pallas_call(
        body, name=name, grid=(L,),
        in_specs=[pl.BlockSpec((D, SUBLANES), lambda l: (0, 0)), pl.BlockSpec((None, SUBLANES, n), lambda l: (l, 0, 0))],
        out_specs=pl.BlockSpec((None, D, n), lambda l: (l, 0, 0)),
        out_shape=jax.ShapeDtypeStruct((L, D, n), F32),
        compiler_params=_params(1, 32),
    )(c_act_t, dmod)


def _stats_reduce(g3, loss_row, name):
    n_dev, rows, _ = g3.shape

    def body(g_ref, o_ref, l_ref):
        acc = g_ref[0]
        for d in range(1, n_dev):
            acc = acc + g_ref[d]
        o_ref[...] = acc
        tot = jnp.sum(o_ref[loss_row:loss_row + 1, :], axis=-1, keepdims=True)
        l_ref[...] = jnp.broadcast_to(tot * (0.5 / D), (SUBLANES, LANES))

    return pl.pallas_call(
        body, name=name,
        in_specs=[pl.BlockSpec(memory_space=pltpu.VMEM)],
        out_specs=[pl.BlockSpec(memory_space=pltpu.VMEM), pl.BlockSpec(memory_space=pltpu.VMEM)],
        out_shape=[jax.ShapeDtypeStruct((rows, D), F32), jax.ShapeDtypeStruct((SUBLANES, LANES), F32)],
        compiler_params=pltpu.CompilerParams(vmem_limit_bytes=32 * 2 ** 20),
    )(g3)


def _sum4(own, land, chip, name, tr=256):
    _, R, C = own.shape
    tr = min(tr, R)

    def body(p_ref, own_ref, land_ref, o_ref):
        o_ref[...] = ((own_ref[...].astype(F32) + land_ref[0].astype(F32)) + land_ref[1].astype(F32)) + land_ref[2].astype(F32)

    return pl.pallas_call(
        body, name=name,
        grid_spec=pltpu.PrefetchScalarGridSpec(
            num_scalar_prefetch=1, grid=(R // tr,),
            in_specs=[pl.BlockSpec((None, tr, C), lambda i, p: (p[0], i, 0)), pl.BlockSpec((3, tr, C), lambda i, p: (0, i, 0))],
            out_specs=pl.BlockSpec((tr, C), lambda i, p: (i, 0))),
        out_shape=jax.ShapeDtypeStruct((R, C), F32),
        compiler_params=_params(1, 32),
    )(chip, own, land)


def _adamw(w, m, v, g, name, tr=256):
    L, R, C = w.shape
    tr = min(tr, R)
    stacked = not isinstance(g, (list, tuple))
    n_g = None if stacked else [len(ps) for ps in g]
    flat = [g] if stacked else [a for ps in g for a in ps]

    def body(*refs):
        w_ref, m_ref, v_ref = refs[:3]
        g_refs = refs[3:3 + len(flat)]
        go_ref, d_ref, mo_ref, vo_ref = refs[3 + len(flat):]
        if stacked:
            gg = g_refs[0][...]
        else:
            layer = pl.program_id(0)
            gg = None
            k = 0
            for li in range(L):
                gl = None
                for _ in range(n_g[li]):
                    x = g_refs[k][...]
                    gl = x if gl is None else gl + x
                    k += 1
                gg = gl if gg is None else jnp.where(layer == li, gl, gg)
        m2 = ADAM_B1 * m_ref[...] + (1.0 - ADAM_B1) * gg
        v2 = ADAM_B2 * v_ref[...] + (1.0 - ADAM_B2) * (gg * gg)
        m_hat = m2 / (1.0 - ADAM_B1 ** ADAM_STEP)
        v_hat = v2 / (1.0 - ADAM_B2 ** ADAM_STEP)
        go_ref[...] = gg
        d_ref[...] = -ADAM_LR * (m_hat / (jnp.sqrt(v_hat) + ADAM_EPS) + ADAM_WD * w_ref[...])
        mo_ref[...] = m2
        vo_ref[...] = v2

    big = pl.BlockSpec((None, tr, C), lambda l, i: (l, i, 0))
    g_specs = [big] if stacked else [pl.BlockSpec((tr, C), lambda l, i: (i, 0))] * len(flat)
    return pl.pallas_call(
        body, name=name, grid=(L, R // tr),
        in_specs=[big, big, big] + g_specs,
        out_specs=[big, big, big, big],
        out_shape=[jax.ShapeDtypeStruct((L, R, C), F32)] * 4,
        compiler_params=_params(2, 40),
    )(w, m, v, *flat)


def _allgather8(xs, name, comm=None):
    m, n = xs.shape
    n_c = 0 if comm is None else comm.n

    def body(*refs):
        x_ref, out_ref = refs[0], refs[1 + n_c]
        send_sems, recv_sems, local_sem = refs[2 + 2 * n_c:5 + 2 * n_c]
        c_refs = (refs[1:1 + n_c], refs[2 + n_c:2 + 2 * n_c]) + tuple(refs[5 + 2 * n_c:])
        if comm is not None:
            comm.start(*c_refs)
        x, y, c = _place()
        me, sibling = (x, y, c), (x, y, 1 - c)
        chips = [(1 - x, y), (x, 1 - y), (1 - x, 1 - y)]

        def rows(px, py, pc):
            return out_ref.at[pl.ds((4 * px + 2 * py + pc) * m, m), :]

        def copy(k, block, to, src=None):
            return pltpu.make_async_remote_copy(
                src_ref=rows(*block) if src is None else src, dst_ref=rows(*block),
                send_sem=send_sems.at[k], recv_sem=recv_sems.at[k], device_id=to, device_id_type=MESH)

        mine = pltpu.make_async_copy(x_ref, rows(*me), local_sem)
        mine.start()
        first = [copy(0, me, sibling, src=x_ref)]
        first += [copy(1 + j, me, (*chip, c), src=x_ref) for j, chip in enumerate(chips)]
        for cp in first:
            cp.start()
        passed = [copy(4 + j, (*chip, c), sibling) for j, chip in enumerate(chips)]
        for j, chip in enumerate(chips):
            copy(1 + j, (*chip, c), me).wait_recv()
            passed[j].start()
        copy(0, sibling, me).wait_recv()
        for j, chip in enumerate(chips):
            copy(4 + j, (*chip, 1 - c), me).wait_recv()
        for cp in first + passed:
            cp.wait_send()
        mine.wait()
        if comm is not None:
            comm.wait(*c_refs)

    vmem = pl.BlockSpec(memory_space=pltpu.VMEM)
    outs = pl.pallas_call(
        body, name=name,
        out_shape=[jax.ShapeDtypeStruct((8 * m, n), xs.dtype)] + ([] if comm is None else comm.out_shape),
        in_specs=[vmem] + [ANY] * n_c,
        out_specs=[vmem] + [ANY] * n_c,
        scratch_shapes=[pltpu.SemaphoreType.DMA((7,)), pltpu.SemaphoreType.DMA((7,)), pltpu.SemaphoreType.DMA]
        + ([] if comm is None else comm.scratch),
        compiler_params=pltpu.CompilerParams(vmem_limit_bytes=32 * 2 ** 20),
    )(xs, *([] if comm is None else comm.arrays))
    return outs[0], list(outs[1:])


def _pad8(a):
    return jnp.pad(a, ((0, SUBLANES - a.shape[0]), (0, 0)))


def _group_rows(wg):
    return wg.transpose(1, 0, 2, 3).reshape(4, GW, GW)


def _example_step(h0, tgt, mods, kvmod, a_scale, norm_g, kv_norm_g, final_g, b_rel_bias, sh, w_first, chip_arr):
    ones_e = jnp.ones((1, E), F32)
    shift = [mods[l:l + 1, 0:D] for l in range(4)]
    scale = [mods[l:l + 1, D:2 * D] for l in range(4)]
    gate = [mods[l:l + 1, 2 * D:3 * D] for l in range(4)]
    gl = [norm_g[l:l + 1] for l in range(4)]
    kv_shift, kv_scale = kvmod[None, 0:D], kvmod[None, D:2 * D]
    kv_g = kv_norm_g[None]

    w_a = w_first
    hs = [h0]
    saved = []
    nxt = [[sh["a_in"][1], sh["a_grp"][1], sh["a_out"][1]], [sh["kv"][0], sh["b_in"][0]]]
    for l in range(2):
        w_in_l, wg_l, wo_l = w_a
        wg_full = _group_rows(wg_l)
        (u, z, pooled, mixed, y, hn), got = _a_fwd(hs[-1], gl[l], shift[l], scale[l], a_scale[l:l + 1], gate[l], w_in_l,
                                                   wg_full, wo_l, f"a{l}_fwd", comm=_Comm(gathers=nxt[l]))
        saved.append((u, z, pooled, mixed, y, w_in_l, wg_full, wo_l))
        hs.append(hn)
        if l == 0:
            w_a = got
        else:
            w_kv, wb_in0 = got

    (uk, kp, vp), (wb_out0, wb_in1) = _in_fwd(hs[2], kv_g, kv_shift, kv_scale, w_kv, BF16, BF16, "kv_in_fwd", pad_rows=PAD,
                                              comm=_Comm(gathers=[sh["b_out"][0], sh["b_in"][1]]))
    wb_in = [wb_in0, wb_in1]
    wb_out = [wb_out0, None]

    for bi in range(2):
        l = 2 + bi
        sa, sb = _bias_build(jnp.pad(b_rel_bias[bi], ((0, 0), (0, NRELP - NREL))), f"b{bi}_bias")
        (u, q, z), got = _in_fwd(hs[-1], gl[l], shift[l], scale[l], wb_in[bi], BF16, BF16, f"b{bi}_in_fwd",
                                 comm=_Comm(gathers=[sh["b_out"][1]]) if bi == 0 else None)
        if bi == 0:
            (wb_out[1],) = got
        (att, probs), _ = _attn_fwd(q, kp, vp, sa.transpose(1, 0, 2), sb.transpose(1, 0, 2), f"b{bi}_attn_fwd")
        if bi == 0:
            (y, hn), _ = _out_fwd(att, z, wb_out[bi], gate[l], hs[-1], f"b{bi}_out_fwd")
            hs.append(hn)
        else:
            (y, dh, st_fin), _ = _out_fwd(att, z, wb_out[bi], gate[l], hs[-1], f"b{bi}_out_fwd", head=(final_g[None], tgt))
        saved.append((u, z, q, att, y, probs))

    st_in = [None] * 4
    st_out = [None] * 4
    grads = {}
    landed = {}

    def carry(names):
        return _Comm(scatters=[grads[n] for n in names]) if names else None

    def land(names, got):
        for n, a in zip(names, got):
            landed[n] = a

    u, z, q, att, y, probs = saved[3]
    (datt, dz, grads["b_out1"], st_out[3]), _ = _out_bwd(dh, y, gate[3], att, ones_e, z, wb_out[1], "b1_out_bwd")
    (dq, dk1, dv1, dsa, dsb), got = _attn_bwd(q, kp, vp, probs, datt, None, "b1_attn_bwd", comm=carry(["b_out1"]))
    land(["b_out1"], got)
    drb1 = _dbias_reduce(dsa.transpose(1, 0, 2), dsb.transpose(1, 0, 2), "b1_dbias")
    (dh, grads["b_in1"], st_in[3]), _ = _in_bwd(dq, dz, 0, u, hs[3], gl[3], scale[3], wb_in[1], dh, "b1_in_bwd")
    u, z, q, att, y, probs = saved[2]
    (datt, dz, grads["b_out0"], st_out[2]), _ = _out_bwd(dh, y, gate[2], att, ones_e, z, wb_out[0], "b0_out_bwd")
    (dq, dk, dv, dsa, dsb), got = _attn_bwd(q, kp, vp, probs, datt, (dk1, dv1), "b0_attn_bwd",
                                            comm=carry(["b_in1", "b_out0"]))
    land(["b_in1", "b_out0"], got)
    drb0 = _dbias_reduce(dsa.transpose(1, 0, 2), dsb.transpose(1, 0, 2), "b0_dbias")
    (dh, grads["b_in0"], st_in[2]), _ = _in_bwd(dq, dz, 0, u, hs[2], gl[2], scale[2], wb_in[0], dh, "b0_in_bwd")
    (dh, grads["kv"], st_kv), got = _in_bwd(dk, dv, PAD, uk, hs[2], kv_g, kv_scale, w_kv, dh, "kv_in_bwd",
                                            comm=carry(["b_in0"]))
    land(["b_in0"], got)
    st_pool = [None] * 2
    plan = {1: dict(o=[], p=["a_out1"], i=["kv", "a_grp1"]), 0: dict(o=["a_in1"], p=["a_out0"], i=[])}
    early = ["b_out1", "b_in1", "b_out0", "b_in0", "kv", "a_out1", "a_grp1", "a_in1"]
    late = ["a_out0", "a_grp0", "a_in0"]
    both = {}

    def sum4(n):
        return _sum4(grads[n], landed[n], chip_arr, f"sum4_{n}")

    for l in (1, 0):
        u, z, pooled, mixed, y, w_in_l, wg_full, wo = saved[l]
        asl = a_scale[l:l + 1]
        (dms, dz, grads[f"a_out{l}"], st_out[l]), got = _out_bwd(dh, y, gate[l], mixed, asl, z, wo, f"a{l}_out_bwd",
                                                                comm=carry(plan[l]["o"]))
        land(plan[l]["o"], got)
        comm = carry(plan[l]["p"])
        if l == 0:
            mine = [sum4(n) for n in early]
            comm = _Comm(scatters=[grads[n] for n in plan[l]["p"]], swaps=mine)
        (dval, dwg, st_pool[l]), got = _pool_bwd(dms, mixed, pooled, wg_full, asl, f"a{l}_pool_bwd", comm=comm)
        land(plan[l]["p"], got)
        if l == 0:
            both.update({n: [a, b] for n, a, b in zip(early, mine, got[len(plan[l]["p"]):])})
        grads[f"a_grp{l}"] = (dwg.reshape(4, NCHIP, GW // NCHIP, GW).transpose(1, 0, 2, 3).reshape(NCHIP, GW, GW)
                              .astype(BF16))
        (dh, grads[f"a_in{l}"], st_in[l]), got = _in_bwd(dval, dz, 0, u, hs[l], gl[l], scale[l], w_in_l, dh, f"a{l}_in_bwd",
                                                         comm=carry(plan[l]["i"]))
        land(plan[l]["i"], got)
    pieces = st_in + [st_kv] + st_out + [st_fin]
    pieces += [_pad8(st_pool[l][0].reshape(2, D)) for l in range(2)]
    pieces += [_pad8(d.reshape(NH * NRELP // D, D)) for d in (drb0, drb1)]
    gathered, got = _allgather8(jnp.concatenate(pieces, axis=0), "gather_stats", comm=carry(["a_grp0", "a_in0"]))
    land(["a_grp0", "a_in0"], got)
    mine = [sum4(n) for n in late]
    both.update({n: [a, b] for n, a, b in zip(late, mine, _comm_only(_Comm(swaps=mine), "swap_last"))})
    return dh, both, gathered.reshape(8, N_STAT, D)


ROW_IN = [8 * l for l in range(4)]
ROW_KV = 32
ROW_OUT = [40 + 8 * l for l in range(4)]
ROW_FIN = 72
ROW_ASC = [80, 88]
ROW_RB = [96, 104]
N_STAT = 112


def kernel(x, c, ada_w, ada_b, norm_g, a_w_in, a_w_group, a_scale, a_w_out, kv_norm_g, kv_ada_w, kv_ada_b, w_kv, b_w_in, b_rel_bias, b_w_out, final_g, loss_target, m_ada_w, m_ada_b, m_norm_g, m_a_w_in, m_a_w_group, m_a_scale, m_a_w_out, m_kv_norm_g, m_kv_ada_w, m_kv_ada_b, m_w_kv, m_b_w_in, m_b_rel_bias, m_b_w_out, m_final_g, v_ada_w, v_ada_b, v_norm_g, v_a_w_in, v_a_w_group, v_a_scale, v_a_w_out, v_kv_norm_g, v_kv_ada_w, v_kv_ada_b, v_w_kv, v_b_w_in, v_b_rel_bias, v_b_w_out, v_final_g):
    xi, yi, ci = _place()
    chip = 2 * xi + yi
    dev = 4 * xi + 2 * yi + ci
    n_ada = ada_w.shape[2]
    n_kva = kv_ada_w.shape[1]
    n_asc = a_scale.shape[1]

    c_all = _allgather8(jnp.broadcast_to(c, (SUBLANES, D)), "gather_c")[0][::SUBLANES]
    ada_b_sh = lax.dynamic_slice_in_dim(ada_b, chip * n_ada, n_ada, axis=1)
    kvb_sh = lax.dynamic_slice_in_dim(kv_ada_b, chip * n_kva, n_kva, axis=0)
    c_act, mod_ada = _cmat(c_all, ada_w, ada_b_sh[:, None, :], "mod_ada")
    _, mod_kv = _cmat(c_all, kv_ada_w[None], kvb_sh[None, None, :], "mod_kv")
    part = jnp.concatenate([mod_ada.transpose(1, 0, 2).reshape(SUBLANES, 4 * n_ada), mod_kv[0],
                            jnp.broadcast_to(a_scale.reshape(1, 2 * n_asc), (SUBLANES, 2 * n_asc))], axis=1)
    sh = dict(a_in=[a_w_in[l].astype(BF16) for l in range(2)], a_grp=[a_w_group[l].astype(BF16) for l in range(2)],
              a_out=[a_w_out[l].astype(BF16) for l in range(2)], kv=[w_kv.astype(BF16)],
              b_in=[b_w_in[l].astype(BF16) for l in range(2)], b_out=[b_w_out[l].astype(BF16) for l in range(2)])
    gathered, w_first = _allgather8(part, "gather_mod", comm=_Comm(gathers=[sh["a_in"][0], sh["a_grp"][0], sh["a_out"][0]]))
    rows = jnp.concatenate([lax.dynamic_slice_in_dim(gathered, SUBLANES * (2 * p + ci) + dev, 1, axis=0)
                            for p in range(NCHIP)], axis=0)
    mods = jnp.stack([rows[:, l * n_ada:(l + 1) * n_ada].reshape(3 * D) for l in range(4)])
    kvmod = rows[:, 4 * n_ada:4 * n_ada + n_kva].reshape(2 * D)
    o_asc = 4 * n_ada + n_kva
    a_scale_full = jnp.stack([rows[:, o_asc + l * n_asc:o_asc + (l + 1) * n_asc].reshape(E) for l in range(2)])

    chip_arr = jnp.reshape(chip, (1,)).astype(jnp.int32)
    dh, both, g3 = _example_step(x[0], loss_target[0], mods, kvmod, a_scale_full, norm_g, kv_norm_g, final_g,
                                 b_rel_bias, sh, w_first, chip_arr)
    grad_x = dh[None]

    red, loss_tile = _stats_reduce(g3, ROW_FIN + 1, "stats_reduce")
    loss = loss_tile[0, 0]

    def cat(rows_):
        return jnp.concatenate(rows_, axis=-1)

    g_ada_b = jnp.stack([cat([red[ROW_IN[l]], red[ROW_IN[l] + 1], red[ROW_OUT[l]]]) for l in range(4)])
    g_norm_g = jnp.stack([red[ROW_IN[l] + 2] for l in range(4)])
    g_kv_norm_g = red[ROW_KV + 2]
    g_kv_ada_b = cat([red[ROW_KV], red[ROW_KV + 1]])
    g_final_g = red[ROW_FIN]
    g_asc_full = jnp.stack([red[ROW_ASC[l]:ROW_ASC[l] + 2].reshape(E) for l in range(2)])
    g_a_scale = lax.dynamic_slice_in_dim(g_asc_full, chip * n_asc, n_asc, axis=1)
    g_rel = jnp.stack([red[ROW_RB[bi]:ROW_RB[bi] + NH * NRELP // D].reshape(NH, NRELP)[:, :NREL] for bi in range(2)])

    dmod = jnp.stack([cat([g3[:, ROW_IN[l]], g3[:, ROW_IN[l] + 1], g3[:, ROW_OUT[l]]]) for l in range(4)])
    dmod_sh = lax.dynamic_slice_in_dim(dmod, chip * n_ada, n_ada, axis=2)
    dkv = cat([g3[:, ROW_KV], g3[:, ROW_KV + 1]])[None]
    dkv_sh = lax.dynamic_slice_in_dim(dkv, chip * n_kva, n_kva, axis=2)
    c_act_t = c_act.T
    g_ada_w = _grad_ada(c_act_t, dmod_sh, "grad_ada_w")
    g_kv_ada_w = _grad_ada(c_act_t, dkv_sh, "grad_kv_ada_w")

    def upd(w, m, v, g, name, shape3):
        g = g.reshape(shape3) if not isinstance(g, list) else g
        outs = _adamw(w.reshape(shape3), m.reshape(shape3), v.reshape(shape3), g, name)
        return [o.reshape(w.shape) for o in outs]

    def pair(name):
        return [both[name + "0"], both[name + "1"]]

    res = {}
    res["ada_w"] = upd(ada_w, m_ada_w, v_ada_w, g_ada_w, "adamw_ada_w", ada_w.shape)
    res["ada_b"] = upd(ada_b, m_ada_b, v_ada_b, g_ada_b, "adamw_ada_b", (1,) + ada_b.shape)
    res["norm_g"] = upd(norm_g, m_norm_g, v_norm_g, g_norm_g, "adamw_norm_g", (1,) + norm_g.shape)
    res["a_w_in"] = upd(a_w_in, m_a_w_in, v_a_w_in, pair("a_in"), "adamw_a_w_in", a_w_in.shape)
    res["a_w_group"] = upd(a_w_group, m_a_w_group, v_a_w_group, pair("a_grp"), "adamw_a_w_group", (2, GW, GW))
    res["a_scale"] = upd(a_scale, m_a_scale, v_a_scale, g_a_scale, "adamw_a_scale", (1,) + a_scale.shape)
    res["a_w_out"] = upd(a_w_out, m_a_w_out, v_a_w_out, pair("a_out"), "adamw_a_w_out", a_w_out.shape)
    res["kv_norm_g"] = upd(kv_norm_g, m_kv_norm_g, v_kv_norm_g, g_kv_norm_g, "adamw_kv_norm_g", (1, 1, D))
    res["kv_ada_w"] = upd(kv_ada_w, m_kv_ada_w, v_kv_ada_w, g_kv_ada_w, "adamw_kv_ada_w", (1,) + kv_ada_w.shape)
    res["kv_ada_b"] = upd(kv_ada_b, m_kv_ada_b, v_kv_ada_b, g_kv_ada_b, "adamw_kv_ada_b", (1, 1, 2 * D))
    res["w_kv"] = upd(w_kv, m_w_kv, v_w_kv, [both["kv"]], "adamw_w_kv", (1,) + w_kv.shape)
    res["b_w_in"] = upd(b_w_in, m_b_w_in, v_b_w_in, pair("b_in"), "adamw_b_w_in", b_w_in.shape)
    res["b_rel_bias"] = upd(b_rel_bias, m_b_rel_bias, v_b_rel_bias, g_rel, "adamw_b_rel_bias", (1, 2 * NH, NREL))
    res["b_w_out"] = upd(b_w_out, m_b_w_out, v_b_w_out, pair("b_out"), "adamw_b_w_out", b_w_out.shape)
    res["final_g"] = upd(final_g, m_final_g, v_final_g, g_final_g, "adamw_final_g", (1, 1, D))

    names = ["ada_w", "ada_b", "norm_g", "a_w_in", "a_w_group", "a_scale", "a_w_out", "kv_norm_g", "kv_ada_w", "kv_ada_b",
             "w_kv", "b_w_in", "b_rel_bias", "b_w_out", "final_g"]
    return (loss, grad_x, *[res[n][0] for n in names], *[res[n][1] for n in names], *[res[n][2] for n in names],
            *[res[n][3] for n in names])
```

```python
import math

import jax
import jax.numpy as jnp
from jax import lax
from jax.experimental import pallas as pl
from jax.experimental.pallas import tpu as pltpu

F32 = jnp.float32
BF16 = jnp.bfloat16

D = 1024
E = 2048
NH = 16
HD = 128
CHUNK = 64
LEFT = 8
PAD = LEFT * CHUNK
NREL = 257
NRELP = 384
REL_CLIP = 128
EPS = 1e-6
NEG = -1e30
LOG2E = math.log2(math.e)
SM_SCALE = HD ** -0.5
POOL_W = (2, 4, 8, 16)
GW = 512
HALO = 16
QC = 4
QB = QC * CHUNK
NMASK = PAD // QB
WIN = (QC + LEFT) * CHUNK
BW = (LEFT + 2) * CHUNK
DBW = 4 * CHUNK
NSUB = 8
NCHIP = 4
LANES = 128
SUBLANES = 8

ADAM_LR = 0.001
ADAM_B1 = 0.9
ADAM_B2 = 0.999
ADAM_EPS = 1e-08
ADAM_WD = 0.01
ADAM_STEP = 10

MESH = pl.DeviceIdType.MESH
ANY = pl.BlockSpec(memory_space=pl.ANY)


def _params(n_axes, vmem_mb):
    return pltpu.CompilerParams(dimension_semantics=("arbitrary",) * n_axes, vmem_limit_bytes=vmem_mb * 2 ** 20)


def _nn(a, b):
    return jnp.dot(a, b, preferred_element_type=F32)


def _nt(a, b):
    return lax.dot_general(a, b, (((1,), (1,)), ((), ())), preferred_element_type=F32)


def _tn(a, b):
    return lax.dot_general(a, b, (((0,), (0,)), ((), ())), preferred_element_type=F32)


def _row(n):
    return pl.BlockSpec((1, n), lambda i: (0, 0))


def _colsum(x):
    return jnp.sum(x, axis=0, keepdims=True)


def _place():
    return lax.axis_index("x"), lax.axis_index("y"), lax.axis_index("c")


class _Comm:
    def __init__(self, gathers=(), scatters=(), swaps=()):
        self.n_g = len(gathers)
        self.n_chip = len(gathers) + len(scatters)
        self.n_sw = len(swaps)
        self.arrays = list(gathers) + list(scatters) + list(swaps)
        self.n = len(self.arrays)
        self.half = [a.shape[0] // 2 for a in gathers]
        self.out_shape = ([jax.ShapeDtypeStruct((NCHIP,) + a.shape, a.dtype) for a in gathers]
                          + [jax.ShapeDtypeStruct((3,) + a.shape[1:], a.dtype) for a in scatters]
                          + [jax.ShapeDtypeStruct(a.shape, a.dtype) for a in swaps])
        n_c, n_f, n_s = max(3 * self.n_chip, 1), max(3 * self.n_g, 1), max(self.n_sw, 1)
        self.scratch = [pltpu.SemaphoreType.DMA((n_c,)), pltpu.SemaphoreType.DMA((n_c,)),
                        pltpu.SemaphoreType.DMA((max(self.n_g, 1),)), pltpu.SemaphoreType.DMA((n_f,)),
                        pltpu.SemaphoreType.DMA((n_f,)), pltpu.SemaphoreType.DMA((n_s,)), pltpu.SemaphoreType.DMA((n_s,))]

    def _chip_copies(self, ins, outs, send, recv, landing):
        x, y, c = _place()
        chips = [(1 - x, y), (x, 1 - y), (1 - x, 1 - y)]
        mine = 2 * x + y
        cps = []
        for k in range(self.n_chip):
            for j, (cx, cy) in enumerate(chips):
                q = 2 * cx + cy
                if k < self.n_g:
                    part = pl.ds(c * self.half[k], self.half[k])
                    src = ins[k].at[part]
                    dst = outs[k].at[q if landing else mine, part]
                else:
                    src = ins[k].at[q]
                    dst = outs[k].at[j]
                cps.append(pltpu.make_async_remote_copy(
                    src_ref=src, dst_ref=dst, send_sem=send.at[3 * k + j], recv_sem=recv.at[3 * k + j],
                    device_id=(cx, cy, c), device_id_type=MESH))
        return cps

    def _core_copies(self, outs, fsend, frecv, landing):
        x, y, c = _place()
        chips = [(1 - x, y), (x, 1 - y), (1 - x, 1 - y)]
        cps = []
        for k in range(self.n_g):
            for j, (cx, cy) in enumerate(chips):
                part = pl.ds((1 - c if landing else c) * self.half[k], self.half[k])
                blk = outs[k].at[2 * cx + cy, part]
                cps.append(pltpu.make_async_remote_copy(
                    src_ref=blk, dst_ref=blk, send_sem=fsend.at[3 * k + j], recv_sem=frecv.at[3 * k + j],
                    device_id=(x, y, 1 - c), device_id_type=MESH))
        return cps

    def _local_copies(self, ins, outs, loc):
        x, y, _ = _place()
        return [pltpu.make_async_copy(ins[k], outs[k].at[2 * x + y], loc.at[k]) for k in range(self.n_g)]

    def _swap_copies(self, ins, outs, ssend, srecv):
        x, y, c = _place()
        return [pltpu.make_async_remote_copy(
            src_ref=ins[k], dst_ref=outs[k], send_sem=ssend.at[k - self.n_chip], recv_sem=srecv.at[k - self.n_chip],
            device_id=(x, y, 1 - c), device_id_type=MESH) for k in range(self.n_chip, self.n)]

    def start(self, ins, outs, send, recv, loc, fsend, frecv, ssend, srecv):
        for cp in (self._local_copies(ins, outs, loc) + self._chip_copies(ins, outs, send, recv, False)
                   + self._swap_copies(ins, outs, ssend, srecv)):
            cp.start()

    def wait(self, ins, outs, send, recv, loc, fsend, frecv, ssend, srecv):
        lands = self._chip_copies(ins, outs, send, recv, True)
        passes = self._core_copies(outs, fsend, frecv, False)
        for k in range(self.n_chip):
            for j in range(3):
                lands[3 * k + j].wait_recv()
                if k < self.n_g:
                    passes[3 * k + j].start()
        for cp in self._core_copies(outs, fsend, frecv, True):
            cp.wait_recv()
        swaps = self._swap_copies(ins, outs, ssend, srecv)
        for cp in swaps:
            cp.wait_recv()
        for cp in self._chip_copies(ins, outs, send, recv, False) + passes + swaps:
            cp.wait_send()
        for cp in self._local_copies(ins, outs, loc):
            cp.wait()


def _call(body, name, grid, in_specs, out_specs, out_shape, scratch, params, args, comm=None):
    n_in, n_out, n_sc = len(in_specs), len(out_specs), len(scratch)
    if comm is None:
        outs = pl.pallas_call(body, name=name, grid=grid, in_specs=in_specs, out_specs=out_specs, out_shape=out_shape,
                              scratch_shapes=scratch, compiler_params=params)(*args)
        return list(outs), []
    n = comm.n
    o0 = n_in + n
    s0 = o0 + n_out + n

    def wrapped(*refs):
        c_refs = (refs[n_in:o0], refs[o0 + n_out:s0]) + tuple(refs[s0 + n_sc:])
        ids = [pl.program_id(a) for a in range(len(grid))]
        first = ids[0] == 0
        last = ids[0] == grid[0] - 1
        for a in range(1, len(grid)):
            first = first & (ids[a] == 0)
            last = last & (ids[a] == grid[a] - 1)

        @pl.when(first)
        def _():
            comm.start(*c_refs)

        body(*refs[:n_in], *refs[o0:o0 + n_out], *refs[s0:s0 + n_sc])

        @pl.when(last)
        def _():
            comm.wait(*c_refs)

    outs = pl.pallas_call(
        wrapped, name=name, grid=grid, in_specs=list(in_specs) + [ANY] * n, out_specs=list(out_specs) + [ANY] * n,
        out_shape=list(out_shape) + comm.out_shape, scratch_shapes=list(scratch) + comm.scratch, compiler_params=params,
    )(*args, *comm.arrays)
    return list(outs[:n_out]), list(outs[n_out:])


def _comm_only(comm, name):
    def body(*refs):
        c_refs = (refs[:comm.n], refs[comm.n:2 * comm.n]) + tuple(refs[2 * comm.n:])
        comm.start(*c_refs)
        comm.wait(*c_refs)

    return pl.pallas_call(body, name=name, in_specs=[ANY] * comm.n, out_specs=[ANY] * comm.n, out_shape=comm.out_shape,
                          scratch_shapes=comm.scratch)(*comm.arrays)


def _in_fwd(h, g, shift, scale, w, dt_a, dt_b, name, pad_rows=0, comm=None, tm=512):
    S = h.shape[0]
    n_pad = pad_rows // tm

    def body(h_ref, g_ref, sh_ref, sc_ref, w_hbm, u_ref, oa_ref, ob_ref, w_v, sem):
        i = pl.program_id(0)

        @pl.when(i == 0)
        def _():
            cp = pltpu.make_async_copy(w_hbm, w_v, sem)
            cp.start()
            cp.wait()

        hh = h_ref[...]
        r = lax.rsqrt(jnp.mean(hh * hh, axis=-1, keepdims=True) + EPS)
        u = (hh * r * g_ref[...]) * (1.0 + sc_ref[...]) + sh_ref[...]
        ub = u.astype(BF16)
        u_ref[...] = ub
        for q in range(NCHIP):
            o_ref = oa_ref if q < 2 else ob_ref
            o_ref[:, (q % 2) * D:(q % 2 + 1) * D] = _nn(ub, w_v[q]).astype(o_ref.dtype)

        if n_pad:
            @pl.when(i < n_pad)
            def _():
                oa_ref[...] = jnp.zeros(oa_ref.shape, oa_ref.dtype)
                ob_ref[...] = jnp.zeros(ob_ref.shape, ob_ref.dtype)

    def src(i):
        return (jnp.maximum(i - n_pad, 0), 0)

    outs, landed = _call(
        body, name, (S // tm + n_pad,),
        [pl.BlockSpec((tm, D), src), _row(D), _row(D), _row(D), ANY],
        [pl.BlockSpec((tm, D), src), pl.BlockSpec((tm, E), lambda i: (i, 0)), pl.BlockSpec((tm, E), lambda i: (i, 0))],
        [jax.ShapeDtypeStruct((S, D), BF16), jax.ShapeDtypeStruct((S + pad_rows, E), dt_a),
         jax.ShapeDtypeStruct((S + pad_rows, E), dt_b)],
        [pltpu.VMEM((NCHIP, D, D), BF16), pltpu.SemaphoreType.DMA],
        _params(1, 52), (h, g, shift, scale, w), comm)
    return outs, landed


def _a_fwd(h, g, shift, scale, asc, gate, w_in, wg, w_out, name, comm=None, tm=512):
    S = h.shape[0]

    def body(h_ref, g_ref, sh_ref, sc_ref, as_ref, gate_ref, wi_hbm, wg_hbm, wo_hbm,
             u_ref, z_ref, p_ref, m_ref, y_ref, ho_ref, wi_v, wg_v, wo_v, buf, sems):
        i = pl.program_id(0)

        @pl.when(i == 0)
        def _():
            cps = [pltpu.make_async_copy(wi_hbm, wi_v, sems.at[0]), pltpu.make_async_copy(wg_hbm, wg_v, sems.at[1]),
                   pltpu.make_async_copy(wo_hbm, wo_v, sems.at[2])]
            for cp in cps:
                cp.start()
            buf[0:HALO, :] = jnp.zeros((HALO, E), F32)
            for cp in cps:
                cp.wait()

        hh = h_ref[...]
        r = lax.rsqrt(jnp.mean(hh * hh, axis=-1, keepdims=True) + EPS)
        ub = ((hh * r * g_ref[...]) * (1.0 + sc_ref[...]) + sh_ref[...]).astype(BF16)
        u_ref[...] = ub
        for q in range(2):
            buf[HALO:HALO + tm, q * D:(q + 1) * D] = _nn(ub, wi_v[q])
        t = i * tm + lax.broadcasted_iota(jnp.int32, (tm, 1), 0)
        y = None
        for gi, w in enumerate(POOL_W):
            cols = slice(gi * GW, (gi + 1) * GW)
            x = buf[:, cols]
            s = x
            k = 1
            while k < w:
                s = s + pltpu.roll(s, k, 0)
                k *= 2
            inv_cnt = 1.0 / jnp.minimum(t + 1, w).astype(F32)
            pb = (s[HALO:, :] * inv_cnt - x[HALO:, :]).astype(BF16)
            p_ref[:, cols] = pb
            mb = _nn(pb, wg_v[gi]).astype(BF16)
            m_ref[:, cols] = mb
            zb = _nn(ub, wi_v[2 + gi // 2, :, (gi % 2) * GW:(gi % 2 + 1) * GW]).astype(BF16)
            z_ref[:, cols] = zb
            zz = zb.astype(F32)
            act = ((mb.astype(F32) * as_ref[:, cols]) * (zz * jax.nn.sigmoid(zz))).astype(BF16)
            part = _nn(act, wo_v[gi])
            y = part if y is None else y + part
        buf[0:HALO, :] = buf[tm:tm + HALO, :]
        y_ref[...] = y.astype(BF16)
        ho_ref[...] = hh + gate_ref[...] * y

    rows_d = pl.BlockSpec((tm, D), lambda i: (i, 0))
    rows_e = pl.BlockSpec((tm, E), lambda i: (i, 0))
    return _call(
        body, name, (S // tm,),
        [rows_d, _row(D), _row(D), _row(D), _row(E), _row(D), ANY, ANY, ANY],
        [rows_d, rows_e, rows_e, rows_e, rows_d, rows_d],
        [jax.ShapeDtypeStruct((S, D), BF16), jax.ShapeDtypeStruct((S, E), BF16), jax.ShapeDtypeStruct((S, E), BF16),
         jax.ShapeDtypeStruct((S, E), BF16), jax.ShapeDtypeStruct((S, D), BF16), jax.ShapeDtypeStruct((S, D), F32)],
        [pltpu.VMEM((NCHIP, D, D), BF16), pltpu.VMEM((4, GW, GW), BF16), pltpu.VMEM((NCHIP, GW, D), BF16),
         pltpu.VMEM((tm + HALO, E), F32), pltpu.SemaphoreType.DMA((3,))],
        _params(1, 60), (h, g, shift, scale, asc, gate, w_in, wg, w_out), comm)


def _out_fwd(a, z, w, gate, h, name, head=None, comm=None, tm=512):
    S = h.shape[0]
    kb = E // NCHIP
    n_in = 5 if head is None else 7

    def body(*refs):
        a_ref, z_ref, w_hbm, gate_ref, h_ref = refs[:5]
        w_v, sem = refs[-2:]
        i = pl.program_id(0)

        @pl.when(i == 0)
        def _():
            cp = pltpu.make_async_copy(w_hbm, w_v, sem)
            cp.start()
            cp.wait()

        y = None
        for p in range(NCHIP):
            cols = slice(p * kb, (p + 1) * kb)
            zz = z_ref[:, cols].astype(F32)
            act = (a_ref[:, cols].astype(F32) * (zz * jax.nn.sigmoid(zz))).astype(BF16)
            part = _nn(act, w_v[p])
            y = part if y is None else y + part
        refs[n_in][...] = y.astype(BF16)
        hh = h_ref[...] + gate_ref[...] * y
        if head is None:
            refs[n_in + 1][...] = hh
            return
        g_ref, t_ref = refs[5:7]
        dh_ref, st_ref = refs[n_in + 1:n_in + 3]

        @pl.when(i == 0)
        def _():
            st_ref[...] = jnp.zeros((SUBLANES, D), F32)

        r = lax.rsqrt(jnp.mean(hh * hh, axis=-1, keepdims=True) + EPS)
        xhat = hh * r
        diff = xhat * g_ref[...] - t_ref[...]
        st_ref[1:2, :] += _colsum(diff * diff)
        dout = diff * (1.0 / D)
        st_ref[0:1, :] += _colsum(dout * xhat)
        dx = dout * g_ref[...]
        dh_ref[...] = r * (dx - xhat * jnp.mean(dx * xhat, axis=-1, keepdims=True))

    rows_d = pl.BlockSpec((tm, D), lambda i: (i, 0))
    rows_e = pl.BlockSpec((tm, E), lambda i: (i, 0))
    in_specs = [rows_e, rows_e, ANY, _row(D), rows_d]
    out_specs = [rows_d, rows_d]
    out_shape = [jax.ShapeDtypeStruct((S, D), BF16), jax.ShapeDtypeStruct((S, D), F32)]
    args = (a, z, w, gate, h)
    if head is not None:
        in_specs += [_row(D), rows_d]
        out_specs += [pl.BlockSpec((SUBLANES, D), lambda i: (0, 0))]
        out_shape += [jax.ShapeDtypeStruct((SUBLANES, D), F32)]
        args += tuple(head)
    return _call(body, name, (S // tm,), in_specs, out_specs, out_shape,
                 [pltpu.VMEM((NCHIP, kb, D), BF16), pltpu.SemaphoreType.DMA], _params(1, 52), args, comm)


TW = BW + LANES


def _diag_onehot(transpose):
    shape = (TW, NRELP) if transpose else (NRELP, TW)
    j = lax.broadcasted_iota(jnp.int32, shape, 0 if transpose else 1)
    r = lax.broadcasted_iota(jnp.int32, shape, 1 if transpose else 0)
    idx = jnp.clip(PAD - (j - LANES), -REL_CLIP, REL_CLIP) + REL_CLIP
    return jnp.where(idx == r, 1.0, 0.0).astype(BF16)


def _strip_valid():
    m = lax.broadcasted_iota(jnp.int32, (NH, BW), 1)
    return m < (LEFT + 1) * CHUNK, m >= CHUNK


def _bias_build(rb, name):
    def body(rb_ref, a_ref, b_ref):
        x = rb_ref[...]
        hi = x.astype(BF16)
        r1 = x - hi.astype(F32)
        mid = r1.astype(BF16)
        lo = (r1 - mid.astype(F32)).astype(BF16)
        oh = _diag_onehot(False)
        diag = (_nn(hi, oh) + _nn(mid, oh)) + _nn(lo, oh)
        valid_a, valid_b = _strip_valid()
        for qi in range(CHUNK):
            a_ref[qi] = jnp.where(valid_a, pltpu.roll(diag, TW - (LANES - qi), 1)[:, :BW], NEG)
            b_ref[qi] = jnp.where(valid_b, pltpu.roll(diag, TW - (CHUNK - qi), 1)[:, :BW], NEG)

    vmem = pl.BlockSpec(memory_space=pltpu.VMEM)
    return pl.pallas_call(
        body, name=name, in_specs=[vmem], out_specs=[vmem, vmem],
        out_shape=[jax.ShapeDtypeStruct((CHUNK, NH, BW), F32), jax.ShapeDtypeStruct((CHUNK, NH, BW), F32)],
        compiler_params=pltpu.CompilerParams(vmem_limit_bytes=32 * 2 ** 20),
    )(rb)


def _dbias_reduce(dba, dbb, name):
    def body(a_ref, b_ref, o_ref):
        valid_a, valid_b = _strip_valid()
        zeros = jnp.zeros((NH, TW - BW), F32)
        acc = jnp.zeros((NH, TW), F32)
        for qi in range(CHUNK):
            xa = jnp.concatenate([jnp.where(valid_a, a_ref[qi], 0.0), zeros], axis=1)
            xb = jnp.concatenate([jnp.where(valid_b, b_ref[qi], 0.0), zeros], axis=1)
            acc = acc + (pltpu.roll(xa, LANES - qi, 1) + pltpu.roll(xb, CHUNK - qi, 1))
        oh = _diag_onehot(True)
        hi = acc.astype(BF16)
        mid = (acc - hi.astype(F32)).astype(BF16)
        r = lax.broadcasted_iota(jnp.int32, (NH, NRELP), 1)
        near = jnp.where(r < 2 * REL_CLIP, _nn(hi, oh) + _nn(mid, oh), 0.0)
        o_ref[...] = jnp.where(r == 2 * REL_CLIP, -jnp.sum(near, axis=-1, keepdims=True), near)

    vmem = pl.BlockSpec(memory_space=pltpu.VMEM)
    return pl.pallas_call(
        body, name=name, in_specs=[vmem, vmem], out_specs=vmem,
        out_shape=jax.ShapeDtypeStruct((NH, NRELP), F32),
        compiler_params=pltpu.CompilerParams(vmem_limit_bytes=32 * 2 ** 20),
    )(dba, dbb)


def _build_bias(bias3, ba_ref, bb_ref):
    bias3[NMASK] = jnp.full((QB, WIN), NEG, F32)
    for qc in range(QC):
        rows = slice(qc * CHUNK, (qc + 1) * CHUNK)
        if qc % 2 == 0:
            bias3[NMASK, rows, qc * CHUNK:qc * CHUNK + BW] = ba_ref[...] * LOG2E
        else:
            bias3[NMASK, rows, (qc - 1) * CHUNK:(qc - 1) * CHUNK + BW] = bb_ref[...] * LOG2E
    col = lax.broadcasted_iota(jnp.int32, (QB, WIN), 1)
    for sub in range(NMASK):
        bias3[sub] = jnp.where(col < PAD - sub * QB, NEG, bias3[NMASK])


def _nsub(S):
    n = min(NSUB, S // QB)
    assert S % (n * QB) == 0 and n >= NMASK
    return n


def _row0(i, sub, nsub):
    return pl.multiple_of((i * nsub + sub) * QB, QB)


def _scores(q_ref, k_ref, i, sub, nsub):
    return _nt(q_ref[sub * QB:(sub + 1) * QB, :], k_ref[pl.ds(_row0(i, sub, nsub), WIN), :])


def _exp_parts(s, bias3, i, sub):
    which = jnp.where(i == 0, sub, NMASK) if sub < NMASK else NMASK
    s = s * (SM_SCALE * LOG2E) + bias3[which]
    e = jnp.exp2(s - jnp.max(s, axis=-1, keepdims=True))
    return e, jnp.sum(e, axis=-1, keepdims=True)


def _attn_fwd(q, kp, vp, ba, bb, name, comm=None):
    S = q.shape[0]
    nsub = _nsub(S)
    R = nsub * QB

    def body(q_ref, k_ref, v_ref, ba_ref, bb_ref, o_ref, p_ref, bias3):
        i = pl.program_id(1)

        @pl.when(i == 0)
        def _():
            _build_bias(bias3, ba_ref, bb_ref)

        s_next = _scores(q_ref, k_ref, i, 0, nsub)
        for sub in range(nsub):
            s = s_next
            if sub + 1 < nsub:
                s_next = _scores(q_ref, k_ref, i, sub + 1, nsub)
            e, l = _exp_parts(s, bias3, i, sub)
            pb = (e * (1.0 / l)).astype(BF16)
            p_ref[sub] = pb
            o_ref[sub * QB:(sub + 1) * QB, :] = _nn(pb, v_ref[pl.ds(_row0(i, sub, nsub), WIN), :]).astype(BF16)

    return _call(
        body, name, (NH, S // R),
        [pl.BlockSpec((R, HD), lambda h, i: (i, h)), pl.BlockSpec((S + PAD, HD), lambda h, i: (0, h)),
         pl.BlockSpec((S + PAD, HD), lambda h, i: (0, h)), pl.BlockSpec((None, CHUNK, BW), lambda h, i: (h, 0, 0)),
         pl.BlockSpec((None, CHUNK, BW), lambda h, i: (h, 0, 0))],
        [pl.BlockSpec((R, HD), lambda h, i: (i, h)), pl.BlockSpec((None, nsub, QB, WIN), lambda h, i: (h, i, 0, 0))],
        [jax.ShapeDtypeStruct((S, E), BF16), jax.ShapeDtypeStruct((NH, S // QB, QB, WIN), BF16)],
        [pltpu.VMEM((NMASK + 1, QB, WIN), F32)],
        _params(2, 48), (q, kp, vp, ba, bb), comm)


def _store_grad(acc, stage, dw_hbm, sem):
    for q in range(NCHIP):
        stage[...] = acc[q].astype(BF16)
        cp = pltpu.make_async_copy(stage, dw_hbm.at[q], sem)
        cp.start()
        cp.wait()


def _out_bwd(dh, y, gate, a, cs, z, w, name, comm=None, tm=256):
    S = dh.shape[0]
    kb = E // NCHIP
    cb = 256
    n_t = S // tm

    def body(dh_ref, y_ref, gate_ref, a_ref, cs_ref, z_ref, w_hbm, da_ref, dz_ref, dw_hbm, st_ref, w_v, acc, stage, sem):
        i = pl.program_id(0)

        @pl.when(i == 0)
        def _():
            cp = pltpu.make_async_copy(w_hbm, w_v, sem)
            cp.start()
            acc[...] = jnp.zeros(acc.shape, F32)
            st_ref[...] = jnp.zeros((SUBLANES, D), F32)
            cp.wait()

        dhh = dh_ref[...]
        st_ref[0:1, :] += _colsum(dhh * y_ref[...].astype(F32))
        dy = (dhh * gate_ref[...]).astype(BF16)
        for blk in range(E // cb):
            p, r0 = divmod(blk * cb, kb)
            cols = slice(blk * cb, (blk + 1) * cb)
            zz = z_ref[:, cols].astype(F32)
            sig = jax.nn.sigmoid(zz)
            sz = zz * sig
            ae = a_ref[:, cols].astype(F32) * cs_ref[:, cols]
            acc[p, r0:r0 + cb, :] += _tn((ae * sz).astype(BF16), dy)
            dact = _nt(dy, w_v[p, r0:r0 + cb, :])
            da_ref[:, cols] = (dact * sz).astype(BF16)
            dz_ref[:, cols] = (dact * ae * (sig * (1.0 + zz * (1.0 - sig)))).astype(BF16)

        @pl.when(i == n_t - 1)
        def _():
            _store_grad(acc, stage, dw_hbm, sem)

    return _call(
        body, name, (n_t,),
        [pl.BlockSpec((tm, D), lambda i: (i, 0)), pl.BlockSpec((tm, D), lambda i: (i, 0)), _row(D),
         pl.BlockSpec((tm, E), lambda i: (i, 0)), _row(E), pl.BlockSpec((tm, E), lambda i: (i, 0)), ANY],
        [pl.BlockSpec((tm, E), lambda i: (i, 0)), pl.BlockSpec((tm, E), lambda i: (i, 0)), ANY,
         pl.BlockSpec((SUBLANES, D), lambda i: (0, 0))],
        [jax.ShapeDtypeStruct((S, E), BF16), jax.ShapeDtypeStruct((S, E), BF16),
         jax.ShapeDtypeStruct((NCHIP, kb, D), BF16), jax.ShapeDtypeStruct((SUBLANES, D), F32)],
        [pltpu.VMEM((NCHIP, kb, D), BF16), pltpu.VMEM((NCHIP, kb, D), F32), pltpu.VMEM((kb, D), BF16),
         pltpu.SemaphoreType.DMA],
        _params(1, 52), (dh, y, gate, a, cs, z, w), comm)


def _attn_bwd(q, kp, vp, probs, do, prev, name, comm=None):
    S = q.shape[0]
    nsub = _nsub(S)
    R = nsub * QB
    n_i = S // R
    dt_kv = F32 if prev is None else BF16

    def body(*refs):
        q_ref, k_ref, v_ref, p_ref, do_ref = refs[:5]
        refs = refs[5:]
        if prev is not None:
            pk_hbm, pv_hbm = refs[:2]
            refs = refs[2:]
        dq_ref, dk_ref, dv_ref, dba_ref, dbb_ref, dbias, dk_acc, dv_acc = refs[:8]
        if prev is not None:
            pk_v, pv_v, sems = refs[8:]
        h = pl.program_id(0)
        i = pl.program_id(1)

        def prev_copies():
            cols = pl.ds(pl.multiple_of(h * HD, HD), HD)
            return (pltpu.make_async_copy(pk_hbm.at[:, cols], pk_v, sems.at[0]),
                    pltpu.make_async_copy(pv_hbm.at[:, cols], pv_v, sems.at[1]))

        @pl.when(i == 0)
        def _():
            if prev is not None:
                for cp in prev_copies():
                    cp.start()
            dbias[...] = jnp.zeros((2, CHUNK, DBW), F32)
            dk_acc[...] = jnp.zeros((S + PAD, HD), F32)
            dv_acc[...] = jnp.zeros((S + PAD, HD), F32)

        def mxu_in(sub):
            return _nt(do_ref[sub * QB:(sub + 1) * QB, :], v_ref[pl.ds(_row0(i, sub, nsub), WIN), :])

        nxt = mxu_in(0)
        for sub in range(nsub):
            rows = slice(sub * QB, (sub + 1) * QB)
            win = pl.ds(_row0(i, sub, nsub), WIN)
            dp = nxt
            if sub + 1 < nsub:
                nxt = mxu_in(sub + 1)
            pb = p_ref[sub]
            p = pb.astype(F32)
            ds = p * (dp - jnp.sum(p * dp, axis=-1, keepdims=True))
            for par in range(2):
                part = None
                for qc in range(par, QC, 2):
                    c0 = (qc - par) * CHUNK + BW - DBW
                    blk_ = ds[qc * CHUNK:(qc + 1) * CHUNK, c0:c0 + DBW]
                    part = blk_ if part is None else part + blk_
                dbias[par] += part
            dsb = (ds * SM_SCALE).astype(BF16)
            dq_ref[rows, :] = _nn(dsb, k_ref[win, :]).astype(BF16)
            dk_acc[win, :] += _tn(dsb, q_ref[rows, :])
            dv_acc[win, :] += _tn(pb, do_ref[rows, :])

        @pl.when(i == n_i - 1)
        def _():
            zeros = jnp.zeros((CHUNK, BW - DBW), F32)
            dba_ref[...] = jnp.concatenate([zeros, dbias[0]], axis=1)
            dbb_ref[...] = jnp.concatenate([zeros, dbias[1]], axis=1)
            if prev is None:
                dk_ref[...] = dk_acc[...]
                dv_ref[...] = dv_acc[...]
            else:
                for cp in prev_copies():
                    cp.wait()
                dk_ref[...] = (dk_acc[...] + pk_v[...]).astype(BF16)
                dv_ref[...] = (dv_acc[...] + pv_v[...]).astype(BF16)

    head = pl.BlockSpec((S + PAD, HD), lambda h, i: (0, h))
    strip = pl.BlockSpec((None, CHUNK, BW), lambda h, i: (h, 0, 0))
    blk = pl.BlockSpec((R, HD), lambda h, i: (i, h))
    in_specs = [blk, head, head, pl.BlockSpec((None, nsub, QB, WIN), lambda h, i: (h, i, 0, 0)), blk]
    scratch = [pltpu.VMEM((2, CHUNK, DBW), F32), pltpu.VMEM((S + PAD, HD), F32), pltpu.VMEM((S + PAD, HD), F32)]
    args = (q, kp, vp, probs, do)
    if prev is not None:
        in_specs += [ANY, ANY]
        scratch += [pltpu.VMEM((S + PAD, HD), F32), pltpu.VMEM((S + PAD, HD), F32), pltpu.SemaphoreType.DMA((2,))]
        args += tuple(prev)
    return _call(
        body, name, (NH, n_i), in_specs, [blk, head, head, strip, strip],
        [jax.ShapeDtypeStruct((S, E), BF16), jax.ShapeDtypeStruct((S + PAD, E), dt_kv),
         jax.ShapeDtypeStruct((S + PAD, E), dt_kv), jax.ShapeDtypeStruct((NH, CHUNK, BW), F32),
         jax.ShapeDtypeStruct((NH, CHUNK, BW), F32)],
        scratch, _params(2, 56), args, comm)


def _pool_bwd(dms, mixed, pooled, wg, a_scale, name, comm=None, tm=512):
    S = dms.shape[0]
    n_t = S // tm

    def rev(i):
        return (n_t - 1 - i, 0)

    def body(d_ref, m_ref, p_ref, wg_ref, as_ref, dv_ref, dwg_ref, st_ref, buf):
        i = pl.program_id(0)

        @pl.when(i == 0)
        def _():
            buf[tm:tm + HALO, :] = jnp.zeros((HALO, E), F32)
            dwg_ref[...] = jnp.zeros((4, GW, GW), F32)
            st_ref[...] = jnp.zeros((SUBLANES, E), F32)

        t = (n_t - 1 - i) * tm + lax.broadcasted_iota(jnp.int32, (tm, 1), 0)
        st_ref[0:1, :] += _colsum(d_ref[...].astype(F32) * m_ref[...].astype(F32))
        for gi, w in enumerate(POOL_W):
            cols = slice(gi * GW, (gi + 1) * GW)
            dm = (d_ref[:, cols].astype(F32) * as_ref[:, cols]).astype(BF16)
            dpool = _nt(dm, wg_ref[gi])
            dwg_ref[gi] += _tn(p_ref[:, cols], dm)
            inv_cnt = 1.0 / jnp.minimum(t + 1, w).astype(F32)
            buf[0:tm, cols] = dpool * inv_cnt
            s = buf[:, cols]
            k = 1
            while k < w:
                s = s + pltpu.roll(s, tm + HALO - k, 0)
                k *= 2
            dv_ref[:, cols] = (s[0:tm, :] - dpool).astype(BF16)
        buf[tm:tm + HALO, :] = buf[0:HALO, :]

    return _call(
        body, name, (n_t,),
        [pl.BlockSpec((tm, E), rev), pl.BlockSpec((tm, E), rev), pl.BlockSpec((tm, E), rev),
         pl.BlockSpec((4, GW, GW), lambda i: (0, 0, 0)), _row(E)],
        [pl.BlockSpec((tm, E), rev), pl.BlockSpec((4, GW, GW), lambda i: (0, 0, 0)),
         pl.BlockSpec((SUBLANES, E), lambda i: (0, 0))],
        [jax.ShapeDtypeStruct((S, E), BF16), jax.ShapeDtypeStruct((4, GW, GW), F32),
         jax.ShapeDtypeStruct((SUBLANES, E), F32)],
        [pltpu.VMEM((tm + HALO, E), F32)],
        _params(1, 52), (dms, mixed, pooled, wg, a_scale), comm)


def _in_bwd(da, db, row_off, u, h, g, scale, w, dh_out, name, comm=None, tm=256):
    S = h.shape[0]
    n_t = S // tm
    off = row_off // tm

    def body(da_ref, db_ref, u_ref, h_ref, g_ref, sc_ref, w_hbm, dho_ref, dhi_ref, dw_hbm, st_ref, w_v, acc, stage, sem):
        i = pl.program_id(0)

        @pl.when(i == 0)
        def _():
            cp = pltpu.make_async_copy(w_hbm, w_v, sem)
            cp.start()
            acc[...] = jnp.zeros(acc.shape, F32)
            st_ref[...] = jnp.zeros((SUBLANES, D), F32)
            cp.wait()

        ub = u_ref[...]
        du = None
        for q in range(NCHIP):
            d_ref = da_ref if q < 2 else db_ref
            dv = d_ref[:, (q % 2) * D:(q % 2 + 1) * D]
            acc[q] += _tn(ub, dv)
            part = _nt(dv, w_v[q])
            du = part if du is None else du + part

        hh = h_ref[...]
        r = lax.rsqrt(jnp.mean(hh * hh, axis=-1, keepdims=True) + EPS)
        xhat = hh * r
        gg = g_ref[...]
        st_ref[0:1, :] += _colsum(du)
        st_ref[1:2, :] += _colsum(du * (xhat * gg))
        dn = du * (1.0 + sc_ref[...])
        st_ref[2:3, :] += _colsum(dn * xhat)
        dx = dn * gg
        dhi_ref[...] = dho_ref[...] + r * (dx - xhat * jnp.mean(dx * xhat, axis=-1, keepdims=True))

        @pl.when(i == n_t - 1)
        def _():
            _store_grad(acc, stage, dw_hbm, sem)

    part_spec = pl.BlockSpec((tm, E), lambda i: (i + off, 0))
    return _call(
        body, name, (n_t,),
        [part_spec, part_spec, pl.BlockSpec((tm, D), lambda i: (i, 0)), pl.BlockSpec((tm, D), lambda i: (i, 0)),
         _row(D), _row(D), ANY, pl.BlockSpec((tm, D), lambda i: (i, 0))],
        [pl.BlockSpec((tm, D), lambda i: (i, 0)), ANY, pl.BlockSpec((SUBLANES, D), lambda i: (0, 0))],
        [jax.ShapeDtypeStruct((S, D), F32), jax.ShapeDtypeStruct((NCHIP, D, D), BF16),
         jax.ShapeDtypeStruct((SUBLANES, D), F32)],
        [pltpu.VMEM((NCHIP, D, D), BF16), pltpu.VMEM((NCHIP, D, D), F32), pltpu.VMEM((D, D), BF16),
         pltpu.SemaphoreType.DMA],
        _params(1, 56), (da, db, u, h, g, scale, w, dh_out), comm)


def _cmat(c_all, w, b, name):
    L, _, n = w.shape

    def body(c_ref, w_ref, b_ref, ca_ref, o_ref):
        cc = c_ref[...]
        ca = cc * jax.nn.sigmoid(cc)
        ca_ref[...] = ca
        o_ref[...] = _nn(ca.astype(BF16), w_ref[...].astype(BF16)) + b_ref[...]

    return pl.pallas_call(
        body, name=name, grid=(L,),
        in_specs=[pl.BlockSpec((SUBLANES, D), lambda l: (0, 0)), pl.BlockSpec((None, D, n), lambda l: (l, 0, 0)),
                  pl.BlockSpec((None, 1, n), lambda l: (l, 0, 0))],
        out_specs=[pl.BlockSpec((SUBLANES, D), lambda l: (0, 0)), pl.BlockSpec((None, SUBLANES, n), lambda l: (l, 0, 0))],
        out_shape=[jax.ShapeDtypeStruct((SUBLANES, D), F32), jax.ShapeDtypeStruct((L, SUBLANES, n), F32)],
        compiler_params=_params(1, 32),
    )(c_all, w, b)


def _grad_ada(c_act_t, dmod, name):
    L, _, n = dmod.shape

    def body(c_ref, d_ref, o_ref):
        acc = None
        for b in range(SUBLANES):
            part = c_ref[:, b:b + 1] * d_ref[b:b + 1, :]
            acc = part if acc is None else acc + part
        o_ref[...] = acc

    return pl.pallas_call(
        body, name=name, grid=(L,),
        in_specs=[pl.BlockSpec((D, SUBLANES), lambda l: (0, 0)), pl.BlockSpec((None, SUBLANES, n), lambda l: (l, 0, 0))],
        out_specs=pl.BlockSpec((None, D, n), lambda l: (l, 0, 0)),
        out_shape=jax.ShapeDtypeStruct((L, D, n), F32),
        compiler_params=_params(1, 32),
    )(c_act_t, dmod)


def _stats_reduce(g3, loss_row, name):
    n_dev, rows, _ = g3.shape

    def body(g_ref, o_ref, l_ref):
        acc = g_ref[0]
        for d in range(1, n_dev):
            acc = acc + g_ref[d]
        o_ref[...] = acc
        tot = jnp.sum(o_ref[loss_row:loss_row + 1, :], axis=-1, keepdims=True)
        l_ref[...] = jnp.broadcast_to(tot * (0.5 / D), (SUBLANES, LANES))

    return pl.pallas_call(
        body, name=name,
        in_specs=[pl.BlockSpec(memory_space=pltpu.VMEM)],
        out_specs=[pl.BlockSpec(memory_space=pltpu.VMEM), pl.BlockSpec(memory_space=pltpu.VMEM)],
        out_shape=[jax.ShapeDtypeStruct((rows, D), F32), jax.ShapeDtypeStruct((SUBLANES, LANES), F32)],
        compiler_params=pltpu.CompilerParams(vmem_limit_bytes=32 * 2 ** 20),
    )(g3)


def _sum4(own, land, chip, name, tr=256):
    _, R, C = own.shape
    tr = min(tr, R)

    def body(p_ref, own_ref, land_ref, o_ref):
        o_ref[...] = ((own_ref[...].astype(F32) + land_ref[0].astype(F32)) + land_ref[1].astype(F32)) + land_ref[2].astype(F32)

    return pl.pallas_call(
        body, name=name,
        grid_spec=pltpu.PrefetchScalarGridSpec(
            num_scalar_prefetch=1, grid=(R // tr,),
            in_specs=[pl.BlockSpec((None, tr, C), lambda i, p: (p[0], i, 0)), pl.BlockSpec((3, tr, C), lambda i, p: (0, i, 0))],
            out_specs=pl.BlockSpec((tr, C), lambda i, p: (i, 0))),
        out_shape=jax.ShapeDtypeStruct((R, C), F32),
        compiler_params=_params(1, 32),
    )(chip, own, land)


def _adamw(w, m, v, g, name, tr=512):
    L, R, C = w.shape
    tr = min(tr, R)
    stacked = not isinstance(g, (list, tuple))
    n_g = None if stacked else [len(ps) for ps in g]
    flat = [g] if stacked else [a for ps in g for a in ps]

    def body(*refs):
        w_ref, m_ref, v_ref = refs[:3]
        g_refs = refs[3:3 + len(flat)]
        go_ref, d_ref, mo_ref, vo_ref = refs[3 + len(flat):]
        if stacked:
            gg = g_refs[0][...]
        else:
            layer = pl.program_id(0)
            gg = None
            k = 0
            for li in range(L):
                gl = None
                for _ in range(n_g[li]):
                    x = g_refs[k][...]
                    gl = x if gl is None else gl + x
                    k += 1
                gg = gl if gg is None else jnp.where(layer == li, gl, gg)
        m2 = ADAM_B1 * m_ref[...] + (1.0 - ADAM_B1) * gg
        v2 = ADAM_B2 * v_ref[...] + (1.0 - ADAM_B2) * (gg * gg)
        m_hat = m2 / (1.0 - ADAM_B1 ** ADAM_STEP)
        v_hat = v2 / (1.0 - ADAM_B2 ** ADAM_STEP)
        go_ref[...] = gg
        d_ref[...] = -ADAM_LR * (m_hat / (jnp.sqrt(v_hat) + ADAM_EPS) + ADAM_WD * w_ref[...])
        mo_ref[...] = m2
        vo_ref[...] = v2

    big = pl.BlockSpec((None, tr, C), lambda l, i: (l, i, 0))
    g_specs = [big] if stacked else [pl.BlockSpec((tr, C), lambda l, i: (i, 0))] * len(flat)
    return pl.pallas_call(
        body, name=name, grid=(L, R // tr),
        in_specs=[big, big, big] + g_specs,
        out_specs=[big, big, big, big],
        out_shape=[jax.ShapeDtypeStruct((L, R, C), F32)] * 4,
        compiler_params=_params(2, 48),
    )(w, m, v, *flat)


def _allgather8(xs, name, comm=None):
    m, n = xs.shape
    n_c = 0 if comm is None else comm.n

    def body(*refs):
        x_ref, out_ref = refs[0], refs[1 + n_c]
        send_sems, recv_sems, local_sem = refs[2 + 2 * n_c:5 + 2 * n_c]
        c_refs = (refs[1:1 + n_c], refs[2 + n_c:2 + 2 * n_c]) + tuple(refs[5 + 2 * n_c:])
        if comm is not None:
            comm.start(*c_refs)
        x, y, c = _place()
        me, sibling = (x, y, c), (x, y, 1 - c)
        chips = [(1 - x, y), (x, 1 - y), (1 - x, 1 - y)]

        def rows(px, py, pc):
            return out_ref.at[pl.ds((4 * px + 2 * py + pc) * m, m), :]

        def copy(k, block, to, src=None):
            return pltpu.make_async_remote_copy(
                src_ref=rows(*block) if src is None else src, dst_ref=rows(*block),
                send_sem=send_sems.at[k], recv_sem=recv_sems.at[k], device_id=to, device_id_type=MESH)

        mine = pltpu.make_async_copy(x_ref, rows(*me), local_sem)
        mine.start()
        first = [copy(0, me, sibling, src=x_ref)]
        first += [copy(1 + j, me, (*chip, c), src=x_ref) for j, chip in enumerate(chips)]
        for cp in first:
            cp.start()
        passed = [copy(4 + j, (*chip, c), sibling) for j, chip in enumerate(chips)]
        for j, chip in enumerate(chips):
            copy(1 + j, (*chip, c), me).wait_recv()
            passed[j].start()
        copy(0, sibling, me).wait_recv()
        for j, chip in enumerate(chips):
            copy(4 + j, (*chip, 1 - c), me).wait_recv()
        for cp in first + passed:
            cp.wait_send()
        mine.wait()
        if comm is not None:
            comm.wait(*c_refs)

    vmem = pl.BlockSpec(memory_space=pltpu.VMEM)
    outs = pl.pallas_call(
        body, name=name,
        out_shape=[jax.ShapeDtypeStruct((8 * m, n), xs.dtype)] + ([] if comm is None else comm.out_shape),
        in_specs=[vmem] + [ANY] * n_c,
        out_specs=[vmem] + [ANY] * n_c,
        scratch_shapes=[pltpu.SemaphoreType.DMA((7,)), pltpu.SemaphoreType.DMA((7,)), pltpu.SemaphoreType.DMA]
        + ([] if comm is None else comm.scratch),
        compiler_params=pltpu.CompilerParams(vmem_limit_bytes=32 * 2 ** 20),
    )(xs, *([] if comm is None else comm.arrays))
    return outs[0], list(outs[1:])


def _pad8(a):
    return jnp.pad(a, ((0, SUBLANES - a.shape[0]), (0, 0)))


def _group_rows(wg):
    return wg.transpose(1, 0, 2, 3).reshape(4, GW, GW)


def _example_step(h0, tgt, mods, kvmod, a_scale, norm_g, kv_norm_g, final_g, b_rel_bias, sh, w_first, chip_arr):
    ones_e = jnp.ones((1, E), F32)
    shift = [mods[l:l + 1, 0:D] for l in range(4)]
    scale = [mods[l:l + 1, D:2 * D] for l in range(4)]
    gate = [mods[l:l + 1, 2 * D:3 * D] for l in range(4)]
    gl = [norm_g[l:l + 1] for l in range(4)]
    kv_shift, kv_scale = kvmod[None, 0:D], kvmod[None, D:2 * D]
    kv_g = kv_norm_g[None]

    w_a = w_first
    hs = [h0]
    saved = []
    nxt = [[sh["a_in"][1], sh["a_grp"][1], sh["a_out"][1]], [sh["kv"][0], sh["b_in"][0]]]
    for l in range(2):
        w_in_l, wg_l, wo_l = w_a
        wg_full = _group_rows(wg_l)
        (u, z, pooled, mixed, y, hn), got = _a_fwd(hs[-1], gl[l], shift[l], scale[l], a_scale[l:l + 1], gate[l], w_in_l,
                                                   wg_full, wo_l, f"a{l}_fwd", comm=_Comm(gathers=nxt[l]))
        saved.append((u, z, pooled, mixed, y, w_in_l, wg_full, wo_l))
        hs.append(hn)
        if l == 0:
            w_a = got
        else:
            w_kv, wb_in0 = got

    (uk, kp, vp), _ = _in_fwd(hs[2], kv_g, kv_shift, kv_scale, w_kv, BF16, BF16, "kv_in_fwd", pad_rows=PAD)
    wb_in = [wb_in0, None]
    wb_out = [None, None]

    for bi in range(2):
        l = 2 + bi
        sa, sb = _bias_build(jnp.pad(b_rel_bias[bi], ((0, 0), (0, NRELP - NREL))), f"b{bi}_bias")
        (u, q, z), _ = _in_fwd(hs[-1], gl[l], shift[l], scale[l], wb_in[bi], BF16, BF16, f"b{bi}_in_fwd")
        comm = _Comm(gathers=[sh["b_out"][0], sh["b_in"][1], sh["b_out"][1]]) if bi == 0 else None
        (att, probs), got = _attn_fwd(q, kp, vp, sa.transpose(1, 0, 2), sb.transpose(1, 0, 2), f"b{bi}_attn_fwd", comm=comm)
        if bi == 0:
            wb_out[0], wb_in[1], wb_out[1] = got
        if bi == 0:
            (y, hn), _ = _out_fwd(att, z, wb_out[bi], gate[l], hs[-1], f"b{bi}_out_fwd")
            hs.append(hn)
        else:
            (y, dh, st_fin), _ = _out_fwd(att, z, wb_out[bi], gate[l], hs[-1], f"b{bi}_out_fwd", head=(final_g[None], tgt))
        saved.append((u, z, q, att, y, probs))

    st_in = [None] * 4
    st_out = [None] * 4
    grads = {}
    landed = {}

    def carry(names):
        return _Comm(scatters=[grads[n] for n in names]) if names else None

    def land(names, got):
        for n, a in zip(names, got):
            landed[n] = a

    u, z, q, att, y, probs = saved[3]
    (datt, dz, grads["b_out1"], st_out[3]), _ = _out_bwd(dh, y, gate[3], att, ones_e, z, wb_out[1], "b1_out_bwd")
    (dq, dk1, dv1, dsa, dsb), got = _attn_bwd(q, kp, vp, probs, datt, None, "b1_attn_bwd", comm=carry(["b_out1"]))
    land(["b_out1"], got)
    drb1 = _dbias_reduce(dsa.transpose(1, 0, 2), dsb.transpose(1, 0, 2), "b1_dbias")
    (dh, grads["b_in1"], st_in[3]), _ = _in_bwd(dq, dz, 0, u, hs[3], gl[3], scale[3], wb_in[1], dh, "b1_in_bwd")
    u, z, q, att, y, probs = saved[2]
    (datt, dz, grads["b_out0"], st_out[2]), _ = _out_bwd(dh, y, gate[2], att, ones_e, z, wb_out[0], "b0_out_bwd")
    (dq, dk, dv, dsa, dsb), got = _attn_bwd(q, kp, vp, probs, datt, (dk1, dv1), "b0_attn_bwd",
                                            comm=carry(["b_in1", "b_out0"]))
    land(["b_in1", "b_out0"], got)
    drb0 = _dbias_reduce(dsa.transpose(1, 0, 2), dsb.transpose(1, 0, 2), "b0_dbias")
    (dh, grads["b_in0"], st_in[2]), _ = _in_bwd(dq, dz, 0, u, hs[2], gl[2], scale[2], wb_in[0], dh, "b0_in_bwd")
    (dh, grads["kv"], st_kv), got = _in_bwd(dk, dv, PAD, uk, hs[2], kv_g, kv_scale, w_kv, dh, "kv_in_bwd",
                                            comm=carry(["b_in0"]))
    land(["b_in0"], got)
    st_pool = [None] * 2
    plan = {1: dict(o=[], p=["a_out1"], i=["kv", "a_grp1"]), 0: dict(o=["a_in1"], p=["a_out0"], i=[])}
    early = ["b_out1", "b_in1", "b_out0", "b_in0", "kv", "a_out1", "a_grp1", "a_in1"]
    late = ["a_out0", "a_grp0", "a_in0"]
    both = {}

    def sum4(n):
        return _sum4(grads[n], landed[n], chip_arr, f"sum4_{n}")

    for l in (1, 0):
        u, z, pooled, mixed, y, w_in_l, wg_full, wo = saved[l]
        asl = a_scale[l:l + 1]
        (dms, dz, grads[f"a_out{l}"], st_out[l]), got = _out_bwd(dh, y, gate[l], mixed, asl, z, wo, f"a{l}_out_bwd",
                                                                comm=carry(plan[l]["o"]))
        land(plan[l]["o"], got)
        comm = carry(plan[l]["p"])
        if l == 0:
            mine = [sum4(n) for n in early]
            comm = _Comm(scatters=[grads[n] for n in plan[l]["p"]], swaps=mine)
        (dval, dwg, st_pool[l]), got = _pool_bwd(dms, mixed, pooled, wg_full, asl, f"a{l}_pool_bwd", comm=comm)
        land(plan[l]["p"], got)
        if l == 0:
            both.update({n: [a, b] for n, a, b in zip(early, mine, got[len(plan[l]["p"]):])})
        grads[f"a_grp{l}"] = (dwg.reshape(4, NCHIP, GW // NCHIP, GW).transpose(1, 0, 2, 3).reshape(NCHIP, GW, GW)
                              .astype(BF16))
        (dh, grads[f"a_in{l}"], st_in[l]), got = _in_bwd(dval, dz, 0, u, hs[l], gl[l], scale[l], w_in_l, dh, f"a{l}_in_bwd",
                                                         comm=carry(plan[l]["i"]))
        land(plan[l]["i"], got)
    pieces = st_in + [st_kv] + st_out + [st_fin]
    pieces += [_pad8(st_pool[l][0].reshape(2, D)) for l in range(2)]
    pieces += [_pad8(d.reshape(NH * NRELP // D, D)) for d in (drb0, drb1)]
    gathered, got = _allgather8(jnp.concatenate(pieces, axis=0), "gather_stats", comm=carry(["a_grp0", "a_in0"]))
    land(["a_grp0", "a_in0"], got)
    mine = [sum4(n) for n in late]
    both.update({n: [a, b] for n, a, b in zip(late, mine, _comm_only(_Comm(swaps=mine), "swap_last"))})
    return dh, both, gathered.reshape(8, N_STAT, D)


ROW_IN = [8 * l for l in range(4)]
ROW_KV = 32
ROW_OUT = [40 + 8 * l for l in range(4)]
ROW_FIN = 72
ROW_ASC = [80, 88]
ROW_RB = [96, 104]
N_STAT = 112


def kernel(x, c, ada_w, ada_b, norm_g, a_w_in, a_w_group, a_scale, a_w_out, kv_norm_g, kv_ada_w, kv_ada_b, w_kv, b_w_in, b_rel_bias, b_w_out, final_g, loss_target, m_ada_w, m_ada_b, m_norm_g, m_a_w_in, m_a_w_group, m_a_scale, m_a_w_out, m_kv_norm_g, m_kv_ada_w, m_kv_ada_b, m_w_kv, m_b_w_in, m_b_rel_bias, m_b_w_out, m_final_g, v_ada_w, v_ada_b, v_norm_g, v_a_w_in, v_a_w_group, v_a_scale, v_a_w_out, v_kv_norm_g, v_kv_ada_w, v_kv_ada_b, v_w_kv, v_b_w_in, v_b_rel_bias, v_b_w_out, v_final_g):
    xi, yi, ci = _place()
    chip = 2 * xi + yi
    dev = 4 * xi + 2 * yi + ci
    n_ada = ada_w.shape[2]
    n_kva = kv_ada_w.shape[1]
    n_asc = a_scale.shape[1]

    c_all = _allgather8(jnp.broadcast_to(c, (SUBLANES, D)), "gather_c")[0][::SUBLANES]
    ada_b_sh = lax.dynamic_slice_in_dim(ada_b, chip * n_ada, n_ada, axis=1)
    kvb_sh = lax.dynamic_slice_in_dim(kv_ada_b, chip * n_kva, n_kva, axis=0)
    c_act, mod_ada = _cmat(c_all, ada_w, ada_b_sh[:, None, :], "mod_ada")
    _, mod_kv = _cmat(c_all, kv_ada_w[None], kvb_sh[None, None, :], "mod_kv")
    part = jnp.concatenate([mod_ada.transpose(1, 0, 2).reshape(SUBLANES, 4 * n_ada), mod_kv[0],
                            jnp.broadcast_to(a_scale.reshape(1, 2 * n_asc), (SUBLANES, 2 * n_asc))], axis=1)
    sh = dict(a_in=[a_w_in[l].astype(BF16) for l in range(2)], a_grp=[a_w_group[l].astype(BF16) for l in range(2)],
              a_out=[a_w_out[l].astype(BF16) for l in range(2)], kv=[w_kv.astype(BF16)],
              b_in=[b_w_in[l].astype(BF16) for l in range(2)], b_out=[b_w_out[l].astype(BF16) for l in range(2)])
    gathered, w_first = _allgather8(part, "gather_mod", comm=_Comm(gathers=[sh["a_in"][0], sh["a_grp"][0], sh["a_out"][0]]))
    rows = jnp.concatenate([lax.dynamic_slice_in_dim(gathered, SUBLANES * (2 * p + ci) + dev, 1, axis=0)
                            for p in range(NCHIP)], axis=0)
    mods = jnp.stack([rows[:, l * n_ada:(l + 1) * n_ada].reshape(3 * D) for l in range(4)])
    kvmod = rows[:, 4 * n_ada:4 * n_ada + n_kva].reshape(2 * D)
    o_asc = 4 * n_ada + n_kva
    a_scale_full = jnp.stack([rows[:, o_asc + l * n_asc:o_asc + (l + 1) * n_asc].reshape(E) for l in range(2)])

    chip_arr = jnp.reshape(chip, (1,)).astype(jnp.int32)
    dh, both, g3 = _example_step(x[0], loss_target[0], mods, kvmod, a_scale_full, norm_g, kv_norm_g, final_g,
                                 b_rel_bias, sh, w_first, chip_arr)
    grad_x = dh[None]

    red, loss_tile = _stats_reduce(g3, ROW_FIN + 1, "stats_reduce")
    loss = loss_tile[0, 0]

    def cat(rows_):
        return jnp.concatenate(rows_, axis=-1)

    g_ada_b = jnp.stack([cat([red[ROW_IN[l]], red[ROW_IN[l] + 1], red[ROW_OUT[l]]]) for l in range(4)])
    g_norm_g = jnp.stack([red[ROW_IN[l] + 2] for l in range(4)])
    g_kv_norm_g = red[ROW_KV + 2]
    g_kv_ada_b = cat([red[ROW_KV], red[ROW_KV + 1]])
    g_final_g = red[ROW_FIN]
    g_asc_full = jnp.stack([red[ROW_ASC[l]:ROW_ASC[l] + 2].reshape(E) for l in range(2)])
    g_a_scale = lax.dynamic_slice_in_dim(g_asc_full, chip * n_asc, n_asc, axis=1)
    g_rel = jnp.stack([red[ROW_RB[bi]:ROW_RB[bi] + NH * NRELP // D].reshape(NH, NRELP)[:, :NREL] for bi in range(2)])

    dmod = jnp.stack([cat([g3[:, ROW_IN[l]], g3[:, ROW_IN[l] + 1], g3[:, ROW_OUT[l]]]) for l in range(4)])
    dmod_sh = lax.dynamic_slice_in_dim(dmod, chip * n_ada, n_ada, axis=2)
    dkv = cat([g3[:, ROW_KV], g3[:, ROW_KV + 1]])[None]
    dkv_sh = lax.dynamic_slice_in_dim(dkv, chip * n_kva, n_kva, axis=2)
    c_act_t = c_act.T
    g_ada_w = _grad_ada(c_act_t, dmod_sh, "grad_ada_w")
    g_kv_ada_w = _grad_ada(c_act_t, dkv_sh, "grad_kv_ada_w")

    def upd(w, m, v, g, name, shape3):
        g = g.reshape(shape3) if not isinstance(g, list) else g
        outs = _adamw(w.reshape(shape3), m.reshape(shape3), v.reshape(shape3), g, name)
        return [o.reshape(w.shape) for o in outs]

    def pair(name):
        return [both[name + "0"], both[name + "1"]]

    res = {}
    res["ada_w"] = upd(ada_w, m_ada_w, v_ada_w, g_ada_w, "adamw_ada_w", ada_w.shape)
    res["ada_b"] = upd(ada_b, m_ada_b, v_ada_b, g_ada_b, "adamw_ada_b", (1,) + ada_b.shape)
    res["norm_g"] = upd(norm_g, m_norm_g, v_norm_g, g_norm_g, "adamw_norm_g", (1,) + norm_g.shape)
    res["a_w_in"] = upd(a_w_in, m_a_w_in, v_a_w_in, pair("a_in"), "adamw_a_w_in", a_w_in.shape)
    res["a_w_group"] = upd(a_w_group, m_a_w_group, v_a_w_group, pair("a_grp"), "adamw_a_w_group", (2, GW, GW))
    res["a_scale"] = upd(a_scale, m_a_scale, v_a_scale, g_a_scale, "adamw_a_scale", (1,) + a_scale.shape)
    res["a_w_out"] = upd(a_w_out, m_a_w_out, v_a_w_out, pair("a_out"), "adamw_a_w_out", a_w_out.shape)
    res["kv_norm_g"] = upd(kv_norm_g, m_kv_norm_g, v_kv_norm_g, g_kv_norm_g, "adamw_kv_norm_g", (1, 1, D))
    res["kv_ada_w"] = upd(kv_ada_w, m_kv_ada_w, v_kv_ada_w, g_kv_ada_w, "adamw_kv_ada_w", (1,) + kv_ada_w.shape)
    res["kv_ada_b"] = upd(kv_ada_b, m_kv_ada_b, v_kv_ada_b, g_kv_ada_b, "adamw_kv_ada_b", (1, 1, 2 * D))
    res["w_kv"] = upd(w_kv, m_w_kv, v_w_kv, [both["kv"]], "adamw_w_kv", (1,) + w_kv.shape)
    res["b_w_in"] = upd(b_w_in, m_b_w_in, v_b_w_in, pair("b_in"), "adamw_b_w_in", b_w_in.shape)
    res["b_rel_bias"] = upd(b_rel_bias, m_b_rel_bias, v_b_rel_bias, g_rel, "adamw_b_rel_bias", (1, 2 * NH, NREL))
    res["b_w_out"] = upd(b_w_out, m_b_w_out, v_b_w_out, pair("b_out"), "adamw_b_w_out", b_w_out.shape)
    res["final_g"] = upd(final_g, m_final_g, v_final_g, g_final_g, "adamw_final_g", (1, 1, D))

    names = ["ada_w", "ada_b", "norm_g", "a_w_in", "a_w_group", "a_scale", "a_w_out", "kv_norm_g", "kv_ada_w", "kv_ada_b",
             "w_kv", "b_w_in", "b_rel_bias", "b_w_out", "final_g"]
    return (loss, grad_x, *[res[n][0] for n in names], *[res[n][1] for n in names], *[res[n][2] for n in names],
            *[res[n][3] for n in names])
```

```python
import math

import jax
import jax.numpy as jnp
from jax import lax
from jax.experimental import pallas as pl
from jax.experimental.pallas import tpu as pltpu

F32 = jnp.float32
BF16 = jnp.bfloat16

D = 1024
E = 2048
NH = 16
HD = 128
CHUNK = 64
LEFT = 8
PAD = LEFT * CHUNK
NREL = 257
NRELP = 384
REL_CLIP = 128
EPS = 1e-6
NEG = -1e30
LOG2E = math.log2(math.e)
SM_SCALE = HD ** -0.5
POOL_W = (2, 4, 8, 16)
GW = 512
HALO = 16
QC = 4
QB = QC * CHUNK
NMASK = PAD // QB
WIN = (QC + LEFT) * CHUNK
BW = (LEFT + 2) * CHUNK
DBW = 4 * CHUNK
NSUB = 8
NCHIP = 4
LANES = 128
SUBLANES = 8

ADAM_LR = 0.001
ADAM_B1 = 0.9
ADAM_B2 = 0.999
ADAM_EPS = 1e-08
ADAM_WD = 0.01
ADAM_STEP = 10

MESH = pl.DeviceIdType.MESH
ANY = pl.BlockSpec(memory_space=pl.ANY)


def _params(n_axes, vmem_mb):
    return pltpu.CompilerParams(dimension_semantics=("arbitrary",) * n_axes, vmem_limit_bytes=vmem_mb * 2 ** 20)


def _nn(a, b):
    return jnp.dot(a, b, preferred_element_type=F32)


def _nt(a, b):
    return lax.dot_general(a, b, (((1,), (1,)), ((), ())), preferred_element_type=F32)


def _tn(a, b):
    return lax.dot_general(a, b, (((0,), (0,)), ((), ())), preferred_element_type=F32)


def _row(n):
    return pl.BlockSpec((1, n), lambda i: (0, 0))


def _colsum(x):
    return jnp.sum(x, axis=0, keepdims=True)


def _place():
    return lax.axis_index("x"), lax.axis_index("y"), lax.axis_index("c")


class _Comm:
    def __init__(self, gathers=(), scatters=(), swaps=()):
        self.n_g = len(gathers)
        self.n_chip = len(gathers) + len(scatters)
        self.n_sw = len(swaps)
        self.arrays = list(gathers) + list(scatters) + list(swaps)
        self.n = len(self.arrays)
        self.half = [a.shape[0] // 2 for a in gathers]
        self.out_shape = ([jax.ShapeDtypeStruct((NCHIP,) + a.shape, a.dtype) for a in gathers]
                          + [jax.ShapeDtypeStruct((3,) + a.shape[1:], a.dtype) for a in scatters]
                          + [jax.ShapeDtypeStruct(a.shape, a.dtype) for a in swaps])
        n_c, n_f, n_s = max(3 * self.n_chip, 1), max(3 * self.n_g, 1), max(self.n_sw, 1)
        self.scratch = [pltpu.SemaphoreType.DMA((n_c,)), pltpu.SemaphoreType.DMA((n_c,)),
                        pltpu.SemaphoreType.DMA((max(self.n_g, 1),)), pltpu.SemaphoreType.DMA((n_f,)),
                        pltpu.SemaphoreType.DMA((n_f,)), pltpu.SemaphoreType.DMA((n_s,)), pltpu.SemaphoreType.DMA((n_s,))]

    def _chip_copies(self, ins, outs, send, recv, landing):
        x, y, c = _place()
        chips = [(1 - x, y), (x, 1 - y), (1 - x, 1 - y)]
        mine = 2 * x + y
        cps = []
        for k in range(self.n_chip):
            for j, (cx, cy) in enumerate(chips):
                q = 2 * cx + cy
                if k < self.n_g:
                    part = pl.ds(c * self.half[k], self.half[k])
                    src = ins[k].at[part]
                    dst = outs[k].at[q if landing else mine, part]
                else:
                    src = ins[k].at[q]
                    dst = outs[k].at[j]
                cps.append(pltpu.make_async_remote_copy(
                    src_ref=src, dst_ref=dst, send_sem=send.at[3 * k + j], recv_sem=recv.at[3 * k + j],
                    device_id=(cx, cy, c), device_id_type=MESH))
        return cps

    def _core_copies(self, outs, fsend, frecv, landing):
        x, y, c = _place()
        chips = [(1 - x, y), (x, 1 - y), (1 - x, 1 - y)]
        cps = []
        for k in range(self.n_g):
            for j, (cx, cy) in enumerate(chips):
                part = pl.ds((1 - c if landing else c) * self.half[k], self.half[k])
                blk = outs[k].at[2 * cx + cy, part]
                cps.append(pltpu.make_async_remote_copy(
                    src_ref=blk, dst_ref=blk, send_sem=fsend.at[3 * k + j], recv_sem=frecv.at[3 * k + j],
                    device_id=(x, y, 1 - c), device_id_type=MESH))
        return cps

    def _local_copies(self, ins, outs, loc):
        x, y, _ = _place()
        return [pltpu.make_async_copy(ins[k], outs[k].at[2 * x + y], loc.at[k]) for k in range(self.n_g)]

    def _swap_copies(self, ins, outs, ssend, srecv):
        x, y, c = _place()
        return [pltpu.make_async_remote_copy(
            src_ref=ins[k], dst_ref=outs[k], send_sem=ssend.at[k - self.n_chip], recv_sem=srecv.at[k - self.n_chip],
            device_id=(x, y, 1 - c), device_id_type=MESH) for k in range(self.n_chip, self.n)]

    def start(self, ins, outs, send, recv, loc, fsend, frecv, ssend, srecv):
        for cp in (self._local_copies(ins, outs, loc) + self._chip_copies(ins, outs, send, recv, False)
                   + self._swap_copies(ins, outs, ssend, srecv)):
            cp.start()

    def wait(self, ins, outs, send, recv, loc, fsend, frecv, ssend, srecv):
        lands = self._chip_copies(ins, outs, send, recv, True)
        passes = self._core_copies(outs, fsend, frecv, False)
        for k in range(self.n_chip):
            for j in range(3):
                lands[3 * k + j].wait_recv()
                if k < self.n_g:
                    passes[3 * k + j].start()
        for cp in self._core_copies(outs, fsend, frecv, True):
            cp.wait_recv()
        swaps = self._swap_copies(ins, outs, ssend, srecv)
        for cp in swaps:
            cp.wait_recv()
        for cp in self._chip_copies(ins, outs, send, recv, False) + passes + swaps:
            cp.wait_send()
        for cp in self._local_copies(ins, outs, loc):
            cp.wait()


def _call(body, name, grid, in_specs, out_specs, out_shape, scratch, params, args, comm=None):
    n_in, n_out, n_sc = len(in_specs), len(out_specs), len(scratch)
    if comm is None:
        outs = pl.pallas_call(body, name=name, grid=grid, in_specs=in_specs, out_specs=out_specs, out_shape=out_shape,
                              scratch_shapes=scratch, compiler_params=params)(*args)
        return list(outs), []
    n = comm.n
    o0 = n_in + n
    s0 = o0 + n_out + n

    def wrapped(*refs):
        c_refs = (refs[n_in:o0], refs[o0 + n_out:s0]) + tuple(refs[s0 + n_sc:])
        ids = [pl.program_id(a) for a in range(len(grid))]
        first = ids[0] == 0
        last = ids[0] == grid[0] - 1
        for a in range(1, len(grid)):
            first = first & (ids[a] == 0)
            last = last & (ids[a] == grid[a] - 1)

        @pl.when(first)
        def _():
            comm.start(*c_refs)

        body(*refs[:n_in], *refs[o0:o0 + n_out], *refs[s0:s0 + n_sc])

        @pl.when(last)
        def _():
            comm.wait(*c_refs)

    outs = pl.pallas_call(
        wrapped, name=name, grid=grid, in_specs=list(in_specs) + [ANY] * n, out_specs=list(out_specs) + [ANY] * n,
        out_shape=list(out_shape) + comm.out_shape, scratch_shapes=list(scratch) + comm.scratch, compiler_params=params,
    )(*args, *comm.arrays)
    return list(outs[:n_out]), list(outs[n_out:])


def _comm_only(comm, name):
    def body(*refs):
        c_refs = (refs[:comm.n], refs[comm.n:2 * comm.n]) + tuple(refs[2 * comm.n:])
        comm.start(*c_refs)
        comm.wait(*c_refs)

    return pl.pallas_call(body, name=name, in_specs=[ANY] * comm.n, out_specs=[ANY] * comm.n, out_shape=comm.out_shape,
                          scratch_shapes=comm.scratch)(*comm.arrays)


def _in_fwd(h, g, shift, scale, w, dt_a, dt_b, name, pad_rows=0, comm=None, tm=512):
    S = h.shape[0]
    n_pad = pad_rows // tm

    def body(h_ref, g_ref, sh_ref, sc_ref, w_hbm, u_ref, oa_ref, ob_ref, w_v, sem):
        i = pl.program_id(0)

        @pl.when(i == 0)
        def _():
            cp = pltpu.make_async_copy(w_hbm, w_v, sem)
            cp.start()
            cp.wait()

        hh = h_ref[...]
        r = lax.rsqrt(jnp.mean(hh * hh, axis=-1, keepdims=True) + EPS)
        u = (hh * r * g_ref[...]) * (1.0 + sc_ref[...]) + sh_ref[...]
        ub = u.astype(BF16)
        u_ref[...] = ub
        for q in range(NCHIP):
            o_ref = oa_ref if q < 2 else ob_ref
            o_ref[:, (q % 2) * D:(q % 2 + 1) * D] = _nn(ub, w_v[q]).astype(o_ref.dtype)

        if n_pad:
            @pl.when(i < n_pad)
            def _():
                oa_ref[...] = jnp.zeros(oa_ref.shape, oa_ref.dtype)
                ob_ref[...] = jnp.zeros(ob_ref.shape, ob_ref.dtype)

    def src(i):
        return (jnp.maximum(i - n_pad, 0), 0)

    outs, landed = _call(
        body, name, (S // tm + n_pad,),
        [pl.BlockSpec((tm, D), src), _row(D), _row(D), _row(D), ANY],
        [pl.BlockSpec((tm, D), src), pl.BlockSpec((tm, E), lambda i: (i, 0)), pl.BlockSpec((tm, E), lambda i: (i, 0))],
        [jax.ShapeDtypeStruct((S, D), BF16), jax.ShapeDtypeStruct((S + pad_rows, E), dt_a),
         jax.ShapeDtypeStruct((S + pad_rows, E), dt_b)],
        [pltpu.VMEM((NCHIP, D, D), BF16), pltpu.SemaphoreType.DMA],
        _params(1, 52), (h, g, shift, scale, w), comm)
    return outs, landed


def _a_fwd(h, g, shift, scale, asc, gate, w_in, wg, w_out, name, comm=None, tm=512):
    S = h.shape[0]

    def body(h_ref, g_ref, sh_ref, sc_ref, as_ref, gate_ref, wi_hbm, wg_hbm, wo_hbm,
             u_ref, z_ref, p_ref, m_ref, y_ref, ho_ref, wi_v, wg_v, wo_v, buf, sems):
        i = pl.program_id(0)

        @pl.when(i == 0)
        def _():
            cps = [pltpu.make_async_copy(wi_hbm, wi_v, sems.at[0]), pltpu.make_async_copy(wg_hbm, wg_v, sems.at[1]),
                   pltpu.make_async_copy(wo_hbm, wo_v, sems.at[2])]
            for cp in cps:
                cp.start()
            buf[0:HALO, :] = jnp.zeros((HALO, E), F32)
            for cp in cps:
                cp.wait()

        hh = h_ref[...]
        r = lax.rsqrt(jnp.mean(hh * hh, axis=-1, keepdims=True) + EPS)
        ub = ((hh * r * g_ref[...]) * (1.0 + sc_ref[...]) + sh_ref[...]).astype(BF16)
        u_ref[...] = ub
        for q in range(2):
            buf[HALO:HALO + tm, q * D:(q + 1) * D] = _nn(ub, wi_v[q])
        t = i * tm + lax.broadcasted_iota(jnp.int32, (tm, 1), 0)
        y = None
        for gi, w in enumerate(POOL_W):
            cols = slice(gi * GW, (gi + 1) * GW)
            x = buf[:, cols]
            s = x
            k = 1
            while k < w:
                s = s + pltpu.roll(s, k, 0)
                k *= 2
            inv_cnt = 1.0 / jnp.minimum(t + 1, w).astype(F32)
            pb = (s[HALO:, :] * inv_cnt - x[HALO:, :]).astype(BF16)
            p_ref[:, cols] = pb
            mb = _nn(pb, wg_v[gi]).astype(BF16)
            m_ref[:, cols] = mb
            zb = _nn(ub, wi_v[2 + gi // 2, :, (gi % 2) * GW:(gi % 2 + 1) * GW]).astype(BF16)
            z_ref[:, cols] = zb
            zz = zb.astype(F32)
            act = ((mb.astype(F32) * as_ref[:, cols]) * (zz * jax.nn.sigmoid(zz))).astype(BF16)
            part = _nn(act, wo_v[gi])
            y = part if y is None else y + part
        buf[0:HALO, :] = buf[tm:tm + HALO, :]
        y_ref[...] = y.astype(BF16)
        ho_ref[...] = hh + gate_ref[...] * y

    rows_d = pl.BlockSpec((tm, D), lambda i: (i, 0))
    rows_e = pl.BlockSpec((tm, E), lambda i: (i, 0))
    return _call(
        body, name, (S // tm,),
        [rows_d, _row(D), _row(D), _row(D), _row(E), _row(D), ANY, ANY, ANY],
        [rows_d, rows_e, rows_e, rows_e, rows_d, rows_d],
        [jax.ShapeDtypeStruct((S, D), BF16), jax.ShapeDtypeStruct((S, E), BF16), jax.ShapeDtypeStruct((S, E), BF16),
         jax.ShapeDtypeStruct((S, E), BF16), jax.ShapeDtypeStruct((S, D), BF16), jax.ShapeDtypeStruct((S, D), F32)],
        [pltpu.VMEM((NCHIP, D, D), BF16), pltpu.VMEM((4, GW, GW), BF16), pltpu.VMEM((NCHIP, GW, D), BF16),
         pltpu.VMEM((tm + HALO, E), F32), pltpu.SemaphoreType.DMA((3,))],
        _params(1, 60), (h, g, shift, scale, asc, gate, w_in, wg, w_out), comm)


def _out_fwd(a, z, w, gate, h, name, head=None, comm=None, tm=512):
    S = h.shape[0]
    kb = E // NCHIP
    n_in = 5 if head is None else 7

    def body(*refs):
        a_ref, z_ref, w_hbm, gate_ref, h_ref = refs[:5]
        w_v, sem = refs[-2:]
        i = pl.program_id(0)

        @pl.when(i == 0)
        def _():
            cp = pltpu.make_async_copy(w_hbm, w_v, sem)
            cp.start()
            cp.wait()

        y = None
        for p in range(NCHIP):
            cols = slice(p * kb, (p + 1) * kb)
            zz = z_ref[:, cols].astype(F32)
            act = (a_ref[:, cols].astype(F32) * (zz * jax.nn.sigmoid(zz))).astype(BF16)
            part = _nn(act, w_v[p])
            y = part if y is None else y + part
        refs[n_in][...] = y.astype(BF16)
        hh = h_ref[...] + gate_ref[...] * y
        if head is None:
            refs[n_in + 1][...] = hh
            return
        g_ref, t_ref = refs[5:7]
        dh_ref, st_ref = refs[n_in + 1:n_in + 3]

        @pl.when(i == 0)
        def _():
            st_ref[...] = jnp.zeros((SUBLANES, D), F32)

        r = lax.rsqrt(jnp.mean(hh * hh, axis=-1, keepdims=True) + EPS)
        xhat = hh * r
        diff = xhat * g_ref[...] - t_ref[...]
        st_ref[1:2, :] += _colsum(diff * diff)
        dout = diff * (1.0 / D)
        st_ref[0:1, :] += _colsum(dout * xhat)
        dx = dout * g_ref[...]
        dh_ref[...] = r * (dx - xhat * jnp.mean(dx * xhat, axis=-1, keepdims=True))

    rows_d = pl.BlockSpec((tm, D), lambda i: (i, 0))
    rows_e = pl.BlockSpec((tm, E), lambda i: (i, 0))
    in_specs = [rows_e, rows_e, ANY, _row(D), rows_d]
    out_specs = [rows_d, rows_d]
    out_shape = [jax.ShapeDtypeStruct((S, D), BF16), jax.ShapeDtypeStruct((S, D), F32)]
    args = (a, z, w, gate, h)
    if head is not None:
        in_specs += [_row(D), rows_d]
        out_specs += [pl.BlockSpec((SUBLANES, D), lambda i: (0, 0))]
        out_shape += [jax.ShapeDtypeStruct((SUBLANES, D), F32)]
        args += tuple(head)
    return _call(body, name, (S // tm,), in_specs, out_specs, out_shape,
                 [pltpu.VMEM((NCHIP, kb, D), BF16), pltpu.SemaphoreType.DMA], _params(1, 52), args, comm)


TW = BW + LANES


def _diag_onehot(transpose):
    shape = (TW, NRELP) if transpose else (NRELP, TW)
    j = lax.broadcasted_iota(jnp.int32, shape, 0 if transpose else 1)
    r = lax.broadcasted_iota(jnp.int32, shape, 1 if transpose else 0)
    idx = jnp.clip(PAD - (j - LANES), -REL_CLIP, REL_CLIP) + REL_CLIP
    return jnp.where(idx == r, 1.0, 0.0).astype(BF16)


def _strip_valid():
    m = lax.broadcasted_iota(jnp.int32, (NH, BW), 1)
    return m < (LEFT + 1) * CHUNK, m >= CHUNK


def _bias_build(rb, name):
    def body(rb_ref, a_ref, b_ref):
        x = rb_ref[...]
        hi = x.astype(BF16)
        r1 = x - hi.astype(F32)
        mid = r1.astype(BF16)
        lo = (r1 - mid.astype(F32)).astype(BF16)
        oh = _diag_onehot(False)
        diag = (_nn(hi, oh) + _nn(mid, oh)) + _nn(lo, oh)
        valid_a, valid_b = _strip_valid()
        for qi in range(CHUNK):
            a_ref[qi] = jnp.where(valid_a, pltpu.roll(diag, TW - (LANES - qi), 1)[:, :BW], NEG)
            b_ref[qi] = jnp.where(valid_b, pltpu.roll(diag, TW - (CHUNK - qi), 1)[:, :BW], NEG)

    vmem = pl.BlockSpec(memory_space=pltpu.VMEM)
    return pl.pallas_call(
        body, name=name, in_specs=[vmem], out_specs=[vmem, vmem],
        out_shape=[jax.ShapeDtypeStruct((CHUNK, NH, BW), F32), jax.ShapeDtypeStruct((CHUNK, NH, BW), F32)],
        compiler_params=pltpu.CompilerParams(vmem_limit_bytes=32 * 2 ** 20),
    )(rb)


def _dbias_reduce(dba, dbb, name):
    def body(a_ref, b_ref, o_ref):
        valid_a, valid_b = _strip_valid()
        zeros = jnp.zeros((NH, TW - BW), F32)
        acc = jnp.zeros((NH, TW), F32)
        for qi in range(CHUNK):
            xa = jnp.concatenate([jnp.where(valid_a, a_ref[qi], 0.0), zeros], axis=1)
            xb = jnp.concatenate([jnp.where(valid_b, b_ref[qi], 0.0), zeros], axis=1)
            acc = acc + (pltpu.roll(xa, LANES - qi, 1) + pltpu.roll(xb, CHUNK - qi, 1))
        oh = _diag_onehot(True)
        hi = acc.astype(BF16)
        mid = (acc - hi.astype(F32)).astype(BF16)
        r = lax.broadcasted_iota(jnp.int32, (NH, NRELP), 1)
        near = jnp.where(r < 2 * REL_CLIP, _nn(hi, oh) + _nn(mid, oh), 0.0)
        o_ref[...] = jnp.where(r == 2 * REL_CLIP, -jnp.sum(near, axis=-1, keepdims=True), near)

    vmem = pl.BlockSpec(memory_space=pltpu.VMEM)
    return pl.pallas_call(
        body, name=name, in_specs=[vmem, vmem], out_specs=vmem,
        out_shape=jax.ShapeDtypeStruct((NH, NRELP), F32),
        compiler_params=pltpu.CompilerParams(vmem_limit_bytes=32 * 2 ** 20),
    )(dba, dbb)


def _build_bias(bias3, ba_ref, bb_ref):
    bias3[NMASK] = jnp.full((QB, WIN), NEG, F32)
    for qc in range(QC):
        rows = slice(qc * CHUNK, (qc + 1) * CHUNK)
        if qc % 2 == 0:
            bias3[NMASK, rows, qc * CHUNK:qc * CHUNK + BW] = ba_ref[...] * LOG2E
        else:
            bias3[NMASK, rows, (qc - 1) * CHUNK:(qc - 1) * CHUNK + BW] = bb_ref[...] * LOG2E
    col = lax.broadcasted_iota(jnp.int32, (QB, WIN), 1)
    for sub in range(NMASK):
        bias3[sub] = jnp.where(col < PAD - sub * QB, NEG, bias3[NMASK])


def _nsub(S):
    n = min(NSUB, S // QB)
    assert S % (n * QB) == 0 and n >= NMASK
    return n


def _row0(i, sub, nsub):
    return pl.multiple_of((i * nsub + sub) * QB, QB)


def _scores(q_ref, k_ref, i, sub, nsub):
    return _nt(q_ref[sub * QB:(sub + 1) * QB, :], k_ref[pl.ds(_row0(i, sub, nsub), WIN), :])


def _exp_parts(s, bias3, i, sub):
    which = jnp.where(i == 0, sub, NMASK) if sub < NMASK else NMASK
    s = s * (SM_SCALE * LOG2E) + bias3[which]
    e = jnp.exp2(s - jnp.max(s, axis=-1, keepdims=True))
    return e, jnp.sum(e, axis=-1, keepdims=True)


def _attn_fwd(q, kp, vp, ba, bb, name, comm=None):
    S = q.shape[0]
    nsub = _nsub(S)
    R = nsub * QB

    def body(q_ref, k_ref, v_ref, ba_ref, bb_ref, o_ref, p_ref, bias3):
        i = pl.program_id(1)

        @pl.when(i == 0)
        def _():
            _build_bias(bias3, ba_ref, bb_ref)

        s_next = _scores(q_ref, k_ref, i, 0, nsub)
        for sub in range(nsub):
            s = s_next
            if sub + 1 < nsub:
                s_next = _scores(q_ref, k_ref, i, sub + 1, nsub)
            e, l = _exp_parts(s, bias3, i, sub)
            pb = (e * (1.0 / l)).astype(BF16)
            p_ref[sub] = pb
            o_ref[sub * QB:(sub + 1) * QB, :] = _nn(pb, v_ref[pl.ds(_row0(i, sub, nsub), WIN), :]).astype(BF16)

    return _call(
        body, name, (NH, S // R),
        [pl.BlockSpec((R, HD), lambda h, i: (i, h)), pl.BlockSpec((S + PAD, HD), lambda h, i: (0, h)),
         pl.BlockSpec((S + PAD, HD), lambda h, i: (0, h)), pl.BlockSpec((None, CHUNK, BW), lambda h, i: (h, 0, 0)),
         pl.BlockSpec((None, CHUNK, BW), lambda h, i: (h, 0, 0))],
        [pl.BlockSpec((R, HD), lambda h, i: (i, h)), pl.BlockSpec((None, nsub, QB, WIN), lambda h, i: (h, i, 0, 0))],
        [jax.ShapeDtypeStruct((S, E), BF16), jax.ShapeDtypeStruct((NH, S // QB, QB, WIN), BF16)],
        [pltpu.VMEM((NMASK + 1, QB, WIN), F32)],
        _params(2, 48), (q, kp, vp, ba, bb), comm)


def _store_grad(acc, stage, dw_hbm, sem):
    for q in range(NCHIP):
        stage[...] = acc[q].astype(BF16)
        cp = pltpu.make_async_copy(stage, dw_hbm.at[q], sem)
        cp.start()
        cp.wait()


def _out_bwd(dh, y, gate, a, cs, z, w, name, comm=None, tm=256):
    S = dh.shape[0]
    kb = E // NCHIP
    cb = 256
    n_t = S // tm

    def body(dh_ref, y_ref, gate_ref, a_ref, cs_ref, z_ref, w_hbm, da_ref, dz_ref, dw_hbm, st_ref, w_v, acc, stage, sem):
        i = pl.program_id(0)

        @pl.when(i == 0)
        def _():
            cp = pltpu.make_async_copy(w_hbm, w_v, sem)
            cp.start()
            acc[...] = jnp.zeros(acc.shape, F32)
            st_ref[...] = jnp.zeros((SUBLANES, D), F32)
            cp.wait()

        dhh = dh_ref[...]
        st_ref[0:1, :] += _colsum(dhh * y_ref[...].astype(F32))
        dy = (dhh * gate_ref[...]).astype(BF16)
        for blk in range(E // cb):
            p, r0 = divmod(blk * cb, kb)
            cols = slice(blk * cb, (blk + 1) * cb)
            zz = z_ref[:, cols].astype(F32)
            sig = jax.nn.sigmoid(zz)
            sz = zz * sig
            ae = a_ref[:, cols].astype(F32) * cs_ref[:, cols]
            acc[p, r0:r0 + cb, :] += _tn((ae * sz).astype(BF16), dy)
            dact = _nt(dy, w_v[p, r0:r0 + cb, :])
            da_ref[:, cols] = (dact * sz).astype(BF16)
            dz_ref[:, cols] = (dact * ae * (sig * (1.0 + zz * (1.0 - sig)))).astype(BF16)

        @pl.when(i == n_t - 1)
        def _():
            _store_grad(acc, stage, dw_hbm, sem)

    return _call(
        body, name, (n_t,),
        [pl.BlockSpec((tm, D), lambda i: (i, 0)), pl.BlockSpec((tm, D), lambda i: (i, 0)), _row(D),
         pl.BlockSpec((tm, E), lambda i: (i, 0)), _row(E), pl.BlockSpec((tm, E), lambda i: (i, 0)), ANY],
        [pl.BlockSpec((tm, E), lambda i: (i, 0)), pl.BlockSpec((tm, E), lambda i: (i, 0)), ANY,
         pl.BlockSpec((SUBLANES, D), lambda i: (0, 0))],
        [jax.ShapeDtypeStruct((S, E), BF16), jax.ShapeDtypeStruct((S, E), BF16),
         jax.ShapeDtypeStruct((NCHIP, kb, D), BF16), jax.ShapeDtypeStruct((SUBLANES, D), F32)],
        [pltpu.VMEM((NCHIP, kb, D), BF16), pltpu.VMEM((NCHIP, kb, D), F32), pltpu.VMEM((kb, D), BF16),
         pltpu.SemaphoreType.DMA],
        _params(1, 52), (dh, y, gate, a, cs, z, w), comm)


def _attn_bwd(q, kp, vp, probs, do, prev, name, comm=None):
    S = q.shape[0]
    nsub = _nsub(S)
    R = nsub * QB
    n_i = S // R
    dt_kv = F32 if prev is None else BF16

    def body(*refs):
        q_ref, k_ref, v_ref, p_ref, do_ref = refs[:5]
        refs = refs[5:]
        if prev is not None:
            pk_hbm, pv_hbm = refs[:2]
            refs = refs[2:]
        dq_ref, dk_ref, dv_ref, dba_ref, dbb_ref, dbias, dk_acc, dv_acc = refs[:8]
        if prev is not None:
            pk_v, pv_v, sems = refs[8:]
        h = pl.program_id(0)
        i = pl.program_id(1)

        def prev_copies():
            cols = pl.ds(pl.multiple_of(h * HD, HD), HD)
            return (pltpu.make_async_copy(pk_hbm.at[:, cols], pk_v, sems.at[0]),
                    pltpu.make_async_copy(pv_hbm.at[:, cols], pv_v, sems.at[1]))

        @pl.when(i == 0)
        def _():
            if prev is not None:
                for cp in prev_copies():
                    cp.start()
            dbias[...] = jnp.zeros((2, CHUNK, DBW), F32)
            dk_acc[...] = jnp.zeros((S + PAD, HD), F32)
            dv_acc[...] = jnp.zeros((S + PAD, HD), F32)

        def mxu_in(sub):
            return _nt(do_ref[sub * QB:(sub + 1) * QB, :], v_ref[pl.ds(_row0(i, sub, nsub), WIN), :])

        nxt = mxu_in(0)
        for sub in range(nsub):
            rows = slice(sub * QB, (sub + 1) * QB)
            win = pl.ds(_row0(i, sub, nsub), WIN)
            dp = nxt
            if sub + 1 < nsub:
                nxt = mxu_in(sub + 1)
            pb = p_ref[sub]
            p = pb.astype(F32)
            ds = p * (dp - jnp.sum(p * dp, axis=-1, keepdims=True))
            for par in range(2):
                part = None
                for qc in range(par, QC, 2):
                    c0 = (qc - par) * CHUNK + BW - DBW
                    blk_ = ds[qc * CHUNK:(qc + 1) * CHUNK, c0:c0 + DBW]
                    part = blk_ if part is None else part + blk_
                dbias[par] += part
            dsb = (ds * SM_SCALE).astype(BF16)
            dq_ref[rows, :] = _nn(dsb, k_ref[win, :]).astype(BF16)
            dk_acc[win, :] += _tn(dsb, q_ref[rows, :])
            dv_acc[win, :] += _tn(pb, do_ref[rows, :])

        @pl.when(i == n_i - 1)
        def _():
            zeros = jnp.zeros((CHUNK, BW - DBW), F32)
            dba_ref[...] = jnp.concatenate([zeros, dbias[0]], axis=1)
            dbb_ref[...] = jnp.concatenate([zeros, dbias[1]], axis=1)
            if prev is None:
                dk_ref[...] = dk_acc[...]
                dv_ref[...] = dv_acc[...]
            else:
                for cp in prev_copies():
                    cp.wait()
                dk_ref[...] = (dk_acc[...] + pk_v[...]).astype(BF16)
                dv_ref[...] = (dv_acc[...] + pv_v[...]).astype(BF16)

    head = pl.BlockSpec((S + PAD, HD), lambda h, i: (0, h))
    strip = pl.BlockSpec((None, CHUNK, BW), lambda h, i: (h, 0, 0))
    blk = pl.BlockSpec((R, HD), lambda h, i: (i, h))
    in_specs = [blk, head, head, pl.BlockSpec((None, nsub, QB, WIN), lambda h, i: (h, i, 0, 0)), blk]
    scratch = [pltpu.VMEM((2, CHUNK, DBW), F32), pltpu.VMEM((S + PAD, HD), F32), pltpu.VMEM((S + PAD, HD), F32)]
    args = (q, kp, vp, probs, do)
    if prev is not None:
        in_specs += [ANY, ANY]
        scratch += [pltpu.VMEM((S + PAD, HD), F32), pltpu.VMEM((S + PAD, HD), F32), pltpu.SemaphoreType.DMA((2,))]
        args += tuple(prev)
    return _call(
        body, name, (NH, n_i), in_specs, [blk, head, head, strip, strip],
        [jax.ShapeDtypeStruct((S, E), BF16), jax.ShapeDtypeStruct((S + PAD, E), dt_kv),
         jax.ShapeDtypeStruct((S + PAD, E), dt_kv), jax.ShapeDtypeStruct((NH, CHUNK, BW), F32),
         jax.ShapeDtypeStruct((NH, CHUNK, BW), F32)],
        scratch, _params(2, 56), args, comm)


def _pool_bwd(dms, mixed, pooled, wg, a_scale, name, comm=None, tm=512):
    S = dms.shape[0]
    n_t = S // tm

    def rev(i):
        return (n_t - 1 - i, 0)

    def body(d_ref, m_ref, p_ref, wg_ref, as_ref, dv_ref, dwg_ref, st_ref, buf):
        i = pl.program_id(0)

        @pl.when(i == 0)
        def _():
            buf[tm:tm + HALO, :] = jnp.zeros((HALO, E), F32)
            dwg_ref[...] = jnp.zeros((4, GW, GW), F32)
            st_ref[...] = jnp.zeros((SUBLANES, E), F32)

        t = (n_t - 1 - i) * tm + lax.broadcasted_iota(jnp.int32, (tm, 1), 0)
        st_ref[0:1, :] += _colsum(d_ref[...].astype(F32) * m_ref[...].astype(F32))
        for gi, w in enumerate(POOL_W):
            cols = slice(gi * GW, (gi + 1) * GW)
            dm = (d_ref[:, cols].astype(F32) * as_ref[:, cols]).astype(BF16)
            dpool = _nt(dm, wg_ref[gi])
            dwg_ref[gi] += _tn(p_ref[:, cols], dm)
            inv_cnt = 1.0 / jnp.minimum(t + 1, w).astype(F32)
            buf[0:tm, cols] = dpool * inv_cnt
            s = buf[:, cols]
            k = 1
            while k < w:
                s = s + pltpu.roll(s, tm + HALO - k, 0)
                k *= 2
            dv_ref[:, cols] = (s[0:tm, :] - dpool).astype(BF16)
        buf[tm:tm + HALO, :] = buf[0:HALO, :]

    return _call(
        body, name, (n_t,),
        [pl.BlockSpec((tm, E), rev), pl.BlockSpec((tm, E), rev), pl.BlockSpec((tm, E), rev),
         pl.BlockSpec((4, GW, GW), lambda i: (0, 0, 0)), _row(E)],
        [pl.BlockSpec((tm, E), rev), pl.BlockSpec((4, GW, GW), lambda i: (0, 0, 0)),
         pl.BlockSpec((SUBLANES, E), lambda i: (0, 0))],
        [jax.ShapeDtypeStruct((S, E), BF16), jax.ShapeDtypeStruct((4, GW, GW), F32),
         jax.ShapeDtypeStruct((SUBLANES, E), F32)],
        [pltpu.VMEM((tm + HALO, E), F32)],
        _params(1, 52), (dms, mixed, pooled, wg, a_scale), comm)


def _in_bwd(da, db, row_off, u, h, g, scale, w, dh_out, name, comm=None, tm=256):
    S = h.shape[0]
    n_t = S // tm
    off = row_off // tm

    def body(da_ref, db_ref, u_ref, h_ref, g_ref, sc_ref, w_hbm, dho_ref, dhi_ref, dw_hbm, st_ref, w_v, acc, stage, sem):
        i = pl.program_id(0)

        @pl.when(i == 0)
        def _():
            cp = pltpu.make_async_copy(w_hbm, w_v, sem)
            cp.start()
            acc[...] = jnp.zeros(acc.shape, F32)
            st_ref[...] = jnp.zeros((SUBLANES, D), F32)
            cp.wait()

        ub = u_ref[...]
        du = None
        for q in range(NCHIP):
            d_ref = da_ref if q < 2 else db_ref
            dv = d_ref[:, (q % 2) * D:(q % 2 + 1) * D]
            acc[q] += _tn(ub, dv)
            part = _nt(dv, w_v[q])
            du = part if du is None else du + part

        hh = h_ref[...]
        r = lax.rsqrt(jnp.mean(hh * hh, axis=-1, keepdims=True) + EPS)
        xhat = hh * r
        gg = g_ref[...]
        st_ref[0:1, :] += _colsum(du)
        st_ref[1:2, :] += _colsum(du * (xhat * gg))
        dn = du * (1.0 + sc_ref[...])
        st_ref[2:3, :] += _colsum(dn * xhat)
        dx = dn * gg
        dhi_ref[...] = dho_ref[...] + r * (dx - xhat * jnp.mean(dx * xhat, axis=-1, keepdims=True))

        @pl.when(i == n_t - 1)
        def _():
            _store_grad(acc, stage, dw_hbm, sem)

    part_spec = pl.BlockSpec((tm, E), lambda i: (i + off, 0))
    return _call(
        body, name, (n_t,),
        [part_spec, part_spec, pl.BlockSpec((tm, D), lambda i: (i, 0)), pl.BlockSpec((tm, D), lambda i: (i, 0)),
         _row(D), _row(D), ANY, pl.BlockSpec((tm, D), lambda i: (i, 0))],
        [pl.BlockSpec((tm, D), lambda i: (i, 0)), ANY, pl.BlockSpec((SUBLANES, D), lambda i: (0, 0))],
        [jax.ShapeDtypeStruct((S, D), F32), jax.ShapeDtypeStruct((NCHIP, D, D), BF16),
         jax.ShapeDtypeStruct((SUBLANES, D), F32)],
        [pltpu.VMEM((NCHIP, D, D), BF16), pltpu.VMEM((NCHIP, D, D), F32), pltpu.VMEM((D, D), BF16),
         pltpu.SemaphoreType.DMA],
        _params(1, 56), (da, db, u, h, g, scale, w, dh_out), comm)


def _cmat(c_all, w, b, name):
    L, _, n = w.shape

    def body(c_ref, w_ref, b_ref, ca_ref, o_ref):
        cc = c_ref[...]
        ca = cc * jax.nn.sigmoid(cc)
        ca_ref[...] = ca
        o_ref[...] = _nn(ca.astype(BF16), w_ref[...].astype(BF16)) + b_ref[...]

    return pl.pallas_call(
        body, name=name, grid=(L,),
        in_specs=[pl.BlockSpec((SUBLANES, D), lambda l: (0, 0)), pl.BlockSpec((None, D, n), lambda l: (l, 0, 0)),
                  pl.BlockSpec((None, 1, n), lambda l: (l, 0, 0))],
        out_specs=[pl.BlockSpec((SUBLANES, D), lambda l: (0, 0)), pl.BlockSpec((None, SUBLANES, n), lambda l: (l, 0, 0))],
        out_shape=[jax.ShapeDtypeStruct((SUBLANES, D), F32), jax.ShapeDtypeStruct((L, SUBLANES, n), F32)],
        compiler_params=_params(1, 32),
    )(c_all, w, b)


def _grad_ada(c_act_t, dmod, name):
    L, _, n = dmod.shape

    def body(c_ref, d_ref, o_ref):
        acc = None
        for b in range(SUBLANES):
            part = c_ref[:, b:b + 1] * d_ref[b:b + 1, :]
            acc = part if acc is None else acc + part
        o_ref[...] = acc

    return pl.pallas_call(
        body, name=name, grid=(L,),
        in_specs=[pl.BlockSpec((D, SUBLANES), lambda l: (0, 0)), pl.BlockSpec((None, SUBLANES, n), lambda l: (l, 0, 0))],
        out_specs=pl.BlockSpec((None, D, n), lambda l: (l, 0, 0)),
        out_shape=jax.ShapeDtypeStruct((L, D, n), F32),
        compiler_params=_params(1, 32),
    )(c_act_t, dmod)


def _stats_reduce(g3, loss_row, name):
    n_dev, rows, _ = g3.shape

    def body(g_ref, o_ref, l_ref):
        acc = g_ref[0]
        for d in range(1, n_dev):
            acc = acc + g_ref[d]
        o_ref[...] = acc
        tot = jnp.sum(o_ref[loss_row:loss_row + 1, :], axis=-1, keepdims=True)
        l_ref[...] = jnp.broadcast_to(tot * (0.5 / D), (SUBLANES, LANES))

    return pl.pallas_call(
        body, name=name,
        in_specs=[pl.BlockSpec(memory_space=pltpu.VMEM)],
        out_specs=[pl.BlockSpec(memory_space=pltpu.VMEM), pl.BlockSpec(memory_space=pltpu.VMEM)],
        out_shape=[jax.ShapeDtypeStruct((rows, D), F32), jax.ShapeDtypeStruct((SUBLANES, LANES), F32)],
        compiler_params=pltpu.CompilerParams(vmem_limit_bytes=32 * 2 ** 20),
    )(g3)


def _sum4(own, land, chip, name, tr=256):
    _, R, C = own.shape
    tr = min(tr, R)

    def body(p_ref, own_ref, land_ref, o_ref):
        o_ref[...] = ((own_ref[...].astype(F32) + land_ref[0].astype(F32)) + land_ref[1].astype(F32)) + land_ref[2].astype(F32)

    return pl.pallas_call(
        body, name=name,
        grid_spec=pltpu.PrefetchScalarGridSpec(
            num_scalar_prefetch=1, grid=(R // tr,),
            in_specs=[pl.BlockSpec((None, tr, C), lambda i, p: (p[0], i, 0)), pl.BlockSpec((3, tr, C), lambda i, p: (0, i, 0))],
            out_specs=pl.BlockSpec((tr, C), lambda i, p: (i, 0))),
        out_shape=jax.ShapeDtypeStruct((R, C), F32),
        compiler_params=_params(1, 32),
    )(chip, own, land)


def _adamw(w, m, v, g, name, tr=256):
    L, R, C = w.shape
    tr = min(tr, R)
    stacked = not isinstance(g, (list, tuple))
    n_g = None if stacked else [len(ps) for ps in g]
    flat = [g] if stacked else [a for ps in g for a in ps]

    def body(*refs):
        w_ref, m_ref, v_ref = refs[:3]
        g_refs = refs[3:3 + len(flat)]
        go_ref, d_ref, mo_ref, vo_ref = refs[3 + len(flat):]
        if stacked:
            gg = g_refs[0][...]
        else:
            layer = pl.program_id(0)
            gg = None
            k = 0
            for li in range(L):
                gl = None
                for _ in range(n_g[li]):
                    x = g_refs[k][...]
                    gl = x if gl is None else gl + x
                    k += 1
                gg = gl if gg is None else jnp.where(layer == li, gl, gg)
        m2 = ADAM_B1 * m_ref[...] + (1.0 - ADAM_B1) * gg
        v2 = ADAM_B2 * v_ref[...] + (1.0 - ADAM_B2) * (gg * gg)
        m_hat = m2 / (1.0 - ADAM_B1 ** ADAM_STEP)
        v_hat = v2 / (1.0 - ADAM_B2 ** ADAM_STEP)
        go_ref[...] = gg
        d_ref[...] = -ADAM_LR * (m_hat / (jnp.sqrt(v_hat) + ADAM_EPS) + ADAM_WD * w_ref[...])
        mo_ref[...] = m2
        vo_ref[...] = v2

    big = pl.BlockSpec((None, tr, C), lambda l, i: (l, i, 0))
    g_specs = [big] if stacked else [pl.BlockSpec((tr, C), lambda l, i: (i, 0))] * len(flat)
    return pl.pallas_call(
        body, name=name, grid=(L, R // tr),
        in_specs=[big, big, big] + g_specs,
        out_specs=[big, big, big, big],
        out_shape=[jax.ShapeDtypeStruct((L, R, C), F32)] * 4,
        compiler_params=_params(2, 48),
    )(w, m, v, *flat)


def _allgather8(xs, name, comm=None):
    m, n = xs.shape
    n_c = 0 if comm is None else comm.n

    def body(*refs):
        x_ref, out_ref = refs[0], refs[1 + n_c]
        send_sems, recv_sems, local_sem = refs[2 + 2 * n_c:5 + 2 * n_c]
        c_refs = (refs[1:1 + n_c], refs[2 + n_c:2 + 2 * n_c]) + tuple(refs[5 + 2 * n_c:])
        if comm is not None:
            comm.start(*c_refs)
        x, y, c = _place()
        me, sibling = (x, y, c), (x, y, 1 - c)
        chips = [(1 - x, y), (x, 1 - y), (1 - x, 1 - y)]

        def rows(px, py, pc):
            return out_ref.at[pl.ds((4 * px + 2 * py + pc) * m, m), :]

        def copy(k, block, to, src=None):
            return pltpu.make_async_remote_copy(
                src_ref=rows(*block) if src is None else src, dst_ref=rows(*block),
                send_sem=send_sems.at[k], recv_sem=recv_sems.at[k], device_id=to, device_id_type=MESH)

        mine = pltpu.make_async_copy(x_ref, rows(*me), local_sem)
        mine.start()
        first = [copy(0, me, sibling, src=x_ref)]
        first += [copy(1 + j, me, (*chip, c), src=x_ref) for j, chip in enumerate(chips)]
        for cp in first:
            cp.start()
        passed = [copy(4 + j, (*chip, c), sibling) for j, chip in enumerate(chips)]
        for j, chip in enumerate(chips):
            copy(1 + j, (*chip, c), me).wait_recv()
            passed[j].start()
        copy(0, sibling, me).wait_recv()
        for j, chip in enumerate(chips):
            copy(4 + j, (*chip, 1 - c), me).wait_recv()
        for cp in first + passed:
            cp.wait_send()
        mine.wait()
        if comm is not None:
            comm.wait(*c_refs)

    vmem = pl.BlockSpec(memory_space=pltpu.VMEM)
    outs = pl.pallas_call(
        body, name=name,
        out_shape=[jax.ShapeDtypeStruct((8 * m, n), xs.dtype)] + ([] if comm is None else comm.out_shape),
        in_specs=[vmem] + [ANY] * n_c,
        out_specs=[vmem] + [ANY] * n_c,
        scratch_shapes=[pltpu.SemaphoreType.DMA((7,)), pltpu.SemaphoreType.DMA((7,)), pltpu.SemaphoreType.DMA]
        + ([] if comm is None else comm.scratch),
        compiler_params=pltpu.CompilerParams(vmem_limit_bytes=32 * 2 ** 20),
    )(xs, *([] if comm is None else comm.arrays))
    return outs[0], list(outs[1:])


def _pad8(a):
    return jnp.pad(a, ((0, SUBLANES - a.shape[0]), (0, 0)))


def _group_rows(wg):
    return wg.transpose(1, 0, 2, 3).reshape(4, GW, GW)


def _example_step(h0, tgt, mods, kvmod, a_scale, norm_g, kv_norm_g, final_g, b_rel_bias, sh, w_first, chip_arr):
    ones_e = jnp.ones((1, E), F32)
    shift = [mods[l:l + 1, 0:D] for l in range(4)]
    scale = [mods[l:l + 1, D:2 * D] for l in range(4)]
    gate = [mods[l:l + 1, 2 * D:3 * D] for l in range(4)]
    gl = [norm_g[l:l + 1] for l in range(4)]
    kv_shift, kv_scale = kvmod[None, 0:D], kvmod[None, D:2 * D]
    kv_g = kv_norm_g[None]

    w_a = w_first
    hs = [h0]
    saved = []
    nxt = [[sh["a_in"][1], sh["a_grp"][1], sh["a_out"][1], sh["kv"][0]],
           [sh["b_in"][0], sh["b_out"][0], sh["b_in"][1], sh["b_out"][1]]]
    for l in range(2):
        w_in_l, wg_l, wo_l = w_a
        wg_full = _group_rows(wg_l)
        (u, z, pooled, mixed, y, hn), got = _a_fwd(hs[-1], gl[l], shift[l], scale[l], a_scale[l:l + 1], gate[l], w_in_l,
                                                   wg_full, wo_l, f"a{l}_fwd", comm=_Comm(gathers=nxt[l]))
        saved.append((u, z, pooled, mixed, y, w_in_l, wg_full, wo_l))
        hs.append(hn)
        if l == 0:
            w_a, w_kv = got[:3], got[3]
        else:
            wb_in = [got[0], got[2]]
            wb_out = [got[1], got[3]]

    (uk, kp, vp), _ = _in_fwd(hs[2], kv_g, kv_shift, kv_scale, w_kv, BF16, BF16, "kv_in_fwd", pad_rows=PAD)

    for bi in range(2):
        l = 2 + bi
        sa, sb = _bias_build(jnp.pad(b_rel_bias[bi], ((0, 0), (0, NRELP - NREL))), f"b{bi}_bias")
        (u, q, z), _ = _in_fwd(hs[-1], gl[l], shift[l], scale[l], wb_in[bi], BF16, BF16, f"b{bi}_in_fwd")
        (att, probs), _ = _attn_fwd(q, kp, vp, sa.transpose(1, 0, 2), sb.transpose(1, 0, 2), f"b{bi}_attn_fwd")
        if bi == 0:
            (y, hn), _ = _out_fwd(att, z, wb_out[bi], gate[l], hs[-1], f"b{bi}_out_fwd")
            hs.append(hn)
        else:
            (y, dh, st_fin), _ = _out_fwd(att, z, wb_out[bi], gate[l], hs[-1], f"b{bi}_out_fwd", head=(final_g[None], tgt))
        saved.append((u, z, q, att, y, probs))

    st_in = [None] * 4
    st_out = [None] * 4
    grads = {}
    landed = {}

    def carry(names):
        return _Comm(scatters=[grads[n] for n in names]) if names else None

    def land(names, got):
        for n, a in zip(names, got):
            landed[n] = a

    u, z, q, att, y, probs = saved[3]
    (datt, dz, grads["b_out1"], st_out[3]), _ = _out_bwd(dh, y, gate[3], att, ones_e, z, wb_out[1], "b1_out_bwd")
    (dq, dk1, dv1, dsa, dsb), _ = _attn_bwd(q, kp, vp, probs, datt, None, "b1_attn_bwd")
    drb1 = _dbias_reduce(dsa.transpose(1, 0, 2), dsb.transpose(1, 0, 2), "b1_dbias")
    (dh, grads["b_in1"], st_in[3]), _ = _in_bwd(dq, dz, 0, u, hs[3], gl[3], scale[3], wb_in[1], dh, "b1_in_bwd")
    u, z, q, att, y, probs = saved[2]
    (datt, dz, grads["b_out0"], st_out[2]), _ = _out_bwd(dh, y, gate[2], att, ones_e, z, wb_out[0], "b0_out_bwd")
    (dq, dk, dv, dsa, dsb), got = _attn_bwd(q, kp, vp, probs, datt, (dk1, dv1), "b0_attn_bwd",
                                            comm=carry(["b_out1", "b_in1", "b_out0"]))
    land(["b_out1", "b_in1", "b_out0"], got)
    drb0 = _dbias_reduce(dsa.transpose(1, 0, 2), dsb.transpose(1, 0, 2), "b0_dbias")
    (dh, grads["b_in0"], st_in[2]), _ = _in_bwd(dq, dz, 0, u, hs[2], gl[2], scale[2], wb_in[0], dh, "b0_in_bwd")
    (dh, grads["kv"], st_kv), got = _in_bwd(dk, dv, PAD, uk, hs[2], kv_g, kv_scale, w_kv, dh, "kv_in_bwd",
                                            comm=carry(["b_in0"]))
    land(["b_in0"], got)
    st_pool = [None] * 2
    plan = {1: dict(o=[], p=[], i=["kv", "a_out1", "a_grp1"]), 0: dict(o=["a_in1"], p=["a_out0"], i=[])}
    early = ["b_out1", "b_in1", "b_out0", "b_in0", "kv", "a_out1", "a_grp1", "a_in1"]
    late = ["a_out0", "a_grp0", "a_in0"]
    both = {}

    def sum4(n):
        return _sum4(grads[n], landed[n], chip_arr, f"sum4_{n}")

    for l in (1, 0):
        u, z, pooled, mixed, y, w_in_l, wg_full, wo = saved[l]
        asl = a_scale[l:l + 1]
        (dms, dz, grads[f"a_out{l}"], st_out[l]), got = _out_bwd(dh, y, gate[l], mixed, asl, z, wo, f"a{l}_out_bwd",
                                                                comm=carry(plan[l]["o"]))
        land(plan[l]["o"], got)
        comm = carry(plan[l]["p"])
        if l == 0:
            mine = [sum4(n) for n in early]
            comm = _Comm(scatters=[grads[n] for n in plan[l]["p"]], swaps=mine)
        (dval, dwg, st_pool[l]), got = _pool_bwd(dms, mixed, pooled, wg_full, asl, f"a{l}_pool_bwd", comm=comm)
        land(plan[l]["p"], got)
        if l == 0:
            both.update({n: [a, b] for n, a, b in zip(early, mine, got[len(plan[l]["p"]):])})
        grads[f"a_grp{l}"] = (dwg.reshape(4, NCHIP, GW // NCHIP, GW).transpose(1, 0, 2, 3).reshape(NCHIP, GW, GW)
                              .astype(BF16))
        (dh, grads[f"a_in{l}"], st_in[l]), got = _in_bwd(dval, dz, 0, u, hs[l], gl[l], scale[l], w_in_l, dh, f"a{l}_in_bwd",
                                                         comm=carry(plan[l]["i"]))
        land(plan[l]["i"], got)
    pieces = st_in + [st_kv] + st_out + [st_fin]
    pieces += [_pad8(st_pool[l][0].reshape(2, D)) for l in range(2)]
    pieces += [_pad8(d.reshape(NH * NRELP // D, D)) for d in (drb0, drb1)]
    gathered, got = _allgather8(jnp.concatenate(pieces, axis=0), "gather_stats", comm=carry(["a_grp0", "a_in0"]))
    land(["a_grp0", "a_in0"], got)
    mine = [sum4(n) for n in late]
    both.update({n: [a, b] for n, a, b in zip(late, mine, _comm_only(_Comm(swaps=mine), "swap_last"))})
    return dh, both, gathered.reshape(8, N_STAT, D)


ROW_IN = [8 * l for l in range(4)]
ROW_KV = 32
ROW_OUT = [40 + 8 * l for l in range(4)]
ROW_FIN = 72
ROW_ASC = [80, 88]
ROW_RB = [96, 104]
N_STAT = 112


def kernel(x, c, ada_w, ada_b, norm_g, a_w_in, a_w_group, a_scale, a_w_out, kv_norm_g, kv_ada_w, kv_ada_b, w_kv, b_w_in, b_rel_bias, b_w_out, final_g, loss_target, m_ada_w, m_ada_b, m_norm_g, m_a_w_in, m_a_w_group, m_a_scale, m_a_w_out, m_kv_norm_g, m_kv_ada_w, m_kv_ada_b, m_w_kv, m_b_w_in, m_b_rel_bias, m_b_w_out, m_final_g, v_ada_w, v_ada_b, v_norm_g, v_a_w_in, v_a_w_group, v_a_scale, v_a_w_out, v_kv_norm_g, v_kv_ada_w, v_kv_ada_b, v_w_kv, v_b_w_in, v_b_rel_bias, v_b_w_out, v_final_g):
    xi, yi, ci = _place()
    chip = 2 * xi + yi
    dev = 4 * xi + 2 * yi + ci
    n_ada = ada_w.shape[2]
    n_kva = kv_ada_w.shape[1]
    n_asc = a_scale.shape[1]

    c_all = _allgather8(jnp.broadcast_to(c, (SUBLANES, D)), "gather_c")[0][::SUBLANES]
    ada_b_sh = lax.dynamic_slice_in_dim(ada_b, chip * n_ada, n_ada, axis=1)
    kvb_sh = lax.dynamic_slice_in_dim(kv_ada_b, chip * n_kva, n_kva, axis=0)
    c_act, mod_ada = _cmat(c_all, ada_w, ada_b_sh[:, None, :], "mod_ada")
    _, mod_kv = _cmat(c_all, kv_ada_w[None], kvb_sh[None, None, :], "mod_kv")
    part = jnp.concatenate([mod_ada.transpose(1, 0, 2).reshape(SUBLANES, 4 * n_ada), mod_kv[0],
                            jnp.broadcast_to(a_scale.reshape(1, 2 * n_asc), (SUBLANES, 2 * n_asc))], axis=1)
    sh = dict(a_in=[a_w_in[l].astype(BF16) for l in range(2)], a_grp=[a_w_group[l].astype(BF16) for l in range(2)],
              a_out=[a_w_out[l].astype(BF16) for l in range(2)], kv=[w_kv.astype(BF16)],
              b_in=[b_w_in[l].astype(BF16) for l in range(2)], b_out=[b_w_out[l].astype(BF16) for l in range(2)])
    gathered, w_first = _allgather8(part, "gather_mod", comm=_Comm(gathers=[sh["a_in"][0], sh["a_grp"][0], sh["a_out"][0]]))
    rows = jnp.concatenate([lax.dynamic_slice_in_dim(gathered, SUBLANES * (2 * p + ci) + dev, 1, axis=0)
                            for p in range(NCHIP)], axis=0)
    mods = jnp.stack([rows[:, l * n_ada:(l + 1) * n_ada].reshape(3 * D) for l in range(4)])
    kvmod = rows[:, 4 * n_ada:4 * n_ada + n_kva].reshape(2 * D)
    o_asc = 4 * n_ada + n_kva
    a_scale_full = jnp.stack([rows[:, o_asc + l * n_asc:o_asc + (l + 1) * n_asc].reshape(E) for l in range(2)])

    chip_arr = jnp.reshape(chip, (1,)).astype(jnp.int32)
    dh, both, g3 = _example_step(x[0], loss_target[0], mods, kvmod, a_scale_full, norm_g, kv_norm_g, final_g,
                                 b_rel_bias, sh, w_first, chip_arr)
    grad_x = dh[None]

    red, loss_tile = _stats_reduce(g3, ROW_FIN + 1, "stats_reduce")
    loss = loss_tile[0, 0]

    def cat(rows_):
        return jnp.concatenate(rows_, axis=-1)

    g_ada_b = jnp.stack([cat([red[ROW_IN[l]], red[ROW_IN[l] + 1], red[ROW_OUT[l]]]) for l in range(4)])
    g_norm_g = jnp.stack([red[ROW_IN[l] + 2] for l in range(4)])
    g_kv_norm_g = red[ROW_KV + 2]
    g_kv_ada_b = cat([red[ROW_KV], red[ROW_KV + 1]])
    g_final_g = red[ROW_FIN]
    g_asc_full = jnp.stack([red[ROW_ASC[l]:ROW_ASC[l] + 2].reshape(E) for l in range(2)])
    g_a_scale = lax.dynamic_slice_in_dim(g_asc_full, chip * n_asc, n_asc, axis=1)
    g_rel = jnp.stack([red[ROW_RB[bi]:ROW_RB[bi] + NH * NRELP // D].reshape(NH, NRELP)[:, :NREL] for bi in range(2)])

    dmod = jnp.stack([cat([g3[:, ROW_IN[l]], g3[:, ROW_IN[l] + 1], g3[:, ROW_OUT[l]]]) for l in range(4)])
    dmod_sh = lax.dynamic_slice_in_dim(dmod, chip * n_ada, n_ada, axis=2)
    dkv = cat([g3[:, ROW_KV], g3[:, ROW_KV + 1]])[None]
    dkv_sh = lax.dynamic_slice_in_dim(dkv, chip * n_kva, n_kva, axis=2)
    c_act_t = c_act.T
    g_ada_w = _grad_ada(c_act_t, dmod_sh, "grad_ada_w")
    g_kv_ada_w = _grad_ada(c_act_t, dkv_sh, "grad_kv_ada_w")

    def upd(w, m, v, g, name, shape3):
        g = g.reshape(shape3) if not isinstance(g, list) else g
        outs = _adamw(w.reshape(shape3), m.reshape(shape3), v.reshape(shape3), g, name)
        return [o.reshape(w.shape) for o in outs]

    def pair(name):
        return [both[name + "0"], both[name + "1"]]

    res = {}
    res["ada_w"] = upd(ada_w, m_ada_w, v_ada_w, g_ada_w, "adamw_ada_w", ada_w.shape)
    res["ada_b"] = upd(ada_b, m_ada_b, v_ada_b, g_ada_b, "adamw_ada_b", (1,) + ada_b.shape)
    res["norm_g"] = upd(norm_g, m_norm_g, v_norm_g, g_norm_g, "adamw_norm_g", (1,) + norm_g.shape)
    res["a_w_in"] = upd(a_w_in, m_a_w_in, v_a_w_in, pair("a_in"), "adamw_a_w_in", a_w_in.shape)
    res["a_w_group"] = upd(a_w_group, m_a_w_group, v_a_w_group, pair("a_grp"), "adamw_a_w_group", (2, GW, GW))
    res["a_scale"] = upd(a_scale, m_a_scale, v_a_scale, g_a_scale, "adamw_a_scale", (1,) + a_scale.shape)
    res["a_w_out"] = upd(a_w_out, m_a_w_out, v_a_w_out, pair("a_out"), "adamw_a_w_out", a_w_out.shape)
    res["kv_norm_g"] = upd(kv_norm_g, m_kv_norm_g, v_kv_norm_g, g_kv_norm_g, "adamw_kv_norm_g", (1, 1, D))
    res["kv_ada_w"] = upd(kv_ada_w, m_kv_ada_w, v_kv_ada_w, g_kv_ada_w, "adamw_kv_ada_w", (1,) + kv_ada_w.shape)
    res["kv_ada_b"] = upd(kv_ada_b, m_kv_ada_b, v_kv_ada_b, g_kv_ada_b, "adamw_kv_ada_b", (1, 1, 2 * D))
    res["w_kv"] = upd(w_kv, m_w_kv, v_w_kv, [both["kv"]], "adamw_w_kv", (1,) + w_kv.shape)
    res["b_w_in"] = upd(b_w_in, m_b_w_in, v_b_w_in, pair("b_in"), "adamw_b_w_in", b_w_in.shape)
    res["b_rel_bias"] = upd(b_rel_bias, m_b_rel_bias, v_b_rel_bias, g_rel, "adamw_b_rel_bias", (1, 2 * NH, NREL))
    res["b_w_out"] = upd(b_w_out, m_b_w_out, v_b_w_out, pair("b_out"), "adamw_b_w_out", b_w_out.shape)
    res["final_g"] = upd(final_g, m_final_g, v_final_g, g_final_g, "adamw_final_g", (1, 1, D))

    names = ["ada_w", "ada_b", "norm_g", "a_w_in", "a_w_group", "a_scale", "a_w_out", "kv_norm_g", "kv_ada_w", "kv_ada_b",
             "w_kv", "b_w_in", "b_rel_bias", "b_w_out", "final_g"]
    return (loss, grad_x, *[res[n][0] for n in names], *[res[n][1] for n in names], *[res[n][2] for n in names],
            *[res[n][3] for n in names])
```

```python
import math

import jax
import jax.numpy as jnp
from jax import lax
from jax.experimental import pallas as pl
from jax.experimental.pallas import tpu as pltpu

F32 = jnp.float32
BF16 = jnp.bfloat16

D = 1024
E = 2048
NH = 16
HD = 128
CHUNK = 64
LEFT = 8
PAD = LEFT * CHUNK
NREL = 257
NRELP = 384
REL_CLIP = 128
EPS = 1e-6
NEG = -1e30
LOG2E = math.log2(math.e)
SM_SCALE = HD ** -0.5
POOL_W = (2, 4, 8, 16)
GW = 512
HALO = 16
QC = 4
QB = QC * CHUNK
NMASK = PAD // QB
WIN = (QC + LEFT) * CHUNK
BW = (LEFT + 2) * CHUNK
DBW = 4 * CHUNK
NSUB = 8
NCHIP = 4
LANES = 128
SUBLANES = 8

ADAM_LR = 0.001
ADAM_B1 = 0.9
ADAM_B2 = 0.999
ADAM_EPS = 1e-08
ADAM_WD = 0.01
ADAM_STEP = 10

MESH = pl.DeviceIdType.MESH
ANY = pl.BlockSpec(memory_space=pl.ANY)


def _params(n_axes, vmem_mb):
    return pltpu.CompilerParams(dimension_semantics=("arbitrary",) * n_axes, vmem_limit_bytes=vmem_mb * 2 ** 20)


def _nn(a, b):
    return jnp.dot(a, b, preferred_element_type=F32)


def _nt(a, b):
    return lax.dot_general(a, b, (((1,), (1,)), ((), ())), preferred_element_type=F32)


def _tn(a, b):
    return lax.dot_general(a, b, (((0,), (0,)), ((), ())), preferred_element_type=F32)


def _row(n):
    return pl.BlockSpec((1, n), lambda i: (0, 0))


def _colsum(x):
    return jnp.sum(x, axis=0, keepdims=True)


def _place():
    return lax.axis_index("x"), lax.axis_index("y"), lax.axis_index("c")


class _Comm:
    def __init__(self, gathers=(), scatters=(), swaps=()):
        self.n_g = len(gathers)
        self.n_chip = len(gathers) + len(scatters)
        self.n_sw = len(swaps)
        self.arrays = list(gathers) + list(scatters) + list(swaps)
        self.n = len(self.arrays)
        self.half = [a.shape[0] // 2 for a in gathers]
        self.out_shape = ([jax.ShapeDtypeStruct((NCHIP,) + a.shape, a.dtype) for a in gathers]
                          + [jax.ShapeDtypeStruct((3,) + a.shape[1:], a.dtype) for a in scatters]
                          + [jax.ShapeDtypeStruct(a.shape, a.dtype) for a in swaps])
        n_c, n_f, n_s = max(3 * self.n_chip, 1), max(3 * self.n_g, 1), max(self.n_sw, 1)
        self.scratch = [pltpu.SemaphoreType.DMA((n_c,)), pltpu.SemaphoreType.DMA((n_c,)),
                        pltpu.SemaphoreType.DMA((max(self.n_g, 1),)), pltpu.SemaphoreType.DMA((n_f,)),
                        pltpu.SemaphoreType.DMA((n_f,)), pltpu.SemaphoreType.DMA((n_s,)), pltpu.SemaphoreType.DMA((n_s,))]

    def _chip_copies(self, ins, outs, send, recv, landing):
        x, y, c = _place()
        chips = [(1 - x, y), (x, 1 - y), (1 - x, 1 - y)]
        mine = 2 * x + y
        cps = []
        for k in range(self.n_chip):
            for j, (cx, cy) in enumerate(chips):
                q = 2 * cx + cy
                if k < self.n_g:
                    part = pl.ds(c * self.half[k], self.half[k])
                    src = ins[k].at[part]
                    dst = outs[k].at[q if landing else mine, part]
                else:
                    src = ins[k].at[q]
                    dst = outs[k].at[j]
                cps.append(pltpu.make_async_remote_copy(
                    src_ref=src, dst_ref=dst, send_sem=send.at[3 * k + j], recv_sem=recv.at[3 * k + j],
                    device_id=(cx, cy, c), device_id_type=MESH))
        return cps

    def _core_copies(self, outs, fsend, frecv, landing):
        x, y, c = _place()
        chips = [(1 - x, y), (x, 1 - y), (1 - x, 1 - y)]
        cps = []
        for k in range(self.n_g):
            for j, (cx, cy) in enumerate(chips):
                part = pl.ds((1 - c if landing else c) * self.half[k], self.half[k])
                blk = outs[k].at[2 * cx + cy, part]
                cps.append(pltpu.make_async_remote_copy(
                    src_ref=blk, dst_ref=blk, send_sem=fsend.at[3 * k + j], recv_sem=frecv.at[3 * k + j],
                    device_id=(x, y, 1 - c), device_id_type=MESH))
        return cps

    def _local_copies(self, ins, outs, loc):
        x, y, _ = _place()
        return [pltpu.make_async_copy(ins[k], outs[k].at[2 * x + y], loc.at[k]) for k in range(self.n_g)]

    def _swap_copies(self, ins, outs, ssend, srecv):
        x, y, c = _place()
        return [pltpu.make_async_remote_copy(
            src_ref=ins[k], dst_ref=outs[k], send_sem=ssend.at[k - self.n_chip], recv_sem=srecv.at[k - self.n_chip],
            device_id=(x, y, 1 - c), device_id_type=MESH) for k in range(self.n_chip, self.n)]

    def start(self, ins, outs, send, recv, loc, fsend, frecv, ssend, srecv):
        for cp in (self._local_copies(ins, outs, loc) + self._chip_copies(ins, outs, send, recv, False)
                   + self._swap_copies(ins, outs, ssend, srecv)):
            cp.start()

    def wait(self, ins, outs, send, recv, loc, fsend, frecv, ssend, srecv):
        lands = self._chip_copies(ins, outs, send, recv, True)
        passes = self._core_copies(outs, fsend, frecv, False)
        for k in range(self.n_chip):
            for j in range(3):
                lands[3 * k + j].wait_recv()
                if k < self.n_g:
                    passes[3 * k + j].start()
        for cp in self._core_copies(outs, fsend, frecv, True):
            cp.wait_recv()
        swaps = self._swap_copies(ins, outs, ssend, srecv)
        for cp in swaps:
            cp.wait_recv()
        for cp in self._chip_copies(ins, outs, send, recv, False) + passes + swaps:
            cp.wait_send()
        for cp in self._local_copies(ins, outs, loc):
            cp.wait()


def _call(body, name, grid, in_specs, out_specs, out_shape, scratch, params, args, comm=None):
    n_in, n_out, n_sc = len(in_specs), len(out_specs), len(scratch)
    if comm is None:
        outs = pl.pallas_call(body, name=name, grid=grid, in_specs=in_specs, out_specs=out_specs, out_shape=out_shape,
                              scratch_shapes=scratch, compiler_params=params)(*args)
        return list(outs), []
    n = comm.n
    o0 = n_in + n
    s0 = o0 + n_out + n

    def wrapped(*refs):
        c_refs = (refs[n_in:o0], refs[o0 + n_out:s0]) + tuple(refs[s0 + n_sc:])
        ids = [pl.program_id(a) for a in range(len(grid))]
        first = ids[0] == 0
        last = ids[0] == grid[0] - 1
        for a in range(1, len(grid)):
            first = first & (ids[a] == 0)
            last = last & (ids[a] == grid[a] - 1)

        @pl.when(first)
        def _():
            comm.start(*c_refs)

        body(*refs[:n_in], *refs[o0:o0 + n_out], *refs[s0:s0 + n_sc])

        @pl.when(last)
        def _():
            comm.wait(*c_refs)

    outs = pl.pallas_call(
        wrapped, name=name, grid=grid, in_specs=list(in_specs) + [ANY] * n, out_specs=list(out_specs) + [ANY] * n,
        out_shape=list(out_shape) + comm.out_shape, scratch_shapes=list(scratch) + comm.scratch, compiler_params=params,
    )(*args, *comm.arrays)
    return list(outs[:n_out]), list(outs[n_out:])


def _comm_only(comm, name):
    def body(*refs):
        c_refs = (refs[:comm.n], refs[comm.n:2 * comm.n]) + tuple(refs[2 * comm.n:])
        comm.start(*c_refs)
        comm.wait(*c_refs)

    return pl.pallas_call(body, name=name, in_specs=[ANY] * comm.n, out_specs=[ANY] * comm.n, out_shape=comm.out_shape,
                          scratch_shapes=comm.scratch)(*comm.arrays)


def _in_fwd(h, g, shift, scale, w, dt_a, dt_b, name, pad_rows=0, comm=None, tm=512):
    S = h.shape[0]
    n_pad = pad_rows // tm

    def body(h_ref, g_ref, sh_ref, sc_ref, w_hbm, u_ref, oa_ref, ob_ref, w_v, sem):
        i = pl.program_id(0)

        @pl.when(i == 0)
        def _():
            cp = pltpu.make_async_copy(w_hbm, w_v, sem)
            cp.start()
            cp.wait()

        hh = h_ref[...]
        r = lax.rsqrt(jnp.mean(hh * hh, axis=-1, keepdims=True) + EPS)
        u = (hh * r * g_ref[...]) * (1.0 + sc_ref[...]) + sh_ref[...]
        ub = u.astype(BF16)
        u_ref[...] = ub
        for q in range(NCHIP):
            o_ref = oa_ref if q < 2 else ob_ref
            o_ref[:, (q % 2) * D:(q % 2 + 1) * D] = _nn(ub, w_v[q]).astype(o_ref.dtype)

        if n_pad:
            @pl.when(i < n_pad)
            def _():
                oa_ref[...] = jnp.zeros(oa_ref.shape, oa_ref.dtype)
                ob_ref[...] = jnp.zeros(ob_ref.shape, ob_ref.dtype)

    def src(i):
        return (jnp.maximum(i - n_pad, 0), 0)

    outs, landed = _call(
        body, name, (S // tm + n_pad,),
        [pl.BlockSpec((tm, D), src), _row(D), _row(D), _row(D), ANY],
        [pl.BlockSpec((tm, D), src), pl.BlockSpec((tm, E), lambda i: (i, 0)), pl.BlockSpec((tm, E), lambda i: (i, 0))],
        [jax.ShapeDtypeStruct((S, D), BF16), jax.ShapeDtypeStruct((S + pad_rows, E), dt_a),
         jax.ShapeDtypeStruct((S + pad_rows, E), dt_b)],
        [pltpu.VMEM((NCHIP, D, D), BF16), pltpu.SemaphoreType.DMA],
        _params(1, 52), (h, g, shift, scale, w), comm)
    return outs, landed


def _a_fwd(h, g, shift, scale, asc, gate, w_in, wg, w_out, name, comm=None, tm=512):
    S = h.shape[0]

    def body(h_ref, g_ref, sh_ref, sc_ref, as_ref, gate_ref, wi_hbm, wg_hbm, wo_hbm,
             u_ref, z_ref, p_ref, m_ref, y_ref, ho_ref, wi_v, wg_v, wo_v, buf, sems):
        i = pl.program_id(0)

        @pl.when(i == 0)
        def _():
            cps = [pltpu.make_async_copy(wi_hbm, wi_v, sems.at[0]), pltpu.make_async_copy(wg_hbm, wg_v, sems.at[1]),
                   pltpu.make_async_copy(wo_hbm, wo_v, sems.at[2])]
            for cp in cps:
                cp.start()
            buf[0:HALO, :] = jnp.zeros((HALO, E), F32)
            for cp in cps:
                cp.wait()

        hh = h_ref[...]
        r = lax.rsqrt(jnp.mean(hh * hh, axis=-1, keepdims=True) + EPS)
        ub = ((hh * r * g_ref[...]) * (1.0 + sc_ref[...]) + sh_ref[...]).astype(BF16)
        u_ref[...] = ub
        for q in range(2):
            buf[HALO:HALO + tm, q * D:(q + 1) * D] = _nn(ub, wi_v[q])
        t = i * tm + lax.broadcasted_iota(jnp.int32, (tm, 1), 0)
        y = None
        for gi, w in enumerate(POOL_W):
            cols = slice(gi * GW, (gi + 1) * GW)
            x = buf[:, cols]
            s = x
            k = 1
            while k < w:
                s = s + pltpu.roll(s, k, 0)
                k *= 2
            inv_cnt = 1.0 / jnp.minimum(t + 1, w).astype(F32)
            pb = (s[HALO:, :] * inv_cnt - x[HALO:, :]).astype(BF16)
            p_ref[:, cols] = pb
            mb = _nn(pb, wg_v[gi]).astype(BF16)
            m_ref[:, cols] = mb
            zb = _nn(ub, wi_v[2 + gi // 2, :, (gi % 2) * GW:(gi % 2 + 1) * GW]).astype(BF16)
            z_ref[:, cols] = zb
            zz = zb.astype(F32)
            act = ((mb.astype(F32) * as_ref[:, cols]) * (zz * jax.nn.sigmoid(zz))).astype(BF16)
            part = _nn(act, wo_v[gi])
            y = part if y is None else y + part
        buf[0:HALO, :] = buf[tm:tm + HALO, :]
        y_ref[...] = y.astype(BF16)
        ho_ref[...] = hh + gate_ref[...] * y

    rows_d = pl.BlockSpec((tm, D), lambda i: (i, 0))
    rows_e = pl.BlockSpec((tm, E), lambda i: (i, 0))
    return _call(
        body, name, (S // tm,),
        [rows_d, _row(D), _row(D), _row(D), _row(E), _row(D), ANY, ANY, ANY],
        [rows_d, rows_e, rows_e, rows_e, rows_d, rows_d],
        [jax.ShapeDtypeStruct((S, D), BF16), jax.ShapeDtypeStruct((S, E), BF16), jax.ShapeDtypeStruct((S, E), BF16),
         jax.ShapeDtypeStruct((S, E), BF16), jax.ShapeDtypeStruct((S, D), BF16), jax.ShapeDtypeStruct((S, D), F32)],
        [pltpu.VMEM((NCHIP, D, D), BF16), pltpu.VMEM((4, GW, GW), BF16), pltpu.VMEM((NCHIP, GW, D), BF16),
         pltpu.VMEM((tm + HALO, E), F32), pltpu.SemaphoreType.DMA((3,))],
        _params(1, 60), (h, g, shift, scale, asc, gate, w_in, wg, w_out), comm)


def _out_fwd(a, z, w, gate, h, name, head=None, comm=None, tm=512):
    S = h.shape[0]
    kb = E // NCHIP
    n_in = 5 if head is None else 7

    def body(*refs):
        a_ref, z_ref, w_hbm, gate_ref, h_ref = refs[:5]
        w_v, sem = refs[-2:]
        i = pl.program_id(0)

        @pl.when(i == 0)
        def _():
            cp = pltpu.make_async_copy(w_hbm, w_v, sem)
            cp.start()
            cp.wait()

        y = None
        for p in range(NCHIP):
            cols = slice(p * kb, (p + 1) * kb)
            zz = z_ref[:, cols].astype(F32)
            act = (a_ref[:, cols].astype(F32) * (zz * jax.nn.sigmoid(zz))).astype(BF16)
            part = _nn(act, w_v[p])
            y = part if y is None else y + part
        refs[n_in][...] = y.astype(BF16)
        hh = h_ref[...] + gate_ref[...] * y
        if head is None:
            refs[n_in + 1][...] = hh
            return
        g_ref, t_ref = refs[5:7]
        dh_ref, st_ref = refs[n_in + 1:n_in + 3]

        @pl.when(i == 0)
        def _():
            st_ref[...] = jnp.zeros((SUBLANES, D), F32)

        r = lax.rsqrt(jnp.mean(hh * hh, axis=-1, keepdims=True) + EPS)
        xhat = hh * r
        diff = xhat * g_ref[...] - t_ref[...]
        st_ref[1:2, :] += _colsum(diff * diff)
        dout = diff * (1.0 / D)
        st_ref[0:1, :] += _colsum(dout * xhat)
        dx = dout * g_ref[...]
        dh_ref[...] = r * (dx - xhat * jnp.mean(dx * xhat, axis=-1, keepdims=True))

    rows_d = pl.BlockSpec((tm, D), lambda i: (i, 0))
    rows_e = pl.BlockSpec((tm, E), lambda i: (i, 0))
    in_specs = [rows_e, rows_e, ANY, _row(D), rows_d]
    out_specs = [rows_d, rows_d]
    out_shape = [jax.ShapeDtypeStruct((S, D), BF16), jax.ShapeDtypeStruct((S, D), F32)]
    args = (a, z, w, gate, h)
    if head is not None:
        in_specs += [_row(D), rows_d]
        out_specs += [pl.BlockSpec((SUBLANES, D), lambda i: (0, 0))]
        out_shape += [jax.ShapeDtypeStruct((SUBLANES, D), F32)]
        args += tuple(head)
    return _call(body, name, (S // tm,), in_specs, out_specs, out_shape,
                 [pltpu.VMEM((NCHIP, kb, D), BF16), pltpu.SemaphoreType.DMA], _params(1, 52), args, comm)


TW = BW + LANES


def _diag_onehot(transpose):
    shape = (TW, NRELP) if transpose else (NRELP, TW)
    j = lax.broadcasted_iota(jnp.int32, shape, 0 if transpose else 1)
    r = lax.broadcasted_iota(jnp.int32, shape, 1 if transpose else 0)
    idx = jnp.clip(PAD - (j - LANES), -REL_CLIP, REL_CLIP) + REL_CLIP
    return jnp.where(idx == r, 1.0, 0.0).astype(BF16)


def _strip_valid():
    m = lax.broadcasted_iota(jnp.int32, (NH, BW), 1)
    return m < (LEFT + 1) * CHUNK, m >= CHUNK


def _bias_build(rb, name):
    def body(rb_ref, a_ref, b_ref):
        x = rb_ref[...]
        hi = x.astype(BF16)
        r1 = x - hi.astype(F32)
        mid = r1.astype(BF16)
        lo = (r1 - mid.astype(F32)).astype(BF16)
        oh = _diag_onehot(False)
        diag = (_nn(hi, oh) + _nn(mid, oh)) + _nn(lo, oh)
        valid_a, valid_b = _strip_valid()
        for qi in range(CHUNK):
            a_ref[qi] = jnp.where(valid_a, pltpu.roll(diag, TW - (LANES - qi), 1)[:, :BW], NEG)
            b_ref[qi] = jnp.where(valid_b, pltpu.roll(diag, TW - (CHUNK - qi), 1)[:, :BW], NEG)

    vmem = pl.BlockSpec(memory_space=pltpu.VMEM)
    return pl.pallas_call(
        body, name=name, in_specs=[vmem], out_specs=[vmem, vmem],
        out_shape=[jax.ShapeDtypeStruct((CHUNK, NH, BW), F32), jax.ShapeDtypeStruct((CHUNK, NH, BW), F32)],
        compiler_params=pltpu.CompilerParams(vmem_limit_bytes=32 * 2 ** 20),
    )(rb)


def _dbias_reduce(dba, dbb, name):
    def body(a_ref, b_ref, o_ref):
        valid_a, valid_b = _strip_valid()
        zeros = jnp.zeros((NH, TW - BW), F32)
        acc = jnp.zeros((NH, TW), F32)
        for qi in range(CHUNK):
            xa = jnp.concatenate([jnp.where(valid_a, a_ref[qi], 0.0), zeros], axis=1)
            xb = jnp.concatenate([jnp.where(valid_b, b_ref[qi], 0.0), zeros], axis=1)
            acc = acc + (pltpu.roll(xa, LANES - qi, 1) + pltpu.roll(xb, CHUNK - qi, 1))
        oh = _diag_onehot(True)
        hi = acc.astype(BF16)
        mid = (acc - hi.astype(F32)).astype(BF16)
        r = lax.broadcasted_iota(jnp.int32, (NH, NRELP), 1)
        near = jnp.where(r < 2 * REL_CLIP, _nn(hi, oh) + _nn(mid, oh), 0.0)
        o_ref[...] = jnp.where(r == 2 * REL_CLIP, -jnp.sum(near, axis=-1, keepdims=True), near)

    vmem = pl.BlockSpec(memory_space=pltpu.VMEM)
    return pl.pallas_call(
        body, name=name, in_specs=[vmem, vmem], out_specs=vmem,
        out_shape=jax.ShapeDtypeStruct((NH, NRELP), F32),
        compiler_params=pltpu.CompilerParams(vmem_limit_bytes=32 * 2 ** 20),
    )(dba, dbb)


def _build_bias(bias3, ba_ref, bb_ref):
    bias3[NMASK] = jnp.full((QB, WIN), NEG, F32)
    for qc in range(QC):
        rows = slice(qc * CHUNK, (qc + 1) * CHUNK)
        if qc % 2 == 0:
            bias3[NMASK, rows, qc * CHUNK:qc * CHUNK + BW] = ba_ref[...] * LOG2E
        else:
            bias3[NMASK, rows, (qc - 1) * CHUNK:(qc - 1) * CHUNK + BW] = bb_ref[...] * LOG2E
    col = lax.broadcasted_iota(jnp.int32, (QB, WIN), 1)
    for sub in range(NMASK):
        bias3[sub] = jnp.where(col < PAD - sub * QB, NEG, bias3[NMASK])


def _nsub(S):
    n = min(NSUB, S // QB)
    assert S % (n * QB) == 0 and n >= NMASK
    return n


def _row0(i, sub, nsub):
    return pl.multiple_of((i * nsub + sub) * QB, QB)


def _scores(q_ref, k_ref, i, sub, nsub):
    return _nt(q_ref[sub * QB:(sub + 1) * QB, :], k_ref[pl.ds(_row0(i, sub, nsub), WIN), :])


def _exp_parts(s, bias3, i, sub):
    which = jnp.where(i == 0, sub, NMASK) if sub < NMASK else NMASK
    s = s * (SM_SCALE * LOG2E) + bias3[which]
    e = jnp.exp2(s - jnp.max(s, axis=-1, keepdims=True))
    return e, jnp.sum(e, axis=-1, keepdims=True)


def _attn_fwd(q, kp, vp, ba, bb, name, comm=None):
    S = q.shape[0]
    nsub = _nsub(S)
    R = nsub * QB

    def body(q_ref, k_ref, v_ref, ba_ref, bb_ref, o_ref, p_ref, bias3):
        i = pl.program_id(1)

        @pl.when(i == 0)
        def _():
            _build_bias(bias3, ba_ref, bb_ref)

        s_next = _scores(q_ref, k_ref, i, 0, nsub)
        for sub in range(nsub):
            s = s_next
            if sub + 1 < nsub:
                s_next = _scores(q_ref, k_ref, i, sub + 1, nsub)
            e, l = _exp_parts(s, bias3, i, sub)
            pb = (e * (1.0 / l)).astype(BF16)
            p_ref[sub] = pb
            o_ref[sub * QB:(sub + 1) * QB, :] = _nn(pb, v_ref[pl.ds(_row0(i, sub, nsub), WIN), :]).astype(BF16)

    return _call(
        body, name, (NH, S // R),
        [pl.BlockSpec((R, HD), lambda h, i: (i, h)), pl.BlockSpec((S + PAD, HD), lambda h, i: (0, h)),
         pl.BlockSpec((S + PAD, HD), lambda h, i: (0, h)), pl.BlockSpec((None, CHUNK, BW), lambda h, i: (h, 0, 0)),
         pl.BlockSpec((None, CHUNK, BW), lambda h, i: (h, 0, 0))],
        [pl.BlockSpec((R, HD), lambda h, i: (i, h)), pl.BlockSpec((None, nsub, QB, WIN), lambda h, i: (h, i, 0, 0))],
        [jax.ShapeDtypeStruct((S, E), BF16), jax.ShapeDtypeStruct((NH, S // QB, QB, WIN), BF16)],
        [pltpu.VMEM((NMASK + 1, QB, WIN), F32)],
        _params(2, 48), (q, kp, vp, ba, bb), comm)


def _store_grad(acc, stage, dw_hbm, sem):
    for q in range(NCHIP):
        stage[...] = acc[q].astype(BF16)
        cp = pltpu.make_async_copy(stage, dw_hbm.at[q], sem)
        cp.start()
        cp.wait()


def _out_bwd(dh, y, gate, a, cs, z, w, name, comm=None, tm=256):
    S = dh.shape[0]
    kb = E // NCHIP
    cb = 256
    n_t = S // tm

    def body(dh_ref, y_ref, gate_ref, a_ref, cs_ref, z_ref, w_hbm, da_ref, dz_ref, dw_hbm, st_ref, w_v, acc, stage, sem):
        i = pl.program_id(0)

        @pl.when(i == 0)
        def _():
            cp = pltpu.make_async_copy(w_hbm, w_v, sem)
            cp.start()
            acc[...] = jnp.zeros(acc.shape, F32)
            st_ref[...] = jnp.zeros((SUBLANES, D), F32)
            cp.wait()

        dhh = dh_ref[...]
        st_ref[0:1, :] += _colsum(dhh * y_ref[...].astype(F32))
        dy = (dhh * gate_ref[...]).astype(BF16)
        for blk in range(E // cb):
            p, r0 = divmod(blk * cb, kb)
            cols = slice(blk * cb, (blk + 1) * cb)
            zz = z_ref[:, cols].astype(F32)
            sig = jax.nn.sigmoid(zz)
            sz = zz * sig
            ae = a_ref[:, cols].astype(F32) * cs_ref[:, cols]
            acc[p, r0:r0 + cb, :] += _tn((ae * sz).astype(BF16), dy)
            dact = _nt(dy, w_v[p, r0:r0 + cb, :])
            da_ref[:, cols] = (dact * sz).astype(BF16)
            dz_ref[:, cols] = (dact * ae * (sig * (1.0 + zz * (1.0 - sig)))).astype(BF16)

        @pl.when(i == n_t - 1)
        def _():
            _store_grad(acc, stage, dw_hbm, sem)

    return _call(
        body, name, (n_t,),
        [pl.BlockSpec((tm, D), lambda i: (i, 0)), pl.BlockSpec((tm, D), lambda i: (i, 0)), _row(D),
         pl.BlockSpec((tm, E), lambda i: (i, 0)), _row(E), pl.BlockSpec((tm, E), lambda i: (i, 0)), ANY],
        [pl.BlockSpec((tm, E), lambda i: (i, 0)), pl.BlockSpec((tm, E), lambda i: (i, 0)), ANY,
         pl.BlockSpec((SUBLANES, D), lambda i: (0, 0))],
        [jax.ShapeDtypeStruct((S, E), BF16), jax.ShapeDtypeStruct((S, E), BF16),
         jax.ShapeDtypeStruct((NCHIP, kb, D), BF16), jax.ShapeDtypeStruct((SUBLANES, D), F32)],
        [pltpu.VMEM((NCHIP, kb, D), BF16), pltpu.VMEM((NCHIP, kb, D), F32), pltpu.VMEM((kb, D), BF16),
         pltpu.SemaphoreType.DMA],
        _params(1, 52), (dh, y, gate, a, cs, z, w), comm)


def _attn_bwd(q, kp, vp, probs, do, prev, name, comm=None):
    S = q.shape[0]
    nsub = _nsub(S)
    R = nsub * QB
    n_i = S // R
    dt_kv = F32 if prev is None else BF16

    def body(*refs):
        q_ref, k_ref, v_ref, p_ref, do_ref = refs[:5]
        refs = refs[5:]
        if prev is not None:
            pk_hbm, pv_hbm = refs[:2]
            refs = refs[2:]
        dq_ref, dk_ref, dv_ref, dba_ref, dbb_ref, dbias, dk_acc, dv_acc = refs[:8]
        if prev is not None:
            pk_v, pv_v, sems = refs[8:]
        h = pl.program_id(0)
        i = pl.program_id(1)

        def prev_copies():
            cols = pl.ds(pl.multiple_of(h * HD, HD), HD)
            return (pltpu.make_async_copy(pk_hbm.at[:, cols], pk_v, sems.at[0]),
                    pltpu.make_async_copy(pv_hbm.at[:, cols], pv_v, sems.at[1]))

        @pl.when(i == 0)
        def _():
            if prev is not None:
                for cp in prev_copies():
                    cp.start()
            dbias[...] = jnp.zeros((2, CHUNK, DBW), F32)
            dk_acc[...] = jnp.zeros((S + PAD, HD), F32)
            dv_acc[...] = jnp.zeros((S + PAD, HD), F32)

        def mxu_in(sub):
            return _nt(do_ref[sub * QB:(sub + 1) * QB, :], v_ref[pl.ds(_row0(i, sub, nsub), WIN), :])

        nxt = mxu_in(0)
        for sub in range(nsub):
            rows = slice(sub * QB, (sub + 1) * QB)
            win = pl.ds(_row0(i, sub, nsub), WIN)
            dp = nxt
            if sub + 1 < nsub:
                nxt = mxu_in(sub + 1)
            pb = p_ref[sub]
            p = pb.astype(F32)
            ds = p * (dp - jnp.sum(p * dp, axis=-1, keepdims=True))
            for par in range(2):
                part = None
                for qc in range(par, QC, 2):
                    c0 = (qc - par) * CHUNK + BW - DBW
                    blk_ = ds[qc * CHUNK:(qc + 1) * CHUNK, c0:c0 + DBW]
                    part = blk_ if part is None else part + blk_
                dbias[par] += part
            dsb = (ds * SM_SCALE).astype(BF16)
            dq_ref[rows, :] = _nn(dsb, k_ref[win, :]).astype(BF16)
            dk_acc[win, :] += _tn(dsb, q_ref[rows, :])
            dv_acc[win, :] += _tn(pb, do_ref[rows, :])

        @pl.when(i == n_i - 1)
        def _():
            zeros = jnp.zeros((CHUNK, BW - DBW), F32)
            dba_ref[...] = jnp.concatenate([zeros, dbias[0]], axis=1)
            dbb_ref[...] = jnp.concatenate([zeros, dbias[1]], axis=1)
            if prev is None:
                dk_ref[...] = dk_acc[...]
                dv_ref[...] = dv_acc[...]
            else:
                for cp in prev_copies():
                    cp.wait()
                dk_ref[...] = (dk_acc[...] + pk_v[...]).astype(BF16)
                dv_ref[...] = (dv_acc[...] + pv_v[...]).astype(BF16)

    head = pl.BlockSpec((S + PAD, HD), lambda h, i: (0, h))
    strip = pl.BlockSpec((None, CHUNK, BW), lambda h, i: (h, 0, 0))
    blk = pl.BlockSpec((R, HD), lambda h, i: (i, h))
    in_specs = [blk, head, head, pl.BlockSpec((None, nsub, QB, WIN), lambda h, i: (h, i, 0, 0)), blk]
    scratch = [pltpu.VMEM((2, CHUNK, DBW), F32), pltpu.VMEM((S + PAD, HD), F32), pltpu.VMEM((S + PAD, HD), F32)]
    args = (q, kp, vp, probs, do)
    if prev is not None:
        in_specs += [ANY, ANY]
        scratch += [pltpu.VMEM((S + PAD, HD), F32), pltpu.VMEM((S + PAD, HD), F32), pltpu.SemaphoreType.DMA((2,))]
        args += tuple(prev)
    return _call(
        body, name, (NH, n_i), in_specs, [blk, head, head, strip, strip],
        [jax.ShapeDtypeStruct((S, E), BF16), jax.ShapeDtypeStruct((S + PAD, E), dt_kv),
         jax.ShapeDtypeStruct((S + PAD, E), dt_kv), jax.ShapeDtypeStruct((NH, CHUNK, BW), F32),
         jax.ShapeDtypeStruct((NH, CHUNK, BW), F32)],
        scratch, _params(2, 56), args, comm)


def _pool_bwd(dms, mixed, pooled, wg, a_scale, name, comm=None, tm=512):
    S = dms.shape[0]
    n_t = S // tm

    def rev(i):
        return (n_t - 1 - i, 0)

    def body(d_ref, m_ref, p_ref, wg_ref, as_ref, dv_ref, dwg_ref, st_ref, buf):
        i = pl.program_id(0)

        @pl.when(i == 0)
        def _():
            buf[tm:tm + HALO, :] = jnp.zeros((HALO, E), F32)
            dwg_ref[...] = jnp.zeros((4, GW, GW), F32)
            st_ref[...] = jnp.zeros((SUBLANES, E), F32)

        t = (n_t - 1 - i) * tm + lax.broadcasted_iota(jnp.int32, (tm, 1), 0)
        st_ref[0:1, :] += _colsum(d_ref[...].astype(F32) * m_ref[...].astype(F32))
        for gi, w in enumerate(POOL_W):
            cols = slice(gi * GW, (gi + 1) * GW)
            dm = (d_ref[:, cols].astype(F32) * as_ref[:, cols]).astype(BF16)
            dpool = _nt(dm, wg_ref[gi])
            dwg_ref[gi] += _tn(p_ref[:, cols], dm)
            inv_cnt = 1.0 / jnp.minimum(t + 1, w).astype(F32)
            buf[0:tm, cols] = dpool * inv_cnt
            s = buf[:, cols]
            k = 1
            while k < w:
                s = s + pltpu.roll(s, tm + HALO - k, 0)
                k *= 2
            dv_ref[:, cols] = (s[0:tm, :] - dpool).astype(BF16)
        buf[tm:tm + HALO, :] = buf[0:HALO, :]

    return _call(
        body, name, (n_t,),
        [pl.BlockSpec((tm, E), rev), pl.BlockSpec((tm, E), rev), pl.BlockSpec((tm, E), rev),
         pl.BlockSpec((4, GW, GW), lambda i: (0, 0, 0)), _row(E)],
        [pl.BlockSpec((tm, E), rev), pl.BlockSpec((4, GW, GW), lambda i: (0, 0, 0)),
         pl.BlockSpec((SUBLANES, E), lambda i: (0, 0))],
        [jax.ShapeDtypeStruct((S, E), BF16), jax.ShapeDtypeStruct((4, GW, GW), F32),
         jax.ShapeDtypeStruct((SUBLANES, E), F32)],
        [pltpu.VMEM((tm + HALO, E), F32)],
        _params(1, 52), (dms, mixed, pooled, wg, a_scale), comm)


def _in_bwd(da, db, row_off, u, h, g, scale, w, dh_out, name, comm=None, tm=256):
    S = h.shape[0]
    n_t = S // tm
    off = row_off // tm

    def body(da_ref, db_ref, u_ref, h_ref, g_ref, sc_ref, w_hbm, dho_ref, dhi_ref, dw_hbm, st_ref, w_v, acc, stage, sem):
        i = pl.program_id(0)

        @pl.when(i == 0)
        def _():
            cp = pltpu.make_async_copy(w_hbm, w_v, sem)
            cp.start()
            acc[...] = jnp.zeros(acc.shape, F32)
            st_ref[...] = jnp.zeros((SUBLANES, D), F32)
            cp.wait()

        ub = u_ref[...]
        du = None
        for q in range(NCHIP):
            d_ref = da_ref if q < 2 else db_ref
            dv = d_ref[:, (q % 2) * D:(q % 2 + 1) * D]
            acc[q] += _tn(ub, dv)
            part = _nt(dv, w_v[q])
            du = part if du is None else du + part

        hh = h_ref[...]
        r = lax.rsqrt(jnp.mean(hh * hh, axis=-1, keepdims=True) + EPS)
        xhat = hh * r
        gg = g_ref[...]
        st_ref[0:1, :] += _colsum(du)
        st_ref[1:2, :] += _colsum(du * (xhat * gg))
        dn = du * (1.0 + sc_ref[...])
        st_ref[2:3, :] += _colsum(dn * xhat)
        dx = dn * gg
        dhi_ref[...] = dho_ref[...] + r * (dx - xhat * jnp.mean(dx * xhat, axis=-1, keepdims=True))

        @pl.when(i == n_t - 1)
        def _():
            _store_grad(acc, stage, dw_hbm, sem)

    part_spec = pl.BlockSpec((tm, E), lambda i: (i + off, 0))
    return _call(
        body, name, (n_t,),
        [part_spec, part_spec, pl.BlockSpec((tm, D), lambda i: (i, 0)), pl.BlockSpec((tm, D), lambda i: (i, 0)),
         _row(D), _row(D), ANY, pl.BlockSpec((tm, D), lambda i: (i, 0))],
        [pl.BlockSpec((tm, D), lambda i: (i, 0)), ANY, pl.BlockSpec((SUBLANES, D), lambda i: (0, 0))],
        [jax.ShapeDtypeStruct((S, D), F32), jax.ShapeDtypeStruct((NCHIP, D, D), BF16),
         jax.ShapeDtypeStruct((SUBLANES, D), F32)],
        [pltpu.VMEM((NCHIP, D, D), BF16), pltpu.VMEM((NCHIP, D, D), F32), pltpu.VMEM((D, D), BF16),
         pltpu.SemaphoreType.DMA],
        _params(1, 56), (da, db, u, h, g, scale, w, dh_out), comm)


def _cmat(c_all, w, b, name):
    L, _, n = w.shape

    def body(c_ref, w_ref, b_ref, ca_ref, o_ref):
        cc = c_ref[...]
        ca = cc * jax.nn.sigmoid(cc)
        ca_ref[...] = ca
        o_ref[...] = _nn(ca.astype(BF16), w_ref[...].astype(BF16)) + b_ref[...]

    return pl.pallas_call(
        body, name=name, grid=(L,),
        in_specs=[pl.BlockSpec((SUBLANES, D), lambda l: (0, 0)), pl.BlockSpec((None, D, n), lambda l: (l, 0, 0)),
                  pl.BlockSpec((None, 1, n), lambda l: (l, 0, 0))],
        out_specs=[pl.BlockSpec((SUBLANES, D), lambda l: (0, 0)), pl.BlockSpec((None, SUBLANES, n), lambda l: (l, 0, 0))],
        out_shape=[jax.ShapeDtypeStruct((SUBLANES, D), F32), jax.ShapeDtypeStruct((L, SUBLANES, n), F32)],
        compiler_params=_params(1, 32),
    )(c_all, w, b)


def _grad_ada(c_act_t, dmod, name):
    L, _, n = dmod.shape

    def body(c_ref, d_ref, o_ref):
        acc = None
        for b in range(SUBLANES):
            part = c_ref[:, b:b + 1] * d_ref[b:b + 1, :]
            acc = part if acc is None else acc + part
        o_ref[...] = acc

    return pl.pallas_call(
        body, name=name, grid=(L,),
        in_specs=[pl.BlockSpec((D, SUBLANES), lambda l: (0, 0)), pl.BlockSpec((None, SUBLANES, n), lambda l: (l, 0, 0))],
        out_specs=pl.BlockSpec((None, D, n), lambda l: (l, 0, 0)),
        out_shape=jax.ShapeDtypeStruct((L, D, n), F32),
        compiler_params=_params(1, 32),
    )(c_act_t, dmod)


def _stats_reduce(g3, loss_row, name):
    n_dev, rows, _ = g3.shape

    def body(g_ref, o_ref, l_ref):
        acc = g_ref[0]
        for d in range(1, n_dev):
            acc = acc + g_ref[d]
        o_ref[...] = acc
        tot = jnp.sum(o_ref[loss_row:loss_row + 1, :], axis=-1, keepdims=True)
        l_ref[...] = jnp.broadcast_to(tot * (0.5 / D), (SUBLANES, LANES))

    return pl.pallas_call(
        body, name=name,
        in_specs=[pl.BlockSpec(memory_space=pltpu.VMEM)],
        out_specs=[pl.BlockSpec(memory_space=pltpu.VMEM), pl.BlockSpec(memory_space=pltpu.VMEM)],
        out_shape=[jax.ShapeDtypeStruct((rows, D), F32), jax.ShapeDtypeStruct((SUBLANES, LANES), F32)],
        compiler_params=pltpu.CompilerParams(vmem_limit_bytes=32 * 2 ** 20),
    )(g3)


def _sum4(own, land, chip, name, tr=256):
    _, R, C = own.shape
    tr = min(tr, R)

    def body(p_ref, own_ref, land_ref, o_ref):
        o_ref[...] = ((own_ref[...].astype(F32) + land_ref[0].astype(F32)) + land_ref[1].astype(F32)) + land_ref[2].astype(F32)

    out = pl.pallas_call(
        body, name=name,
        grid_spec=pltpu.PrefetchScalarGridSpec(
            num_scalar_prefetch=1, grid=(R // tr,),
            in_specs=[pl.BlockSpec((None, tr, C), lambda i, p: (p[0], i, 0)), pl.BlockSpec((3, tr, C), lambda i, p: (0, i, 0))],
            out_specs=pl.BlockSpec((tr, C), lambda i, p: (i, 0))),
        out_shape=jax.ShapeDtypeStruct((R, C), F32),
        compiler_params=_params(1, 32),
    )(chip, pltpu.with_memory_space_constraint(own, pltpu.HBM), pltpu.with_memory_space_constraint(land, pltpu.HBM))
    return pltpu.with_memory_space_constraint(out, pltpu.HBM)


def _adamw(w, m, v, g, name, tr=256):
    L, R, C = w.shape
    tr = min(tr, R)
    stacked = not isinstance(g, (list, tuple))
    n_g = None if stacked else [len(ps) for ps in g]
    flat = [g] if stacked else [a for ps in g for a in ps]

    def body(*refs):
        w_ref, m_ref, v_ref = refs[:3]
        g_refs = refs[3:3 + len(flat)]
        go_ref, d_ref, mo_ref, vo_ref = refs[3 + len(flat):]
        if stacked:
            gg = g_refs[0][...]
        else:
            layer = pl.program_id(0)
            gg = None
            k = 0
            for li in range(L):
                gl = None
                for _ in range(n_g[li]):
                    x = g_refs[k][...]
                    gl = x if gl is None else gl + x
                    k += 1
                gg = gl if gg is None else jnp.where(layer == li, gl, gg)
        m2 = ADAM_B1 * m_ref[...] + (1.0 - ADAM_B1) * gg
        v2 = ADAM_B2 * v_ref[...] + (1.0 - ADAM_B2) * (gg * gg)
        m_hat = m2 / (1.0 - ADAM_B1 ** ADAM_STEP)
        v_hat = v2 / (1.0 - ADAM_B2 ** ADAM_STEP)
        go_ref[...] = gg
        d_ref[...] = -ADAM_LR * (m_hat / (jnp.sqrt(v_hat) + ADAM_EPS) + ADAM_WD * w_ref[...])
        mo_ref[...] = m2
        vo_ref[...] = v2

    big = pl.BlockSpec((None, tr, C), lambda l, i: (l, i, 0))
    g_specs = [big] if stacked else [pl.BlockSpec((tr, C), lambda l, i: (i, 0))] * len(flat)
    return pl.pallas_call(
        body, name=name, grid=(L, R // tr),
        in_specs=[big, big, big] + g_specs,
        out_specs=[big, big, big, big],
        out_shape=[jax.ShapeDtypeStruct((L, R, C), F32)] * 4,
        compiler_params=_params(2, 48),
    )(w, m, v, *flat)


def _allgather8(xs, name, comm=None):
    m, n = xs.shape
    n_c = 0 if comm is None else comm.n

    def body(*refs):
        x_ref, out_ref = refs[0], refs[1 + n_c]
        send_sems, recv_sems, local_sem = refs[2 + 2 * n_c:5 + 2 * n_c]
        c_refs = (refs[1:1 + n_c], refs[2 + n_c:2 + 2 * n_c]) + tuple(refs[5 + 2 * n_c:])
        if comm is not None:
            comm.start(*c_refs)
        x, y, c = _place()
        me, sibling = (x, y, c), (x, y, 1 - c)
        chips = [(1 - x, y), (x, 1 - y), (1 - x, 1 - y)]

        def rows(px, py, pc):
            return out_ref.at[pl.ds((4 * px + 2 * py + pc) * m, m), :]

        def copy(k, block, to, src=None):
            return pltpu.make_async_remote_copy(
                src_ref=rows(*block) if src is None else src, dst_ref=rows(*block),
                send_sem=send_sems.at[k], recv_sem=recv_sems.at[k], device_id=to, device_id_type=MESH)

        mine = pltpu.make_async_copy(x_ref, rows(*me), local_sem)
        mine.start()
        first = [copy(0, me, sibling, src=x_ref)]
        first += [copy(1 + j, me, (*chip, c), src=x_ref) for j, chip in enumerate(chips)]
        for cp in first:
            cp.start()
        passed = [copy(4 + j, (*chip, c), sibling) for j, chip in enumerate(chips)]
        for j, chip in enumerate(chips):
            copy(1 + j, (*chip, c), me).wait_recv()
            passed[j].start()
        copy(0, sibling, me).wait_recv()
        for j, chip in enumerate(chips):
            copy(4 + j, (*chip, 1 - c), me).wait_recv()
        for cp in first + passed:
            cp.wait_send()
        mine.wait()
        if comm is not None:
            comm.wait(*c_refs)

    vmem = pl.BlockSpec(memory_space=pltpu.VMEM)
    outs = pl.pallas_call(
        body, name=name,
        out_shape=[jax.ShapeDtypeStruct((8 * m, n), xs.dtype)] + ([] if comm is None else comm.out_shape),
        in_specs=[vmem] + [ANY] * n_c,
        out_specs=[vmem] + [ANY] * n_c,
        scratch_shapes=[pltpu.SemaphoreType.DMA((7,)), pltpu.SemaphoreType.DMA((7,)), pltpu.SemaphoreType.DMA]
        + ([] if comm is None else comm.scratch),
        compiler_params=pltpu.CompilerParams(vmem_limit_bytes=32 * 2 ** 20),
    )(xs, *([] if comm is None else comm.arrays))
    return outs[0], list(outs[1:])


def _pad8(a):
    return jnp.pad(a, ((0, SUBLANES - a.shape[0]), (0, 0)))


def _group_rows(wg):
    return wg.transpose(1, 0, 2, 3).reshape(4, GW, GW)


def _example_step(h0, tgt, mods, kvmod, a_scale, norm_g, kv_norm_g, final_g, b_rel_bias, sh, w_first, chip_arr):
    ones_e = jnp.ones((1, E), F32)
    shift = [mods[l:l + 1, 0:D] for l in range(4)]
    scale = [mods[l:l + 1, D:2 * D] for l in range(4)]
    gate = [mods[l:l + 1, 2 * D:3 * D] for l in range(4)]
    gl = [norm_g[l:l + 1] for l in range(4)]
    kv_shift, kv_scale = kvmod[None, 0:D], kvmod[None, D:2 * D]
    kv_g = kv_norm_g[None]

    w_a = w_first
    hs = [h0]
    saved = []
    nxt = [[sh["a_in"][1], sh["a_grp"][1], sh["a_out"][1], sh["kv"][0]],
           [sh["b_in"][0], sh["b_out"][0], sh["b_in"][1], sh["b_out"][1]]]
    for l in range(2):
        w_in_l, wg_l, wo_l = w_a
        wg_full = _group_rows(wg_l)
        (u, z, pooled, mixed, y, hn), got = _a_fwd(hs[-1], gl[l], shift[l], scale[l], a_scale[l:l + 1], gate[l], w_in_l,
                                                   wg_full, wo_l, f"a{l}_fwd", comm=_Comm(gathers=nxt[l]))
        saved.append((u, z, pooled, mixed, y, w_in_l, wg_full, wo_l))
        hs.append(hn)
        if l == 0:
            w_a, w_kv = got[:3], got[3]
        else:
            wb_in = [got[0], got[2]]
            wb_out = [got[1], got[3]]

    (uk, kp, vp), _ = _in_fwd(hs[2], kv_g, kv_shift, kv_scale, w_kv, BF16, BF16, "kv_in_fwd", pad_rows=PAD)

    for bi in range(2):
        l = 2 + bi
        sa, sb = _bias_build(jnp.pad(b_rel_bias[bi], ((0, 0), (0, NRELP - NREL))), f"b{bi}_bias")
        (u, q, z), _ = _in_fwd(hs[-1], gl[l], shift[l], scale[l], wb_in[bi], BF16, BF16, f"b{bi}_in_fwd")
        (att, probs), _ = _attn_fwd(q, kp, vp, sa.transpose(1, 0, 2), sb.transpose(1, 0, 2), f"b{bi}_attn_fwd")
        if bi == 0:
            (y, hn), _ = _out_fwd(att, z, wb_out[bi], gate[l], hs[-1], f"b{bi}_out_fwd")
            hs.append(hn)
        else:
            (y, dh, st_fin), _ = _out_fwd(att, z, wb_out[bi], gate[l], hs[-1], f"b{bi}_out_fwd", head=(final_g[None], tgt))
        saved.append((u, z, q, att, y, probs))

    st_in = [None] * 4
    st_out = [None] * 4
    grads = {}
    landed = {}

    def carry(names):
        return _Comm(scatters=[grads[n] for n in names]) if names else None

    def land(names, got):
        for n, a in zip(names, got):
            landed[n] = a

    u, z, q, att, y, probs = saved[3]
    (datt, dz, grads["b_out1"], st_out[3]), _ = _out_bwd(dh, y, gate[3], att, ones_e, z, wb_out[1], "b1_out_bwd")
    (dq, dk1, dv1, dsa, dsb), _ = _attn_bwd(q, kp, vp, probs, datt, None, "b1_attn_bwd")
    drb1 = _dbias_reduce(dsa.transpose(1, 0, 2), dsb.transpose(1, 0, 2), "b1_dbias")
    (dh, grads["b_in1"], st_in[3]), _ = _in_bwd(dq, dz, 0, u, hs[3], gl[3], scale[3], wb_in[1], dh, "b1_in_bwd")
    u, z, q, att, y, probs = saved[2]
    (datt, dz, grads["b_out0"], st_out[2]), _ = _out_bwd(dh, y, gate[2], att, ones_e, z, wb_out[0], "b0_out_bwd")
    (dq, dk, dv, dsa, dsb), got = _attn_bwd(q, kp, vp, probs, datt, (dk1, dv1), "b0_attn_bwd",
                                            comm=carry(["b_out1", "b_in1", "b_out0"]))
    land(["b_out1", "b_in1", "b_out0"], got)
    drb0 = _dbias_reduce(dsa.transpose(1, 0, 2), dsb.transpose(1, 0, 2), "b0_dbias")
    (dh, grads["b_in0"], st_in[2]), _ = _in_bwd(dq, dz, 0, u, hs[2], gl[2], scale[2], wb_in[0], dh, "b0_in_bwd")
    (dh, grads["kv"], st_kv), got = _in_bwd(dk, dv, PAD, uk, hs[2], kv_g, kv_scale, w_kv, dh, "kv_in_bwd",
                                            comm=carry(["b_in0"]))
    land(["b_in0"], got)
    st_pool = [None] * 2
    plan = {1: dict(o=[], p=[], i=["kv", "a_out1", "a_grp1"]), 0: dict(o=["a_in1"], p=["a_out0"], i=[])}
    early = ["b_out1", "b_in1", "b_out0", "b_in0", "kv", "a_out1", "a_grp1", "a_in1"]
    late = ["a_out0", "a_grp0", "a_in0"]
    both = {}

    def sum4(n):
        return _sum4(grads[n], landed[n], chip_arr, f"sum4_{n}")

    for l in (1, 0):
        u, z, pooled, mixed, y, w_in_l, wg_full, wo = saved[l]
        asl = a_scale[l:l + 1]
        (dms, dz, grads[f"a_out{l}"], st_out[l]), got = _out_bwd(dh, y, gate[l], mixed, asl, z, wo, f"a{l}_out_bwd",
                                                                comm=carry(plan[l]["o"]))
        land(plan[l]["o"], got)
        comm = carry(plan[l]["p"])
        if l == 0:
            mine = [sum4(n) for n in early]
            comm = _Comm(scatters=[grads[n] for n in plan[l]["p"]], swaps=mine)
        (dval, dwg, st_pool[l]), got = _pool_bwd(dms, mixed, pooled, wg_full, asl, f"a{l}_pool_bwd", comm=comm)
        land(plan[l]["p"], got)
        if l == 0:
            both.update({n: [a, b] for n, a, b in zip(early, mine, got[len(plan[l]["p"]):])})
        grads[f"a_grp{l}"] = (dwg.reshape(4, NCHIP, GW // NCHIP, GW).transpose(1, 0, 2, 3).reshape(NCHIP, GW, GW)
                              .astype(BF16))
        (dh, grads[f"a_in{l}"], st_in[l]), got = _in_bwd(dval, dz, 0, u, hs[l], gl[l], scale[l], w_in_l, dh, f"a{l}_in_bwd",
                                                         comm=carry(plan[l]["i"]))
        land(plan[l]["i"], got)
    pieces = st_in + [st_kv] + st_out + [st_fin]
    pieces += [_pad8(st_pool[l][0].reshape(2, D)) for l in range(2)]
    pieces += [_pad8(d.reshape(NH * NRELP // D, D)) for d in (drb0, drb1)]
    gathered, got = _allgather8(jnp.concatenate(pieces, axis=0), "gather_stats", comm=carry(["a_grp0", "a_in0"]))
    land(["a_grp0", "a_in0"], got)
    mine = [sum4(n) for n in late]
    both.update({n: [a, b] for n, a, b in zip(late, mine, _comm_only(_Comm(swaps=mine), "swap_last"))})
    return dh, both, gathered.reshape(8, N_STAT, D)


ROW_IN = [8 * l for l in range(4)]
ROW_KV = 32
ROW_OUT = [40 + 8 * l for l in range(4)]
ROW_FIN = 72
ROW_ASC = [80, 88]
ROW_RB = [96, 104]
N_STAT = 112


def kernel(x, c, ada_w, ada_b, norm_g, a_w_in, a_w_group, a_scale, a_w_out, kv_norm_g, kv_ada_w, kv_ada_b, w_kv, b_w_in, b_rel_bias, b_w_out, final_g, loss_target, m_ada_w, m_ada_b, m_norm_g, m_a_w_in, m_a_w_group, m_a_scale, m_a_w_out, m_kv_norm_g, m_kv_ada_w, m_kv_ada_b, m_w_kv, m_b_w_in, m_b_rel_bias, m_b_w_out, m_final_g, v_ada_w, v_ada_b, v_norm_g, v_a_w_in, v_a_w_group, v_a_scale, v_a_w_out, v_kv_norm_g, v_kv_ada_w, v_kv_ada_b, v_w_kv, v_b_w_in, v_b_rel_bias, v_b_w_out, v_final_g):
    xi, yi, ci = _place()
    chip = 2 * xi + yi
    dev = 4 * xi + 2 * yi + ci
    n_ada = ada_w.shape[2]
    n_kva = kv_ada_w.shape[1]
    n_asc = a_scale.shape[1]

    c_all = _allgather8(jnp.broadcast_to(c, (SUBLANES, D)), "gather_c")[0][::SUBLANES]
    ada_b_sh = lax.dynamic_slice_in_dim(ada_b, chip * n_ada, n_ada, axis=1)
    kvb_sh = lax.dynamic_slice_in_dim(kv_ada_b, chip * n_kva, n_kva, axis=0)
    c_act, mod_ada = _cmat(c_all, ada_w, ada_b_sh[:, None, :], "mod_ada")
    _, mod_kv = _cmat(c_all, kv_ada_w[None], kvb_sh[None, None, :], "mod_kv")
    part = jnp.concatenate([mod_ada.transpose(1, 0, 2).reshape(SUBLANES, 4 * n_ada), mod_kv[0],
                            jnp.broadcast_to(a_scale.reshape(1, 2 * n_asc), (SUBLANES, 2 * n_asc))], axis=1)
    sh = dict(a_in=[a_w_in[l].astype(BF16) for l in range(2)], a_grp=[a_w_group[l].astype(BF16) for l in range(2)],
              a_out=[a_w_out[l].astype(BF16) for l in range(2)], kv=[w_kv.astype(BF16)],
              b_in=[b_w_in[l].astype(BF16) for l in range(2)], b_out=[b_w_out[l].astype(BF16) for l in range(2)])
    gathered, w_first = _allgather8(part, "gather_mod", comm=_Comm(gathers=[sh["a_in"][0], sh["a_grp"][0], sh["a_out"][0]]))
    rows = jnp.concatenate([lax.dynamic_slice_in_dim(gathered, SUBLANES * (2 * p + ci) + dev, 1, axis=0)
                            for p in range(NCHIP)], axis=0)
    mods = jnp.stack([rows[:, l * n_ada:(l + 1) * n_ada].reshape(3 * D) for l in range(4)])
    kvmod = rows[:, 4 * n_ada:4 * n_ada + n_kva].reshape(2 * D)
    o_asc = 4 * n_ada + n_kva
    a_scale_full = jnp.stack([rows[:, o_asc + l * n_asc:o_asc + (l + 1) * n_asc].reshape(E) for l in range(2)])

    chip_arr = jnp.reshape(chip, (1,)).astype(jnp.int32)
    dh, both, g3 = _example_step(x[0], loss_target[0], mods, kvmod, a_scale_full, norm_g, kv_norm_g, final_g,
                                 b_rel_bias, sh, w_first, chip_arr)
    grad_x = dh[None]

    red, loss_tile = _stats_reduce(g3, ROW_FIN + 1, "stats_reduce")
    loss = loss_tile[0, 0]

    def cat(rows_):
        return jnp.concatenate(rows_, axis=-1)

    g_ada_b = jnp.stack([cat([red[ROW_IN[l]], red[ROW_IN[l] + 1], red[ROW_OUT[l]]]) for l in range(4)])
    g_norm_g = jnp.stack([red[ROW_IN[l] + 2] for l in range(4)])
    g_kv_norm_g = red[ROW_KV + 2]
    g_kv_ada_b = cat([red[ROW_KV], red[ROW_KV + 1]])
    g_final_g = red[ROW_FIN]
    g_asc_full = jnp.stack([red[ROW_ASC[l]:ROW_ASC[l] + 2].reshape(E) for l in range(2)])
    g_a_scale = lax.dynamic_slice_in_dim(g_asc_full, chip * n_asc, n_asc, axis=1)
    g_rel = jnp.stack([red[ROW_RB[bi]:ROW_RB[bi] + NH * NRELP // D].reshape(NH, NRELP)[:, :NREL] for bi in range(2)])

    dmod = jnp.stack([cat([g3[:, ROW_IN[l]], g3[:, ROW_IN[l] + 1], g3[:, ROW_OUT[l]]]) for l in range(4)])
    dmod_sh = lax.dynamic_slice_in_dim(dmod, chip * n_ada, n_ada, axis=2)
    dkv = cat([g3[:, ROW_KV], g3[:, ROW_KV + 1]])[None]
    dkv_sh = lax.dynamic_slice_in_dim(dkv, chip * n_kva, n_kva, axis=2)
    c_act_t = c_act.T
    g_ada_w = _grad_ada(c_act_t, dmod_sh, "grad_ada_w")
    g_kv_ada_w = _grad_ada(c_act_t, dkv_sh, "grad_kv_ada_w")

    def upd(w, m, v, g, name, shape3):
        g = g.reshape(shape3) if not isinstance(g, list) else g
        outs = _adamw(w.reshape(shape3), m.reshape(shape3), v.reshape(shape3), g, name)
        return [o.reshape(w.shape) for o in outs]

    def pair(name):
        return [both[name + "0"], both[name + "1"]]

    res = {}
    res["ada_w"] = upd(ada_w, m_ada_w, v_ada_w, g_ada_w, "adamw_ada_w", ada_w.shape)
    res["ada_b"] = upd(ada_b, m_ada_b, v_ada_b, g_ada_b, "adamw_ada_b", (1,) + ada_b.shape)
    res["norm_g"] = upd(norm_g, m_norm_g, v_norm_g, g_norm_g, "adamw_norm_g", (1,) + norm_g.shape)
    res["a_w_in"] = upd(a_w_in, m_a_w_in, v_a_w_in, pair("a_in"), "adamw_a_w_in", a_w_in.shape)
    res["a_w_group"] = upd(a_w_group, m_a_w_group, v_a_w_group, pair("a_grp"), "adamw_a_w_group", (2, GW, GW))
    res["a_scale"] = upd(a_scale, m_a_scale, v_a_scale, g_a_scale, "adamw_a_scale", (1,) + a_scale.shape)
    res["a_w_out"] = upd(a_w_out, m_a_w_out, v_a_w_out, pair("a_out"), "adamw_a_w_out", a_w_out.shape)
    res["kv_norm_g"] = upd(kv_norm_g, m_kv_norm_g, v_kv_norm_g, g_kv_norm_g, "adamw_kv_norm_g", (1, 1, D))
    res["kv_ada_w"] = upd(kv_ada_w, m_kv_ada_w, v_kv_ada_w, g_kv_ada_w, "adamw_kv_ada_w", (1,) + kv_ada_w.shape)
    res["kv_ada_b"] = upd(kv_ada_b, m_kv_ada_b, v_kv_ada_b, g_kv_ada_b, "adamw_kv_ada_b", (1, 1, 2 * D))
    res["w_kv"] = upd(w_kv, m_w_kv, v_w_kv, [both["kv"]], "adamw_w_kv", (1,) + w_kv.shape)
    res["b_w_in"] = upd(b_w_in, m_b_w_in, v_b_w_in, pair("b_in"), "adamw_b_w_in", b_w_in.shape)
    res["b_rel_bias"] = upd(b_rel_bias, m_b_rel_bias, v_b_rel_bias, g_rel, "adamw_b_rel_bias", (1, 2 * NH, NREL))
    res["b_w_out"] = upd(b_w_out, m_b_w_out, v_b_w_out, pair("b_out"), "adamw_b_w_out", b_w_out.shape)
    res["final_g"] = upd(final_g, m_final_g, v_final_g, g_final_g, "adamw_final_g", (1, 1, D))

    names = ["ada_w", "ada_b", "norm_g", "a_w_in", "a_w_group", "a_scale", "a_w_out", "kv_norm_g", "kv_ada_w", "kv_ada_b",
             "w_kv", "b_w_in", "b_rel_bias", "b_w_out", "final_g"]
    return (loss, grad_x, *[res[n][0] for n in names], *[res[n][1] for n in names], *[res[n][2] for n in names],
            *[res[n][3] for n in names])
```

```python
import math

import jax
import jax.numpy as jnp
from jax import lax
from jax.experimental import pallas as pl
from jax.experimental.pallas import tpu as pltpu

F32 = jnp.float32
BF16 = jnp.bfloat16

D = 1024
E = 2048
NH = 16
HD = 128
CHUNK = 64
LEFT = 8
PAD = LEFT * CHUNK
NREL = 257
NRELP = 384
REL_CLIP = 128
EPS = 1e-6
NEG = -1e30
LOG2E = math.log2(math.e)
SM_SCALE = HD ** -0.5
POOL_W = (2, 4, 8, 16)
GW = 512
HALO = 16
QC = 4
QB = QC * CHUNK
NMASK = PAD // QB
WIN = (QC + LEFT) * CHUNK
BW = (LEFT + 2) * CHUNK
DBW = 4 * CHUNK
NSUB = 8
NCHIP = 4
LANES = 128
SUBLANES = 8

ADAM_LR = 0.001
ADAM_B1 = 0.9
ADAM_B2 = 0.999
ADAM_EPS = 1e-08
ADAM_WD = 0.01
ADAM_STEP = 10

MESH = pl.DeviceIdType.MESH
ANY = pl.BlockSpec(memory_space=pl.ANY)


def _params(n_axes, vmem_mb):
    return pltpu.CompilerParams(dimension_semantics=("arbitrary",) * n_axes, vmem_limit_bytes=vmem_mb * 2 ** 20)


def _nn(a, b):
    return jnp.dot(a, b, preferred_element_type=F32)


def _nt(a, b):
    return lax.dot_general(a, b, (((1,), (1,)), ((), ())), preferred_element_type=F32)


def _tn(a, b):
    return lax.dot_general(a, b, (((0,), (0,)), ((), ())), preferred_element_type=F32)


def _row(n):
    return pl.BlockSpec((1, n), lambda i: (0, 0))


def _colsum(x):
    return jnp.sum(x, axis=0, keepdims=True)


def _place():
    return lax.axis_index("x"), lax.axis_index("y"), lax.axis_index("c")


class _Comm:
    def __init__(self, gathers=(), scatters=(), swaps=()):
        self.n_g = len(gathers)
        self.n_chip = len(gathers) + len(scatters)
        self.n_sw = len(swaps)
        self.arrays = list(gathers) + list(scatters) + list(swaps)
        self.n = len(self.arrays)
        self.half = [a.shape[0] // 2 for a in gathers]
        self.out_shape = ([jax.ShapeDtypeStruct((NCHIP,) + a.shape, a.dtype) for a in gathers]
                          + [jax.ShapeDtypeStruct((3,) + a.shape[1:], a.dtype) for a in scatters]
                          + [jax.ShapeDtypeStruct(a.shape, a.dtype) for a in swaps])
        n_c, n_f, n_s = max(3 * self.n_chip, 1), max(3 * self.n_g, 1), max(self.n_sw, 1)
        self.scratch = [pltpu.SemaphoreType.DMA((n_c,)), pltpu.SemaphoreType.DMA((n_c,)),
                        pltpu.SemaphoreType.DMA((max(self.n_g, 1),)), pltpu.SemaphoreType.DMA((n_f,)),
                        pltpu.SemaphoreType.DMA((n_f,)), pltpu.SemaphoreType.DMA((n_s,)), pltpu.SemaphoreType.DMA((n_s,))]

    def _chip_copies(self, ins, outs, send, recv, landing):
        x, y, c = _place()
        chips = [(1 - x, y), (x, 1 - y), (1 - x, 1 - y)]
        mine = 2 * x + y
        cps = []
        for k in range(self.n_chip):
            for j, (cx, cy) in enumerate(chips):
                q = 2 * cx + cy
                if k < self.n_g:
                    part = pl.ds(c * self.half[k], self.half[k])
                    src = ins[k].at[part]
                    dst = outs[k].at[q if landing else mine, part]
                else:
                    src = ins[k].at[q]
                    dst = outs[k].at[j]
                cps.append(pltpu.make_async_remote_copy(
                    src_ref=src, dst_ref=dst, send_sem=send.at[3 * k + j], recv_sem=recv.at[3 * k + j],
                    device_id=(cx, cy, c), device_id_type=MESH))
        return cps

    def _core_copies(self, outs, fsend, frecv, landing):
        x, y, c = _place()
        chips = [(1 - x, y), (x, 1 - y), (1 - x, 1 - y)]
        cps = []
        for k in range(self.n_g):
            for j, (cx, cy) in enumerate(chips):
                part = pl.ds((1 - c if landing else c) * self.half[k], self.half[k])
                blk = outs[k].at[2 * cx + cy, part]
                cps.append(pltpu.make_async_remote_copy(
                    src_ref=blk, dst_ref=blk, send_sem=fsend.at[3 * k + j], recv_sem=frecv.at[3 * k + j],
                    device_id=(x, y, 1 - c), device_id_type=MESH))
        return cps

    def _local_copies(self, ins, outs, loc):
        x, y, _ = _place()
        return [pltpu.make_async_copy(ins[k], outs[k].at[2 * x + y], loc.at[k]) for k in range(self.n_g)]

    def _swap_copies(self, ins, outs, ssend, srecv):
        x, y, c = _place()
        return [pltpu.make_async_remote_copy(
            src_ref=ins[k], dst_ref=outs[k], send_sem=ssend.at[k - self.n_chip], recv_sem=srecv.at[k - self.n_chip],
            device_id=(x, y, 1 - c), device_id_type=MESH) for k in range(self.n_chip, self.n)]

    def start(self, ins, outs, send, recv, loc, fsend, frecv, ssend, srecv):
        for cp in (self._local_copies(ins, outs, loc) + self._chip_copies(ins, outs, send, recv, False)
                   + self._swap_copies(ins, outs, ssend, srecv)):
            cp.start()

    def wait(self, ins, outs, send, recv, loc, fsend, frecv, ssend, srecv):
        lands = self._chip_copies(ins, outs, send, recv, True)
        passes = self._core_copies(outs, fsend, frecv, False)
        for k in range(self.n_chip):
            for j in range(3):
                lands[3 * k + j].wait_recv()
                if k < self.n_g:
                    passes[3 * k + j].start()
        for cp in self._core_copies(outs, fsend, frecv, True):
            cp.wait_recv()
        swaps = self._swap_copies(ins, outs, ssend, srecv)
        for cp in swaps:
            cp.wait_recv()
        for cp in self._chip_copies(ins, outs, send, recv, False) + passes + swaps:
            cp.wait_send()
        for cp in self._local_copies(ins, outs, loc):
            cp.wait()


def _call(body, name, grid, in_specs, out_specs, out_shape, scratch, params, args, comm=None):
    n_in, n_out, n_sc = len(in_specs), len(out_specs), len(scratch)
    if comm is None:
        outs = pl.pallas_call(body, name=name, grid=grid, in_specs=in_specs, out_specs=out_specs, out_shape=out_shape,
                              scratch_shapes=scratch, compiler_params=params)(*args)
        return list(outs), []
    n = comm.n
    o0 = n_in + n
    s0 = o0 + n_out + n

    def wrapped(*refs):
        c_refs = (refs[n_in:o0], refs[o0 + n_out:s0]) + tuple(refs[s0 + n_sc:])
        ids = [pl.program_id(a) for a in range(len(grid))]
        first = ids[0] == 0
        last = ids[0] == grid[0] - 1
        for a in range(1, len(grid)):
            first = first & (ids[a] == 0)
            last = last & (ids[a] == grid[a] - 1)

        @pl.when(first)
        def _():
            comm.start(*c_refs)

        body(*refs[:n_in], *refs[o0:o0 + n_out], *refs[s0:s0 + n_sc])

        @pl.when(last)
        def _():
            comm.wait(*c_refs)

    outs = pl.pallas_call(
        wrapped, name=name, grid=grid, in_specs=list(in_specs) + [ANY] * n, out_specs=list(out_specs) + [ANY] * n,
        out_shape=list(out_shape) + comm.out_shape, scratch_shapes=list(scratch) + comm.scratch, compiler_params=params,
    )(*args, *comm.arrays)
    return list(outs[:n_out]), list(outs[n_out:])


def _comm_only(comm, name):
    def body(*refs):
        c_refs = (refs[:comm.n], refs[comm.n:2 * comm.n]) + tuple(refs[2 * comm.n:])
        comm.start(*c_refs)
        comm.wait(*c_refs)

    return pl.pallas_call(body, name=name, in_specs=[ANY] * comm.n, out_specs=[ANY] * comm.n, out_shape=comm.out_shape,
                          scratch_shapes=comm.scratch)(*comm.arrays)


def _in_fwd(h, g, shift, scale, w, dt_a, dt_b, name, pad_rows=0, comm=None, tm=512):
    S = h.shape[0]
    n_pad = pad_rows // tm

    def body(h_ref, g_ref, sh_ref, sc_ref, w_hbm, u_ref, oa_ref, ob_ref, w_v, sem):
        i = pl.program_id(0)

        @pl.when(i == 0)
        def _():
            cp = pltpu.make_async_copy(w_hbm, w_v, sem)
            cp.start()
            cp.wait()

        hh = h_ref[...]
        r = lax.rsqrt(jnp.mean(hh * hh, axis=-1, keepdims=True) + EPS)
        u = (hh * r * g_ref[...]) * (1.0 + sc_ref[...]) + sh_ref[...]
        ub = u.astype(BF16)
        u_ref[...] = ub
        for q in range(NCHIP):
            o_ref = oa_ref if q < 2 else ob_ref
            o_ref[:, (q % 2) * D:(q % 2 + 1) * D] = _nn(ub, w_v[q]).astype(o_ref.dtype)

        if n_pad:
            @pl.when(i < n_pad)
            def _():
                oa_ref[...] = jnp.zeros(oa_ref.shape, oa_ref.dtype)
                ob_ref[...] = jnp.zeros(ob_ref.shape, ob_ref.dtype)

    def src(i):
        return (jnp.maximum(i - n_pad, 0), 0)

    outs, landed = _call(
        body, name, (S // tm + n_pad,),
        [pl.BlockSpec((tm, D), src), _row(D), _row(D), _row(D), ANY],
        [pl.BlockSpec((tm, D), src), pl.BlockSpec((tm, E), lambda i: (i, 0)), pl.BlockSpec((tm, E), lambda i: (i, 0))],
        [jax.ShapeDtypeStruct((S, D), BF16), jax.ShapeDtypeStruct((S + pad_rows, E), dt_a),
         jax.ShapeDtypeStruct((S + pad_rows, E), dt_b)],
        [pltpu.VMEM((NCHIP, D, D), BF16), pltpu.SemaphoreType.DMA],
        _params(1, 52), (h, g, shift, scale, w), comm)
    return outs, landed


def _a_fwd(h, g, shift, scale, asc, gate, w_in, wg, w_out, name, comm=None, tm=512):
    S = h.shape[0]

    def body(h_ref, g_ref, sh_ref, sc_ref, as_ref, gate_ref, wi_hbm, wg_hbm, wo_hbm,
             u_ref, z_ref, p_ref, m_ref, y_ref, ho_ref, wi_v, wg_v, wo_v, buf, sems):
        i = pl.program_id(0)

        @pl.when(i == 0)
        def _():
            cps = [pltpu.make_async_copy(wi_hbm, wi_v, sems.at[0]), pltpu.make_async_copy(wg_hbm, wg_v, sems.at[1]),
                   pltpu.make_async_copy(wo_hbm, wo_v, sems.at[2])]
            for cp in cps:
                cp.start()
            buf[0:HALO, :] = jnp.zeros((HALO, E), F32)
            for cp in cps:
                cp.wait()

        hh = h_ref[...]
        r = lax.rsqrt(jnp.mean(hh * hh, axis=-1, keepdims=True) + EPS)
        ub = ((hh * r * g_ref[...]) * (1.0 + sc_ref[...]) + sh_ref[...]).astype(BF16)
        u_ref[...] = ub
        for q in range(2):
            buf[HALO:HALO + tm, q * D:(q + 1) * D] = _nn(ub, wi_v[q])
        t = i * tm + lax.broadcasted_iota(jnp.int32, (tm, 1), 0)
        y = None
        for gi, w in enumerate(POOL_W):
            cols = slice(gi * GW, (gi + 1) * GW)
            x = buf[:, cols]
            s = x
            k = 1
            while k < w:
                s = s + pltpu.roll(s, k, 0)
                k *= 2
            inv_cnt = 1.0 / jnp.minimum(t + 1, w).astype(F32)
            pb = (s[HALO:, :] * inv_cnt - x[HALO:, :]).astype(BF16)
            p_ref[:, cols] = pb
            mb = _nn(pb, wg_v[gi]).astype(BF16)
            m_ref[:, cols] = mb
            zb = _nn(ub, wi_v[2 + gi // 2, :, (gi % 2) * GW:(gi % 2 + 1) * GW]).astype(BF16)
            z_ref[:, cols] = zb
            zz = zb.astype(F32)
            act = ((mb.astype(F32) * as_ref[:, cols]) * (zz * jax.nn.sigmoid(zz))).astype(BF16)
            part = _nn(act, wo_v[gi])
            y = part if y is None else y + part
        buf[0:HALO, :] = buf[tm:tm + HALO, :]
        y_ref[...] = y.astype(BF16)
        ho_ref[...] = hh + gate_ref[...] * y

    rows_d = pl.BlockSpec((tm, D), lambda i: (i, 0))
    rows_e = pl.BlockSpec((tm, E), lambda i: (i, 0))
    return _call(
        body, name, (S // tm,),
        [rows_d, _row(D), _row(D), _row(D), _row(E), _row(D), ANY, ANY, ANY],
        [rows_d, rows_e, rows_e, rows_e, rows_d, rows_d],
        [jax.ShapeDtypeStruct((S, D), BF16), jax.ShapeDtypeStruct((S, E), BF16), jax.ShapeDtypeStruct((S, E), BF16),
         jax.ShapeDtypeStruct((S, E), BF16), jax.ShapeDtypeStruct((S, D), BF16), jax.ShapeDtypeStruct((S, D), F32)],
        [pltpu.VMEM((NCHIP, D, D), BF16), pltpu.VMEM((4, GW, GW), BF16), pltpu.VMEM((NCHIP, GW, D), BF16),
         pltpu.VMEM((tm + HALO, E), F32), pltpu.SemaphoreType.DMA((3,))],
        _params(1, 60), (h, g, shift, scale, asc, gate, w_in, wg, w_out), comm)


def _out_fwd(a, z, w, gate, h, name, head=None, comm=None, tm=512):
    S = h.shape[0]
    kb = E // NCHIP
    n_in = 5 if head is None else 7

    def body(*refs):
        a_ref, z_ref, w_hbm, gate_ref, h_ref = refs[:5]
        w_v, sem = refs[-2:]
        i = pl.program_id(0)

        @pl.when(i == 0)
        def _():
            cp = pltpu.make_async_copy(w_hbm, w_v, sem)
            cp.start()
            cp.wait()

        y = None
        for p in range(NCHIP):
            cols = slice(p * kb, (p + 1) * kb)
            zz = z_ref[:, cols].astype(F32)
            act = (a_ref[:, cols].astype(F32) * (zz * jax.nn.sigmoid(zz))).astype(BF16)
            part = _nn(act, w_v[p])
            y = part if y is None else y + part
        refs[n_in][...] = y.astype(BF16)
        hh = h_ref[...] + gate_ref[...] * y
        if head is None:
            refs[n_in + 1][...] = hh
            return
        g_ref, t_ref = refs[5:7]
        dh_ref, st_ref = refs[n_in + 1:n_in + 3]

        @pl.when(i == 0)
        def _():
            st_ref[...] = jnp.zeros((SUBLANES, D), F32)

        r = lax.rsqrt(jnp.mean(hh * hh, axis=-1, keepdims=True) + EPS)
        xhat = hh * r
        diff = xhat * g_ref[...] - t_ref[...]
        st_ref[1:2, :] += _colsum(diff * diff)
        dout = diff * (1.0 / D)
        st_ref[0:1, :] += _colsum(dout * xhat)
        dx = dout * g_ref[...]
        dh_ref[...] = r * (dx - xhat * jnp.mean(dx * xhat, axis=-1, keepdims=True))

    rows_d = pl.BlockSpec((tm, D), lambda i: (i, 0))
    rows_e = pl.BlockSpec((tm, E), lambda i: (i, 0))
    in_specs = [rows_e, rows_e, ANY, _row(D), rows_d]
    out_specs = [rows_d, rows_d]
    out_shape = [jax.ShapeDtypeStruct((S, D), BF16), jax.ShapeDtypeStruct((S, D), F32)]
    args = (a, z, w, gate, h)
    if head is not None:
        in_specs += [_row(D), rows_d]
        out_specs += [pl.BlockSpec((SUBLANES, D), lambda i: (0, 0))]
        out_shape += [jax.ShapeDtypeStruct((SUBLANES, D), F32)]
        args += tuple(head)
    return _call(body, name, (S // tm,), in_specs, out_specs, out_shape,
                 [pltpu.VMEM((NCHIP, kb, D), BF16), pltpu.SemaphoreType.DMA], _params(1, 52), args, comm)


TW = BW + LANES


def _diag_onehot(transpose):
    shape = (TW, NRELP) if transpose else (NRELP, TW)
    j = lax.broadcasted_iota(jnp.int32, shape, 0 if transpose else 1)
    r = lax.broadcasted_iota(jnp.int32, shape, 1 if transpose else 0)
    idx = jnp.clip(PAD - (j - LANES), -REL_CLIP, REL_CLIP) + REL_CLIP
    return jnp.where(idx == r, 1.0, 0.0).astype(BF16)


def _strip_valid():
    m = lax.broadcasted_iota(jnp.int32, (NH, BW), 1)
    return m < (LEFT + 1) * CHUNK, m >= CHUNK


def _bias_build(rb, name):
    def body(rb_ref, a_ref, b_ref):
        x = rb_ref[...]
        hi = x.astype(BF16)
        r1 = x - hi.astype(F32)
        mid = r1.astype(BF16)
        lo = (r1 - mid.astype(F32)).astype(BF16)
        oh = _diag_onehot(False)
        diag = (_nn(hi, oh) + _nn(mid, oh)) + _nn(lo, oh)
        valid_a, valid_b = _strip_valid()
        for qi in range(CHUNK):
            a_ref[qi] = jnp.where(valid_a, pltpu.roll(diag, TW - (LANES - qi), 1)[:, :BW], NEG)
            b_ref[qi] = jnp.where(valid_b, pltpu.roll(diag, TW - (CHUNK - qi), 1)[:, :BW], NEG)

    vmem = pl.BlockSpec(memory_space=pltpu.VMEM)
    return pl.pallas_call(
        body, name=name, in_specs=[vmem], out_specs=[vmem, vmem],
        out_shape=[jax.ShapeDtypeStruct((CHUNK, NH, BW), F32), jax.ShapeDtypeStruct((CHUNK, NH, BW), F32)],
        compiler_params=pltpu.CompilerParams(vmem_limit_bytes=32 * 2 ** 20),
    )(rb)


def _dbias_reduce(dba, dbb, name):
    def body(a_ref, b_ref, o_ref):
        valid_a, valid_b = _strip_valid()
        zeros = jnp.zeros((NH, TW - BW), F32)
        acc = jnp.zeros((NH, TW), F32)
        for qi in range(CHUNK):
            xa = jnp.concatenate([jnp.where(valid_a, a_ref[qi], 0.0), zeros], axis=1)
            xb = jnp.concatenate([jnp.where(valid_b, b_ref[qi], 0.0), zeros], axis=1)
            acc = acc + (pltpu.roll(xa, LANES - qi, 1) + pltpu.roll(xb, CHUNK - qi, 1))
        oh = _diag_onehot(True)
        hi = acc.astype(BF16)
        mid = (acc - hi.astype(F32)).astype(BF16)
        r = lax.broadcasted_iota(jnp.int32, (NH, NRELP), 1)
        near = jnp.where(r < 2 * REL_CLIP, _nn(hi, oh) + _nn(mid, oh), 0.0)
        o_ref[...] = jnp.where(r == 2 * REL_CLIP, -jnp.sum(near, axis=-1, keepdims=True), near)

    vmem = pl.BlockSpec(memory_space=pltpu.VMEM)
    return pl.pallas_call(
        body, name=name, in_specs=[vmem, vmem], out_specs=vmem,
        out_shape=jax.ShapeDtypeStruct((NH, NRELP), F32),
        compiler_params=pltpu.CompilerParams(vmem_limit_bytes=32 * 2 ** 20),
    )(dba, dbb)


def _build_bias(bias3, ba_ref, bb_ref):
    bias3[NMASK] = jnp.full((QB, WIN), NEG, F32)
    for qc in range(QC):
        rows = slice(qc * CHUNK, (qc + 1) * CHUNK)
        if qc % 2 == 0:
            bias3[NMASK, rows, qc * CHUNK:qc * CHUNK + BW] = ba_ref[...] * LOG2E
        else:
            bias3[NMASK, rows, (qc - 1) * CHUNK:(qc - 1) * CHUNK + BW] = bb_ref[...] * LOG2E
    col = lax.broadcasted_iota(jnp.int32, (QB, WIN), 1)
    for sub in range(NMASK):
        bias3[sub] = jnp.where(col < PAD - sub * QB, NEG, bias3[NMASK])


def _nsub(S):
    n = min(NSUB, S // QB)
    assert S % (n * QB) == 0 and n >= NMASK
    return n


def _row0(i, sub, nsub):
    return pl.multiple_of((i * nsub + sub) * QB, QB)


def _scores(q_ref, k_ref, i, sub, nsub):
    return _nt(q_ref[sub * QB:(sub + 1) * QB, :], k_ref[pl.ds(_row0(i, sub, nsub), WIN), :])


def _exp_parts(s, bias3, i, sub):
    which = jnp.where(i == 0, sub, NMASK) if sub < NMASK else NMASK
    s = s * (SM_SCALE * LOG2E) + bias3[which]
    e = jnp.exp2(s - jnp.max(s, axis=-1, keepdims=True))
    return e, jnp.sum(e, axis=-1, keepdims=True)


def _attn_fwd(q, kp, vp, ba, bb, name, comm=None):
    S = q.shape[0]
    nsub = _nsub(S)
    R = nsub * QB

    def body(q_ref, k_ref, v_ref, ba_ref, bb_ref, o_ref, p_ref, bias3):
        i = pl.program_id(1)

        @pl.when(i == 0)
        def _():
            _build_bias(bias3, ba_ref, bb_ref)

        s_next = _scores(q_ref, k_ref, i, 0, nsub)
        for sub in range(nsub):
            s = s_next
            if sub + 1 < nsub:
                s_next = _scores(q_ref, k_ref, i, sub + 1, nsub)
            e, l = _exp_parts(s, bias3, i, sub)
            pb = (e * (1.0 / l)).astype(BF16)
            p_ref[sub] = pb
            o_ref[sub * QB:(sub + 1) * QB, :] = _nn(pb, v_ref[pl.ds(_row0(i, sub, nsub), WIN), :]).astype(BF16)

    return _call(
        body, name, (NH, S // R),
        [pl.BlockSpec((R, HD), lambda h, i: (i, h)), pl.BlockSpec((S + PAD, HD), lambda h, i: (0, h)),
         pl.BlockSpec((S + PAD, HD), lambda h, i: (0, h)), pl.BlockSpec((None, CHUNK, BW), lambda h, i: (h, 0, 0)),
         pl.BlockSpec((None, CHUNK, BW), lambda h, i: (h, 0, 0))],
        [pl.BlockSpec((R, HD), lambda h, i: (i, h)), pl.BlockSpec((None, nsub, QB, WIN), lambda h, i: (h, i, 0, 0))],
        [jax.ShapeDtypeStruct((S, E), BF16), jax.ShapeDtypeStruct((NH, S // QB, QB, WIN), BF16)],
        [pltpu.VMEM((NMASK + 1, QB, WIN), F32)],
        _params(2, 48), (q, kp, vp, ba, bb), comm)


def _store_grad(acc, stage, dw_hbm, sem):
    for q in range(NCHIP):
        stage[...] = acc[q].astype(BF16)
        cp = pltpu.make_async_copy(stage, dw_hbm.at[q], sem)
        cp.start()
        cp.wait()


def _out_bwd(dh, y, gate, a, cs, z, w, name, comm=None, tm=256):
    S = dh.shape[0]
    kb = E // NCHIP
    cb = 256
    n_t = S // tm

    def body(dh_ref, y_ref, gate_ref, a_ref, cs_ref, z_ref, w_hbm, da_ref, dz_ref, dw_hbm, st_ref, w_v, acc, stage, sem):
        i = pl.program_id(0)

        @pl.when(i == 0)
        def _():
            cp = pltpu.make_async_copy(w_hbm, w_v, sem)
            cp.start()
            acc[...] = jnp.zeros(acc.shape, F32)
            st_ref[...] = jnp.zeros((SUBLANES, D), F32)
            cp.wait()

        dhh = dh_ref[...]
        st_ref[0:1, :] += _colsum(dhh * y_ref[...].astype(F32))
        dy = (dhh * gate_ref[...]).astype(BF16)
        for blk in range(E // cb):
            p, r0 = divmod(blk * cb, kb)
            cols = slice(blk * cb, (blk + 1) * cb)
            zz = z_ref[:, cols].astype(F32)
            sig = jax.nn.sigmoid(zz)
            sz = zz * sig
            ae = a_ref[:, cols].astype(F32) * cs_ref[:, cols]
            acc[p, r0:r0 + cb, :] += _tn((ae * sz).astype(BF16), dy)
            dact = _nt(dy, w_v[p, r0:r0 + cb, :])
            da_ref[:, cols] = (dact * sz).astype(BF16)
            dz_ref[:, cols] = (dact * ae * (sig * (1.0 + zz * (1.0 - sig)))).astype(BF16)

        @pl.when(i == n_t - 1)
        def _():
            _store_grad(acc, stage, dw_hbm, sem)

    return _call(
        body, name, (n_t,),
        [pl.BlockSpec((tm, D), lambda i: (i, 0)), pl.BlockSpec((tm, D), lambda i: (i, 0)), _row(D),
         pl.BlockSpec((tm, E), lambda i: (i, 0)), _row(E), pl.BlockSpec((tm, E), lambda i: (i, 0)), ANY],
        [pl.BlockSpec((tm, E), lambda i: (i, 0)), pl.BlockSpec((tm, E), lambda i: (i, 0)), ANY,
         pl.BlockSpec((SUBLANES, D), lambda i: (0, 0))],
        [jax.ShapeDtypeStruct((S, E), BF16), jax.ShapeDtypeStruct((S, E), BF16),
         jax.ShapeDtypeStruct((NCHIP, kb, D), BF16), jax.ShapeDtypeStruct((SUBLANES, D), F32)],
        [pltpu.VMEM((NCHIP, kb, D), BF16), pltpu.VMEM((NCHIP, kb, D), F32), pltpu.VMEM((kb, D), BF16),
         pltpu.SemaphoreType.DMA],
        _params(1, 52), (dh, y, gate, a, cs, z, w), comm)


def _attn_bwd(q, kp, vp, probs, do, prev, name, comm=None):
    S = q.shape[0]
    nsub = _nsub(S)
    R = nsub * QB
    n_i = S // R
    dt_kv = F32 if prev is None else BF16

    def body(*refs):
        q_ref, k_ref, v_ref, p_ref, do_ref = refs[:5]
        refs = refs[5:]
        if prev is not None:
            pk_hbm, pv_hbm = refs[:2]
            refs = refs[2:]
        dq_ref, dk_ref, dv_ref, dba_ref, dbb_ref, dbias, dk_acc, dv_acc = refs[:8]
        if prev is not None:
            pk_v, pv_v, sems = refs[8:]
        h = pl.program_id(0)
        i = pl.program_id(1)

        def prev_copies():
            cols = pl.ds(pl.multiple_of(h * HD, HD), HD)
            return (pltpu.make_async_copy(pk_hbm.at[:, cols], pk_v, sems.at[0]),
                    pltpu.make_async_copy(pv_hbm.at[:, cols], pv_v, sems.at[1]))

        @pl.when(i == 0)
        def _():
            if prev is not None:
                for cp in prev_copies():
                    cp.start()
            dbias[...] = jnp.zeros((2, CHUNK, DBW), F32)
            dk_acc[...] = jnp.zeros((S + PAD, HD), F32)
            dv_acc[...] = jnp.zeros((S + PAD, HD), F32)

        def mxu_in(sub):
            return _nt(do_ref[sub * QB:(sub + 1) * QB, :], v_ref[pl.ds(_row0(i, sub, nsub), WIN), :])

        nxt = mxu_in(0)
        for sub in range(nsub):
            rows = slice(sub * QB, (sub + 1) * QB)
            win = pl.ds(_row0(i, sub, nsub), WIN)
            dp = nxt
            if sub + 1 < nsub:
                nxt = mxu_in(sub + 1)
            pb = p_ref[sub]
            p = pb.astype(F32)
            ds = p * (dp - jnp.sum(p * dp, axis=-1, keepdims=True))
            for par in range(2):
                part = None
                for qc in range(par, QC, 2):
                    c0 = (qc - par) * CHUNK + BW - DBW
                    blk_ = ds[qc * CHUNK:(qc + 1) * CHUNK, c0:c0 + DBW]
                    part = blk_ if part is None else part + blk_
                dbias[par] += part
            dsb = (ds * SM_SCALE).astype(BF16)
            dq_ref[rows, :] = _nn(dsb, k_ref[win, :]).astype(BF16)
            dk_acc[win, :] += _tn(dsb, q_ref[rows, :])
            dv_acc[win, :] += _tn(pb, do_ref[rows, :])

        @pl.when(i == n_i - 1)
        def _():
            zeros = jnp.zeros((CHUNK, BW - DBW), F32)
            dba_ref[...] = jnp.concatenate([zeros, dbias[0]], axis=1)
            dbb_ref[...] = jnp.concatenate([zeros, dbias[1]], axis=1)
            if prev is None:
                dk_ref[...] = dk_acc[...]
                dv_ref[...] = dv_acc[...]
            else:
                for cp in prev_copies():
                    cp.wait()
                dk_ref[...] = (dk_acc[...] + pk_v[...]).astype(BF16)
                dv_ref[...] = (dv_acc[...] + pv_v[...]).astype(BF16)

    head = pl.BlockSpec((S + PAD, HD), lambda h, i: (0, h))
    strip = pl.BlockSpec((None, CHUNK, BW), lambda h, i: (h, 0, 0))
    blk = pl.BlockSpec((R, HD), lambda h, i: (i, h))
    in_specs = [blk, head, head, pl.BlockSpec((None, nsub, QB, WIN), lambda h, i: (h, i, 0, 0)), blk]
    scratch = [pltpu.VMEM((2, CHUNK, DBW), F32), pltpu.VMEM((S + PAD, HD), F32), pltpu.VMEM((S + PAD, HD), F32)]
    args = (q, kp, vp, probs, do)
    if prev is not None:
        in_specs += [ANY, ANY]
        scratch += [pltpu.VMEM((S + PAD, HD), F32), pltpu.VMEM((S + PAD, HD), F32), pltpu.SemaphoreType.DMA((2,))]
        args += tuple(prev)
    return _call(
        body, name, (NH, n_i), in_specs, [blk, head, head, strip, strip],
        [jax.ShapeDtypeStruct((S, E), BF16), jax.ShapeDtypeStruct((S + PAD, E), dt_kv),
         jax.ShapeDtypeStruct((S + PAD, E), dt_kv), jax.ShapeDtypeStruct((NH, CHUNK, BW), F32),
         jax.ShapeDtypeStruct((NH, CHUNK, BW), F32)],
        scratch, _params(2, 56), args, comm)


def _pool_bwd(dms, mixed, pooled, wg, a_scale, name, comm=None, tm=512):
    S = dms.shape[0]
    n_t = S // tm

    def rev(i):
        return (n_t - 1 - i, 0)

    def body(d_ref, m_ref, p_ref, wg_ref, as_ref, dv_ref, dwg_ref, st_ref, buf):
        i = pl.program_id(0)

        @pl.when(i == 0)
        def _():
            buf[tm:tm + HALO, :] = jnp.zeros((HALO, E), F32)
            dwg_ref[...] = jnp.zeros((4, GW, GW), F32)
            st_ref[...] = jnp.zeros((SUBLANES, E), F32)

        t = (n_t - 1 - i) * tm + lax.broadcasted_iota(jnp.int32, (tm, 1), 0)
        st_ref[0:1, :] += _colsum(d_ref[...].astype(F32) * m_ref[...].astype(F32))
        for gi, w in enumerate(POOL_W):
            cols = slice(gi * GW, (gi + 1) * GW)
            dm = (d_ref[:, cols].astype(F32) * as_ref[:, cols]).astype(BF16)
            dpool = _nt(dm, wg_ref[gi])
            dwg_ref[gi] += _tn(p_ref[:, cols], dm)
            inv_cnt = 1.0 / jnp.minimum(t + 1, w).astype(F32)
            buf[0:tm, cols] = dpool * inv_cnt
            s = buf[:, cols]
            k = 1
            while k < w:
                s = s + pltpu.roll(s, tm + HALO - k, 0)
                k *= 2
            dv_ref[:, cols] = (s[0:tm, :] - dpool).astype(BF16)
        buf[tm:tm + HALO, :] = buf[0:HALO, :]

    return _call(
        body, name, (n_t,),
        [pl.BlockSpec((tm, E), rev), pl.BlockSpec((tm, E), rev), pl.BlockSpec((tm, E), rev),
         pl.BlockSpec((4, GW, GW), lambda i: (0, 0, 0)), _row(E)],
        [pl.BlockSpec((tm, E), rev), pl.BlockSpec((4, GW, GW), lambda i: (0, 0, 0)),
         pl.BlockSpec((SUBLANES, E), lambda i: (0, 0))],
        [jax.ShapeDtypeStruct((S, E), BF16), jax.ShapeDtypeStruct((4, GW, GW), F32),
         jax.ShapeDtypeStruct((SUBLANES, E), F32)],
        [pltpu.VMEM((tm + HALO, E), F32)],
        _params(1, 52), (dms, mixed, pooled, wg, a_scale), comm)


def _in_bwd(da, db, row_off, u, h, g, scale, w, dh_out, name, comm=None, tm=256):
    S = h.shape[0]
    n_t = S // tm
    off = row_off // tm

    def body(da_ref, db_ref, u_ref, h_ref, g_ref, sc_ref, w_hbm, dho_ref, dhi_ref, dw_hbm, st_ref, w_v, acc, stage, sem):
        i = pl.program_id(0)

        @pl.when(i == 0)
        def _():
            cp = pltpu.make_async_copy(w_hbm, w_v, sem)
            cp.start()
            acc[...] = jnp.zeros(acc.shape, F32)
            st_ref[...] = jnp.zeros((SUBLANES, D), F32)
            cp.wait()

        ub = u_ref[...]
        du = None
        for q in range(NCHIP):
            d_ref = da_ref if q < 2 else db_ref
            dv = d_ref[:, (q % 2) * D:(q % 2 + 1) * D]
            acc[q] += _tn(ub, dv)
            part = _nt(dv, w_v[q])
            du = part if du is None else du + part

        hh = h_ref[...]
        r = lax.rsqrt(jnp.mean(hh * hh, axis=-1, keepdims=True) + EPS)
        xhat = hh * r
        gg = g_ref[...]
        st_ref[0:1, :] += _colsum(du)
        st_ref[1:2, :] += _colsum(du * (xhat * gg))
        dn = du * (1.0 + sc_ref[...])
        st_ref[2:3, :] += _colsum(dn * xhat)
        dx = dn * gg
        dhi_ref[...] = dho_ref[...] + r * (dx - xhat * jnp.mean(dx * xhat, axis=-1, keepdims=True))

        @pl.when(i == n_t - 1)
        def _():
            _store_grad(acc, stage, dw_hbm, sem)

    part_spec = pl.BlockSpec((tm, E), lambda i: (i + off, 0))
    return _call(
        body, name, (n_t,),
        [part_spec, part_spec, pl.BlockSpec((tm, D), lambda i: (i, 0)), pl.BlockSpec((tm, D), lambda i: (i, 0)),
         _row(D), _row(D), ANY, pl.BlockSpec((tm, D), lambda i: (i, 0))],
        [pl.BlockSpec((tm, D), lambda i: (i, 0)), ANY, pl.BlockSpec((SUBLANES, D), lambda i: (0, 0))],
        [jax.ShapeDtypeStruct((S, D), F32), jax.ShapeDtypeStruct((NCHIP, D, D), BF16),
         jax.ShapeDtypeStruct((SUBLANES, D), F32)],
        [pltpu.VMEM((NCHIP, D, D), BF16), pltpu.VMEM((NCHIP, D, D), F32), pltpu.VMEM((D, D), BF16),
         pltpu.SemaphoreType.DMA],
        _params(1, 56), (da, db, u, h, g, scale, w, dh_out), comm)


def _grad_ada(c_act_t, dmod, name):
    L, _, n = dmod.shape

    def body(c_ref, d_ref, o_ref):
        acc = None
        for b in range(SUBLANES):
            part = c_ref[:, b:b + 1] * d_ref[b:b + 1, :]
            acc = part if acc is None else acc + part
        o_ref[...] = acc

    return pl.pallas_call(
        body, name=name, grid=(L,),
        in_specs=[pl.BlockSpec((D, SUBLANES), lambda l: (0, 0)), pl.BlockSpec((None, SUBLANES, n), lambda l: (l, 0, 0))],
        out_specs=pl.BlockSpec((None, D, n), lambda l: (l, 0, 0)),
        out_shape=jax.ShapeDtypeStruct((L, D, n), F32),
        compiler_params=_params(1, 32),
    )(c_act_t, dmod)


def _stats_reduce(g3, loss_row, name):
    n_dev, rows, _ = g3.shape

    def body(g_ref, o_ref, l_ref):
        acc = g_ref[0]
        for d in range(1, n_dev):
            acc = acc + g_ref[d]
        o_ref[...] = acc
        tot = jnp.sum(o_ref[loss_row:loss_row + 1, :], axis=-1, keepdims=True)
        l_ref[...] = jnp.broadcast_to(tot * (0.5 / D), (SUBLANES, LANES))

    return pl.pallas_call(
        body, name=name,
        in_specs=[pl.BlockSpec(memory_space=pltpu.VMEM)],
        out_specs=[pl.BlockSpec(memory_space=pltpu.VMEM), pl.BlockSpec(memory_space=pltpu.VMEM)],
        out_shape=[jax.ShapeDtypeStruct((rows, D), F32), jax.ShapeDtypeStruct((SUBLANES, LANES), F32)],
        compiler_params=pltpu.CompilerParams(vmem_limit_bytes=32 * 2 ** 20),
    )(g3)


def _sum4(own, land, chip, name, tr=256):
    _, R, C = own.shape
    tr = min(tr, R)

    def body(p_ref, own_ref, land_ref, o_ref):
        o_ref[...] = ((own_ref[...].astype(F32) + land_ref[0].astype(F32)) + land_ref[1].astype(F32)) + land_ref[2].astype(F32)

    out = pl.pallas_call(
        body, name=name,
        grid_spec=pltpu.PrefetchScalarGridSpec(
            num_scalar_prefetch=1, grid=(R // tr,),
            in_specs=[pl.BlockSpec((None, tr, C), lambda i, p: (p[0], i, 0)), pl.BlockSpec((3, tr, C), lambda i, p: (0, i, 0))],
            out_specs=pl.BlockSpec((tr, C), lambda i, p: (i, 0))),
        out_shape=jax.ShapeDtypeStruct((R, C), F32),
        compiler_params=_params(1, 32),
    )(chip, pltpu.with_memory_space_constraint(own, pltpu.HBM), pltpu.with_memory_space_constraint(land, pltpu.HBM))
    return pltpu.with_memory_space_constraint(out, pltpu.HBM)


def _adamw(w, m, v, g, name, tr=256):
    L, R, C = w.shape
    tr = min(tr, R)
    stacked = not isinstance(g, (list, tuple))
    n_g = None if stacked else [len(ps) for ps in g]
    flat = [g] if stacked else [a for ps in g for a in ps]

    def body(*refs):
        w_ref, m_ref, v_ref = refs[:3]
        g_refs = refs[3:3 + len(flat)]
        go_ref, d_ref, mo_ref, vo_ref = refs[3 + len(flat):]
        if stacked:
            gg = g_refs[0][...]
        else:
            layer = pl.program_id(0)
            gg = None
            k = 0
            for li in range(L):
                gl = None
                for _ in range(n_g[li]):
                    x = g_refs[k][...]
                    gl = x if gl is None else gl + x
                    k += 1
                gg = gl if gg is None else jnp.where(layer == li, gl, gg)
        m2 = ADAM_B1 * m_ref[...] + (1.0 - ADAM_B1) * gg
        v2 = ADAM_B2 * v_ref[...] + (1.0 - ADAM_B2) * (gg * gg)
        m_hat = m2 / (1.0 - ADAM_B1 ** ADAM_STEP)
        v_hat = v2 / (1.0 - ADAM_B2 ** ADAM_STEP)
        go_ref[...] = gg
        d_ref[...] = -ADAM_LR * (m_hat / (jnp.sqrt(v_hat) + ADAM_EPS) + ADAM_WD * w_ref[...])
        mo_ref[...] = m2
        vo_ref[...] = v2

    big = pl.BlockSpec((None, tr, C), lambda l, i: (l, i, 0))
    g_specs = [big] if stacked else [pl.BlockSpec((tr, C), lambda l, i: (i, 0))] * len(flat)
    return pl.pallas_call(
        body, name=name, grid=(L, R // tr),
        in_specs=[big, big, big] + g_specs,
        out_specs=[big, big, big, big],
        out_shape=[jax.ShapeDtypeStruct((L, R, C), F32)] * 4,
        compiler_params=_params(2, 48),
    )(w, m, v, *flat)


GATHER8_SEMS = [pltpu.SemaphoreType.DMA((7,)), pltpu.SemaphoreType.DMA((7,)), pltpu.SemaphoreType.DMA]


def _gather8(x_ref, out_ref, send_sems, recv_sems, local_sem):
    m = x_ref.shape[0]
    x, y, c = _place()
    me, sibling = (x, y, c), (x, y, 1 - c)
    chips = [(1 - x, y), (x, 1 - y), (1 - x, 1 - y)]

    def rows(px, py, pc):
        return out_ref.at[pl.ds((4 * px + 2 * py + pc) * m, m), :]

    def copy(k, block, to, src=None):
        return pltpu.make_async_remote_copy(
            src_ref=rows(*block) if src is None else src, dst_ref=rows(*block),
            send_sem=send_sems.at[k], recv_sem=recv_sems.at[k], device_id=to, device_id_type=MESH)

    mine = pltpu.make_async_copy(x_ref, rows(*me), local_sem)
    mine.start()
    first = [copy(0, me, sibling, src=x_ref)]
    first += [copy(1 + j, me, (*chip, c), src=x_ref) for j, chip in enumerate(chips)]
    for cp in first:
        cp.start()
    passed = [copy(4 + j, (*chip, c), sibling) for j, chip in enumerate(chips)]
    for j, chip in enumerate(chips):
        copy(1 + j, (*chip, c), me).wait_recv()
        passed[j].start()
    copy(0, sibling, me).wait_recv()
    for j, chip in enumerate(chips):
        copy(4 + j, (*chip, 1 - c), me).wait_recv()
    for cp in first + passed:
        cp.wait_send()
    mine.wait()


def _allgather8(xs, name, comm=None):
    m, n = xs.shape
    n_c = 0 if comm is None else comm.n

    def body(*refs):
        x_ref, out_ref = refs[0], refs[1 + n_c]
        c_refs = (refs[1:1 + n_c], refs[2 + n_c:2 + 2 * n_c]) + tuple(refs[5 + 2 * n_c:])
        if comm is not None:
            comm.start(*c_refs)
        _gather8(x_ref, out_ref, *refs[2 + 2 * n_c:5 + 2 * n_c])
        if comm is not None:
            comm.wait(*c_refs)

    vmem = pl.BlockSpec(memory_space=pltpu.VMEM)
    outs = pl.pallas_call(
        body, name=name,
        out_shape=[jax.ShapeDtypeStruct((8 * m, n), xs.dtype)] + ([] if comm is None else comm.out_shape),
        in_specs=[vmem] + [ANY] * n_c,
        out_specs=[vmem] + [ANY] * n_c,
        scratch_shapes=GATHER8_SEMS + ([] if comm is None else comm.scratch),
        compiler_params=pltpu.CompilerParams(vmem_limit_bytes=32 * 2 ** 20),
    )(xs, *([] if comm is None else comm.arrays))
    return outs[0], list(outs[1:])


def _prologue(c8, ada_w, ada_b, kv_ada_w, kv_ada_b, extra, comm, name):
    L, _, n = ada_w.shape
    k = kv_ada_w.shape[1]
    e = extra.shape[1]
    width = L * n + k + e
    n_c = comm.n

    def body(*refs):
        c_ref, w_hbm, b_ref, kw_hbm, kb_ref, x_ref = refs[:6]
        c_in = refs[6:6 + n_c]
        ca_ref, out_ref = refs[6 + n_c:8 + n_c]
        c_out = refs[8 + n_c:8 + 2 * n_c]
        cbuf, wbuf, kbuf, part, wsems = refs[8 + 2 * n_c:13 + 2 * n_c]
        sems_a = refs[13 + 2 * n_c:16 + 2 * n_c]
        sems_b = refs[16 + 2 * n_c:19 + 2 * n_c]
        c_refs = (c_in, c_out) + tuple(refs[19 + 2 * n_c:])
        comm.start(*c_refs)

        def fetch(l):
            return pltpu.make_async_copy(w_hbm.at[l], wbuf.at[l % 2], wsems.at[l % 2])

        fetch(0).start()
        kv_copy = pltpu.make_async_copy(kw_hbm, kbuf, wsems.at[2])
        kv_copy.start()
        _gather8(c_ref, cbuf, *sems_a)
        cc = jnp.concatenate([cbuf[SUBLANES * d:SUBLANES * d + 1, :] for d in range(8)], axis=0)
        ca = cc * jax.nn.sigmoid(cc)
        ca_ref[...] = ca
        cab = ca.astype(BF16)
        for l in range(L):
            fetch(l).wait()
            if l + 1 < L:
                fetch(l + 1).start()
            part[:, l * n:(l + 1) * n] = _nn(cab, wbuf[l % 2].astype(BF16)) + b_ref[l]
        kv_copy.wait()
        part[:, L * n:L * n + k] = _nn(cab, kbuf[...].astype(BF16)) + kb_ref[...]
        part[:, L * n + k:] = jnp.broadcast_to(x_ref[...], (SUBLANES, e))
        _gather8(part, out_ref, *sems_b)
        comm.wait(*c_refs)

    vmem = pl.BlockSpec(memory_space=pltpu.VMEM)
    outs = pl.pallas_call(
        body, name=name,
        out_shape=[jax.ShapeDtypeStruct((SUBLANES, D), F32), jax.ShapeDtypeStruct((8 * SUBLANES, width), F32)] + comm.out_shape,
        in_specs=[vmem, ANY, vmem, ANY, vmem, vmem] + [ANY] * n_c,
        out_specs=[vmem, vmem] + [ANY] * n_c,
        scratch_shapes=[pltpu.VMEM((8 * SUBLANES, D), F32), pltpu.VMEM((2, D, n), F32), pltpu.VMEM((D, k), F32),
                        pltpu.VMEM((SUBLANES, width), F32), pltpu.SemaphoreType.DMA((3,))] + GATHER8_SEMS + GATHER8_SEMS
        + comm.scratch,
        compiler_params=pltpu.CompilerParams(vmem_limit_bytes=32 * 2 ** 20),
    )(c8, ada_w, ada_b, kv_ada_w, kv_ada_b, extra, *comm.arrays)
    return outs[0], outs[1], list(outs[2:])


def _pad8(a):
    return jnp.pad(a, ((0, SUBLANES - a.shape[0]), (0, 0)))


def _group_rows(wg):
    return wg.transpose(1, 0, 2, 3).reshape(4, GW, GW)


def _example_step(h0, tgt, mods, kvmod, a_scale, norm_g, kv_norm_g, final_g, b_rel_bias, sh, w_first, chip_arr):
    ones_e = jnp.ones((1, E), F32)
    shift = [mods[l:l + 1, 0:D] for l in range(4)]
    scale = [mods[l:l + 1, D:2 * D] for l in range(4)]
    gate = [mods[l:l + 1, 2 * D:3 * D] for l in range(4)]
    gl = [norm_g[l:l + 1] for l in range(4)]
    kv_shift, kv_scale = kvmod[None, 0:D], kvmod[None, D:2 * D]
    kv_g = kv_norm_g[None]

    w_a = w_first
    hs = [h0]
    saved = []
    nxt = [[sh["a_in"][1], sh["a_grp"][1], sh["a_out"][1], sh["kv"][0]],
           [sh["b_in"][0], sh["b_out"][0], sh["b_in"][1], sh["b_out"][1]]]
    for l in range(2):
        w_in_l, wg_l, wo_l = w_a
        wg_full = _group_rows(wg_l)
        (u, z, pooled, mixed, y, hn), got = _a_fwd(hs[-1], gl[l], shift[l], scale[l], a_scale[l:l + 1], gate[l], w_in_l,
                                                   wg_full, wo_l, f"a{l}_fwd", comm=_Comm(gathers=nxt[l]))
        saved.append((u, z, pooled, mixed, y, w_in_l, wg_full, wo_l))
        hs.append(hn)
        if l == 0:
            w_a, w_kv = got[:3], got[3]
        else:
            wb_in = [got[0], got[2]]
            wb_out = [got[1], got[3]]

    (uk, kp, vp), _ = _in_fwd(hs[2], kv_g, kv_shift, kv_scale, w_kv, BF16, BF16, "kv_in_fwd", pad_rows=PAD)

    for bi in range(2):
        l = 2 + bi
        sa, sb = _bias_build(jnp.pad(b_rel_bias[bi], ((0, 0), (0, NRELP - NREL))), f"b{bi}_bias")
        (u, q, z), _ = _in_fwd(hs[-1], gl[l], shift[l], scale[l], wb_in[bi], BF16, BF16, f"b{bi}_in_fwd")
        (att, probs), _ = _attn_fwd(q, kp, vp, sa.transpose(1, 0, 2), sb.transpose(1, 0, 2), f"b{bi}_attn_fwd")
        if bi == 0:
            (y, hn), _ = _out_fwd(att, z, wb_out[bi], gate[l], hs[-1], f"b{bi}_out_fwd")
            hs.append(hn)
        else:
            (y, dh, st_fin), _ = _out_fwd(att, z, wb_out[bi], gate[l], hs[-1], f"b{bi}_out_fwd", head=(final_g[None], tgt))
        saved.append((u, z, q, att, y, probs))

    st_in = [None] * 4
    st_out = [None] * 4
    grads = {}
    landed = {}

    def carry(names):
        return _Comm(scatters=[grads[n] for n in names]) if names else None

    def land(names, got):
        for n, a in zip(names, got):
            landed[n] = a

    u, z, q, att, y, probs = saved[3]
    (datt, dz, grads["b_out1"], st_out[3]), _ = _out_bwd(dh, y, gate[3], att, ones_e, z, wb_out[1], "b1_out_bwd")
    (dq, dk1, dv1, dsa, dsb), _ = _attn_bwd(q, kp, vp, probs, datt, None, "b1_attn_bwd")
    drb1 = _dbias_reduce(dsa.transpose(1, 0, 2), dsb.transpose(1, 0, 2), "b1_dbias")
    (dh, grads["b_in1"], st_in[3]), _ = _in_bwd(dq, dz, 0, u, hs[3], gl[3], scale[3], wb_in[1], dh, "b1_in_bwd")
    u, z, q, att, y, probs = saved[2]
    (datt, dz, grads["b_out0"], st_out[2]), _ = _out_bwd(dh, y, gate[2], att, ones_e, z, wb_out[0], "b0_out_bwd")
    (dq, dk, dv, dsa, dsb), got = _attn_bwd(q, kp, vp, probs, datt, (dk1, dv1), "b0_attn_bwd",
                                            comm=carry(["b_out1", "b_in1", "b_out0"]))
    land(["b_out1", "b_in1", "b_out0"], got)
    drb0 = _dbias_reduce(dsa.transpose(1, 0, 2), dsb.transpose(1, 0, 2), "b0_dbias")
    (dh, grads["b_in0"], st_in[2]), _ = _in_bwd(dq, dz, 0, u, hs[2], gl[2], scale[2], wb_in[0], dh, "b0_in_bwd")
    (dh, grads["kv"], st_kv), got = _in_bwd(dk, dv, PAD, uk, hs[2], kv_g, kv_scale, w_kv, dh, "kv_in_bwd",
                                            comm=carry(["b_in0"]))
    land(["b_in0"], got)
    st_pool = [None] * 2
    plan = {1: dict(o=[], p=[], i=["kv", "a_out1", "a_grp1"]), 0: dict(o=["a_in1"], p=["a_out0"], i=[])}
    early = ["b_out1", "b_in1", "b_out0", "b_in0", "kv", "a_out1", "a_grp1", "a_in1"]
    late = ["a_out0", "a_grp0", "a_in0"]
    both = {}

    def sum4(n):
        return _sum4(grads[n], landed[n], chip_arr, f"sum4_{n}")

    for l in (1, 0):
        u, z, pooled, mixed, y, w_in_l, wg_full, wo = saved[l]
        asl = a_scale[l:l + 1]
        (dms, dz, grads[f"a_out{l}"], st_out[l]), got = _out_bwd(dh, y, gate[l], mixed, asl, z, wo, f"a{l}_out_bwd",
                                                                comm=carry(plan[l]["o"]))
        land(plan[l]["o"], got)
        comm = carry(plan[l]["p"])
        if l == 0:
            mine = [sum4(n) for n in early]
            comm = _Comm(scatters=[grads[n] for n in plan[l]["p"]], swaps=mine)
        (dval, dwg, st_pool[l]), got = _pool_bwd(dms, mixed, pooled, wg_full, asl, f"a{l}_pool_bwd", comm=comm)
        land(plan[l]["p"], got)
        if l == 0:
            both.update({n: [a, b] for n, a, b in zip(early, mine, got[len(plan[l]["p"]):])})
        grads[f"a_grp{l}"] = (dwg.reshape(4, NCHIP, GW // NCHIP, GW).transpose(1, 0, 2, 3).reshape(NCHIP, GW, GW)
                              .astype(BF16))
        (dh, grads[f"a_in{l}"], st_in[l]), got = _in_bwd(dval, dz, 0, u, hs[l], gl[l], scale[l], w_in_l, dh, f"a{l}_in_bwd",
                                                         comm=carry(plan[l]["i"]))
        land(plan[l]["i"], got)
    pieces = st_in + [st_kv] + st_out + [st_fin]
    pieces += [_pad8(st_pool[l][0].reshape(2, D)) for l in range(2)]
    pieces += [_pad8(d.reshape(NH * NRELP // D, D)) for d in (drb0, drb1)]
    gathered, got = _allgather8(jnp.concatenate(pieces, axis=0), "gather_stats", comm=carry(["a_grp0", "a_in0"]))
    land(["a_grp0", "a_in0"], got)
    mine = [sum4(n) for n in late]
    both.update({n: [a, b] for n, a, b in zip(late, mine, _comm_only(_Comm(swaps=mine), "swap_last"))})
    return dh, both, gathered.reshape(8, N_STAT, D)


ROW_IN = [8 * l for l in range(4)]
ROW_KV = 32
ROW_OUT = [40 + 8 * l for l in range(4)]
ROW_FIN = 72
ROW_ASC = [80, 88]
ROW_RB = [96, 104]
N_STAT = 112


def kernel(x, c, ada_w, ada_b, norm_g, a_w_in, a_w_group, a_scale, a_w_out, kv_norm_g, kv_ada_w, kv_ada_b, w_kv, b_w_in, b_rel_bias, b_w_out, final_g, loss_target, m_ada_w, m_ada_b, m_norm_g, m_a_w_in, m_a_w_group, m_a_scale, m_a_w_out, m_kv_norm_g, m_kv_ada_w, m_kv_ada_b, m_w_kv, m_b_w_in, m_b_rel_bias, m_b_w_out, m_final_g, v_ada_w, v_ada_b, v_norm_g, v_a_w_in, v_a_w_group, v_a_scale, v_a_w_out, v_kv_norm_g, v_kv_ada_w, v_kv_ada_b, v_w_kv, v_b_w_in, v_b_rel_bias, v_b_w_out, v_final_g):
    xi, yi, ci = _place()
    chip = 2 * xi + yi
    dev = 4 * xi + 2 * yi + ci
    n_ada = ada_w.shape[2]
    n_kva = kv_ada_w.shape[1]
    n_asc = a_scale.shape[1]

    ada_b_sh = lax.dynamic_slice_in_dim(ada_b, chip * n_ada, n_ada, axis=1)
    kvb_sh = lax.dynamic_slice_in_dim(kv_ada_b, chip * n_kva, n_kva, axis=0)
    sh = dict(a_in=[a_w_in[l].astype(BF16) for l in range(2)], a_grp=[a_w_group[l].astype(BF16) for l in range(2)],
              a_out=[a_w_out[l].astype(BF16) for l in range(2)], kv=[w_kv.astype(BF16)],
              b_in=[b_w_in[l].astype(BF16) for l in range(2)], b_out=[b_w_out[l].astype(BF16) for l in range(2)])
    c_act, gathered, w_first = _prologue(
        jnp.broadcast_to(c, (SUBLANES, D)), ada_w, ada_b_sh[:, None, :], kv_ada_w, kvb_sh[None, :],
        a_scale.reshape(1, 2 * n_asc), _Comm(gathers=[sh["a_in"][0], sh["a_grp"][0], sh["a_out"][0]]), "prologue")
    rows = jnp.concatenate([lax.dynamic_slice_in_dim(gathered, SUBLANES * (2 * p + ci) + dev, 1, axis=0)
                            for p in range(NCHIP)], axis=0)
    mods = jnp.stack([rows[:, l * n_ada:(l + 1) * n_ada].reshape(3 * D) for l in range(4)])
    kvmod = rows[:, 4 * n_ada:4 * n_ada + n_kva].reshape(2 * D)
    o_asc = 4 * n_ada + n_kva
    a_scale_full = jnp.stack([rows[:, o_asc + l * n_asc:o_asc + (l + 1) * n_asc].reshape(E) for l in range(2)])

    chip_arr = jnp.reshape(chip, (1,)).astype(jnp.int32)
    dh, both, g3 = _example_step(x[0], loss_target[0], mods, kvmod, a_scale_full, norm_g, kv_norm_g, final_g,
                                 b_rel_bias, sh, w_first, chip_arr)
    grad_x = dh[None]

    red, loss_tile = _stats_reduce(g3, ROW_FIN + 1, "stats_reduce")
    loss = loss_tile[0, 0]

    def cat(rows_):
        return jnp.concatenate(rows_, axis=-1)

    g_ada_b = jnp.stack([cat([red[ROW_IN[l]], red[ROW_IN[l] + 1], red[ROW_OUT[l]]]) for l in range(4)])
    g_norm_g = jnp.stack([red[ROW_IN[l] + 2] for l in range(4)])
    g_kv_norm_g = red[ROW_KV + 2]
    g_kv_ada_b = cat([red[ROW_KV], red[ROW_KV + 1]])
    g_final_g = red[ROW_FIN]
    g_asc_full = jnp.stack([red[ROW_ASC[l]:ROW_ASC[l] + 2].reshape(E) for l in range(2)])
    g_a_scale = lax.dynamic_slice_in_dim(g_asc_full, chip * n_asc, n_asc, axis=1)
    g_rel = jnp.stack([red[ROW_RB[bi]:ROW_RB[bi] + NH * NRELP // D].reshape(NH, NRELP)[:, :NREL] for bi in range(2)])

    dmod = jnp.stack([cat([g3[:, ROW_IN[l]], g3[:, ROW_IN[l] + 1], g3[:, ROW_OUT[l]]]) for l in range(4)])
    dmod_sh = lax.dynamic_slice_in_dim(dmod, chip * n_ada, n_ada, axis=2)
    dkv = cat([g3[:, ROW_KV], g3[:, ROW_KV + 1]])[None]
    dkv_sh = lax.dynamic_slice_in_dim(dkv, chip * n_kva, n_kva, axis=2)
    c_act_t = c_act.T
    g_ada_w = _grad_ada(c_act_t, dmod_sh, "grad_ada_w")
    g_kv_ada_w = _grad_ada(c_act_t, dkv_sh, "grad_kv_ada_w")

    def upd(w, m, v, g, name, shape3):
        g = g.reshape(shape3) if not isinstance(g, list) else g
        outs = _adamw(w.reshape(shape3), m.reshape(shape3), v.reshape(shape3), g, name)
        return [o.reshape(w.shape) for o in outs]

    def pair(name):
        return [both[name + "0"], both[name + "1"]]

    res = {}
    res["ada_w"] = upd(ada_w, m_ada_w, v_ada_w, g_ada_w, "adamw_ada_w", ada_w.shape)
    res["ada_b"] = upd(ada_b, m_ada_b, v_ada_b, g_ada_b, "adamw_ada_b", (1,) + ada_b.shape)
    res["norm_g"] = upd(norm_g, m_norm_g, v_norm_g, g_norm_g, "adamw_norm_g", (1,) + norm_g.shape)
    res["a_w_in"] = upd(a_w_in, m_a_w_in, v_a_w_in, pair("a_in"), "adamw_a_w_in", a_w_in.shape)
    res["a_w_group"] = upd(a_w_group, m_a_w_group, v_a_w_group, pair("a_grp"), "adamw_a_w_group", (2, GW, GW))
    res["a_scale"] = upd(a_scale, m_a_scale, v_a_scale, g_a_scale, "adamw_a_scale", (1,) + a_scale.shape)
    res["a_w_out"] = upd(a_w_out, m_a_w_out, v_a_w_out, pair("a_out"), "adamw_a_w_out", a_w_out.shape)
    res["kv_norm_g"] = upd(kv_norm_g, m_kv_norm_g, v_kv_norm_g, g_kv_norm_g, "adamw_kv_norm_g", (1, 1, D))
    res["kv_ada_w"] = upd(kv_ada_w, m_kv_ada_w, v_kv_ada_w, g_kv_ada_w, "adamw_kv_ada_w", (1,) + kv_ada_w.shape)
    res["kv_ada_b"] = upd(kv_ada_b, m_kv_ada_b, v_kv_ada_b, g_kv_ada_b, "adamw_kv_ada_b", (1, 1, 2 * D))
    res["w_kv"] = upd(w_kv, m_w_kv, v_w_kv, [both["kv"]], "adamw_w_kv", (1,) + w_kv.shape)
    res["b_w_in"] = upd(b_w_in, m_b_w_in, v_b_w_in, pair("b_in"), "adamw_b_w_in", b_w_in.shape)
    res["b_rel_bias"] = upd(b_rel_bias, m_b_rel_bias, v_b_rel_bias, g_rel, "adamw_b_rel_bias", (1, 2 * NH, NREL))
    res["b_w_out"] = upd(b_w_out, m_b_w_out, v_b_w_out, pair("b_out"), "adamw_b_w_out", b_w_out.shape)
    res["final_g"] = upd(final_g, m_final_g, v_final_g, g_final_g, "adamw_final_g", (1, 1, D))

    names = ["ada_w", "ada_b", "norm_g", "a_w_in", "a_w_group", "a_scale", "a_w_out", "kv_norm_g", "kv_ada_w", "kv_ada_b",
             "w_kv", "b_w_in", "b_rel_bias", "b_w_out", "final_g"]
    return (loss, grad_x, *[res[n][0] for n in names], *[res[n][1] for n in names], *[res[n][2] for n in names],
            *[res[n][3] for n in names])
```

```python
import math

import jax
import jax.numpy as jnp
from jax import lax
from jax.experimental import pallas as pl
from jax.experimental.pallas import tpu as pltpu

F32 = jnp.float32
BF16 = jnp.bfloat16

D = 1024
E = 2048
NH = 16
HD = 128
CHUNK = 64
LEFT = 8
PAD = LEFT * CHUNK
NREL = 257
NRELP = 384
REL_CLIP = 128
EPS = 1e-6
NEG = -1e30
LOG2E = math.log2(math.e)
SM_SCALE = HD ** -0.5
POOL_W = (2, 4, 8, 16)
GW = 512
HALO = 16
QC = 4
QB = QC * CHUNK
NMASK = PAD // QB
WIN = (QC + LEFT) * CHUNK
BW = (LEFT + 2) * CHUNK
DBW = 4 * CHUNK
NSUB = 8
NCHIP = 4
LANES = 128
SUBLANES = 8

ADAM_LR = 0.001
ADAM_B1 = 0.9
ADAM_B2 = 0.999
ADAM_EPS = 1e-08
ADAM_WD = 0.01
ADAM_STEP = 10

MESH = pl.DeviceIdType.MESH
ANY = pl.BlockSpec(memory_space=pl.ANY)


def _params(n_axes, vmem_mb):
    return pltpu.CompilerParams(dimension_semantics=("arbitrary",) * n_axes, vmem_limit_bytes=vmem_mb * 2 ** 20)


def _nn(a, b):
    return jnp.dot(a, b, preferred_element_type=F32)


def _nt(a, b):
    return lax.dot_general(a, b, (((1,), (1,)), ((), ())), preferred_element_type=F32)


def _tn(a, b):
    return lax.dot_general(a, b, (((0,), (0,)), ((), ())), preferred_element_type=F32)


def _row(n):
    return pl.BlockSpec((1, n), lambda i: (0, 0))


def _colsum(x):
    return jnp.sum(x, axis=0, keepdims=True)


def _place():
    return lax.axis_index("x"), lax.axis_index("y"), lax.axis_index("c")


class _Comm:
    def __init__(self, gathers=(), scatters=(), swaps=()):
        self.n_g = len(gathers)
        self.n_chip = len(gathers) + len(scatters)
        self.n_sw = len(swaps)
        self.arrays = list(gathers) + list(scatters) + list(swaps)
        self.n = len(self.arrays)
        self.half = [a.shape[0] // 2 for a in gathers]
        self.out_shape = ([jax.ShapeDtypeStruct((NCHIP,) + a.shape, a.dtype) for a in gathers]
                          + [jax.ShapeDtypeStruct((3,) + a.shape[1:], a.dtype) for a in scatters]
                          + [jax.ShapeDtypeStruct(a.shape, a.dtype) for a in swaps])
        n_c, n_f, n_s = max(3 * self.n_chip, 1), max(3 * self.n_g, 1), max(self.n_sw, 1)
        self.scratch = [pltpu.SemaphoreType.DMA((n_c,)), pltpu.SemaphoreType.DMA((n_c,)),
                        pltpu.SemaphoreType.DMA((max(self.n_g, 1),)), pltpu.SemaphoreType.DMA((n_f,)),
                        pltpu.SemaphoreType.DMA((n_f,)), pltpu.SemaphoreType.DMA((n_s,)), pltpu.SemaphoreType.DMA((n_s,))]

    def _chip_copies(self, ins, outs, send, recv, landing):
        x, y, c = _place()
        chips = [(1 - x, y), (x, 1 - y), (1 - x, 1 - y)]
        mine = 2 * x + y
        cps = []
        for k in range(self.n_chip):
            for j, (cx, cy) in enumerate(chips):
                q = 2 * cx + cy
                if k < self.n_g:
                    part = pl.ds(c * self.half[k], self.half[k])
                    src = ins[k].at[part]
                    dst = outs[k].at[q if landing else mine, part]
                else:
                    src = ins[k].at[q]
                    dst = outs[k].at[j]
                cps.append(pltpu.make_async_remote_copy(
                    src_ref=src, dst_ref=dst, send_sem=send.at[3 * k + j], recv_sem=recv.at[3 * k + j],
                    device_id=(cx, cy, c), device_id_type=MESH))
        return cps

    def _core_copies(self, outs, fsend, frecv, landing):
        x, y, c = _place()
        chips = [(1 - x, y), (x, 1 - y), (1 - x, 1 - y)]
        cps = []
        for k in range(self.n_g):
            for j, (cx, cy) in enumerate(chips):
                part = pl.ds((1 - c if landing else c) * self.half[k], self.half[k])
                blk = outs[k].at[2 * cx + cy, part]
                cps.append(pltpu.make_async_remote_copy(
                    src_ref=blk, dst_ref=blk, send_sem=fsend.at[3 * k + j], recv_sem=frecv.at[3 * k + j],
                    device_id=(x, y, 1 - c), device_id_type=MESH))
        return cps

    def _local_copies(self, ins, outs, loc):
        x, y, _ = _place()
        return [pltpu.make_async_copy(ins[k], outs[k].at[2 * x + y], loc.at[k]) for k in range(self.n_g)]

    def _swap_copies(self, ins, outs, ssend, srecv):
        x, y, c = _place()
        return [pltpu.make_async_remote_copy(
            src_ref=ins[k], dst_ref=outs[k], send_sem=ssend.at[k - self.n_chip], recv_sem=srecv.at[k - self.n_chip],
            device_id=(x, y, 1 - c), device_id_type=MESH) for k in range(self.n_chip, self.n)]

    def start(self, ins, outs, send, recv, loc, fsend, frecv, ssend, srecv):
        for cp in (self._local_copies(ins, outs, loc) + self._chip_copies(ins, outs, send, recv, False)
                   + self._swap_copies(ins, outs, ssend, srecv)):
            cp.start()

    def wait(self, ins, outs, send, recv, loc, fsend, frecv, ssend, srecv):
        lands = self._chip_copies(ins, outs, send, recv, True)
        passes = self._core_copies(outs, fsend, frecv, False)
        for k in range(self.n_chip):
            for j in range(3):
                lands[3 * k + j].wait_recv()
                if k < self.n_g:
                    passes[3 * k + j].start()
        for cp in self._core_copies(outs, fsend, frecv, True):
            cp.wait_recv()
        swaps = self._swap_copies(ins, outs, ssend, srecv)
        for cp in swaps:
            cp.wait_recv()
        for cp in self._chip_copies(ins, outs, send, recv, False) + passes + swaps:
            cp.wait_send()
        for cp in self._local_copies(ins, outs, loc):
            cp.wait()


def _call(body, name, grid, in_specs, out_specs, out_shape, scratch, params, args, comm=None):
    n_in, n_out, n_sc = len(in_specs), len(out_specs), len(scratch)
    if comm is None:
        outs = pl.pallas_call(body, name=name, grid=grid, in_specs=in_specs, out_specs=out_specs, out_shape=out_shape,
                              scratch_shapes=scratch, compiler_params=params)(*args)
        return list(outs), []
    n = comm.n
    o0 = n_in + n
    s0 = o0 + n_out + n

    def wrapped(*refs):
        c_refs = (refs[n_in:o0], refs[o0 + n_out:s0]) + tuple(refs[s0 + n_sc:])
        ids = [pl.program_id(a) for a in range(len(grid))]
        first = ids[0] == 0
        last = ids[0] == grid[0] - 1
        for a in range(1, len(grid)):
            first = first & (ids[a] == 0)
            last = last & (ids[a] == grid[a] - 1)

        @pl.when(first)
        def _():
            comm.start(*c_refs)

        body(*refs[:n_in], *refs[o0:o0 + n_out], *refs[s0:s0 + n_sc])

        @pl.when(last)
        def _():
            comm.wait(*c_refs)

    outs = pl.pallas_call(
        wrapped, name=name, grid=grid, in_specs=list(in_specs) + [ANY] * n, out_specs=list(out_specs) + [ANY] * n,
        out_shape=list(out_shape) + comm.out_shape, scratch_shapes=list(scratch) + comm.scratch, compiler_params=params,
    )(*args, *comm.arrays)
    return list(outs[:n_out]), list(outs[n_out:])


def _comm_only(comm, name):
    def body(*refs):
        c_refs = (refs[:comm.n], refs[comm.n:2 * comm.n]) + tuple(refs[2 * comm.n:])
        comm.start(*c_refs)
        comm.wait(*c_refs)

    return pl.pallas_call(body, name=name, in_specs=[ANY] * comm.n, out_specs=[ANY] * comm.n, out_shape=comm.out_shape,
                          scratch_shapes=comm.scratch)(*comm.arrays)


def _in_fwd(h, g, shift, scale, w, dt_a, dt_b, name, pad_rows=0, comm=None, tm=512):
    S = h.shape[0]
    n_pad = pad_rows // tm

    def body(h_ref, g_ref, sh_ref, sc_ref, w_hbm, u_ref, oa_ref, ob_ref, w_v, sem):
        i = pl.program_id(0)

        @pl.when(i == 0)
        def _():
            cp = pltpu.make_async_copy(w_hbm, w_v, sem)
            cp.start()
            cp.wait()

        hh = h_ref[...]
        r = lax.rsqrt(jnp.mean(hh * hh, axis=-1, keepdims=True) + EPS)
        u = (hh * r * g_ref[...]) * (1.0 + sc_ref[...]) + sh_ref[...]
        ub = u.astype(BF16)
        u_ref[...] = ub
        for q in range(NCHIP):
            o_ref = oa_ref if q < 2 else ob_ref
            o_ref[:, (q % 2) * D:(q % 2 + 1) * D] = _nn(ub, w_v[q]).astype(o_ref.dtype)

        if n_pad:
            @pl.when(i < n_pad)
            def _():
                oa_ref[...] = jnp.zeros(oa_ref.shape, oa_ref.dtype)
                ob_ref[...] = jnp.zeros(ob_ref.shape, ob_ref.dtype)

    def src(i):
        return (jnp.maximum(i - n_pad, 0), 0)

    outs, landed = _call(
        body, name, (S // tm + n_pad,),
        [pl.BlockSpec((tm, D), src), _row(D), _row(D), _row(D), ANY],
        [pl.BlockSpec((tm, D), src), pl.BlockSpec((tm, E), lambda i: (i, 0)), pl.BlockSpec((tm, E), lambda i: (i, 0))],
        [jax.ShapeDtypeStruct((S, D), BF16), jax.ShapeDtypeStruct((S + pad_rows, E), dt_a),
         jax.ShapeDtypeStruct((S + pad_rows, E), dt_b)],
        [pltpu.VMEM((NCHIP, D, D), BF16), pltpu.SemaphoreType.DMA],
        _params(1, 52), (h, g, shift, scale, w), comm)
    return outs, landed


def _a_fwd(h, g, shift, scale, asc, gate, w_in, wg, w_out, name, comm=None, tm=512):
    S = h.shape[0]

    def body(h_ref, g_ref, sh_ref, sc_ref, as_ref, gate_ref, wi_hbm, wg_hbm, wo_hbm,
             u_ref, z_ref, p_ref, m_ref, y_ref, ho_ref, wi_v, wg_v, wo_v, buf, sems):
        i = pl.program_id(0)

        @pl.when(i == 0)
        def _():
            cps = [pltpu.make_async_copy(wi_hbm, wi_v, sems.at[0]), pltpu.make_async_copy(wg_hbm, wg_v, sems.at[1]),
                   pltpu.make_async_copy(wo_hbm, wo_v, sems.at[2])]
            for cp in cps:
                cp.start()
            buf[0:HALO, :] = jnp.zeros((HALO, E), F32)
            for cp in cps:
                cp.wait()

        hh = h_ref[...]
        r = lax.rsqrt(jnp.mean(hh * hh, axis=-1, keepdims=True) + EPS)
        ub = ((hh * r * g_ref[...]) * (1.0 + sc_ref[...]) + sh_ref[...]).astype(BF16)
        u_ref[...] = ub
        for q in range(2):
            buf[HALO:HALO + tm, q * D:(q + 1) * D] = _nn(ub, wi_v[q])
        t = i * tm + lax.broadcasted_iota(jnp.int32, (tm, 1), 0)
        y = None
        for gi, w in enumerate(POOL_W):
            cols = slice(gi * GW, (gi + 1) * GW)
            x = buf[:, cols]
            s = x
            k = 1
            while k < w:
                s = s + pltpu.roll(s, k, 0)
                k *= 2
            inv_cnt = 1.0 / jnp.minimum(t + 1, w).astype(F32)
            pb = (s[HALO:, :] * inv_cnt - x[HALO:, :]).astype(BF16)
            p_ref[:, cols] = pb
            mb = _nn(pb, wg_v[gi]).astype(BF16)
            m_ref[:, cols] = mb
            zb = _nn(ub, wi_v[2 + gi // 2, :, (gi % 2) * GW:(gi % 2 + 1) * GW]).astype(BF16)
            z_ref[:, cols] = zb
            zz = zb.astype(F32)
            act = ((mb.astype(F32) * as_ref[:, cols]) * (zz * jax.nn.sigmoid(zz))).astype(BF16)
            part = _nn(act, wo_v[gi])
            y = part if y is None else y + part
        buf[0:HALO, :] = buf[tm:tm + HALO, :]
        y_ref[...] = y.astype(BF16)
        ho_ref[...] = hh + gate_ref[...] * y

    rows_d = pl.BlockSpec((tm, D), lambda i: (i, 0))
    rows_e = pl.BlockSpec((tm, E), lambda i: (i, 0))
    return _call(
        body, name, (S // tm,),
        [rows_d, _row(D), _row(D), _row(D), _row(E), _row(D), ANY, ANY, ANY],
        [rows_d, rows_e, rows_e, rows_e, rows_d, rows_d],
        [jax.ShapeDtypeStruct((S, D), BF16), jax.ShapeDtypeStruct((S, E), BF16), jax.ShapeDtypeStruct((S, E), BF16),
         jax.ShapeDtypeStruct((S, E), BF16), jax.ShapeDtypeStruct((S, D), BF16), jax.ShapeDtypeStruct((S, D), F32)],
        [pltpu.VMEM((NCHIP, D, D), BF16), pltpu.VMEM((4, GW, GW), BF16), pltpu.VMEM((NCHIP, GW, D), BF16),
         pltpu.VMEM((tm + HALO, E), F32), pltpu.SemaphoreType.DMA((3,))],
        _params(1, 60), (h, g, shift, scale, asc, gate, w_in, wg, w_out), comm)


def _out_fwd(a, z, w, gate, h, name, head=None, comm=None, tm=512):
    S = h.shape[0]
    kb = E // NCHIP
    n_in = 5 if head is None else 7

    def body(*refs):
        a_ref, z_ref, w_hbm, gate_ref, h_ref = refs[:5]
        w_v, sem = refs[-2:]
        i = pl.program_id(0)

        @pl.when(i == 0)
        def _():
            cp = pltpu.make_async_copy(w_hbm, w_v, sem)
            cp.start()
            cp.wait()

        y = None
        for p in range(NCHIP):
            cols = slice(p * kb, (p + 1) * kb)
            zz = z_ref[:, cols].astype(F32)
            act = (a_ref[:, cols].astype(F32) * (zz * jax.nn.sigmoid(zz))).astype(BF16)
            part = _nn(act, w_v[p])
            y = part if y is None else y + part
        refs[n_in][...] = y.astype(BF16)
        hh = h_ref[...] + gate_ref[...] * y
        if head is None:
            refs[n_in + 1][...] = hh
            return
        g_ref, t_ref = refs[5:7]
        dh_ref, st_ref = refs[n_in + 1:n_in + 3]

        @pl.when(i == 0)
        def _():
            st_ref[...] = jnp.zeros((SUBLANES, D), F32)

        r = lax.rsqrt(jnp.mean(hh * hh, axis=-1, keepdims=True) + EPS)
        xhat = hh * r
        diff = xhat * g_ref[...] - t_ref[...]
        st_ref[1:2, :] += _colsum(diff * diff)
        dout = diff * (1.0 / D)
        st_ref[0:1, :] += _colsum(dout * xhat)
        dx = dout * g_ref[...]
        dh_ref[...] = r * (dx - xhat * jnp.mean(dx * xhat, axis=-1, keepdims=True))

    rows_d = pl.BlockSpec((tm, D), lambda i: (i, 0))
    rows_e = pl.BlockSpec((tm, E), lambda i: (i, 0))
    in_specs = [rows_e, rows_e, ANY, _row(D), rows_d]
    out_specs = [rows_d, rows_d]
    out_shape = [jax.ShapeDtypeStruct((S, D), BF16), jax.ShapeDtypeStruct((S, D), F32)]
    args = (a, z, w, gate, h)
    if head is not None:
        in_specs += [_row(D), rows_d]
        out_specs += [pl.BlockSpec((SUBLANES, D), lambda i: (0, 0))]
        out_shape += [jax.ShapeDtypeStruct((SUBLANES, D), F32)]
        args += tuple(head)
    return _call(body, name, (S // tm,), in_specs, out_specs, out_shape,
                 [pltpu.VMEM((NCHIP, kb, D), BF16), pltpu.SemaphoreType.DMA], _params(1, 52), args, comm)


TW = BW + LANES


def _diag_onehot(transpose):
    shape = (TW, NRELP) if transpose else (NRELP, TW)
    j = lax.broadcasted_iota(jnp.int32, shape, 0 if transpose else 1)
    r = lax.broadcasted_iota(jnp.int32, shape, 1 if transpose else 0)
    idx = jnp.clip(PAD - (j - LANES), -REL_CLIP, REL_CLIP) + REL_CLIP
    return jnp.where(idx == r, 1.0, 0.0).astype(BF16)


def _strip_valid():
    m = lax.broadcasted_iota(jnp.int32, (NH, BW), 1)
    return m < (LEFT + 1) * CHUNK, m >= CHUNK


def _bias_build(rb, name):
    def body(rb_ref, a_ref, b_ref):
        x = rb_ref[...]
        hi = x.astype(BF16)
        r1 = x - hi.astype(F32)
        mid = r1.astype(BF16)
        lo = (r1 - mid.astype(F32)).astype(BF16)
        oh = _diag_onehot(False)
        diag = (_nn(hi, oh) + _nn(mid, oh)) + _nn(lo, oh)
        valid_a, valid_b = _strip_valid()
        for qi in range(CHUNK):
            a_ref[qi] = jnp.where(valid_a, pltpu.roll(diag, TW - (LANES - qi), 1)[:, :BW], NEG)
            b_ref[qi] = jnp.where(valid_b, pltpu.roll(diag, TW - (CHUNK - qi), 1)[:, :BW], NEG)

    vmem = pl.BlockSpec(memory_space=pltpu.VMEM)
    return pl.pallas_call(
        body, name=name, in_specs=[vmem], out_specs=[vmem, vmem],
        out_shape=[jax.ShapeDtypeStruct((CHUNK, NH, BW), F32), jax.ShapeDtypeStruct((CHUNK, NH, BW), F32)],
        compiler_params=pltpu.CompilerParams(vmem_limit_bytes=32 * 2 ** 20),
    )(rb)


def _dbias_reduce(dba, dbb, name):
    def body(a_ref, b_ref, o_ref):
        valid_a, valid_b = _strip_valid()
        zeros = jnp.zeros((NH, TW - BW), F32)
        acc = jnp.zeros((NH, TW), F32)
        for qi in range(CHUNK):
            xa = jnp.concatenate([jnp.where(valid_a, a_ref[qi], 0.0), zeros], axis=1)
            xb = jnp.concatenate([jnp.where(valid_b, b_ref[qi], 0.0), zeros], axis=1)
            acc = acc + (pltpu.roll(xa, LANES - qi, 1) + pltpu.roll(xb, CHUNK - qi, 1))
        oh = _diag_onehot(True)
        hi = acc.astype(BF16)
        mid = (acc - hi.astype(F32)).astype(BF16)
        r = lax.broadcasted_iota(jnp.int32, (NH, NRELP), 1)
        near = jnp.where(r < 2 * REL_CLIP, _nn(hi, oh) + _nn(mid, oh), 0.0)
        o_ref[...] = jnp.where(r == 2 * REL_CLIP, -jnp.sum(near, axis=-1, keepdims=True), near)

    vmem = pl.BlockSpec(memory_space=pltpu.VMEM)
    return pl.pallas_call(
        body, name=name, in_specs=[vmem, vmem], out_specs=vmem,
        out_shape=jax.ShapeDtypeStruct((NH, NRELP), F32),
        compiler_params=pltpu.CompilerParams(vmem_limit_bytes=32 * 2 ** 20),
    )(dba, dbb)


def _build_bias(bias3, ba_ref, bb_ref):
    bias3[NMASK] = jnp.full((QB, WIN), NEG, F32)
    for qc in range(QC):
        rows = slice(qc * CHUNK, (qc + 1) * CHUNK)
        if qc % 2 == 0:
            bias3[NMASK, rows, qc * CHUNK:qc * CHUNK + BW] = ba_ref[...] * LOG2E
        else:
            bias3[NMASK, rows, (qc - 1) * CHUNK:(qc - 1) * CHUNK + BW] = bb_ref[...] * LOG2E
    col = lax.broadcasted_iota(jnp.int32, (QB, WIN), 1)
    for sub in range(NMASK):
        bias3[sub] = jnp.where(col < PAD - sub * QB, NEG, bias3[NMASK])


def _nsub(S):
    n = min(NSUB, S // QB)
    assert S % (n * QB) == 0 and n >= NMASK
    return n


def _row0(i, sub, nsub):
    return pl.multiple_of((i * nsub + sub) * QB, QB)


def _scores(q_ref, k_ref, i, sub, nsub):
    return _nt(q_ref[sub * QB:(sub + 1) * QB, :], k_ref[pl.ds(_row0(i, sub, nsub), WIN), :])


HALF = QB // 2
LIVE = WIN - LANES


def _live(half):
    return slice(half * HALF, (half + 1) * HALF), slice(half * LANES, half * LANES + LIVE)


def _widen(x, half):
    zeros = jnp.zeros((HALF, LANES), x.dtype)
    return jnp.concatenate([x, zeros] if half == 0 else [zeros, x], axis=1)


def _probs(s, bias3, i, sub):
    which = jnp.where(i == 0, sub, NMASK) if sub < NMASK else NMASK
    out = []
    for half in range(2):
        rows, cols = _live(half)
        t = s[rows, cols] * (SM_SCALE * LOG2E) + bias3[which, rows, cols]
        e = jnp.exp2(t - jnp.max(t, axis=-1, keepdims=True))
        out.append(_widen((e * (1.0 / jnp.sum(e, axis=-1, keepdims=True))).astype(BF16), half))
    return jnp.concatenate(out, axis=0)


def _attn_fwd(q, kp, vp, ba, bb, name, comm=None):
    S = q.shape[0]
    nsub = _nsub(S)
    R = nsub * QB

    def body(q_ref, k_ref, v_ref, ba_ref, bb_ref, o_ref, p_ref, bias3):
        i = pl.program_id(1)

        @pl.when(i == 0)
        def _():
            _build_bias(bias3, ba_ref, bb_ref)

        s_next = _scores(q_ref, k_ref, i, 0, nsub)
        for sub in range(nsub):
            s = s_next
            if sub + 1 < nsub:
                s_next = _scores(q_ref, k_ref, i, sub + 1, nsub)
            pb = _probs(s, bias3, i, sub)
            p_ref[sub] = pb
            o_ref[sub * QB:(sub + 1) * QB, :] = _nn(pb, v_ref[pl.ds(_row0(i, sub, nsub), WIN), :]).astype(BF16)

    return _call(
        body, name, (NH, S // R),
        [pl.BlockSpec((R, HD), lambda h, i: (i, h)), pl.BlockSpec((S + PAD, HD), lambda h, i: (0, h)),
         pl.BlockSpec((S + PAD, HD), lambda h, i: (0, h)), pl.BlockSpec((None, CHUNK, BW), lambda h, i: (h, 0, 0)),
         pl.BlockSpec((None, CHUNK, BW), lambda h, i: (h, 0, 0))],
        [pl.BlockSpec((R, HD), lambda h, i: (i, h)), pl.BlockSpec((None, nsub, QB, WIN), lambda h, i: (h, i, 0, 0))],
        [jax.ShapeDtypeStruct((S, E), BF16), jax.ShapeDtypeStruct((NH, S // QB, QB, WIN), BF16)],
        [pltpu.VMEM((NMASK + 1, QB, WIN), F32)],
        _params(2, 48), (q, kp, vp, ba, bb), comm)


def _store_grad(acc, stage, dw_hbm, sem):
    for q in range(NCHIP):
        stage[...] = acc[q].astype(BF16)
        cp = pltpu.make_async_copy(stage, dw_hbm.at[q], sem)
        cp.start()
        cp.wait()


def _out_bwd(dh, y, gate, a, cs, z, w, name, comm=None, tm=256):
    S = dh.shape[0]
    kb = E // NCHIP
    cb = 256
    n_t = S // tm

    def body(dh_ref, y_ref, gate_ref, a_ref, cs_ref, z_ref, w_hbm, da_ref, dz_ref, dw_hbm, st_ref, w_v, acc, stage, sem):
        i = pl.program_id(0)

        @pl.when(i == 0)
        def _():
            cp = pltpu.make_async_copy(w_hbm, w_v, sem)
            cp.start()
            acc[...] = jnp.zeros(acc.shape, F32)
            st_ref[...] = jnp.zeros((SUBLANES, D), F32)
            cp.wait()

        dhh = dh_ref[...]
        st_ref[0:1, :] += _colsum(dhh * y_ref[...].astype(F32))
        dy = (dhh * gate_ref[...]).astype(BF16)
        for blk in range(E // cb):
            p, r0 = divmod(blk * cb, kb)
            cols = slice(blk * cb, (blk + 1) * cb)
            zz = z_ref[:, cols].astype(F32)
            sig = jax.nn.sigmoid(zz)
            sz = zz * sig
            ae = a_ref[:, cols].astype(F32) * cs_ref[:, cols]
            acc[p, r0:r0 + cb, :] += _tn((ae * sz).astype(BF16), dy)
            dact = _nt(dy, w_v[p, r0:r0 + cb, :])
            da_ref[:, cols] = (dact * sz).astype(BF16)
            dz_ref[:, cols] = (dact * ae * (sig * (1.0 + zz * (1.0 - sig)))).astype(BF16)

        @pl.when(i == n_t - 1)
        def _():
            _store_grad(acc, stage, dw_hbm, sem)

    return _call(
        body, name, (n_t,),
        [pl.BlockSpec((tm, D), lambda i: (i, 0)), pl.BlockSpec((tm, D), lambda i: (i, 0)), _row(D),
         pl.BlockSpec((tm, E), lambda i: (i, 0)), _row(E), pl.BlockSpec((tm, E), lambda i: (i, 0)), ANY],
        [pl.BlockSpec((tm, E), lambda i: (i, 0)), pl.BlockSpec((tm, E), lambda i: (i, 0)), ANY,
         pl.BlockSpec((SUBLANES, D), lambda i: (0, 0))],
        [jax.ShapeDtypeStruct((S, E), BF16), jax.ShapeDtypeStruct((S, E), BF16),
         jax.ShapeDtypeStruct((NCHIP, kb, D), BF16), jax.ShapeDtypeStruct((SUBLANES, D), F32)],
        [pltpu.VMEM((NCHIP, kb, D), BF16), pltpu.VMEM((NCHIP, kb, D), F32), pltpu.VMEM((kb, D), BF16),
         pltpu.SemaphoreType.DMA],
        _params(1, 52), (dh, y, gate, a, cs, z, w), comm)


def _attn_bwd(q, kp, vp, probs, do, prev, name, comm=None):
    S = q.shape[0]
    nsub = _nsub(S)
    R = nsub * QB
    n_i = S // R
    dt_kv = F32 if prev is None else BF16

    def body(*refs):
        q_ref, k_ref, v_ref, p_ref, do_ref = refs[:5]
        refs = refs[5:]
        if prev is not None:
            pk_hbm, pv_hbm = refs[:2]
            refs = refs[2:]
        dq_ref, dk_ref, dv_ref, dba_ref, dbb_ref, dbias, dk_acc, dv_acc = refs[:8]
        if prev is not None:
            pk_v, pv_v, sems = refs[8:]
        h = pl.program_id(0)
        i = pl.program_id(1)

        def prev_copies():
            cols = pl.ds(pl.multiple_of(h * HD, HD), HD)
            return (pltpu.make_async_copy(pk_hbm.at[:, cols], pk_v, sems.at[0]),
                    pltpu.make_async_copy(pv_hbm.at[:, cols], pv_v, sems.at[1]))

        @pl.when(i == 0)
        def _():
            if prev is not None:
                for cp in prev_copies():
                    cp.start()
            dbias[...] = jnp.zeros((2, CHUNK, DBW), F32)
            dk_acc[...] = jnp.zeros((S + PAD, HD), F32)
            dv_acc[...] = jnp.zeros((S + PAD, HD), F32)

        def mxu_in(sub):
            return _nt(do_ref[sub * QB:(sub + 1) * QB, :], v_ref[pl.ds(_row0(i, sub, nsub), WIN), :])

        nxt = mxu_in(0)
        for sub in range(nsub):
            rows = slice(sub * QB, (sub + 1) * QB)
            win = pl.ds(_row0(i, sub, nsub), WIN)
            dp = nxt
            if sub + 1 < nsub:
                nxt = mxu_in(sub + 1)
            parts = []
            for half in range(2):
                hrows, hcols = _live(half)
                p = p_ref[sub, hrows, hcols].astype(F32)
                dph = dp[hrows, hcols]
                ds = p * (dph - jnp.sum(p * dph, axis=-1, keepdims=True))
                dbias[0] += ds[0:CHUNK, LIVE - DBW:LIVE]
                dbias[1] += ds[CHUNK:HALF, LIVE - DBW:LIVE]
                parts.append(_widen((ds * SM_SCALE).astype(BF16), half))
            dsb = jnp.concatenate(parts, axis=0)
            dq_ref[rows, :] = _nn(dsb, k_ref[win, :]).astype(BF16)
            dk_acc[win, :] += _tn(dsb, q_ref[rows, :])
            dv_acc[win, :] += _tn(p_ref[sub], do_ref[rows, :])

        @pl.when(i == n_i - 1)
        def _():
            zeros = jnp.zeros((CHUNK, BW - DBW), F32)
            dba_ref[...] = jnp.concatenate([zeros, dbias[0]], axis=1)
            dbb_ref[...] = jnp.concatenate([zeros, dbias[1]], axis=1)
            if prev is None:
                dk_ref[...] = dk_acc[...]
                dv_ref[...] = dv_acc[...]
            else:
                for cp in prev_copies():
                    cp.wait()
                dk_ref[...] = (dk_acc[...] + pk_v[...]).astype(BF16)
                dv_ref[...] = (dv_acc[...] + pv_v[...]).astype(BF16)

    head = pl.BlockSpec((S + PAD, HD), lambda h, i: (0, h))
    strip = pl.BlockSpec((None, CHUNK, BW), lambda h, i: (h, 0, 0))
    blk = pl.BlockSpec((R, HD), lambda h, i: (i, h))
    in_specs = [blk, head, head, pl.BlockSpec((None, nsub, QB, WIN), lambda h, i: (h, i, 0, 0)), blk]
    scratch = [pltpu.VMEM((2, CHUNK, DBW), F32), pltpu.VMEM((S + PAD, HD), F32), pltpu.VMEM((S + PAD, HD), F32)]
    args = (q, kp, vp, probs, do)
    if prev is not None:
        in_specs += [ANY, ANY]
        scratch += [pltpu.VMEM((S + PAD, HD), F32), pltpu.VMEM((S + PAD, HD), F32), pltpu.SemaphoreType.DMA((2,))]
        args += tuple(prev)
    return _call(
        body, name, (NH, n_i), in_specs, [blk, head, head, strip, strip],
        [jax.ShapeDtypeStruct((S, E), BF16), jax.ShapeDtypeStruct((S + PAD, E), dt_kv),
         jax.ShapeDtypeStruct((S + PAD, E), dt_kv), jax.ShapeDtypeStruct((NH, CHUNK, BW), F32),
         jax.ShapeDtypeStruct((NH, CHUNK, BW), F32)],
        scratch, _params(2, 56), args, comm)


def _pool_bwd(dms, mixed, pooled, wg, a_scale, name, comm=None, tm=512):
    S = dms.shape[0]
    n_t = S // tm

    def rev(i):
        return (n_t - 1 - i, 0)

    def body(d_ref, m_ref, p_ref, wg_ref, as_ref, dv_ref, dwg_ref, st_ref, buf):
        i = pl.program_id(0)

        @pl.when(i == 0)
        def _():
            buf[tm:tm + HALO, :] = jnp.zeros((HALO, E), F32)
            dwg_ref[...] = jnp.zeros((4, GW, GW), F32)
            st_ref[...] = jnp.zeros((SUBLANES, E), F32)

        t = (n_t - 1 - i) * tm + lax.broadcasted_iota(jnp.int32, (tm, 1), 0)
        st_ref[0:1, :] += _colsum(d_ref[...].astype(F32) * m_ref[...].astype(F32))
        for gi, w in enumerate(POOL_W):
            cols = slice(gi * GW, (gi + 1) * GW)
            dm = (d_ref[:, cols].astype(F32) * as_ref[:, cols]).astype(BF16)
            dpool = _nt(dm, wg_ref[gi])
            dwg_ref[gi] += _tn(p_ref[:, cols], dm)
            inv_cnt = 1.0 / jnp.minimum(t + 1, w).astype(F32)
            buf[0:tm, cols] = dpool * inv_cnt
            s = buf[:, cols]
            k = 1
            while k < w:
                s = s + pltpu.roll(s, tm + HALO - k, 0)
                k *= 2
            dv_ref[:, cols] = (s[0:tm, :] - dpool).astype(BF16)
        buf[tm:tm + HALO, :] = buf[0:HALO, :]

    return _call(
        body, name, (n_t,),
        [pl.BlockSpec((tm, E), rev), pl.BlockSpec((tm, E), rev), pl.BlockSpec((tm, E), rev),
         pl.BlockSpec((4, GW, GW), lambda i: (0, 0, 0)), _row(E)],
        [pl.BlockSpec((tm, E), rev), pl.BlockSpec((4, GW, GW), lambda i: (0, 0, 0)),
         pl.BlockSpec((SUBLANES, E), lambda i: (0, 0))],
        [jax.ShapeDtypeStruct((S, E), BF16), jax.ShapeDtypeStruct((4, GW, GW), F32),
         jax.ShapeDtypeStruct((SUBLANES, E), F32)],
        [pltpu.VMEM((tm + HALO, E), F32)],
        _params(1, 52), (dms, mixed, pooled, wg, a_scale), comm)


def _in_bwd(da, db, row_off, u, h, g, scale, w, dh_out, name, comm=None, tm=256):
    S = h.shape[0]
    n_t = S // tm
    off = row_off // tm

    def body(da_ref, db_ref, u_ref, h_ref, g_ref, sc_ref, w_hbm, dho_ref, dhi_ref, dw_hbm, st_ref, w_v, acc, stage, sem):
        i = pl.program_id(0)

        @pl.when(i == 0)
        def _():
            cp = pltpu.make_async_copy(w_hbm, w_v, sem)
            cp.start()
            acc[...] = jnp.zeros(acc.shape, F32)
            st_ref[...] = jnp.zeros((SUBLANES, D), F32)
            cp.wait()

        ub = u_ref[...]
        du = None
        for q in range(NCHIP):
            d_ref = da_ref if q < 2 else db_ref
            dv = d_ref[:, (q % 2) * D:(q % 2 + 1) * D]
            acc[q] += _tn(ub, dv)
            part = _nt(dv, w_v[q])
            du = part if du is None else du + part

        hh = h_ref[...]
        r = lax.rsqrt(jnp.mean(hh * hh, axis=-1, keepdims=True) + EPS)
        xhat = hh * r
        gg = g_ref[...]
        st_ref[0:1, :] += _colsum(du)
        st_ref[1:2, :] += _colsum(du * (xhat * gg))
        dn = du * (1.0 + sc_ref[...])
        st_ref[2:3, :] += _colsum(dn * xhat)
        dx = dn * gg
        dhi_ref[...] = dho_ref[...] + r * (dx - xhat * jnp.mean(dx * xhat, axis=-1, keepdims=True))

        @pl.when(i == n_t - 1)
        def _():
            _store_grad(acc, stage, dw_hbm, sem)

    part_spec = pl.BlockSpec((tm, E), lambda i: (i + off, 0))
    return _call(
        body, name, (n_t,),
        [part_spec, part_spec, pl.BlockSpec((tm, D), lambda i: (i, 0)), pl.BlockSpec((tm, D), lambda i: (i, 0)),
         _row(D), _row(D), ANY, pl.BlockSpec((tm, D), lambda i: (i, 0))],
        [pl.BlockSpec((tm, D), lambda i: (i, 0)), ANY, pl.BlockSpec((SUBLANES, D), lambda i: (0, 0))],
        [jax.ShapeDtypeStruct((S, D), F32), jax.ShapeDtypeStruct((NCHIP, D, D), BF16),
         jax.ShapeDtypeStruct((SUBLANES, D), F32)],
        [pltpu.VMEM((NCHIP, D, D), BF16), pltpu.VMEM((NCHIP, D, D), F32), pltpu.VMEM((D, D), BF16),
         pltpu.SemaphoreType.DMA],
        _params(1, 56), (da, db, u, h, g, scale, w, dh_out), comm)


def _grad_ada(c_act_t, dmod, name):
    L, _, n = dmod.shape

    def body(c_ref, d_ref, o_ref):
        acc = None
        for b in range(SUBLANES):
            part = c_ref[:, b:b + 1] * d_ref[b:b + 1, :]
            acc = part if acc is None else acc + part
        o_ref[...] = acc

    return pl.pallas_call(
        body, name=name, grid=(L,),
        in_specs=[pl.BlockSpec((D, SUBLANES), lambda l: (0, 0)), pl.BlockSpec((None, SUBLANES, n), lambda l: (l, 0, 0))],
        out_specs=pl.BlockSpec((None, D, n), lambda l: (l, 0, 0)),
        out_shape=jax.ShapeDtypeStruct((L, D, n), F32),
        compiler_params=_params(1, 32),
    )(c_act_t, dmod)


def _stats_reduce(g3, loss_row, name):
    n_dev, rows, _ = g3.shape

    def body(g_ref, o_ref, l_ref):
        acc = g_ref[0]
        for d in range(1, n_dev):
            acc = acc + g_ref[d]
        o_ref[...] = acc
        tot = jnp.sum(o_ref[loss_row:loss_row + 1, :], axis=-1, keepdims=True)
        l_ref[...] = jnp.broadcast_to(tot * (0.5 / D), (SUBLANES, LANES))

    return pl.pallas_call(
        body, name=name,
        in_specs=[pl.BlockSpec(memory_space=pltpu.VMEM)],
        out_specs=[pl.BlockSpec(memory_space=pltpu.VMEM), pl.BlockSpec(memory_space=pltpu.VMEM)],
        out_shape=[jax.ShapeDtypeStruct((rows, D), F32), jax.ShapeDtypeStruct((SUBLANES, LANES), F32)],
        compiler_params=pltpu.CompilerParams(vmem_limit_bytes=32 * 2 ** 20),
    )(g3)


def _sum4(own, land, chip, name, tr=256):
    _, R, C = own.shape
    tr = min(tr, R)

    def body(p_ref, own_ref, land_ref, o_ref):
        o_ref[...] = ((own_ref[...].astype(F32) + land_ref[0].astype(F32)) + land_ref[1].astype(F32)) + land_ref[2].astype(F32)

    out = pl.pallas_call(
        body, name=name,
        grid_spec=pltpu.PrefetchScalarGridSpec(
            num_scalar_prefetch=1, grid=(R // tr,),
            in_specs=[pl.BlockSpec((None, tr, C), lambda i, p: (p[0], i, 0)), pl.BlockSpec((3, tr, C), lambda i, p: (0, i, 0))],
            out_specs=pl.BlockSpec((tr, C), lambda i, p: (i, 0))),
        out_shape=jax.ShapeDtypeStruct((R, C), F32),
        compiler_params=_params(1, 32),
    )(chip, pltpu.with_memory_space_constraint(own, pltpu.HBM), pltpu.with_memory_space_constraint(land, pltpu.HBM))
    return pltpu.with_memory_space_constraint(out, pltpu.HBM)


def _adamw(w, m, v, g, name, tr=256):
    L, R, C = w.shape
    tr = min(tr, R)
    stacked = not isinstance(g, (list, tuple))
    n_g = None if stacked else [len(ps) for ps in g]
    flat = [g] if stacked else [a for ps in g for a in ps]

    def body(*refs):
        w_ref, m_ref, v_ref = refs[:3]
        g_refs = refs[3:3 + len(flat)]
        go_ref, d_ref, mo_ref, vo_ref = refs[3 + len(flat):]
        if stacked:
            gg = g_refs[0][...]
        else:
            layer = pl.program_id(0)
            gg = None
            k = 0
            for li in range(L):
                gl = None
                for _ in range(n_g[li]):
                    x = g_refs[k][...]
                    gl = x if gl is None else gl + x
                    k += 1
                gg = gl if gg is None else jnp.where(layer == li, gl, gg)
        m2 = ADAM_B1 * m_ref[...] + (1.0 - ADAM_B1) * gg
        v2 = ADAM_B2 * v_ref[...] + (1.0 - ADAM_B2) * (gg * gg)
        m_hat = m2 / (1.0 - ADAM_B1 ** ADAM_STEP)
        v_hat = v2 / (1.0 - ADAM_B2 ** ADAM_STEP)
        go_ref[...] = gg
        d_ref[...] = -ADAM_LR * (m_hat / (jnp.sqrt(v_hat) + ADAM_EPS) + ADAM_WD * w_ref[...])
        mo_ref[...] = m2
        vo_ref[...] = v2

    big = pl.BlockSpec((None, tr, C), lambda l, i: (l, i, 0))
    g_specs = [big] if stacked else [pl.BlockSpec((tr, C), lambda l, i: (i, 0))] * len(flat)
    return pl.pallas_call(
        body, name=name, grid=(L, R // tr),
        in_specs=[big, big, big] + g_specs,
        out_specs=[big, big, big, big],
        out_shape=[jax.ShapeDtypeStruct((L, R, C), F32)] * 4,
        compiler_params=_params(2, 48),
    )(w, m, v, *flat)


GATHER8_SEMS = [pltpu.SemaphoreType.DMA((7,)), pltpu.SemaphoreType.DMA((7,)), pltpu.SemaphoreType.DMA]


def _gather8(x_ref, out_ref, send_sems, recv_sems, local_sem):
    m = x_ref.shape[0]
    x, y, c = _place()
    me, sibling = (x, y, c), (x, y, 1 - c)
    chips = [(1 - x, y), (x, 1 - y), (1 - x, 1 - y)]

    def rows(px, py, pc):
        return out_ref.at[pl.ds((4 * px + 2 * py + pc) * m, m), :]

    def copy(k, block, to, src=None):
        return pltpu.make_async_remote_copy(
            src_ref=rows(*block) if src is None else src, dst_ref=rows(*block),
            send_sem=send_sems.at[k], recv_sem=recv_sems.at[k], device_id=to, device_id_type=MESH)

    mine = pltpu.make_async_copy(x_ref, rows(*me), local_sem)
    mine.start()
    first = [copy(0, me, sibling, src=x_ref)]
    first += [copy(1 + j, me, (*chip, c), src=x_ref) for j, chip in enumerate(chips)]
    for cp in first:
        cp.start()
    passed = [copy(4 + j, (*chip, c), sibling) for j, chip in enumerate(chips)]
    for j, chip in enumerate(chips):
        copy(1 + j, (*chip, c), me).wait_recv()
        passed[j].start()
    copy(0, sibling, me).wait_recv()
    for j, chip in enumerate(chips):
        copy(4 + j, (*chip, 1 - c), me).wait_recv()
    for cp in first + passed:
        cp.wait_send()
    mine.wait()


def _allgather8(xs, name, comm=None):
    m, n = xs.shape
    n_c = 0 if comm is None else comm.n

    def body(*refs):
        x_ref, out_ref = refs[0], refs[1 + n_c]
        c_refs = (refs[1:1 + n_c], refs[2 + n_c:2 + 2 * n_c]) + tuple(refs[5 + 2 * n_c:])
        if comm is not None:
            comm.start(*c_refs)
        _gather8(x_ref, out_ref, *refs[2 + 2 * n_c:5 + 2 * n_c])
        if comm is not None:
            comm.wait(*c_refs)

    vmem = pl.BlockSpec(memory_space=pltpu.VMEM)
    outs = pl.pallas_call(
        body, name=name,
        out_shape=[jax.ShapeDtypeStruct((8 * m, n), xs.dtype)] + ([] if comm is None else comm.out_shape),
        in_specs=[vmem] + [ANY] * n_c,
        out_specs=[vmem] + [ANY] * n_c,
        scratch_shapes=GATHER8_SEMS + ([] if comm is None else comm.scratch),
        compiler_params=pltpu.CompilerParams(vmem_limit_bytes=32 * 2 ** 20),
    )(xs, *([] if comm is None else comm.arrays))
    return outs[0], list(outs[1:])


def _prologue(c8, ada_w, ada_b, kv_ada_w, kv_ada_b, extra, comm, name):
    L, _, n = ada_w.shape
    k = kv_ada_w.shape[1]
    e = extra.shape[1]
    width = L * n + k + e
    n_c = comm.n

    def body(*refs):
        c_ref, w_hbm, b_ref, kw_hbm, kb_ref, x_ref = refs[:6]
        c_in = refs[6:6 + n_c]
        ca_ref, out_ref = refs[6 + n_c:8 + n_c]
        c_out = refs[8 + n_c:8 + 2 * n_c]
        cbuf, wbuf, kbuf, part, wsems = refs[8 + 2 * n_c:13 + 2 * n_c]
        sems_a = refs[13 + 2 * n_c:16 + 2 * n_c]
        sems_b = refs[16 + 2 * n_c:19 + 2 * n_c]
        c_refs = (c_in, c_out) + tuple(refs[19 + 2 * n_c:])
        comm.start(*c_refs)

        def fetch(l):
            return pltpu.make_async_copy(w_hbm.at[l], wbuf.at[l % 2], wsems.at[l % 2])

        fetch(0).start()
        kv_copy = pltpu.make_async_copy(kw_hbm, kbuf, wsems.at[2])
        kv_copy.start()
        _gather8(c_ref, cbuf, *sems_a)
        cc = jnp.concatenate([cbuf[SUBLANES * d:SUBLANES * d + 1, :] for d in range(8)], axis=0)
        ca = cc * jax.nn.sigmoid(cc)
        ca_ref[...] = ca
        cab = ca.astype(BF16)
        for l in range(L):
            fetch(l).wait()
            if l + 1 < L:
                fetch(l + 1).start()
            part[:, l * n:(l + 1) * n] = _nn(cab, wbuf[l % 2].astype(BF16)) + b_ref[l]
        kv_copy.wait()
        part[:, L * n:L * n + k] = _nn(cab, kbuf[...].astype(BF16)) + kb_ref[...]
        part[:, L * n + k:] = jnp.broadcast_to(x_ref[...], (SUBLANES, e))
        _gather8(part, out_ref, *sems_b)
        comm.wait(*c_refs)

    vmem = pl.BlockSpec(memory_space=pltpu.VMEM)
    outs = pl.pallas_call(
        body, name=name,
        out_shape=[jax.ShapeDtypeStruct((SUBLANES, D), F32), jax.ShapeDtypeStruct((8 * SUBLANES, width), F32)] + comm.out_shape,
        in_specs=[vmem, ANY, vmem, ANY, vmem, vmem] + [ANY] * n_c,
        out_specs=[vmem, vmem] + [ANY] * n_c,
        scratch_shapes=[pltpu.VMEM((8 * SUBLANES, D), F32), pltpu.VMEM((2, D, n), F32), pltpu.VMEM((D, k), F32),
                        pltpu.VMEM((SUBLANES, width), F32), pltpu.SemaphoreType.DMA((3,))] + GATHER8_SEMS + GATHER8_SEMS
        + comm.scratch,
        compiler_params=pltpu.CompilerParams(vmem_limit_bytes=32 * 2 ** 20),
    )(c8, ada_w, ada_b, kv_ada_w, kv_ada_b, extra, *comm.arrays)
    return outs[0], outs[1], list(outs[2:])


def _pad8(a):
    return jnp.pad(a, ((0, SUBLANES - a.shape[0]), (0, 0)))


def _group_rows(wg):
    return wg.transpose(1, 0, 2, 3).reshape(4, GW, GW)


def _example_step(h0, tgt, mods, kvmod, a_scale, norm_g, kv_norm_g, final_g, b_rel_bias, sh, w_first, chip_arr):
    ones_e = jnp.ones((1, E), F32)
    shift = [mods[l:l + 1, 0:D] for l in range(4)]
    scale = [mods[l:l + 1, D:2 * D] for l in range(4)]
    gate = [mods[l:l + 1, 2 * D:3 * D] for l in range(4)]
    gl = [norm_g[l:l + 1] for l in range(4)]
    kv_shift, kv_scale = kvmod[None, 0:D], kvmod[None, D:2 * D]
    kv_g = kv_norm_g[None]

    w_a = w_first
    hs = [h0]
    saved = []
    nxt = [[sh["a_in"][1], sh["a_grp"][1], sh["a_out"][1], sh["kv"][0]],
           [sh["b_in"][0], sh["b_out"][0], sh["b_in"][1], sh["b_out"][1]]]
    for l in range(2):
        w_in_l, wg_l, wo_l = w_a
        wg_full = _group_rows(wg_l)
        (u, z, pooled, mixed, y, hn), got = _a_fwd(hs[-1], gl[l], shift[l], scale[l], a_scale[l:l + 1], gate[l], w_in_l,
                                                   wg_full, wo_l, f"a{l}_fwd", comm=_Comm(gathers=nxt[l]))
        saved.append((u, z, pooled, mixed, y, w_in_l, wg_full, wo_l))
        hs.append(hn)
        if l == 0:
            w_a, w_kv = got[:3], got[3]
        else:
            wb_in = [got[0], got[2]]
            wb_out = [got[1], got[3]]

    (uk, kp, vp), _ = _in_fwd(hs[2], kv_g, kv_shift, kv_scale, w_kv, BF16, BF16, "kv_in_fwd", pad_rows=PAD)

    for bi in range(2):
        l = 2 + bi
        sa, sb = _bias_build(jnp.pad(b_rel_bias[bi], ((0, 0), (0, NRELP - NREL))), f"b{bi}_bias")
        (u, q, z), _ = _in_fwd(hs[-1], gl[l], shift[l], scale[l], wb_in[bi], BF16, BF16, f"b{bi}_in_fwd")
        (att, probs), _ = _attn_fwd(q, kp, vp, sa.transpose(1, 0, 2), sb.transpose(1, 0, 2), f"b{bi}_attn_fwd")
        if bi == 0:
            (y, hn), _ = _out_fwd(att, z, wb_out[bi], gate[l], hs[-1], f"b{bi}_out_fwd")
            hs.append(hn)
        else:
            (y, dh, st_fin), _ = _out_fwd(att, z, wb_out[bi], gate[l], hs[-1], f"b{bi}_out_fwd", head=(final_g[None], tgt))
        saved.append((u, z, q, att, y, probs))

    st_in = [None] * 4
    st_out = [None] * 4
    grads = {}
    landed = {}

    def carry(names):
        return _Comm(scatters=[grads[n] for n in names]) if names else None

    def land(names, got):
        for n, a in zip(names, got):
            landed[n] = a

    u, z, q, att, y, probs = saved[3]
    (datt, dz, grads["b_out1"], st_out[3]), _ = _out_bwd(dh, y, gate[3], att, ones_e, z, wb_out[1], "b1_out_bwd")
    (dq, dk1, dv1, dsa, dsb), _ = _attn_bwd(q, kp, vp, probs, datt, None, "b1_attn_bwd")
    drb1 = _dbias_reduce(dsa.transpose(1, 0, 2), dsb.transpose(1, 0, 2), "b1_dbias")
    (dh, grads["b_in1"], st_in[3]), _ = _in_bwd(dq, dz, 0, u, hs[3], gl[3], scale[3], wb_in[1], dh, "b1_in_bwd")
    u, z, q, att, y, probs = saved[2]
    (datt, dz, grads["b_out0"], st_out[2]), _ = _out_bwd(dh, y, gate[2], att, ones_e, z, wb_out[0], "b0_out_bwd")
    (dq, dk, dv, dsa, dsb), got = _attn_bwd(q, kp, vp, probs, datt, (dk1, dv1), "b0_attn_bwd",
                                            comm=carry(["b_out1", "b_in1", "b_out0"]))
    land(["b_out1", "b_in1", "b_out0"], got)
    drb0 = _dbias_reduce(dsa.transpose(1, 0, 2), dsb.transpose(1, 0, 2), "b0_dbias")
    (dh, grads["b_in0"], st_in[2]), _ = _in_bwd(dq, dz, 0, u, hs[2], gl[2], scale[2], wb_in[0], dh, "b0_in_bwd")
    (dh, grads["kv"], st_kv), got = _in_bwd(dk, dv, PAD, uk, hs[2], kv_g, kv_scale, w_kv, dh, "kv_in_bwd",
                                            comm=carry(["b_in0"]))
    land(["b_in0"], got)
    st_pool = [None] * 2
    plan = {1: dict(o=[], p=[], i=["kv", "a_out1", "a_grp1"]), 0: dict(o=["a_in1"], p=["a_out0"], i=[])}
    early = ["b_out1", "b_in1", "b_out0", "b_in0", "kv", "a_out1", "a_grp1", "a_in1"]
    late = ["a_out0", "a_grp0", "a_in0"]
    both = {}

    def sum4(n):
        return _sum4(grads[n], landed[n], chip_arr, f"sum4_{n}")

    for l in (1, 0):
        u, z, pooled, mixed, y, w_in_l, wg_full, wo = saved[l]
        asl = a_scale[l:l + 1]
        (dms, dz, grads[f"a_out{l}"], st_out[l]), got = _out_bwd(dh, y, gate[l], mixed, asl, z, wo, f"a{l}_out_bwd",
                                                                comm=carry(plan[l]["o"]))
        land(plan[l]["o"], got)
        comm = carry(plan[l]["p"])
        if l == 0:
            mine = [sum4(n) for n in early]
            comm = _Comm(scatters=[grads[n] for n in plan[l]["p"]], swaps=mine)
        (dval, dwg, st_pool[l]), got = _pool_bwd(dms, mixed, pooled, wg_full, asl, f"a{l}_pool_bwd", comm=comm)
        land(plan[l]["p"], got)
        if l == 0:
            both.update({n: [a, b] for n, a, b in zip(early, mine, got[len(plan[l]["p"]):])})
        grads[f"a_grp{l}"] = (dwg.reshape(4, NCHIP, GW // NCHIP, GW).transpose(1, 0, 2, 3).reshape(NCHIP, GW, GW)
                              .astype(BF16))
        (dh, grads[f"a_in{l}"], st_in[l]), got = _in_bwd(dval, dz, 0, u, hs[l], gl[l], scale[l], w_in_l, dh, f"a{l}_in_bwd",
                                                         comm=carry(plan[l]["i"]))
        land(plan[l]["i"], got)
    pieces = st_in + [st_kv] + st_out + [st_fin]
    pieces += [_pad8(st_pool[l][0].reshape(2, D)) for l in range(2)]
    pieces += [_pad8(d.reshape(NH * NRELP // D, D)) for d in (drb0, drb1)]
    gathered, got = _allgather8(jnp.concatenate(pieces, axis=0), "gather_stats", comm=carry(["a_grp0", "a_in0"]))
    land(["a_grp0", "a_in0"], got)
    mine = [sum4(n) for n in late]
    both.update({n: [a, b] for n, a, b in zip(late, mine, _comm_only(_Comm(swaps=mine), "swap_last"))})
    return dh, both, gathered.reshape(8, N_STAT, D)


ROW_IN = [8 * l for l in range(4)]
ROW_KV = 32
ROW_OUT = [40 + 8 * l for l in range(4)]
ROW_FIN = 72
ROW_ASC = [80, 88]
ROW_RB = [96, 104]
N_STAT = 112


def kernel(x, c, ada_w, ada_b, norm_g, a_w_in, a_w_group, a_scale, a_w_out, kv_norm_g, kv_ada_w, kv_ada_b, w_kv, b_w_in, b_rel_bias, b_w_out, final_g, loss_target, m_ada_w, m_ada_b, m_norm_g, m_a_w_in, m_a_w_group, m_a_scale, m_a_w_out, m_kv_norm_g, m_kv_ada_w, m_kv_ada_b, m_w_kv, m_b_w_in, m_b_rel_bias, m_b_w_out, m_final_g, v_ada_w, v_ada_b, v_norm_g, v_a_w_in, v_a_w_group, v_a_scale, v_a_w_out, v_kv_norm_g, v_kv_ada_w, v_kv_ada_b, v_w_kv, v_b_w_in, v_b_rel_bias, v_b_w_out, v_final_g):
    xi, yi, ci = _place()
    chip = 2 * xi + yi
    dev = 4 * xi + 2 * yi + ci
    n_ada = ada_w.shape[2]
    n_kva = kv_ada_w.shape[1]
    n_asc = a_scale.shape[1]

    ada_b_sh = lax.dynamic_slice_in_dim(ada_b, chip * n_ada, n_ada, axis=1)
    kvb_sh = lax.dynamic_slice_in_dim(kv_ada_b, chip * n_kva, n_kva, axis=0)
    sh = dict(a_in=[a_w_in[l].astype(BF16) for l in range(2)], a_grp=[a_w_group[l].astype(BF16) for l in range(2)],
              a_out=[a_w_out[l].astype(BF16) for l in range(2)], kv=[w_kv.astype(BF16)],
              b_in=[b_w_in[l].astype(BF16) for l in range(2)], b_out=[b_w_out[l].astype(BF16) for l in range(2)])
    c_act, gathered, w_first = _prologue(
        jnp.broadcast_to(c, (SUBLANES, D)), ada_w, ada_b_sh[:, None, :], kv_ada_w, kvb_sh[None, :],
        a_scale.reshape(1, 2 * n_asc), _Comm(gathers=[sh["a_in"][0], sh["a_grp"][0], sh["a_out"][0]]), "prologue")
    rows = jnp.concatenate([lax.dynamic_slice_in_dim(gathered, SUBLANES * (2 * p + ci) + dev, 1, axis=0)
                            for p in range(NCHIP)], axis=0)
    mods = jnp.stack([rows[:, l * n_ada:(l + 1) * n_ada].reshape(3 * D) for l in range(4)])
    kvmod = rows[:, 4 * n_ada:4 * n_ada + n_kva].reshape(2 * D)
    o_asc = 4 * n_ada + n_kva
    a_scale_full = jnp.stack([rows[:, o_asc + l * n_asc:o_asc + (l + 1) * n_asc].reshape(E) for l in range(2)])

    chip_arr = jnp.reshape(chip, (1,)).astype(jnp.int32)
    dh, both, g3 = _example_step(x[0], loss_target[0], mods, kvmod, a_scale_full, norm_g, kv_norm_g, final_g,
                                 b_rel_bias, sh, w_first, chip_arr)
    grad_x = dh[None]

    red, loss_tile = _stats_reduce(g3, ROW_FIN + 1, "stats_reduce")
    loss = loss_tile[0, 0]

    def cat(rows_):
        return jnp.concatenate(rows_, axis=-1)

    g_ada_b = jnp.stack([cat([red[ROW_IN[l]], red[ROW_IN[l] + 1], red[ROW_OUT[l]]]) for l in range(4)])
    g_norm_g = jnp.stack([red[ROW_IN[l] + 2] for l in range(4)])
    g_kv_norm_g = red[ROW_KV + 2]
    g_kv_ada_b = cat([red[ROW_KV], red[ROW_KV + 1]])
    g_final_g = red[ROW_FIN]
    g_asc_full = jnp.stack([red[ROW_ASC[l]:ROW_ASC[l] + 2].reshape(E) for l in range(2)])
    g_a_scale = lax.dynamic_slice_in_dim(g_asc_full, chip * n_asc, n_asc, axis=1)
    g_rel = jnp.stack([red[ROW_RB[bi]:ROW_RB[bi] + NH * NRELP // D].reshape(NH, NRELP)[:, :NREL] for bi in range(2)])

    dmod = jnp.stack([cat([g3[:, ROW_IN[l]], g3[:, ROW_IN[l] + 1], g3[:, ROW_OUT[l]]]) for l in range(4)])
    dmod_sh = lax.dynamic_slice_in_dim(dmod, chip * n_ada, n_ada, axis=2)
    dkv = cat([g3[:, ROW_KV], g3[:, ROW_KV + 1]])[None]
    dkv_sh = lax.dynamic_slice_in_dim(dkv, chip * n_kva, n_kva, axis=2)
    c_act_t = c_act.T
    g_ada_w = _grad_ada(c_act_t, dmod_sh, "grad_ada_w")
    g_kv_ada_w = _grad_ada(c_act_t, dkv_sh, "grad_kv_ada_w")

    def upd(w, m, v, g, name, shape3):
        g = g.reshape(shape3) if not isinstance(g, list) else g
        outs = _adamw(w.reshape(shape3), m.reshape(shape3), v.reshape(shape3), g, name)
        return [o.reshape(w.shape) for o in outs]

    def pair(name):
        return [both[name + "0"], both[name + "1"]]

    res = {}
    res["ada_w"] = upd(ada_w, m_ada_w, v_ada_w, g_ada_w, "adamw_ada_w", ada_w.shape)
    res["ada_b"] = upd(ada_b, m_ada_b, v_ada_b, g_ada_b, "adamw_ada_b", (1,) + ada_b.shape)
    res["norm_g"] = upd(norm_g, m_norm_g, v_norm_g, g_norm_g, "adamw_norm_g", (1,) + norm_g.shape)
    res["a_w_in"] = upd(a_w_in, m_a_w_in, v_a_w_in, pair("a_in"), "adamw_a_w_in", a_w_in.shape)
    res["a_w_group"] = upd(a_w_group, m_a_w_group, v_a_w_group, pair("a_grp"), "adamw_a_w_group", (2, GW, GW))
    res["a_scale"] = upd(a_scale, m_a_scale, v_a_scale, g_a_scale, "adamw_a_scale", (1,) + a_scale.shape)
    res["a_w_out"] = upd(a_w_out, m_a_w_out, v_a_w_out, pair("a_out"), "adamw_a_w_out", a_w_out.shape)
    res["kv_norm_g"] = upd(kv_norm_g, m_kv_norm_g, v_kv_norm_g, g_kv_norm_g, "adamw_kv_norm_g", (1, 1, D))
    res["kv_ada_w"] = upd(kv_ada_w, m_kv_ada_w, v_kv_ada_w, g_kv_ada_w, "adamw_kv_ada_w", (1,) + kv_ada_w.shape)
    res["kv_ada_b"] = upd(kv_ada_b, m_kv_ada_b, v_kv_ada_b, g_kv_ada_b, "adamw_kv_ada_b", (1, 1, 2 * D))
    res["w_kv"] = upd(w_kv, m_w_kv, v_w_kv, [both["kv"]], "adamw_w_kv", (1,) + w_kv.shape)
    res["b_w_in"] = upd(b_w_in, m_b_w_in, v_b_w_in, pair("b_in"), "adamw_b_w_in", b_w_in.shape)
    res["b_rel_bias"] = upd(b_rel_bias, m_b_rel_bias, v_b_rel_bias, g_rel, "adamw_b_rel_bias", (1, 2 * NH, NREL))
    res["b_w_out"] = upd(b_w_out, m_b_w_out, v_b_w_out, pair("b_out"), "adamw_b_w_out", b_w_out.shape)
    res["final_g"] = upd(final_g, m_final_g, v_final_g, g_final_g, "adamw_final_g", (1, 1, D))

    names = ["ada_w", "ada_b", "norm_g", "a_w_in", "a_w_group", "a_scale", "a_w_out", "kv_norm_g", "kv_ada_w", "kv_ada_b",
             "w_kv", "b_w_in", "b_rel_bias", "b_w_out", "final_g"]
    return (loss, grad_x, *[res[n][0] for n in names], *[res[n][1] for n in names], *[res[n][2] for n in names],
            *[res[n][3] for n in names])
```

```python
import math

import jax
import jax.numpy as jnp
from jax import lax
from jax.experimental import pallas as pl
from jax.experimental.pallas import tpu as pltpu

F32 = jnp.float32
BF16 = jnp.bfloat16

D = 1024
E = 2048
NH = 16
HD = 128
CHUNK = 64
LEFT = 8
PAD = LEFT * CHUNK
NREL = 257
NRELP = 384
REL_CLIP = 128
EPS = 1e-6
NEG = -1e30
LOG2E = math.log2(math.e)
SM_SCALE = HD ** -0.5
POOL_W = (2, 4, 8, 16)
GW = 512
HALO = 16
QC = 4
QB = QC * CHUNK
NMASK = PAD // QB
WIN = (QC + LEFT) * CHUNK
BW = (LEFT + 2) * CHUNK
DBW = 4 * CHUNK
NSUB = 8
NCHIP = 4
LANES = 128
SUBLANES = 8

ADAM_LR = 0.001
ADAM_B1 = 0.9
ADAM_B2 = 0.999
ADAM_EPS = 1e-08
ADAM_WD = 0.01
ADAM_STEP = 10

MESH = pl.DeviceIdType.MESH
ANY = pl.BlockSpec(memory_space=pl.ANY)


def _params(n_axes, vmem_mb):
    return pltpu.CompilerParams(dimension_semantics=("arbitrary",) * n_axes, vmem_limit_bytes=vmem_mb * 2 ** 20)


def _nn(a, b):
    return jnp.dot(a, b, preferred_element_type=F32)


def _nt(a, b):
    return lax.dot_general(a, b, (((1,), (1,)), ((), ())), preferred_element_type=F32)


def _tn(a, b):
    return lax.dot_general(a, b, (((0,), (0,)), ((), ())), preferred_element_type=F32)


def _row(n):
    return pl.BlockSpec((1, n), lambda i: (0, 0))


def _colsum(x):
    return jnp.sum(x, axis=0, keepdims=True)


def _place():
    return lax.axis_index("x"), lax.axis_index("y"), lax.axis_index("c")


class _Comm:
    def __init__(self, gathers=(), scatters=(), swaps=()):
        self.n_g = len(gathers)
        self.n_chip = len(gathers) + len(scatters)
        self.n_sw = len(swaps)
        self.arrays = list(gathers) + list(scatters) + list(swaps)
        self.n = len(self.arrays)
        self.half = [a.shape[0] // 2 for a in gathers]
        self.out_shape = ([jax.ShapeDtypeStruct((NCHIP,) + a.shape, a.dtype) for a in gathers]
                          + [jax.ShapeDtypeStruct((3,) + a.shape[1:], a.dtype) for a in scatters]
                          + [jax.ShapeDtypeStruct(a.shape, a.dtype) for a in swaps])
        n_c, n_f, n_s = max(3 * self.n_chip, 1), max(3 * self.n_g, 1), max(self.n_sw, 1)
        self.scratch = [pltpu.SemaphoreType.DMA((n_c,)), pltpu.SemaphoreType.DMA((n_c,)),
                        pltpu.SemaphoreType.DMA((max(self.n_g, 1),)), pltpu.SemaphoreType.DMA((n_f,)),
                        pltpu.SemaphoreType.DMA((n_f,)), pltpu.SemaphoreType.DMA((n_s,)), pltpu.SemaphoreType.DMA((n_s,))]

    def _chip_copies(self, ins, outs, send, recv, landing):
        x, y, c = _place()
        chips = [(1 - x, y), (x, 1 - y), (1 - x, 1 - y)]
        mine = 2 * x + y
        cps = []
        for k in range(self.n_chip):
            for j, (cx, cy) in enumerate(chips):
                q = 2 * cx + cy
                if k < self.n_g:
                    part = pl.ds(c * self.half[k], self.half[k])
                    src = ins[k].at[part]
                    dst = outs[k].at[q if landing else mine, part]
                else:
                    src = ins[k].at[q]
                    dst = outs[k].at[j]
                cps.append(pltpu.make_async_remote_copy(
                    src_ref=src, dst_ref=dst, send_sem=send.at[3 * k + j], recv_sem=recv.at[3 * k + j],
                    device_id=(cx, cy, c), device_id_type=MESH))
        return cps

    def _core_copies(self, outs, fsend, frecv, landing):
        x, y, c = _place()
        chips = [(1 - x, y), (x, 1 - y), (1 - x, 1 - y)]
        cps = []
        for k in range(self.n_g):
            for j, (cx, cy) in enumerate(chips):
                part = pl.ds((1 - c if landing else c) * self.half[k], self.half[k])
                blk = outs[k].at[2 * cx + cy, part]
                cps.append(pltpu.make_async_remote_copy(
                    src_ref=blk, dst_ref=blk, send_sem=fsend.at[3 * k + j], recv_sem=frecv.at[3 * k + j],
                    device_id=(x, y, 1 - c), device_id_type=MESH))
        return cps

    def _local_copies(self, ins, outs, loc):
        x, y, _ = _place()
        return [pltpu.make_async_copy(ins[k], outs[k].at[2 * x + y], loc.at[k]) for k in range(self.n_g)]

    def _swap_copies(self, ins, outs, ssend, srecv):
        x, y, c = _place()
        return [pltpu.make_async_remote_copy(
            src_ref=ins[k], dst_ref=outs[k], send_sem=ssend.at[k - self.n_chip], recv_sem=srecv.at[k - self.n_chip],
            device_id=(x, y, 1 - c), device_id_type=MESH) for k in range(self.n_chip, self.n)]

    def start(self, ins, outs, send, recv, loc, fsend, frecv, ssend, srecv):
        for cp in (self._local_copies(ins, outs, loc) + self._chip_copies(ins, outs, send, recv, False)
                   + self._swap_copies(ins, outs, ssend, srecv)):
            cp.start()

    def wait(self, ins, outs, send, recv, loc, fsend, frecv, ssend, srecv):
        lands = self._chip_copies(ins, outs, send, recv, True)
        passes = self._core_copies(outs, fsend, frecv, False)
        for k in range(self.n_chip):
            for j in range(3):
                lands[3 * k + j].wait_recv()
                if k < self.n_g:
                    passes[3 * k + j].start()
        for cp in self._core_copies(outs, fsend, frecv, True):
            cp.wait_recv()
        swaps = self._swap_copies(ins, outs, ssend, srecv)
        for cp in swaps:
            cp.wait_recv()
        for cp in self._chip_copies(ins, outs, send, recv, False) + passes + swaps:
            cp.wait_send()
        for cp in self._local_copies(ins, outs, loc):
            cp.wait()


def _call(body, name, grid, in_specs, out_specs, out_shape, scratch, params, args, comm=None):
    n_in, n_out, n_sc = len(in_specs), len(out_specs), len(scratch)
    if comm is None:
        outs = pl.pallas_call(body, name=name, grid=grid, in_specs=in_specs, out_specs=out_specs, out_shape=out_shape,
                              scratch_shapes=scratch, compiler_params=params)(*args)
        return list(outs), []
    n = comm.n
    o0 = n_in + n
    s0 = o0 + n_out + n

    def wrapped(*refs):
        c_refs = (refs[n_in:o0], refs[o0 + n_out:s0]) + tuple(refs[s0 + n_sc:])
        ids = [pl.program_id(a) for a in range(len(grid))]
        first = ids[0] == 0
        last = ids[0] == grid[0] - 1
        for a in range(1, len(grid)):
            first = first & (ids[a] == 0)
            last = last & (ids[a] == grid[a] - 1)

        @pl.when(first)
        def _():
            comm.start(*c_refs)

        body(*refs[:n_in], *refs[o0:o0 + n_out], *refs[s0:s0 + n_sc])

        @pl.when(last)
        def _():
            comm.wait(*c_refs)

    outs = pl.pallas_call(
        wrapped, name=name, grid=grid, in_specs=list(in_specs) + [ANY] * n, out_specs=list(out_specs) + [ANY] * n,
        out_shape=list(out_shape) + comm.out_shape, scratch_shapes=list(scratch) + comm.scratch, compiler_params=params,
    )(*args, *comm.arrays)
    return list(outs[:n_out]), list(outs[n_out:])


def _comm_only(comm, name):
    def body(*refs):
        c_refs = (refs[:comm.n], refs[comm.n:2 * comm.n]) + tuple(refs[2 * comm.n:])
        comm.start(*c_refs)
        comm.wait(*c_refs)

    return pl.pallas_call(body, name=name, in_specs=[ANY] * comm.n, out_specs=[ANY] * comm.n, out_shape=comm.out_shape,
                          scratch_shapes=comm.scratch)(*comm.arrays)


def _in_fwd(h, g, shift, scale, w, dt_a, dt_b, name, pad_rows=0, comm=None, tm=512):
    S = h.shape[0]
    n_pad = pad_rows // tm

    def body(h_ref, g_ref, sh_ref, sc_ref, w_hbm, u_ref, oa_ref, ob_ref, w_v, sem):
        i = pl.program_id(0)

        @pl.when(i == 0)
        def _():
            cp = pltpu.make_async_copy(w_hbm, w_v, sem)
            cp.start()
            cp.wait()

        hh = h_ref[...]
        r = lax.rsqrt(jnp.mean(hh * hh, axis=-1, keepdims=True) + EPS)
        u = (hh * r * g_ref[...]) * (1.0 + sc_ref[...]) + sh_ref[...]
        ub = u.astype(BF16)
        u_ref[...] = ub
        for q in range(NCHIP):
            o_ref = oa_ref if q < 2 else ob_ref
            o_ref[:, (q % 2) * D:(q % 2 + 1) * D] = _nn(ub, w_v[q]).astype(o_ref.dtype)

        if n_pad:
            @pl.when(i < n_pad)
            def _():
                oa_ref[...] = jnp.zeros(oa_ref.shape, oa_ref.dtype)
                ob_ref[...] = jnp.zeros(ob_ref.shape, ob_ref.dtype)

    def src(i):
        return (jnp.maximum(i - n_pad, 0), 0)

    outs, landed = _call(
        body, name, (S // tm + n_pad,),
        [pl.BlockSpec((tm, D), src), _row(D), _row(D), _row(D), ANY],
        [pl.BlockSpec((tm, D), src), pl.BlockSpec((tm, E), lambda i: (i, 0)), pl.BlockSpec((tm, E), lambda i: (i, 0))],
        [jax.ShapeDtypeStruct((S, D), BF16), jax.ShapeDtypeStruct((S + pad_rows, E), dt_a),
         jax.ShapeDtypeStruct((S + pad_rows, E), dt_b)],
        [pltpu.VMEM((NCHIP, D, D), BF16), pltpu.SemaphoreType.DMA],
        _params(1, 52), (h, g, shift, scale, w), comm)
    return outs, landed


def _a_fwd(h, g, shift, scale, asc, gate, w_in, wg, w_out, name, comm=None, tm=512):
    S = h.shape[0]

    def body(h_ref, g_ref, sh_ref, sc_ref, as_ref, gate_ref, wi_hbm, wg_hbm, wo_hbm,
             u_ref, z_ref, p_ref, m_ref, y_ref, ho_ref, wi_v, wg_v, wo_v, buf, sems):
        i = pl.program_id(0)

        @pl.when(i == 0)
        def _():
            cps = [pltpu.make_async_copy(wi_hbm, wi_v, sems.at[0]), pltpu.make_async_copy(wg_hbm, wg_v, sems.at[1]),
                   pltpu.make_async_copy(wo_hbm, wo_v, sems.at[2])]
            for cp in cps:
                cp.start()
            buf[0:HALO, :] = jnp.zeros((HALO, E), F32)
            for cp in cps:
                cp.wait()

        hh = h_ref[...]
        r = lax.rsqrt(jnp.mean(hh * hh, axis=-1, keepdims=True) + EPS)
        ub = ((hh * r * g_ref[...]) * (1.0 + sc_ref[...]) + sh_ref[...]).astype(BF16)
        u_ref[...] = ub
        for q in range(2):
            buf[HALO:HALO + tm, q * D:(q + 1) * D] = _nn(ub, wi_v[q])
        t = i * tm + lax.broadcasted_iota(jnp.int32, (tm, 1), 0)
        y = None
        for gi, w in enumerate(POOL_W):
            cols = slice(gi * GW, (gi + 1) * GW)
            x = buf[:, cols]
            s = x
            k = 1
            while k < w:
                s = s + pltpu.roll(s, k, 0)
                k *= 2
            inv_cnt = 1.0 / jnp.minimum(t + 1, w).astype(F32)
            pb = (s[HALO:, :] * inv_cnt - x[HALO:, :]).astype(BF16)
            p_ref[:, cols] = pb
            mb = _nn(pb, wg_v[gi]).astype(BF16)
            m_ref[:, cols] = mb
            zb = _nn(ub, wi_v[2 + gi // 2, :, (gi % 2) * GW:(gi % 2 + 1) * GW]).astype(BF16)
            z_ref[:, cols] = zb
            zz = zb.astype(F32)
            act = ((mb.astype(F32) * as_ref[:, cols]) * (zz * jax.nn.sigmoid(zz))).astype(BF16)
            part = _nn(act, wo_v[gi])
            y = part if y is None else y + part
        buf[0:HALO, :] = buf[tm:tm + HALO, :]
        y_ref[...] = y.astype(BF16)
        ho_ref[...] = hh + gate_ref[...] * y

    rows_d = pl.BlockSpec((tm, D), lambda i: (i, 0))
    rows_e = pl.BlockSpec((tm, E), lambda i: (i, 0))
    return _call(
        body, name, (S // tm,),
        [rows_d, _row(D), _row(D), _row(D), _row(E), _row(D), ANY, ANY, ANY],
        [rows_d, rows_e, rows_e, rows_e, rows_d, rows_d],
        [jax.ShapeDtypeStruct((S, D), BF16), jax.ShapeDtypeStruct((S, E), BF16), jax.ShapeDtypeStruct((S, E), BF16),
         jax.ShapeDtypeStruct((S, E), BF16), jax.ShapeDtypeStruct((S, D), BF16), jax.ShapeDtypeStruct((S, D), F32)],
        [pltpu.VMEM((NCHIP, D, D), BF16), pltpu.VMEM((4, GW, GW), BF16), pltpu.VMEM((NCHIP, GW, D), BF16),
         pltpu.VMEM((tm + HALO, E), F32), pltpu.SemaphoreType.DMA((3,))],
        _params(1, 60), (h, g, shift, scale, asc, gate, w_in, wg, w_out), comm)


def _out_fwd(a, z, w, gate, h, name, head=None, comm=None, tm=512):
    S = h.shape[0]
    kb = E // NCHIP
    n_in = 5 if head is None else 7

    def body(*refs):
        a_ref, z_ref, w_hbm, gate_ref, h_ref = refs[:5]
        w_v, sem = refs[-2:]
        i = pl.program_id(0)

        @pl.when(i == 0)
        def _():
            cp = pltpu.make_async_copy(w_hbm, w_v, sem)
            cp.start()
            cp.wait()

        y = None
        for p in range(NCHIP):
            cols = slice(p * kb, (p + 1) * kb)
            zz = z_ref[:, cols].astype(F32)
            act = (a_ref[:, cols].astype(F32) * (zz * jax.nn.sigmoid(zz))).astype(BF16)
            part = _nn(act, w_v[p])
            y = part if y is None else y + part
        refs[n_in][...] = y.astype(BF16)
        hh = h_ref[...] + gate_ref[...] * y
        if head is None:
            refs[n_in + 1][...] = hh
            return
        g_ref, t_ref = refs[5:7]
        dh_ref, st_ref = refs[n_in + 1:n_in + 3]

        @pl.when(i == 0)
        def _():
            st_ref[...] = jnp.zeros((SUBLANES, D), F32)

        r = lax.rsqrt(jnp.mean(hh * hh, axis=-1, keepdims=True) + EPS)
        xhat = hh * r
        diff = xhat * g_ref[...] - t_ref[...]
        st_ref[1:2, :] += _colsum(diff * diff)
        dout = diff * (1.0 / D)
        st_ref[0:1, :] += _colsum(dout * xhat)
        dx = dout * g_ref[...]
        dh_ref[...] = r * (dx - xhat * jnp.mean(dx * xhat, axis=-1, keepdims=True))

    rows_d = pl.BlockSpec((tm, D), lambda i: (i, 0))
    rows_e = pl.BlockSpec((tm, E), lambda i: (i, 0))
    in_specs = [rows_e, rows_e, ANY, _row(D), rows_d]
    out_specs = [rows_d, rows_d]
    out_shape = [jax.ShapeDtypeStruct((S, D), BF16), jax.ShapeDtypeStruct((S, D), F32)]
    args = (a, z, w, gate, h)
    if head is not None:
        in_specs += [_row(D), rows_d]
        out_specs += [pl.BlockSpec((SUBLANES, D), lambda i: (0, 0))]
        out_shape += [jax.ShapeDtypeStruct((SUBLANES, D), F32)]
        args += tuple(head)
    return _call(body, name, (S // tm,), in_specs, out_specs, out_shape,
                 [pltpu.VMEM((NCHIP, kb, D), BF16), pltpu.SemaphoreType.DMA], _params(1, 52), args, comm)


TW = BW + LANES


def _diag_onehot(transpose):
    shape = (TW, NRELP) if transpose else (NRELP, TW)
    j = lax.broadcasted_iota(jnp.int32, shape, 0 if transpose else 1)
    r = lax.broadcasted_iota(jnp.int32, shape, 1 if transpose else 0)
    idx = jnp.clip(PAD - (j - LANES), -REL_CLIP, REL_CLIP) + REL_CLIP
    return jnp.where(idx == r, 1.0, 0.0).astype(BF16)


def _strip_valid():
    m = lax.broadcasted_iota(jnp.int32, (NH, BW), 1)
    return m < (LEFT + 1) * CHUNK, m >= CHUNK


def _bias_build(rb, name):
    def body(rb_ref, a_ref, b_ref):
        x = rb_ref[...]
        hi = x.astype(BF16)
        r1 = x - hi.astype(F32)
        mid = r1.astype(BF16)
        lo = (r1 - mid.astype(F32)).astype(BF16)
        oh = _diag_onehot(False)
        diag = (_nn(hi, oh) + _nn(mid, oh)) + _nn(lo, oh)
        valid_a, valid_b = _strip_valid()
        for qi in range(CHUNK):
            a_ref[qi] = jnp.where(valid_a, pltpu.roll(diag, TW - (LANES - qi), 1)[:, :BW], NEG)
            b_ref[qi] = jnp.where(valid_b, pltpu.roll(diag, TW - (CHUNK - qi), 1)[:, :BW], NEG)

    vmem = pl.BlockSpec(memory_space=pltpu.VMEM)
    return pl.pallas_call(
        body, name=name, in_specs=[vmem], out_specs=[vmem, vmem],
        out_shape=[jax.ShapeDtypeStruct((CHUNK, NH, BW), F32), jax.ShapeDtypeStruct((CHUNK, NH, BW), F32)],
        compiler_params=pltpu.CompilerParams(vmem_limit_bytes=32 * 2 ** 20),
    )(rb)


def _dbias_reduce(dba, dbb, name):
    def body(a_ref, b_ref, o_ref):
        valid_a, valid_b = _strip_valid()
        zeros = jnp.zeros((NH, TW - BW), F32)
        acc = jnp.zeros((NH, TW), F32)
        for qi in range(CHUNK):
            xa = jnp.concatenate([jnp.where(valid_a, a_ref[qi], 0.0), zeros], axis=1)
            xb = jnp.concatenate([jnp.where(valid_b, b_ref[qi], 0.0), zeros], axis=1)
            acc = acc + (pltpu.roll(xa, LANES - qi, 1) + pltpu.roll(xb, CHUNK - qi, 1))
        oh = _diag_onehot(True)
        hi = acc.astype(BF16)
        mid = (acc - hi.astype(F32)).astype(BF16)
        r = lax.broadcasted_iota(jnp.int32, (NH, NRELP), 1)
        near = jnp.where(r < 2 * REL_CLIP, _nn(hi, oh) + _nn(mid, oh), 0.0)
        o_ref[...] = jnp.where(r == 2 * REL_CLIP, -jnp.sum(near, axis=-1, keepdims=True), near)

    vmem = pl.BlockSpec(memory_space=pltpu.VMEM)
    return pl.pallas_call(
        body, name=name, in_specs=[vmem, vmem], out_specs=vmem,
        out_shape=jax.ShapeDtypeStruct((NH, NRELP), F32),
        compiler_params=pltpu.CompilerParams(vmem_limit_bytes=32 * 2 ** 20),
    )(dba, dbb)


def _build_bias(bias3, ba_ref, bb_ref):
    bias3[NMASK] = jnp.full((QB, WIN), NEG, F32)
    for qc in range(QC):
        rows = slice(qc * CHUNK, (qc + 1) * CHUNK)
        if qc % 2 == 0:
            bias3[NMASK, rows, qc * CHUNK:qc * CHUNK + BW] = ba_ref[...] * LOG2E
        else:
            bias3[NMASK, rows, (qc - 1) * CHUNK:(qc - 1) * CHUNK + BW] = bb_ref[...] * LOG2E
    col = lax.broadcasted_iota(jnp.int32, (QB, WIN), 1)
    for sub in range(NMASK):
        bias3[sub] = jnp.where(col < PAD - sub * QB, NEG, bias3[NMASK])


def _nsub(S):
    n = min(NSUB, S // QB)
    assert S % (n * QB) == 0 and n >= NMASK
    return n


def _row0(i, sub, nsub):
    return pl.multiple_of((i * nsub + sub) * QB, QB)


def _scores(q_ref, k_ref, i, sub, nsub):
    return _nt(q_ref[sub * QB:(sub + 1) * QB, :], k_ref[pl.ds(_row0(i, sub, nsub), WIN), :])


HALF = QB // 2
LIVE = WIN - LANES


def _live(half):
    return slice(half * HALF, (half + 1) * HALF), slice(half * LANES, half * LANES + LIVE)


def _widen(x, half):
    zeros = jnp.zeros((HALF, LANES), x.dtype)
    return jnp.concatenate([x, zeros] if half == 0 else [zeros, x], axis=1)


def _probs(s, bias3, i, sub):
    which = jnp.where(i == 0, sub, NMASK) if sub < NMASK else NMASK
    out = []
    for half in range(2):
        rows, cols = _live(half)
        t = s[rows, cols] * (SM_SCALE * LOG2E) + bias3[which, rows, cols]
        e = jnp.exp2(t - jnp.max(t, axis=-1, keepdims=True))
        out.append(_widen((e * (1.0 / jnp.sum(e, axis=-1, keepdims=True))).astype(BF16), half))
    return jnp.concatenate(out, axis=0)


def _attn_fwd(q, kp, vp, ba, bb, name, comm=None):
    S = q.shape[0]
    nsub = _nsub(S)
    R = nsub * QB

    def body(q_ref, k_ref, v_ref, ba_ref, bb_ref, o_ref, p_ref, bias3):
        i = pl.program_id(1)

        @pl.when(i == 0)
        def _():
            _build_bias(bias3, ba_ref, bb_ref)

        s_next = _scores(q_ref, k_ref, i, 0, nsub)
        for sub in range(nsub):
            s = s_next
            if sub + 1 < nsub:
                s_next = _scores(q_ref, k_ref, i, sub + 1, nsub)
            pb = _probs(s, bias3, i, sub)
            p_ref[sub] = pb
            o_ref[sub * QB:(sub + 1) * QB, :] = _nn(pb, v_ref[pl.ds(_row0(i, sub, nsub), WIN), :]).astype(BF16)

    return _call(
        body, name, (NH, S // R),
        [pl.BlockSpec((R, HD), lambda h, i: (i, h)), pl.BlockSpec((S + PAD, HD), lambda h, i: (0, h)),
         pl.BlockSpec((S + PAD, HD), lambda h, i: (0, h)), pl.BlockSpec((None, CHUNK, BW), lambda h, i: (h, 0, 0)),
         pl.BlockSpec((None, CHUNK, BW), lambda h, i: (h, 0, 0))],
        [pl.BlockSpec((R, HD), lambda h, i: (i, h)), pl.BlockSpec((None, nsub, QB, WIN), lambda h, i: (h, i, 0, 0))],
        [jax.ShapeDtypeStruct((S, E), BF16), jax.ShapeDtypeStruct((NH, S // QB, QB, WIN), BF16)],
        [pltpu.VMEM((NMASK + 1, QB, WIN), F32)],
        _params(2, 48), (q, kp, vp, ba, bb), comm)


def _store_grad(acc, stage, dw_hbm, sem):
    for q in range(NCHIP):
        stage[...] = acc[q].astype(BF16)
        cp = pltpu.make_async_copy(stage, dw_hbm.at[q], sem)
        cp.start()
        cp.wait()


def _out_bwd(dh, y, gate, a, cs, z, w, name, comm=None, tm=256):
    S = dh.shape[0]
    kb = E // NCHIP
    cb = 256
    n_t = S // tm

    def body(dh_ref, y_ref, gate_ref, a_ref, cs_ref, z_ref, w_hbm, da_ref, dz_ref, dw_hbm, st_ref, w_v, acc, stage, sem):
        i = pl.program_id(0)

        @pl.when(i == 0)
        def _():
            cp = pltpu.make_async_copy(w_hbm, w_v, sem)
            cp.start()
            acc[...] = jnp.zeros(acc.shape, F32)
            st_ref[...] = jnp.zeros((SUBLANES, D), F32)
            cp.wait()

        dhh = dh_ref[...]
        st_ref[0:1, :] += _colsum(dhh * y_ref[...].astype(F32))
        dy = (dhh * gate_ref[...]).astype(BF16)
        for blk in range(E // cb):
            p, r0 = divmod(blk * cb, kb)
            cols = slice(blk * cb, (blk + 1) * cb)
            zz = z_ref[:, cols].astype(F32)
            sig = jax.nn.sigmoid(zz)
            sz = zz * sig
            ae = a_ref[:, cols].astype(F32) * cs_ref[:, cols]
            acc[p, r0:r0 + cb, :] += _tn((ae * sz).astype(BF16), dy)
            dact = _nt(dy, w_v[p, r0:r0 + cb, :])
            da_ref[:, cols] = (dact * sz).astype(BF16)
            dz_ref[:, cols] = (dact * ae * (sig * (1.0 + zz * (1.0 - sig)))).astype(BF16)

        @pl.when(i == n_t - 1)
        def _():
            _store_grad(acc, stage, dw_hbm, sem)

    return _call(
        body, name, (n_t,),
        [pl.BlockSpec((tm, D), lambda i: (i, 0)), pl.BlockSpec((tm, D), lambda i: (i, 0)), _row(D),
         pl.BlockSpec((tm, E), lambda i: (i, 0)), _row(E), pl.BlockSpec((tm, E), lambda i: (i, 0)), ANY],
        [pl.BlockSpec((tm, E), lambda i: (i, 0)), pl.BlockSpec((tm, E), lambda i: (i, 0)), ANY,
         pl.BlockSpec((SUBLANES, D), lambda i: (0, 0))],
        [jax.ShapeDtypeStruct((S, E), BF16), jax.ShapeDtypeStruct((S, E), BF16),
         jax.ShapeDtypeStruct((NCHIP, kb, D), BF16), jax.ShapeDtypeStruct((SUBLANES, D), F32)],
        [pltpu.VMEM((NCHIP, kb, D), BF16), pltpu.VMEM((NCHIP, kb, D), F32), pltpu.VMEM((kb, D), BF16),
         pltpu.SemaphoreType.DMA],
        _params(1, 52), (dh, y, gate, a, cs, z, w), comm)


def _attn_bwd(q, kp, vp, probs, do, prev, name, comm=None):
    S = q.shape[0]
    nsub = _nsub(S)
    R = nsub * QB
    n_i = S // R
    dt_kv = F32 if prev is None else BF16

    def body(*refs):
        q_ref, k_ref, v_ref, p_ref, do_ref = refs[:5]
        refs = refs[5:]
        if prev is not None:
            pk_hbm, pv_hbm = refs[:2]
            refs = refs[2:]
        dq_ref, dk_ref, dv_ref, dba_ref, dbb_ref, dbias, dk_acc, dv_acc = refs[:8]
        if prev is not None:
            pk_v, pv_v, sems = refs[8:]
        h = pl.program_id(0)
        i = pl.program_id(1)

        def prev_copies():
            cols = pl.ds(pl.multiple_of(h * HD, HD), HD)
            return (pltpu.make_async_copy(pk_hbm.at[:, cols], pk_v, sems.at[0]),
                    pltpu.make_async_copy(pv_hbm.at[:, cols], pv_v, sems.at[1]))

        @pl.when(i == 0)
        def _():
            if prev is not None:
                for cp in prev_copies():
                    cp.start()
            dbias[...] = jnp.zeros((2, CHUNK, DBW), F32)
            dk_acc[...] = jnp.zeros((S + PAD, HD), F32)
            dv_acc[...] = jnp.zeros((S + PAD, HD), F32)

        def mxu_in(sub):
            return _nt(do_ref[sub * QB:(sub + 1) * QB, :], v_ref[pl.ds(_row0(i, sub, nsub), WIN), :])

        nxt = mxu_in(0)
        for sub in range(nsub):
            rows = slice(sub * QB, (sub + 1) * QB)
            win = pl.ds(_row0(i, sub, nsub), WIN)
            dp = nxt
            if sub + 1 < nsub:
                nxt = mxu_in(sub + 1)
            parts = []
            for half in range(2):
                hrows, hcols = _live(half)
                p = p_ref[sub, hrows, hcols].astype(F32)
                dph = dp[hrows, hcols]
                ds = p * (dph - jnp.sum(p * dph, axis=-1, keepdims=True))
                dbias[0] += ds[0:CHUNK, LIVE - DBW:LIVE]
                dbias[1] += ds[CHUNK:HALF, LIVE - DBW:LIVE]
                parts.append(_widen((ds * SM_SCALE).astype(BF16), half))
            dsb = jnp.concatenate(parts, axis=0)
            dq_ref[rows, :] = _nn(dsb, k_ref[win, :]).astype(BF16)
            dk_acc[win, :] += _tn(dsb, q_ref[rows, :])
            dv_acc[win, :] += _tn(p_ref[sub], do_ref[rows, :])

        @pl.when(i == n_i - 1)
        def _():
            zeros = jnp.zeros((CHUNK, BW - DBW), F32)
            dba_ref[...] = jnp.concatenate([zeros, dbias[0]], axis=1)
            dbb_ref[...] = jnp.concatenate([zeros, dbias[1]], axis=1)
            if prev is None:
                dk_ref[...] = dk_acc[...]
                dv_ref[...] = dv_acc[...]
            else:
                for cp in prev_copies():
                    cp.wait()
                dk_ref[...] = (dk_acc[...] + pk_v[...]).astype(BF16)
                dv_ref[...] = (dv_acc[...] + pv_v[...]).astype(BF16)

    head = pl.BlockSpec((S + PAD, HD), lambda h, i: (0, h))
    strip = pl.BlockSpec((None, CHUNK, BW), lambda h, i: (h, 0, 0))
    blk = pl.BlockSpec((R, HD), lambda h, i: (i, h))
    in_specs = [blk, head, head, pl.BlockSpec((None, nsub, QB, WIN), lambda h, i: (h, i, 0, 0)), blk]
    scratch = [pltpu.VMEM((2, CHUNK, DBW), F32), pltpu.VMEM((S + PAD, HD), F32), pltpu.VMEM((S + PAD, HD), F32)]
    args = (q, kp, vp, probs, do)
    if prev is not None:
        in_specs += [ANY, ANY]
        scratch += [pltpu.VMEM((S + PAD, HD), F32), pltpu.VMEM((S + PAD, HD), F32), pltpu.SemaphoreType.DMA((2,))]
        args += tuple(prev)
    return _call(
        body, name, (NH, n_i), in_specs, [blk, head, head, strip, strip],
        [jax.ShapeDtypeStruct((S, E), BF16), jax.ShapeDtypeStruct((S + PAD, E), dt_kv),
         jax.ShapeDtypeStruct((S + PAD, E), dt_kv), jax.ShapeDtypeStruct((NH, CHUNK, BW), F32),
         jax.ShapeDtypeStruct((NH, CHUNK, BW), F32)],
        scratch, _params(2, 56), args, comm)


def _pool_bwd(dms, mixed, pooled, wg, a_scale, name, comm=None, tm=512):
    S = dms.shape[0]
    n_t = S // tm

    def rev(i):
        return (n_t - 1 - i, 0)

    def body(d_ref, m_ref, p_ref, wg_ref, as_ref, dv_ref, dwg_ref, st_ref, buf):
        i = pl.program_id(0)

        @pl.when(i == 0)
        def _():
            buf[tm:tm + HALO, :] = jnp.zeros((HALO, E), F32)
            dwg_ref[...] = jnp.zeros((4, GW, GW), F32)
            st_ref[...] = jnp.zeros((SUBLANES, E), F32)

        t = (n_t - 1 - i) * tm + lax.broadcasted_iota(jnp.int32, (tm, 1), 0)
        st_ref[0:1, :] += _colsum(d_ref[...].astype(F32) * m_ref[...].astype(F32))
        for gi, w in enumerate(POOL_W):
            cols = slice(gi * GW, (gi + 1) * GW)
            dm = (d_ref[:, cols].astype(F32) * as_ref[:, cols]).astype(BF16)
            dpool = _nt(dm, wg_ref[gi])
            dwg_ref[gi] += _tn(p_ref[:, cols], dm)
            inv_cnt = 1.0 / jnp.minimum(t + 1, w).astype(F32)
            buf[0:tm, cols] = dpool * inv_cnt
            s = buf[:, cols]
            k = 1
            while k < w:
                s = s + pltpu.roll(s, tm + HALO - k, 0)
                k *= 2
            dv_ref[:, cols] = (s[0:tm, :] - dpool).astype(BF16)
        buf[tm:tm + HALO, :] = buf[0:HALO, :]

    return _call(
        body, name, (n_t,),
        [pl.BlockSpec((tm, E), rev), pl.BlockSpec((tm, E), rev), pl.BlockSpec((tm, E), rev),
         pl.BlockSpec((4, GW, GW), lambda i: (0, 0, 0)), _row(E)],
        [pl.BlockSpec((tm, E), rev), pl.BlockSpec((4, GW, GW), lambda i: (0, 0, 0)),
         pl.BlockSpec((SUBLANES, E), lambda i: (0, 0))],
        [jax.ShapeDtypeStruct((S, E), BF16), jax.ShapeDtypeStruct((4, GW, GW), F32),
         jax.ShapeDtypeStruct((SUBLANES, E), F32)],
        [pltpu.VMEM((tm + HALO, E), F32)],
        _params(1, 52), (dms, mixed, pooled, wg, a_scale), comm)


def _in_bwd(da, db, row_off, u, h, g, scale, w, dh_out, name, comm=None, tm=256):
    S = h.shape[0]
    n_t = S // tm
    off = row_off // tm

    def body(da_ref, db_ref, u_ref, h_ref, g_ref, sc_ref, w_hbm, dho_ref, dhi_ref, dw_hbm, st_ref, w_v, acc, stage, sem):
        i = pl.program_id(0)

        @pl.when(i == 0)
        def _():
            cp = pltpu.make_async_copy(w_hbm, w_v, sem)
            cp.start()
            acc[...] = jnp.zeros(acc.shape, F32)
            st_ref[...] = jnp.zeros((SUBLANES, D), F32)
            cp.wait()

        ub = u_ref[...]
        du = None
        for q in range(NCHIP):
            d_ref = da_ref if q < 2 else db_ref
            dv = d_ref[:, (q % 2) * D:(q % 2 + 1) * D]
            acc[q] += _tn(ub, dv)
            part = _nt(dv, w_v[q])
            du = part if du is None else du + part

        hh = h_ref[...]
        r = lax.rsqrt(jnp.mean(hh * hh, axis=-1, keepdims=True) + EPS)
        xhat = hh * r
        gg = g_ref[...]
        st_ref[0:1, :] += _colsum(du)
        st_ref[1:2, :] += _colsum(du * (xhat * gg))
        dn = du * (1.0 + sc_ref[...])
        st_ref[2:3, :] += _colsum(dn * xhat)
        dx = dn * gg
        dhi_ref[...] = dho_ref[...] + r * (dx - xhat * jnp.mean(dx * xhat, axis=-1, keepdims=True))

        @pl.when(i == n_t - 1)
        def _():
            _store_grad(acc, stage, dw_hbm, sem)

    part_spec = pl.BlockSpec((tm, E), lambda i: (i + off, 0))
    return _call(
        body, name, (n_t,),
        [part_spec, part_spec, pl.BlockSpec((tm, D), lambda i: (i, 0)), pl.BlockSpec((tm, D), lambda i: (i, 0)),
         _row(D), _row(D), ANY, pl.BlockSpec((tm, D), lambda i: (i, 0))],
        [pl.BlockSpec((tm, D), lambda i: (i, 0)), ANY, pl.BlockSpec((SUBLANES, D), lambda i: (0, 0))],
        [jax.ShapeDtypeStruct((S, D), F32), jax.ShapeDtypeStruct((NCHIP, D, D), BF16),
         jax.ShapeDtypeStruct((SUBLANES, D), F32)],
        [pltpu.VMEM((NCHIP, D, D), BF16), pltpu.VMEM((NCHIP, D, D), F32), pltpu.VMEM((D, D), BF16),
         pltpu.SemaphoreType.DMA],
        _params(1, 56), (da, db, u, h, g, scale, w, dh_out), comm)


def _grad_ada(c_act_t, dmod, name):
    L, _, n = dmod.shape

    def body(c_ref, d_ref, o_ref):
        acc = None
        for b in range(SUBLANES):
            part = c_ref[:, b:b + 1] * d_ref[b:b + 1, :]
            acc = part if acc is None else acc + part
        o_ref[...] = acc

    return pl.pallas_call(
        body, name=name, grid=(L,),
        in_specs=[pl.BlockSpec((D, SUBLANES), lambda l: (0, 0)), pl.BlockSpec((None, SUBLANES, n), lambda l: (l, 0, 0))],
        out_specs=pl.BlockSpec((None, D, n), lambda l: (l, 0, 0)),
        out_shape=jax.ShapeDtypeStruct((L, D, n), F32),
        compiler_params=_params(1, 32),
    )(c_act_t, dmod)


def _stats_reduce(g3, loss_row, name):
    n_dev, rows, _ = g3.shape

    def body(g_ref, o_ref, l_ref):
        acc = g_ref[0]
        for d in range(1, n_dev):
            acc = acc + g_ref[d]
        o_ref[...] = acc
        tot = jnp.sum(o_ref[loss_row:loss_row + 1, :], axis=-1, keepdims=True)
        l_ref[...] = jnp.broadcast_to(tot * (0.5 / D), (SUBLANES, LANES))

    return pl.pallas_call(
        body, name=name,
        in_specs=[pl.BlockSpec(memory_space=pltpu.VMEM)],
        out_specs=[pl.BlockSpec(memory_space=pltpu.VMEM), pl.BlockSpec(memory_space=pltpu.VMEM)],
        out_shape=[jax.ShapeDtypeStruct((rows, D), F32), jax.ShapeDtypeStruct((SUBLANES, LANES), F32)],
        compiler_params=pltpu.CompilerParams(vmem_limit_bytes=32 * 2 ** 20),
    )(g3)


def _sum4(own, land, chip, name, tr=256):
    _, R, C = own.shape
    tr = min(tr, R)

    def body(p_ref, own_ref, land_ref, o_ref):
        o_ref[...] = ((own_ref[...].astype(F32) + land_ref[0].astype(F32)) + land_ref[1].astype(F32)) + land_ref[2].astype(F32)

    out = pl.pallas_call(
        body, name=name,
        grid_spec=pltpu.PrefetchScalarGridSpec(
            num_scalar_prefetch=1, grid=(R // tr,),
            in_specs=[pl.BlockSpec((None, tr, C), lambda i, p: (p[0], i, 0)), pl.BlockSpec((3, tr, C), lambda i, p: (0, i, 0))],
            out_specs=pl.BlockSpec((tr, C), lambda i, p: (i, 0))),
        out_shape=jax.ShapeDtypeStruct((R, C), F32),
        compiler_params=_params(1, 32),
    )(chip, pltpu.with_memory_space_constraint(own, pltpu.HBM), pltpu.with_memory_space_constraint(land, pltpu.HBM))
    return pltpu.with_memory_space_constraint(out, pltpu.HBM)


def _adamw(w, m, v, g, name, tr=256):
    L, R, C = w.shape
    tr = min(tr, R)
    stacked = not isinstance(g, (list, tuple))
    n_g = None if stacked else [len(ps) for ps in g]
    flat = [g] if stacked else [a for ps in g for a in ps]

    def body(*refs):
        w_ref, m_ref, v_ref = refs[:3]
        g_refs = refs[3:3 + len(flat)]
        go_ref, d_ref, mo_ref, vo_ref = refs[3 + len(flat):]
        if stacked:
            gg = g_refs[0][...]
        else:
            layer = pl.program_id(0)
            gg = None
            k = 0
            for li in range(L):
                gl = None
                for _ in range(n_g[li]):
                    x = g_refs[k][...]
                    gl = x if gl is None else gl + x
                    k += 1
                gg = gl if gg is None else jnp.where(layer == li, gl, gg)
        m2 = ADAM_B1 * m_ref[...] + (1.0 - ADAM_B1) * gg
        v2 = ADAM_B2 * v_ref[...] + (1.0 - ADAM_B2) * (gg * gg)
        m_hat = m2 / (1.0 - ADAM_B1 ** ADAM_STEP)
        v_hat = v2 / (1.0 - ADAM_B2 ** ADAM_STEP)
        go_ref[...] = gg
        d_ref[...] = -ADAM_LR * (m_hat / (jnp.sqrt(v_hat) + ADAM_EPS) + ADAM_WD * w_ref[...])
        mo_ref[...] = m2
        vo_ref[...] = v2

    big = pl.BlockSpec((None, tr, C), lambda l, i: (l, i, 0))
    g_specs = [big] if stacked else [pl.BlockSpec((tr, C), lambda l, i: (i, 0))] * len(flat)
    return pl.pallas_call(
        body, name=name, grid=(L, R // tr),
        in_specs=[big, big, big] + g_specs,
        out_specs=[big, big, big, big],
        out_shape=[jax.ShapeDtypeStruct((L, R, C), F32)] * 4,
        compiler_params=_params(2, 48),
    )(w, m, v, *flat)


GATHER8_SEMS = [pltpu.SemaphoreType.DMA((7,)), pltpu.SemaphoreType.DMA((7,)), pltpu.SemaphoreType.DMA]


def _gather8(x_ref, out_ref, send_sems, recv_sems, local_sem):
    m = x_ref.shape[0]
    x, y, c = _place()
    me, sibling = (x, y, c), (x, y, 1 - c)
    chips = [(1 - x, y), (x, 1 - y), (1 - x, 1 - y)]

    def rows(px, py, pc):
        return out_ref.at[pl.ds((4 * px + 2 * py + pc) * m, m), :]

    def copy(k, block, to, src=None):
        return pltpu.make_async_remote_copy(
            src_ref=rows(*block) if src is None else src, dst_ref=rows(*block),
            send_sem=send_sems.at[k], recv_sem=recv_sems.at[k], device_id=to, device_id_type=MESH)

    mine = pltpu.make_async_copy(x_ref, rows(*me), local_sem)
    mine.start()
    first = [copy(0, me, sibling, src=x_ref)]
    first += [copy(1 + j, me, (*chip, c), src=x_ref) for j, chip in enumerate(chips)]
    for cp in first:
        cp.start()
    passed = [copy(4 + j, (*chip, c), sibling) for j, chip in enumerate(chips)]
    for j, chip in enumerate(chips):
        copy(1 + j, (*chip, c), me).wait_recv()
        passed[j].start()
    copy(0, sibling, me).wait_recv()
    for j, chip in enumerate(chips):
        copy(4 + j, (*chip, 1 - c), me).wait_recv()
    for cp in first + passed:
        cp.wait_send()
    mine.wait()


def _allgather8(xs, name, comm=None):
    m, n = xs.shape
    n_c = 0 if comm is None else comm.n

    def body(*refs):
        x_ref, out_ref = refs[0], refs[1 + n_c]
        c_refs = (refs[1:1 + n_c], refs[2 + n_c:2 + 2 * n_c]) + tuple(refs[5 + 2 * n_c:])
        if comm is not None:
            comm.start(*c_refs)
        _gather8(x_ref, out_ref, *refs[2 + 2 * n_c:5 + 2 * n_c])
        if comm is not None:
            comm.wait(*c_refs)

    vmem = pl.BlockSpec(memory_space=pltpu.VMEM)
    outs = pl.pallas_call(
        body, name=name,
        out_shape=[jax.ShapeDtypeStruct((8 * m, n), xs.dtype)] + ([] if comm is None else comm.out_shape),
        in_specs=[vmem] + [ANY] * n_c,
        out_specs=[vmem] + [ANY] * n_c,
        scratch_shapes=GATHER8_SEMS + ([] if comm is None else comm.scratch),
        compiler_params=pltpu.CompilerParams(vmem_limit_bytes=32 * 2 ** 20),
    )(xs, *([] if comm is None else comm.arrays))
    return outs[0], list(outs[1:])


def _prologue(c8, ada_w, ada_b, kv_ada_w, kv_ada_b, extra, comm, name):
    L, _, n = ada_w.shape
    k = kv_ada_w.shape[1]
    e = extra.shape[1]
    width = L * n + k + e
    n_c = comm.n

    def body(*refs):
        c_ref, w_hbm, b_ref, kw_hbm, kb_ref, x_ref = refs[:6]
        c_in = refs[6:6 + n_c]
        ca_ref, out_ref = refs[6 + n_c:8 + n_c]
        c_out = refs[8 + n_c:8 + 2 * n_c]
        cbuf, wbuf, kbuf, part, wsems = refs[8 + 2 * n_c:13 + 2 * n_c]
        sems_a = refs[13 + 2 * n_c:16 + 2 * n_c]
        sems_b = refs[16 + 2 * n_c:19 + 2 * n_c]
        c_refs = (c_in, c_out) + tuple(refs[19 + 2 * n_c:])
        comm.start(*c_refs)

        def fetch(l):
            return pltpu.make_async_copy(w_hbm.at[l], wbuf.at[l % 2], wsems.at[l % 2])

        fetch(0).start()
        kv_copy = pltpu.make_async_copy(kw_hbm, kbuf, wsems.at[2])
        kv_copy.start()
        _gather8(c_ref, cbuf, *sems_a)
        cc = jnp.concatenate([cbuf[SUBLANES * d:SUBLANES * d + 1, :] for d in range(8)], axis=0)
        ca = cc * jax.nn.sigmoid(cc)
        ca_ref[...] = ca
        cab = ca.astype(BF16)
        for l in range(L):
            fetch(l).wait()
            if l + 1 < L:
                fetch(l + 1).start()
            part[:, l * n:(l + 1) * n] = _nn(cab, wbuf[l % 2].astype(BF16)) + b_ref[l]
        kv_copy.wait()
        part[:, L * n:L * n + k] = _nn(cab, kbuf[...].astype(BF16)) + kb_ref[...]
        part[:, L * n + k:] = jnp.broadcast_to(x_ref[...], (SUBLANES, e))
        _gather8(part, out_ref, *sems_b)
        comm.wait(*c_refs)

    vmem = pl.BlockSpec(memory_space=pltpu.VMEM)
    outs = pl.pallas_call(
        body, name=name,
        out_shape=[jax.ShapeDtypeStruct((SUBLANES, D), F32), jax.ShapeDtypeStruct((8 * SUBLANES, width), F32)] + comm.out_shape,
        in_specs=[vmem, ANY, vmem, ANY, vmem, vmem] + [ANY] * n_c,
        out_specs=[vmem, vmem] + [ANY] * n_c,
        scratch_shapes=[pltpu.VMEM((8 * SUBLANES, D), F32), pltpu.VMEM((2, D, n), F32), pltpu.VMEM((D, k), F32),
                        pltpu.VMEM((SUBLANES, width), F32), pltpu.SemaphoreType.DMA((3,))] + GATHER8_SEMS + GATHER8_SEMS
        + comm.scratch,
        compiler_params=pltpu.CompilerParams(vmem_limit_bytes=32 * 2 ** 20),
    )(c8, ada_w, ada_b, kv_ada_w, kv_ada_b, extra, *comm.arrays)
    return outs[0], outs[1], list(outs[2:])


def _pad8(a):
    return jnp.pad(a, ((0, SUBLANES - a.shape[0]), (0, 0)))


def _group_rows(wg):
    return wg.transpose(1, 0, 2, 3).reshape(4, GW, GW)


def _example_step(h0, tgt, mods, kvmod, a_scale, norm_g, kv_norm_g, final_g, b_rel_bias, sh, w_first, chip_arr):
    ones_e = jnp.ones((1, E), F32)
    shift = [mods[l:l + 1, 0:D] for l in range(4)]
    scale = [mods[l:l + 1, D:2 * D] for l in range(4)]
    gate = [mods[l:l + 1, 2 * D:3 * D] for l in range(4)]
    gl = [norm_g[l:l + 1] for l in range(4)]
    kv_shift, kv_scale = kvmod[None, 0:D], kvmod[None, D:2 * D]
    kv_g = kv_norm_g[None]

    w_a = w_first
    hs = [h0]
    saved = []
    nxt = [[sh["a_in"][1], sh["a_grp"][1], sh["a_out"][1]], [sh["kv"][0], sh["b_in"][0]]]
    for l in range(2):
        w_in_l, wg_l, wo_l = w_a
        wg_full = _group_rows(wg_l)
        (u, z, pooled, mixed, y, hn), got = _a_fwd(hs[-1], gl[l], shift[l], scale[l], a_scale[l:l + 1], gate[l], w_in_l,
                                                   wg_full, wo_l, f"a{l}_fwd", comm=_Comm(gathers=nxt[l]))
        saved.append((u, z, pooled, mixed, y, w_in_l, wg_full, wo_l))
        hs.append(hn)
        if l == 0:
            w_a = got
        else:
            w_kv, wb_in0 = got

    (uk, kp, vp), _ = _in_fwd(hs[2], kv_g, kv_shift, kv_scale, w_kv, BF16, BF16, "kv_in_fwd", pad_rows=PAD)
    wb_in = [wb_in0, None]
    wb_out = [None, None]

    for bi in range(2):
        l = 2 + bi
        sa, sb = _bias_build(jnp.pad(b_rel_bias[bi], ((0, 0), (0, NRELP - NREL))), f"b{bi}_bias")
        (u, q, z), _ = _in_fwd(hs[-1], gl[l], shift[l], scale[l], wb_in[bi], BF16, BF16, f"b{bi}_in_fwd")
        comm = _Comm(gathers=[sh["b_out"][0], sh["b_in"][1], sh["b_out"][1]]) if bi == 0 else None
        (att, probs), got = _attn_fwd(q, kp, vp, sa.transpose(1, 0, 2), sb.transpose(1, 0, 2), f"b{bi}_attn_fwd", comm=comm)
        if bi == 0:
            wb_out[0], wb_in[1], wb_out[1] = got
        if bi == 0:
            (y, hn), _ = _out_fwd(att, z, wb_out[bi], gate[l], hs[-1], f"b{bi}_out_fwd")
            hs.append(hn)
        else:
            (y, dh, st_fin), _ = _out_fwd(att, z, wb_out[bi], gate[l], hs[-1], f"b{bi}_out_fwd", head=(final_g[None], tgt))
        saved.append((u, z, q, att, y, probs))

    st_in = [None] * 4
    st_out = [None] * 4
    grads = {}
    landed = {}

    def carry(names):
        return _Comm(scatters=[grads[n] for n in names]) if names else None

    def land(names, got):
        for n, a in zip(names, got):
            landed[n] = a

    u, z, q, att, y, probs = saved[3]
    (datt, dz, grads["b_out1"], st_out[3]), _ = _out_bwd(dh, y, gate[3], att, ones_e, z, wb_out[1], "b1_out_bwd")
    (dq, dk1, dv1, dsa, dsb), _ = _attn_bwd(q, kp, vp, probs, datt, None, "b1_attn_bwd")
    drb1 = _dbias_reduce(dsa.transpose(1, 0, 2), dsb.transpose(1, 0, 2), "b1_dbias")
    (dh, grads["b_in1"], st_in[3]), _ = _in_bwd(dq, dz, 0, u, hs[3], gl[3], scale[3], wb_in[1], dh, "b1_in_bwd")
    u, z, q, att, y, probs = saved[2]
    (datt, dz, grads["b_out0"], st_out[2]), _ = _out_bwd(dh, y, gate[2], att, ones_e, z, wb_out[0], "b0_out_bwd")
    (dq, dk, dv, dsa, dsb), got = _attn_bwd(q, kp, vp, probs, datt, (dk1, dv1), "b0_attn_bwd",
                                            comm=carry(["b_out1", "b_in1", "b_out0"]))
    land(["b_out1", "b_in1", "b_out0"], got)
    drb0 = _dbias_reduce(dsa.transpose(1, 0, 2), dsb.transpose(1, 0, 2), "b0_dbias")
    (dh, grads["b_in0"], st_in[2]), _ = _in_bwd(dq, dz, 0, u, hs[2], gl[2], scale[2], wb_in[0], dh, "b0_in_bwd")
    (dh, grads["kv"], st_kv), got = _in_bwd(dk, dv, PAD, uk, hs[2], kv_g, kv_scale, w_kv, dh, "kv_in_bwd",
                                            comm=carry(["b_in0"]))
    land(["b_in0"], got)
    st_pool = [None] * 2
    plan = {1: dict(o=[], p=[], i=["kv", "a_out1", "a_grp1"]), 0: dict(o=["a_in1"], p=["a_out0"], i=[])}
    early = ["b_out1", "b_in1", "b_out0", "b_in0", "kv", "a_out1", "a_grp1", "a_in1"]
    late = ["a_out0", "a_grp0", "a_in0"]
    both = {}

    def sum4(n):
        return _sum4(grads[n], landed[n], chip_arr, f"sum4_{n}")

    for l in (1, 0):
        u, z, pooled, mixed, y, w_in_l, wg_full, wo = saved[l]
        asl = a_scale[l:l + 1]
        (dms, dz, grads[f"a_out{l}"], st_out[l]), got = _out_bwd(dh, y, gate[l], mixed, asl, z, wo, f"a{l}_out_bwd",
                                                                comm=carry(plan[l]["o"]))
        land(plan[l]["o"], got)
        comm = carry(plan[l]["p"])
        if l == 0:
            mine = [sum4(n) for n in early]
            comm = _Comm(scatters=[grads[n] for n in plan[l]["p"]], swaps=mine)
        (dval, dwg, st_pool[l]), got = _pool_bwd(dms, mixed, pooled, wg_full, asl, f"a{l}_pool_bwd", comm=comm)
        land(plan[l]["p"], got)
        if l == 0:
            both.update({n: [a, b] for n, a, b in zip(early, mine, got[len(plan[l]["p"]):])})
        grads[f"a_grp{l}"] = (dwg.reshape(4, NCHIP, GW // NCHIP, GW).transpose(1, 0, 2, 3).reshape(NCHIP, GW, GW)
                              .astype(BF16))
        (dh, grads[f"a_in{l}"], st_in[l]), got = _in_bwd(dval, dz, 0, u, hs[l], gl[l], scale[l], w_in_l, dh, f"a{l}_in_bwd",
                                                         comm=carry(plan[l]["i"]))
        land(plan[l]["i"], got)
    pieces = st_in + [st_kv] + st_out + [st_fin]
    pieces += [_pad8(st_pool[l][0].reshape(2, D)) for l in range(2)]
    pieces += [_pad8(d.reshape(NH * NRELP // D, D)) for d in (drb0, drb1)]
    gathered, got = _allgather8(jnp.concatenate(pieces, axis=0), "gather_stats", comm=carry(["a_grp0", "a_in0"]))
    land(["a_grp0", "a_in0"], got)
    mine = [sum4(n) for n in late]
    both.update({n: [a, b] for n, a, b in zip(late, mine, _comm_only(_Comm(swaps=mine), "swap_last"))})
    return dh, both, gathered.reshape(8, N_STAT, D)


ROW_IN = [8 * l for l in range(4)]
ROW_KV = 32
ROW_OUT = [40 + 8 * l for l in range(4)]
ROW_FIN = 72
ROW_ASC = [80, 88]
ROW_RB = [96, 104]
N_STAT = 112


def kernel(x, c, ada_w, ada_b, norm_g, a_w_in, a_w_group, a_scale, a_w_out, kv_norm_g, kv_ada_w, kv_ada_b, w_kv, b_w_in, b_rel_bias, b_w_out, final_g, loss_target, m_ada_w, m_ada_b, m_norm_g, m_a_w_in, m_a_w_group, m_a_scale, m_a_w_out, m_kv_norm_g, m_kv_ada_w, m_kv_ada_b, m_w_kv, m_b_w_in, m_b_rel_bias, m_b_w_out, m_final_g, v_ada_w, v_ada_b, v_norm_g, v_a_w_in, v_a_w_group, v_a_scale, v_a_w_out, v_kv_norm_g, v_kv_ada_w, v_kv_ada_b, v_w_kv, v_b_w_in, v_b_rel_bias, v_b_w_out, v_final_g):
    xi, yi, ci = _place()
    chip = 2 * xi + yi
    dev = 4 * xi + 2 * yi + ci
    n_ada = ada_w.shape[2]
    n_kva = kv_ada_w.shape[1]
    n_asc = a_scale.shape[1]

    ada_b_sh = lax.dynamic_slice_in_dim(ada_b, chip * n_ada, n_ada, axis=1)
    kvb_sh = lax.dynamic_slice_in_dim(kv_ada_b, chip * n_kva, n_kva, axis=0)
    sh = dict(a_in=[a_w_in[l].astype(BF16) for l in range(2)], a_grp=[a_w_group[l].astype(BF16) for l in range(2)],
              a_out=[a_w_out[l].astype(BF16) for l in range(2)], kv=[w_kv.astype(BF16)],
              b_in=[b_w_in[l].astype(BF16) for l in range(2)], b_out=[b_w_out[l].astype(BF16) for l in range(2)])
    c_act, gathered, w_first = _prologue(
        jnp.broadcast_to(c, (SUBLANES, D)), ada_w, ada_b_sh[:, None, :], kv_ada_w, kvb_sh[None, :],
        a_scale.reshape(1, 2 * n_asc), _Comm(gathers=[sh["a_in"][0], sh["a_grp"][0], sh["a_out"][0]]), "prologue")
    rows = jnp.concatenate([lax.dynamic_slice_in_dim(gathered, SUBLANES * (2 * p + ci) + dev, 1, axis=0)
                            for p in range(NCHIP)], axis=0)
    mods = jnp.stack([rows[:, l * n_ada:(l + 1) * n_ada].reshape(3 * D) for l in range(4)])
    kvmod = rows[:, 4 * n_ada:4 * n_ada + n_kva].reshape(2 * D)
    o_asc = 4 * n_ada + n_kva
    a_scale_full = jnp.stack([rows[:, o_asc + l * n_asc:o_asc + (l + 1) * n_asc].reshape(E) for l in range(2)])

    chip_arr = jnp.reshape(chip, (1,)).astype(jnp.int32)
    dh, both, g3 = _example_step(x[0], loss_target[0], mods, kvmod, a_scale_full, norm_g, kv_norm_g, final_g,
                                 b_rel_bias, sh, w_first, chip_arr)
    grad_x = dh[None]

    red, loss_tile = _stats_reduce(g3, ROW_FIN + 1, "stats_reduce")
    loss = loss_tile[0, 0]

    def cat(rows_):
        return jnp.concatenate(rows_, axis=-1)

    g_ada_b = jnp.stack([cat([red[ROW_IN[l]], red[ROW_IN[l] + 1], red[ROW_OUT[l]]]) for l in range(4)])
    g_norm_g = jnp.stack([red[ROW_IN[l] + 2] for l in range(4)])
    g_kv_norm_g = red[ROW_KV + 2]
    g_kv_ada_b = cat([red[ROW_KV], red[ROW_KV + 1]])
    g_final_g = red[ROW_FIN]
    g_asc_full = jnp.stack([red[ROW_ASC[l]:ROW_ASC[l] + 2].reshape(E) for l in range(2)])
    g_a_scale = lax.dynamic_slice_in_dim(g_asc_full, chip * n_asc, n_asc, axis=1)
    g_rel = jnp.stack([red[ROW_RB[bi]:ROW_RB[bi] + NH * NRELP // D].reshape(NH, NRELP)[:, :NREL] for bi in range(2)])

    dmod = jnp.stack([cat([g3[:, ROW_IN[l]], g3[:, ROW_IN[l] + 1], g3[:, ROW_OUT[l]]]) for l in range(4)])
    dmod_sh = lax.dynamic_slice_in_dim(dmod, chip * n_ada, n_ada, axis=2)
    dkv = cat([g3[:, ROW_KV], g3[:, ROW_KV + 1]])[None]
    dkv_sh = lax.dynamic_slice_in_dim(dkv, chip * n_kva, n_kva, axis=2)
    c_act_t = c_act.T
    g_ada_w = _grad_ada(c_act_t, dmod_sh, "grad_ada_w")
    g_kv_ada_w = _grad_ada(c_act_t, dkv_sh, "grad_kv_ada_w")

    def upd(w, m, v, g, name, shape3):
        g = g.reshape(shape3) if not isinstance(g, list) else g
        outs = _adamw(w.reshape(shape3), m.reshape(shape3), v.reshape(shape3), g, name)
        return [o.reshape(w.shape) for o in outs]

    def pair(name):
        return [both[name + "0"], both[name + "1"]]

    res = {}
    res["ada_w"] = upd(ada_w, m_ada_w, v_ada_w, g_ada_w, "adamw_ada_w", ada_w.shape)
    res["ada_b"] = upd(ada_b, m_ada_b, v_ada_b, g_ada_b, "adamw_ada_b", (1,) + ada_b.shape)
    res["norm_g"] = upd(norm_g, m_norm_g, v_norm_g, g_norm_g, "adamw_norm_g", (1,) + norm_g.shape)
    res["a_w_in"] = upd(a_w_in, m_a_w_in, v_a_w_in, pair("a_in"), "adamw_a_w_in", a_w_in.shape)
    res["a_w_group"] = upd(a_w_group, m_a_w_group, v_a_w_group, pair("a_grp"), "adamw_a_w_group", (2, GW, GW))
    res["a_scale"] = upd(a_scale, m_a_scale, v_a_scale, g_a_scale, "adamw_a_scale", (1,) + a_scale.shape)
    res["a_w_out"] = upd(a_w_out, m_a_w_out, v_a_w_out, pair("a_out"), "adamw_a_w_out", a_w_out.shape)
    res["kv_norm_g"] = upd(kv_norm_g, m_kv_norm_g, v_kv_norm_g, g_kv_norm_g, "adamw_kv_norm_g", (1, 1, D))
    res["kv_ada_w"] = upd(kv_ada_w, m_kv_ada_w, v_kv_ada_w, g_kv_ada_w, "adamw_kv_ada_w", (1,) + kv_ada_w.shape)
    res["kv_ada_b"] = upd(kv_ada_b, m_kv_ada_b, v_kv_ada_b, g_kv_ada_b, "adamw_kv_ada_b", (1, 1, 2 * D))
    res["w_kv"] = upd(w_kv, m_w_kv, v_w_kv, [both["kv"]], "adamw_w_kv", (1,) + w_kv.shape)
    res["b_w_in"] = upd(b_w_in, m_b_w_in, v_b_w_in, pair("b_in"), "adamw_b_w_in", b_w_in.shape)
    res["b_rel_bias"] = upd(b_rel_bias, m_b_rel_bias, v_b_rel_bias, g_rel, "adamw_b_rel_bias", (1, 2 * NH, NREL))
    res["b_w_out"] = upd(b_w_out, m_b_w_out, v_b_w_out, pair("b_out"), "adamw_b_w_out", b_w_out.shape)
    res["final_g"] = upd(final_g, m_final_g, v_final_g, g_final_g, "adamw_final_g", (1, 1, D))

    names = ["ada_w", "ada_b", "norm_g", "a_w_in", "a_w_group", "a_scale", "a_w_out", "kv_norm_g", "kv_ada_w", "kv_ada_b",
             "w_kv", "b_w_in", "b_rel_bias", "b_w_out", "final_g"]
    return (loss, grad_x, *[res[n][0] for n in names], *[res[n][1] for n in names], *[res[n][2] for n in names],
            *[res[n][3] for n in names])
```

```python
import math

import jax
import jax.numpy as jnp
from jax import lax
from jax.experimental import pallas as pl
from jax.experimental.pallas import tpu as pltpu

F32 = jnp.float32
BF16 = jnp.bfloat16

D = 1024
E = 2048
NH = 16
HD = 128
CHUNK = 64
LEFT = 8
PAD = LEFT * CHUNK
NREL = 257
NRELP = 384
REL_CLIP = 128
EPS = 1e-6
NEG = -1e30
LOG2E = math.log2(math.e)
SM_SCALE = HD ** -0.5
POOL_W = (2, 4, 8, 16)
GW = 512
HALO = 16
QC = 4
QB = QC * CHUNK
NMASK = PAD // QB
WIN = (QC + LEFT) * CHUNK
BW = (LEFT + 2) * CHUNK
DBW = 4 * CHUNK
NSUB = 16
NCHIP = 4
LANES = 128
SUBLANES = 8

ADAM_LR = 0.001
ADAM_B1 = 0.9
ADAM_B2 = 0.999
ADAM_EPS = 1e-08
ADAM_WD = 0.01
ADAM_STEP = 10

MESH = pl.DeviceIdType.MESH
ANY = pl.BlockSpec(memory_space=pl.ANY)


def _params(n_axes, vmem_mb):
    return pltpu.CompilerParams(dimension_semantics=("arbitrary",) * n_axes, vmem_limit_bytes=vmem_mb * 2 ** 20)


def _nn(a, b):
    return jnp.dot(a, b, preferred_element_type=F32)


def _nt(a, b):
    return lax.dot_general(a, b, (((1,), (1,)), ((), ())), preferred_element_type=F32)


def _tn(a, b):
    return lax.dot_general(a, b, (((0,), (0,)), ((), ())), preferred_element_type=F32)


def _row(n):
    return pl.BlockSpec((1, n), lambda i: (0, 0))


def _colsum(x):
    return jnp.sum(x, axis=0, keepdims=True)


def _place():
    return lax.axis_index("x"), lax.axis_index("y"), lax.axis_index("c")


class _Comm:
    def __init__(self, gathers=(), scatters=(), swaps=()):
        self.n_g = len(gathers)
        self.n_chip = len(gathers) + len(scatters)
        self.n_sw = len(swaps)
        self.arrays = list(gathers) + list(scatters) + list(swaps)
        self.n = len(self.arrays)
        self.half = [a.shape[0] // 2 for a in gathers]
        self.out_shape = ([jax.ShapeDtypeStruct((NCHIP,) + a.shape, a.dtype) for a in gathers]
                          + [jax.ShapeDtypeStruct((3,) + a.shape[1:], a.dtype) for a in scatters]
                          + [jax.ShapeDtypeStruct(a.shape, a.dtype) for a in swaps])
        n_c, n_f, n_s = max(3 * self.n_chip, 1), max(3 * self.n_g, 1), max(self.n_sw, 1)
        self.scratch = [pltpu.SemaphoreType.DMA((n_c,)), pltpu.SemaphoreType.DMA((n_c,)),
                        pltpu.SemaphoreType.DMA((max(self.n_g, 1),)), pltpu.SemaphoreType.DMA((n_f,)),
                        pltpu.SemaphoreType.DMA((n_f,)), pltpu.SemaphoreType.DMA((n_s,)), pltpu.SemaphoreType.DMA((n_s,))]

    def _chip_copies(self, ins, outs, send, recv, landing):
        x, y, c = _place()
        chips = [(1 - x, y), (x, 1 - y), (1 - x, 1 - y)]
        mine = 2 * x + y
        cps = []
        for k in range(self.n_chip):
            for j, (cx, cy) in enumerate(chips):
                q = 2 * cx + cy
                if k < self.n_g:
                    part = pl.ds(c * self.half[k], self.half[k])
                    src = ins[k].at[part]
                    dst = outs[k].at[q if landing else mine, part]
                else:
                    src = ins[k].at[q]
                    dst = outs[k].at[j]
                cps.append(pltpu.make_async_remote_copy(
                    src_ref=src, dst_ref=dst, send_sem=send.at[3 * k + j], recv_sem=recv.at[3 * k + j],
                    device_id=(cx, cy, c), device_id_type=MESH))
        return cps

    def _core_copies(self, outs, fsend, frecv, landing):
        x, y, c = _place()
        chips = [(1 - x, y), (x, 1 - y), (1 - x, 1 - y)]
        cps = []
        for k in range(self.n_g):
            for j, (cx, cy) in enumerate(chips):
                part = pl.ds((1 - c if landing else c) * self.half[k], self.half[k])
                blk = outs[k].at[2 * cx + cy, part]
                cps.append(pltpu.make_async_remote_copy(
                    src_ref=blk, dst_ref=blk, send_sem=fsend.at[3 * k + j], recv_sem=frecv.at[3 * k + j],
                    device_id=(x, y, 1 - c), device_id_type=MESH))
        return cps

    def _local_copies(self, ins, outs, loc):
        x, y, _ = _place()
        return [pltpu.make_async_copy(ins[k], outs[k].at[2 * x + y], loc.at[k]) for k in range(self.n_g)]

    def _swap_copies(self, ins, outs, ssend, srecv):
        x, y, c = _place()
        return [pltpu.make_async_remote_copy(
            src_ref=ins[k], dst_ref=outs[k], send_sem=ssend.at[k - self.n_chip], recv_sem=srecv.at[k - self.n_chip],
            device_id=(x, y, 1 - c), device_id_type=MESH) for k in range(self.n_chip, self.n)]

    def start(self, ins, outs, send, recv, loc, fsend, frecv, ssend, srecv):
        for cp in (self._local_copies(ins, outs, loc) + self._chip_copies(ins, outs, send, recv, False)
                   + self._swap_copies(ins, outs, ssend, srecv)):
            cp.start()

    def wait(self, ins, outs, send, recv, loc, fsend, frecv, ssend, srecv):
        lands = self._chip_copies(ins, outs, send, recv, True)
        passes = self._core_copies(outs, fsend, frecv, False)
        for k in range(self.n_chip):
            for j in range(3):
                lands[3 * k + j].wait_recv()
                if k < self.n_g:
                    passes[3 * k + j].start()
        for cp in self._core_copies(outs, fsend, frecv, True):
            cp.wait_recv()
        swaps = self._swap_copies(ins, outs, ssend, srecv)
        for cp in swaps:
            cp.wait_recv()
        for cp in self._chip_copies(ins, outs, send, recv, False) + passes + swaps:
            cp.wait_send()
        for cp in self._local_copies(ins, outs, loc):
            cp.wait()


def _call(body, name, grid, in_specs, out_specs, out_shape, scratch, params, args, comm=None):
    n_in, n_out, n_sc = len(in_specs), len(out_specs), len(scratch)
    if comm is None:
        outs = pl.pallas_call(body, name=name, grid=grid, in_specs=in_specs, out_specs=out_specs, out_shape=out_shape,
                              scratch_shapes=scratch, compiler_params=params)(*args)
        return list(outs), []
    n = comm.n
    o0 = n_in + n
    s0 = o0 + n_out + n

    def wrapped(*refs):
        c_refs = (refs[n_in:o0], refs[o0 + n_out:s0]) + tuple(refs[s0 + n_sc:])
        ids = [pl.program_id(a) for a in range(len(grid))]
        first = ids[0] == 0
        last = ids[0] == grid[0] - 1
        for a in range(1, len(grid)):
            first = first & (ids[a] == 0)
            last = last & (ids[a] == grid[a] - 1)

        @pl.when(first)
        def _():
            comm.start(*c_refs)

        body(*refs[:n_in], *refs[o0:o0 + n_out], *refs[s0:s0 + n_sc])

        @pl.when(last)
        def _():
            comm.wait(*c_refs)

    outs = pl.pallas_call(
        wrapped, name=name, grid=grid, in_specs=list(in_specs) + [ANY] * n, out_specs=list(out_specs) + [ANY] * n,
        out_shape=list(out_shape) + comm.out_shape, scratch_shapes=list(scratch) + comm.scratch, compiler_params=params,
    )(*args, *comm.arrays)
    return list(outs[:n_out]), list(outs[n_out:])


def _comm_only(comm, name):
    def body(*refs):
        c_refs = (refs[:comm.n], refs[comm.n:2 * comm.n]) + tuple(refs[2 * comm.n:])
        comm.start(*c_refs)
        comm.wait(*c_refs)

    return pl.pallas_call(body, name=name, in_specs=[ANY] * comm.n, out_specs=[ANY] * comm.n, out_shape=comm.out_shape,
                          scratch_shapes=comm.scratch)(*comm.arrays)


def _in_fwd(h, g, shift, scale, w, dt_a, dt_b, name, pad_rows=0, comm=None, tm=512):
    S = h.shape[0]
    n_pad = pad_rows // tm

    def body(h_ref, g_ref, sh_ref, sc_ref, w_hbm, u_ref, oa_ref, ob_ref, w_v, sem):
        i = pl.program_id(0)

        @pl.when(i == 0)
        def _():
            cp = pltpu.make_async_copy(w_hbm, w_v, sem)
            cp.start()
            cp.wait()

        hh = h_ref[...]
        r = lax.rsqrt(jnp.mean(hh * hh, axis=-1, keepdims=True) + EPS)
        u = (hh * r * g_ref[...]) * (1.0 + sc_ref[...]) + sh_ref[...]
        ub = u.astype(BF16)
        u_ref[...] = ub
        for q in range(NCHIP):
            o_ref = oa_ref if q < 2 else ob_ref
            o_ref[:, (q % 2) * D:(q % 2 + 1) * D] = _nn(ub, w_v[q]).astype(o_ref.dtype)

        if n_pad:
            @pl.when(i < n_pad)
            def _():
                oa_ref[...] = jnp.zeros(oa_ref.shape, oa_ref.dtype)
                ob_ref[...] = jnp.zeros(ob_ref.shape, ob_ref.dtype)

    def src(i):
        return (jnp.maximum(i - n_pad, 0), 0)

    outs, landed = _call(
        body, name, (S // tm + n_pad,),
        [pl.BlockSpec((tm, D), src), _row(D), _row(D), _row(D), ANY],
        [pl.BlockSpec((tm, D), src), pl.BlockSpec((tm, E), lambda i: (i, 0)), pl.BlockSpec((tm, E), lambda i: (i, 0))],
        [jax.ShapeDtypeStruct((S, D), BF16), jax.ShapeDtypeStruct((S + pad_rows, E), dt_a),
         jax.ShapeDtypeStruct((S + pad_rows, E), dt_b)],
        [pltpu.VMEM((NCHIP, D, D), BF16), pltpu.SemaphoreType.DMA],
        _params(1, 52), (h, g, shift, scale, w), comm)
    return outs, landed


def _a_fwd(h, g, shift, scale, asc, gate, w_in, wg, w_out, name, comm=None, tm=512):
    S = h.shape[0]

    def body(h_ref, g_ref, sh_ref, sc_ref, as_ref, gate_ref, wi_hbm, wg_hbm, wo_hbm,
             u_ref, z_ref, p_ref, m_ref, y_ref, ho_ref, wi_v, wg_v, wo_v, buf, sems):
        i = pl.program_id(0)

        @pl.when(i == 0)
        def _():
            cps = [pltpu.make_async_copy(wi_hbm, wi_v, sems.at[0]), pltpu.make_async_copy(wg_hbm, wg_v, sems.at[1]),
                   pltpu.make_async_copy(wo_hbm, wo_v, sems.at[2])]
            for cp in cps:
                cp.start()
            buf[0:HALO, :] = jnp.zeros((HALO, E), F32)
            for cp in cps:
                cp.wait()

        hh = h_ref[...]
        r = lax.rsqrt(jnp.mean(hh * hh, axis=-1, keepdims=True) + EPS)
        ub = ((hh * r * g_ref[...]) * (1.0 + sc_ref[...]) + sh_ref[...]).astype(BF16)
        u_ref[...] = ub
        for q in range(2):
            buf[HALO:HALO + tm, q * D:(q + 1) * D] = _nn(ub, wi_v[q])
        t = i * tm + lax.broadcasted_iota(jnp.int32, (tm, 1), 0)
        y = None
        for gi, w in enumerate(POOL_W):
            cols = slice(gi * GW, (gi + 1) * GW)
            x = buf[:, cols]
            s = x
            k = 1
            while k < w:
                s = s + pltpu.roll(s, k, 0)
                k *= 2
            inv_cnt = 1.0 / jnp.minimum(t + 1, w).astype(F32)
            pb = (s[HALO:, :] * inv_cnt - x[HALO:, :]).astype(BF16)
            p_ref[:, cols] = pb
            mb = _nn(pb, wg_v[gi]).astype(BF16)
            m_ref[:, cols] = mb
            zb = _nn(ub, wi_v[2 + gi // 2, :, (gi % 2) * GW:(gi % 2 + 1) * GW]).astype(BF16)
            z_ref[:, cols] = zb
            zz = zb.astype(F32)
            act = ((mb.astype(F32) * as_ref[:, cols]) * (zz * jax.nn.sigmoid(zz))).astype(BF16)
            part = _nn(act, wo_v[gi])
            y = part if y is None else y + part
        buf[0:HALO, :] = buf[tm:tm + HALO, :]
        y_ref[...] = y.astype(BF16)
        ho_ref[...] = hh + gate_ref[...] * y

    rows_d = pl.BlockSpec((tm, D), lambda i: (i, 0))
    rows_e = pl.BlockSpec((tm, E), lambda i: (i, 0))
    return _call(
        body, name, (S // tm,),
        [rows_d, _row(D), _row(D), _row(D), _row(E), _row(D), ANY, ANY, ANY],
        [rows_d, rows_e, rows_e, rows_e, rows_d, rows_d],
        [jax.ShapeDtypeStruct((S, D), BF16), jax.ShapeDtypeStruct((S, E), BF16), jax.ShapeDtypeStruct((S, E), BF16),
         jax.ShapeDtypeStruct((S, E), BF16), jax.ShapeDtypeStruct((S, D), BF16), jax.ShapeDtypeStruct((S, D), F32)],
        [pltpu.VMEM((NCHIP, D, D), BF16), pltpu.VMEM((4, GW, GW), BF16), pltpu.VMEM((NCHIP, GW, D), BF16),
         pltpu.VMEM((tm + HALO, E), F32), pltpu.SemaphoreType.DMA((3,))],
        _params(1, 60), (h, g, shift, scale, asc, gate, w_in, wg, w_out), comm)


def _out_fwd(a, z, w, gate, h, name, head=None, comm=None, tm=512):
    S = h.shape[0]
    kb = E // NCHIP
    n_in = 5 if head is None else 7

    def body(*refs):
        a_ref, z_ref, w_hbm, gate_ref, h_ref = refs[:5]
        w_v, sem = refs[-2:]
        i = pl.program_id(0)

        @pl.when(i == 0)
        def _():
            cp = pltpu.make_async_copy(w_hbm, w_v, sem)
            cp.start()
            cp.wait()

        y = None
        for p in range(NCHIP):
            cols = slice(p * kb, (p + 1) * kb)
            zz = z_ref[:, cols].astype(F32)
            act = (a_ref[:, cols].astype(F32) * (zz * jax.nn.sigmoid(zz))).astype(BF16)
            part = _nn(act, w_v[p])
            y = part if y is None else y + part
        refs[n_in][...] = y.astype(BF16)
        hh = h_ref[...] + gate_ref[...] * y
        if head is None:
            refs[n_in + 1][...] = hh
            return
        g_ref, t_ref = refs[5:7]
        dh_ref, st_ref = refs[n_in + 1:n_in + 3]

        @pl.when(i == 0)
        def _():
            st_ref[...] = jnp.zeros((SUBLANES, D), F32)

        r = lax.rsqrt(jnp.mean(hh * hh, axis=-1, keepdims=True) + EPS)
        xhat = hh * r
        diff = xhat * g_ref[...] - t_ref[...]
        st_ref[1:2, :] += _colsum(diff * diff)
        dout = diff * (1.0 / D)
        st_ref[0:1, :] += _colsum(dout * xhat)
        dx = dout * g_ref[...]
        dh_ref[...] = r * (dx - xhat * jnp.mean(dx * xhat, axis=-1, keepdims=True))

    rows_d = pl.BlockSpec((tm, D), lambda i: (i, 0))
    rows_e = pl.BlockSpec((tm, E), lambda i: (i, 0))
    in_specs = [rows_e, rows_e, ANY, _row(D), rows_d]
    out_specs = [rows_d, rows_d]
    out_shape = [jax.ShapeDtypeStruct((S, D), BF16), jax.ShapeDtypeStruct((S, D), F32)]
    args = (a, z, w, gate, h)
    if head is not None:
        in_specs += [_row(D), rows_d]
        out_specs += [pl.BlockSpec((SUBLANES, D), lambda i: (0, 0))]
        out_shape += [jax.ShapeDtypeStruct((SUBLANES, D), F32)]
        args += tuple(head)
    return _call(body, name, (S // tm,), in_specs, out_specs, out_shape,
                 [pltpu.VMEM((NCHIP, kb, D), BF16), pltpu.SemaphoreType.DMA], _params(1, 52), args, comm)


TW = BW + LANES


def _diag_onehot(transpose):
    shape = (TW, NRELP) if transpose else (NRELP, TW)
    j = lax.broadcasted_iota(jnp.int32, shape, 0 if transpose else 1)
    r = lax.broadcasted_iota(jnp.int32, shape, 1 if transpose else 0)
    idx = jnp.clip(PAD - (j - LANES), -REL_CLIP, REL_CLIP) + REL_CLIP
    return jnp.where(idx == r, 1.0, 0.0).astype(BF16)


def _strip_valid():
    m = lax.broadcasted_iota(jnp.int32, (NH, BW), 1)
    return m < (LEFT + 1) * CHUNK, m >= CHUNK


def _bias_build(rb, name):
    def body(rb_ref, a_ref, b_ref):
        x = rb_ref[...]
        hi = x.astype(BF16)
        r1 = x - hi.astype(F32)
        mid = r1.astype(BF16)
        lo = (r1 - mid.astype(F32)).astype(BF16)
        oh = _diag_onehot(False)
        diag = (_nn(hi, oh) + _nn(mid, oh)) + _nn(lo, oh)
        valid_a, valid_b = _strip_valid()
        for qi in range(CHUNK):
            a_ref[qi] = jnp.where(valid_a, pltpu.roll(diag, TW - (LANES - qi), 1)[:, :BW], NEG)
            b_ref[qi] = jnp.where(valid_b, pltpu.roll(diag, TW - (CHUNK - qi), 1)[:, :BW], NEG)

    vmem = pl.BlockSpec(memory_space=pltpu.VMEM)
    return pl.pallas_call(
        body, name=name, in_specs=[vmem], out_specs=[vmem, vmem],
        out_shape=[jax.ShapeDtypeStruct((CHUNK, NH, BW), F32), jax.ShapeDtypeStruct((CHUNK, NH, BW), F32)],
        compiler_params=pltpu.CompilerParams(vmem_limit_bytes=32 * 2 ** 20),
    )(rb)


def _dbias_reduce(dba, dbb, name):
    def body(a_ref, b_ref, o_ref):
        valid_a, valid_b = _strip_valid()
        zeros = jnp.zeros((NH, TW - BW), F32)
        acc = jnp.zeros((NH, TW), F32)
        for qi in range(CHUNK):
            xa = jnp.concatenate([jnp.where(valid_a, a_ref[qi], 0.0), zeros], axis=1)
            xb = jnp.concatenate([jnp.where(valid_b, b_ref[qi], 0.0), zeros], axis=1)
            acc = acc + (pltpu.roll(xa, LANES - qi, 1) + pltpu.roll(xb, CHUNK - qi, 1))
        oh = _diag_onehot(True)
        hi = acc.astype(BF16)
        mid = (acc - hi.astype(F32)).astype(BF16)
        r = lax.broadcasted_iota(jnp.int32, (NH, NRELP), 1)
        near = jnp.where(r < 2 * REL_CLIP, _nn(hi, oh) + _nn(mid, oh), 0.0)
        o_ref[...] = jnp.where(r == 2 * REL_CLIP, -jnp.sum(near, axis=-1, keepdims=True), near)

    vmem = pl.BlockSpec(memory_space=pltpu.VMEM)
    return pl.pallas_call(
        body, name=name, in_specs=[vmem, vmem], out_specs=vmem,
        out_shape=jax.ShapeDtypeStruct((NH, NRELP), F32),
        compiler_params=pltpu.CompilerParams(vmem_limit_bytes=32 * 2 ** 20),
    )(dba, dbb)


def _build_bias(bias3, ba_ref, bb_ref):
    bias3[NMASK] = jnp.full((QB, WIN), NEG, F32)
    for qc in range(QC):
        rows = slice(qc * CHUNK, (qc + 1) * CHUNK)
        if qc % 2 == 0:
            bias3[NMASK, rows, qc * CHUNK:qc * CHUNK + BW] = ba_ref[...] * LOG2E
        else:
            bias3[NMASK, rows, (qc - 1) * CHUNK:(qc - 1) * CHUNK + BW] = bb_ref[...] * LOG2E
    col = lax.broadcasted_iota(jnp.int32, (QB, WIN), 1)
    for sub in range(NMASK):
        bias3[sub] = jnp.where(col < PAD - sub * QB, NEG, bias3[NMASK])


def _nsub(S):
    n = min(NSUB, S // QB)
    assert S % (n * QB) == 0 and n >= NMASK
    return n


def _row0(i, sub, nsub):
    return pl.multiple_of((i * nsub + sub) * QB, QB)


def _scores(q_ref, k_ref, i, sub, nsub):
    return _nt(q_ref[sub * QB:(sub + 1) * QB, :], k_ref[pl.ds(_row0(i, sub, nsub), WIN), :])


HALF = QB // 2
LIVE = WIN - LANES


def _live(half):
    return slice(half * HALF, (half + 1) * HALF), slice(half * LANES, half * LANES + LIVE)


def _widen(x, half):
    zeros = jnp.zeros((HALF, LANES), x.dtype)
    return jnp.concatenate([x, zeros] if half == 0 else [zeros, x], axis=1)


def _probs(s, bias3, i, sub):
    which = jnp.where(i == 0, sub, NMASK) if sub < NMASK else NMASK
    out = []
    for half in range(2):
        rows, cols = _live(half)
        t = s[rows, cols] * (SM_SCALE * LOG2E) + bias3[which, rows, cols]
        e = jnp.exp2(t - jnp.max(t, axis=-1, keepdims=True))
        out.append(_widen((e * (1.0 / jnp.sum(e, axis=-1, keepdims=True))).astype(BF16), half))
    return jnp.concatenate(out, axis=0)


def _attn_fwd(q, kp, vp, ba, bb, name, comm=None):
    S = q.shape[0]
    nsub = _nsub(S)
    R = nsub * QB

    def body(q_ref, k_ref, v_ref, ba_ref, bb_ref, o_ref, p_ref, bias3):
        i = pl.program_id(1)

        @pl.when(i == 0)
        def _():
            _build_bias(bias3, ba_ref, bb_ref)

        s_next = _scores(q_ref, k_ref, i, 0, nsub)
        for sub in range(nsub):
            s = s_next
            if sub + 1 < nsub:
                s_next = _scores(q_ref, k_ref, i, sub + 1, nsub)
            pb = _probs(s, bias3, i, sub)
            p_ref[sub] = pb
            o_ref[sub * QB:(sub + 1) * QB, :] = _nn(pb, v_ref[pl.ds(_row0(i, sub, nsub), WIN), :]).astype(BF16)

    return _call(
        body, name, (NH, S // R),
        [pl.BlockSpec((R, HD), lambda h, i: (i, h)), pl.BlockSpec((S + PAD, HD), lambda h, i: (0, h)),
         pl.BlockSpec((S + PAD, HD), lambda h, i: (0, h)), pl.BlockSpec((None, CHUNK, BW), lambda h, i: (h, 0, 0)),
         pl.BlockSpec((None, CHUNK, BW), lambda h, i: (h, 0, 0))],
        [pl.BlockSpec((R, HD), lambda h, i: (i, h)), pl.BlockSpec((None, nsub, QB, WIN), lambda h, i: (h, i, 0, 0))],
        [jax.ShapeDtypeStruct((S, E), BF16), jax.ShapeDtypeStruct((NH, S // QB, QB, WIN), BF16)],
        [pltpu.VMEM((NMASK + 1, QB, WIN), F32)],
        _params(2, 48), (q, kp, vp, ba, bb), comm)


def _store_grad(acc, stage, dw_hbm, sem):
    for q in range(NCHIP):
        stage[...] = acc[q].astype(BF16)
        cp = pltpu.make_async_copy(stage, dw_hbm.at[q], sem)
        cp.start()
        cp.wait()


def _out_bwd(dh, y, gate, a, cs, z, w, name, comm=None, tm=256):
    S = dh.shape[0]
    kb = E // NCHIP
    cb = 256
    n_t = S // tm

    def body(dh_ref, y_ref, gate_ref, a_ref, cs_ref, z_ref, w_hbm, da_ref, dz_ref, dw_hbm, st_ref, w_v, acc, stage, sem):
        i = pl.program_id(0)

        @pl.when(i == 0)
        def _():
            cp = pltpu.make_async_copy(w_hbm, w_v, sem)
            cp.start()
            acc[...] = jnp.zeros(acc.shape, F32)
            st_ref[...] = jnp.zeros((SUBLANES, D), F32)
            cp.wait()

        dhh = dh_ref[...]
        st_ref[0:1, :] += _colsum(dhh * y_ref[...].astype(F32))
        dy = (dhh * gate_ref[...]).astype(BF16)
        for blk in range(E // cb):
            p, r0 = divmod(blk * cb, kb)
            cols = slice(blk * cb, (blk + 1) * cb)
            zz = z_ref[:, cols].astype(F32)
            sig = jax.nn.sigmoid(zz)
            sz = zz * sig
            ae = a_ref[:, cols].astype(F32) * cs_ref[:, cols]
            acc[p, r0:r0 + cb, :] += _tn((ae * sz).astype(BF16), dy)
            dact = _nt(dy, w_v[p, r0:r0 + cb, :])
            da_ref[:, cols] = (dact * sz).astype(BF16)
            dz_ref[:, cols] = (dact * ae * (sig * (1.0 + zz * (1.0 - sig)))).astype(BF16)

        @pl.when(i == n_t - 1)
        def _():
            _store_grad(acc, stage, dw_hbm, sem)

    return _call(
        body, name, (n_t,),
        [pl.BlockSpec((tm, D), lambda i: (i, 0)), pl.BlockSpec((tm, D), lambda i: (i, 0)), _row(D),
         pl.BlockSpec((tm, E), lambda i: (i, 0)), _row(E), pl.BlockSpec((tm, E), lambda i: (i, 0)), ANY],
        [pl.BlockSpec((tm, E), lambda i: (i, 0)), pl.BlockSpec((tm, E), lambda i: (i, 0)), ANY,
         pl.BlockSpec((SUBLANES, D), lambda i: (0, 0))],
        [jax.ShapeDtypeStruct((S, E), BF16), jax.ShapeDtypeStruct((S, E), BF16),
         jax.ShapeDtypeStruct((NCHIP, kb, D), BF16), jax.ShapeDtypeStruct((SUBLANES, D), F32)],
        [pltpu.VMEM((NCHIP, kb, D), BF16), pltpu.VMEM((NCHIP, kb, D), F32), pltpu.VMEM((kb, D), BF16),
         pltpu.SemaphoreType.DMA],
        _params(1, 52), (dh, y, gate, a, cs, z, w), comm)


def _attn_bwd(q, kp, vp, probs, do, prev, name, comm=None):
    S = q.shape[0]
    nsub = _nsub(S)
    R = nsub * QB
    n_i = S // R
    dt_kv = F32 if prev is None else BF16

    def body(*refs):
        q_ref, k_ref, v_ref, p_ref, do_ref = refs[:5]
        refs = refs[5:]
        if prev is not None:
            pk_hbm, pv_hbm = refs[:2]
            refs = refs[2:]
        dq_ref, dk_ref, dv_ref, dba_ref, dbb_ref, dbias, dk_acc, dv_acc = refs[:8]
        if prev is not None:
            pk_v, pv_v, sems = refs[8:]
        h = pl.program_id(0)
        i = pl.program_id(1)

        def prev_copies():
            cols = pl.ds(pl.multiple_of(h * HD, HD), HD)
            return (pltpu.make_async_copy(pk_hbm.at[:, cols], pk_v, sems.at[0]),
                    pltpu.make_async_copy(pv_hbm.at[:, cols], pv_v, sems.at[1]))

        @pl.when(i == 0)
        def _():
            if prev is not None:
                for cp in prev_copies():
                    cp.start()
            dbias[...] = jnp.zeros((2, CHUNK, DBW), F32)
            dk_acc[...] = jnp.zeros((S + PAD, HD), F32)
            dv_acc[...] = jnp.zeros((S + PAD, HD), F32)

        def mxu_in(sub):
            return _nt(do_ref[sub * QB:(sub + 1) * QB, :], v_ref[pl.ds(_row0(i, sub, nsub), WIN), :])

        nxt = mxu_in(0)
        for sub in range(nsub):
            rows = slice(sub * QB, (sub + 1) * QB)
            win = pl.ds(_row0(i, sub, nsub), WIN)
            dp = nxt
            if sub + 1 < nsub:
                nxt = mxu_in(sub + 1)
            parts = []
            for half in range(2):
                hrows, hcols = _live(half)
                p = p_ref[sub, hrows, hcols].astype(F32)
                dph = dp[hrows, hcols]
                ds = p * (dph - jnp.sum(p * dph, axis=-1, keepdims=True))
                dbias[0] += ds[0:CHUNK, LIVE - DBW:LIVE]
                dbias[1] += ds[CHUNK:HALF, LIVE - DBW:LIVE]
                parts.append(_widen((ds * SM_SCALE).astype(BF16), half))
            dsb = jnp.concatenate(parts, axis=0)
            dq_ref[rows, :] = _nn(dsb, k_ref[win, :]).astype(BF16)
            dk_acc[win, :] += _tn(dsb, q_ref[rows, :])
            dv_acc[win, :] += _tn(p_ref[sub], do_ref[rows, :])

        @pl.when(i == n_i - 1)
        def _():
            zeros = jnp.zeros((CHUNK, BW - DBW), F32)
            dba_ref[...] = jnp.concatenate([zeros, dbias[0]], axis=1)
            dbb_ref[...] = jnp.concatenate([zeros, dbias[1]], axis=1)
            if prev is None:
                dk_ref[...] = dk_acc[...]
                dv_ref[...] = dv_acc[...]
            else:
                for cp in prev_copies():
                    cp.wait()
                dk_ref[...] = (dk_acc[...] + pk_v[...]).astype(BF16)
                dv_ref[...] = (dv_acc[...] + pv_v[...]).astype(BF16)

    head = pl.BlockSpec((S + PAD, HD), lambda h, i: (0, h))
    strip = pl.BlockSpec((None, CHUNK, BW), lambda h, i: (h, 0, 0))
    blk = pl.BlockSpec((R, HD), lambda h, i: (i, h))
    in_specs = [blk, head, head, pl.BlockSpec((None, nsub, QB, WIN), lambda h, i: (h, i, 0, 0)), blk]
    scratch = [pltpu.VMEM((2, CHUNK, DBW), F32), pltpu.VMEM((S + PAD, HD), F32), pltpu.VMEM((S + PAD, HD), F32)]
    args = (q, kp, vp, probs, do)
    if prev is not None:
        in_specs += [ANY, ANY]
        scratch += [pltpu.VMEM((S + PAD, HD), F32), pltpu.VMEM((S + PAD, HD), F32), pltpu.SemaphoreType.DMA((2,))]
        args += tuple(prev)
    return _call(
        body, name, (NH, n_i), in_specs, [blk, head, head, strip, strip],
        [jax.ShapeDtypeStruct((S, E), BF16), jax.ShapeDtypeStruct((S + PAD, E), dt_kv),
         jax.ShapeDtypeStruct((S + PAD, E), dt_kv), jax.ShapeDtypeStruct((NH, CHUNK, BW), F32),
         jax.ShapeDtypeStruct((NH, CHUNK, BW), F32)],
        scratch, _params(2, 56), args, comm)


def _pool_bwd(dms, mixed, pooled, wg, a_scale, name, comm=None, tm=512):
    S = dms.shape[0]
    n_t = S // tm

    def rev(i):
        return (n_t - 1 - i, 0)

    def body(d_ref, m_ref, p_ref, wg_ref, as_ref, dv_ref, dwg_ref, st_ref, buf):
        i = pl.program_id(0)

        @pl.when(i == 0)
        def _():
            buf[tm:tm + HALO, :] = jnp.zeros((HALO, E), F32)
            dwg_ref[...] = jnp.zeros((4, GW, GW), F32)
            st_ref[...] = jnp.zeros((SUBLANES, E), F32)

        t = (n_t - 1 - i) * tm + lax.broadcasted_iota(jnp.int32, (tm, 1), 0)
        st_ref[0:1, :] += _colsum(d_ref[...].astype(F32) * m_ref[...].astype(F32))
        for gi, w in enumerate(POOL_W):
            cols = slice(gi * GW, (gi + 1) * GW)
            dm = (d_ref[:, cols].astype(F32) * as_ref[:, cols]).astype(BF16)
            dpool = _nt(dm, wg_ref[gi])
            dwg_ref[gi] += _tn(p_ref[:, cols], dm)
            inv_cnt = 1.0 / jnp.minimum(t + 1, w).astype(F32)
            buf[0:tm, cols] = dpool * inv_cnt
            s = buf[:, cols]
            k = 1
            while k < w:
                s = s + pltpu.roll(s, tm + HALO - k, 0)
                k *= 2
            dv_ref[:, cols] = (s[0:tm, :] - dpool).astype(BF16)
        buf[tm:tm + HALO, :] = buf[0:HALO, :]

    return _call(
        body, name, (n_t,),
        [pl.BlockSpec((tm, E), rev), pl.BlockSpec((tm, E), rev), pl.BlockSpec((tm, E), rev),
         pl.BlockSpec((4, GW, GW), lambda i: (0, 0, 0)), _row(E)],
        [pl.BlockSpec((tm, E), rev), pl.BlockSpec((4, GW, GW), lambda i: (0, 0, 0)),
         pl.BlockSpec((SUBLANES, E), lambda i: (0, 0))],
        [jax.ShapeDtypeStruct((S, E), BF16), jax.ShapeDtypeStruct((4, GW, GW), F32),
         jax.ShapeDtypeStruct((SUBLANES, E), F32)],
        [pltpu.VMEM((tm + HALO, E), F32)],
        _params(1, 52), (dms, mixed, pooled, wg, a_scale), comm)


def _in_bwd(da, db, row_off, u, h, g, scale, w, dh_out, name, comm=None, tm=256):
    S = h.shape[0]
    n_t = S // tm
    off = row_off // tm

    def body(da_ref, db_ref, u_ref, h_ref, g_ref, sc_ref, w_hbm, dho_ref, dhi_ref, dw_hbm, st_ref, w_v, acc, stage, sem):
        i = pl.program_id(0)

        @pl.when(i == 0)
        def _():
            cp = pltpu.make_async_copy(w_hbm, w_v, sem)
            cp.start()
            acc[...] = jnp.zeros(acc.shape, F32)
            st_ref[...] = jnp.zeros((SUBLANES, D), F32)
            cp.wait()

        ub = u_ref[...]
        du = None
        for q in range(NCHIP):
            d_ref = da_ref if q < 2 else db_ref
            dv = d_ref[:, (q % 2) * D:(q % 2 + 1) * D]
            acc[q] += _tn(ub, dv)
            part = _nt(dv, w_v[q])
            du = part if du is None else du + part

        hh = h_ref[...]
        r = lax.rsqrt(jnp.mean(hh * hh, axis=-1, keepdims=True) + EPS)
        xhat = hh * r
        gg = g_ref[...]
        st_ref[0:1, :] += _colsum(du)
        st_ref[1:2, :] += _colsum(du * (xhat * gg))
        dn = du * (1.0 + sc_ref[...])
        st_ref[2:3, :] += _colsum(dn * xhat)
        dx = dn * gg
        dhi_ref[...] = dho_ref[...] + r * (dx - xhat * jnp.mean(dx * xhat, axis=-1, keepdims=True))

        @pl.when(i == n_t - 1)
        def _():
            _store_grad(acc, stage, dw_hbm, sem)

    part_spec = pl.BlockSpec((tm, E), lambda i: (i + off, 0))
    return _call(
        body, name, (n_t,),
        [part_spec, part_spec, pl.BlockSpec((tm, D), lambda i: (i, 0)), pl.BlockSpec((tm, D), lambda i: (i, 0)),
         _row(D), _row(D), ANY, pl.BlockSpec((tm, D), lambda i: (i, 0))],
        [pl.BlockSpec((tm, D), lambda i: (i, 0)), ANY, pl.BlockSpec((SUBLANES, D), lambda i: (0, 0))],
        [jax.ShapeDtypeStruct((S, D), F32), jax.ShapeDtypeStruct((NCHIP, D, D), BF16),
         jax.ShapeDtypeStruct((SUBLANES, D), F32)],
        [pltpu.VMEM((NCHIP, D, D), BF16), pltpu.VMEM((NCHIP, D, D), F32), pltpu.VMEM((D, D), BF16),
         pltpu.SemaphoreType.DMA],
        _params(1, 56), (da, db, u, h, g, scale, w, dh_out), comm)


def _grad_ada(c_act_t, dmod, name):
    L, _, n = dmod.shape

    def body(c_ref, d_ref, o_ref):
        acc = None
        for b in range(SUBLANES):
            part = c_ref[:, b:b + 1] * d_ref[b:b + 1, :]
            acc = part if acc is None else acc + part
        o_ref[...] = acc

    return pl.pallas_call(
        body, name=name, grid=(L,),
        in_specs=[pl.BlockSpec((D, SUBLANES), lambda l: (0, 0)), pl.BlockSpec((None, SUBLANES, n), lambda l: (l, 0, 0))],
        out_specs=pl.BlockSpec((None, D, n), lambda l: (l, 0, 0)),
        out_shape=jax.ShapeDtypeStruct((L, D, n), F32),
        compiler_params=_params(1, 32),
    )(c_act_t, dmod)


def _stats_reduce(g3, loss_row, name):
    n_dev, rows, _ = g3.shape

    def body(g_ref, o_ref, l_ref):
        acc = g_ref[0]
        for d in range(1, n_dev):
            acc = acc + g_ref[d]
        o_ref[...] = acc
        tot = jnp.sum(o_ref[loss_row:loss_row + 1, :], axis=-1, keepdims=True)
        l_ref[...] = jnp.broadcast_to(tot * (0.5 / D), (SUBLANES, LANES))

    return pl.pallas_call(
        body, name=name,
        in_specs=[pl.BlockSpec(memory_space=pltpu.VMEM)],
        out_specs=[pl.BlockSpec(memory_space=pltpu.VMEM), pl.BlockSpec(memory_space=pltpu.VMEM)],
        out_shape=[jax.ShapeDtypeStruct((rows, D), F32), jax.ShapeDtypeStruct((SUBLANES, LANES), F32)],
        compiler_params=pltpu.CompilerParams(vmem_limit_bytes=32 * 2 ** 20),
    )(g3)


def _sum4(own, land, chip, name, tr=256):
    _, R, C = own.shape
    tr = min(tr, R)

    def body(p_ref, own_ref, land_ref, o_ref):
        o_ref[...] = ((own_ref[...].astype(F32) + land_ref[0].astype(F32)) + land_ref[1].astype(F32)) + land_ref[2].astype(F32)

    out = pl.pallas_call(
        body, name=name,
        grid_spec=pltpu.PrefetchScalarGridSpec(
            num_scalar_prefetch=1, grid=(R // tr,),
            in_specs=[pl.BlockSpec((None, tr, C), lambda i, p: (p[0], i, 0)), pl.BlockSpec((3, tr, C), lambda i, p: (0, i, 0))],
            out_specs=pl.BlockSpec((tr, C), lambda i, p: (i, 0))),
        out_shape=jax.ShapeDtypeStruct((R, C), F32),
        compiler_params=_params(1, 32),
    )(chip, pltpu.with_memory_space_constraint(own, pltpu.HBM), pltpu.with_memory_space_constraint(land, pltpu.HBM))
    return pltpu.with_memory_space_constraint(out, pltpu.HBM)


def _adamw(w, m, v, g, name, tr=256):
    L, R, C = w.shape
    tr = min(tr, R)
    stacked = not isinstance(g, (list, tuple))
    n_g = None if stacked else [len(ps) for ps in g]
    flat = [g] if stacked else [a for ps in g for a in ps]

    def body(*refs):
        w_ref, m_ref, v_ref = refs[:3]
        g_refs = refs[3:3 + len(flat)]
        go_ref, d_ref, mo_ref, vo_ref = refs[3 + len(flat):]
        if stacked:
            gg = g_refs[0][...]
        else:
            layer = pl.program_id(0)
            gg = None
            k = 0
            for li in range(L):
                gl = None
                for _ in range(n_g[li]):
                    x = g_refs[k][...]
                    gl = x if gl is None else gl + x
                    k += 1
                gg = gl if gg is None else jnp.where(layer == li, gl, gg)
        m2 = ADAM_B1 * m_ref[...] + (1.0 - ADAM_B1) * gg
        v2 = ADAM_B2 * v_ref[...] + (1.0 - ADAM_B2) * (gg * gg)
        m_hat = m2 / (1.0 - ADAM_B1 ** ADAM_STEP)
        v_hat = v2 / (1.0 - ADAM_B2 ** ADAM_STEP)
        go_ref[...] = gg
        d_ref[...] = -ADAM_LR * (m_hat / (jnp.sqrt(v_hat) + ADAM_EPS) + ADAM_WD * w_ref[...])
        mo_ref[...] = m2
        vo_ref[...] = v2

    big = pl.BlockSpec((None, tr, C), lambda l, i: (l, i, 0))
    g_specs = [big] if stacked else [pl.BlockSpec((tr, C), lambda l, i: (i, 0))] * len(flat)
    return pl.pallas_call(
        body, name=name, grid=(L, R // tr),
        in_specs=[big, big, big] + g_specs,
        out_specs=[big, big, big, big],
        out_shape=[jax.ShapeDtypeStruct((L, R, C), F32)] * 4,
        compiler_params=_params(2, 48),
    )(w, m, v, *flat)


GATHER8_SEMS = [pltpu.SemaphoreType.DMA((7,)), pltpu.SemaphoreType.DMA((7,)), pltpu.SemaphoreType.DMA]


def _gather8(x_ref, out_ref, send_sems, recv_sems, local_sem):
    m = x_ref.shape[0]
    x, y, c = _place()
    me, sibling = (x, y, c), (x, y, 1 - c)
    chips = [(1 - x, y), (x, 1 - y), (1 - x, 1 - y)]

    def rows(px, py, pc):
        return out_ref.at[pl.ds((4 * px + 2 * py + pc) * m, m), :]

    def copy(k, block, to, src=None):
        return pltpu.make_async_remote_copy(
            src_ref=rows(*block) if src is None else src, dst_ref=rows(*block),
            send_sem=send_sems.at[k], recv_sem=recv_sems.at[k], device_id=to, device_id_type=MESH)

    mine = pltpu.make_async_copy(x_ref, rows(*me), local_sem)
    mine.start()
    first = [copy(0, me, sibling, src=x_ref)]
    first += [copy(1 + j, me, (*chip, c), src=x_ref) for j, chip in enumerate(chips)]
    for cp in first:
        cp.start()
    passed = [copy(4 + j, (*chip, c), sibling) for j, chip in enumerate(chips)]
    for j, chip in enumerate(chips):
        copy(1 + j, (*chip, c), me).wait_recv()
        passed[j].start()
    copy(0, sibling, me).wait_recv()
    for j, chip in enumerate(chips):
        copy(4 + j, (*chip, 1 - c), me).wait_recv()
    for cp in first + passed:
        cp.wait_send()
    mine.wait()


def _allgather8(xs, name, comm=None):
    m, n = xs.shape
    n_c = 0 if comm is None else comm.n

    def body(*refs):
        x_ref, out_ref = refs[0], refs[1 + n_c]
        c_refs = (refs[1:1 + n_c], refs[2 + n_c:2 + 2 * n_c]) + tuple(refs[5 + 2 * n_c:])
        if comm is not None:
            comm.start(*c_refs)
        _gather8(x_ref, out_ref, *refs[2 + 2 * n_c:5 + 2 * n_c])
        if comm is not None:
            comm.wait(*c_refs)

    vmem = pl.BlockSpec(memory_space=pltpu.VMEM)
    outs = pl.pallas_call(
        body, name=name,
        out_shape=[jax.ShapeDtypeStruct((8 * m, n), xs.dtype)] + ([] if comm is None else comm.out_shape),
        in_specs=[vmem] + [ANY] * n_c,
        out_specs=[vmem] + [ANY] * n_c,
        scratch_shapes=GATHER8_SEMS + ([] if comm is None else comm.scratch),
        compiler_params=pltpu.CompilerParams(vmem_limit_bytes=32 * 2 ** 20),
    )(xs, *([] if comm is None else comm.arrays))
    return outs[0], list(outs[1:])


def _prologue(c8, ada_w, ada_b, kv_ada_w, kv_ada_b, extra, comm, name):
    L, _, n = ada_w.shape
    k = kv_ada_w.shape[1]
    e = extra.shape[1]
    width = L * n + k + e
    n_c = comm.n

    def body(*refs):
        c_ref, w_hbm, b_ref, kw_hbm, kb_ref, x_ref = refs[:6]
        c_in = refs[6:6 + n_c]
        ca_ref, out_ref = refs[6 + n_c:8 + n_c]
        c_out = refs[8 + n_c:8 + 2 * n_c]
        cbuf, wbuf, kbuf, part, wsems = refs[8 + 2 * n_c:13 + 2 * n_c]
        sems_a = refs[13 + 2 * n_c:16 + 2 * n_c]
        sems_b = refs[16 + 2 * n_c:19 + 2 * n_c]
        c_refs = (c_in, c_out) + tuple(refs[19 + 2 * n_c:])
        comm.start(*c_refs)

        def fetch(l):
            return pltpu.make_async_copy(w_hbm.at[l], wbuf.at[l % 2], wsems.at[l % 2])

        fetch(0).start()
        kv_copy = pltpu.make_async_copy(kw_hbm, kbuf, wsems.at[2])
        kv_copy.start()
        _gather8(c_ref, cbuf, *sems_a)
        cc = jnp.concatenate([cbuf[SUBLANES * d:SUBLANES * d + 1, :] for d in range(8)], axis=0)
        ca = cc * jax.nn.sigmoid(cc)
        ca_ref[...] = ca
        cab = ca.astype(BF16)
        for l in range(L):
            fetch(l).wait()
            if l + 1 < L:
                fetch(l + 1).start()
            part[:, l * n:(l + 1) * n] = _nn(cab, wbuf[l % 2].astype(BF16)) + b_ref[l]
        kv_copy.wait()
        part[:, L * n:L * n + k] = _nn(cab, kbuf[...].astype(BF16)) + kb_ref[...]
        part[:, L * n + k:] = jnp.broadcast_to(x_ref[...], (SUBLANES, e))
        _gather8(part, out_ref, *sems_b)
        comm.wait(*c_refs)

    vmem = pl.BlockSpec(memory_space=pltpu.VMEM)
    outs = pl.pallas_call(
        body, name=name,
        out_shape=[jax.ShapeDtypeStruct((SUBLANES, D), F32), jax.ShapeDtypeStruct((8 * SUBLANES, width), F32)] + comm.out_shape,
        in_specs=[vmem, ANY, vmem, ANY, vmem, vmem] + [ANY] * n_c,
        out_specs=[vmem, vmem] + [ANY] * n_c,
        scratch_shapes=[pltpu.VMEM((8 * SUBLANES, D), F32), pltpu.VMEM((2, D, n), F32), pltpu.VMEM((D, k), F32),
                        pltpu.VMEM((SUBLANES, width), F32), pltpu.SemaphoreType.DMA((3,))] + GATHER8_SEMS + GATHER8_SEMS
        + comm.scratch,
        compiler_params=pltpu.CompilerParams(vmem_limit_bytes=32 * 2 ** 20),
    )(c8, ada_w, ada_b, kv_ada_w, kv_ada_b, extra, *comm.arrays)
    return outs[0], outs[1], list(outs[2:])


def _pad8(a):
    return jnp.pad(a, ((0, SUBLANES - a.shape[0]), (0, 0)))


def _group_rows(wg):
    return wg.transpose(1, 0, 2, 3).reshape(4, GW, GW)


def _example_step(h0, tgt, mods, kvmod, a_scale, norm_g, kv_norm_g, final_g, b_rel_bias, sh, w_first, chip_arr):
    ones_e = jnp.ones((1, E), F32)
    shift = [mods[l:l + 1, 0:D] for l in range(4)]
    scale = [mods[l:l + 1, D:2 * D] for l in range(4)]
    gate = [mods[l:l + 1, 2 * D:3 * D] for l in range(4)]
    gl = [norm_g[l:l + 1] for l in range(4)]
    kv_shift, kv_scale = kvmod[None, 0:D], kvmod[None, D:2 * D]
    kv_g = kv_norm_g[None]

    w_a = w_first
    hs = [h0]
    saved = []
    nxt = [[sh["a_in"][1], sh["a_grp"][1], sh["a_out"][1]], [sh["kv"][0], sh["b_in"][0]]]
    for l in range(2):
        w_in_l, wg_l, wo_l = w_a
        wg_full = _group_rows(wg_l)
        (u, z, pooled, mixed, y, hn), got = _a_fwd(hs[-1], gl[l], shift[l], scale[l], a_scale[l:l + 1], gate[l], w_in_l,
                                                   wg_full, wo_l, f"a{l}_fwd", comm=_Comm(gathers=nxt[l]))
        saved.append((u, z, pooled, mixed, y, w_in_l, wg_full, wo_l))
        hs.append(hn)
        if l == 0:
            w_a = got
        else:
            w_kv, wb_in0 = got

    (uk, kp, vp), _ = _in_fwd(hs[2], kv_g, kv_shift, kv_scale, w_kv, BF16, BF16, "kv_in_fwd", pad_rows=PAD)
    wb_in = [wb_in0, None]
    wb_out = [None, None]

    for bi in range(2):
        l = 2 + bi
        sa, sb = _bias_build(jnp.pad(b_rel_bias[bi], ((0, 0), (0, NRELP - NREL))), f"b{bi}_bias")
        (u, q, z), _ = _in_fwd(hs[-1], gl[l], shift[l], scale[l], wb_in[bi], BF16, BF16, f"b{bi}_in_fwd")
        comm = _Comm(gathers=[sh["b_out"][0], sh["b_in"][1], sh["b_out"][1]]) if bi == 0 else None
        (att, probs), got = _attn_fwd(q, kp, vp, sa.transpose(1, 0, 2), sb.transpose(1, 0, 2), f"b{bi}_attn_fwd", comm=comm)
        if bi == 0:
            wb_out[0], wb_in[1], wb_out[1] = got
        if bi == 0:
            (y, hn), _ = _out_fwd(att, z, wb_out[bi], gate[l], hs[-1], f"b{bi}_out_fwd")
            hs.append(hn)
        else:
            (y, dh, st_fin), _ = _out_fwd(att, z, wb_out[bi], gate[l], hs[-1], f"b{bi}_out_fwd", head=(final_g[None], tgt))
        saved.append((u, z, q, att, y, probs))

    st_in = [None] * 4
    st_out = [None] * 4
    grads = {}
    landed = {}

    def carry(names):
        return _Comm(scatters=[grads[n] for n in names]) if names else None

    def land(names, got):
        for n, a in zip(names, got):
            landed[n] = a

    u, z, q, att, y, probs = saved[3]
    (datt, dz, grads["b_out1"], st_out[3]), _ = _out_bwd(dh, y, gate[3], att, ones_e, z, wb_out[1], "b1_out_bwd")
    (dq, dk1, dv1, dsa, dsb), _ = _attn_bwd(q, kp, vp, probs, datt, None, "b1_attn_bwd")
    drb1 = _dbias_reduce(dsa.transpose(1, 0, 2), dsb.transpose(1, 0, 2), "b1_dbias")
    (dh, grads["b_in1"], st_in[3]), _ = _in_bwd(dq, dz, 0, u, hs[3], gl[3], scale[3], wb_in[1], dh, "b1_in_bwd")
    u, z, q, att, y, probs = saved[2]
    (datt, dz, grads["b_out0"], st_out[2]), _ = _out_bwd(dh, y, gate[2], att, ones_e, z, wb_out[0], "b0_out_bwd")
    (dq, dk, dv, dsa, dsb), got = _attn_bwd(q, kp, vp, probs, datt, (dk1, dv1), "b0_attn_bwd",
                                            comm=carry(["b_out1", "b_in1", "b_out0"]))
    land(["b_out1", "b_in1", "b_out0"], got)
    drb0 = _dbias_reduce(dsa.transpose(1, 0, 2), dsb.transpose(1, 0, 2), "b0_dbias")
    (dh, grads["b_in0"], st_in[2]), _ = _in_bwd(dq, dz, 0, u, hs[2], gl[2], scale[2], wb_in[0], dh, "b0_in_bwd")
    (dh, grads["kv"], st_kv), got = _in_bwd(dk, dv, PAD, uk, hs[2], kv_g, kv_scale, w_kv, dh, "kv_in_bwd",
                                            comm=carry(["b_in0"]))
    land(["b_in0"], got)
    st_pool = [None] * 2
    plan = {1: dict(o=[], p=[], i=["kv", "a_out1", "a_grp1"]), 0: dict(o=["a_in1"], p=["a_out0"], i=[])}
    early = ["b_out1", "b_in1", "b_out0", "b_in0", "kv", "a_out1", "a_grp1", "a_in1"]
    late = ["a_out0", "a_grp0", "a_in0"]
    both = {}

    def sum4(n):
        return _sum4(grads[n], landed[n], chip_arr, f"sum4_{n}")

    for l in (1, 0):
        u, z, pooled, mixed, y, w_in_l, wg_full, wo = saved[l]
        asl = a_scale[l:l + 1]
        (dms, dz, grads[f"a_out{l}"], st_out[l]), got = _out_bwd(dh, y, gate[l], mixed, asl, z, wo, f"a{l}_out_bwd",
                                                                comm=carry(plan[l]["o"]))
        land(plan[l]["o"], got)
        comm = carry(plan[l]["p"])
        if l == 0:
            mine = [sum4(n) for n in early]
            comm = _Comm(scatters=[grads[n] for n in plan[l]["p"]], swaps=mine)
        (dval, dwg, st_pool[l]), got = _pool_bwd(dms, mixed, pooled, wg_full, asl, f"a{l}_pool_bwd", comm=comm)
        land(plan[l]["p"], got)
        if l == 0:
            both.update({n: [a, b] for n, a, b in zip(early, mine, got[len(plan[l]["p"]):])})
        grads[f"a_grp{l}"] = (dwg.reshape(4, NCHIP, GW // NCHIP, GW).transpose(1, 0, 2, 3).reshape(NCHIP, GW, GW)
                              .astype(BF16))
        (dh, grads[f"a_in{l}"], st_in[l]), got = _in_bwd(dval, dz, 0, u, hs[l], gl[l], scale[l], w_in_l, dh, f"a{l}_in_bwd",
                                                         comm=carry(plan[l]["i"]))
        land(plan[l]["i"], got)
    pieces = st_in + [st_kv] + st_out + [st_fin]
    pieces += [_pad8(st_pool[l][0].reshape(2, D)) for l in range(2)]
    pieces += [_pad8(d.reshape(NH * NRELP // D, D)) for d in (drb0, drb1)]
    gathered, got = _allgather8(jnp.concatenate(pieces, axis=0), "gather_stats", comm=carry(["a_grp0", "a_in0"]))
    land(["a_grp0", "a_in0"], got)
    mine = [sum4(n) for n in late]
    both.update({n: [a, b] for n, a, b in zip(late, mine, _comm_only(_Comm(swaps=mine), "swap_last"))})
    return dh, both, gathered.reshape(8, N_STAT, D)


ROW_IN = [8 * l for l in range(4)]
ROW_KV = 32
ROW_OUT = [40 + 8 * l for l in range(4)]
ROW_FIN = 72
ROW_ASC = [80, 88]
ROW_RB = [96, 104]
N_STAT = 112


def kernel(x, c, ada_w, ada_b, norm_g, a_w_in, a_w_group, a_scale, a_w_out, kv_norm_g, kv_ada_w, kv_ada_b, w_kv, b_w_in, b_rel_bias, b_w_out, final_g, loss_target, m_ada_w, m_ada_b, m_norm_g, m_a_w_in, m_a_w_group, m_a_scale, m_a_w_out, m_kv_norm_g, m_kv_ada_w, m_kv_ada_b, m_w_kv, m_b_w_in, m_b_rel_bias, m_b_w_out, m_final_g, v_ada_w, v_ada_b, v_norm_g, v_a_w_in, v_a_w_group, v_a_scale, v_a_w_out, v_kv_norm_g, v_kv_ada_w, v_kv_ada_b, v_w_kv, v_b_w_in, v_b_rel_bias, v_b_w_out, v_final_g):
    xi, yi, ci = _place()
    chip = 2 * xi + yi
    dev = 4 * xi + 2 * yi + ci
    n_ada = ada_w.shape[2]
    n_kva = kv_ada_w.shape[1]
    n_asc = a_scale.shape[1]

    ada_b_sh = lax.dynamic_slice_in_dim(ada_b, chip * n_ada, n_ada, axis=1)
    kvb_sh = lax.dynamic_slice_in_dim(kv_ada_b, chip * n_kva, n_kva, axis=0)
    sh = dict(a_in=[a_w_in[l].astype(BF16) for l in range(2)], a_grp=[a_w_group[l].astype(BF16) for l in range(2)],
              a_out=[a_w_out[l].astype(BF16) for l in range(2)], kv=[w_kv.astype(BF16)],
              b_in=[b_w_in[l].astype(BF16) for l in range(2)], b_out=[b_w_out[l].astype(BF16) for l in range(2)])
    c_act, gathered, w_first = _prologue(
        jnp.broadcast_to(c, (SUBLANES, D)), ada_w, ada_b_sh[:, None, :], kv_ada_w, kvb_sh[None, :],
        a_scale.reshape(1, 2 * n_asc), _Comm(gathers=[sh["a_in"][0], sh["a_grp"][0], sh["a_out"][0]]), "prologue")
    rows = jnp.concatenate([lax.dynamic_slice_in_dim(gathered, SUBLANES * (2 * p + ci) + dev, 1, axis=0)
                            for p in range(NCHIP)], axis=0)
    mods = jnp.stack([rows[:, l * n_ada:(l + 1) * n_ada].reshape(3 * D) for l in range(4)])
    kvmod = rows[:, 4 * n_ada:4 * n_ada + n_kva].reshape(2 * D)
    o_asc = 4 * n_ada + n_kva
    a_scale_full = jnp.stack([rows[:, o_asc + l * n_asc:o_asc + (l + 1) * n_asc].reshape(E) for l in range(2)])

    chip_arr = jnp.reshape(chip, (1,)).astype(jnp.int32)
    dh, both, g3 = _example_step(x[0], loss_target[0], mods, kvmod, a_scale_full, norm_g, kv_norm_g, final_g,
                                 b_rel_bias, sh, w_first, chip_arr)
    grad_x = dh[None]

    red, loss_tile = _stats_reduce(g3, ROW_FIN + 1, "stats_reduce")
    loss = loss_tile[0, 0]

    def cat(rows_):
        return jnp.concatenate(rows_, axis=-1)

    g_ada_b = jnp.stack([cat([red[ROW_IN[l]], red[ROW_IN[l] + 1], red[ROW_OUT[l]]]) for l in range(4)])
    g_norm_g = jnp.stack([red[ROW_IN[l] + 2] for l in range(4)])
    g_kv_norm_g = red[ROW_KV + 2]
    g_kv_ada_b = cat([red[ROW_KV], red[ROW_KV + 1]])
    g_final_g = red[ROW_FIN]
    g_asc_full = jnp.stack([red[ROW_ASC[l]:ROW_ASC[l] + 2].reshape(E) for l in range(2)])
    g_a_scale = lax.dynamic_slice_in_dim(g_asc_full, chip * n_asc, n_asc, axis=1)
    g_rel = jnp.stack([red[ROW_RB[bi]:ROW_RB[bi] + NH * NRELP // D].reshape(NH, NRELP)[:, :NREL] for bi in range(2)])

    dmod = jnp.stack([cat([g3[:, ROW_IN[l]], g3[:, ROW_IN[l] + 1], g3[:, ROW_OUT[l]]]) for l in range(4)])
    dmod_sh = lax.dynamic_slice_in_dim(dmod, chip * n_ada, n_ada, axis=2)
    dkv = cat([g3[:, ROW_KV], g3[:, ROW_KV + 1]])[None]
    dkv_sh = lax.dynamic_slice_in_dim(dkv, chip * n_kva, n_kva, axis=2)
    c_act_t = c_act.T
    g_ada_w = _grad_ada(c_act_t, dmod_sh, "grad_ada_w")
    g_kv_ada_w = _grad_ada(c_act_t, dkv_sh, "grad_kv_ada_w")

    def upd(w, m, v, g, name, shape3):
        g = g.reshape(shape3) if not isinstance(g, list) else g
        outs = _adamw(w.reshape(shape3), m.reshape(shape3), v.reshape(shape3), g, name)
        return [o.reshape(w.shape) for o in outs]

    def pair(name):
        return [both[name + "0"], both[name + "1"]]

    res = {}
    res["ada_w"] = upd(ada_w, m_ada_w, v_ada_w, g_ada_w, "adamw_ada_w", ada_w.shape)
    res["ada_b"] = upd(ada_b, m_ada_b, v_ada_b, g_ada_b, "adamw_ada_b", (1,) + ada_b.shape)
    res["norm_g"] = upd(norm_g, m_norm_g, v_norm_g, g_norm_g, "adamw_norm_g", (1,) + norm_g.shape)
    res["a_w_in"] = upd(a_w_in, m_a_w_in, v_a_w_in, pair("a_in"), "adamw_a_w_in", a_w_in.shape)
    res["a_w_group"] = upd(a_w_group, m_a_w_group, v_a_w_group, pair("a_grp"), "adamw_a_w_group", (2, GW, GW))
    res["a_scale"] = upd(a_scale, m_a_scale, v_a_scale, g_a_scale, "adamw_a_scale", (1,) + a_scale.shape)
    res["a_w_out"] = upd(a_w_out, m_a_w_out, v_a_w_out, pair("a_out"), "adamw_a_w_out", a_w_out.shape)
    res["kv_norm_g"] = upd(kv_norm_g, m_kv_norm_g, v_kv_norm_g, g_kv_norm_g, "adamw_kv_norm_g", (1, 1, D))
    res["kv_ada_w"] = upd(kv_ada_w, m_kv_ada_w, v_kv_ada_w, g_kv_ada_w, "adamw_kv_ada_w", (1,) + kv_ada_w.shape)
    res["kv_ada_b"] = upd(kv_ada_b, m_kv_ada_b, v_kv_ada_b, g_kv_ada_b, "adamw_kv_ada_b", (1, 1, 2 * D))
    res["w_kv"] = upd(w_kv, m_w_kv, v_w_kv, [both["kv"]], "adamw_w_kv", (1,) + w_kv.shape)
    res["b_w_in"] = upd(b_w_in, m_b_w_in, v_b_w_in, pair("b_in"), "adamw_b_w_in", b_w_in.shape)
    res["b_rel_bias"] = upd(b_rel_bias, m_b_rel_bias, v_b_rel_bias, g_rel, "adamw_b_rel_bias", (1, 2 * NH, NREL))
    res["b_w_out"] = upd(b_w_out, m_b_w_out, v_b_w_out, pair("b_out"), "adamw_b_w_out", b_w_out.shape)
    res["final_g"] = upd(final_g, m_final_g, v_final_g, g_final_g, "adamw_final_g", (1, 1, D))

    names = ["ada_w", "ada_b", "norm_g", "a_w_in", "a_w_group", "a_scale", "a_w_out", "kv_norm_g", "kv_ada_w", "kv_ada_b",
             "w_kv", "b_w_in", "b_rel_bias", "b_w_out", "final_g"]
    return (loss, grad_x, *[res[n][0] for n in names], *[res[n][1] for n in names], *[res[n][2] for n in names],
            *[res[n][3] for n in names])
```

```python
import math

import jax
import jax.numpy as jnp
from jax import lax
from jax.experimental import pallas as pl
from jax.experimental.pallas import tpu as pltpu

F32 = jnp.float32
BF16 = jnp.bfloat16

D = 1024
E = 2048
NH = 16
HD = 128
CHUNK = 64
LEFT = 8
PAD = LEFT * CHUNK
NREL = 257
NRELP = 384
REL_CLIP = 128
EPS = 1e-6
NEG = -1e30
LOG2E = math.log2(math.e)
SM_SCALE = HD ** -0.5
POOL_W = (2, 4, 8, 16)
GW = 512
HALO = 16
QC = 4
QB = QC * CHUNK
NMASK = PAD // QB
WIN = (QC + LEFT) * CHUNK
BW = (LEFT + 2) * CHUNK
DBW = 4 * CHUNK
NSUB_FWD = 32
NSUB_BWD = 16
NCHIP = 4
LANES = 128
SUBLANES = 8

ADAM_LR = 0.001
ADAM_B1 = 0.9
ADAM_B2 = 0.999
ADAM_EPS = 1e-08
ADAM_WD = 0.01
ADAM_STEP = 10

MESH = pl.DeviceIdType.MESH
ANY = pl.BlockSpec(memory_space=pl.ANY)


def _params(n_axes, vmem_mb):
    return pltpu.CompilerParams(dimension_semantics=("arbitrary",) * n_axes, vmem_limit_bytes=vmem_mb * 2 ** 20)


def _nn(a, b):
    return jnp.dot(a, b, preferred_element_type=F32)


def _nt(a, b):
    return lax.dot_general(a, b, (((1,), (1,)), ((), ())), preferred_element_type=F32)


def _tn(a, b):
    return lax.dot_general(a, b, (((0,), (0,)), ((), ())), preferred_element_type=F32)


def _row(n):
    return pl.BlockSpec((1, n), lambda i: (0, 0))


def _colsum(x):
    return jnp.sum(x, axis=0, keepdims=True)


def _place():
    return lax.axis_index("x"), lax.axis_index("y"), lax.axis_index("c")


class _Comm:
    def __init__(self, gathers=(), scatters=(), swaps=()):
        self.n_g = len(gathers)
        self.n_chip = len(gathers) + len(scatters)
        self.n_sw = len(swaps)
        self.arrays = list(gathers) + list(scatters) + list(swaps)
        self.n = len(self.arrays)
        self.half = [a.shape[0] // 2 for a in gathers]
        self.out_shape = ([jax.ShapeDtypeStruct((NCHIP,) + a.shape, a.dtype) for a in gathers]
                          + [jax.ShapeDtypeStruct((3,) + a.shape[1:], a.dtype) for a in scatters]
                          + [jax.ShapeDtypeStruct(a.shape, a.dtype) for a in swaps])
        n_c, n_f, n_s = max(3 * self.n_chip, 1), max(3 * self.n_g, 1), max(self.n_sw, 1)
        self.scratch = [pltpu.SemaphoreType.DMA((n_c,)), pltpu.SemaphoreType.DMA((n_c,)),
                        pltpu.SemaphoreType.DMA((max(self.n_g, 1),)), pltpu.SemaphoreType.DMA((n_f,)),
                        pltpu.SemaphoreType.DMA((n_f,)), pltpu.SemaphoreType.DMA((n_s,)), pltpu.SemaphoreType.DMA((n_s,))]

    def _chip_copies(self, ins, outs, send, recv, landing):
        x, y, c = _place()
        chips = [(1 - x, y), (x, 1 - y), (1 - x, 1 - y)]
        mine = 2 * x + y
        cps = []
        for k in range(self.n_chip):
            for j, (cx, cy) in enumerate(chips):
                q = 2 * cx + cy
                if k < self.n_g:
                    part = pl.ds(c * self.half[k], self.half[k])
                    src = ins[k].at[part]
                    dst = outs[k].at[q if landing else mine, part]
                else:
                    src = ins[k].at[q]
                    dst = outs[k].at[j]
                cps.append(pltpu.make_async_remote_copy(
                    src_ref=src, dst_ref=dst, send_sem=send.at[3 * k + j], recv_sem=recv.at[3 * k + j],
                    device_id=(cx, cy, c), device_id_type=MESH))
        return cps

    def _core_copies(self, outs, fsend, frecv, landing):
        x, y, c = _place()
        chips = [(1 - x, y), (x, 1 - y), (1 - x, 1 - y)]
        cps = []
        for k in range(self.n_g):
            for j, (cx, cy) in enumerate(chips):
                part = pl.ds((1 - c if landing else c) * self.half[k], self.half[k])
                blk = outs[k].at[2 * cx + cy, part]
                cps.append(pltpu.make_async_remote_copy(
                    src_ref=blk, dst_ref=blk, send_sem=fsend.at[3 * k + j], recv_sem=frecv.at[3 * k + j],
                    device_id=(x, y, 1 - c), device_id_type=MESH))
        return cps

    def _local_copies(self, ins, outs, loc):
        x, y, _ = _place()
        return [pltpu.make_async_copy(ins[k], outs[k].at[2 * x + y], loc.at[k]) for k in range(self.n_g)]

    def _swap_copies(self, ins, outs, ssend, srecv):
        x, y, c = _place()
        return [pltpu.make_async_remote_copy(
            src_ref=ins[k], dst_ref=outs[k], send_sem=ssend.at[k - self.n_chip], recv_sem=srecv.at[k - self.n_chip],
            device_id=(x, y, 1 - c), device_id_type=MESH) for k in range(self.n_chip, self.n)]

    def start(self, ins, outs, send, recv, loc, fsend, frecv, ssend, srecv):
        for cp in (self._local_copies(ins, outs, loc) + self._chip_copies(ins, outs, send, recv, False)
                   + self._swap_copies(ins, outs, ssend, srecv)):
            cp.start()

    def wait(self, ins, outs, send, recv, loc, fsend, frecv, ssend, srecv):
        lands = self._chip_copies(ins, outs, send, recv, True)
        passes = self._core_copies(outs, fsend, frecv, False)
        for k in range(self.n_chip):
            for j in range(3):
                lands[3 * k + j].wait_recv()
                if k < self.n_g:
                    passes[3 * k + j].start()
        for cp in self._core_copies(outs, fsend, frecv, True):
            cp.wait_recv()
        swaps = self._swap_copies(ins, outs, ssend, srecv)
        for cp in swaps:
            cp.wait_recv()
        for cp in self._chip_copies(ins, outs, send, recv, False) + passes + swaps:
            cp.wait_send()
        for cp in self._local_copies(ins, outs, loc):
            cp.wait()


def _call(body, name, grid, in_specs, out_specs, out_shape, scratch, params, args, comm=None):
    n_in, n_out, n_sc = len(in_specs), len(out_specs), len(scratch)
    if comm is None:
        outs = pl.pallas_call(body, name=name, grid=grid, in_specs=in_specs, out_specs=out_specs, out_shape=out_shape,
                              scratch_shapes=scratch, compiler_params=params)(*args)
        return list(outs), []
    n = comm.n
    o0 = n_in + n
    s0 = o0 + n_out + n

    def wrapped(*refs):
        c_refs = (refs[n_in:o0], refs[o0 + n_out:s0]) + tuple(refs[s0 + n_sc:])
        ids = [pl.program_id(a) for a in range(len(grid))]
        first = ids[0] == 0
        last = ids[0] == grid[0] - 1
        for a in range(1, len(grid)):
            first = first & (ids[a] == 0)
            last = last & (ids[a] == grid[a] - 1)

        @pl.when(first)
        def _():
            comm.start(*c_refs)

        body(*refs[:n_in], *refs[o0:o0 + n_out], *refs[s0:s0 + n_sc])

        @pl.when(last)
        def _():
            comm.wait(*c_refs)

    outs = pl.pallas_call(
        wrapped, name=name, grid=grid, in_specs=list(in_specs) + [ANY] * n, out_specs=list(out_specs) + [ANY] * n,
        out_shape=list(out_shape) + comm.out_shape, scratch_shapes=list(scratch) + comm.scratch, compiler_params=params,
    )(*args, *comm.arrays)
    return list(outs[:n_out]), list(outs[n_out:])


def _comm_only(comm, name):
    def body(*refs):
        c_refs = (refs[:comm.n], refs[comm.n:2 * comm.n]) + tuple(refs[2 * comm.n:])
        comm.start(*c_refs)
        comm.wait(*c_refs)

    return pl.pallas_call(body, name=name, in_specs=[ANY] * comm.n, out_specs=[ANY] * comm.n, out_shape=comm.out_shape,
                          scratch_shapes=comm.scratch)(*comm.arrays)


def _in_fwd(h, g, shift, scale, w, dt_a, dt_b, name, pad_rows=0, comm=None, tm=512):
    S = h.shape[0]
    n_pad = pad_rows // tm

    def body(h_ref, g_ref, sh_ref, sc_ref, w_hbm, u_ref, oa_ref, ob_ref, w_v, sem):
        i = pl.program_id(0)

        @pl.when(i == 0)
        def _():
            cp = pltpu.make_async_copy(w_hbm, w_v, sem)
            cp.start()
            cp.wait()

        hh = h_ref[...]
        r = lax.rsqrt(jnp.mean(hh * hh, axis=-1, keepdims=True) + EPS)
        u = (hh * r * g_ref[...]) * (1.0 + sc_ref[...]) + sh_ref[...]
        ub = u.astype(BF16)
        u_ref[...] = ub
        for q in range(NCHIP):
            o_ref = oa_ref if q < 2 else ob_ref
            o_ref[:, (q % 2) * D:(q % 2 + 1) * D] = _nn(ub, w_v[q]).astype(o_ref.dtype)

        if n_pad:
            @pl.when(i < n_pad)
            def _():
                oa_ref[...] = jnp.zeros(oa_ref.shape, oa_ref.dtype)
                ob_ref[...] = jnp.zeros(ob_ref.shape, ob_ref.dtype)

    def src(i):
        return (jnp.maximum(i - n_pad, 0), 0)

    outs, landed = _call(
        body, name, (S // tm + n_pad,),
        [pl.BlockSpec((tm, D), src), _row(D), _row(D), _row(D), ANY],
        [pl.BlockSpec((tm, D), src), pl.BlockSpec((tm, E), lambda i: (i, 0)), pl.BlockSpec((tm, E), lambda i: (i, 0))],
        [jax.ShapeDtypeStruct((S, D), BF16), jax.ShapeDtypeStruct((S + pad_rows, E), dt_a),
         jax.ShapeDtypeStruct((S + pad_rows, E), dt_b)],
        [pltpu.VMEM((NCHIP, D, D), BF16), pltpu.SemaphoreType.DMA],
        _params(1, 52), (h, g, shift, scale, w), comm)
    return outs, landed


def _a_fwd(h, g, shift, scale, asc, gate, w_in, wg, w_out, name, comm=None, tm=512):
    S = h.shape[0]

    def body(h_ref, g_ref, sh_ref, sc_ref, as_ref, gate_ref, wi_hbm, wg_hbm, wo_hbm,
             u_ref, z_ref, p_ref, m_ref, y_ref, ho_ref, wi_v, wg_v, wo_v, buf, sems):
        i = pl.program_id(0)

        @pl.when(i == 0)
        def _():
            cps = [pltpu.make_async_copy(wi_hbm, wi_v, sems.at[0]), pltpu.make_async_copy(wg_hbm, wg_v, sems.at[1]),
                   pltpu.make_async_copy(wo_hbm, wo_v, sems.at[2])]
            for cp in cps:
                cp.start()
            buf[0:HALO, :] = jnp.zeros((HALO, E), F32)
            for cp in cps:
                cp.wait()

        hh = h_ref[...]
        r = lax.rsqrt(jnp.mean(hh * hh, axis=-1, keepdims=True) + EPS)
        ub = ((hh * r * g_ref[...]) * (1.0 + sc_ref[...]) + sh_ref[...]).astype(BF16)
        u_ref[...] = ub
        for q in range(2):
            buf[HALO:HALO + tm, q * D:(q + 1) * D] = _nn(ub, wi_v[q])
        t = i * tm + lax.broadcasted_iota(jnp.int32, (tm, 1), 0)
        y = None
        for gi, w in enumerate(POOL_W):
            cols = slice(gi * GW, (gi + 1) * GW)
            x = buf[:, cols]
            s = x
            k = 1
            while k < w:
                s = s + pltpu.roll(s, k, 0)
                k *= 2
            inv_cnt = 1.0 / jnp.minimum(t + 1, w).astype(F32)
            pb = (s[HALO:, :] * inv_cnt - x[HALO:, :]).astype(BF16)
            p_ref[:, cols] = pb
            mb = _nn(pb, wg_v[gi]).astype(BF16)
            m_ref[:, cols] = mb
            zb = _nn(ub, wi_v[2 + gi // 2, :, (gi % 2) * GW:(gi % 2 + 1) * GW]).astype(BF16)
            z_ref[:, cols] = zb
            zz = zb.astype(F32)
            act = ((mb.astype(F32) * as_ref[:, cols]) * (zz * jax.nn.sigmoid(zz))).astype(BF16)
            part = _nn(act, wo_v[gi])
            y = part if y is None else y + part
        buf[0:HALO, :] = buf[tm:tm + HALO, :]
        y_ref[...] = y.astype(BF16)
        ho_ref[...] = hh + gate_ref[...] * y

    rows_d = pl.BlockSpec((tm, D), lambda i: (i, 0))
    rows_e = pl.BlockSpec((tm, E), lambda i: (i, 0))
    return _call(
        body, name, (S // tm,),
        [rows_d, _row(D), _row(D), _row(D), _row(E), _row(D), ANY, ANY, ANY],
        [rows_d, rows_e, rows_e, rows_e, rows_d, rows_d],
        [jax.ShapeDtypeStruct((S, D), BF16), jax.ShapeDtypeStruct((S, E), BF16), jax.ShapeDtypeStruct((S, E), BF16),
         jax.ShapeDtypeStruct((S, E), BF16), jax.ShapeDtypeStruct((S, D), BF16), jax.ShapeDtypeStruct((S, D), F32)],
        [pltpu.VMEM((NCHIP, D, D), BF16), pltpu.VMEM((4, GW, GW), BF16), pltpu.VMEM((NCHIP, GW, D), BF16),
         pltpu.VMEM((tm + HALO, E), F32), pltpu.SemaphoreType.DMA((3,))],
        _params(1, 60), (h, g, shift, scale, asc, gate, w_in, wg, w_out), comm)


def _out_fwd(a, z, w, gate, h, name, head=None, comm=None, tm=512):
    S = h.shape[0]
    kb = E // NCHIP
    n_in = 5 if head is None else 7

    def body(*refs):
        a_ref, z_ref, w_hbm, gate_ref, h_ref = refs[:5]
        w_v, sem = refs[-2:]
        i = pl.program_id(0)

        @pl.when(i == 0)
        def _():
            cp = pltpu.make_async_copy(w_hbm, w_v, sem)
            cp.start()
            cp.wait()

        y = None
        for p in range(NCHIP):
            cols = slice(p * kb, (p + 1) * kb)
            zz = z_ref[:, cols].astype(F32)
            act = (a_ref[:, cols].astype(F32) * (zz * jax.nn.sigmoid(zz))).astype(BF16)
            part = _nn(act, w_v[p])
            y = part if y is None else y + part
        refs[n_in][...] = y.astype(BF16)
        hh = h_ref[...] + gate_ref[...] * y
        if head is None:
            refs[n_in + 1][...] = hh
            return
        g_ref, t_ref = refs[5:7]
        dh_ref, st_ref = refs[n_in + 1:n_in + 3]

        @pl.when(i == 0)
        def _():
            st_ref[...] = jnp.zeros((SUBLANES, D), F32)

        r = lax.rsqrt(jnp.mean(hh * hh, axis=-1, keepdims=True) + EPS)
        xhat = hh * r
        diff = xhat * g_ref[...] - t_ref[...]
        st_ref[1:2, :] += _colsum(diff * diff)
        dout = diff * (1.0 / D)
        st_ref[0:1, :] += _colsum(dout * xhat)
        dx = dout * g_ref[...]
        dh_ref[...] = r * (dx - xhat * jnp.mean(dx * xhat, axis=-1, keepdims=True))

    rows_d = pl.BlockSpec((tm, D), lambda i: (i, 0))
    rows_e = pl.BlockSpec((tm, E), lambda i: (i, 0))
    in_specs = [rows_e, rows_e, ANY, _row(D), rows_d]
    out_specs = [rows_d, rows_d]
    out_shape = [jax.ShapeDtypeStruct((S, D), BF16), jax.ShapeDtypeStruct((S, D), F32)]
    args = (a, z, w, gate, h)
    if head is not None:
        in_specs += [_row(D), rows_d]
        out_specs += [pl.BlockSpec((SUBLANES, D), lambda i: (0, 0))]
        out_shape += [jax.ShapeDtypeStruct((SUBLANES, D), F32)]
        args += tuple(head)
    return _call(body, name, (S // tm,), in_specs, out_specs, out_shape,
                 [pltpu.VMEM((NCHIP, kb, D), BF16), pltpu.SemaphoreType.DMA], _params(1, 52), args, comm)


TW = BW + LANES


def _diag_onehot(transpose):
    shape = (TW, NRELP) if transpose else (NRELP, TW)
    j = lax.broadcasted_iota(jnp.int32, shape, 0 if transpose else 1)
    r = lax.broadcasted_iota(jnp.int32, shape, 1 if transpose else 0)
    idx = jnp.clip(PAD - (j - LANES), -REL_CLIP, REL_CLIP) + REL_CLIP
    return jnp.where(idx == r, 1.0, 0.0).astype(BF16)


def _strip_valid():
    m = lax.broadcasted_iota(jnp.int32, (NH, BW), 1)
    return m < (LEFT + 1) * CHUNK, m >= CHUNK


def _bias_build(rb, name):
    def body(rb_ref, a_ref, b_ref):
        x = rb_ref[...]
        hi = x.astype(BF16)
        r1 = x - hi.astype(F32)
        mid = r1.astype(BF16)
        lo = (r1 - mid.astype(F32)).astype(BF16)
        oh = _diag_onehot(False)
        diag = (_nn(hi, oh) + _nn(mid, oh)) + _nn(lo, oh)
        valid_a, valid_b = _strip_valid()
        for qi in range(CHUNK):
            a_ref[qi] = jnp.where(valid_a, pltpu.roll(diag, TW - (LANES - qi), 1)[:, :BW], NEG)
            b_ref[qi] = jnp.where(valid_b, pltpu.roll(diag, TW - (CHUNK - qi), 1)[:, :BW], NEG)

    vmem = pl.BlockSpec(memory_space=pltpu.VMEM)
    return pl.pallas_call(
        body, name=name, in_specs=[vmem], out_specs=[vmem, vmem],
        out_shape=[jax.ShapeDtypeStruct((CHUNK, NH, BW), F32), jax.ShapeDtypeStruct((CHUNK, NH, BW), F32)],
        compiler_params=pltpu.CompilerParams(vmem_limit_bytes=32 * 2 ** 20),
    )(rb)


def _dbias_reduce(dba, dbb, name):
    def body(a_ref, b_ref, o_ref):
        valid_a, valid_b = _strip_valid()
        zeros = jnp.zeros((NH, TW - BW), F32)
        acc = jnp.zeros((NH, TW), F32)
        for qi in range(CHUNK):
            xa = jnp.concatenate([jnp.where(valid_a, a_ref[qi], 0.0), zeros], axis=1)
            xb = jnp.concatenate([jnp.where(valid_b, b_ref[qi], 0.0), zeros], axis=1)
            acc = acc + (pltpu.roll(xa, LANES - qi, 1) + pltpu.roll(xb, CHUNK - qi, 1))
        oh = _diag_onehot(True)
        hi = acc.astype(BF16)
        mid = (acc - hi.astype(F32)).astype(BF16)
        r = lax.broadcasted_iota(jnp.int32, (NH, NRELP), 1)
        near = jnp.where(r < 2 * REL_CLIP, _nn(hi, oh) + _nn(mid, oh), 0.0)
        o_ref[...] = jnp.where(r == 2 * REL_CLIP, -jnp.sum(near, axis=-1, keepdims=True), near)

    vmem = pl.BlockSpec(memory_space=pltpu.VMEM)
    return pl.pallas_call(
        body, name=name, in_specs=[vmem, vmem], out_specs=vmem,
        out_shape=jax.ShapeDtypeStruct((NH, NRELP), F32),
        compiler_params=pltpu.CompilerParams(vmem_limit_bytes=32 * 2 ** 20),
    )(dba, dbb)


def _build_bias(bias3, ba_ref, bb_ref):
    bias3[NMASK] = jnp.full((QB, WIN), NEG, F32)
    for qc in range(QC):
        rows = slice(qc * CHUNK, (qc + 1) * CHUNK)
        if qc % 2 == 0:
            bias3[NMASK, rows, qc * CHUNK:qc * CHUNK + BW] = ba_ref[...] * LOG2E
        else:
            bias3[NMASK, rows, (qc - 1) * CHUNK:(qc - 1) * CHUNK + BW] = bb_ref[...] * LOG2E
    col = lax.broadcasted_iota(jnp.int32, (QB, WIN), 1)
    for sub in range(NMASK):
        bias3[sub] = jnp.where(col < PAD - sub * QB, NEG, bias3[NMASK])


def _nsub(S, most):
    n = min(most, S // QB)
    assert S % (n * QB) == 0 and n >= NMASK
    return n


def _row0(i, sub, nsub):
    return pl.multiple_of((i * nsub + sub) * QB, QB)


def _scores(q_ref, k_ref, i, sub, nsub):
    return _nt(q_ref[sub * QB:(sub + 1) * QB, :], k_ref[pl.ds(_row0(i, sub, nsub), WIN), :])


HALF = QB // 2
LIVE = WIN - LANES


def _live(half):
    return slice(half * HALF, (half + 1) * HALF), slice(half * LANES, half * LANES + LIVE)


def _widen(x, half):
    zeros = jnp.zeros((HALF, LANES), x.dtype)
    return jnp.concatenate([x, zeros] if half == 0 else [zeros, x], axis=1)


def _probs(s, bias3, i, sub):
    which = jnp.where(i == 0, sub, NMASK) if sub < NMASK else NMASK
    out = []
    for half in range(2):
        rows, cols = _live(half)
        t = s[rows, cols] * (SM_SCALE * LOG2E) + bias3[which, rows, cols]
        e = jnp.exp2(t - jnp.max(t, axis=-1, keepdims=True))
        out.append(_widen((e * (1.0 / jnp.sum(e, axis=-1, keepdims=True))).astype(BF16), half))
    return jnp.concatenate(out, axis=0)


def _attn_fwd(q, kp, vp, ba, bb, name, comm=None):
    S = q.shape[0]
    nsub = _nsub(S, NSUB_FWD)
    R = nsub * QB

    def body(q_ref, k_ref, v_ref, ba_ref, bb_ref, o_ref, p_ref, bias3):
        i = pl.program_id(1)

        @pl.when(i == 0)
        def _():
            _build_bias(bias3, ba_ref, bb_ref)

        s_next = _scores(q_ref, k_ref, i, 0, nsub)
        for sub in range(nsub):
            s = s_next
            if sub + 1 < nsub:
                s_next = _scores(q_ref, k_ref, i, sub + 1, nsub)
            pb = _probs(s, bias3, i, sub)
            p_ref[sub] = pb
            o_ref[sub * QB:(sub + 1) * QB, :] = _nn(pb, v_ref[pl.ds(_row0(i, sub, nsub), WIN), :]).astype(BF16)

    return _call(
        body, name, (NH, S // R),
        [pl.BlockSpec((R, HD), lambda h, i: (i, h)), pl.BlockSpec((S + PAD, HD), lambda h, i: (0, h)),
         pl.BlockSpec((S + PAD, HD), lambda h, i: (0, h)), pl.BlockSpec((None, CHUNK, BW), lambda h, i: (h, 0, 0)),
         pl.BlockSpec((None, CHUNK, BW), lambda h, i: (h, 0, 0))],
        [pl.BlockSpec((R, HD), lambda h, i: (i, h)), pl.BlockSpec((None, nsub, QB, WIN), lambda h, i: (h, i, 0, 0))],
        [jax.ShapeDtypeStruct((S, E), BF16), jax.ShapeDtypeStruct((NH, S // QB, QB, WIN), BF16)],
        [pltpu.VMEM((NMASK + 1, QB, WIN), F32)],
        _params(2, 48), (q, kp, vp, ba, bb), comm)


def _store_grad(acc, stage, dw_hbm, sem):
    for q in range(NCHIP):
        stage[...] = acc[q].astype(BF16)
        cp = pltpu.make_async_copy(stage, dw_hbm.at[q], sem)
        cp.start()
        cp.wait()


def _out_bwd(dh, y, gate, a, cs, z, w, name, comm=None, tm=256):
    S = dh.shape[0]
    kb = E // NCHIP
    cb = 256
    n_t = S // tm

    def body(dh_ref, y_ref, gate_ref, a_ref, cs_ref, z_ref, w_hbm, da_ref, dz_ref, dw_hbm, st_ref, w_v, acc, stage, sem):
        i = pl.program_id(0)

        @pl.when(i == 0)
        def _():
            cp = pltpu.make_async_copy(w_hbm, w_v, sem)
            cp.start()
            acc[...] = jnp.zeros(acc.shape, F32)
            st_ref[...] = jnp.zeros((SUBLANES, D), F32)
            cp.wait()

        dhh = dh_ref[...]
        st_ref[0:1, :] += _colsum(dhh * y_ref[...].astype(F32))
        dy = (dhh * gate_ref[...]).astype(BF16)
        for blk in range(E // cb):
            p, r0 = divmod(blk * cb, kb)
            cols = slice(blk * cb, (blk + 1) * cb)
            zz = z_ref[:, cols].astype(F32)
            sig = jax.nn.sigmoid(zz)
            sz = zz * sig
            ae = a_ref[:, cols].astype(F32) * cs_ref[:, cols]
            acc[p, r0:r0 + cb, :] += _tn((ae * sz).astype(BF16), dy)
            dact = _nt(dy, w_v[p, r0:r0 + cb, :])
            da_ref[:, cols] = (dact * sz).astype(BF16)
            dz_ref[:, cols] = (dact * ae * (sig * (1.0 + zz * (1.0 - sig)))).astype(BF16)

        @pl.when(i == n_t - 1)
        def _():
            _store_grad(acc, stage, dw_hbm, sem)

    return _call(
        body, name, (n_t,),
        [pl.BlockSpec((tm, D), lambda i: (i, 0)), pl.BlockSpec((tm, D), lambda i: (i, 0)), _row(D),
         pl.BlockSpec((tm, E), lambda i: (i, 0)), _row(E), pl.BlockSpec((tm, E), lambda i: (i, 0)), ANY],
        [pl.BlockSpec((tm, E), lambda i: (i, 0)), pl.BlockSpec((tm, E), lambda i: (i, 0)), ANY,
         pl.BlockSpec((SUBLANES, D), lambda i: (0, 0))],
        [jax.ShapeDtypeStruct((S, E), BF16), jax.ShapeDtypeStruct((S, E), BF16),
         jax.ShapeDtypeStruct((NCHIP, kb, D), BF16), jax.ShapeDtypeStruct((SUBLANES, D), F32)],
        [pltpu.VMEM((NCHIP, kb, D), BF16), pltpu.VMEM((NCHIP, kb, D), F32), pltpu.VMEM((kb, D), BF16),
         pltpu.SemaphoreType.DMA],
        _params(1, 52), (dh, y, gate, a, cs, z, w), comm)


def _attn_bwd(q, kp, vp, probs, do, prev, name, comm=None):
    S = q.shape[0]
    nsub = _nsub(S, NSUB_BWD)
    R = nsub * QB
    n_i = S // R
    dt_kv = F32 if prev is None else BF16

    def body(*refs):
        q_ref, k_ref, v_ref, p_ref, do_ref = refs[:5]
        refs = refs[5:]
        if prev is not None:
            pk_hbm, pv_hbm = refs[:2]
            refs = refs[2:]
        dq_ref, dk_ref, dv_ref, dba_ref, dbb_ref, dbias, dk_acc, dv_acc = refs[:8]
        if prev is not None:
            pk_v, pv_v, sems = refs[8:]
        h = pl.program_id(0)
        i = pl.program_id(1)

        def prev_copies():
            cols = pl.ds(pl.multiple_of(h * HD, HD), HD)
            return (pltpu.make_async_copy(pk_hbm.at[:, cols], pk_v, sems.at[0]),
                    pltpu.make_async_copy(pv_hbm.at[:, cols], pv_v, sems.at[1]))

        @pl.when(i == 0)
        def _():
            if prev is not None:
                for cp in prev_copies():
                    cp.start()
            dbias[...] = jnp.zeros((2, CHUNK, DBW), F32)
            dk_acc[...] = jnp.zeros((S + PAD, HD), F32)
            dv_acc[...] = jnp.zeros((S + PAD, HD), F32)

        def mxu_in(sub):
            return _nt(do_ref[sub * QB:(sub + 1) * QB, :], v_ref[pl.ds(_row0(i, sub, nsub), WIN), :])

        nxt = mxu_in(0)
        for sub in range(nsub):
            rows = slice(sub * QB, (sub + 1) * QB)
            win = pl.ds(_row0(i, sub, nsub), WIN)
            dp = nxt
            if sub + 1 < nsub:
                nxt = mxu_in(sub + 1)
            parts = []
            for half in range(2):
                hrows, hcols = _live(half)
                p = p_ref[sub, hrows, hcols].astype(F32)
                dph = dp[hrows, hcols]
                ds = p * (dph - jnp.sum(p * dph, axis=-1, keepdims=True))
                dbias[0] += ds[0:CHUNK, LIVE - DBW:LIVE]
                dbias[1] += ds[CHUNK:HALF, LIVE - DBW:LIVE]
                parts.append(_widen((ds * SM_SCALE).astype(BF16), half))
            dsb = jnp.concatenate(parts, axis=0)
            dq_ref[rows, :] = _nn(dsb, k_ref[win, :]).astype(BF16)
            dk_acc[win, :] += _tn(dsb, q_ref[rows, :])
            dv_acc[win, :] += _tn(p_ref[sub], do_ref[rows, :])

        @pl.when(i == n_i - 1)
        def _():
            zeros = jnp.zeros((CHUNK, BW - DBW), F32)
            dba_ref[...] = jnp.concatenate([zeros, dbias[0]], axis=1)
            dbb_ref[...] = jnp.concatenate([zeros, dbias[1]], axis=1)
            if prev is None:
                dk_ref[...] = dk_acc[...]
                dv_ref[...] = dv_acc[...]
            else:
                for cp in prev_copies():
                    cp.wait()
                dk_ref[...] = (dk_acc[...] + pk_v[...]).astype(BF16)
                dv_ref[...] = (dv_acc[...] + pv_v[...]).astype(BF16)

    head = pl.BlockSpec((S + PAD, HD), lambda h, i: (0, h))
    strip = pl.BlockSpec((None, CHUNK, BW), lambda h, i: (h, 0, 0))
    blk = pl.BlockSpec((R, HD), lambda h, i: (i, h))
    in_specs = [blk, head, head, pl.BlockSpec((None, nsub, QB, WIN), lambda h, i: (h, i, 0, 0)), blk]
    scratch = [pltpu.VMEM((2, CHUNK, DBW), F32), pltpu.VMEM((S + PAD, HD), F32), pltpu.VMEM((S + PAD, HD), F32)]
    args = (q, kp, vp, probs, do)
    if prev is not None:
        in_specs += [ANY, ANY]
        scratch += [pltpu.VMEM((S + PAD, HD), F32), pltpu.VMEM((S + PAD, HD), F32), pltpu.SemaphoreType.DMA((2,))]
        args += tuple(prev)
    return _call(
        body, name, (NH, n_i), in_specs, [blk, head, head, strip, strip],
        [jax.ShapeDtypeStruct((S, E), BF16), jax.ShapeDtypeStruct((S + PAD, E), dt_kv),
         jax.ShapeDtypeStruct((S + PAD, E), dt_kv), jax.ShapeDtypeStruct((NH, CHUNK, BW), F32),
         jax.ShapeDtypeStruct((NH, CHUNK, BW), F32)],
        scratch, _params(2, 56), args, comm)


def _pool_bwd(dms, mixed, pooled, wg, a_scale, name, comm=None, tm=512):
    S = dms.shape[0]
    n_t = S // tm

    def rev(i):
        return (n_t - 1 - i, 0)

    def body(d_ref, m_ref, p_ref, wg_ref, as_ref, dv_ref, dwg_ref, st_ref, buf):
        i = pl.program_id(0)

        @pl.when(i == 0)
        def _():
            buf[tm:tm + HALO, :] = jnp.zeros((HALO, E), F32)
            dwg_ref[...] = jnp.zeros((4, GW, GW), F32)
            st_ref[...] = jnp.zeros((SUBLANES, E), F32)

        t = (n_t - 1 - i) * tm + lax.broadcasted_iota(jnp.int32, (tm, 1), 0)
        st_ref[0:1, :] += _colsum(d_ref[...].astype(F32) * m_ref[...].astype(F32))
        for gi, w in enumerate(POOL_W):
            cols = slice(gi * GW, (gi + 1) * GW)
            dm = (d_ref[:, cols].astype(F32) * as_ref[:, cols]).astype(BF16)
            dpool = _nt(dm, wg_ref[gi])
            dwg_ref[gi] += _tn(p_ref[:, cols], dm)
            inv_cnt = 1.0 / jnp.minimum(t + 1, w).astype(F32)
            buf[0:tm, cols] = dpool * inv_cnt
            s = buf[:, cols]
            k = 1
            while k < w:
                s = s + pltpu.roll(s, tm + HALO - k, 0)
                k *= 2
            dv_ref[:, cols] = (s[0:tm, :] - dpool).astype(BF16)
        buf[tm:tm + HALO, :] = buf[0:HALO, :]

    return _call(
        body, name, (n_t,),
        [pl.BlockSpec((tm, E), rev), pl.BlockSpec((tm, E), rev), pl.BlockSpec((tm, E), rev),
         pl.BlockSpec((4, GW, GW), lambda i: (0, 0, 0)), _row(E)],
        [pl.BlockSpec((tm, E), rev), pl.BlockSpec((4, GW, GW), lambda i: (0, 0, 0)),
         pl.BlockSpec((SUBLANES, E), lambda i: (0, 0))],
        [jax.ShapeDtypeStruct((S, E), BF16), jax.ShapeDtypeStruct((4, GW, GW), F32),
         jax.ShapeDtypeStruct((SUBLANES, E), F32)],
        [pltpu.VMEM((tm + HALO, E), F32)],
        _params(1, 52), (dms, mixed, pooled, wg, a_scale), comm)


def _in_bwd(da, db, row_off, u, h, g, scale, w, dh_out, name, comm=None, tm=256):
    S = h.shape[0]
    n_t = S // tm
    off = row_off // tm

    def body(da_ref, db_ref, u_ref, h_ref, g_ref, sc_ref, w_hbm, dho_ref, dhi_ref, dw_hbm, st_ref, w_v, acc, stage, sem):
        i = pl.program_id(0)

        @pl.when(i == 0)
        def _():
            cp = pltpu.make_async_copy(w_hbm, w_v, sem)
            cp.start()
            acc[...] = jnp.zeros(acc.shape, F32)
            st_ref[...] = jnp.zeros((SUBLANES, D), F32)
            cp.wait()

        ub = u_ref[...]
        du = None
        for q in range(NCHIP):
            d_ref = da_ref if q < 2 else db_ref
            dv = d_ref[:, (q % 2) * D:(q % 2 + 1) * D]
            acc[q] += _tn(ub, dv)
            part = _nt(dv, w_v[q])
            du = part if du is None else du + part

        hh = h_ref[...]
        r = lax.rsqrt(jnp.mean(hh * hh, axis=-1, keepdims=True) + EPS)
        xhat = hh * r
        gg = g_ref[...]
        st_ref[0:1, :] += _colsum(du)
        st_ref[1:2, :] += _colsum(du * (xhat * gg))
        dn = du * (1.0 + sc_ref[...])
        st_ref[2:3, :] += _colsum(dn * xhat)
        dx = dn * gg
        dhi_ref[...] = dho_ref[...] + r * (dx - xhat * jnp.mean(dx * xhat, axis=-1, keepdims=True))

        @pl.when(i == n_t - 1)
        def _():
            _store_grad(acc, stage, dw_hbm, sem)

    part_spec = pl.BlockSpec((tm, E), lambda i: (i + off, 0))
    return _call(
        body, name, (n_t,),
        [part_spec, part_spec, pl.BlockSpec((tm, D), lambda i: (i, 0)), pl.BlockSpec((tm, D), lambda i: (i, 0)),
         _row(D), _row(D), ANY, pl.BlockSpec((tm, D), lambda i: (i, 0))],
        [pl.BlockSpec((tm, D), lambda i: (i, 0)), ANY, pl.BlockSpec((SUBLANES, D), lambda i: (0, 0))],
        [jax.ShapeDtypeStruct((S, D), F32), jax.ShapeDtypeStruct((NCHIP, D, D), BF16),
         jax.ShapeDtypeStruct((SUBLANES, D), F32)],
        [pltpu.VMEM((NCHIP, D, D), BF16), pltpu.VMEM((NCHIP, D, D), F32), pltpu.VMEM((D, D), BF16),
         pltpu.SemaphoreType.DMA],
        _params(1, 56), (da, db, u, h, g, scale, w, dh_out), comm)


def _grad_ada(c_act_t, dmod, name):
    L, _, n = dmod.shape

    def body(c_ref, d_ref, o_ref):
        acc = None
        for b in range(SUBLANES):
            part = c_ref[:, b:b + 1] * d_ref[b:b + 1, :]
            acc = part if acc is None else acc + part
        o_ref[...] = acc

    return pl.pallas_call(
        body, name=name, grid=(L,),
        in_specs=[pl.BlockSpec((D, SUBLANES), lambda l: (0, 0)), pl.BlockSpec((None, SUBLANES, n), lambda l: (l, 0, 0))],
        out_specs=pl.BlockSpec((None, D, n), lambda l: (l, 0, 0)),
        out_shape=jax.ShapeDtypeStruct((L, D, n), F32),
        compiler_params=_params(1, 32),
    )(c_act_t, dmod)


def _stats_reduce(g3, loss_row, name):
    n_dev, rows, _ = g3.shape

    def body(g_ref, o_ref, l_ref):
        acc = g_ref[0]
        for d in range(1, n_dev):
            acc = acc + g_ref[d]
        o_ref[...] = acc
        tot = jnp.sum(o_ref[loss_row:loss_row + 1, :], axis=-1, keepdims=True)
        l_ref[...] = jnp.broadcast_to(tot * (0.5 / D), (SUBLANES, LANES))

    return pl.pallas_call(
        body, name=name,
        in_specs=[pl.BlockSpec(memory_space=pltpu.VMEM)],
        out_specs=[pl.BlockSpec(memory_space=pltpu.VMEM), pl.BlockSpec(memory_space=pltpu.VMEM)],
        out_shape=[jax.ShapeDtypeStruct((rows, D), F32), jax.ShapeDtypeStruct((SUBLANES, LANES), F32)],
        compiler_params=pltpu.CompilerParams(vmem_limit_bytes=32 * 2 ** 20),
    )(g3)


def _sum4(own, land, chip, name, tr=256):
    _, R, C = own.shape
    tr = min(tr, R)

    def body(p_ref, own_ref, land_ref, o_ref):
        o_ref[...] = ((own_ref[...].astype(F32) + land_ref[0].astype(F32)) + land_ref[1].astype(F32)) + land_ref[2].astype(F32)

    out = pl.pallas_call(
        body, name=name,
        grid_spec=pltpu.PrefetchScalarGridSpec(
            num_scalar_prefetch=1, grid=(R // tr,),
            in_specs=[pl.BlockSpec((None, tr, C), lambda i, p: (p[0], i, 0)), pl.BlockSpec((3, tr, C), lambda i, p: (0, i, 0))],
            out_specs=pl.BlockSpec((tr, C), lambda i, p: (i, 0))),
        out_shape=jax.ShapeDtypeStruct((R, C), F32),
        compiler_params=_params(1, 32),
    )(chip, pltpu.with_memory_space_constraint(own, pltpu.HBM), pltpu.with_memory_space_constraint(land, pltpu.HBM))
    return pltpu.with_memory_space_constraint(out, pltpu.HBM)


def _adamw(w, m, v, g, name, tr=256):
    L, R, C = w.shape
    tr = min(tr, R)
    stacked = not isinstance(g, (list, tuple))
    n_g = None if stacked else [len(ps) for ps in g]
    flat = [g] if stacked else [a for ps in g for a in ps]

    def body(*refs):
        w_ref, m_ref, v_ref = refs[:3]
        g_refs = refs[3:3 + len(flat)]
        go_ref, d_ref, mo_ref, vo_ref = refs[3 + len(flat):]
        if stacked:
            gg = g_refs[0][...]
        else:
            layer = pl.program_id(0)
            gg = None
            k = 0
            for li in range(L):
                gl = None
                for _ in range(n_g[li]):
                    x = g_refs[k][...]
                    gl = x if gl is None else gl + x
                    k += 1
                gg = gl if gg is None else jnp.where(layer == li, gl, gg)
        m2 = ADAM_B1 * m_ref[...] + (1.0 - ADAM_B1) * gg
        v2 = ADAM_B2 * v_ref[...] + (1.0 - ADAM_B2) * (gg * gg)
        m_hat = m2 / (1.0 - ADAM_B1 ** ADAM_STEP)
        v_hat = v2 / (1.0 - ADAM_B2 ** ADAM_STEP)
        go_ref[...] = gg
        d_ref[...] = -ADAM_LR * (m_hat / (jnp.sqrt(v_hat) + ADAM_EPS) + ADAM_WD * w_ref[...])
        mo_ref[...] = m2
        vo_ref[...] = v2

    big = pl.BlockSpec((None, tr, C), lambda l, i: (l, i, 0))
    g_specs = [big] if stacked else [pl.BlockSpec((tr, C), lambda l, i: (i, 0))] * len(flat)
    return pl.pallas_call(
        body, name=name, grid=(L, R // tr),
        in_specs=[big, big, big] + g_specs,
        out_specs=[big, big, big, big],
        out_shape=[jax.ShapeDtypeStruct((L, R, C), F32)] * 4,
        compiler_params=_params(2, 48),
    )(w, m, v, *flat)


GATHER8_SEMS = [pltpu.SemaphoreType.DMA((7,)), pltpu.SemaphoreType.DMA((7,)), pltpu.SemaphoreType.DMA]


def _gather8(x_ref, out_ref, send_sems, recv_sems, local_sem):
    m = x_ref.shape[0]
    x, y, c = _place()
    me, sibling = (x, y, c), (x, y, 1 - c)
    chips = [(1 - x, y), (x, 1 - y), (1 - x, 1 - y)]

    def rows(px, py, pc):
        return out_ref.at[pl.ds((4 * px + 2 * py + pc) * m, m), :]

    def copy(k, block, to, src=None):
        return pltpu.make_async_remote_copy(
            src_ref=rows(*block) if src is None else src, dst_ref=rows(*block),
            send_sem=send_sems.at[k], recv_sem=recv_sems.at[k], device_id=to, device_id_type=MESH)

    mine = pltpu.make_async_copy(x_ref, rows(*me), local_sem)
    mine.start()
    first = [copy(0, me, sibling, src=x_ref)]
    first += [copy(1 + j, me, (*chip, c), src=x_ref) for j, chip in enumerate(chips)]
    for cp in first:
        cp.start()
    passed = [copy(4 + j, (*chip, c), sibling) for j, chip in enumerate(chips)]
    for j, chip in enumerate(chips):
        copy(1 + j, (*chip, c), me).wait_recv()
        passed[j].start()
    copy(0, sibling, me).wait_recv()
    for j, chip in enumerate(chips):
        copy(4 + j, (*chip, 1 - c), me).wait_recv()
    for cp in first + passed:
        cp.wait_send()
    mine.wait()


def _allgather8(xs, name, comm=None):
    m, n = xs.shape
    n_c = 0 if comm is None else comm.n

    def body(*refs):
        x_ref, out_ref = refs[0], refs[1 + n_c]
        c_refs = (refs[1:1 + n_c], refs[2 + n_c:2 + 2 * n_c]) + tuple(refs[5 + 2 * n_c:])
        if comm is not None:
            comm.start(*c_refs)
        _gather8(x_ref, out_ref, *refs[2 + 2 * n_c:5 + 2 * n_c])
        if comm is not None:
            comm.wait(*c_refs)

    vmem = pl.BlockSpec(memory_space=pltpu.VMEM)
    outs = pl.pallas_call(
        body, name=name,
        out_shape=[jax.ShapeDtypeStruct((8 * m, n), xs.dtype)] + ([] if comm is None else comm.out_shape),
        in_specs=[vmem] + [ANY] * n_c,
        out_specs=[vmem] + [ANY] * n_c,
        scratch_shapes=GATHER8_SEMS + ([] if comm is None else comm.scratch),
        compiler_params=pltpu.CompilerParams(vmem_limit_bytes=32 * 2 ** 20),
    )(xs, *([] if comm is None else comm.arrays))
    return outs[0], list(outs[1:])


def _prologue(c8, ada_w, ada_b, kv_ada_w, kv_ada_b, extra, comm, name):
    L, _, n = ada_w.shape
    k = kv_ada_w.shape[1]
    e = extra.shape[1]
    width = L * n + k + e
    n_c = comm.n

    def body(*refs):
        c_ref, w_hbm, b_ref, kw_hbm, kb_ref, x_ref = refs[:6]
        c_in = refs[6:6 + n_c]
        ca_ref, out_ref = refs[6 + n_c:8 + n_c]
        c_out = refs[8 + n_c:8 + 2 * n_c]
        cbuf, wbuf, kbuf, part, wsems = refs[8 + 2 * n_c:13 + 2 * n_c]
        sems_a = refs[13 + 2 * n_c:16 + 2 * n_c]
        sems_b = refs[16 + 2 * n_c:19 + 2 * n_c]
        c_refs = (c_in, c_out) + tuple(refs[19 + 2 * n_c:])
        comm.start(*c_refs)

        def fetch(l):
            return pltpu.make_async_copy(w_hbm.at[l], wbuf.at[l % 2], wsems.at[l % 2])

        fetch(0).start()
        kv_copy = pltpu.make_async_copy(kw_hbm, kbuf, wsems.at[2])
        kv_copy.start()
        _gather8(c_ref, cbuf, *sems_a)
        cc = jnp.concatenate([cbuf[SUBLANES * d:SUBLANES * d + 1, :] for d in range(8)], axis=0)
        ca = cc * jax.nn.sigmoid(cc)
        ca_ref[...] = ca
        cab = ca.astype(BF16)
        for l in range(L):
            fetch(l).wait()
            if l + 1 < L:
                fetch(l + 1).start()
            part[:, l * n:(l + 1) * n] = _nn(cab, wbuf[l % 2].astype(BF16)) + b_ref[l]
        kv_copy.wait()
        part[:, L * n:L * n + k] = _nn(cab, kbuf[...].astype(BF16)) + kb_ref[...]
        part[:, L * n + k:] = jnp.broadcast_to(x_ref[...], (SUBLANES, e))
        _gather8(part, out_ref, *sems_b)
        comm.wait(*c_refs)

    vmem = pl.BlockSpec(memory_space=pltpu.VMEM)
    outs = pl.pallas_call(
        body, name=name,
        out_shape=[jax.ShapeDtypeStruct((SUBLANES, D), F32), jax.ShapeDtypeStruct((8 * SUBLANES, width), F32)] + comm.out_shape,
        in_specs=[vmem, ANY, vmem, ANY, vmem, vmem] + [ANY] * n_c,
        out_specs=[vmem, vmem] + [ANY] * n_c,
        scratch_shapes=[pltpu.VMEM((8 * SUBLANES, D), F32), pltpu.VMEM((2, D, n), F32), pltpu.VMEM((D, k), F32),
                        pltpu.VMEM((SUBLANES, width), F32), pltpu.SemaphoreType.DMA((3,))] + GATHER8_SEMS + GATHER8_SEMS
        + comm.scratch,
        compiler_params=pltpu.CompilerParams(vmem_limit_bytes=32 * 2 ** 20),
    )(c8, ada_w, ada_b, kv_ada_w, kv_ada_b, extra, *comm.arrays)
    return outs[0], outs[1], list(outs[2:])


def _pad8(a):
    return jnp.pad(a, ((0, SUBLANES - a.shape[0]), (0, 0)))


def _group_rows(wg):
    return wg.transpose(1, 0, 2, 3).reshape(4, GW, GW)


def _example_step(h0, tgt, mods, kvmod, a_scale, norm_g, kv_norm_g, final_g, b_rel_bias, sh, w_first, chip_arr):
    ones_e = jnp.ones((1, E), F32)
    shift = [mods[l:l + 1, 0:D] for l in range(4)]
    scale = [mods[l:l + 1, D:2 * D] for l in range(4)]
    gate = [mods[l:l + 1, 2 * D:3 * D] for l in range(4)]
    gl = [norm_g[l:l + 1] for l in range(4)]
    kv_shift, kv_scale = kvmod[None, 0:D], kvmod[None, D:2 * D]
    kv_g = kv_norm_g[None]

    w_a = w_first
    hs = [h0]
    saved = []
    nxt = [[sh["a_in"][1], sh["a_grp"][1], sh["a_out"][1]], [sh["kv"][0], sh["b_in"][0]]]
    for l in range(2):
        w_in_l, wg_l, wo_l = w_a
        wg_full = _group_rows(wg_l)
        (u, z, pooled, mixed, y, hn), got = _a_fwd(hs[-1], gl[l], shift[l], scale[l], a_scale[l:l + 1], gate[l], w_in_l,
                                                   wg_full, wo_l, f"a{l}_fwd", comm=_Comm(gathers=nxt[l]))
        saved.append((u, z, pooled, mixed, y, w_in_l, wg_full, wo_l))
        hs.append(hn)
        if l == 0:
            w_a = got
        else:
            w_kv, wb_in0 = got

    (uk, kp, vp), _ = _in_fwd(hs[2], kv_g, kv_shift, kv_scale, w_kv, BF16, BF16, "kv_in_fwd", pad_rows=PAD)
    wb_in = [wb_in0, None]
    wb_out = [None, None]

    for bi in range(2):
        l = 2 + bi
        sa, sb = _bias_build(jnp.pad(b_rel_bias[bi], ((0, 0), (0, NRELP - NREL))), f"b{bi}_bias")
        (u, q, z), _ = _in_fwd(hs[-1], gl[l], shift[l], scale[l], wb_in[bi], BF16, BF16, f"b{bi}_in_fwd")
        comm = _Comm(gathers=[sh["b_out"][0], sh["b_in"][1], sh["b_out"][1]]) if bi == 0 else None
        (att, probs), got = _attn_fwd(q, kp, vp, sa.transpose(1, 0, 2), sb.transpose(1, 0, 2), f"b{bi}_attn_fwd", comm=comm)
        if bi == 0:
            wb_out[0], wb_in[1], wb_out[1] = got
        if bi == 0:
            (y, hn), _ = _out_fwd(att, z, wb_out[bi], gate[l], hs[-1], f"b{bi}_out_fwd")
            hs.append(hn)
        else:
            (y, dh, st_fin), _ = _out_fwd(att, z, wb_out[bi], gate[l], hs[-1], f"b{bi}_out_fwd", head=(final_g[None], tgt))
        saved.append((u, z, q, att, y, probs))

    st_in = [None] * 4
    st_out = [None] * 4
    grads = {}
    landed = {}

    def carry(names):
        return _Comm(scatters=[grads[n] for n in names]) if names else None

    def land(names, got):
        for n, a in zip(names, got):
            landed[n] = a

    u, z, q, att, y, probs = saved[3]
    (datt, dz, grads["b_out1"], st_out[3]), _ = _out_bwd(dh, y, gate[3], att, ones_e, z, wb_out[1], "b1_out_bwd")
    (dq, dk1, dv1, dsa, dsb), _ = _attn_bwd(q, kp, vp, probs, datt, None, "b1_attn_bwd")
    drb1 = _dbias_reduce(dsa.transpose(1, 0, 2), dsb.transpose(1, 0, 2), "b1_dbias")
    (dh, grads["b_in1"], st_in[3]), _ = _in_bwd(dq, dz, 0, u, hs[3], gl[3], scale[3], wb_in[1], dh, "b1_in_bwd")
    u, z, q, att, y, probs = saved[2]
    (datt, dz, grads["b_out0"], st_out[2]), _ = _out_bwd(dh, y, gate[2], att, ones_e, z, wb_out[0], "b0_out_bwd")
    (dq, dk, dv, dsa, dsb), got = _attn_bwd(q, kp, vp, probs, datt, (dk1, dv1), "b0_attn_bwd",
                                            comm=carry(["b_out1", "b_in1", "b_out0"]))
    land(["b_out1", "b_in1", "b_out0"], got)
    drb0 = _dbias_reduce(dsa.transpose(1, 0, 2), dsb.transpose(1, 0, 2), "b0_dbias")
    (dh, grads["b_in0"], st_in[2]), _ = _in_bwd(dq, dz, 0, u, hs[2], gl[2], scale[2], wb_in[0], dh, "b0_in_bwd")
    (dh, grads["kv"], st_kv), got = _in_bwd(dk, dv, PAD, uk, hs[2], kv_g, kv_scale, w_kv, dh, "kv_in_bwd",
                                            comm=carry(["b_in0"]))
    land(["b_in0"], got)
    st_pool = [None] * 2
    plan = {1: dict(o=[], p=[], i=["kv", "a_out1", "a_grp1"]), 0: dict(o=["a_in1"], p=["a_out0"], i=[])}
    early = ["b_out1", "b_in1", "b_out0", "b_in0", "kv", "a_out1", "a_grp1", "a_in1"]
    late = ["a_out0", "a_grp0", "a_in0"]
    both = {}

    def sum4(n):
        return _sum4(grads[n], landed[n], chip_arr, f"sum4_{n}")

    for l in (1, 0):
        u, z, pooled, mixed, y, w_in_l, wg_full, wo = saved[l]
        asl = a_scale[l:l + 1]
        (dms, dz, grads[f"a_out{l}"], st_out[l]), got = _out_bwd(dh, y, gate[l], mixed, asl, z, wo, f"a{l}_out_bwd",
                                                                comm=carry(plan[l]["o"]))
        land(plan[l]["o"], got)
        comm = carry(plan[l]["p"])
        if l == 0:
            mine = [sum4(n) for n in early]
            comm = _Comm(scatters=[grads[n] for n in plan[l]["p"]], swaps=mine)
        (dval, dwg, st_pool[l]), got = _pool_bwd(dms, mixed, pooled, wg_full, asl, f"a{l}_pool_bwd", comm=comm)
        land(plan[l]["p"], got)
        if l == 0:
            both.update({n: [a, b] for n, a, b in zip(early, mine, got[len(plan[l]["p"]):])})
        grads[f"a_grp{l}"] = (dwg.reshape(4, NCHIP, GW // NCHIP, GW).transpose(1, 0, 2, 3).reshape(NCHIP, GW, GW)
                              .astype(BF16))
        (dh, grads[f"a_in{l}"], st_in[l]), got = _in_bwd(dval, dz, 0, u, hs[l], gl[l], scale[l], w_in_l, dh, f"a{l}_in_bwd",
                                                         comm=carry(plan[l]["i"]))
        land(plan[l]["i"], got)
    pieces = st_in + [st_kv] + st_out + [st_fin]
    pieces += [_pad8(st_pool[l][0].reshape(2, D)) for l in range(2)]
    pieces += [_pad8(d.reshape(NH * NRELP // D, D)) for d in (drb0, drb1)]
    gathered, got = _allgather8(jnp.concatenate(pieces, axis=0), "gather_stats", comm=carry(["a_grp0", "a_in0"]))
    land(["a_grp0", "a_in0"], got)
    mine = [sum4(n) for n in late]
    both.update({n: [a, b] for n, a, b in zip(late, mine, _comm_only(_Comm(swaps=mine), "swap_last"))})
    return dh, both, gathered.reshape(8, N_STAT, D)


ROW_IN = [8 * l for l in range(4)]
ROW_KV = 32
ROW_OUT = [40 + 8 * l for l in range(4)]
ROW_FIN = 72
ROW_ASC = [80, 88]
ROW_RB = [96, 104]
N_STAT = 112


def kernel(x, c, ada_w, ada_b, norm_g, a_w_in, a_w_group, a_scale, a_w_out, kv_norm_g, kv_ada_w, kv_ada_b, w_kv, b_w_in, b_rel_bias, b_w_out, final_g, loss_target, m_ada_w, m_ada_b, m_norm_g, m_a_w_in, m_a_w_group, m_a_scale, m_a_w_out, m_kv_norm_g, m_kv_ada_w, m_kv_ada_b, m_w_kv, m_b_w_in, m_b_rel_bias, m_b_w_out, m_final_g, v_ada_w, v_ada_b, v_norm_g, v_a_w_in, v_a_w_group, v_a_scale, v_a_w_out, v_kv_norm_g, v_kv_ada_w, v_kv_ada_b, v_w_kv, v_b_w_in, v_b_rel_bias, v_b_w_out, v_final_g):
    xi, yi, ci = _place()
    chip = 2 * xi + yi
    dev = 4 * xi + 2 * yi + ci
    n_ada = ada_w.shape[2]
    n_kva = kv_ada_w.shape[1]
    n_asc = a_scale.shape[1]

    ada_b_sh = lax.dynamic_slice_in_dim(ada_b, chip * n_ada, n_ada, axis=1)
    kvb_sh = lax.dynamic_slice_in_dim(kv_ada_b, chip * n_kva, n_kva, axis=0)
    sh = dict(a_in=[a_w_in[l].astype(BF16) for l in range(2)], a_grp=[a_w_group[l].astype(BF16) for l in range(2)],
              a_out=[a_w_out[l].astype(BF16) for l in range(2)], kv=[w_kv.astype(BF16)],
              b_in=[b_w_in[l].astype(BF16) for l in range(2)], b_out=[b_w_out[l].astype(BF16) for l in range(2)])
    c_act, gathered, w_first = _prologue(
        jnp.broadcast_to(c, (SUBLANES, D)), ada_w, ada_b_sh[:, None, :], kv_ada_w, kvb_sh[None, :],
        a_scale.reshape(1, 2 * n_asc), _Comm(gathers=[sh["a_in"][0], sh["a_grp"][0], sh["a_out"][0]]), "prologue")
    rows = jnp.concatenate([lax.dynamic_slice_in_dim(gathered, SUBLANES * (2 * p + ci) + dev, 1, axis=0)
                            for p in range(NCHIP)], axis=0)
    mods = jnp.stack([rows[:, l * n_ada:(l + 1) * n_ada].reshape(3 * D) for l in range(4)])
    kvmod = rows[:, 4 * n_ada:4 * n_ada + n_kva].reshape(2 * D)
    o_asc = 4 * n_ada + n_kva
    a_scale_full = jnp.stack([rows[:, o_asc + l * n_asc:o_asc + (l + 1) * n_asc].reshape(E) for l in range(2)])

    chip_arr = jnp.reshape(chip, (1,)).astype(jnp.int32)
    dh, both, g3 = _example_step(x[0], loss_target[0], mods, kvmod, a_scale_full, norm_g, kv_norm_g, final_g,
                                 b_rel_bias, sh, w_first, chip_arr)
    grad_x = dh[None]

    red, loss_tile = _stats_reduce(g3, ROW_FIN + 1, "stats_reduce")
    loss = loss_tile[0, 0]

    def cat(rows_):
        return jnp.concatenate(rows_, axis=-1)

    g_ada_b = jnp.stack([cat([red[ROW_IN[l]], red[ROW_IN[l] + 1], red[ROW_OUT[l]]]) for l in range(4)])
    g_norm_g = jnp.stack([red[ROW_IN[l] + 2] for l in range(4)])
    g_kv_norm_g = red[ROW_KV + 2]
    g_kv_ada_b = cat([red[ROW_KV], red[ROW_KV + 1]])
    g_final_g = red[ROW_FIN]
    g_asc_full = jnp.stack([red[ROW_ASC[l]:ROW_ASC[l] + 2].reshape(E) for l in range(2)])
    g_a_scale = lax.dynamic_slice_in_dim(g_asc_full, chip * n_asc, n_asc, axis=1)
    g_rel = jnp.stack([red[ROW_RB[bi]:ROW_RB[bi] + NH * NRELP // D].reshape(NH, NRELP)[:, :NREL] for bi in range(2)])

    dmod = jnp.stack([cat([g3[:, ROW_IN[l]], g3[:, ROW_IN[l] + 1], g3[:, ROW_OUT[l]]]) for l in range(4)])
    dmod_sh = lax.dynamic_slice_in_dim(dmod, chip * n_ada, n_ada, axis=2)
    dkv = cat([g3[:, ROW_KV], g3[:, ROW_KV + 1]])[None]
    dkv_sh = lax.dynamic_slice_in_dim(dkv, chip * n_kva, n_kva, axis=2)
    c_act_t = c_act.T
    g_ada_w = _grad_ada(c_act_t, dmod_sh, "grad_ada_w")
    g_kv_ada_w = _grad_ada(c_act_t, dkv_sh, "grad_kv_ada_w")

    def upd(w, m, v, g, name, shape3):
        g = g.reshape(shape3) if not isinstance(g, list) else g
        outs = _adamw(w.reshape(shape3), m.reshape(shape3), v.reshape(shape3), g, name)
        return [o.reshape(w.shape) for o in outs]

    def pair(name):
        return [both[name + "0"], both[name + "1"]]

    res = {}
    res["ada_w"] = upd(ada_w, m_ada_w, v_ada_w, g_ada_w, "adamw_ada_w", ada_w.shape)
    res["ada_b"] = upd(ada_b, m_ada_b, v_ada_b, g_ada_b, "adamw_ada_b", (1,) + ada_b.shape)
    res["norm_g"] = upd(norm_g, m_norm_g, v_norm_g, g_norm_g, "adamw_norm_g", (1,) + norm_g.shape)
    res["a_w_in"] = upd(a_w_in, m_a_w_in, v_a_w_in, pair("a_in"), "adamw_a_w_in", a_w_in.shape)
    res["a_w_group"] = upd(a_w_group, m_a_w_group, v_a_w_group, pair("a_grp"), "adamw_a_w_group", (2, GW, GW))
    res["a_scale"] = upd(a_scale, m_a_scale, v_a_scale, g_a_scale, "adamw_a_scale", (1,) + a_scale.shape)
    res["a_w_out"] = upd(a_w_out, m_a_w_out, v_a_w_out, pair("a_out"), "adamw_a_w_out", a_w_out.shape)
    res["kv_norm_g"] = upd(kv_norm_g, m_kv_norm_g, v_kv_norm_g, g_kv_norm_g, "adamw_kv_norm_g", (1, 1, D))
    res["kv_ada_w"] = upd(kv_ada_w, m_kv_ada_w, v_kv_ada_w, g_kv_ada_w, "adamw_kv_ada_w", (1,) + kv_ada_w.shape)
    res["kv_ada_b"] = upd(kv_ada_b, m_kv_ada_b, v_kv_ada_b, g_kv_ada_b, "adamw_kv_ada_b", (1, 1, 2 * D))
    res["w_kv"] = upd(w_kv, m_w_kv, v_w_kv, [both["kv"]], "adamw_w_kv", (1,) + w_kv.shape)
    res["b_w_in"] = upd(b_w_in, m_b_w_in, v_b_w_in, pair("b_in"), "adamw_b_w_in", b_w_in.shape)
    res["b_rel_bias"] = upd(b_rel_bias, m_b_rel_bias, v_b_rel_bias, g_rel, "adamw_b_rel_bias", (1, 2 * NH, NREL))
    res["b_w_out"] = upd(b_w_out, m_b_w_out, v_b_w_out, pair("b_out"), "adamw_b_w_out", b_w_out.shape)
    res["final_g"] = upd(final_g, m_final_g, v_final_g, g_final_g, "adamw_final_g", (1, 1, D))

    names = ["ada_w", "ada_b", "norm_g", "a_w_in", "a_w_group", "a_scale", "a_w_out", "kv_norm_g", "kv_ada_w", "kv_ada_b",
             "w_kv", "b_w_in", "b_rel_bias", "b_w_out", "final_g"]
    return (loss, grad_x, *[res[n][0] for n in names], *[res[n][1] for n in names], *[res[n][2] for n in names],
            *[res[n][3] for n in names])
```

```python
import math

import jax
import jax.numpy as jnp
from jax import lax
from jax.experimental import pallas as pl
from jax.experimental.pallas import tpu as pltpu

F32 = jnp.float32
BF16 = jnp.bfloat16

D = 1024
E = 2048
NH = 16
HD = 128
CHUNK = 64
LEFT = 8
PAD = LEFT * CHUNK
NREL = 257
NRELP = 384
REL_CLIP = 128
EPS = 1e-6
NEG = -1e30
LOG2E = math.log2(math.e)
SM_SCALE = HD ** -0.5
POOL_W = (2, 4, 8, 16)
GW = 512
HALO = 16
QC = 4
QB = QC * CHUNK
NMASK = PAD // QB
WIN = (QC + LEFT) * CHUNK
BW = (LEFT + 2) * CHUNK
DBW = 4 * CHUNK
NSUB_FWD = 32
NSUB_BWD = 16
NCHIP = 4
LANES = 128
SUBLANES = 8

ADAM_LR = 0.001
ADAM_B1 = 0.9
ADAM_B2 = 0.999
ADAM_EPS = 1e-08
ADAM_WD = 0.01
ADAM_STEP = 10

MESH = pl.DeviceIdType.MESH
ANY = pl.BlockSpec(memory_space=pl.ANY)


def _params(n_axes, vmem_mb):
    return pltpu.CompilerParams(dimension_semantics=("arbitrary",) * n_axes, vmem_limit_bytes=vmem_mb * 2 ** 20)


def _nn(a, b):
    return jnp.dot(a, b, preferred_element_type=F32)


def _nt(a, b):
    return lax.dot_general(a, b, (((1,), (1,)), ((), ())), preferred_element_type=F32)


def _tn(a, b):
    return lax.dot_general(a, b, (((0,), (0,)), ((), ())), preferred_element_type=F32)


def _row(n):
    return pl.BlockSpec((1, n), lambda i: (0, 0))


def _colsum(x):
    return jnp.sum(x, axis=0, keepdims=True)


def _place():
    return lax.axis_index("x"), lax.axis_index("y"), lax.axis_index("c")


class _Comm:
    def __init__(self, gathers=(), scatters=(), swaps=()):
        self.n_g = len(gathers)
        self.n_chip = len(gathers) + len(scatters)
        self.n_sw = len(swaps)
        self.arrays = list(gathers) + list(scatters) + list(swaps)
        self.n = len(self.arrays)
        self.half = [a.shape[0] // 2 for a in gathers]
        self.out_shape = ([jax.ShapeDtypeStruct((NCHIP,) + a.shape, a.dtype) for a in gathers]
                          + [jax.ShapeDtypeStruct((3,) + a.shape[1:], a.dtype) for a in scatters]
                          + [jax.ShapeDtypeStruct(a.shape, a.dtype) for a in swaps])
        n_c, n_f, n_s = max(3 * self.n_chip, 1), max(3 * self.n_g, 1), max(self.n_sw, 1)
        self.scratch = [pltpu.SemaphoreType.DMA((n_c,)), pltpu.SemaphoreType.DMA((n_c,)),
                        pltpu.SemaphoreType.DMA((max(self.n_g, 1),)), pltpu.SemaphoreType.DMA((n_f,)),
                        pltpu.SemaphoreType.DMA((n_f,)), pltpu.SemaphoreType.DMA((n_s,)), pltpu.SemaphoreType.DMA((n_s,))]

    def _chip_copies(self, ins, outs, send, recv, landing):
        x, y, c = _place()
        chips = [(1 - x, y), (x, 1 - y), (1 - x, 1 - y)]
        mine = 2 * x + y
        cps = []
        for k in range(self.n_chip):
            for j, (cx, cy) in enumerate(chips):
                q = 2 * cx + cy
                if k < self.n_g:
                    part = pl.ds(c * self.half[k], self.half[k])
                    src = ins[k].at[part]
                    dst = outs[k].at[q if landing else mine, part]
                else:
                    src = ins[k].at[q]
                    dst = outs[k].at[j]
                cps.append(pltpu.make_async_remote_copy(
                    src_ref=src, dst_ref=dst, send_sem=send.at[3 * k + j], recv_sem=recv.at[3 * k + j],
                    device_id=(cx, cy, c), device_id_type=MESH))
        return cps

    def _core_copies(self, outs, fsend, frecv, landing):
        x, y, c = _place()
        chips = [(1 - x, y), (x, 1 - y), (1 - x, 1 - y)]
        cps = []
        for k in range(self.n_g):
            for j, (cx, cy) in enumerate(chips):
                part = pl.ds((1 - c if landing else c) * self.half[k], self.half[k])
                blk = outs[k].at[2 * cx + cy, part]
                cps.append(pltpu.make_async_remote_copy(
                    src_ref=blk, dst_ref=blk, send_sem=fsend.at[3 * k + j], recv_sem=frecv.at[3 * k + j],
                    device_id=(x, y, 1 - c), device_id_type=MESH))
        return cps

    def _local_copies(self, ins, outs, loc):
        x, y, _ = _place()
        return [pltpu.make_async_copy(ins[k], outs[k].at[2 * x + y], loc.at[k]) for k in range(self.n_g)]

    def _swap_copies(self, ins, outs, ssend, srecv):
        x, y, c = _place()
        return [pltpu.make_async_remote_copy(
            src_ref=ins[k], dst_ref=outs[k], send_sem=ssend.at[k - self.n_chip], recv_sem=srecv.at[k - self.n_chip],
            device_id=(x, y, 1 - c), device_id_type=MESH) for k in range(self.n_chip, self.n)]

    def start(self, ins, outs, send, recv, loc, fsend, frecv, ssend, srecv):
        for cp in (self._local_copies(ins, outs, loc) + self._chip_copies(ins, outs, send, recv, False)
                   + self._swap_copies(ins, outs, ssend, srecv)):
            cp.start()

    def wait(self, ins, outs, send, recv, loc, fsend, frecv, ssend, srecv):
        lands = self._chip_copies(ins, outs, send, recv, True)
        passes = self._core_copies(outs, fsend, frecv, False)
        for k in range(self.n_chip):
            for j in range(3):
                lands[3 * k + j].wait_recv()
                if k < self.n_g:
                    passes[3 * k + j].start()
        for cp in self._core_copies(outs, fsend, frecv, True):
            cp.wait_recv()
        swaps = self._swap_copies(ins, outs, ssend, srecv)
        for cp in swaps:
            cp.wait_recv()
        for cp in self._chip_copies(ins, outs, send, recv, False) + passes + swaps:
            cp.wait_send()
        for cp in self._local_copies(ins, outs, loc):
            cp.wait()


def _call(body, name, grid, in_specs, out_specs, out_shape, scratch, params, args, comm=None):
    n_in, n_out, n_sc = len(in_specs), len(out_specs), len(scratch)
    if comm is None:
        outs = pl.pallas_call(body, name=name, grid=grid, in_specs=in_specs, out_specs=out_specs, out_shape=out_shape,
                              scratch_shapes=scratch, compiler_params=params)(*args)
        return list(outs), []
    n = comm.n
    o0 = n_in + n
    s0 = o0 + n_out + n

    def wrapped(*refs):
        c_refs = (refs[n_in:o0], refs[o0 + n_out:s0]) + tuple(refs[s0 + n_sc:])
        ids = [pl.program_id(a) for a in range(len(grid))]
        first = ids[0] == 0
        last = ids[0] == grid[0] - 1
        for a in range(1, len(grid)):
            first = first & (ids[a] == 0)
            last = last & (ids[a] == grid[a] - 1)

        @pl.when(first)
        def _():
            comm.start(*c_refs)

        body(*refs[:n_in], *refs[o0:o0 + n_out], *refs[s0:s0 + n_sc])

        @pl.when(last)
        def _():
            comm.wait(*c_refs)

    outs = pl.pallas_call(
        wrapped, name=name, grid=grid, in_specs=list(in_specs) + [ANY] * n, out_specs=list(out_specs) + [ANY] * n,
        out_shape=list(out_shape) + comm.out_shape, scratch_shapes=list(scratch) + comm.scratch, compiler_params=params,
    )(*args, *comm.arrays)
    return list(outs[:n_out]), list(outs[n_out:])


def _comm_only(comm, name):
    def body(*refs):
        c_refs = (refs[:comm.n], refs[comm.n:2 * comm.n]) + tuple(refs[2 * comm.n:])
        comm.start(*c_refs)
        comm.wait(*c_refs)

    return pl.pallas_call(body, name=name, in_specs=[ANY] * comm.n, out_specs=[ANY] * comm.n, out_shape=comm.out_shape,
                          scratch_shapes=comm.scratch)(*comm.arrays)


def _in_fwd(h, g, shift, scale, w, dt_a, dt_b, name, pad_rows=0, comm=None, tm=512):
    S = h.shape[0]
    n_pad = pad_rows // tm

    def body(h_ref, g_ref, sh_ref, sc_ref, w_hbm, u_ref, oa_ref, ob_ref, w_v, sem):
        i = pl.program_id(0)

        @pl.when(i == 0)
        def _():
            cp = pltpu.make_async_copy(w_hbm, w_v, sem)
            cp.start()
            cp.wait()

        hh = h_ref[...]
        r = lax.rsqrt(jnp.mean(hh * hh, axis=-1, keepdims=True) + EPS)
        u = (hh * r * g_ref[...]) * (1.0 + sc_ref[...]) + sh_ref[...]
        ub = u.astype(BF16)
        u_ref[...] = ub
        for q in range(NCHIP):
            o_ref = oa_ref if q < 2 else ob_ref
            o_ref[:, (q % 2) * D:(q % 2 + 1) * D] = _nn(ub, w_v[q]).astype(o_ref.dtype)

        if n_pad:
            @pl.when(i < n_pad)
            def _():
                oa_ref[...] = jnp.zeros(oa_ref.shape, oa_ref.dtype)
                ob_ref[...] = jnp.zeros(ob_ref.shape, ob_ref.dtype)

    def src(i):
        return (jnp.maximum(i - n_pad, 0), 0)

    outs, landed = _call(
        body, name, (S // tm + n_pad,),
        [pl.BlockSpec((tm, D), src), _row(D), _row(D), _row(D), ANY],
        [pl.BlockSpec((tm, D), src), pl.BlockSpec((tm, E), lambda i: (i, 0)), pl.BlockSpec((tm, E), lambda i: (i, 0))],
        [jax.ShapeDtypeStruct((S, D), BF16), jax.ShapeDtypeStruct((S + pad_rows, E), dt_a),
         jax.ShapeDtypeStruct((S + pad_rows, E), dt_b)],
        [pltpu.VMEM((NCHIP, D, D), BF16), pltpu.SemaphoreType.DMA],
        _params(1, 52), (h, g, shift, scale, w), comm)
    return outs, landed


def _a_fwd(h, g, shift, scale, asc, gate, w_in, wg, w_out, name, comm=None, tm=512):
    S = h.shape[0]

    def body(h_ref, g_ref, sh_ref, sc_ref, as_ref, gate_ref, wi_hbm, wg_hbm, wo_hbm,
             u_ref, z_ref, p_ref, m_ref, y_ref, ho_ref, wi_v, wg_v, wo_v, buf, sems):
        i = pl.program_id(0)

        @pl.when(i == 0)
        def _():
            cps = [pltpu.make_async_copy(wi_hbm, wi_v, sems.at[0]), pltpu.make_async_copy(wg_hbm, wg_v, sems.at[1]),
                   pltpu.make_async_copy(wo_hbm, wo_v, sems.at[2])]
            for cp in cps:
                cp.start()
            buf[0:HALO, :] = jnp.zeros((HALO, E), F32)
            for cp in cps:
                cp.wait()

        hh = h_ref[...]
        r = lax.rsqrt(jnp.mean(hh * hh, axis=-1, keepdims=True) + EPS)
        ub = ((hh * r * g_ref[...]) * (1.0 + sc_ref[...]) + sh_ref[...]).astype(BF16)
        u_ref[...] = ub
        for q in range(2):
            buf[HALO:HALO + tm, q * D:(q + 1) * D] = _nn(ub, wi_v[q])
        t = i * tm + lax.broadcasted_iota(jnp.int32, (tm, 1), 0)
        y = None
        for gi, w in enumerate(POOL_W):
            cols = slice(gi * GW, (gi + 1) * GW)
            x = buf[:, cols]
            s = x
            k = 1
            while k < w:
                s = s + pltpu.roll(s, k, 0)
                k *= 2
            inv_cnt = 1.0 / jnp.minimum(t + 1, w).astype(F32)
            pb = (s[HALO:, :] * inv_cnt - x[HALO:, :]).astype(BF16)
            p_ref[:, cols] = pb
            mb = _nn(pb, wg_v[gi]).astype(BF16)
            m_ref[:, cols] = mb
            zb = _nn(ub, wi_v[2 + gi // 2, :, (gi % 2) * GW:(gi % 2 + 1) * GW]).astype(BF16)
            z_ref[:, cols] = zb
            zz = zb.astype(F32)
            act = ((mb.astype(F32) * as_ref[:, cols]) * (zz * jax.nn.sigmoid(zz))).astype(BF16)
            part = _nn(act, wo_v[gi])
            y = part if y is None else y + part
        buf[0:HALO, :] = buf[tm:tm + HALO, :]
        y_ref[...] = y.astype(BF16)
        ho_ref[...] = hh + gate_ref[...] * y

    rows_d = pl.BlockSpec((tm, D), lambda i: (i, 0))
    rows_e = pl.BlockSpec((tm, E), lambda i: (i, 0))
    return _call(
        body, name, (S // tm,),
        [rows_d, _row(D), _row(D), _row(D), _row(E), _row(D), ANY, ANY, ANY],
        [rows_d, rows_e, rows_e, rows_e, rows_d, rows_d],
        [jax.ShapeDtypeStruct((S, D), BF16), jax.ShapeDtypeStruct((S, E), BF16), jax.ShapeDtypeStruct((S, E), BF16),
         jax.ShapeDtypeStruct((S, E), BF16), jax.ShapeDtypeStruct((S, D), BF16), jax.ShapeDtypeStruct((S, D), F32)],
        [pltpu.VMEM((NCHIP, D, D), BF16), pltpu.VMEM((4, GW, GW), BF16), pltpu.VMEM((NCHIP, GW, D), BF16),
         pltpu.VMEM((tm + HALO, E), F32), pltpu.SemaphoreType.DMA((3,))],
        _params(1, 60), (h, g, shift, scale, asc, gate, w_in, wg, w_out), comm)


def _out_fwd(a, z, w, gate, h, name, head=None, comm=None, tm=512):
    S = h.shape[0]
    kb = E // NCHIP
    n_in = 5 if head is None else 7

    def body(*refs):
        a_ref, z_ref, w_hbm, gate_ref, h_ref = refs[:5]
        w_v, sem = refs[-2:]
        i = pl.program_id(0)

        @pl.when(i == 0)
        def _():
            cp = pltpu.make_async_copy(w_hbm, w_v, sem)
            cp.start()
            cp.wait()

        y = None
        for p in range(NCHIP):
            cols = slice(p * kb, (p + 1) * kb)
            zz = z_ref[:, cols].astype(F32)
            act = (a_ref[:, cols].astype(F32) * (zz * jax.nn.sigmoid(zz))).astype(BF16)
            part = _nn(act, w_v[p])
            y = part if y is None else y + part
        refs[n_in][...] = y.astype(BF16)
        hh = h_ref[...] + gate_ref[...] * y
        if head is None:
            refs[n_in + 1][...] = hh
            return
        g_ref, t_ref = refs[5:7]
        dh_ref, st_ref = refs[n_in + 1:n_in + 3]

        @pl.when(i == 0)
        def _():
            st_ref[...] = jnp.zeros((SUBLANES, D), F32)

        r = lax.rsqrt(jnp.mean(hh * hh, axis=-1, keepdims=True) + EPS)
        xhat = hh * r
        diff = xhat * g_ref[...] - t_ref[...]
        st_ref[1:2, :] += _colsum(diff * diff)
        dout = diff * (1.0 / D)
        st_ref[0:1, :] += _colsum(dout * xhat)
        dx = dout * g_ref[...]
        dh_ref[...] = r * (dx - xhat * jnp.mean(dx * xhat, axis=-1, keepdims=True))

    rows_d = pl.BlockSpec((tm, D), lambda i: (i, 0))
    rows_e = pl.BlockSpec((tm, E), lambda i: (i, 0))
    in_specs = [rows_e, rows_e, ANY, _row(D), rows_d]
    out_specs = [rows_d, rows_d]
    out_shape = [jax.ShapeDtypeStruct((S, D), BF16), jax.ShapeDtypeStruct((S, D), F32)]
    args = (a, z, w, gate, h)
    if head is not None:
        in_specs += [_row(D), rows_d]
        out_specs += [pl.BlockSpec((SUBLANES, D), lambda i: (0, 0))]
        out_shape += [jax.ShapeDtypeStruct((SUBLANES, D), F32)]
        args += tuple(head)
    return _call(body, name, (S // tm,), in_specs, out_specs, out_shape,
                 [pltpu.VMEM((NCHIP, kb, D), BF16), pltpu.SemaphoreType.DMA], _params(1, 52), args, comm)


TW = BW + LANES


def _diag_onehot(transpose):
    shape = (TW, NRELP) if transpose else (NRELP, TW)
    j = lax.broadcasted_iota(jnp.int32, shape, 0 if transpose else 1)
    r = lax.broadcasted_iota(jnp.int32, shape, 1 if transpose else 0)
    idx = jnp.clip(PAD - (j - LANES), -REL_CLIP, REL_CLIP) + REL_CLIP
    return jnp.where(idx == r, 1.0, 0.0).astype(BF16)


def _strip_valid():
    m = lax.broadcasted_iota(jnp.int32, (NH, BW), 1)
    return m < (LEFT + 1) * CHUNK, m >= CHUNK


def _bias_build(rb, name):
    def body(rb_ref, a_ref, b_ref):
        x = rb_ref[...]
        hi = x.astype(BF16)
        r1 = x - hi.astype(F32)
        mid = r1.astype(BF16)
        lo = (r1 - mid.astype(F32)).astype(BF16)
        oh = _diag_onehot(False)
        diag = (_nn(hi, oh) + _nn(mid, oh)) + _nn(lo, oh)
        valid_a, valid_b = _strip_valid()
        for qi in range(CHUNK):
            a_ref[qi] = jnp.where(valid_a, pltpu.roll(diag, TW - (LANES - qi), 1)[:, :BW], NEG)
            b_ref[qi] = jnp.where(valid_b, pltpu.roll(diag, TW - (CHUNK - qi), 1)[:, :BW], NEG)

    vmem = pl.BlockSpec(memory_space=pltpu.VMEM)
    return pl.pallas_call(
        body, name=name, in_specs=[vmem], out_specs=[vmem, vmem],
        out_shape=[jax.ShapeDtypeStruct((CHUNK, NH, BW), F32), jax.ShapeDtypeStruct((CHUNK, NH, BW), F32)],
        compiler_params=pltpu.CompilerParams(vmem_limit_bytes=32 * 2 ** 20),
    )(rb)


def _dbias_reduce(dba, dbb, name):
    def body(a_ref, b_ref, o_ref):
        valid_a, valid_b = _strip_valid()
        zeros = jnp.zeros((NH, TW - BW), F32)
        acc = jnp.zeros((NH, TW), F32)
        for qi in range(CHUNK):
            xa = jnp.concatenate([jnp.where(valid_a, a_ref[qi], 0.0), zeros], axis=1)
            xb = jnp.concatenate([jnp.where(valid_b, b_ref[qi], 0.0), zeros], axis=1)
            acc = acc + (pltpu.roll(xa, LANES - qi, 1) + pltpu.roll(xb, CHUNK - qi, 1))
        oh = _diag_onehot(True)
        hi = acc.astype(BF16)
        mid = (acc - hi.astype(F32)).astype(BF16)
        r = lax.broadcasted_iota(jnp.int32, (NH, NRELP), 1)
        near = jnp.where(r < 2 * REL_CLIP, _nn(hi, oh) + _nn(mid, oh), 0.0)
        o_ref[...] = jnp.where(r == 2 * REL_CLIP, -jnp.sum(near, axis=-1, keepdims=True), near)

    vmem = pl.BlockSpec(memory_space=pltpu.VMEM)
    return pl.pallas_call(
        body, name=name, in_specs=[vmem, vmem], out_specs=vmem,
        out_shape=jax.ShapeDtypeStruct((NH, NRELP), F32),
        compiler_params=pltpu.CompilerParams(vmem_limit_bytes=32 * 2 ** 20),
    )(dba, dbb)


def _build_bias(bias3, ba_ref, bb_ref):
    bias3[NMASK] = jnp.full((QB, WIN), NEG, F32)
    for qc in range(QC):
        rows = slice(qc * CHUNK, (qc + 1) * CHUNK)
        if qc % 2 == 0:
            bias3[NMASK, rows, qc * CHUNK:qc * CHUNK + BW] = ba_ref[...] * LOG2E
        else:
            bias3[NMASK, rows, (qc - 1) * CHUNK:(qc - 1) * CHUNK + BW] = bb_ref[...] * LOG2E
    col = lax.broadcasted_iota(jnp.int32, (QB, WIN), 1)
    for sub in range(NMASK):
        bias3[sub] = jnp.where(col < PAD - sub * QB, NEG, bias3[NMASK])


def _nsub(S, most):
    n = min(most, S // QB)
    assert S % (n * QB) == 0 and n >= NMASK
    return n


def _row0(i, sub, nsub):
    return pl.multiple_of((i * nsub + sub) * QB, QB)


def _scores(q_ref, k_ref, i, sub, nsub):
    return _nt(q_ref[sub * QB:(sub + 1) * QB, :], k_ref[pl.ds(_row0(i, sub, nsub), WIN), :])


HALF = QB // 2
LIVE = WIN - LANES


def _live(half):
    return slice(half * HALF, (half + 1) * HALF), slice(half * LANES, half * LANES + LIVE)


def _widen(x, half):
    zeros = jnp.zeros((HALF, LANES), x.dtype)
    return jnp.concatenate([x, zeros] if half == 0 else [zeros, x], axis=1)


def _probs(s, bias3, i, sub):
    which = jnp.where(i == 0, sub, NMASK) if sub < NMASK else NMASK
    out = []
    for half in range(2):
        rows, cols = _live(half)
        t = s[rows, cols] * (SM_SCALE * LOG2E) + bias3[which, rows, cols]
        e = jnp.exp2(t - jnp.max(t, axis=-1, keepdims=True))
        out.append(_widen((e * (1.0 / jnp.sum(e, axis=-1, keepdims=True))).astype(BF16), half))
    return jnp.concatenate(out, axis=0)


def _attn_fwd(q, kp, vp, ba, bb, name, comm=None):
    S = q.shape[0]
    nsub = _nsub(S, NSUB_FWD)
    R = nsub * QB

    def body(q_ref, k_ref, v_ref, ba_ref, bb_ref, o_ref, p_ref, bias3):
        i = pl.program_id(1)

        @pl.when(i == 0)
        def _():
            _build_bias(bias3, ba_ref, bb_ref)

        s_next = _scores(q_ref, k_ref, i, 0, nsub)
        for sub in range(nsub):
            s = s_next
            if sub + 1 < nsub:
                s_next = _scores(q_ref, k_ref, i, sub + 1, nsub)
            pb = _probs(s, bias3, i, sub)
            p_ref[sub] = pb
            o_ref[sub * QB:(sub + 1) * QB, :] = _nn(pb, v_ref[pl.ds(_row0(i, sub, nsub), WIN), :]).astype(BF16)

    return _call(
        body, name, (NH, S // R),
        [pl.BlockSpec((R, HD), lambda h, i: (i, h)), pl.BlockSpec((S + PAD, HD), lambda h, i: (0, h)),
         pl.BlockSpec((S + PAD, HD), lambda h, i: (0, h)), pl.BlockSpec((None, CHUNK, BW), lambda h, i: (h, 0, 0)),
         pl.BlockSpec((None, CHUNK, BW), lambda h, i: (h, 0, 0))],
        [pl.BlockSpec((R, HD), lambda h, i: (i, h)), pl.BlockSpec((None, nsub, QB, WIN), lambda h, i: (h, i, 0, 0))],
        [jax.ShapeDtypeStruct((S, E), BF16), jax.ShapeDtypeStruct((NH, S // QB, QB, WIN), BF16)],
        [pltpu.VMEM((NMASK + 1, QB, WIN), F32)],
        _params(2, 48), (q, kp, vp, ba, bb), comm)


def _store_grad(acc, stage, dw_hbm, sem):
    for q in range(NCHIP):
        stage[...] = acc[q].astype(BF16)
        cp = pltpu.make_async_copy(stage, dw_hbm.at[q], sem)
        cp.start()
        cp.wait()


def _out_bwd(dh, y, gate, a, cs, z, w, name, comm=None, tm=256):
    S = dh.shape[0]
    kb = E // NCHIP
    cb = 256
    n_t = S // tm

    def body(dh_ref, y_ref, gate_ref, a_ref, cs_ref, z_ref, w_hbm, da_ref, dz_ref, dw_hbm, st_ref, w_v, acc, stage, sem):
        i = pl.program_id(0)

        @pl.when(i == 0)
        def _():
            cp = pltpu.make_async_copy(w_hbm, w_v, sem)
            cp.start()
            acc[...] = jnp.zeros(acc.shape, F32)
            st_ref[...] = jnp.zeros((SUBLANES, D), F32)
            cp.wait()

        dhh = dh_ref[...]
        st_ref[0:1, :] += _colsum(dhh * y_ref[...].astype(F32))
        dy = (dhh * gate_ref[...]).astype(BF16)
        for blk in range(E // cb):
            p, r0 = divmod(blk * cb, kb)
            cols = slice(blk * cb, (blk + 1) * cb)
            zz = z_ref[:, cols].astype(F32)
            sig = jax.nn.sigmoid(zz)
            sz = zz * sig
            ae = a_ref[:, cols].astype(F32) * cs_ref[:, cols]
            acc[p, r0:r0 + cb, :] += _tn((ae * sz).astype(BF16), dy)
            dact = _nt(dy, w_v[p, r0:r0 + cb, :])
            da_ref[:, cols] = (dact * sz).astype(BF16)
            dz_ref[:, cols] = (dact * ae * (sig * (1.0 + zz * (1.0 - sig)))).astype(BF16)

        @pl.when(i == n_t - 1)
        def _():
            _store_grad(acc, stage, dw_hbm, sem)

    return _call(
        body, name, (n_t,),
        [pl.BlockSpec((tm, D), lambda i: (i, 0)), pl.BlockSpec((tm, D), lambda i: (i, 0)), _row(D),
         pl.BlockSpec((tm, E), lambda i: (i, 0)), _row(E), pl.BlockSpec((tm, E), lambda i: (i, 0)), ANY],
        [pl.BlockSpec((tm, E), lambda i: (i, 0)), pl.BlockSpec((tm, E), lambda i: (i, 0)), ANY,
         pl.BlockSpec((SUBLANES, D), lambda i: (0, 0))],
        [jax.ShapeDtypeStruct((S, E), BF16), jax.ShapeDtypeStruct((S, E), BF16),
         jax.ShapeDtypeStruct((NCHIP, kb, D), BF16), jax.ShapeDtypeStruct((SUBLANES, D), F32)],
        [pltpu.VMEM((NCHIP, kb, D), BF16), pltpu.VMEM((NCHIP, kb, D), F32), pltpu.VMEM((kb, D), BF16),
         pltpu.SemaphoreType.DMA],
        _params(1, 52), (dh, y, gate, a, cs, z, w), comm)


def _attn_bwd(q, kp, vp, probs, do, prev, name, comm=None):
    S = q.shape[0]
    nsub = _nsub(S, NSUB_BWD)
    R = nsub * QB
    n_i = S // R
    dt_kv = F32 if prev is None else BF16

    def body(*refs):
        q_ref, k_ref, v_ref, p_ref, do_ref = refs[:5]
        refs = refs[5:]
        if prev is not None:
            pk_hbm, pv_hbm = refs[:2]
            refs = refs[2:]
        dq_ref, dk_ref, dv_ref, dba_ref, dbb_ref, dbias, dk_acc, dv_acc = refs[:8]
        if prev is not None:
            pk_v, pv_v, sems = refs[8:]
        h = pl.program_id(0)
        i = pl.program_id(1)

        def prev_copies():
            cols = pl.ds(pl.multiple_of(h * HD, HD), HD)
            return (pltpu.make_async_copy(pk_hbm.at[:, cols], pk_v, sems.at[0]),
                    pltpu.make_async_copy(pv_hbm.at[:, cols], pv_v, sems.at[1]))

        @pl.when(i == 0)
        def _():
            if prev is not None:
                for cp in prev_copies():
                    cp.start()
            dbias[...] = jnp.zeros((2, CHUNK, DBW), F32)
            dk_acc[...] = jnp.zeros((S + PAD, HD), F32)
            dv_acc[...] = jnp.zeros((S + PAD, HD), F32)

        def mxu_in(sub):
            return _nt(do_ref[sub * QB:(sub + 1) * QB, :], v_ref[pl.ds(_row0(i, sub, nsub), WIN), :])

        nxt = mxu_in(0)
        for sub in range(nsub):
            rows = slice(sub * QB, (sub + 1) * QB)
            win = pl.ds(_row0(i, sub, nsub), WIN)
            dp = nxt
            if sub + 1 < nsub:
                nxt = mxu_in(sub + 1)
            parts = []
            for half in range(2):
                hrows, hcols = _live(half)
                p = p_ref[sub, hrows, hcols].astype(F32)
                dph = dp[hrows, hcols]
                ds = p * (dph - jnp.sum(p * dph, axis=-1, keepdims=True))
                dbias[0] += ds[0:CHUNK, LIVE - DBW:LIVE]
                dbias[1] += ds[CHUNK:HALF, LIVE - DBW:LIVE]
                parts.append(_widen((ds * SM_SCALE).astype(BF16), half))
            dsb = jnp.concatenate(parts, axis=0)
            dq_ref[rows, :] = _nn(dsb, k_ref[win, :]).astype(BF16)
            dk_acc[win, :] += _tn(dsb, q_ref[rows, :])
            dv_acc[win, :] += _tn(p_ref[sub], do_ref[rows, :])

        @pl.when(i == n_i - 1)
        def _():
            zeros = jnp.zeros((CHUNK, BW - DBW), F32)
            dba_ref[...] = jnp.concatenate([zeros, dbias[0]], axis=1)
            dbb_ref[...] = jnp.concatenate([zeros, dbias[1]], axis=1)
            if prev is None:
                dk_ref[...] = dk_acc[...]
                dv_ref[...] = dv_acc[...]
            else:
                for cp in prev_copies():
                    cp.wait()
                dk_ref[...] = (dk_acc[...] + pk_v[...]).astype(BF16)
                dv_ref[...] = (dv_acc[...] + pv_v[...]).astype(BF16)

    head = pl.BlockSpec((S + PAD, HD), lambda h, i: (0, h))
    strip = pl.BlockSpec((None, CHUNK, BW), lambda h, i: (h, 0, 0))
    blk = pl.BlockSpec((R, HD), lambda h, i: (i, h))
    in_specs = [blk, head, head, pl.BlockSpec((None, nsub, QB, WIN), lambda h, i: (h, i, 0, 0)), blk]
    scratch = [pltpu.VMEM((2, CHUNK, DBW), F32), pltpu.VMEM((S + PAD, HD), F32), pltpu.VMEM((S + PAD, HD), F32)]
    args = (q, kp, vp, probs, do)
    if prev is not None:
        in_specs += [ANY, ANY]
        scratch += [pltpu.VMEM((S + PAD, HD), F32), pltpu.VMEM((S + PAD, HD), F32), pltpu.SemaphoreType.DMA((2,))]
        args += tuple(prev)
    return _call(
        body, name, (NH, n_i), in_specs, [blk, head, head, strip, strip],
        [jax.ShapeDtypeStruct((S, E), BF16), jax.ShapeDtypeStruct((S + PAD, E), dt_kv),
         jax.ShapeDtypeStruct((S + PAD, E), dt_kv), jax.ShapeDtypeStruct((NH, CHUNK, BW), F32),
         jax.ShapeDtypeStruct((NH, CHUNK, BW), F32)],
        scratch, _params(2, 56), args, comm)


def _pool_bwd(dms, mixed, pooled, wg, a_scale, name, comm=None, tm=512):
    S = dms.shape[0]
    n_t = S // tm

    def rev(i):
        return (n_t - 1 - i, 0)

    def body(d_ref, m_ref, p_ref, wg_ref, as_ref, dv_ref, dwg_ref, st_ref, buf):
        i = pl.program_id(0)

        @pl.when(i == 0)
        def _():
            buf[tm:tm + HALO, :] = jnp.zeros((HALO, E), F32)
            dwg_ref[...] = jnp.zeros((4, GW, GW), F32)
            st_ref[...] = jnp.zeros((SUBLANES, E), F32)

        t = (n_t - 1 - i) * tm + lax.broadcasted_iota(jnp.int32, (tm, 1), 0)
        st_ref[0:1, :] += _colsum(d_ref[...].astype(F32) * m_ref[...].astype(F32))
        for gi, w in enumerate(POOL_W):
            cols = slice(gi * GW, (gi + 1) * GW)
            dm = (d_ref[:, cols].astype(F32) * as_ref[:, cols]).astype(BF16)
            dpool = _nt(dm, wg_ref[gi])
            dwg_ref[gi] += _tn(p_ref[:, cols], dm)
            inv_cnt = 1.0 / jnp.minimum(t + 1, w).astype(F32)
            buf[0:tm, cols] = dpool * inv_cnt
            s = buf[:, cols]
            k = 1
            while k < w:
                s = s + pltpu.roll(s, tm + HALO - k, 0)
                k *= 2
            dv_ref[:, cols] = (s[0:tm, :] - dpool).astype(BF16)
        buf[tm:tm + HALO, :] = buf[0:HALO, :]

    return _call(
        body, name, (n_t,),
        [pl.BlockSpec((tm, E), rev), pl.BlockSpec((tm, E), rev), pl.BlockSpec((tm, E), rev),
         pl.BlockSpec((4, GW, GW), lambda i: (0, 0, 0)), _row(E)],
        [pl.BlockSpec((tm, E), rev), pl.BlockSpec((4, GW, GW), lambda i: (0, 0, 0)),
         pl.BlockSpec((SUBLANES, E), lambda i: (0, 0))],
        [jax.ShapeDtypeStruct((S, E), BF16), jax.ShapeDtypeStruct((4, GW, GW), F32),
         jax.ShapeDtypeStruct((SUBLANES, E), F32)],
        [pltpu.VMEM((tm + HALO, E), F32)],
        _params(1, 52), (dms, mixed, pooled, wg, a_scale), comm)


def _in_bwd(da, db, row_off, u, h, g, scale, w, dh_out, name, comm=None, tm=512):
    S = h.shape[0]
    n_t = S // tm
    off = row_off // tm

    def body(da_ref, db_ref, u_ref, h_ref, g_ref, sc_ref, w_hbm, dho_ref, dhi_ref, dw_hbm, st_ref, w_v, acc, stage, sem):
        i = pl.program_id(0)

        @pl.when(i == 0)
        def _():
            cp = pltpu.make_async_copy(w_hbm, w_v, sem)
            cp.start()
            acc[...] = jnp.zeros(acc.shape, F32)
            st_ref[...] = jnp.zeros((SUBLANES, D), F32)
            cp.wait()

        ub = u_ref[...]
        du = None
        for q in range(NCHIP):
            d_ref = da_ref if q < 2 else db_ref
            dv = d_ref[:, (q % 2) * D:(q % 2 + 1) * D]
            acc[q] += _tn(ub, dv)
            part = _nt(dv, w_v[q])
            du = part if du is None else du + part

        hh = h_ref[...]
        r = lax.rsqrt(jnp.mean(hh * hh, axis=-1, keepdims=True) + EPS)
        xhat = hh * r
        gg = g_ref[...]
        st_ref[0:1, :] += _colsum(du)
        st_ref[1:2, :] += _colsum(du * (xhat * gg))
        dn = du * (1.0 + sc_ref[...])
        st_ref[2:3, :] += _colsum(dn * xhat)
        dx = dn * gg
        dhi_ref[...] = dho_ref[...] + r * (dx - xhat * jnp.mean(dx * xhat, axis=-1, keepdims=True))

        @pl.when(i == n_t - 1)
        def _():
            _store_grad(acc, stage, dw_hbm, sem)

    part_spec = pl.BlockSpec((tm, E), lambda i: (i + off, 0))
    return _call(
        body, name, (n_t,),
        [part_spec, part_spec, pl.BlockSpec((tm, D), lambda i: (i, 0)), pl.BlockSpec((tm, D), lambda i: (i, 0)),
         _row(D), _row(D), ANY, pl.BlockSpec((tm, D), lambda i: (i, 0))],
        [pl.BlockSpec((tm, D), lambda i: (i, 0)), ANY, pl.BlockSpec((SUBLANES, D), lambda i: (0, 0))],
        [jax.ShapeDtypeStruct((S, D), F32), jax.ShapeDtypeStruct((NCHIP, D, D), BF16),
         jax.ShapeDtypeStruct((SUBLANES, D), F32)],
        [pltpu.VMEM((NCHIP, D, D), BF16), pltpu.VMEM((NCHIP, D, D), F32), pltpu.VMEM((D, D), BF16),
         pltpu.SemaphoreType.DMA],
        _params(1, 60), (da, db, u, h, g, scale, w, dh_out), comm)


def _grad_ada(c_act_t, dmod, name):
    L, _, n = dmod.shape

    def body(c_ref, d_ref, o_ref):
        acc = None
        for b in range(SUBLANES):
            part = c_ref[:, b:b + 1] * d_ref[b:b + 1, :]
            acc = part if acc is None else acc + part
        o_ref[...] = acc

    return pl.pallas_call(
        body, name=name, grid=(L,),
        in_specs=[pl.BlockSpec((D, SUBLANES), lambda l: (0, 0)), pl.BlockSpec((None, SUBLANES, n), lambda l: (l, 0, 0))],
        out_specs=pl.BlockSpec((None, D, n), lambda l: (l, 0, 0)),
        out_shape=jax.ShapeDtypeStruct((L, D, n), F32),
        compiler_params=_params(1, 32),
    )(c_act_t, dmod)


def _stats_reduce(g3, loss_row, name):
    n_dev, rows, _ = g3.shape

    def body(g_ref, o_ref, l_ref):
        acc = g_ref[0]
        for d in range(1, n_dev):
            acc = acc + g_ref[d]
        o_ref[...] = acc
        tot = jnp.sum(o_ref[loss_row:loss_row + 1, :], axis=-1, keepdims=True)
        l_ref[...] = jnp.broadcast_to(tot * (0.5 / D), (SUBLANES, LANES))

    return pl.pallas_call(
        body, name=name,
        in_specs=[pl.BlockSpec(memory_space=pltpu.VMEM)],
        out_specs=[pl.BlockSpec(memory_space=pltpu.VMEM), pl.BlockSpec(memory_space=pltpu.VMEM)],
        out_shape=[jax.ShapeDtypeStruct((rows, D), F32), jax.ShapeDtypeStruct((SUBLANES, LANES), F32)],
        compiler_params=pltpu.CompilerParams(vmem_limit_bytes=32 * 2 ** 20),
    )(g3)


def _sum4(own, land, chip, name, tr=256):
    _, R, C = own.shape
    tr = min(tr, R)

    def body(p_ref, own_ref, land_ref, o_ref):
        o_ref[...] = ((own_ref[...].astype(F32) + land_ref[0].astype(F32)) + land_ref[1].astype(F32)) + land_ref[2].astype(F32)

    out = pl.pallas_call(
        body, name=name,
        grid_spec=pltpu.PrefetchScalarGridSpec(
            num_scalar_prefetch=1, grid=(R // tr,),
            in_specs=[pl.BlockSpec((None, tr, C), lambda i, p: (p[0], i, 0)), pl.BlockSpec((3, tr, C), lambda i, p: (0, i, 0))],
            out_specs=pl.BlockSpec((tr, C), lambda i, p: (i, 0))),
        out_shape=jax.ShapeDtypeStruct((R, C), F32),
        compiler_params=_params(1, 32),
    )(chip, pltpu.with_memory_space_constraint(own, pltpu.HBM), pltpu.with_memory_space_constraint(land, pltpu.HBM))
    return pltpu.with_memory_space_constraint(out, pltpu.HBM)


def _adamw(w, m, v, g, name, tr=256):
    L, R, C = w.shape
    tr = min(tr, R)
    stacked = not isinstance(g, (list, tuple))
    n_g = None if stacked else [len(ps) for ps in g]
    flat = [g] if stacked else [a for ps in g for a in ps]

    def body(*refs):
        w_ref, m_ref, v_ref = refs[:3]
        g_refs = refs[3:3 + len(flat)]
        go_ref, d_ref, mo_ref, vo_ref = refs[3 + len(flat):]
        if stacked:
            gg = g_refs[0][...]
        else:
            layer = pl.program_id(0)
            gg = None
            k = 0
            for li in range(L):
                gl = None
                for _ in range(n_g[li]):
                    x = g_refs[k][...]
                    gl = x if gl is None else gl + x
                    k += 1
                gg = gl if gg is None else jnp.where(layer == li, gl, gg)
        m2 = ADAM_B1 * m_ref[...] + (1.0 - ADAM_B1) * gg
        v2 = ADAM_B2 * v_ref[...] + (1.0 - ADAM_B2) * (gg * gg)
        m_hat = m2 / (1.0 - ADAM_B1 ** ADAM_STEP)
        v_hat = v2 / (1.0 - ADAM_B2 ** ADAM_STEP)
        go_ref[...] = gg
        d_ref[...] = -ADAM_LR * (m_hat / (jnp.sqrt(v_hat) + ADAM_EPS) + ADAM_WD * w_ref[...])
        mo_ref[...] = m2
        vo_ref[...] = v2

    big = pl.BlockSpec((None, tr, C), lambda l, i: (l, i, 0))
    g_specs = [big] if stacked else [pl.BlockSpec((tr, C), lambda l, i: (i, 0))] * len(flat)
    return pl.pallas_call(
        body, name=name, grid=(L, R // tr),
        in_specs=[big, big, big] + g_specs,
        out_specs=[big, big, big, big],
        out_shape=[jax.ShapeDtypeStruct((L, R, C), F32)] * 4,
        compiler_params=_params(2, 48),
    )(w, m, v, *flat)


GATHER8_SEMS = [pltpu.SemaphoreType.DMA((7,)), pltpu.SemaphoreType.DMA((7,)), pltpu.SemaphoreType.DMA]


def _gather8(x_ref, out_ref, send_sems, recv_sems, local_sem):
    m = x_ref.shape[0]
    x, y, c = _place()
    me, sibling = (x, y, c), (x, y, 1 - c)
    chips = [(1 - x, y), (x, 1 - y), (1 - x, 1 - y)]

    def rows(px, py, pc):
        return out_ref.at[pl.ds((4 * px + 2 * py + pc) * m, m), :]

    def copy(k, block, to, src=None):
        return pltpu.make_async_remote_copy(
            src_ref=rows(*block) if src is None else src, dst_ref=rows(*block),
            send_sem=send_sems.at[k], recv_sem=recv_sems.at[k], device_id=to, device_id_type=MESH)

    mine = pltpu.make_async_copy(x_ref, rows(*me), local_sem)
    mine.start()
    first = [copy(0, me, sibling, src=x_ref)]
    first += [copy(1 + j, me, (*chip, c), src=x_ref) for j, chip in enumerate(chips)]
    for cp in first:
        cp.start()
    passed = [copy(4 + j, (*chip, c), sibling) for j, chip in enumerate(chips)]
    for j, chip in enumerate(chips):
        copy(1 + j, (*chip, c), me).wait_recv()
        passed[j].start()
    copy(0, sibling, me).wait_recv()
    for j, chip in enumerate(chips):
        copy(4 + j, (*chip, 1 - c), me).wait_recv()
    for cp in first + passed:
        cp.wait_send()
    mine.wait()


def _allgather8(xs, name, comm=None):
    m, n = xs.shape
    n_c = 0 if comm is None else comm.n

    def body(*refs):
        x_ref, out_ref = refs[0], refs[1 + n_c]
        c_refs = (refs[1:1 + n_c], refs[2 + n_c:2 + 2 * n_c]) + tuple(refs[5 + 2 * n_c:])
        if comm is not None:
            comm.start(*c_refs)
        _gather8(x_ref, out_ref, *refs[2 + 2 * n_c:5 + 2 * n_c])
        if comm is not None:
            comm.wait(*c_refs)

    vmem = pl.BlockSpec(memory_space=pltpu.VMEM)
    outs = pl.pallas_call(
        body, name=name,
        out_shape=[jax.ShapeDtypeStruct((8 * m, n), xs.dtype)] + ([] if comm is None else comm.out_shape),
        in_specs=[vmem] + [ANY] * n_c,
        out_specs=[vmem] + [ANY] * n_c,
        scratch_shapes=GATHER8_SEMS + ([] if comm is None else comm.scratch),
        compiler_params=pltpu.CompilerParams(vmem_limit_bytes=32 * 2 ** 20),
    )(xs, *([] if comm is None else comm.arrays))
    return outs[0], list(outs[1:])


def _prologue(c8, ada_w, ada_b, kv_ada_w, kv_ada_b, extra, comm, name):
    L, _, n = ada_w.shape
    k = kv_ada_w.shape[1]
    e = extra.shape[1]
    width = L * n + k + e
    n_c = comm.n

    def body(*refs):
        c_ref, w_hbm, b_ref, kw_hbm, kb_ref, x_ref = refs[:6]
        c_in = refs[6:6 + n_c]
        ca_ref, out_ref = refs[6 + n_c:8 + n_c]
        c_out = refs[8 + n_c:8 + 2 * n_c]
        cbuf, wbuf, kbuf, part, wsems = refs[8 + 2 * n_c:13 + 2 * n_c]
        sems_a = refs[13 + 2 * n_c:16 + 2 * n_c]
        sems_b = refs[16 + 2 * n_c:19 + 2 * n_c]
        c_refs = (c_in, c_out) + tuple(refs[19 + 2 * n_c:])
        comm.start(*c_refs)

        def fetch(l):
            return pltpu.make_async_copy(w_hbm.at[l], wbuf.at[l % 2], wsems.at[l % 2])

        fetch(0).start()
        kv_copy = pltpu.make_async_copy(kw_hbm, kbuf, wsems.at[2])
        kv_copy.start()
        _gather8(c_ref, cbuf, *sems_a)
        cc = jnp.concatenate([cbuf[SUBLANES * d:SUBLANES * d + 1, :] for d in range(8)], axis=0)
        ca = cc * jax.nn.sigmoid(cc)
        ca_ref[...] = ca
        cab = ca.astype(BF16)
        for l in range(L):
            fetch(l).wait()
            if l + 1 < L:
                fetch(l + 1).start()
            part[:, l * n:(l + 1) * n] = _nn(cab, wbuf[l % 2].astype(BF16)) + b_ref[l]
        kv_copy.wait()
        part[:, L * n:L * n + k] = _nn(cab, kbuf[...].astype(BF16)) + kb_ref[...]
        part[:, L * n + k:] = jnp.broadcast_to(x_ref[...], (SUBLANES, e))
        _gather8(part, out_ref, *sems_b)
        comm.wait(*c_refs)

    vmem = pl.BlockSpec(memory_space=pltpu.VMEM)
    outs = pl.pallas_call(
        body, name=name,
        out_shape=[jax.ShapeDtypeStruct((SUBLANES, D), F32), jax.ShapeDtypeStruct((8 * SUBLANES, width), F32)] + comm.out_shape,
        in_specs=[vmem, ANY, vmem, ANY, vmem, vmem] + [ANY] * n_c,
        out_specs=[vmem, vmem] + [ANY] * n_c,
        scratch_shapes=[pltpu.VMEM((8 * SUBLANES, D), F32), pltpu.VMEM((2, D, n), F32), pltpu.VMEM((D, k), F32),
                        pltpu.VMEM((SUBLANES, width), F32), pltpu.SemaphoreType.DMA((3,))] + GATHER8_SEMS + GATHER8_SEMS
        + comm.scratch,
        compiler_params=pltpu.CompilerParams(vmem_limit_bytes=32 * 2 ** 20),
    )(c8, ada_w, ada_b, kv_ada_w, kv_ada_b, extra, *comm.arrays)
    return outs[0], outs[1], list(outs[2:])


def _pad8(a):
    return jnp.pad(a, ((0, SUBLANES - a.shape[0]), (0, 0)))


def _group_rows(wg):
    return wg.transpose(1, 0, 2, 3).reshape(4, GW, GW)


def _example_step(h0, tgt, mods, kvmod, a_scale, norm_g, kv_norm_g, final_g, b_rel_bias, sh, w_first, chip_arr):
    ones_e = jnp.ones((1, E), F32)
    shift = [mods[l:l + 1, 0:D] for l in range(4)]
    scale = [mods[l:l + 1, D:2 * D] for l in range(4)]
    gate = [mods[l:l + 1, 2 * D:3 * D] for l in range(4)]
    gl = [norm_g[l:l + 1] for l in range(4)]
    kv_shift, kv_scale = kvmod[None, 0:D], kvmod[None, D:2 * D]
    kv_g = kv_norm_g[None]

    w_a = w_first
    hs = [h0]
    saved = []
    nxt = [[sh["a_in"][1], sh["a_grp"][1], sh["a_out"][1]], [sh["kv"][0], sh["b_in"][0]]]
    for l in range(2):
        w_in_l, wg_l, wo_l = w_a
        wg_full = _group_rows(wg_l)
        (u, z, pooled, mixed, y, hn), got = _a_fwd(hs[-1], gl[l], shift[l], scale[l], a_scale[l:l + 1], gate[l], w_in_l,
                                                   wg_full, wo_l, f"a{l}_fwd", comm=_Comm(gathers=nxt[l]))
        saved.append((u, z, pooled, mixed, y, w_in_l, wg_full, wo_l))
        hs.append(hn)
        if l == 0:
            w_a = got
        else:
            w_kv, wb_in0 = got

    (uk, kp, vp), _ = _in_fwd(hs[2], kv_g, kv_shift, kv_scale, w_kv, BF16, BF16, "kv_in_fwd", pad_rows=PAD)
    wb_in = [wb_in0, None]
    wb_out = [None, None]

    for bi in range(2):
        l = 2 + bi
        sa, sb = _bias_build(jnp.pad(b_rel_bias[bi], ((0, 0), (0, NRELP - NREL))), f"b{bi}_bias")
        (u, q, z), _ = _in_fwd(hs[-1], gl[l], shift[l], scale[l], wb_in[bi], BF16, BF16, f"b{bi}_in_fwd")
        comm = _Comm(gathers=[sh["b_out"][0], sh["b_in"][1], sh["b_out"][1]]) if bi == 0 else None
        (att, probs), got = _attn_fwd(q, kp, vp, sa.transpose(1, 0, 2), sb.transpose(1, 0, 2), f"b{bi}_attn_fwd", comm=comm)
        if bi == 0:
            wb_out[0], wb_in[1], wb_out[1] = got
        if bi == 0:
            (y, hn), _ = _out_fwd(att, z, wb_out[bi], gate[l], hs[-1], f"b{bi}_out_fwd")
            hs.append(hn)
        else:
            (y, dh, st_fin), _ = _out_fwd(att, z, wb_out[bi], gate[l], hs[-1], f"b{bi}_out_fwd", head=(final_g[None], tgt))
        saved.append((u, z, q, att, y, probs))

    st_in = [None] * 4
    st_out = [None] * 4
    grads = {}
    landed = {}

    def carry(names):
        return _Comm(scatters=[grads[n] for n in names]) if names else None

    def land(names, got):
        for n, a in zip(names, got):
            landed[n] = a

    u, z, q, att, y, probs = saved[3]
    (datt, dz, grads["b_out1"], st_out[3]), _ = _out_bwd(dh, y, gate[3], att, ones_e, z, wb_out[1], "b1_out_bwd")
    (dq, dk1, dv1, dsa, dsb), _ = _attn_bwd(q, kp, vp, probs, datt, None, "b1_attn_bwd")
    drb1 = _dbias_reduce(dsa.transpose(1, 0, 2), dsb.transpose(1, 0, 2), "b1_dbias")
    (dh, grads["b_in1"], st_in[3]), _ = _in_bwd(dq, dz, 0, u, hs[3], gl[3], scale[3], wb_in[1], dh, "b1_in_bwd")
    u, z, q, att, y, probs = saved[2]
    (datt, dz, grads["b_out0"], st_out[2]), _ = _out_bwd(dh, y, gate[2], att, ones_e, z, wb_out[0], "b0_out_bwd")
    (dq, dk, dv, dsa, dsb), got = _attn_bwd(q, kp, vp, probs, datt, (dk1, dv1), "b0_attn_bwd",
                                            comm=carry(["b_out1", "b_in1", "b_out0"]))
    land(["b_out1", "b_in1", "b_out0"], got)
    drb0 = _dbias_reduce(dsa.transpose(1, 0, 2), dsb.transpose(1, 0, 2), "b0_dbias")
    (dh, grads["b_in0"], st_in[2]), _ = _in_bwd(dq, dz, 0, u, hs[2], gl[2], scale[2], wb_in[0], dh, "b0_in_bwd")
    (dh, grads["kv"], st_kv), got = _in_bwd(dk, dv, PAD, uk, hs[2], kv_g, kv_scale, w_kv, dh, "kv_in_bwd",
                                            comm=carry(["b_in0"]))
    land(["b_in0"], got)
    st_pool = [None] * 2
    plan = {1: dict(o=[], p=[], i=["kv", "a_out1", "a_grp1"]), 0: dict(o=["a_in1"], p=["a_out0"], i=[])}
    early = ["b_out1", "b_in1", "b_out0", "b_in0", "kv", "a_out1", "a_grp1", "a_in1"]
    late = ["a_out0", "a_grp0", "a_in0"]
    both = {}

    def sum4(n):
        return _sum4(grads[n], landed[n], chip_arr, f"sum4_{n}")

    for l in (1, 0):
        u, z, pooled, mixed, y, w_in_l, wg_full, wo = saved[l]
        asl = a_scale[l:l + 1]
        (dms, dz, grads[f"a_out{l}"], st_out[l]), got = _out_bwd(dh, y, gate[l], mixed, asl, z, wo, f"a{l}_out_bwd",
                                                                comm=carry(plan[l]["o"]))
        land(plan[l]["o"], got)
        comm = carry(plan[l]["p"])
        if l == 0:
            mine = [sum4(n) for n in early]
            comm = _Comm(scatters=[grads[n] for n in plan[l]["p"]], swaps=mine)
        (dval, dwg, st_pool[l]), got = _pool_bwd(dms, mixed, pooled, wg_full, asl, f"a{l}_pool_bwd", comm=comm)
        land(plan[l]["p"], got)
        if l == 0:
            both.update({n: [a, b] for n, a, b in zip(early, mine, got[len(plan[l]["p"]):])})
        grads[f"a_grp{l}"] = (dwg.reshape(4, NCHIP, GW // NCHIP, GW).transpose(1, 0, 2, 3).reshape(NCHIP, GW, GW)
                              .astype(BF16))
        (dh, grads[f"a_in{l}"], st_in[l]), got = _in_bwd(dval, dz, 0, u, hs[l], gl[l], scale[l], w_in_l, dh, f"a{l}_in_bwd",
                                                         comm=carry(plan[l]["i"]))
        land(plan[l]["i"], got)
    pieces = st_in + [st_kv] + st_out + [st_fin]
    pieces += [_pad8(st_pool[l][0].reshape(2, D)) for l in range(2)]
    pieces += [_pad8(d.reshape(NH * NRELP // D, D)) for d in (drb0, drb1)]
    gathered, got = _allgather8(jnp.concatenate(pieces, axis=0), "gather_stats", comm=carry(["a_grp0", "a_in0"]))
    land(["a_grp0", "a_in0"], got)
    mine = [sum4(n) for n in late]
    both.update({n: [a, b] for n, a, b in zip(late, mine, _comm_only(_Comm(swaps=mine), "swap_last"))})
    return dh, both, gathered.reshape(8, N_STAT, D)


ROW_IN = [8 * l for l in range(4)]
ROW_KV = 32
ROW_OUT = [40 + 8 * l for l in range(4)]
ROW_FIN = 72
ROW_ASC = [80, 88]
ROW_RB = [96, 104]
N_STAT = 112


def kernel(x, c, ada_w, ada_b, norm_g, a_w_in, a_w_group, a_scale, a_w_out, kv_norm_g, kv_ada_w, kv_ada_b, w_kv, b_w_in, b_rel_bias, b_w_out, final_g, loss_target, m_ada_w, m_ada_b, m_norm_g, m_a_w_in, m_a_w_group, m_a_scale, m_a_w_out, m_kv_norm_g, m_kv_ada_w, m_kv_ada_b, m_w_kv, m_b_w_in, m_b_rel_bias, m_b_w_out, m_final_g, v_ada_w, v_ada_b, v_norm_g, v_a_w_in, v_a_w_group, v_a_scale, v_a_w_out, v_kv_norm_g, v_kv_ada_w, v_kv_ada_b, v_w_kv, v_b_w_in, v_b_rel_bias, v_b_w_out, v_final_g):
    xi, yi, ci = _place()
    chip = 2 * xi + yi
    dev = 4 * xi + 2 * yi + ci
    n_ada = ada_w.shape[2]
    n_kva = kv_ada_w.shape[1]
    n_asc = a_scale.shape[1]

    ada_b_sh = lax.dynamic_slice_in_dim(ada_b, chip * n_ada, n_ada, axis=1)
    kvb_sh = lax.dynamic_slice_in_dim(kv_ada_b, chip * n_kva, n_kva, axis=0)
    sh = dict(a_in=[a_w_in[l].astype(BF16) for l in range(2)], a_grp=[a_w_group[l].astype(BF16) for l in range(2)],
              a_out=[a_w_out[l].astype(BF16) for l in range(2)], kv=[w_kv.astype(BF16)],
              b_in=[b_w_in[l].astype(BF16) for l in range(2)], b_out=[b_w_out[l].astype(BF16) for l in range(2)])
    c_act, gathered, w_first = _prologue(
        jnp.broadcast_to(c, (SUBLANES, D)), ada_w, ada_b_sh[:, None, :], kv_ada_w, kvb_sh[None, :],
        a_scale.reshape(1, 2 * n_asc), _Comm(gathers=[sh["a_in"][0], sh["a_grp"][0], sh["a_out"][0]]), "prologue")
    rows = jnp.concatenate([lax.dynamic_slice_in_dim(gathered, SUBLANES * (2 * p + ci) + dev, 1, axis=0)
                            for p in range(NCHIP)], axis=0)
    mods = jnp.stack([rows[:, l * n_ada:(l + 1) * n_ada].reshape(3 * D) for l in range(4)])
    kvmod = rows[:, 4 * n_ada:4 * n_ada + n_kva].reshape(2 * D)
    o_asc = 4 * n_ada + n_kva
    a_scale_full = jnp.stack([rows[:, o_asc + l * n_asc:o_asc + (l + 1) * n_asc].reshape(E) for l in range(2)])

    chip_arr = jnp.reshape(chip, (1,)).astype(jnp.int32)
    dh, both, g3 = _example_step(x[0], loss_target[0], mods, kvmod, a_scale_full, norm_g, kv_norm_g, final_g,
                                 b_rel_bias, sh, w_first, chip_arr)
    grad_x = dh[None]

    red, loss_tile = _stats_reduce(g3, ROW_FIN + 1, "stats_reduce")
    loss = loss_tile[0, 0]

    def cat(rows_):
        return jnp.concatenate(rows_, axis=-1)

    g_ada_b = jnp.stack([cat([red[ROW_IN[l]], red[ROW_IN[l] + 1], red[ROW_OUT[l]]]) for l in range(4)])
    g_norm_g = jnp.stack([red[ROW_IN[l] + 2] for l in range(4)])
    g_kv_norm_g = red[ROW_KV + 2]
    g_kv_ada_b = cat([red[ROW_KV], red[ROW_KV + 1]])
    g_final_g = red[ROW_FIN]
    g_asc_full = jnp.stack([red[ROW_ASC[l]:ROW_ASC[l] + 2].reshape(E) for l in range(2)])
    g_a_scale = lax.dynamic_slice_in_dim(g_asc_full, chip * n_asc, n_asc, axis=1)
    g_rel = jnp.stack([red[ROW_RB[bi]:ROW_RB[bi] + NH * NRELP // D].reshape(NH, NRELP)[:, :NREL] for bi in range(2)])

    dmod = jnp.stack([cat([g3[:, ROW_IN[l]], g3[:, ROW_IN[l] + 1], g3[:, ROW_OUT[l]]]) for l in range(4)])
    dmod_sh = lax.dynamic_slice_in_dim(dmod, chip * n_ada, n_ada, axis=2)
    dkv = cat([g3[:, ROW_KV], g3[:, ROW_KV + 1]])[None]
    dkv_sh = lax.dynamic_slice_in_dim(dkv, chip * n_kva, n_kva, axis=2)
    c_act_t = c_act.T
    g_ada_w = _grad_ada(c_act_t, dmod_sh, "grad_ada_w")
    g_kv_ada_w = _grad_ada(c_act_t, dkv_sh, "grad_kv_ada_w")

    def upd(w, m, v, g, name, shape3):
        g = g.reshape(shape3) if not isinstance(g, list) else g
        outs = _adamw(w.reshape(shape3), m.reshape(shape3), v.reshape(shape3), g, name)
        return [o.reshape(w.shape) for o in outs]

    def pair(name):
        return [both[name + "0"], both[name + "1"]]

    res = {}
    res["ada_w"] = upd(ada_w, m_ada_w, v_ada_w, g_ada_w, "adamw_ada_w", ada_w.shape)
    res["ada_b"] = upd(ada_b, m_ada_b, v_ada_b, g_ada_b, "adamw_ada_b", (1,) + ada_b.shape)
    res["norm_g"] = upd(norm_g, m_norm_g, v_norm_g, g_norm_g, "adamw_norm_g", (1,) + norm_g.shape)
    res["a_w_in"] = upd(a_w_in, m_a_w_in, v_a_w_in, pair("a_in"), "adamw_a_w_in", a_w_in.shape)
    res["a_w_group"] = upd(a_w_group, m_a_w_group, v_a_w_group, pair("a_grp"), "adamw_a_w_group", (2, GW, GW))
    res["a_scale"] = upd(a_scale, m_a_scale, v_a_scale, g_a_scale, "adamw_a_scale", (1,) + a_scale.shape)
    res["a_w_out"] = upd(a_w_out, m_a_w_out, v_a_w_out, pair("a_out"), "adamw_a_w_out", a_w_out.shape)
    res["kv_norm_g"] = upd(kv_norm_g, m_kv_norm_g, v_kv_norm_g, g_kv_norm_g, "adamw_kv_norm_g", (1, 1, D))
    res["kv_ada_w"] = upd(kv_ada_w, m_kv_ada_w, v_kv_ada_w, g_kv_ada_w, "adamw_kv_ada_w", (1,) + kv_ada_w.shape)
    res["kv_ada_b"] = upd(kv_ada_b, m_kv_ada_b, v_kv_ada_b, g_kv_ada_b, "adamw_kv_ada_b", (1, 1, 2 * D))
    res["w_kv"] = upd(w_kv, m_w_kv, v_w_kv, [both["kv"]], "adamw_w_kv", (1,) + w_kv.shape)
    res["b_w_in"] = upd(b_w_in, m_b_w_in, v_b_w_in, pair("b_in"), "adamw_b_w_in", b_w_in.shape)
    res["b_rel_bias"] = upd(b_rel_bias, m_b_rel_bias, v_b_rel_bias, g_rel, "adamw_b_rel_bias", (1, 2 * NH, NREL))
    res["b_w_out"] = upd(b_w_out, m_b_w_out, v_b_w_out, pair("b_out"), "adamw_b_w_out", b_w_out.shape)
    res["final_g"] = upd(final_g, m_final_g, v_final_g, g_final_g, "adamw_final_g", (1, 1, D))

    names = ["ada_w", "ada_b", "norm_g", "a_w_in", "a_w_group", "a_scale", "a_w_out", "kv_norm_g", "kv_ada_w", "kv_ada_b",
             "w_kv", "b_w_in", "b_rel_bias", "b_w_out", "final_g"]
    return (loss, grad_x, *[res[n][0] for n in names], *[res[n][1] for n in names], *[res[n][2] for n in names],
            *[res[n][3] for n in names])
```

```python
import math

import jax
import jax.numpy as jnp
from jax import lax
from jax.experimental import pallas as pl
from jax.experimental.pallas import tpu as pltpu

F32 = jnp.float32
BF16 = jnp.bfloat16

D = 1024
E = 2048
NH = 16
HD = 128
CHUNK = 64
LEFT = 8
PAD = LEFT * CHUNK
NREL = 257
NRELP = 384
REL_CLIP = 128
EPS = 1e-6
NEG = -1e30
LOG2E = math.log2(math.e)
SM_SCALE = HD ** -0.5
POOL_W = (2, 4, 8, 16)
GW = 512
HALO = 16
QC = 4
QB = QC * CHUNK
NMASK = PAD // QB
WIN = (QC + LEFT) * CHUNK
BW = (LEFT + 2) * CHUNK
DBW = 4 * CHUNK
NSUB_FWD = 32
NSUB_BWD = 16
NCHIP = 4
LANES = 128
SUBLANES = 8

ADAM_LR = 0.001
ADAM_B1 = 0.9
ADAM_B2 = 0.999
ADAM_EPS = 1e-08
ADAM_WD = 0.01
ADAM_STEP = 10

MESH = pl.DeviceIdType.MESH
ANY = pl.BlockSpec(memory_space=pl.ANY)


def _params(n_axes, vmem_mb):
    return pltpu.CompilerParams(dimension_semantics=("arbitrary",) * n_axes, vmem_limit_bytes=vmem_mb * 2 ** 20)


def _nn(a, b):
    return jnp.dot(a, b, preferred_element_type=F32)


def _nt(a, b):
    return lax.dot_general(a, b, (((1,), (1,)), ((), ())), preferred_element_type=F32)


def _tn(a, b):
    return lax.dot_general(a, b, (((0,), (0,)), ((), ())), preferred_element_type=F32)


def _row(n):
    return pl.BlockSpec((1, n), lambda i: (0, 0))


def _colsum(x):
    return jnp.sum(x, axis=0, keepdims=True)


def _place():
    return lax.axis_index("x"), lax.axis_index("y"), lax.axis_index("c")


class _Comm:
    def __init__(self, gathers=(), scatters=(), swaps=()):
        self.n_g = len(gathers)
        self.n_chip = len(gathers) + len(scatters)
        self.n_sw = len(swaps)
        self.arrays = list(gathers) + list(scatters) + list(swaps)
        self.n = len(self.arrays)
        self.half = [a.shape[0] // 2 for a in gathers]
        self.out_shape = ([jax.ShapeDtypeStruct((NCHIP,) + a.shape, a.dtype) for a in gathers]
                          + [jax.ShapeDtypeStruct((3,) + a.shape[1:], a.dtype) for a in scatters]
                          + [jax.ShapeDtypeStruct(a.shape, a.dtype) for a in swaps])
        n_c, n_f, n_s = max(3 * self.n_chip, 1), max(3 * self.n_g, 1), max(self.n_sw, 1)
        self.scratch = [pltpu.SemaphoreType.DMA((n_c,)), pltpu.SemaphoreType.DMA((n_c,)),
                        pltpu.SemaphoreType.DMA((max(self.n_g, 1),)), pltpu.SemaphoreType.DMA((n_f,)),
                        pltpu.SemaphoreType.DMA((n_f,)), pltpu.SemaphoreType.DMA((n_s,)), pltpu.SemaphoreType.DMA((n_s,))]

    def _chip_copies(self, ins, outs, send, recv, landing):
        x, y, c = _place()
        chips = [(1 - x, y), (x, 1 - y), (1 - x, 1 - y)]
        mine = 2 * x + y
        cps = []
        for k in range(self.n_chip):
            for j, (cx, cy) in enumerate(chips):
                q = 2 * cx + cy
                if k < self.n_g:
                    part = pl.ds(c * self.half[k], self.half[k])
                    src = ins[k].at[part]
                    dst = outs[k].at[q if landing else mine, part]
                else:
                    src = ins[k].at[q]
                    dst = outs[k].at[j]
                cps.append(pltpu.make_async_remote_copy(
                    src_ref=src, dst_ref=dst, send_sem=send.at[3 * k + j], recv_sem=recv.at[3 * k + j],
                    device_id=(cx, cy, c), device_id_type=MESH))
        return cps

    def _core_copies(self, outs, fsend, frecv, landing):
        x, y, c = _place()
        chips = [(1 - x, y), (x, 1 - y), (1 - x, 1 - y)]
        cps = []
        for k in range(self.n_g):
            for j, (cx, cy) in enumerate(chips):
                part = pl.ds((1 - c if landing else c) * self.half[k], self.half[k])
                blk = outs[k].at[2 * cx + cy, part]
                cps.append(pltpu.make_async_remote_copy(
                    src_ref=blk, dst_ref=blk, send_sem=fsend.at[3 * k + j], recv_sem=frecv.at[3 * k + j],
                    device_id=(x, y, 1 - c), device_id_type=MESH))
        return cps

    def _local_copies(self, ins, outs, loc):
        x, y, _ = _place()
        return [pltpu.make_async_copy(ins[k], outs[k].at[2 * x + y], loc.at[k]) for k in range(self.n_g)]

    def _swap_copies(self, ins, outs, ssend, srecv):
        x, y, c = _place()
        return [pltpu.make_async_remote_copy(
            src_ref=ins[k], dst_ref=outs[k], send_sem=ssend.at[k - self.n_chip], recv_sem=srecv.at[k - self.n_chip],
            device_id=(x, y, 1 - c), device_id_type=MESH) for k in range(self.n_chip, self.n)]

    def start(self, ins, outs, send, recv, loc, fsend, frecv, ssend, srecv):
        for cp in (self._local_copies(ins, outs, loc) + self._chip_copies(ins, outs, send, recv, False)
                   + self._swap_copies(ins, outs, ssend, srecv)):
            cp.start()

    def wait(self, ins, outs, send, recv, loc, fsend, frecv, ssend, srecv):
        lands = self._chip_copies(ins, outs, send, recv, True)
        passes = self._core_copies(outs, fsend, frecv, False)
        for k in range(self.n_chip):
            for j in range(3):
                lands[3 * k + j].wait_recv()
                if k < self.n_g:
                    passes[3 * k + j].start()
        for cp in self._core_copies(outs, fsend, frecv, True):
            cp.wait_recv()
        swaps = self._swap_copies(ins, outs, ssend, srecv)
        for cp in swaps:
            cp.wait_recv()
        for cp in self._chip_copies(ins, outs, send, recv, False) + passes + swaps:
            cp.wait_send()
        for cp in self._local_copies(ins, outs, loc):
            cp.wait()


def _call(body, name, grid, in_specs, out_specs, out_shape, scratch, params, args, comm=None):
    n_in, n_out, n_sc = len(in_specs), len(out_specs), len(scratch)
    if comm is None:
        outs = pl.pallas_call(body, name=name, grid=grid, in_specs=in_specs, out_specs=out_specs, out_shape=out_shape,
                              scratch_shapes=scratch, compiler_params=params)(*args)
        return list(outs), []
    n = comm.n
    o0 = n_in + n
    s0 = o0 + n_out + n

    def wrapped(*refs):
        c_refs = (refs[n_in:o0], refs[o0 + n_out:s0]) + tuple(refs[s0 + n_sc:])
        ids = [pl.program_id(a) for a in range(len(grid))]
        first = ids[0] == 0
        last = ids[0] == grid[0] - 1
        for a in range(1, len(grid)):
            first = first & (ids[a] == 0)
            last = last & (ids[a] == grid[a] - 1)

        @pl.when(first)
        def _():
            comm.start(*c_refs)

        body(*refs[:n_in], *refs[o0:o0 + n_out], *refs[s0:s0 + n_sc])

        @pl.when(last)
        def _():
            comm.wait(*c_refs)

    outs = pl.pallas_call(
        wrapped, name=name, grid=grid, in_specs=list(in_specs) + [ANY] * n, out_specs=list(out_specs) + [ANY] * n,
        out_shape=list(out_shape) + comm.out_shape, scratch_shapes=list(scratch) + comm.scratch, compiler_params=params,
    )(*args, *comm.arrays)
    return list(outs[:n_out]), list(outs[n_out:])


def _comm_only(comm, name):
    def body(*refs):
        c_refs = (refs[:comm.n], refs[comm.n:2 * comm.n]) + tuple(refs[2 * comm.n:])
        comm.start(*c_refs)
        comm.wait(*c_refs)

    return pl.pallas_call(body, name=name, in_specs=[ANY] * comm.n, out_specs=[ANY] * comm.n, out_shape=comm.out_shape,
                          scratch_shapes=comm.scratch)(*comm.arrays)


def _in_fwd(h, g, shift, scale, w, dt_a, dt_b, name, pad_rows=0, comm=None, tm=512):
    S = h.shape[0]
    n_pad = pad_rows // tm

    def body(h_ref, g_ref, sh_ref, sc_ref, w_hbm, u_ref, oa_ref, ob_ref, w_v, sem):
        i = pl.program_id(0)

        @pl.when(i == 0)
        def _():
            cp = pltpu.make_async_copy(w_hbm, w_v, sem)
            cp.start()
            cp.wait()

        hh = h_ref[...]
        r = lax.rsqrt(jnp.mean(hh * hh, axis=-1, keepdims=True) + EPS)
        u = (hh * r * g_ref[...]) * (1.0 + sc_ref[...]) + sh_ref[...]
        ub = u.astype(BF16)
        u_ref[...] = ub
        for q in range(NCHIP):
            o_ref = oa_ref if q < 2 else ob_ref
            o_ref[:, (q % 2) * D:(q % 2 + 1) * D] = _nn(ub, w_v[q]).astype(o_ref.dtype)

        if n_pad:
            @pl.when(i < n_pad)
            def _():
                oa_ref[...] = jnp.zeros(oa_ref.shape, oa_ref.dtype)
                ob_ref[...] = jnp.zeros(ob_ref.shape, ob_ref.dtype)

    def src(i):
        return (jnp.maximum(i - n_pad, 0), 0)

    outs, landed = _call(
        body, name, (S // tm + n_pad,),
        [pl.BlockSpec((tm, D), src), _row(D), _row(D), _row(D), ANY],
        [pl.BlockSpec((tm, D), src), pl.BlockSpec((tm, E), lambda i: (i, 0)), pl.BlockSpec((tm, E), lambda i: (i, 0))],
        [jax.ShapeDtypeStruct((S, D), BF16), jax.ShapeDtypeStruct((S + pad_rows, E), dt_a),
         jax.ShapeDtypeStruct((S + pad_rows, E), dt_b)],
        [pltpu.VMEM((NCHIP, D, D), BF16), pltpu.SemaphoreType.DMA],
        _params(1, 52), (h, g, shift, scale, w), comm)
    return outs, landed


def _a_fwd(h, g, shift, scale, asc, gate, w_in, wg, w_out, name, comm=None, tm=512):
    S = h.shape[0]

    def body(h_ref, g_ref, sh_ref, sc_ref, as_ref, gate_ref, wi_hbm, wg_hbm, wo_hbm,
             u_ref, z_ref, p_ref, m_ref, y_ref, ho_ref, wi_v, wg_v, wo_v, buf, sems):
        i = pl.program_id(0)

        @pl.when(i == 0)
        def _():
            cps = [pltpu.make_async_copy(wi_hbm, wi_v, sems.at[0]), pltpu.make_async_copy(wg_hbm, wg_v, sems.at[1]),
                   pltpu.make_async_copy(wo_hbm, wo_v, sems.at[2])]
            for cp in cps:
                cp.start()
            buf[0:HALO, :] = jnp.zeros((HALO, E), F32)
            for cp in cps:
                cp.wait()

        hh = h_ref[...]
        r = lax.rsqrt(jnp.mean(hh * hh, axis=-1, keepdims=True) + EPS)
        ub = ((hh * r * g_ref[...]) * (1.0 + sc_ref[...]) + sh_ref[...]).astype(BF16)
        u_ref[...] = ub
        for q in range(2):
            buf[HALO:HALO + tm, q * D:(q + 1) * D] = _nn(ub, wi_v[q])
        t = i * tm + lax.broadcasted_iota(jnp.int32, (tm, 1), 0)
        y = None
        for gi, w in enumerate(POOL_W):
            cols = slice(gi * GW, (gi + 1) * GW)
            x = buf[:, cols]
            s = x
            k = 1
            while k < w:
                s = s + pltpu.roll(s, k, 0)
                k *= 2
            inv_cnt = 1.0 / jnp.minimum(t + 1, w).astype(F32)
            pb = (s[HALO:, :] * inv_cnt - x[HALO:, :]).astype(BF16)
            p_ref[:, cols] = pb
            mb = _nn(pb, wg_v[gi]).astype(BF16)
            m_ref[:, cols] = mb
            zb = _nn(ub, wi_v[2 + gi // 2, :, (gi % 2) * GW:(gi % 2 + 1) * GW]).astype(BF16)
            z_ref[:, cols] = zb
            zz = zb.astype(F32)
            act = ((mb.astype(F32) * as_ref[:, cols]) * (zz * jax.nn.sigmoid(zz))).astype(BF16)
            part = _nn(act, wo_v[gi])
            y = part if y is None else y + part
        buf[0:HALO, :] = buf[tm:tm + HALO, :]
        y_ref[...] = y.astype(BF16)
        ho_ref[...] = hh + gate_ref[...] * y

    rows_d = pl.BlockSpec((tm, D), lambda i: (i, 0))
    rows_e = pl.BlockSpec((tm, E), lambda i: (i, 0))
    return _call(
        body, name, (S // tm,),
        [rows_d, _row(D), _row(D), _row(D), _row(E), _row(D), ANY, ANY, ANY],
        [rows_d, rows_e, rows_e, rows_e, rows_d, rows_d],
        [jax.ShapeDtypeStruct((S, D), BF16), jax.ShapeDtypeStruct((S, E), BF16), jax.ShapeDtypeStruct((S, E), BF16),
         jax.ShapeDtypeStruct((S, E), BF16), jax.ShapeDtypeStruct((S, D), BF16), jax.ShapeDtypeStruct((S, D), F32)],
        [pltpu.VMEM((NCHIP, D, D), BF16), pltpu.VMEM((4, GW, GW), BF16), pltpu.VMEM((NCHIP, GW, D), BF16),
         pltpu.VMEM((tm + HALO, E), F32), pltpu.SemaphoreType.DMA((3,))],
        _params(1, 60), (h, g, shift, scale, asc, gate, w_in, wg, w_out), comm)


def _out_fwd(a, z, w, gate, h, name, head=None, comm=None, tm=512):
    S = h.shape[0]
    kb = E // NCHIP
    n_in = 5 if head is None else 7

    def body(*refs):
        a_ref, z_ref, w_hbm, gate_ref, h_ref = refs[:5]
        w_v, sem = refs[-2:]
        i = pl.program_id(0)

        @pl.when(i == 0)
        def _():
            cp = pltpu.make_async_copy(w_hbm, w_v, sem)
            cp.start()
            cp.wait()

        y = None
        for p in range(NCHIP):
            cols = slice(p * kb, (p + 1) * kb)
            zz = z_ref[:, cols].astype(F32)
            act = (a_ref[:, cols].astype(F32) * (zz * jax.nn.sigmoid(zz))).astype(BF16)
            part = _nn(act, w_v[p])
            y = part if y is None else y + part
        refs[n_in][...] = y.astype(BF16)
        hh = h_ref[...] + gate_ref[...] * y
        if head is None:
            refs[n_in + 1][...] = hh
            return
        g_ref, t_ref = refs[5:7]
        dh_ref, st_ref = refs[n_in + 1:n_in + 3]

        @pl.when(i == 0)
        def _():
            st_ref[...] = jnp.zeros((SUBLANES, D), F32)

        r = lax.rsqrt(jnp.mean(hh * hh, axis=-1, keepdims=True) + EPS)
        xhat = hh * r
        diff = xhat * g_ref[...] - t_ref[...]
        st_ref[1:2, :] += _colsum(diff * diff)
        dout = diff * (1.0 / D)
        st_ref[0:1, :] += _colsum(dout * xhat)
        dx = dout * g_ref[...]
        dh_ref[...] = r * (dx - xhat * jnp.mean(dx * xhat, axis=-1, keepdims=True))

    rows_d = pl.BlockSpec((tm, D), lambda i: (i, 0))
    rows_e = pl.BlockSpec((tm, E), lambda i: (i, 0))
    in_specs = [rows_e, rows_e, ANY, _row(D), rows_d]
    out_specs = [rows_d, rows_d]
    out_shape = [jax.ShapeDtypeStruct((S, D), BF16), jax.ShapeDtypeStruct((S, D), F32)]
    args = (a, z, w, gate, h)
    if head is not None:
        in_specs += [_row(D), rows_d]
        out_specs += [pl.BlockSpec((SUBLANES, D), lambda i: (0, 0))]
        out_shape += [jax.ShapeDtypeStruct((SUBLANES, D), F32)]
        args += tuple(head)
    return _call(body, name, (S // tm,), in_specs, out_specs, out_shape,
                 [pltpu.VMEM((NCHIP, kb, D), BF16), pltpu.SemaphoreType.DMA], _params(1, 52), args, comm)


TW = BW + LANES


def _diag_onehot(transpose):
    shape = (TW, NRELP) if transpose else (NRELP, TW)
    j = lax.broadcasted_iota(jnp.int32, shape, 0 if transpose else 1)
    r = lax.broadcasted_iota(jnp.int32, shape, 1 if transpose else 0)
    idx = jnp.clip(PAD - (j - LANES), -REL_CLIP, REL_CLIP) + REL_CLIP
    return jnp.where(idx == r, 1.0, 0.0).astype(BF16)


def _strip_valid():
    m = lax.broadcasted_iota(jnp.int32, (NH, BW), 1)
    return m < (LEFT + 1) * CHUNK, m >= CHUNK


def _bias_build(rb, name):
    def body(rb_ref, a_ref, b_ref):
        x = rb_ref[...]
        hi = x.astype(BF16)
        r1 = x - hi.astype(F32)
        mid = r1.astype(BF16)
        lo = (r1 - mid.astype(F32)).astype(BF16)
        oh = _diag_onehot(False)
        diag = (_nn(hi, oh) + _nn(mid, oh)) + _nn(lo, oh)
        valid_a, valid_b = _strip_valid()
        for qi in range(CHUNK):
            a_ref[qi] = jnp.where(valid_a, pltpu.roll(diag, TW - (LANES - qi), 1)[:, :BW], NEG)
            b_ref[qi] = jnp.where(valid_b, pltpu.roll(diag, TW - (CHUNK - qi), 1)[:, :BW], NEG)

    vmem = pl.BlockSpec(memory_space=pltpu.VMEM)
    return pl.pallas_call(
        body, name=name, in_specs=[vmem], out_specs=[vmem, vmem],
        out_shape=[jax.ShapeDtypeStruct((CHUNK, NH, BW), F32), jax.ShapeDtypeStruct((CHUNK, NH, BW), F32)],
        compiler_params=pltpu.CompilerParams(vmem_limit_bytes=32 * 2 ** 20),
    )(rb)


def _dbias_reduce(dba, dbb, name):
    def body(a_ref, b_ref, o_ref):
        valid_a, valid_b = _strip_valid()
        zeros = jnp.zeros((NH, TW - BW), F32)
        acc = jnp.zeros((NH, TW), F32)
        for qi in range(CHUNK):
            xa = jnp.concatenate([jnp.where(valid_a, a_ref[qi], 0.0), zeros], axis=1)
            xb = jnp.concatenate([jnp.where(valid_b, b_ref[qi], 0.0), zeros], axis=1)
            acc = acc + (pltpu.roll(xa, LANES - qi, 1) + pltpu.roll(xb, CHUNK - qi, 1))
        oh = _diag_onehot(True)
        hi = acc.astype(BF16)
        mid = (acc - hi.astype(F32)).astype(BF16)
        r = lax.broadcasted_iota(jnp.int32, (NH, NRELP), 1)
        near = jnp.where(r < 2 * REL_CLIP, _nn(hi, oh) + _nn(mid, oh), 0.0)
        o_ref[...] = jnp.where(r == 2 * REL_CLIP, -jnp.sum(near, axis=-1, keepdims=True), near)

    vmem = pl.BlockSpec(memory_space=pltpu.VMEM)
    return pl.pallas_call(
        body, name=name, in_specs=[vmem, vmem], out_specs=vmem,
        out_shape=jax.ShapeDtypeStruct((NH, NRELP), F32),
        compiler_params=pltpu.CompilerParams(vmem_limit_bytes=32 * 2 ** 20),
    )(dba, dbb)


def _build_bias(bias3, ba_ref, bb_ref):
    bias3[NMASK] = jnp.full((QB, WIN), NEG, F32)
    for qc in range(QC):
        rows = slice(qc * CHUNK, (qc + 1) * CHUNK)
        if qc % 2 == 0:
            bias3[NMASK, rows, qc * CHUNK:qc * CHUNK + BW] = ba_ref[...] * LOG2E
        else:
            bias3[NMASK, rows, (qc - 1) * CHUNK:(qc - 1) * CHUNK + BW] = bb_ref[...] * LOG2E
    col = lax.broadcasted_iota(jnp.int32, (QB, WIN), 1)
    for sub in range(NMASK):
        bias3[sub] = jnp.where(col < PAD - sub * QB, NEG, bias3[NMASK])


def _nsub(S, most):
    n = min(most, S // QB)
    assert S % (n * QB) == 0 and n >= NMASK
    return n


def _row0(i, sub, nsub):
    return pl.multiple_of((i * nsub + sub) * QB, QB)


def _scores(q_ref, k_ref, i, sub, nsub):
    return _nt(q_ref[sub * QB:(sub + 1) * QB, :], k_ref[pl.ds(_row0(i, sub, nsub), WIN), :])


HALF = QB // 2
LIVE = WIN - LANES


def _live(half):
    return slice(half * HALF, (half + 1) * HALF), slice(half * LANES, half * LANES + LIVE)


def _widen(x, half):
    zeros = jnp.zeros((HALF, LANES), x.dtype)
    return jnp.concatenate([x, zeros] if half == 0 else [zeros, x], axis=1)


def _probs(s, bias3, i, sub):
    which = jnp.where(i == 0, sub, NMASK) if sub < NMASK else NMASK
    out = []
    for half in range(2):
        rows, cols = _live(half)
        t = s[rows, cols] * (SM_SCALE * LOG2E) + bias3[which, rows, cols]
        e = jnp.exp2(t - jnp.max(t, axis=-1, keepdims=True))
        out.append(_widen((e * (1.0 / jnp.sum(e, axis=-1, keepdims=True))).astype(BF16), half))
    return jnp.concatenate(out, axis=0)


def _attn_fwd(q, kp, vp, ba, bb, name, comm=None):
    S = q.shape[0]
    nsub = _nsub(S, NSUB_FWD)
    R = nsub * QB

    def body(q_ref, k_ref, v_ref, ba_ref, bb_ref, o_ref, p_ref, bias3):
        i = pl.program_id(1)

        @pl.when(i == 0)
        def _():
            _build_bias(bias3, ba_ref, bb_ref)

        s_next = _scores(q_ref, k_ref, i, 0, nsub)
        for sub in range(nsub):
            s = s_next
            if sub + 1 < nsub:
                s_next = _scores(q_ref, k_ref, i, sub + 1, nsub)
            pb = _probs(s, bias3, i, sub)
            p_ref[sub] = pb
            o_ref[sub * QB:(sub + 1) * QB, :] = _nn(pb, v_ref[pl.ds(_row0(i, sub, nsub), WIN), :]).astype(BF16)

    return _call(
        body, name, (NH, S // R),
        [pl.BlockSpec((R, HD), lambda h, i: (i, h)), pl.BlockSpec((S + PAD, HD), lambda h, i: (0, h)),
         pl.BlockSpec((S + PAD, HD), lambda h, i: (0, h)), pl.BlockSpec((None, CHUNK, BW), lambda h, i: (h, 0, 0)),
         pl.BlockSpec((None, CHUNK, BW), lambda h, i: (h, 0, 0))],
        [pl.BlockSpec((R, HD), lambda h, i: (i, h)), pl.BlockSpec((None, nsub, QB, WIN), lambda h, i: (h, i, 0, 0))],
        [jax.ShapeDtypeStruct((S, E), BF16), jax.ShapeDtypeStruct((NH, S // QB, QB, WIN), BF16)],
        [pltpu.VMEM((NMASK + 1, QB, WIN), F32)],
        _params(2, 48), (q, kp, vp, ba, bb), comm)


def _store_grad(acc, stage, dw_hbm, sem):
    for q in range(NCHIP):
        stage[...] = acc[q].astype(BF16)
        cp = pltpu.make_async_copy(stage, dw_hbm.at[q], sem)
        cp.start()
        cp.wait()


def _out_bwd(dh, y, gate, a, cs, z, w, name, comm=None, tm=256):
    S = dh.shape[0]
    kb = E // NCHIP
    cb = 256
    n_t = S // tm

    def body(dh_ref, y_ref, gate_ref, a_ref, cs_ref, z_ref, w_hbm, da_ref, dz_ref, dw_hbm, st_ref, w_v, acc, stage, sem):
        i = pl.program_id(0)

        @pl.when(i == 0)
        def _():
            cp = pltpu.make_async_copy(w_hbm, w_v, sem)
            cp.start()
            acc[...] = jnp.zeros(acc.shape, F32)
            st_ref[...] = jnp.zeros((SUBLANES, D), F32)
            cp.wait()

        dhh = dh_ref[...]
        st_ref[0:1, :] += _colsum(dhh * y_ref[...].astype(F32))
        dy = (dhh * gate_ref[...]).astype(BF16)
        for blk in range(E // cb):
            p, r0 = divmod(blk * cb, kb)
            cols = slice(blk * cb, (blk + 1) * cb)
            zz = z_ref[:, cols].astype(F32)
            sig = jax.nn.sigmoid(zz)
            sz = zz * sig
            ae = a_ref[:, cols].astype(F32) * cs_ref[:, cols]
            acc[p, r0:r0 + cb, :] += _tn((ae * sz).astype(BF16), dy)
            dact = _nt(dy, w_v[p, r0:r0 + cb, :])
            da_ref[:, cols] = (dact * sz).astype(BF16)
            dz_ref[:, cols] = (dact * ae * (sig * (1.0 + zz * (1.0 - sig)))).astype(BF16)

        @pl.when(i == n_t - 1)
        def _():
            _store_grad(acc, stage, dw_hbm, sem)

    return _call(
        body, name, (n_t,),
        [pl.BlockSpec((tm, D), lambda i: (i, 0)), pl.BlockSpec((tm, D), lambda i: (i, 0)), _row(D),
         pl.BlockSpec((tm, E), lambda i: (i, 0)), _row(E), pl.BlockSpec((tm, E), lambda i: (i, 0)), ANY],
        [pl.BlockSpec((tm, E), lambda i: (i, 0)), pl.BlockSpec((tm, E), lambda i: (i, 0)), ANY,
         pl.BlockSpec((SUBLANES, D), lambda i: (0, 0))],
        [jax.ShapeDtypeStruct((S, E), BF16), jax.ShapeDtypeStruct((S, E), BF16),
         jax.ShapeDtypeStruct((NCHIP, kb, D), BF16), jax.ShapeDtypeStruct((SUBLANES, D), F32)],
        [pltpu.VMEM((NCHIP, kb, D), BF16), pltpu.VMEM((NCHIP, kb, D), F32), pltpu.VMEM((kb, D), BF16),
         pltpu.SemaphoreType.DMA],
        _params(1, 52), (dh, y, gate, a, cs, z, w), comm)


def _attn_bwd(q, kp, vp, probs, do, prev, name, comm=None):
    S = q.shape[0]
    nsub = _nsub(S, NSUB_BWD)
    R = nsub * QB
    n_i = S // R
    dt_kv = F32 if prev is None else BF16

    def body(*refs):
        q_ref, k_ref, v_ref, p_ref, do_ref = refs[:5]
        refs = refs[5:]
        if prev is not None:
            pk_hbm, pv_hbm = refs[:2]
            refs = refs[2:]
        dq_ref, dk_ref, dv_ref, dba_ref, dbb_ref, dbias, dk_acc, dv_acc = refs[:8]
        if prev is not None:
            pk_v, pv_v, sems = refs[8:]
        h = pl.program_id(0)
        i = pl.program_id(1)

        def prev_copies():
            cols = pl.ds(pl.multiple_of(h * HD, HD), HD)
            return (pltpu.make_async_copy(pk_hbm.at[:, cols], pk_v, sems.at[0]),
                    pltpu.make_async_copy(pv_hbm.at[:, cols], pv_v, sems.at[1]))

        @pl.when(i == 0)
        def _():
            if prev is not None:
                for cp in prev_copies():
                    cp.start()
            dbias[...] = jnp.zeros((2, CHUNK, DBW), F32)
            dk_acc[...] = jnp.zeros((S + PAD, HD), F32)
            dv_acc[...] = jnp.zeros((S + PAD, HD), F32)

        def mxu_in(sub):
            return _nt(do_ref[sub * QB:(sub + 1) * QB, :], v_ref[pl.ds(_row0(i, sub, nsub), WIN), :])

        nxt = mxu_in(0)
        for sub in range(nsub):
            rows = slice(sub * QB, (sub + 1) * QB)
            win = pl.ds(_row0(i, sub, nsub), WIN)
            dp = nxt
            if sub + 1 < nsub:
                nxt = mxu_in(sub + 1)
            parts = []
            for half in range(2):
                hrows, hcols = _live(half)
                p = p_ref[sub, hrows, hcols].astype(F32)
                dph = dp[hrows, hcols]
                ds = p * (dph - jnp.sum(p * dph, axis=-1, keepdims=True))
                dbias[0] += ds[0:CHUNK, LIVE - DBW:LIVE]
                dbias[1] += ds[CHUNK:HALF, LIVE - DBW:LIVE]
                parts.append(_widen((ds * SM_SCALE).astype(BF16), half))
            dsb = jnp.concatenate(parts, axis=0)
            dq_ref[rows, :] = _nn(dsb, k_ref[win, :]).astype(BF16)
            dk_acc[win, :] += _tn(dsb, q_ref[rows, :])
            dv_acc[win, :] += _tn(p_ref[sub], do_ref[rows, :])

        @pl.when(i == n_i - 1)
        def _():
            zeros = jnp.zeros((CHUNK, BW - DBW), F32)
            dba_ref[...] = jnp.concatenate([zeros, dbias[0]], axis=1)
            dbb_ref[...] = jnp.concatenate([zeros, dbias[1]], axis=1)
            if prev is None:
                dk_ref[...] = dk_acc[...]
                dv_ref[...] = dv_acc[...]
            else:
                for cp in prev_copies():
                    cp.wait()
                dk_ref[...] = (dk_acc[...] + pk_v[...]).astype(BF16)
                dv_ref[...] = (dv_acc[...] + pv_v[...]).astype(BF16)

    head = pl.BlockSpec((S + PAD, HD), lambda h, i: (0, h))
    strip = pl.BlockSpec((None, CHUNK, BW), lambda h, i: (h, 0, 0))
    blk = pl.BlockSpec((R, HD), lambda h, i: (i, h))
    in_specs = [blk, head, head, pl.BlockSpec((None, nsub, QB, WIN), lambda h, i: (h, i, 0, 0)), blk]
    scratch = [pltpu.VMEM((2, CHUNK, DBW), F32), pltpu.VMEM((S + PAD, HD), F32), pltpu.VMEM((S + PAD, HD), F32)]
    args = (q, kp, vp, probs, do)
    if prev is not None:
        in_specs += [ANY, ANY]
        scratch += [pltpu.VMEM((S + PAD, HD), F32), pltpu.VMEM((S + PAD, HD), F32), pltpu.SemaphoreType.DMA((2,))]
        args += tuple(prev)
    return _call(
        body, name, (NH, n_i), in_specs, [blk, head, head, strip, strip],
        [jax.ShapeDtypeStruct((S, E), BF16), jax.ShapeDtypeStruct((S + PAD, E), dt_kv),
         jax.ShapeDtypeStruct((S + PAD, E), dt_kv), jax.ShapeDtypeStruct((NH, CHUNK, BW), F32),
         jax.ShapeDtypeStruct((NH, CHUNK, BW), F32)],
        scratch, _params(2, 56), args, comm)


def _pool_bwd(dms, mixed, pooled, wg, a_scale, name, comm=None, tm=512):
    S = dms.shape[0]
    n_t = S // tm

    def rev(i):
        return (n_t - 1 - i, 0)

    def body(d_ref, m_ref, p_ref, wg_ref, as_ref, dv_ref, dwg_ref, st_ref, buf):
        i = pl.program_id(0)

        @pl.when(i == 0)
        def _():
            buf[tm:tm + HALO, :] = jnp.zeros((HALO, E), F32)
            dwg_ref[...] = jnp.zeros((4, GW, GW), F32)
            st_ref[...] = jnp.zeros((SUBLANES, E), F32)

        t = (n_t - 1 - i) * tm + lax.broadcasted_iota(jnp.int32, (tm, 1), 0)
        st_ref[0:1, :] += _colsum(d_ref[...].astype(F32) * m_ref[...].astype(F32))
        for gi, w in enumerate(POOL_W):
            cols = slice(gi * GW, (gi + 1) * GW)
            dm = (d_ref[:, cols].astype(F32) * as_ref[:, cols]).astype(BF16)
            dpool = _nt(dm, wg_ref[gi])
            dwg_ref[gi] += _tn(p_ref[:, cols], dm)
            inv_cnt = 1.0 / jnp.minimum(t + 1, w).astype(F32)
            buf[0:tm, cols] = dpool * inv_cnt
            s = buf[:, cols]
            k = 1
            while k < w:
                s = s + pltpu.roll(s, tm + HALO - k, 0)
                k *= 2
            dv_ref[:, cols] = (s[0:tm, :] - dpool).astype(BF16)
        buf[tm:tm + HALO, :] = buf[0:HALO, :]

    return _call(
        body, name, (n_t,),
        [pl.BlockSpec((tm, E), rev), pl.BlockSpec((tm, E), rev), pl.BlockSpec((tm, E), rev),
         pl.BlockSpec((4, GW, GW), lambda i: (0, 0, 0)), _row(E)],
        [pl.BlockSpec((tm, E), rev), pl.BlockSpec((4, GW, GW), lambda i: (0, 0, 0)),
         pl.BlockSpec((SUBLANES, E), lambda i: (0, 0))],
        [jax.ShapeDtypeStruct((S, E), BF16), jax.ShapeDtypeStruct((4, GW, GW), F32),
         jax.ShapeDtypeStruct((SUBLANES, E), F32)],
        [pltpu.VMEM((tm + HALO, E), F32)],
        _params(1, 52), (dms, mixed, pooled, wg, a_scale), comm)


def _in_bwd(da, db, row_off, u, h, g, scale, w, dh_out, name, comm=None, tm=512):
    S = h.shape[0]
    n_t = S // tm
    off = row_off // tm

    def body(da_ref, db_ref, u_ref, h_ref, g_ref, sc_ref, w_hbm, dho_ref, dhi_ref, dw_hbm, st_ref, w_v, acc, stage, sem):
        i = pl.program_id(0)

        @pl.when(i == 0)
        def _():
            cp = pltpu.make_async_copy(w_hbm, w_v, sem)
            cp.start()
            acc[...] = jnp.zeros(acc.shape, F32)
            st_ref[...] = jnp.zeros((SUBLANES, D), F32)
            cp.wait()

        ub = u_ref[...]
        du = None
        for q in range(NCHIP):
            d_ref = da_ref if q < 2 else db_ref
            dv = d_ref[:, (q % 2) * D:(q % 2 + 1) * D]
            acc[q] += _tn(ub, dv)
            part = _nt(dv, w_v[q])
            du = part if du is None else du + part

        hh = h_ref[...]
        r = lax.rsqrt(jnp.mean(hh * hh, axis=-1, keepdims=True) + EPS)
        xhat = hh * r
        gg = g_ref[...]
        st_ref[0:1, :] += _colsum(du)
        st_ref[1:2, :] += _colsum(du * (xhat * gg))
        dn = du * (1.0 + sc_ref[...])
        st_ref[2:3, :] += _colsum(dn * xhat)
        dx = dn * gg
        dhi_ref[...] = dho_ref[...] + r * (dx - xhat * jnp.mean(dx * xhat, axis=-1, keepdims=True))

        @pl.when(i == n_t - 1)
        def _():
            _store_grad(acc, stage, dw_hbm, sem)

    part_spec = pl.BlockSpec((tm, E), lambda i: (i + off, 0))
    return _call(
        body, name, (n_t,),
        [part_spec, part_spec, pl.BlockSpec((tm, D), lambda i: (i, 0)), pl.BlockSpec((tm, D), lambda i: (i, 0)),
         _row(D), _row(D), ANY, pl.BlockSpec((tm, D), lambda i: (i, 0))],
        [pl.BlockSpec((tm, D), lambda i: (i, 0)), ANY, pl.BlockSpec((SUBLANES, D), lambda i: (0, 0))],
        [jax.ShapeDtypeStruct((S, D), F32), jax.ShapeDtypeStruct((NCHIP, D, D), BF16),
         jax.ShapeDtypeStruct((SUBLANES, D), F32)],
        [pltpu.VMEM((NCHIP, D, D), BF16), pltpu.VMEM((NCHIP, D, D), F32), pltpu.VMEM((D, D), BF16),
         pltpu.SemaphoreType.DMA],
        _params(1, 60), (da, db, u, h, g, scale, w, dh_out), comm)


def _grad_ada(c_act_t, dmod, name):
    L, _, n = dmod.shape

    def body(c_ref, d_ref, o_ref):
        acc = None
        for b in range(SUBLANES):
            part = c_ref[:, b:b + 1] * d_ref[b:b + 1, :]
            acc = part if acc is None else acc + part
        o_ref[...] = acc

    return pl.pallas_call(
        body, name=name, grid=(L,),
        in_specs=[pl.BlockSpec((D, SUBLANES), lambda l: (0, 0)), pl.BlockSpec((None, SUBLANES, n), lambda l: (l, 0, 0))],
        out_specs=pl.BlockSpec((None, D, n), lambda l: (l, 0, 0)),
        out_shape=jax.ShapeDtypeStruct((L, D, n), F32),
        compiler_params=_params(1, 32),
    )(c_act_t, dmod)


def _stats_reduce(g3, loss_row, name):
    n_dev, rows, _ = g3.shape

    def body(g_ref, o_ref, l_ref):
        acc = g_ref[0]
        for d in range(1, n_dev):
            acc = acc + g_ref[d]
        o_ref[...] = acc
        tot = jnp.sum(o_ref[loss_row:loss_row + 1, :], axis=-1, keepdims=True)
        l_ref[...] = jnp.broadcast_to(tot * (0.5 / D), (SUBLANES, LANES))

    return pl.pallas_call(
        body, name=name,
        in_specs=[pl.BlockSpec(memory_space=pltpu.VMEM)],
        out_specs=[pl.BlockSpec(memory_space=pltpu.VMEM), pl.BlockSpec(memory_space=pltpu.VMEM)],
        out_shape=[jax.ShapeDtypeStruct((rows, D), F32), jax.ShapeDtypeStruct((SUBLANES, LANES), F32)],
        compiler_params=pltpu.CompilerParams(vmem_limit_bytes=32 * 2 ** 20),
    )(g3)


def _sum4(own, land, chip, name, tr=256):
    _, R, C = own.shape
    tr = min(tr, R)

    def body(p_ref, own_ref, land_ref, o_ref):
        o_ref[...] = ((own_ref[...].astype(F32) + land_ref[0].astype(F32)) + land_ref[1].astype(F32)) + land_ref[2].astype(F32)

    return pl.pallas_call(
        body, name=name,
        grid_spec=pltpu.PrefetchScalarGridSpec(
            num_scalar_prefetch=1, grid=(R // tr,),
            in_specs=[pl.BlockSpec((None, tr, C), lambda i, p: (p[0], i, 0)), pl.BlockSpec((3, tr, C), lambda i, p: (0, i, 0))],
            out_specs=pl.BlockSpec((tr, C), lambda i, p: (i, 0))),
        out_shape=jax.ShapeDtypeStruct((R, C), F32),
        compiler_params=_params(1, 32),
    )(chip, pltpu.with_memory_space_constraint(own, pltpu.HBM), pltpu.with_memory_space_constraint(land, pltpu.HBM))


def _adamw(w, m, v, g, name, tr=256):
    L, R, C = w.shape
    tr = min(tr, R)
    stacked = not isinstance(g, (list, tuple))
    n_g = None if stacked else [len(ps) for ps in g]
    flat = [g] if stacked else [a for ps in g for a in ps]

    def body(*refs):
        w_ref, m_ref, v_ref = refs[:3]
        g_refs = refs[3:3 + len(flat)]
        go_ref, d_ref, mo_ref, vo_ref = refs[3 + len(flat):]
        if stacked:
            gg = g_refs[0][...]
        else:
            layer = pl.program_id(0)
            gg = None
            k = 0
            for li in range(L):
                gl = None
                for _ in range(n_g[li]):
                    x = g_refs[k][...]
                    gl = x if gl is None else gl + x
                    k += 1
                gg = gl if gg is None else jnp.where(layer == li, gl, gg)
        m2 = ADAM_B1 * m_ref[...] + (1.0 - ADAM_B1) * gg
        v2 = ADAM_B2 * v_ref[...] + (1.0 - ADAM_B2) * (gg * gg)
        m_hat = m2 / (1.0 - ADAM_B1 ** ADAM_STEP)
        v_hat = v2 / (1.0 - ADAM_B2 ** ADAM_STEP)
        go_ref[...] = gg
        d_ref[...] = -ADAM_LR * (m_hat / (jnp.sqrt(v_hat) + ADAM_EPS) + ADAM_WD * w_ref[...])
        mo_ref[...] = m2
        vo_ref[...] = v2

    big = pl.BlockSpec((None, tr, C), lambda l, i: (l, i, 0))
    g_specs = [big] if stacked else [pl.BlockSpec((tr, C), lambda l, i: (i, 0))] * len(flat)
    return pl.pallas_call(
        body, name=name, grid=(L, R // tr),
        in_specs=[big, big, big] + g_specs,
        out_specs=[big, big, big, big],
        out_shape=[jax.ShapeDtypeStruct((L, R, C), F32)] * 4,
        compiler_params=_params(2, 48),
    )(w, m, v, *flat)


GATHER8_SEMS = [pltpu.SemaphoreType.DMA((7,)), pltpu.SemaphoreType.DMA((7,)), pltpu.SemaphoreType.DMA]


def _gather8(x_ref, out_ref, send_sems, recv_sems, local_sem):
    m = x_ref.shape[0]
    x, y, c = _place()
    me, sibling = (x, y, c), (x, y, 1 - c)
    chips = [(1 - x, y), (x, 1 - y), (1 - x, 1 - y)]

    def rows(px, py, pc):
        return out_ref.at[pl.ds((4 * px + 2 * py + pc) * m, m), :]

    def copy(k, block, to, src=None):
        return pltpu.make_async_remote_copy(
            src_ref=rows(*block) if src is None else src, dst_ref=rows(*block),
            send_sem=send_sems.at[k], recv_sem=recv_sems.at[k], device_id=to, device_id_type=MESH)

    mine = pltpu.make_async_copy(x_ref, rows(*me), local_sem)
    mine.start()
    first = [copy(0, me, sibling, src=x_ref)]
    first += [copy(1 + j, me, (*chip, c), src=x_ref) for j, chip in enumerate(chips)]
    for cp in first:
        cp.start()
    passed = [copy(4 + j, (*chip, c), sibling) for j, chip in enumerate(chips)]
    for j, chip in enumerate(chips):
        copy(1 + j, (*chip, c), me).wait_recv()
        passed[j].start()
    copy(0, sibling, me).wait_recv()
    for j, chip in enumerate(chips):
        copy(4 + j, (*chip, 1 - c), me).wait_recv()
    for cp in first + passed:
        cp.wait_send()
    mine.wait()


def _allgather8(xs, name, comm=None):
    m, n = xs.shape
    n_c = 0 if comm is None else comm.n

    def body(*refs):
        x_ref, out_ref = refs[0], refs[1 + n_c]
        c_refs = (refs[1:1 + n_c], refs[2 + n_c:2 + 2 * n_c]) + tuple(refs[5 + 2 * n_c:])
        if comm is not None:
            comm.start(*c_refs)
        _gather8(x_ref, out_ref, *refs[2 + 2 * n_c:5 + 2 * n_c])
        if comm is not None:
            comm.wait(*c_refs)

    vmem = pl.BlockSpec(memory_space=pltpu.VMEM)
    outs = pl.pallas_call(
        body, name=name,
        out_shape=[jax.ShapeDtypeStruct((8 * m, n), xs.dtype)] + ([] if comm is None else comm.out_shape),
        in_specs=[vmem] + [ANY] * n_c,
        out_specs=[vmem] + [ANY] * n_c,
        scratch_shapes=GATHER8_SEMS + ([] if comm is None else comm.scratch),
        compiler_params=pltpu.CompilerParams(vmem_limit_bytes=32 * 2 ** 20),
    )(xs, *([] if comm is None else comm.arrays))
    return outs[0], list(outs[1:])


def _prologue(c8, ada_w, ada_b, kv_ada_w, kv_ada_b, extra, comm, name):
    L, _, n = ada_w.shape
    k = kv_ada_w.shape[1]
    e = extra.shape[1]
    width = L * n + k + e
    n_c = comm.n

    def body(*refs):
        c_ref, w_hbm, b_ref, kw_hbm, kb_ref, x_ref = refs[:6]
        c_in = refs[6:6 + n_c]
        ca_ref, out_ref = refs[6 + n_c:8 + n_c]
        c_out = refs[8 + n_c:8 + 2 * n_c]
        cbuf, wbuf, kbuf, part, wsems = refs[8 + 2 * n_c:13 + 2 * n_c]
        sems_a = refs[13 + 2 * n_c:16 + 2 * n_c]
        sems_b = refs[16 + 2 * n_c:19 + 2 * n_c]
        c_refs = (c_in, c_out) + tuple(refs[19 + 2 * n_c:])
        comm.start(*c_refs)

        def fetch(l):
            return pltpu.make_async_copy(w_hbm.at[l], wbuf.at[l % 2], wsems.at[l % 2])

        fetch(0).start()
        kv_copy = pltpu.make_async_copy(kw_hbm, kbuf, wsems.at[2])
        kv_copy.start()
        _gather8(c_ref, cbuf, *sems_a)
        cc = jnp.concatenate([cbuf[SUBLANES * d:SUBLANES * d + 1, :] for d in range(8)], axis=0)
        ca = cc * jax.nn.sigmoid(cc)
        ca_ref[...] = ca
        cab = ca.astype(BF16)
        for l in range(L):
            fetch(l).wait()
            if l + 1 < L:
                fetch(l + 1).start()
            part[:, l * n:(l + 1) * n] = _nn(cab, wbuf[l % 2].astype(BF16)) + b_ref[l]
        kv_copy.wait()
        part[:, L * n:L * n + k] = _nn(cab, kbuf[...].astype(BF16)) + kb_ref[...]
        part[:, L * n + k:] = jnp.broadcast_to(x_ref[...], (SUBLANES, e))
        _gather8(part, out_ref, *sems_b)
        comm.wait(*c_refs)

    vmem = pl.BlockSpec(memory_space=pltpu.VMEM)
    outs = pl.pallas_call(
        body, name=name,
        out_shape=[jax.ShapeDtypeStruct((SUBLANES, D), F32), jax.ShapeDtypeStruct((8 * SUBLANES, width), F32)] + comm.out_shape,
        in_specs=[vmem, ANY, vmem, ANY, vmem, vmem] + [ANY] * n_c,
        out_specs=[vmem, vmem] + [ANY] * n_c,
        scratch_shapes=[pltpu.VMEM((8 * SUBLANES, D), F32), pltpu.VMEM((2, D, n), F32), pltpu.VMEM((D, k), F32),
                        pltpu.VMEM((SUBLANES, width), F32), pltpu.SemaphoreType.DMA((3,))] + GATHER8_SEMS + GATHER8_SEMS
        + comm.scratch,
        compiler_params=pltpu.CompilerParams(vmem_limit_bytes=32 * 2 ** 20),
    )(c8, ada_w, ada_b, kv_ada_w, kv_ada_b, extra, *comm.arrays)
    return outs[0], outs[1], list(outs[2:])


def _pad8(a):
    return jnp.pad(a, ((0, SUBLANES - a.shape[0]), (0, 0)))


def _group_rows(wg):
    return wg.transpose(1, 0, 2, 3).reshape(4, GW, GW)


def _example_step(h0, tgt, mods, kvmod, a_scale, norm_g, kv_norm_g, final_g, b_rel_bias, sh, w_first, chip_arr):
    ones_e = jnp.ones((1, E), F32)
    shift = [mods[l:l + 1, 0:D] for l in range(4)]
    scale = [mods[l:l + 1, D:2 * D] for l in range(4)]
    gate = [mods[l:l + 1, 2 * D:3 * D] for l in range(4)]
    gl = [norm_g[l:l + 1] for l in range(4)]
    kv_shift, kv_scale = kvmod[None, 0:D], kvmod[None, D:2 * D]
    kv_g = kv_norm_g[None]

    w_a = w_first
    hs = [h0]
    saved = []
    nxt = [[sh["a_in"][1], sh["a_grp"][1], sh["a_out"][1]], [sh["kv"][0], sh["b_in"][0]]]
    for l in range(2):
        w_in_l, wg_l, wo_l = w_a
        wg_full = _group_rows(wg_l)
        (u, z, pooled, mixed, y, hn), got = _a_fwd(hs[-1], gl[l], shift[l], scale[l], a_scale[l:l + 1], gate[l], w_in_l,
                                                   wg_full, wo_l, f"a{l}_fwd", comm=_Comm(gathers=nxt[l]))
        saved.append((u, z, pooled, mixed, y, w_in_l, wg_full, wo_l))
        hs.append(hn)
        if l == 0:
            w_a = got
        else:
            w_kv, wb_in0 = got

    (uk, kp, vp), _ = _in_fwd(hs[2], kv_g, kv_shift, kv_scale, w_kv, BF16, BF16, "kv_in_fwd", pad_rows=PAD)
    wb_in = [wb_in0, None]
    wb_out = [None, None]

    for bi in range(2):
        l = 2 + bi
        sa, sb = _bias_build(jnp.pad(b_rel_bias[bi], ((0, 0), (0, NRELP - NREL))), f"b{bi}_bias")
        (u, q, z), _ = _in_fwd(hs[-1], gl[l], shift[l], scale[l], wb_in[bi], BF16, BF16, f"b{bi}_in_fwd")
        comm = _Comm(gathers=[sh["b_out"][0], sh["b_in"][1], sh["b_out"][1]]) if bi == 0 else None
        (att, probs), got = _attn_fwd(q, kp, vp, sa.transpose(1, 0, 2), sb.transpose(1, 0, 2), f"b{bi}_attn_fwd", comm=comm)
        if bi == 0:
            wb_out[0], wb_in[1], wb_out[1] = got
        if bi == 0:
            (y, hn), _ = _out_fwd(att, z, wb_out[bi], gate[l], hs[-1], f"b{bi}_out_fwd")
            hs.append(hn)
        else:
            (y, dh, st_fin), _ = _out_fwd(att, z, wb_out[bi], gate[l], hs[-1], f"b{bi}_out_fwd", head=(final_g[None], tgt))
        saved.append((u, z, q, att, y, probs))

    st_in = [None] * 4
    st_out = [None] * 4
    grads = {}
    landed = {}

    def carry(names):
        return _Comm(scatters=[grads[n] for n in names]) if names else None

    def land(names, got):
        for n, a in zip(names, got):
            landed[n] = a

    u, z, q, att, y, probs = saved[3]
    (datt, dz, grads["b_out1"], st_out[3]), _ = _out_bwd(dh, y, gate[3], att, ones_e, z, wb_out[1], "b1_out_bwd")
    (dq, dk1, dv1, dsa, dsb), _ = _attn_bwd(q, kp, vp, probs, datt, None, "b1_attn_bwd")
    drb1 = _dbias_reduce(dsa.transpose(1, 0, 2), dsb.transpose(1, 0, 2), "b1_dbias")
    (dh, grads["b_in1"], st_in[3]), _ = _in_bwd(dq, dz, 0, u, hs[3], gl[3], scale[3], wb_in[1], dh, "b1_in_bwd")
    u, z, q, att, y, probs = saved[2]
    (datt, dz, grads["b_out0"], st_out[2]), _ = _out_bwd(dh, y, gate[2], att, ones_e, z, wb_out[0], "b0_out_bwd")
    (dq, dk, dv, dsa, dsb), got = _attn_bwd(q, kp, vp, probs, datt, (dk1, dv1), "b0_attn_bwd",
                                            comm=carry(["b_out1", "b_in1", "b_out0"]))
    land(["b_out1", "b_in1", "b_out0"], got)
    drb0 = _dbias_reduce(dsa.transpose(1, 0, 2), dsb.transpose(1, 0, 2), "b0_dbias")
    (dh, grads["b_in0"], st_in[2]), _ = _in_bwd(dq, dz, 0, u, hs[2], gl[2], scale[2], wb_in[0], dh, "b0_in_bwd")
    (dh, grads["kv"], st_kv), got = _in_bwd(dk, dv, PAD, uk, hs[2], kv_g, kv_scale, w_kv, dh, "kv_in_bwd",
                                            comm=carry(["b_in0"]))
    land(["b_in0"], got)
    st_pool = [None] * 2
    plan = {1: dict(o=[], p=[], i=["kv", "a_out1", "a_grp1"]), 0: dict(o=["a_in1"], p=["a_out0"], i=[])}
    early = ["b_out1", "b_in1", "b_out0", "b_in0", "kv", "a_out1", "a_grp1", "a_in1"]
    late = ["a_out0", "a_grp0", "a_in0"]
    both = {}

    def sum4(n):
        return _sum4(grads[n], landed[n], chip_arr, f"sum4_{n}")

    for l in (1, 0):
        u, z, pooled, mixed, y, w_in_l, wg_full, wo = saved[l]
        asl = a_scale[l:l + 1]
        (dms, dz, grads[f"a_out{l}"], st_out[l]), got = _out_bwd(dh, y, gate[l], mixed, asl, z, wo, f"a{l}_out_bwd",
                                                                comm=carry(plan[l]["o"]))
        land(plan[l]["o"], got)
        comm = carry(plan[l]["p"])
        if l == 0:
            mine = [sum4(n) for n in early]
            comm = _Comm(scatters=[grads[n] for n in plan[l]["p"]], swaps=mine)
        (dval, dwg, st_pool[l]), got = _pool_bwd(dms, mixed, pooled, wg_full, asl, f"a{l}_pool_bwd", comm=comm)
        land(plan[l]["p"], got)
        if l == 0:
            both.update({n: [a, b] for n, a, b in zip(early, mine, got[len(plan[l]["p"]):])})
        grads[f"a_grp{l}"] = (dwg.reshape(4, NCHIP, GW // NCHIP, GW).transpose(1, 0, 2, 3).reshape(NCHIP, GW, GW)
                              .astype(BF16))
        (dh, grads[f"a_in{l}"], st_in[l]), got = _in_bwd(dval, dz, 0, u, hs[l], gl[l], scale[l], w_in_l, dh, f"a{l}_in_bwd",
                                                         comm=carry(plan[l]["i"]))
        land(plan[l]["i"], got)
    pieces = st_in + [st_kv] + st_out + [st_fin]
    pieces += [_pad8(st_pool[l][0].reshape(2, D)) for l in range(2)]
    pieces += [_pad8(d.reshape(NH * NRELP // D, D)) for d in (drb0, drb1)]
    gathered, got = _allgather8(jnp.concatenate(pieces, axis=0), "gather_stats", comm=carry(["a_grp0", "a_in0"]))
    land(["a_grp0", "a_in0"], got)
    mine = [sum4(n) for n in late]
    both.update({n: [a, b] for n, a, b in zip(late, mine, _comm_only(_Comm(swaps=mine), "swap_last"))})
    return dh, both, gathered.reshape(8, N_STAT, D)


ROW_IN = [8 * l for l in range(4)]
ROW_KV = 32
ROW_OUT = [40 + 8 * l for l in range(4)]
ROW_FIN = 72
ROW_ASC = [80, 88]
ROW_RB = [96, 104]
N_STAT = 112


def kernel(x, c, ada_w, ada_b, norm_g, a_w_in, a_w_group, a_scale, a_w_out, kv_norm_g, kv_ada_w, kv_ada_b, w_kv, b_w_in, b_rel_bias, b_w_out, final_g, loss_target, m_ada_w, m_ada_b, m_norm_g, m_a_w_in, m_a_w_group, m_a_scale, m_a_w_out, m_kv_norm_g, m_kv_ada_w, m_kv_ada_b, m_w_kv, m_b_w_in, m_b_rel_bias, m_b_w_out, m_final_g, v_ada_w, v_ada_b, v_norm_g, v_a_w_in, v_a_w_group, v_a_scale, v_a_w_out, v_kv_norm_g, v_kv_ada_w, v_kv_ada_b, v_w_kv, v_b_w_in, v_b_rel_bias, v_b_w_out, v_final_g):
    xi, yi, ci = _place()
    chip = 2 * xi + yi
    dev = 4 * xi + 2 * yi + ci
    n_ada = ada_w.shape[2]
    n_kva = kv_ada_w.shape[1]
    n_asc = a_scale.shape[1]

    ada_b_sh = lax.dynamic_slice_in_dim(ada_b, chip * n_ada, n_ada, axis=1)
    kvb_sh = lax.dynamic_slice_in_dim(kv_ada_b, chip * n_kva, n_kva, axis=0)
    sh = dict(a_in=[a_w_in[l].astype(BF16) for l in range(2)], a_grp=[a_w_group[l].astype(BF16) for l in range(2)],
              a_out=[a_w_out[l].astype(BF16) for l in range(2)], kv=[w_kv.astype(BF16)],
              b_in=[b_w_in[l].astype(BF16) for l in range(2)], b_out=[b_w_out[l].astype(BF16) for l in range(2)])
    c_act, gathered, w_first = _prologue(
        jnp.broadcast_to(c, (SUBLANES, D)), ada_w, ada_b_sh[:, None, :], kv_ada_w, kvb_sh[None, :],
        a_scale.reshape(1, 2 * n_asc), _Comm(gathers=[sh["a_in"][0], sh["a_grp"][0], sh["a_out"][0]]), "prologue")
    rows = jnp.concatenate([lax.dynamic_slice_in_dim(gathered, SUBLANES * (2 * p + ci) + dev, 1, axis=0)
                            for p in range(NCHIP)], axis=0)
    mods = jnp.stack([rows[:, l * n_ada:(l + 1) * n_ada].reshape(3 * D) for l in range(4)])
    kvmod = rows[:, 4 * n_ada:4 * n_ada + n_kva].reshape(2 * D)
    o_asc = 4 * n_ada + n_kva
    a_scale_full = jnp.stack([rows[:, o_asc + l * n_asc:o_asc + (l + 1) * n_asc].reshape(E) for l in range(2)])

    chip_arr = jnp.reshape(chip, (1,)).astype(jnp.int32)
    dh, both, g3 = _example_step(x[0], loss_target[0], mods, kvmod, a_scale_full, norm_g, kv_norm_g, final_g,
                                 b_rel_bias, sh, w_first, chip_arr)
    grad_x = dh[None]

    red, loss_tile = _stats_reduce(g3, ROW_FIN + 1, "stats_reduce")
    loss = loss_tile[0, 0]

    def cat(rows_):
        return jnp.concatenate(rows_, axis=-1)

    g_ada_b = jnp.stack([cat([red[ROW_IN[l]], red[ROW_IN[l] + 1], red[ROW_OUT[l]]]) for l in range(4)])
    g_norm_g = jnp.stack([red[ROW_IN[l] + 2] for l in range(4)])
    g_kv_norm_g = red[ROW_KV + 2]
    g_kv_ada_b = cat([red[ROW_KV], red[ROW_KV + 1]])
    g_final_g = red[ROW_FIN]
    g_asc_full = jnp.stack([red[ROW_ASC[l]:ROW_ASC[l] + 2].reshape(E) for l in range(2)])
    g_a_scale = lax.dynamic_slice_in_dim(g_asc_full, chip * n_asc, n_asc, axis=1)
    g_rel = jnp.stack([red[ROW_RB[bi]:ROW_RB[bi] + NH * NRELP // D].reshape(NH, NRELP)[:, :NREL] for bi in range(2)])

    dmod = jnp.stack([cat([g3[:, ROW_IN[l]], g3[:, ROW_IN[l] + 1], g3[:, ROW_OUT[l]]]) for l in range(4)])
    dmod_sh = lax.dynamic_slice_in_dim(dmod, chip * n_ada, n_ada, axis=2)
    dkv = cat([g3[:, ROW_KV], g3[:, ROW_KV + 1]])[None]
    dkv_sh = lax.dynamic_slice_in_dim(dkv, chip * n_kva, n_kva, axis=2)
    c_act_t = c_act.T
    g_ada_w = _grad_ada(c_act_t, dmod_sh, "grad_ada_w")
    g_kv_ada_w = _grad_ada(c_act_t, dkv_sh, "grad_kv_ada_w")

    def upd(w, m, v, g, name, shape3):
        g = g.reshape(shape3) if not isinstance(g, list) else g
        outs = _adamw(w.reshape(shape3), m.reshape(shape3), v.reshape(shape3), g, name)
        return [o.reshape(w.shape) for o in outs]

    def pair(name):
        return [both[name + "0"], both[name + "1"]]

    res = {}
    res["ada_w"] = upd(ada_w, m_ada_w, v_ada_w, g_ada_w, "adamw_ada_w", ada_w.shape)
    res["ada_b"] = upd(ada_b, m_ada_b, v_ada_b, g_ada_b, "adamw_ada_b", (1,) + ada_b.shape)
    res["norm_g"] = upd(norm_g, m_norm_g, v_norm_g, g_norm_g, "adamw_norm_g", (1,) + norm_g.shape)
    res["a_w_in"] = upd(a_w_in, m_a_w_in, v_a_w_in, pair("a_in"), "adamw_a_w_in", a_w_in.shape)
    res["a_w_group"] = upd(a_w_group, m_a_w_group, v_a_w_group, pair("a_grp"), "adamw_a_w_group", (2, GW, GW))
    res["a_scale"] = upd(a_scale, m_a_scale, v_a_scale, g_a_scale, "adamw_a_scale", (1,) + a_scale.shape)
    res["a_w_out"] = upd(a_w_out, m_a_w_out, v_a_w_out, pair("a_out"), "adamw_a_w_out", a_w_out.shape)
    res["kv_norm_g"] = upd(kv_norm_g, m_kv_norm_g, v_kv_norm_g, g_kv_norm_g, "adamw_kv_norm_g", (1, 1, D))
    res["kv_ada_w"] = upd(kv_ada_w, m_kv_ada_w, v_kv_ada_w, g_kv_ada_w, "adamw_kv_ada_w", (1,) + kv_ada_w.shape)
    res["kv_ada_b"] = upd(kv_ada_b, m_kv_ada_b, v_kv_ada_b, g_kv_ada_b, "adamw_kv_ada_b", (1, 1, 2 * D))
    res["w_kv"] = upd(w_kv, m_w_kv, v_w_kv, [both["kv"]], "adamw_w_kv", (1,) + w_kv.shape)
    res["b_w_in"] = upd(b_w_in, m_b_w_in, v_b_w_in, pair("b_in"), "adamw_b_w_in", b_w_in.shape)
    res["b_rel_bias"] = upd(b_rel_bias, m_b_rel_bias, v_b_rel_bias, g_rel, "adamw_b_rel_bias", (1, 2 * NH, NREL))
    res["b_w_out"] = upd(b_w_out, m_b_w_out, v_b_w_out, pair("b_out"), "adamw_b_w_out", b_w_out.shape)
    res["final_g"] = upd(final_g, m_final_g, v_final_g, g_final_g, "adamw_final_g", (1, 1, D))

    names = ["ada_w", "ada_b", "norm_g", "a_w_in", "a_w_group", "a_scale", "a_w_out", "kv_norm_g", "kv_ada_w", "kv_ada_b",
             "w_kv", "b_w_in", "b_rel_bias", "b_w_out", "final_g"]
    return (loss, grad_x, *[res[n][0] for n in names], *[res[n][1] for n in names], *[res[n][2] for n in names],
            *[res[n][3] for n in names])
```

```python
import math

import jax
import jax.numpy as jnp
from jax import lax
from jax.experimental import pallas as pl
from jax.experimental.pallas import tpu as pltpu

F32 = jnp.float32
BF16 = jnp.bfloat16

D = 1024
E = 2048
NH = 16
HD = 128
CHUNK = 64
LEFT = 8
PAD = LEFT * CHUNK
NREL = 257
NRELP = 384
REL_CLIP = 128
EPS = 1e-6
NEG = -1e30
LOG2E = math.log2(math.e)
SM_SCALE = HD ** -0.5
POOL_W = (2, 4, 8, 16)
GW = 512
HALO = 16
QC = 4
QB = QC * CHUNK
NMASK = PAD // QB
WIN = (QC + LEFT) * CHUNK
BW = (LEFT + 2) * CHUNK
DBW = 4 * CHUNK
NSUB_FWD = 32
NSUB_BWD = 16
NCHIP = 4
LANES = 128
SUBLANES = 8

ADAM_LR = 0.001
ADAM_B1 = 0.9
ADAM_B2 = 0.999
ADAM_EPS = 1e-08
ADAM_WD = 0.01
ADAM_STEP = 10

MESH = pl.DeviceIdType.MESH
ANY = pl.BlockSpec(memory_space=pl.ANY)


def _params(n_axes, vmem_mb):
    return pltpu.CompilerParams(dimension_semantics=("arbitrary",) * n_axes, vmem_limit_bytes=vmem_mb * 2 ** 20)


def _nn(a, b):
    return jnp.dot(a, b, preferred_element_type=F32)


def _nt(a, b):
    return lax.dot_general(a, b, (((1,), (1,)), ((), ())), preferred_element_type=F32)


def _tn(a, b):
    return lax.dot_general(a, b, (((0,), (0,)), ((), ())), preferred_element_type=F32)


def _row(n):
    return pl.BlockSpec((1, n), lambda i: (0, 0))


def _colsum(x):
    return jnp.sum(x, axis=0, keepdims=True)


def _place():
    return lax.axis_index("x"), lax.axis_index("y"), lax.axis_index("c")


class _Comm:
    def __init__(self, gathers=(), scatters=(), swaps=()):
        self.n_g = len(gathers)
        self.n_chip = len(gathers) + len(scatters)
        self.n_sw = len(swaps)
        self.arrays = list(gathers) + list(scatters) + list(swaps)
        self.n = len(self.arrays)
        self.half = [a.shape[0] // 2 for a in gathers]
        self.out_shape = ([jax.ShapeDtypeStruct((NCHIP,) + a.shape, a.dtype) for a in gathers]
                          + [jax.ShapeDtypeStruct((3,) + a.shape[1:], a.dtype) for a in scatters]
                          + [jax.ShapeDtypeStruct(a.shape, a.dtype) for a in swaps])
        n_c, n_f, n_s = max(3 * self.n_chip, 1), max(3 * self.n_g, 1), max(self.n_sw, 1)
        self.scratch = [pltpu.SemaphoreType.DMA((n_c,)), pltpu.SemaphoreType.DMA((n_c,)),
                        pltpu.SemaphoreType.DMA((max(self.n_g, 1),)), pltpu.SemaphoreType.DMA((n_f,)),
                        pltpu.SemaphoreType.DMA((n_f,)), pltpu.SemaphoreType.DMA((n_s,)), pltpu.SemaphoreType.DMA((n_s,))]

    def _chip_copies(self, ins, outs, send, recv, landing):
        x, y, c = _place()
        chips = [(1 - x, y), (x, 1 - y), (1 - x, 1 - y)]
        mine = 2 * x + y
        cps = []
        for k in range(self.n_chip):
            for j, (cx, cy) in enumerate(chips):
                q = 2 * cx + cy
                if k < self.n_g:
                    part = pl.ds(c * self.half[k], self.half[k])
                    src = ins[k].at[part]
                    dst = outs[k].at[q if landing else mine, part]
                else:
                    src = ins[k].at[q]
                    dst = outs[k].at[j]
                cps.append(pltpu.make_async_remote_copy(
                    src_ref=src, dst_ref=dst, send_sem=send.at[3 * k + j], recv_sem=recv.at[3 * k + j],
                    device_id=(cx, cy, c), device_id_type=MESH))
        return cps

    def _core_copies(self, outs, fsend, frecv, landing):
        x, y, c = _place()
        chips = [(1 - x, y), (x, 1 - y), (1 - x, 1 - y)]
        cps = []
        for k in range(self.n_g):
            for j, (cx, cy) in enumerate(chips):
                part = pl.ds((1 - c if landing else c) * self.half[k], self.half[k])
                blk = outs[k].at[2 * cx + cy, part]
                cps.append(pltpu.make_async_remote_copy(
                    src_ref=blk, dst_ref=blk, send_sem=fsend.at[3 * k + j], recv_sem=frecv.at[3 * k + j],
                    device_id=(x, y, 1 - c), device_id_type=MESH))
        return cps

    def _local_copies(self, ins, outs, loc):
        x, y, _ = _place()
        return [pltpu.make_async_copy(ins[k], outs[k].at[2 * x + y], loc.at[k]) for k in range(self.n_g)]

    def _swap_copies(self, ins, outs, ssend, srecv):
        x, y, c = _place()
        return [pltpu.make_async_remote_copy(
            src_ref=ins[k], dst_ref=outs[k], send_sem=ssend.at[k - self.n_chip], recv_sem=srecv.at[k - self.n_chip],
            device_id=(x, y, 1 - c), device_id_type=MESH) for k in range(self.n_chip, self.n)]

    def start(self, ins, outs, send, recv, loc, fsend, frecv, ssend, srecv):
        for cp in (self._local_copies(ins, outs, loc) + self._chip_copies(ins, outs, send, recv, False)
                   + self._swap_copies(ins, outs, ssend, srecv)):
            cp.start()

    def wait(self, ins, outs, send, recv, loc, fsend, frecv, ssend, srecv):
        lands = self._chip_copies(ins, outs, send, recv, True)
        passes = self._core_copies(outs, fsend, frecv, False)
        for k in range(self.n_chip):
            for j in range(3):
                lands[3 * k + j].wait_recv()
                if k < self.n_g:
                    passes[3 * k + j].start()
        for cp in self._core_copies(outs, fsend, frecv, True):
            cp.wait_recv()
        swaps = self._swap_copies(ins, outs, ssend, srecv)
        for cp in swaps:
            cp.wait_recv()
        for cp in self._chip_copies(ins, outs, send, recv, False) + passes + swaps:
            cp.wait_send()
        for cp in self._local_copies(ins, outs, loc):
            cp.wait()


def _call(body, name, grid, in_specs, out_specs, out_shape, scratch, params, args, comm=None):
    n_in, n_out, n_sc = len(in_specs), len(out_specs), len(scratch)
    if comm is None:
        outs = pl.pallas_call(body, name=name, grid=grid, in_specs=in_specs, out_specs=out_specs, out_shape=out_shape,
                              scratch_shapes=scratch, compiler_params=params)(*args)
        return list(outs), []
    n = comm.n
    o0 = n_in + n
    s0 = o0 + n_out + n

    def wrapped(*refs):
        c_refs = (refs[n_in:o0], refs[o0 + n_out:s0]) + tuple(refs[s0 + n_sc:])
        ids = [pl.program_id(a) for a in range(len(grid))]
        first = ids[0] == 0
        last = ids[0] == grid[0] - 1
        for a in range(1, len(grid)):
            first = first & (ids[a] == 0)
            last = last & (ids[a] == grid[a] - 1)

        @pl.when(first)
        def _():
            comm.start(*c_refs)

        body(*refs[:n_in], *refs[o0:o0 + n_out], *refs[s0:s0 + n_sc])

        @pl.when(last)
        def _():
            comm.wait(*c_refs)

    outs = pl.pallas_call(
        wrapped, name=name, grid=grid, in_specs=list(in_specs) + [ANY] * n, out_specs=list(out_specs) + [ANY] * n,
        out_shape=list(out_shape) + comm.out_shape, scratch_shapes=list(scratch) + comm.scratch, compiler_params=params,
    )(*args, *comm.arrays)
    return list(outs[:n_out]), list(outs[n_out:])


def _comm_only(comm, name):
    def body(*refs):
        c_refs = (refs[:comm.n], refs[comm.n:2 * comm.n]) + tuple(refs[2 * comm.n:])
        comm.start(*c_refs)
        comm.wait(*c_refs)

    return pl.pallas_call(body, name=name, in_specs=[ANY] * comm.n, out_specs=[ANY] * comm.n, out_shape=comm.out_shape,
                          scratch_shapes=comm.scratch)(*comm.arrays)


def _in_fwd(h, g, shift, scale, w, dt_a, dt_b, name, pad_rows=0, comm=None, tm=512):
    S = h.shape[0]
    n_pad = pad_rows // tm

    def body(h_ref, g_ref, sh_ref, sc_ref, w_hbm, u_ref, oa_ref, ob_ref, w_v, sem):
        i = pl.program_id(0)

        @pl.when(i == 0)
        def _():
            cp = pltpu.make_async_copy(w_hbm, w_v, sem)
            cp.start()
            cp.wait()

        hh = h_ref[...]
        r = lax.rsqrt(jnp.mean(hh * hh, axis=-1, keepdims=True) + EPS)
        u = (hh * r * g_ref[...]) * (1.0 + sc_ref[...]) + sh_ref[...]
        ub = u.astype(BF16)
        u_ref[...] = ub
        for q in range(NCHIP):
            o_ref = oa_ref if q < 2 else ob_ref
            o_ref[:, (q % 2) * D:(q % 2 + 1) * D] = _nn(ub, w_v[q]).astype(o_ref.dtype)

        if n_pad:
            @pl.when(i < n_pad)
            def _():
                oa_ref[...] = jnp.zeros(oa_ref.shape, oa_ref.dtype)
                ob_ref[...] = jnp.zeros(ob_ref.shape, ob_ref.dtype)

    def src(i):
        return (jnp.maximum(i - n_pad, 0), 0)

    outs, landed = _call(
        body, name, (S // tm + n_pad,),
        [pl.BlockSpec((tm, D), src), _row(D), _row(D), _row(D), ANY],
        [pl.BlockSpec((tm, D), src), pl.BlockSpec((tm, E), lambda i: (i, 0)), pl.BlockSpec((tm, E), lambda i: (i, 0))],
        [jax.ShapeDtypeStruct((S, D), BF16), jax.ShapeDtypeStruct((S + pad_rows, E), dt_a),
         jax.ShapeDtypeStruct((S + pad_rows, E), dt_b)],
        [pltpu.VMEM((NCHIP, D, D), BF16), pltpu.SemaphoreType.DMA],
        _params(1, 52), (h, g, shift, scale, w), comm)
    return outs, landed


def _a_fwd(h, g, shift, scale, asc, gate, w_in, wg, w_out, name, comm=None, tm=512):
    S = h.shape[0]

    def body(h_ref, g_ref, sh_ref, sc_ref, as_ref, gate_ref, wi_hbm, wg_hbm, wo_hbm,
             u_ref, z_ref, p_ref, m_ref, y_ref, ho_ref, wi_v, wg_v, wo_v, buf, sems):
        i = pl.program_id(0)

        @pl.when(i == 0)
        def _():
            cps = [pltpu.make_async_copy(wi_hbm, wi_v, sems.at[0]), pltpu.make_async_copy(wg_hbm, wg_v, sems.at[1]),
                   pltpu.make_async_copy(wo_hbm, wo_v, sems.at[2])]
            for cp in cps:
                cp.start()
            buf[0:HALO, :] = jnp.zeros((HALO, E), F32)
            for cp in cps:
                cp.wait()

        hh = h_ref[...]
        r = lax.rsqrt(jnp.mean(hh * hh, axis=-1, keepdims=True) + EPS)
        ub = ((hh * r * g_ref[...]) * (1.0 + sc_ref[...]) + sh_ref[...]).astype(BF16)
        u_ref[...] = ub
        for q in range(2):
            buf[HALO:HALO + tm, q * D:(q + 1) * D] = _nn(ub, wi_v[q])
        t = i * tm + lax.broadcasted_iota(jnp.int32, (tm, 1), 0)
        y = None
        for gi, w in enumerate(POOL_W):
            cols = slice(gi * GW, (gi + 1) * GW)
            x = buf[:, cols]
            s = x
            k = 1
            while k < w:
                s = s + pltpu.roll(s, k, 0)
                k *= 2
            inv_cnt = 1.0 / jnp.minimum(t + 1, w).astype(F32)
            pb = (s[HALO:, :] * inv_cnt - x[HALO:, :]).astype(BF16)
            p_ref[:, cols] = pb
            mb = _nn(pb, wg_v[gi]).astype(BF16)
            m_ref[:, cols] = mb
            zb = _nn(ub, wi_v[2 + gi // 2, :, (gi % 2) * GW:(gi % 2 + 1) * GW]).astype(BF16)
            z_ref[:, cols] = zb
            zz = zb.astype(F32)
            act = ((mb.astype(F32) * as_ref[:, cols]) * (zz * jax.nn.sigmoid(zz))).astype(BF16)
            part = _nn(act, wo_v[gi])
            y = part if y is None else y + part
        buf[0:HALO, :] = buf[tm:tm + HALO, :]
        y_ref[...] = y.astype(BF16)
        ho_ref[...] = hh + gate_ref[...] * y

    rows_d = pl.BlockSpec((tm, D), lambda i: (i, 0))
    rows_e = pl.BlockSpec((tm, E), lambda i: (i, 0))
    return _call(
        body, name, (S // tm,),
        [rows_d, _row(D), _row(D), _row(D), _row(E), _row(D), ANY, ANY, ANY],
        [rows_d, rows_e, rows_e, rows_e, rows_d, rows_d],
        [jax.ShapeDtypeStruct((S, D), BF16), jax.ShapeDtypeStruct((S, E), BF16), jax.ShapeDtypeStruct((S, E), BF16),
         jax.ShapeDtypeStruct((S, E), BF16), jax.ShapeDtypeStruct((S, D), BF16), jax.ShapeDtypeStruct((S, D), F32)],
        [pltpu.VMEM((NCHIP, D, D), BF16), pltpu.VMEM((4, GW, GW), BF16), pltpu.VMEM((NCHIP, GW, D), BF16),
         pltpu.VMEM((tm + HALO, E), F32), pltpu.SemaphoreType.DMA((3,))],
        _params(1, 60), (h, g, shift, scale, asc, gate, w_in, wg, w_out), comm)


def _out_fwd(a, z, w, gate, h, name, head=None, comm=None, tm=512):
    S = h.shape[0]
    kb = E // NCHIP
    n_in = 5 if head is None else 7

    def body(*refs):
        a_ref, z_ref, w_hbm, gate_ref, h_ref = refs[:5]
        w_v, sem = refs[-2:]
        i = pl.program_id(0)

        @pl.when(i == 0)
        def _():
            cp = pltpu.make_async_copy(w_hbm, w_v, sem)
            cp.start()
            cp.wait()

        y = None
        for p in range(NCHIP):
            cols = slice(p * kb, (p + 1) * kb)
            zz = z_ref[:, cols].astype(F32)
            act = (a_ref[:, cols].astype(F32) * (zz * jax.nn.sigmoid(zz))).astype(BF16)
            part = _nn(act, w_v[p])
            y = part if y is None else y + part
        refs[n_in][...] = y.astype(BF16)
        hh = h_ref[...] + gate_ref[...] * y
        if head is None:
            refs[n_in + 1][...] = hh
            return
        g_ref, t_ref = refs[5:7]
        dh_ref, st_ref = refs[n_in + 1:n_in + 3]

        @pl.when(i == 0)
        def _():
            st_ref[...] = jnp.zeros((SUBLANES, D), F32)

        r = lax.rsqrt(jnp.mean(hh * hh, axis=-1, keepdims=True) + EPS)
        xhat = hh * r
        diff = xhat * g_ref[...] - t_ref[...]
        st_ref[1:2, :] += _colsum(diff * diff)
        dout = diff * (1.0 / D)
        st_ref[0:1, :] += _colsum(dout * xhat)
        dx = dout * g_ref[...]
        dh_ref[...] = r * (dx - xhat * jnp.mean(dx * xhat, axis=-1, keepdims=True))

    rows_d = pl.BlockSpec((tm, D), lambda i: (i, 0))
    rows_e = pl.BlockSpec((tm, E), lambda i: (i, 0))
    in_specs = [rows_e, rows_e, ANY, _row(D), rows_d]
    out_specs = [rows_d, rows_d]
    out_shape = [jax.ShapeDtypeStruct((S, D), BF16), jax.ShapeDtypeStruct((S, D), F32)]
    args = (a, z, w, gate, h)
    if head is not None:
        in_specs += [_row(D), rows_d]
        out_specs += [pl.BlockSpec((SUBLANES, D), lambda i: (0, 0))]
        out_shape += [jax.ShapeDtypeStruct((SUBLANES, D), F32)]
        args += tuple(head)
    return _call(body, name, (S // tm,), in_specs, out_specs, out_shape,
                 [pltpu.VMEM((NCHIP, kb, D), BF16), pltpu.SemaphoreType.DMA], _params(1, 52), args, comm)


TW = BW + LANES


def _diag_onehot(transpose):
    shape = (TW, NRELP) if transpose else (NRELP, TW)
    j = lax.broadcasted_iota(jnp.int32, shape, 0 if transpose else 1)
    r = lax.broadcasted_iota(jnp.int32, shape, 1 if transpose else 0)
    idx = jnp.clip(PAD - (j - LANES), -REL_CLIP, REL_CLIP) + REL_CLIP
    return jnp.where(idx == r, 1.0, 0.0).astype(BF16)


def _strip_valid():
    m = lax.broadcasted_iota(jnp.int32, (NH, BW), 1)
    return m < (LEFT + 1) * CHUNK, m >= CHUNK


def _bias_build(rb, name):
    def body(rb_ref, a_ref, b_ref):
        x = rb_ref[...]
        hi = x.astype(BF16)
        r1 = x - hi.astype(F32)
        mid = r1.astype(BF16)
        lo = (r1 - mid.astype(F32)).astype(BF16)
        oh = _diag_onehot(False)
        diag = (_nn(hi, oh) + _nn(mid, oh)) + _nn(lo, oh)
        valid_a, valid_b = _strip_valid()
        for qi in range(CHUNK):
            a_ref[qi] = jnp.where(valid_a, pltpu.roll(diag, TW - (LANES - qi), 1)[:, :BW], NEG)
            b_ref[qi] = jnp.where(valid_b, pltpu.roll(diag, TW - (CHUNK - qi), 1)[:, :BW], NEG)

    vmem = pl.BlockSpec(memory_space=pltpu.VMEM)
    return pl.pallas_call(
        body, name=name, in_specs=[vmem], out_specs=[vmem, vmem],
        out_shape=[jax.ShapeDtypeStruct((CHUNK, NH, BW), F32), jax.ShapeDtypeStruct((CHUNK, NH, BW), F32)],
        compiler_params=pltpu.CompilerParams(vmem_limit_bytes=32 * 2 ** 20),
    )(rb)


def _dbias_reduce(dba, dbb, name):
    def body(a_ref, b_ref, o_ref):
        valid_a, valid_b = _strip_valid()
        zeros = jnp.zeros((NH, TW - BW), F32)
        acc = jnp.zeros((NH, TW), F32)
        for qi in range(CHUNK):
            xa = jnp.concatenate([jnp.where(valid_a, a_ref[qi], 0.0), zeros], axis=1)
            xb = jnp.concatenate([jnp.where(valid_b, b_ref[qi], 0.0), zeros], axis=1)
            acc = acc + (pltpu.roll(xa, LANES - qi, 1) + pltpu.roll(xb, CHUNK - qi, 1))
        oh = _diag_onehot(True)
        hi = acc.astype(BF16)
        mid = (acc - hi.astype(F32)).astype(BF16)
        r = lax.broadcasted_iota(jnp.int32, (NH, NRELP), 1)
        near = jnp.where(r < 2 * REL_CLIP, _nn(hi, oh) + _nn(mid, oh), 0.0)
        o_ref[...] = jnp.where(r == 2 * REL_CLIP, -jnp.sum(near, axis=-1, keepdims=True), near)

    vmem = pl.BlockSpec(memory_space=pltpu.VMEM)
    return pl.pallas_call(
        body, name=name, in_specs=[vmem, vmem], out_specs=vmem,
        out_shape=jax.ShapeDtypeStruct((NH, NRELP), F32),
        compiler_params=pltpu.CompilerParams(vmem_limit_bytes=32 * 2 ** 20),
    )(dba, dbb)


def _build_bias(bias3, ba_ref, bb_ref):
    bias3[NMASK] = jnp.full((QB, WIN), NEG, F32)
    for qc in range(QC):
        rows = slice(qc * CHUNK, (qc + 1) * CHUNK)
        if qc % 2 == 0:
            bias3[NMASK, rows, qc * CHUNK:qc * CHUNK + BW] = ba_ref[...] * LOG2E
        else:
            bias3[NMASK, rows, (qc - 1) * CHUNK:(qc - 1) * CHUNK + BW] = bb_ref[...] * LOG2E
    col = lax.broadcasted_iota(jnp.int32, (QB, WIN), 1)
    for sub in range(NMASK):
        bias3[sub] = jnp.where(col < PAD - sub * QB, NEG, bias3[NMASK])


def _nsub(S, most):
    n = min(most, S // QB)
    assert S % (n * QB) == 0 and n >= NMASK
    return n


def _row0(i, sub, nsub):
    return pl.multiple_of((i * nsub + sub) * QB, QB)


def _scores(q_ref, k_ref, i, sub, nsub):
    return _nt(q_ref[sub * QB:(sub + 1) * QB, :], k_ref[pl.ds(_row0(i, sub, nsub), WIN), :])


HALF = QB // 2
LIVE = WIN - LANES


def _live(half):
    return slice(half * HALF, (half + 1) * HALF), slice(half * LANES, half * LANES + LIVE)


def _widen(x, half):
    zeros = jnp.zeros((HALF, LANES), x.dtype)
    return jnp.concatenate([x, zeros] if half == 0 else [zeros, x], axis=1)


def _probs(s, bias3, i, sub):
    which = jnp.where(i == 0, sub, NMASK) if sub < NMASK else NMASK
    out = []
    for half in range(2):
        rows, cols = _live(half)
        t = s[rows, cols] * (SM_SCALE * LOG2E) + bias3[which, rows, cols]
        e = jnp.exp2(t - jnp.max(t, axis=-1, keepdims=True))
        out.append(_widen((e * (1.0 / jnp.sum(e, axis=-1, keepdims=True))).astype(BF16), half))
    return jnp.concatenate(out, axis=0)


def _attn_fwd(q, kp, vp, ba, bb, name, comm=None):
    S = q.shape[0]
    nsub = _nsub(S, NSUB_FWD)
    R = nsub * QB

    def body(q_ref, k_ref, v_ref, ba_ref, bb_ref, o_ref, p_ref, bias3):
        i = pl.program_id(1)

        @pl.when(i == 0)
        def _():
            _build_bias(bias3, ba_ref, bb_ref)

        s_next = _scores(q_ref, k_ref, i, 0, nsub)
        for sub in range(nsub):
            s = s_next
            if sub + 1 < nsub:
                s_next = _scores(q_ref, k_ref, i, sub + 1, nsub)
            pb = _probs(s, bias3, i, sub)
            p_ref[sub] = pb
            o_ref[sub * QB:(sub + 1) * QB, :] = _nn(pb, v_ref[pl.ds(_row0(i, sub, nsub), WIN), :]).astype(BF16)

    return _call(
        body, name, (NH, S // R),
        [pl.BlockSpec((R, HD), lambda h, i: (i, h)), pl.BlockSpec((S + PAD, HD), lambda h, i: (0, h)),
         pl.BlockSpec((S + PAD, HD), lambda h, i: (0, h)), pl.BlockSpec((None, CHUNK, BW), lambda h, i: (h, 0, 0)),
         pl.BlockSpec((None, CHUNK, BW), lambda h, i: (h, 0, 0))],
        [pl.BlockSpec((R, HD), lambda h, i: (i, h)), pl.BlockSpec((None, nsub, QB, WIN), lambda h, i: (h, i, 0, 0))],
        [jax.ShapeDtypeStruct((S, E), BF16), jax.ShapeDtypeStruct((NH, S // QB, QB, WIN), BF16)],
        [pltpu.VMEM((NMASK + 1, QB, WIN), F32)],
        _params(2, 48), (q, kp, vp, ba, bb), comm)


def _store_grad(acc, stage, dw_hbm, sem):
    for q in range(NCHIP):
        stage[...] = acc[q].astype(BF16)
        cp = pltpu.make_async_copy(stage, dw_hbm.at[q], sem)
        cp.start()
        cp.wait()


def _out_bwd(dh, y, gate, a, cs, z, w, name, scale_stat=False, comm=None, tm=256):
    S = dh.shape[0]
    kb = E // NCHIP
    cb = 256
    n_t = S // tm

    def body(dh_ref, y_ref, gate_ref, a_ref, cs_ref, z_ref, w_hbm, da_ref, dz_ref, dw_hbm, st_ref, w_v, acc, stage, sem):
        i = pl.program_id(0)

        @pl.when(i == 0)
        def _():
            cp = pltpu.make_async_copy(w_hbm, w_v, sem)
            cp.start()
            acc[...] = jnp.zeros(acc.shape, F32)
            st_ref[...] = jnp.zeros((SUBLANES, D), F32)
            cp.wait()

        dhh = dh_ref[...]
        st_ref[0:1, :] += _colsum(dhh * y_ref[...].astype(F32))
        dy = (dhh * gate_ref[...]).astype(BF16)
        for blk in range(E // cb):
            p, r0 = divmod(blk * cb, kb)
            cols = slice(blk * cb, (blk + 1) * cb)
            zz = z_ref[:, cols].astype(F32)
            sig = jax.nn.sigmoid(zz)
            sz = zz * sig
            aa = a_ref[:, cols].astype(F32)
            ae = aa * cs_ref[:, cols]
            acc[p, r0:r0 + cb, :] += _tn((ae * sz).astype(BF16), dy)
            dact = _nt(dy, w_v[p, r0:r0 + cb, :])
            dae = dact * sz
            if scale_stat:
                row, c0 = divmod(blk * cb, D)
                st_ref[1 + row:2 + row, c0:c0 + cb] += _colsum(dae * aa)
            da_ref[:, cols] = dae.astype(BF16)
            dz_ref[:, cols] = (dact * ae * (sig * (1.0 + zz * (1.0 - sig)))).astype(BF16)

        @pl.when(i == n_t - 1)
        def _():
            _store_grad(acc, stage, dw_hbm, sem)

    return _call(
        body, name, (n_t,),
        [pl.BlockSpec((tm, D), lambda i: (i, 0)), pl.BlockSpec((tm, D), lambda i: (i, 0)), _row(D),
         pl.BlockSpec((tm, E), lambda i: (i, 0)), _row(E), pl.BlockSpec((tm, E), lambda i: (i, 0)), ANY],
        [pl.BlockSpec((tm, E), lambda i: (i, 0)), pl.BlockSpec((tm, E), lambda i: (i, 0)), ANY,
         pl.BlockSpec((SUBLANES, D), lambda i: (0, 0))],
        [jax.ShapeDtypeStruct((S, E), BF16), jax.ShapeDtypeStruct((S, E), BF16),
         jax.ShapeDtypeStruct((NCHIP, kb, D), BF16), jax.ShapeDtypeStruct((SUBLANES, D), F32)],
        [pltpu.VMEM((NCHIP, kb, D), BF16), pltpu.VMEM((NCHIP, kb, D), F32), pltpu.VMEM((kb, D), BF16),
         pltpu.SemaphoreType.DMA],
        _params(1, 52), (dh, y, gate, a, cs, z, w), comm)


def _attn_bwd(q, kp, vp, probs, do, prev, name, comm=None):
    S = q.shape[0]
    nsub = _nsub(S, NSUB_BWD)
    R = nsub * QB
    n_i = S // R
    dt_kv = F32 if prev is None else BF16

    def body(*refs):
        q_ref, k_ref, v_ref, p_ref, do_ref = refs[:5]
        refs = refs[5:]
        if prev is not None:
            pk_hbm, pv_hbm = refs[:2]
            refs = refs[2:]
        dq_ref, dk_ref, dv_ref, dba_ref, dbb_ref, dbias, dk_acc, dv_acc = refs[:8]
        if prev is not None:
            pk_v, pv_v, sems = refs[8:]
        h = pl.program_id(0)
        i = pl.program_id(1)

        def prev_copies():
            cols = pl.ds(pl.multiple_of(h * HD, HD), HD)
            return (pltpu.make_async_copy(pk_hbm.at[:, cols], pk_v, sems.at[0]),
                    pltpu.make_async_copy(pv_hbm.at[:, cols], pv_v, sems.at[1]))

        @pl.when(i == 0)
        def _():
            if prev is not None:
                for cp in prev_copies():
                    cp.start()
            dbias[...] = jnp.zeros((2, CHUNK, DBW), F32)
            dk_acc[...] = jnp.zeros((S + PAD, HD), F32)
            dv_acc[...] = jnp.zeros((S + PAD, HD), F32)

        def mxu_in(sub):
            return _nt(do_ref[sub * QB:(sub + 1) * QB, :], v_ref[pl.ds(_row0(i, sub, nsub), WIN), :])

        nxt = mxu_in(0)
        for sub in range(nsub):
            rows = slice(sub * QB, (sub + 1) * QB)
            win = pl.ds(_row0(i, sub, nsub), WIN)
            dp = nxt
            if sub + 1 < nsub:
                nxt = mxu_in(sub + 1)
            parts = []
            for half in range(2):
                hrows, hcols = _live(half)
                p = p_ref[sub, hrows, hcols].astype(F32)
                dph = dp[hrows, hcols]
                ds = p * (dph - jnp.sum(p * dph, axis=-1, keepdims=True))
                dbias[0] += ds[0:CHUNK, LIVE - DBW:LIVE]
                dbias[1] += ds[CHUNK:HALF, LIVE - DBW:LIVE]
                parts.append(_widen((ds * SM_SCALE).astype(BF16), half))
            dsb = jnp.concatenate(parts, axis=0)
            dq_ref[rows, :] = _nn(dsb, k_ref[win, :]).astype(BF16)
            dk_acc[win, :] += _tn(dsb, q_ref[rows, :])
            dv_acc[win, :] += _tn(p_ref[sub], do_ref[rows, :])

        @pl.when(i == n_i - 1)
        def _():
            zeros = jnp.zeros((CHUNK, BW - DBW), F32)
            dba_ref[...] = jnp.concatenate([zeros, dbias[0]], axis=1)
            dbb_ref[...] = jnp.concatenate([zeros, dbias[1]], axis=1)
            if prev is None:
                dk_ref[...] = dk_acc[...]
                dv_ref[...] = dv_acc[...]
            else:
                for cp in prev_copies():
                    cp.wait()
                dk_ref[...] = (dk_acc[...] + pk_v[...]).astype(BF16)
                dv_ref[...] = (dv_acc[...] + pv_v[...]).astype(BF16)

    head = pl.BlockSpec((S + PAD, HD), lambda h, i: (0, h))
    strip = pl.BlockSpec((None, CHUNK, BW), lambda h, i: (h, 0, 0))
    blk = pl.BlockSpec((R, HD), lambda h, i: (i, h))
    in_specs = [blk, head, head, pl.BlockSpec((None, nsub, QB, WIN), lambda h, i: (h, i, 0, 0)), blk]
    scratch = [pltpu.VMEM((2, CHUNK, DBW), F32), pltpu.VMEM((S + PAD, HD), F32), pltpu.VMEM((S + PAD, HD), F32)]
    args = (q, kp, vp, probs, do)
    if prev is not None:
        in_specs += [ANY, ANY]
        scratch += [pltpu.VMEM((S + PAD, HD), F32), pltpu.VMEM((S + PAD, HD), F32), pltpu.SemaphoreType.DMA((2,))]
        args += tuple(prev)
    return _call(
        body, name, (NH, n_i), in_specs, [blk, head, head, strip, strip],
        [jax.ShapeDtypeStruct((S, E), BF16), jax.ShapeDtypeStruct((S + PAD, E), dt_kv),
         jax.ShapeDtypeStruct((S + PAD, E), dt_kv), jax.ShapeDtypeStruct((NH, CHUNK, BW), F32),
         jax.ShapeDtypeStruct((NH, CHUNK, BW), F32)],
        scratch, _params(2, 56), args, comm)


def _pool_bwd(dms, pooled, wg, a_scale, name, comm=None, tm=512):
    S = dms.shape[0]
    n_t = S // tm

    def rev(i):
        return (n_t - 1 - i, 0)

    def body(d_ref, p_ref, wg_ref, as_ref, dv_ref, dwg_ref, buf):
        i = pl.program_id(0)

        @pl.when(i == 0)
        def _():
            buf[tm:tm + HALO, :] = jnp.zeros((HALO, E), F32)
            dwg_ref[...] = jnp.zeros((4, GW, GW), F32)

        t = (n_t - 1 - i) * tm + lax.broadcasted_iota(jnp.int32, (tm, 1), 0)
        for gi, w in enumerate(POOL_W):
            cols = slice(gi * GW, (gi + 1) * GW)
            dm = (d_ref[:, cols].astype(F32) * as_ref[:, cols]).astype(BF16)
            dpool = _nt(dm, wg_ref[gi])
            dwg_ref[gi] += _tn(p_ref[:, cols], dm)
            inv_cnt = 1.0 / jnp.minimum(t + 1, w).astype(F32)
            buf[0:tm, cols] = dpool * inv_cnt
            s = buf[:, cols]
            k = 1
            while k < w:
                s = s + pltpu.roll(s, tm + HALO - k, 0)
                k *= 2
            dv_ref[:, cols] = (s[0:tm, :] - dpool).astype(BF16)
        buf[tm:tm + HALO, :] = buf[0:HALO, :]

    return _call(
        body, name, (n_t,),
        [pl.BlockSpec((tm, E), rev), pl.BlockSpec((tm, E), rev), pl.BlockSpec((4, GW, GW), lambda i: (0, 0, 0)), _row(E)],
        [pl.BlockSpec((tm, E), rev), pl.BlockSpec((4, GW, GW), lambda i: (0, 0, 0))],
        [jax.ShapeDtypeStruct((S, E), BF16), jax.ShapeDtypeStruct((4, GW, GW), F32)],
        [pltpu.VMEM((tm + HALO, E), F32)],
        _params(1, 52), (dms, pooled, wg, a_scale), comm)


def _in_bwd(da, db, row_off, u, h, g, scale, w, dh_out, name, comm=None, tm=512):
    S = h.shape[0]
    n_t = S // tm
    off = row_off // tm

    def body(da_ref, db_ref, u_ref, h_ref, g_ref, sc_ref, w_hbm, dho_ref, dhi_ref, dw_hbm, st_ref, w_v, acc, stage, sem):
        i = pl.program_id(0)

        @pl.when(i == 0)
        def _():
            cp = pltpu.make_async_copy(w_hbm, w_v, sem)
            cp.start()
            acc[...] = jnp.zeros(acc.shape, F32)
            st_ref[...] = jnp.zeros((SUBLANES, D), F32)
            cp.wait()

        ub = u_ref[...]
        du = None
        for q in range(NCHIP):
            d_ref = da_ref if q < 2 else db_ref
            dv = d_ref[:, (q % 2) * D:(q % 2 + 1) * D]
            acc[q] += _tn(ub, dv)
            part = _nt(dv, w_v[q])
            du = part if du is None else du + part

        hh = h_ref[...]
        r = lax.rsqrt(jnp.mean(hh * hh, axis=-1, keepdims=True) + EPS)
        xhat = hh * r
        gg = g_ref[...]
        st_ref[0:1, :] += _colsum(du)
        st_ref[1:2, :] += _colsum(du * (xhat * gg))
        dn = du * (1.0 + sc_ref[...])
        st_ref[2:3, :] += _colsum(dn * xhat)
        dx = dn * gg
        dhi_ref[...] = dho_ref[...] + r * (dx - xhat * jnp.mean(dx * xhat, axis=-1, keepdims=True))

        @pl.when(i == n_t - 1)
        def _():
            _store_grad(acc, stage, dw_hbm, sem)

    part_spec = pl.BlockSpec((tm, E), lambda i: (i + off, 0))
    return _call(
        body, name, (n_t,),
        [part_spec, part_spec, pl.BlockSpec((tm, D), lambda i: (i, 0)), pl.BlockSpec((tm, D), lambda i: (i, 0)),
         _row(D), _row(D), ANY, pl.BlockSpec((tm, D), lambda i: (i, 0))],
        [pl.BlockSpec((tm, D), lambda i: (i, 0)), ANY, pl.BlockSpec((SUBLANES, D), lambda i: (0, 0))],
        [jax.ShapeDtypeStruct((S, D), F32), jax.ShapeDtypeStruct((NCHIP, D, D), BF16),
         jax.ShapeDtypeStruct((SUBLANES, D), F32)],
        [pltpu.VMEM((NCHIP, D, D), BF16), pltpu.VMEM((NCHIP, D, D), F32), pltpu.VMEM((D, D), BF16),
         pltpu.SemaphoreType.DMA],
        _params(1, 60), (da, db, u, h, g, scale, w, dh_out), comm)


def _grad_ada(c_act_t, dmod, name):
    L, _, n = dmod.shape

    def body(c_ref, d_ref, o_ref):
        acc = None
        for b in range(SUBLANES):
            part = c_ref[:, b:b + 1] * d_ref[b:b + 1, :]
            acc = part if acc is None else acc + part
        o_ref[...] = acc

    return pl.pallas_call(
        body, name=name, grid=(L,),
        in_specs=[pl.BlockSpec((D, SUBLANES), lambda l: (0, 0)), pl.BlockSpec((None, SUBLANES, n), lambda l: (l, 0, 0))],
        out_specs=pl.BlockSpec((None, D, n), lambda l: (l, 0, 0)),
        out_shape=jax.ShapeDtypeStruct((L, D, n), F32),
        compiler_params=_params(1, 32),
    )(c_act_t, dmod)


def _stats_reduce(g3, loss_row, name):
    n_dev, rows, _ = g3.shape

    def body(g_ref, o_ref, l_ref):
        acc = g_ref[0]
        for d in range(1, n_dev):
            acc = acc + g_ref[d]
        o_ref[...] = acc
        tot = jnp.sum(o_ref[loss_row:loss_row + 1, :], axis=-1, keepdims=True)
        l_ref[...] = jnp.broadcast_to(tot * (0.5 / D), (SUBLANES, LANES))

    return pl.pallas_call(
        body, name=name,
        in_specs=[pl.BlockSpec(memory_space=pltpu.VMEM)],
        out_specs=[pl.BlockSpec(memory_space=pltpu.VMEM), pl.BlockSpec(memory_space=pltpu.VMEM)],
        out_shape=[jax.ShapeDtypeStruct((rows, D), F32), jax.ShapeDtypeStruct((SUBLANES, LANES), F32)],
        compiler_params=pltpu.CompilerParams(vmem_limit_bytes=32 * 2 ** 20),
    )(g3)


def _sum4(own, land, chip, name, tr=256):
    _, R, C = own.shape
    tr = min(tr, R)

    def body(p_ref, own_ref, land_ref, o_ref):
        o_ref[...] = ((own_ref[...].astype(F32) + land_ref[0].astype(F32)) + land_ref[1].astype(F32)) + land_ref[2].astype(F32)

    return pl.pallas_call(
        body, name=name,
        grid_spec=pltpu.PrefetchScalarGridSpec(
            num_scalar_prefetch=1, grid=(R // tr,),
            in_specs=[pl.BlockSpec((None, tr, C), lambda i, p: (p[0], i, 0)), pl.BlockSpec((3, tr, C), lambda i, p: (0, i, 0))],
            out_specs=pl.BlockSpec((tr, C), lambda i, p: (i, 0))),
        out_shape=jax.ShapeDtypeStruct((R, C), F32),
        compiler_params=_params(1, 32),
    )(chip, pltpu.with_memory_space_constraint(own, pltpu.HBM), pltpu.with_memory_space_constraint(land, pltpu.HBM))


def _adamw(w, m, v, g, name, tr=256):
    L, R, C = w.shape
    tr = min(tr, R)
    stacked = not isinstance(g, (list, tuple))
    n_g = None if stacked else [len(ps) for ps in g]
    flat = [g] if stacked else [a for ps in g for a in ps]

    def body(*refs):
        w_ref, m_ref, v_ref = refs[:3]
        g_refs = refs[3:3 + len(flat)]
        go_ref, d_ref, mo_ref, vo_ref = refs[3 + len(flat):]
        if stacked:
            gg = g_refs[0][...]
        else:
            layer = pl.program_id(0)
            gg = None
            k = 0
            for li in range(L):
                gl = None
                for _ in range(n_g[li]):
                    x = g_refs[k][...]
                    gl = x if gl is None else gl + x
                    k += 1
                gg = gl if gg is None else jnp.where(layer == li, gl, gg)
        m2 = ADAM_B1 * m_ref[...] + (1.0 - ADAM_B1) * gg
        v2 = ADAM_B2 * v_ref[...] + (1.0 - ADAM_B2) * (gg * gg)
        m_hat = m2 / (1.0 - ADAM_B1 ** ADAM_STEP)
        v_hat = v2 / (1.0 - ADAM_B2 ** ADAM_STEP)
        go_ref[...] = gg
        d_ref[...] = -ADAM_LR * (m_hat / (jnp.sqrt(v_hat) + ADAM_EPS) + ADAM_WD * w_ref[...])
        mo_ref[...] = m2
        vo_ref[...] = v2

    big = pl.BlockSpec((None, tr, C), lambda l, i: (l, i, 0))
    g_specs = [big] if stacked else [pl.BlockSpec((tr, C), lambda l, i: (i, 0))] * len(flat)
    return pl.pallas_call(
        body, name=name, grid=(L, R // tr),
        in_specs=[big, big, big] + g_specs,
        out_specs=[big, big, big, big],
        out_shape=[jax.ShapeDtypeStruct((L, R, C), F32)] * 4,
        compiler_params=_params(2, 48),
    )(w, m, v, *flat)


GATHER8_SEMS = [pltpu.SemaphoreType.DMA((7,)), pltpu.SemaphoreType.DMA((7,)), pltpu.SemaphoreType.DMA]


def _gather8(x_ref, out_ref, send_sems, recv_sems, local_sem):
    m = x_ref.shape[0]
    x, y, c = _place()
    me, sibling = (x, y, c), (x, y, 1 - c)
    chips = [(1 - x, y), (x, 1 - y), (1 - x, 1 - y)]

    def rows(px, py, pc):
        return out_ref.at[pl.ds((4 * px + 2 * py + pc) * m, m), :]

    def copy(k, block, to, src=None):
        return pltpu.make_async_remote_copy(
            src_ref=rows(*block) if src is None else src, dst_ref=rows(*block),
            send_sem=send_sems.at[k], recv_sem=recv_sems.at[k], device_id=to, device_id_type=MESH)

    mine = pltpu.make_async_copy(x_ref, rows(*me), local_sem)
    mine.start()
    first = [copy(0, me, sibling, src=x_ref)]
    first += [copy(1 + j, me, (*chip, c), src=x_ref) for j, chip in enumerate(chips)]
    for cp in first:
        cp.start()
    passed = [copy(4 + j, (*chip, c), sibling) for j, chip in enumerate(chips)]
    for j, chip in enumerate(chips):
        copy(1 + j, (*chip, c), me).wait_recv()
        passed[j].start()
    copy(0, sibling, me).wait_recv()
    for j, chip in enumerate(chips):
        copy(4 + j, (*chip, 1 - c), me).wait_recv()
    for cp in first + passed:
        cp.wait_send()
    mine.wait()


def _allgather8(xs, name, comm=None):
    m, n = xs.shape
    n_c = 0 if comm is None else comm.n

    def body(*refs):
        x_ref, out_ref = refs[0], refs[1 + n_c]
        c_refs = (refs[1:1 + n_c], refs[2 + n_c:2 + 2 * n_c]) + tuple(refs[5 + 2 * n_c:])
        if comm is not None:
            comm.start(*c_refs)
        _gather8(x_ref, out_ref, *refs[2 + 2 * n_c:5 + 2 * n_c])
        if comm is not None:
            comm.wait(*c_refs)

    vmem = pl.BlockSpec(memory_space=pltpu.VMEM)
    outs = pl.pallas_call(
        body, name=name,
        out_shape=[jax.ShapeDtypeStruct((8 * m, n), xs.dtype)] + ([] if comm is None else comm.out_shape),
        in_specs=[vmem] + [ANY] * n_c,
        out_specs=[vmem] + [ANY] * n_c,
        scratch_shapes=GATHER8_SEMS + ([] if comm is None else comm.scratch),
        compiler_params=pltpu.CompilerParams(vmem_limit_bytes=32 * 2 ** 20),
    )(xs, *([] if comm is None else comm.arrays))
    return outs[0], list(outs[1:])


def _prologue(c8, ada_w, ada_b, kv_ada_w, kv_ada_b, extra, comm, name):
    L, _, n = ada_w.shape
    k = kv_ada_w.shape[1]
    e = extra.shape[1]
    width = L * n + k + e
    n_c = comm.n

    def body(*refs):
        c_ref, w_hbm, b_ref, kw_hbm, kb_ref, x_ref = refs[:6]
        c_in = refs[6:6 + n_c]
        ca_ref, out_ref = refs[6 + n_c:8 + n_c]
        c_out = refs[8 + n_c:8 + 2 * n_c]
        cbuf, wbuf, kbuf, part, wsems = refs[8 + 2 * n_c:13 + 2 * n_c]
        sems_a = refs[13 + 2 * n_c:16 + 2 * n_c]
        sems_b = refs[16 + 2 * n_c:19 + 2 * n_c]
        c_refs = (c_in, c_out) + tuple(refs[19 + 2 * n_c:])
        comm.start(*c_refs)

        def fetch(l):
            return pltpu.make_async_copy(w_hbm.at[l], wbuf.at[l % 2], wsems.at[l % 2])

        fetch(0).start()
        kv_copy = pltpu.make_async_copy(kw_hbm, kbuf, wsems.at[2])
        kv_copy.start()
        _gather8(c_ref, cbuf, *sems_a)
        cc = jnp.concatenate([cbuf[SUBLANES * d:SUBLANES * d + 1, :] for d in range(8)], axis=0)
        ca = cc * jax.nn.sigmoid(cc)
        ca_ref[...] = ca
        cab = ca.astype(BF16)
        for l in range(L):
            fetch(l).wait()
            if l + 1 < L:
                fetch(l + 1).start()
            part[:, l * n:(l + 1) * n] = _nn(cab, wbuf[l % 2].astype(BF16)) + b_ref[l]
        kv_copy.wait()
        part[:, L * n:L * n + k] = _nn(cab, kbuf[...].astype(BF16)) + kb_ref[...]
        part[:, L * n + k:] = jnp.broadcast_to(x_ref[...], (SUBLANES, e))
        _gather8(part, out_ref, *sems_b)
        comm.wait(*c_refs)

    vmem = pl.BlockSpec(memory_space=pltpu.VMEM)
    outs = pl.pallas_call(
        body, name=name,
        out_shape=[jax.ShapeDtypeStruct((SUBLANES, D), F32), jax.ShapeDtypeStruct((8 * SUBLANES, width), F32)] + comm.out_shape,
        in_specs=[vmem, ANY, vmem, ANY, vmem, vmem] + [ANY] * n_c,
        out_specs=[vmem, vmem] + [ANY] * n_c,
        scratch_shapes=[pltpu.VMEM((8 * SUBLANES, D), F32), pltpu.VMEM((2, D, n), F32), pltpu.VMEM((D, k), F32),
                        pltpu.VMEM((SUBLANES, width), F32), pltpu.SemaphoreType.DMA((3,))] + GATHER8_SEMS + GATHER8_SEMS
        + comm.scratch,
        compiler_params=pltpu.CompilerParams(vmem_limit_bytes=32 * 2 ** 20),
    )(c8, ada_w, ada_b, kv_ada_w, kv_ada_b, extra, *comm.arrays)
    return outs[0], outs[1], list(outs[2:])


def _pad8(a):
    return jnp.pad(a, ((0, SUBLANES - a.shape[0]), (0, 0)))


def _group_rows(wg):
    return wg.transpose(1, 0, 2, 3).reshape(4, GW, GW)


def _example_step(h0, tgt, mods, kvmod, a_scale, norm_g, kv_norm_g, final_g, b_rel_bias, sh, w_first, chip_arr):
    ones_e = jnp.ones((1, E), F32)
    shift = [mods[l:l + 1, 0:D] for l in range(4)]
    scale = [mods[l:l + 1, D:2 * D] for l in range(4)]
    gate = [mods[l:l + 1, 2 * D:3 * D] for l in range(4)]
    gl = [norm_g[l:l + 1] for l in range(4)]
    kv_shift, kv_scale = kvmod[None, 0:D], kvmod[None, D:2 * D]
    kv_g = kv_norm_g[None]

    w_a = w_first
    hs = [h0]
    saved = []
    nxt = [[sh["a_in"][1], sh["a_grp"][1], sh["a_out"][1]], [sh["kv"][0], sh["b_in"][0]]]
    for l in range(2):
        w_in_l, wg_l, wo_l = w_a
        wg_full = _group_rows(wg_l)
        (u, z, pooled, mixed, y, hn), got = _a_fwd(hs[-1], gl[l], shift[l], scale[l], a_scale[l:l + 1], gate[l], w_in_l,
                                                   wg_full, wo_l, f"a{l}_fwd", comm=_Comm(gathers=nxt[l]))
        saved.append((u, z, pooled, mixed, y, w_in_l, wg_full, wo_l))
        hs.append(hn)
        if l == 0:
            w_a = got
        else:
            w_kv, wb_in0 = got

    (uk, kp, vp), _ = _in_fwd(hs[2], kv_g, kv_shift, kv_scale, w_kv, BF16, BF16, "kv_in_fwd", pad_rows=PAD)
    wb_in = [wb_in0, None]
    wb_out = [None, None]

    for bi in range(2):
        l = 2 + bi
        sa, sb = _bias_build(jnp.pad(b_rel_bias[bi], ((0, 0), (0, NRELP - NREL))), f"b{bi}_bias")
        (u, q, z), _ = _in_fwd(hs[-1], gl[l], shift[l], scale[l], wb_in[bi], BF16, BF16, f"b{bi}_in_fwd")
        comm = _Comm(gathers=[sh["b_out"][0], sh["b_in"][1], sh["b_out"][1]]) if bi == 0 else None
        (att, probs), got = _attn_fwd(q, kp, vp, sa.transpose(1, 0, 2), sb.transpose(1, 0, 2), f"b{bi}_attn_fwd", comm=comm)
        if bi == 0:
            wb_out[0], wb_in[1], wb_out[1] = got
        if bi == 0:
            (y, hn), _ = _out_fwd(att, z, wb_out[bi], gate[l], hs[-1], f"b{bi}_out_fwd")
            hs.append(hn)
        else:
            (y, dh, st_fin), _ = _out_fwd(att, z, wb_out[bi], gate[l], hs[-1], f"b{bi}_out_fwd", head=(final_g[None], tgt))
        saved.append((u, z, q, att, y, probs))

    st_in = [None] * 4
    st_out = [None] * 4
    grads = {}
    landed = {}

    def carry(names):
        return _Comm(scatters=[grads[n] for n in names]) if names else None

    def land(names, got):
        for n, a in zip(names, got):
            landed[n] = a

    u, z, q, att, y, probs = saved[3]
    (datt, dz, grads["b_out1"], st_out[3]), _ = _out_bwd(dh, y, gate[3], att, ones_e, z, wb_out[1], "b1_out_bwd")
    (dq, dk1, dv1, dsa, dsb), _ = _attn_bwd(q, kp, vp, probs, datt, None, "b1_attn_bwd")
    drb1 = _dbias_reduce(dsa.transpose(1, 0, 2), dsb.transpose(1, 0, 2), "b1_dbias")
    (dh, grads["b_in1"], st_in[3]), _ = _in_bwd(dq, dz, 0, u, hs[3], gl[3], scale[3], wb_in[1], dh, "b1_in_bwd")
    u, z, q, att, y, probs = saved[2]
    (datt, dz, grads["b_out0"], st_out[2]), _ = _out_bwd(dh, y, gate[2], att, ones_e, z, wb_out[0], "b0_out_bwd")
    (dq, dk, dv, dsa, dsb), got = _attn_bwd(q, kp, vp, probs, datt, (dk1, dv1), "b0_attn_bwd",
                                            comm=carry(["b_out1", "b_in1", "b_out0"]))
    land(["b_out1", "b_in1", "b_out0"], got)
    drb0 = _dbias_reduce(dsa.transpose(1, 0, 2), dsb.transpose(1, 0, 2), "b0_dbias")
    (dh, grads["b_in0"], st_in[2]), _ = _in_bwd(dq, dz, 0, u, hs[2], gl[2], scale[2], wb_in[0], dh, "b0_in_bwd")
    (dh, grads["kv"], st_kv), got = _in_bwd(dk, dv, PAD, uk, hs[2], kv_g, kv_scale, w_kv, dh, "kv_in_bwd",
                                            comm=carry(["b_in0"]))
    land(["b_in0"], got)
    plan = {1: dict(o=[], p=[], i=["kv", "a_out1", "a_grp1"]), 0: dict(o=["a_in1"], p=["a_out0"], i=[])}
    early = ["b_out1", "b_in1", "b_out0", "b_in0", "kv", "a_out1", "a_grp1", "a_in1"]
    late = ["a_out0", "a_grp0", "a_in0"]
    both = {}

    def sum4(n):
        return _sum4(grads[n], landed[n], chip_arr, f"sum4_{n}")

    for l in (1, 0):
        u, z, pooled, mixed, y, w_in_l, wg_full, wo = saved[l]
        asl = a_scale[l:l + 1]
        (dms, dz, grads[f"a_out{l}"], st_out[l]), got = _out_bwd(dh, y, gate[l], mixed, asl, z, wo, f"a{l}_out_bwd",
                                                                scale_stat=True, comm=carry(plan[l]["o"]))
        land(plan[l]["o"], got)
        comm = carry(plan[l]["p"])
        if l == 0:
            mine = [sum4(n) for n in early]
            comm = _Comm(scatters=[grads[n] for n in plan[l]["p"]], swaps=mine)
        (dval, dwg), got = _pool_bwd(dms, pooled, wg_full, asl, f"a{l}_pool_bwd", comm=comm)
        land(plan[l]["p"], got)
        if l == 0:
            both.update({n: [a, b] for n, a, b in zip(early, mine, got[len(plan[l]["p"]):])})
        grads[f"a_grp{l}"] = (dwg.reshape(4, NCHIP, GW // NCHIP, GW).transpose(1, 0, 2, 3).reshape(NCHIP, GW, GW)
                              .astype(BF16))
        (dh, grads[f"a_in{l}"], st_in[l]), got = _in_bwd(dval, dz, 0, u, hs[l], gl[l], scale[l], w_in_l, dh, f"a{l}_in_bwd",
                                                         comm=carry(plan[l]["i"]))
        land(plan[l]["i"], got)
    pieces = st_in + [st_kv] + st_out + [st_fin]
    pieces += [_pad8(st_out[l][1:3]) for l in range(2)]
    pieces += [_pad8(d.reshape(NH * NRELP // D, D)) for d in (drb0, drb1)]
    gathered, got = _allgather8(jnp.concatenate(pieces, axis=0), "gather_stats", comm=carry(["a_grp0", "a_in0"]))
    land(["a_grp0", "a_in0"], got)
    mine = [sum4(n) for n in late]
    both.update({n: [a, b] for n, a, b in zip(late, mine, _comm_only(_Comm(swaps=mine), "swap_last"))})
    return dh, both, gathered.reshape(8, N_STAT, D)


ROW_IN = [8 * l for l in range(4)]
ROW_KV = 32
ROW_OUT = [40 + 8 * l for l in range(4)]
ROW_FIN = 72
ROW_ASC = [80, 88]
ROW_RB = [96, 104]
N_STAT = 112


def kernel(x, c, ada_w, ada_b, norm_g, a_w_in, a_w_group, a_scale, a_w_out, kv_norm_g, kv_ada_w, kv_ada_b, w_kv, b_w_in, b_rel_bias, b_w_out, final_g, loss_target, m_ada_w, m_ada_b, m_norm_g, m_a_w_in, m_a_w_group, m_a_scale, m_a_w_out, m_kv_norm_g, m_kv_ada_w, m_kv_ada_b, m_w_kv, m_b_w_in, m_b_rel_bias, m_b_w_out, m_final_g, v_ada_w, v_ada_b, v_norm_g, v_a_w_in, v_a_w_group, v_a_scale, v_a_w_out, v_kv_norm_g, v_kv_ada_w, v_kv_ada_b, v_w_kv, v_b_w_in, v_b_rel_bias, v_b_w_out, v_final_g):
    xi, yi, ci = _place()
    chip = 2 * xi + yi
    dev = 4 * xi + 2 * yi + ci
    n_ada = ada_w.shape[2]
    n_kva = kv_ada_w.shape[1]
    n_asc = a_scale.shape[1]

    ada_b_sh = lax.dynamic_slice_in_dim(ada_b, chip * n_ada, n_ada, axis=1)
    kvb_sh = lax.dynamic_slice_in_dim(kv_ada_b, chip * n_kva, n_kva, axis=0)
    sh = dict(a_in=[a_w_in[l].astype(BF16) for l in range(2)], a_grp=[a_w_group[l].astype(BF16) for l in range(2)],
              a_out=[a_w_out[l].astype(BF16) for l in range(2)], kv=[w_kv.astype(BF16)],
              b_in=[b_w_in[l].astype(BF16) for l in range(2)], b_out=[b_w_out[l].astype(BF16) for l in range(2)])
    c_act, gathered, w_first = _prologue(
        jnp.broadcast_to(c, (SUBLANES, D)), ada_w, ada_b_sh[:, None, :], kv_ada_w, kvb_sh[None, :],
        a_scale.reshape(1, 2 * n_asc), _Comm(gathers=[sh["a_in"][0], sh["a_grp"][0], sh["a_out"][0]]), "prologue")
    rows = jnp.concatenate([lax.dynamic_slice_in_dim(gathered, SUBLANES * (2 * p + ci) + dev, 1, axis=0)
                            for p in range(NCHIP)], axis=0)
    mods = jnp.stack([rows[:, l * n_ada:(l + 1) * n_ada].reshape(3 * D) for l in range(4)])
    kvmod = rows[:, 4 * n_ada:4 * n_ada + n_kva].reshape(2 * D)
    o_asc = 4 * n_ada + n_kva
    a_scale_full = jnp.stack([rows[:, o_asc + l * n_asc:o_asc + (l + 1) * n_asc].reshape(E) for l in range(2)])

    chip_arr = jnp.reshape(chip, (1,)).astype(jnp.int32)
    dh, both, g3 = _example_step(x[0], loss_target[0], mods, kvmod, a_scale_full, norm_g, kv_norm_g, final_g,
                                 b_rel_bias, sh, w_first, chip_arr)
    grad_x = dh[None]

    red, loss_tile = _stats_reduce(g3, ROW_FIN + 1, "stats_reduce")
    loss = loss_tile[0, 0]

    def cat(rows_):
        return jnp.concatenate(rows_, axis=-1)

    g_ada_b = jnp.stack([cat([red[ROW_IN[l]], red[ROW_IN[l] + 1], red[ROW_OUT[l]]]) for l in range(4)])
    g_norm_g = jnp.stack([red[ROW_IN[l] + 2] for l in range(4)])
    g_kv_norm_g = red[ROW_KV + 2]
    g_kv_ada_b = cat([red[ROW_KV], red[ROW_KV + 1]])
    g_final_g = red[ROW_FIN]
    g_asc_full = jnp.stack([red[ROW_ASC[l]:ROW_ASC[l] + 2].reshape(E) for l in range(2)])
    g_a_scale = lax.dynamic_slice_in_dim(g_asc_full, chip * n_asc, n_asc, axis=1)
    g_rel = jnp.stack([red[ROW_RB[bi]:ROW_RB[bi] + NH * NRELP // D].reshape(NH, NRELP)[:, :NREL] for bi in range(2)])

    dmod = jnp.stack([cat([g3[:, ROW_IN[l]], g3[:, ROW_IN[l] + 1], g3[:, ROW_OUT[l]]]) for l in range(4)])
    dmod_sh = lax.dynamic_slice_in_dim(dmod, chip * n_ada, n_ada, axis=2)
    dkv = cat([g3[:, ROW_KV], g3[:, ROW_KV + 1]])[None]
    dkv_sh = lax.dynamic_slice_in_dim(dkv, chip * n_kva, n_kva, axis=2)
    c_act_t = c_act.T
    g_ada_w = _grad_ada(c_act_t, dmod_sh, "grad_ada_w")
    g_kv_ada_w = _grad_ada(c_act_t, dkv_sh, "grad_kv_ada_w")

    def upd(w, m, v, g, name, shape3):
        g = g.reshape(shape3) if not isinstance(g, list) else g
        outs = _adamw(w.reshape(shape3), m.reshape(shape3), v.reshape(shape3), g, name)
        return [o.reshape(w.shape) for o in outs]

    def pair(name):
        return [both[name + "0"], both[name + "1"]]

    res = {}
    res["ada_w"] = upd(ada_w, m_ada_w, v_ada_w, g_ada_w, "adamw_ada_w", ada_w.shape)
    res["ada_b"] = upd(ada_b, m_ada_b, v_ada_b, g_ada_b, "adamw_ada_b", (1,) + ada_b.shape)
    res["norm_g"] = upd(norm_g, m_norm_g, v_norm_g, g_norm_g, "adamw_norm_g", (1,) + norm_g.shape)
    res["a_w_in"] = upd(a_w_in, m_a_w_in, v_a_w_in, pair("a_in"), "adamw_a_w_in", a_w_in.shape)
    res["a_w_group"] = upd(a_w_group, m_a_w_group, v_a_w_group, pair("a_grp"), "adamw_a_w_group", (2, GW, GW))
    res["a_scale"] = upd(a_scale, m_a_scale, v_a_scale, g_a_scale, "adamw_a_scale", (1,) + a_scale.shape)
    res["a_w_out"] = upd(a_w_out, m_a_w_out, v_a_w_out, pair("a_out"), "adamw_a_w_out", a_w_out.shape)
    res["kv_norm_g"] = upd(kv_norm_g, m_kv_norm_g, v_kv_norm_g, g_kv_norm_g, "adamw_kv_norm_g", (1, 1, D))
    res["kv_ada_w"] = upd(kv_ada_w, m_kv_ada_w, v_kv_ada_w, g_kv_ada_w, "adamw_kv_ada_w", (1,) + kv_ada_w.shape)
    res["kv_ada_b"] = upd(kv_ada_b, m_kv_ada_b, v_kv_ada_b, g_kv_ada_b, "adamw_kv_ada_b", (1, 1, 2 * D))
    res["w_kv"] = upd(w_kv, m_w_kv, v_w_kv, [both["kv"]], "adamw_w_kv", (1,) + w_kv.shape)
    res["b_w_in"] = upd(b_w_in, m_b_w_in, v_b_w_in, pair("b_in"), "adamw_b_w_in", b_w_in.shape)
    res["b_rel_bias"] = upd(b_rel_bias, m_b_rel_bias, v_b_rel_bias, g_rel, "adamw_b_rel_bias", (1, 2 * NH, NREL))
    res["b_w_out"] = upd(b_w_out, m_b_w_out, v_b_w_out, pair("b_out"), "adamw_b_w_out", b_w_out.shape)
    res["final_g"] = upd(final_g, m_final_g, v_final_g, g_final_g, "adamw_final_g", (1, 1, D))

    names = ["ada_w", "ada_b", "norm_g", "a_w_in", "a_w_group", "a_scale", "a_w_out", "kv_norm_g", "kv_ada_w", "kv_ada_b",
             "w_kv", "b_w_in", "b_rel_bias", "b_w_out", "final_g"]
    return (loss, grad_x, *[res[n][0] for n in names], *[res[n][1] for n in names], *[res[n][2] for n in names],
            *[res[n][3] for n in names])
```
